```python
import jax
import jax.numpy as jnp
from jax import lax
import numpy as np

D_MODEL = 1024
BATCH = 8
SEQ = 8192
DEPTH = 2

GRID_W = 64
CTX_LEN = 256
N_EVEN = (DEPTH + 1) // 2
N_ODD = DEPTH // 2

HEAD_DIM = 64
ATTN_HEADS = (D_MODEL // 2) // HEAD_DIM
ATTN_KV_HEADS = ATTN_HEADS // 4
ATTN_GROUP = ATTN_HEADS // ATTN_KV_HEADS
WINDOW = 128
ATTN_BLOCK = 128
ROPE_BASE = 10000.0

HGRN_EXPAND = 128
HGRN_WIDTH = D_MODEL // 2
HGRN_HEADS = HGRN_WIDTH // HGRN_EXPAND
HGRN_DK = HGRN_EXPAND
HGRN_DV = HGRN_EXPAND
CHUNK = 64

RET_HEADS = 4
RET_DK = D_MODEL // RET_HEADS
RET_DV = 2 * RET_DK
RET_BASE = 10000.0

FFN_HIDDEN = -(-8 * D_MODEL // (3 * 256)) * 256

A_Q = ATTN_HEADS * HEAD_DIM
A_KV = ATTN_KV_HEADS * HEAD_DIM
EVEN_PARTS = (('a_q', A_Q), ('a_k', A_KV), ('a_v', A_KV), ('b_q', HGRN_WIDTH), ('b_ff', HGRN_WIDTH),
              ('b_fb', HGRN_WIDTH), ('b_i', HGRN_WIDTH), ('b_g', HGRN_WIDTH))
EVEN_IN = A_Q + 2 * A_KV + 5 * HGRN_WIDTH
EVEN_OUT = A_Q + HGRN_HEADS * HGRN_DV
ODD_PARTS = (('q', RET_HEADS * RET_DK), ('k', RET_HEADS * RET_DK), ('v', RET_HEADS * RET_DV),
             ('g', RET_HEADS * RET_DV))
ODD_IN = 2 * RET_HEADS * RET_DK + 2 * RET_HEADS * RET_DV
ODD_OUT = RET_HEADS * RET_DV
EPS = 1e-6
F32 = jnp.float32

kernel_name = 'hybrid_swa_hgrn2_retention_dit_prefix'


def _offsets(parts):
    out, start = {}, 0
    for name, width in parts:
        out[name] = (start, start + width)
        start += width
    return out


def split_parts(u, parts):
    return {n: u[..., s:e] for n, (s, e) in _offsets(parts).items()}


def project_parts(h, w, parts, names):
    off = _offsets(parts)
    return {n: h @ w[:, off[n][0]:off[n][1]] for n in names}


def rms_norm(x, g=None):
    xf = x.astype(F32)
    y = xf * lax.rsqrt(jnp.mean(xf * xf, axis=-1, keepdims=True) + EPS)
    if g is not None:
        y = y * g.astype(F32)
    return y.astype(x.dtype)


def modulate(h, shift, scale):
    return h * (1.0 + scale) + shift


def to_heads(u, n_heads):
    b, l, _ = u.shape
    return u.reshape(b, l, n_heads, -1).transpose(0, 2, 1, 3)


def from_heads(o):
    return o.transpose(0, 2, 1, 3)


def flip(a):
    return jnp.flip(a, axis=2)


def axial_rope_tables(n_tok):
    t = jnp.arange(n_tok)
    row = (t // GRID_W).astype(F32)
    col = (t % GRID_W).astype(F32)
    n_freq = HEAD_DIM // 4
    inv = ROPE_BASE ** (-jnp.arange(n_freq, dtype=F32) / n_freq)
    ang = jnp.concatenate([row[:, None] * inv, col[:, None] * inv], axis=-1)
    return jnp.cos(ang), jnp.sin(ang)


def retention_rope_tables(n_tok):
    theta = 1.0 / (RET_BASE ** jnp.linspace(0.0, 1.0, RET_DK // 2, dtype=F32))
    ang = jnp.arange(n_tok, dtype=F32)[:, None] * theta
    return jnp.cos(ang), jnp.sin(ang)


def apply_rope(x, cos, sin):
    half = x.shape[-1] // 2
    xf = x.astype(F32)
    x1, x2 = xf[..., :half], xf[..., half:]
    return jnp.concatenate([x1 * cos - x2 * sin, x2 * cos + x1 * sin], axis=-1).astype(x.dtype)


def attn_q_heads(u, g):
    b, l, _ = u.shape
    q = rms_norm(u.reshape(b, l, ATTN_KV_HEADS, ATTN_GROUP, HEAD_DIM), g)
    return q.transpose(0, 2, 3, 1, 4)


def kv_heads(u, g=None):
    b, l, _ = u.shape
    h = u.reshape(b, l, ATTN_KV_HEADS, HEAD_DIM)
    if g is not None:
        h = rms_norm(h, g)
    return h.transpose(0, 2, 1, 3)


def window_attention(q, k, v, k_ctx, v_ctx, sink):
    b, kv, g, l, d = q.shape
    n_blocks = l // ATTN_BLOCK
    span = ATTN_BLOCK + 2 * WINDOW
    pad = ((0, 0), (0, 0), (WINDOW, WINDOW), (0, 0))
    kp, vp = jnp.pad(k, pad), jnp.pad(v, pad)
    scale = d ** -0.5
    sink_col = jnp.broadcast_to(sink[None, :, :, None, None], (b, kv, g, ATTN_BLOCK, 1))

    def block(i):
        start = i * ATTN_BLOCK
        qb = lax.dynamic_slice_in_dim(q, start, ATTN_BLOCK, axis=3)
        kb = lax.dynamic_slice_in_dim(kp, start, span, axis=2)
        vb = lax.dynamic_slice_in_dim(vp, start, span, axis=2)
        q_pos = start + jnp.arange(ATTN_BLOCK)
        k_pos = start - WINDOW + jnp.arange(span)
        valid = ((jnp.abs(k_pos[None, :] - q_pos[:, None]) <= WINDOW)
                 & (k_pos >= 0)[None, :] & (k_pos < l)[None, :])
        s_lat = jnp.einsum('bkgqd,bksd->bkgqs', qb, kb, preferred_element_type=F32) * scale
        s_lat = jnp.where(valid, s_lat, -jnp.inf)
        s_ctx = jnp.einsum('bkgqd,bkcd->bkgqc', qb, k_ctx, preferred_element_type=F32) * scale
        p = jax.nn.softmax(jnp.concatenate([sink_col, s_lat, s_ctx], axis=-1), axis=-1)
        p_lat = p[..., 1:1 + span].astype(v.dtype)
        p_ctx = p[..., 1 + span:].astype(v.dtype)
        return (jnp.einsum('bkgqs,bksd->bkgqd', p_lat, vb)
                + jnp.einsum('bkgqc,bkcd->bkgqd', p_ctx, v_ctx))

    o = lax.map(block, jnp.arange(n_blocks))
    return o.transpose(1, 0, 4, 2, 3, 5).reshape(b, l, kv * g * d)


def context_attention(q, k, v, sink):
    b, kv, g, lc, d = q.shape
    s = jnp.einsum('bkgqd,bkcd->bkgqc', q, k, preferred_element_type=F32) * d ** -0.5
    sink_col = jnp.broadcast_to(sink[None, :, :, None, None], (b, kv, g, lc, 1))
    p = jax.nn.softmax(jnp.concatenate([sink_col, s], axis=-1), axis=-1)[..., 1:].astype(v.dtype)
    o = jnp.einsum('bkgqc,bkcd->bkgqd', p, v)
    return o.transpose(0, 3, 1, 2, 4).reshape(b, lc, kv * g * d)


def chunk_scan(q_in, k_in, v, decay, s0):
    xs = (jnp.moveaxis(q_in, 2, 0), jnp.moveaxis(k_in, 2, 0), jnp.moveaxis(v, 2, 0), decay)

    def step(s, inp):
        q_n, k_n, v_n, a_n = inp
        o_n = jnp.einsum('bhcd,bhde->bhce', q_n, s)
        s = a_n * s + jnp.einsum('bhcd,bhce->bhde', k_n, v_n)
        return s, o_n

    s, o = lax.scan(step, s0, xs)
    return jnp.moveaxis(o, 0, 2), s


def gla_chunked(q, k, v, log_f, s0):
    b, h, l, dk = q.shape
    dv = v.shape[-1]
    n = l // CHUNK
    qc, kc, lc = (a.reshape(b, h, n, CHUNK, dk) for a in (q, k, log_f))
    vc = v.reshape(b, h, n, CHUNK, dv)
    cum = jnp.cumsum(lc, axis=3)
    ref = cum[:, :, :, CHUNK // 2:CHUNK // 2 + 1]
    scores = jnp.einsum('bhntd,bhnsd->bhnts', qc * jnp.exp(cum - ref), kc * jnp.exp(ref - cum))
    lower = jnp.tril(jnp.ones((CHUNK, CHUNK), dtype=bool))
    o_intra = jnp.einsum('bhnts,bhnse->bhnte', jnp.where(lower, scores, 0.0), vc)
    cum_last = cum[:, :, :, -1:]
    decay = jnp.moveaxis(jnp.exp(cum_last[:, :, :, 0]), 2, 0)[..., None]
    o_inter, s = chunk_scan(qc * jnp.exp(cum), kc * jnp.exp(cum_last - cum), vc, decay, s0)
    return (o_intra + o_inter).reshape(b, h, l, dv), s


def gla_final_state(k, v, log_f):
    cum = jnp.cumsum(log_f, axis=2)
    return jnp.einsum('bhld,bhle->bhde', k * jnp.exp(cum[:, :, -1:] - cum), v)


def retention_chunked(q, k, v, log_gamma, s0):
    b, h, l, dk = q.shape
    dv = v.shape[-1]
    n = l // CHUNK
    qc = q.reshape(b, h, n, CHUNK, dk)
    kc = k.reshape(b, h, n, CHUNK, dk)
    vc = v.reshape(b, h, n, CHUNK, dv)
    pos = jnp.arange(CHUNK, dtype=F32)
    rel = pos[:, None] - pos[None, :]
    dmat = jnp.where(rel >= 0, jnp.exp(log_gamma[:, None, None] * jnp.maximum(rel, 0.0)), 0.0)
    scores = jnp.einsum('bhntd,bhnsd->bhnts', qc, kc) * dmat[None, :, None]
    o_intra = jnp.einsum('bhnts,bhnse->bhnte', scores, vc)
    lg = log_gamma[:, None]
    q_in = qc * jnp.exp(lg * (pos + 1.0))[None, :, None, :, None]
    k_in = kc * jnp.exp(lg * (CHUNK - 1.0 - pos))[None, :, None, :, None]
    decay = jnp.broadcast_to(jnp.exp(log_gamma * CHUNK)[None, None, :, None, None], (n, 1, h, 1, 1))
    o_inter, s = chunk_scan(q_in, k_in, vc, decay, s0)
    return (o_intra + o_inter).reshape(b, h, l, dv), s


def retention_final_state(k, v, log_gamma):
    lc = k.shape[2]
    w = jnp.exp(log_gamma[:, None] * (lc - 1.0 - jnp.arange(lc, dtype=F32)))
    return jnp.einsum('bhld,bhle->bhde', k * w[None, :, :, None], v)


def gated_head_norm(o, g_raw, gain=None):
    b, h, l, dv = o.shape
    y = rms_norm(from_heads(o), gain) * jax.nn.silu(g_raw.reshape(b, l, h, dv).astype(F32))
    return y.reshape(b, l, h * dv)


def even_mixer(h_ctx, h_lat, w_in, w_out, qk_g, sink, out_g, lb, cos, sin, need_ctx):
    dt = h_lat.dtype
    n_b = h_ctx.shape[0]
    sink = sink.astype(F32).reshape(ATTN_KV_HEADS, ATTN_GROUP)
    lb = lb.reshape(HGRN_HEADS, 1, HGRN_DK)
    p = split_parts(h_lat @ w_in, EVEN_PARTS)
    names = [n for n, _ in EVEN_PARTS] if need_ctx else ['a_k', 'a_v', 'b_ff', 'b_fb', 'b_i']
    pc = project_parts(h_ctx, w_in, EVEN_PARTS, names)

    def gates(f_raw):
        f = lb + (1.0 - lb) * jax.nn.sigmoid(to_heads(f_raw, HGRN_HEADS).astype(F32))
        return 1.0 - f, jnp.log(f)

    k_ctx = kv_heads(pc['a_k'], qk_g[1])
    v_ctx = kv_heads(pc['a_v'])
    q_lat = apply_rope(attn_q_heads(p['a_q'], qk_g[0]), cos, sin)
    k_lat = apply_rope(kv_heads(p['a_k'], qk_g[1]), cos, sin)
    a_lat = window_attention(q_lat, k_lat, kv_heads(p['a_v']), k_ctx, v_ctx, sink)

    k_fw_c, lf_fw_c = gates(pc['b_ff'])
    k_bw_c, lf_bw_c = gates(pc['b_fb'])
    i_c = to_heads(pc['b_i'], HGRN_HEADS).astype(F32)
    if need_ctx:
        zeros = jnp.zeros((n_b, HGRN_HEADS, HGRN_DK, HGRN_DV), F32)
        q_c = jax.nn.silu(to_heads(pc['b_q'], HGRN_HEADS).astype(F32))
        o_fw_c, s_fw = gla_chunked(q_c, k_fw_c, i_c, lf_fw_c, zeros)
        o_bw_c, s_bw = gla_chunked(flip(q_c), flip(k_bw_c), flip(i_c), flip(lf_bw_c), zeros)
    else:
        s_fw = gla_final_state(k_fw_c, i_c, lf_fw_c)
        s_bw = gla_final_state(flip(k_bw_c), flip(i_c), flip(lf_bw_c))
    k_fw, lf_fw = gates(p['b_ff'])
    k_bw, lf_bw = gates(p['b_fb'])
    q_l = jax.nn.silu(to_heads(p['b_q'], HGRN_HEADS).astype(F32))
    i_l = to_heads(p['b_i'], HGRN_HEADS).astype(F32)
    o_fw, _ = gla_chunked(q_l, k_fw, i_l, lf_fw, s_fw)
    o_bw, _ = gla_chunked(flip(q_l), flip(k_bw), flip(i_l), flip(lf_bw), s_bw)
    b_lat = gated_head_norm(o_fw + flip(o_bw), p['b_g'], out_g)

    y_lat = jnp.concatenate([a_lat.astype(dt), b_lat.astype(dt)], axis=-1) @ w_out
    if not need_ctx:
        return None, y_lat
    a_ctx = context_attention(attn_q_heads(pc['a_q'], qk_g[0]), k_ctx, v_ctx, sink)
    b_ctx = gated_head_norm(o_fw_c + flip(o_bw_c), pc['b_g'], out_g)
    y_ctx = jnp.concatenate([a_ctx.astype(dt), b_ctx.astype(dt)], axis=-1) @ w_out
    return y_ctx, y_lat


def odd_mixer(h_ctx, h_lat, w_in, w_out, cos, sin, need_ctx):
    dt = h_lat.dtype
    n_b = h_ctx.shape[0]
    log_g_fw = jnp.log(1.0 - 2.0 ** (-5.0 - jnp.arange(RET_HEADS, dtype=F32)))
    log_g_bw = log_g_fw[::-1]
    k_scale = RET_DK ** -0.5
    p = split_parts(h_lat @ w_in, ODD_PARTS)
    names = [n for n, _ in ODD_PARTS] if need_ctx else ['k', 'v']
    pc = project_parts(h_ctx, w_in, ODD_PARTS, names)

    k_c = to_heads(pc['k'], RET_HEADS).astype(F32) * k_scale
    v_c = to_heads(pc['v'], RET_HEADS).astype(F32)
    if need_ctx:
        zeros = jnp.zeros((n_b, RET_HEADS, RET_DK, RET_DV), F32)
        q_c = to_heads(pc['q'], RET_HEADS).astype(F32)
        o_fw_c, s_fw = retention_chunked(q_c, k_c, v_c, log_g_fw, zeros)
        o_bw_c, s_bw = retention_chunked(flip(q_c), flip(k_c), flip(v_c), log_g_bw, zeros)
    else:
        s_fw = retention_final_state(k_c, v_c, log_g_fw)
        s_bw = retention_final_state(flip(k_c), flip(v_c), log_g_bw)

    q_l = apply_rope(to_heads(p['q'], RET_HEADS).astype(F32), cos, sin)
    k_l = apply_rope(to_heads(p['k'], RET_HEADS).astype(F32), cos, sin) * k_scale
    v_l = to_heads(p['v'], RET_HEADS).astype(F32)
    o_fw, _ = retention_chunked(q_l, k_l, v_l, log_g_fw, s_fw)
    o_bw, _ = retention_chunked(flip(q_l), flip(k_l), flip(v_l), log_g_bw, s_bw)
    y_lat = gated_head_norm(o_fw + flip(o_bw), p['g']).astype(dt) @ w_out
    if not need_ctx:
        return None, y_lat
    y_ctx = gated_head_norm(o_fw_c + flip(o_bw_c), pc['g']).astype(dt) @ w_out
    return y_ctx, y_lat


def swiglu(h, w_in, w_out):
    gate, up = jnp.split(h @ w_in, 2, axis=-1)
    return (jax.nn.silu(gate) * up) @ w_out


def _fwd_setup_inputs(seed: int = 0) -> dict:
    key = jax.random.key(seed)
    ks = jax.random.split(key, 17)
    d = D_MODEL

    def nrm(k, shape, scale):
        return jax.random.normal(k, shape, F32) * scale

    return {
        'x': nrm(ks[0], (BATCH, SEQ, d), 1.0),
        'c': nrm(ks[1], (BATCH, d), 1.0),
        'ctx': nrm(ks[2], (BATCH, CTX_LEN, d), 1.0),
        'c_ctx': nrm(ks[3], (d,), 1.0),
        'mod_w': nrm(ks[4], (DEPTH, d, 6 * d), 0.5 * d ** -0.5),
        'mod_b': nrm(ks[5], (DEPTH, 6 * d), 0.02),
        'norm_g': 1.0 + nrm(ks[6], (DEPTH, 2, d), 0.02),
        'ffn_w_in': nrm(ks[7], (DEPTH, d, 2 * FFN_HIDDEN), d ** -0.5),
        'ffn_w_out': nrm(ks[8], (DEPTH, FFN_HIDDEN, d), FFN_HIDDEN ** -0.5),
        'even_w_in': nrm(ks[9], (N_EVEN, d, EVEN_IN), d ** -0.5),
        'even_w_out': nrm(ks[10], (N_EVEN, EVEN_OUT, d), EVEN_OUT ** -0.5),
        'attn_qk_norm_g': 1.0 + nrm(ks[11], (N_EVEN, 2, HEAD_DIM), 0.02),
        'attn_sink': nrm(ks[12], (N_EVEN, ATTN_HEADS), 0.5),
        'hgrn_out_norm_g': 1.0 + nrm(ks[13], (N_EVEN, HGRN_DV), 0.02),
        'hgrn_lb': nrm(ks[14], (N_EVEN + 1, HGRN_WIDTH), 0.1),
        'odd_w_in': nrm(ks[15], (N_ODD, d, ODD_IN), d ** -0.5),
        'odd_w_out': nrm(ks[16], (N_ODD, ODD_OUT, d), ODD_OUT ** -0.5),
    }


def _fwd_reference(x, c, ctx, c_ctx, mod_w, mod_b, norm_g, ffn_w_in, ffn_w_out, even_w_in, even_w_out,
              attn_qk_norm_g, attn_sink, hgrn_out_norm_g, hgrn_lb, odd_w_in, odd_w_out):
    n_tok = x.shape[1]
    rope_cos, rope_sin = axial_rope_tables(n_tok)
    ret_cos, ret_sin = retention_rope_tables(n_tok)
    lower_bounds = jnp.cumsum(jax.nn.softmax(hgrn_lb.astype(F32), axis=0), axis=0)
    cond_lat = jax.nn.silu(c)
    cond_ctx = jax.nn.silu(c_ctx)
    for layer in range(DEPTH):
        last = layer == DEPTH - 1
        j = layer // 2
        m_lat = jnp.split((cond_lat @ mod_w[layer] + mod_b[layer])[:, None, :], 6, axis=-1)
        m_ctx = jnp.split((cond_ctx @ mod_w[layer] + mod_b[layer])[None, None, :], 6, axis=-1)
        h_lat = modulate(rms_norm(x, norm_g[layer, 0]), m_lat[0], m_lat[1])
        h_ctx = modulate(rms_norm(ctx, norm_g[layer, 0]), m_ctx[0], m_ctx[1])
        if layer % 2 == 0:
            y_ctx, y_lat = even_mixer(h_ctx, h_lat, even_w_in[j], even_w_out[j], attn_qk_norm_g[j],
                                      attn_sink[j], hgrn_out_norm_g[j], lower_bounds[j],
                                      rope_cos, rope_sin, not last)
        else:
            y_ctx, y_lat = odd_mixer(h_ctx, h_lat, odd_w_in[j], odd_w_out[j], ret_cos, ret_sin, not last)
        x = x + m_lat[2] * y_lat
        x = x + m_lat[5] * swiglu(modulate(rms_norm(x, norm_g[layer, 1]), m_lat[3], m_lat[4]),
                                  ffn_w_in[layer], ffn_w_out[layer])
        if not last:
            ctx = ctx + m_ctx[2] * y_ctx
            ctx = ctx + m_ctx[5] * swiglu(modulate(rms_norm(ctx, norm_g[layer, 1]), m_ctx[3], m_ctx[4]),
                                          ffn_w_in[layer], ffn_w_out[layer])
    return x


import jax as _jax
import jax.numpy as _jnp

TWIN_FORMAT = 'train_step'
FWD_PARAMS = ['x', 'c', 'ctx', 'c_ctx', 'mod_w', 'mod_b', 'norm_g', 'ffn_w_in', 'ffn_w_out', 'even_w_in', 'even_w_out', 'attn_qk_norm_g', 'attn_sink', 'hgrn_out_norm_g', 'hgrn_lb', 'odd_w_in', 'odd_w_out']
TWIN_WEIGHTS = ['c_ctx', 'mod_w', 'mod_b', 'norm_g', 'ffn_w_in', 'ffn_w_out', 'even_w_in', 'even_w_out', 'attn_qk_norm_g', 'attn_sink', 'hgrn_out_norm_g', 'hgrn_lb', 'odd_w_in', 'odd_w_out']
TWIN_DIFF_INPUT = 'x'
TWIN_INPUTS = ['x', 'c', 'ctx', 'c_ctx', 'mod_w', 'mod_b', 'norm_g', 'ffn_w_in', 'ffn_w_out', 'even_w_in', 'even_w_out', 'attn_qk_norm_g', 'attn_sink', 'hgrn_out_norm_g', 'hgrn_lb', 'odd_w_in', 'odd_w_out', 'loss_target', 'm_c_ctx', 'm_mod_w', 'm_mod_b', 'm_norm_g', 'm_ffn_w_in', 'm_ffn_w_out', 'm_even_w_in', 'm_even_w_out', 'm_attn_qk_norm_g', 'm_attn_sink', 'm_hgrn_out_norm_g', 'm_hgrn_lb', 'm_odd_w_in', 'm_odd_w_out', 'v_c_ctx', 'v_mod_w', 'v_mod_b', 'v_norm_g', 'v_ffn_w_in', 'v_ffn_w_out', 'v_even_w_in', 'v_even_w_out', 'v_attn_qk_norm_g', 'v_attn_sink', 'v_hgrn_out_norm_g', 'v_hgrn_lb', 'v_odd_w_in', 'v_odd_w_out']
TWIN_OUTPUTS = ['loss', 'grad_x', 'grad_c_ctx', 'grad_mod_w', 'grad_mod_b', 'grad_norm_g', 'grad_ffn_w_in', 'grad_ffn_w_out', 'grad_even_w_in', 'grad_even_w_out', 'grad_attn_qk_norm_g', 'grad_attn_sink', 'grad_hgrn_out_norm_g', 'grad_hgrn_lb', 'grad_odd_w_in', 'grad_odd_w_out', 'delta_c_ctx', 'delta_mod_w', 'delta_mod_b', 'delta_norm_g', 'delta_ffn_w_in', 'delta_ffn_w_out', 'delta_even_w_in', 'delta_even_w_out', 'delta_attn_qk_norm_g', 'delta_attn_sink', 'delta_hgrn_out_norm_g', 'delta_hgrn_lb', 'delta_odd_w_in', 'delta_odd_w_out', 'new_m_c_ctx', 'new_m_mod_w', 'new_m_mod_b', 'new_m_norm_g', 'new_m_ffn_w_in', 'new_m_ffn_w_out', 'new_m_even_w_in', 'new_m_even_w_out', 'new_m_attn_qk_norm_g', 'new_m_attn_sink', 'new_m_hgrn_out_norm_g', 'new_m_hgrn_lb', 'new_m_odd_w_in', 'new_m_odd_w_out', 'new_v_c_ctx', 'new_v_mod_w', 'new_v_mod_b', 'new_v_norm_g', 'new_v_ffn_w_in', 'new_v_ffn_w_out', 'new_v_even_w_in', 'new_v_even_w_out', 'new_v_attn_qk_norm_g', 'new_v_attn_sink', 'new_v_hgrn_out_norm_g', 'new_v_hgrn_lb', 'new_v_odd_w_in', 'new_v_odd_w_out']
TWIN_LEAF_KINDS = {'loss': 'loss', 'grad_x': 'grad_x', 'grad_c_ctx': 'grad_w', 'grad_mod_w': 'grad_w', 'grad_mod_b': 'grad_w', 'grad_norm_g': 'grad_w', 'grad_ffn_w_in': 'grad_w', 'grad_ffn_w_out': 'grad_w', 'grad_even_w_in': 'grad_w', 'grad_even_w_out': 'grad_w', 'grad_attn_qk_norm_g': 'grad_w', 'grad_attn_sink': 'grad_w', 'grad_hgrn_out_norm_g': 'grad_w', 'grad_hgrn_lb': 'grad_w', 'grad_odd_w_in': 'grad_w', 'grad_odd_w_out': 'grad_w', 'delta_c_ctx': 'delta_w', 'delta_mod_w': 'delta_w', 'delta_mod_b': 'delta_w', 'delta_norm_g': 'delta_w', 'delta_ffn_w_in': 'delta_w', 'delta_ffn_w_out': 'delta_w', 'delta_even_w_in': 'delta_w', 'delta_even_w_out': 'delta_w', 'delta_attn_qk_norm_g': 'delta_w', 'delta_attn_sink': 'delta_w', 'delta_hgrn_out_norm_g': 'delta_w', 'delta_hgrn_lb': 'delta_w', 'delta_odd_w_in': 'delta_w', 'delta_odd_w_out': 'delta_w', 'new_m_c_ctx': 'new_m', 'new_m_mod_w': 'new_m', 'new_m_mod_b': 'new_m', 'new_m_norm_g': 'new_m', 'new_m_ffn_w_in': 'new_m', 'new_m_ffn_w_out': 'new_m', 'new_m_even_w_in': 'new_m', 'new_m_even_w_out': 'new_m', 'new_m_attn_qk_norm_g': 'new_m', 'new_m_attn_sink': 'new_m', 'new_m_hgrn_out_norm_g': 'new_m', 'new_m_hgrn_lb': 'new_m', 'new_m_odd_w_in': 'new_m', 'new_m_odd_w_out': 'new_m', 'new_v_c_ctx': 'new_v', 'new_v_mod_w': 'new_v', 'new_v_mod_b': 'new_v', 'new_v_norm_g': 'new_v', 'new_v_ffn_w_in': 'new_v', 'new_v_ffn_w_out': 'new_v', 'new_v_even_w_in': 'new_v', 'new_v_even_w_out': 'new_v', 'new_v_attn_qk_norm_g': 'new_v', 'new_v_attn_sink': 'new_v', 'new_v_hgrn_out_norm_g': 'new_v', 'new_v_hgrn_lb': 'new_v', 'new_v_odd_w_in': 'new_v', 'new_v_odd_w_out': 'new_v'}


def _forward(args):
    return _fwd_reference(*[args[k] for k in FWD_PARAMS])


def _output_shape():
    out = _jax.eval_shape(lambda: _forward(_fwd_setup_inputs(0)))
    return out.shape, out.dtype

N_MICROBATCH = 1
ADAM_LR = 0.001
ADAM_B1 = 0.9
ADAM_B2 = 0.999
ADAM_EPS = 1e-08
ADAM_WD = 0.01
ADAM_STEP = 10
PER_EXAMPLE_BATCH_AXIS = {'x': 0, 'c': 0, 'ctx': 0, 'loss_target': 0}
SHARED_INPUTS = []
_WEIGHT_DTYPES = {'c_ctx': _jnp.float32, 'mod_w': _jnp.float32, 'mod_b': _jnp.float32, 'norm_g': _jnp.float32, 'ffn_w_in': _jnp.float32, 'ffn_w_out': _jnp.float32, 'even_w_in': _jnp.float32, 'even_w_out': _jnp.float32, 'attn_qk_norm_g': _jnp.float32, 'attn_sink': _jnp.float32, 'hgrn_out_norm_g': _jnp.float32, 'hgrn_lb': _jnp.float32, 'odd_w_in': _jnp.float32, 'odd_w_out': _jnp.float32}
MOMENT_SCALE = {'c_ctx': 9.942082e-02, 'mod_w': 1.745580e+00, 'mod_b': 3.877405e+00, 'norm_g': 5.034670e+00, 'ffn_w_in': 9.325223e-02, 'ffn_w_out': 1.184980e-01, 'even_w_in': 1.890918e-01, 'even_w_out': 1.816861e-01, 'attn_qk_norm_g': 3.117021e-01, 'attn_sink': 2.469725e-02, 'hgrn_out_norm_g': 1.077741e+01, 'hgrn_lb': 8.900817e-03, 'odd_w_in': 1.048983e-01, 'odd_w_out': 1.011178e-01}


def _to_microbatches(a, axis):
    t = _jnp.moveaxis(a, axis, 0)
    t = t.reshape((N_MICROBATCH, t.shape[0] // N_MICROBATCH) + t.shape[1:])
    return _jnp.moveaxis(t, 1, axis + 1)


def setup_inputs(seed: int = 0) -> dict:
    inp = _fwd_setup_inputs(seed)
    key = _jax.random.fold_in(_jax.random.key(seed), 7919)
    shape, _ = _output_shape()
    out = dict(inp)
    out["loss_target"] = _jax.random.normal(_jax.random.fold_in(key, 0), shape, _jnp.float32)
    for i, name in enumerate(TWIN_WEIGHTS):
        w = inp[name].astype(_jnp.float32)
        if MOMENT_SCALE is None:
            s = _jnp.sqrt(_jnp.mean(_jnp.square(w)) + 1e-30)
        else:
            s = MOMENT_SCALE[name]
        km, kv = _jax.random.split(_jax.random.fold_in(key, i + 1))
        out[name] = w
        out["m_" + name] = s * _jax.random.normal(km, w.shape, _jnp.float32)
        out["v_" + name] = (s * s) * _jax.random.uniform(kv, w.shape, _jnp.float32, 0.5, 1.5)
    if N_MICROBATCH > 1:
        for name, axis in PER_EXAMPLE_BATCH_AXIS.items():
            out[name] = _to_microbatches(out[name], axis)
    return {'x': out['x'], 'c': out['c'], 'ctx': out['ctx'], 'c_ctx': out['c_ctx'], 'mod_w': out['mod_w'], 'mod_b': out['mod_b'], 'norm_g': out['norm_g'], 'ffn_w_in': out['ffn_w_in'], 'ffn_w_out': out['ffn_w_out'], 'even_w_in': out['even_w_in'], 'even_w_out': out['even_w_out'], 'attn_qk_norm_g': out['attn_qk_norm_g'], 'attn_sink': out['attn_sink'], 'hgrn_out_norm_g': out['hgrn_out_norm_g'], 'hgrn_lb': out['hgrn_lb'], 'odd_w_in': out['odd_w_in'], 'odd_w_out': out['odd_w_out'], 'loss_target': out['loss_target'], 'm_c_ctx': out['m_c_ctx'], 'm_mod_w': out['m_mod_w'], 'm_mod_b': out['m_mod_b'], 'm_norm_g': out['m_norm_g'], 'm_ffn_w_in': out['m_ffn_w_in'], 'm_ffn_w_out': out['m_ffn_w_out'], 'm_even_w_in': out['m_even_w_in'], 'm_even_w_out': out['m_even_w_out'], 'm_attn_qk_norm_g': out['m_attn_qk_norm_g'], 'm_attn_sink': out['m_attn_sink'], 'm_hgrn_out_norm_g': out['m_hgrn_out_norm_g'], 'm_hgrn_lb': out['m_hgrn_lb'], 'm_odd_w_in': out['m_odd_w_in'], 'm_odd_w_out': out['m_odd_w_out'], 'v_c_ctx': out['v_c_ctx'], 'v_mod_w': out['v_mod_w'], 'v_mod_b': out['v_mod_b'], 'v_norm_g': out['v_norm_g'], 'v_ffn_w_in': out['v_ffn_w_in'], 'v_ffn_w_out': out['v_ffn_w_out'], 'v_even_w_in': out['v_even_w_in'], 'v_even_w_out': out['v_even_w_out'], 'v_attn_qk_norm_g': out['v_attn_qk_norm_g'], 'v_attn_sink': out['v_attn_sink'], 'v_hgrn_out_norm_g': out['v_hgrn_out_norm_g'], 'v_hgrn_lb': out['v_hgrn_lb'], 'v_odd_w_in': out['v_odd_w_in'], 'v_odd_w_out': out['v_odd_w_out']}


def _loss(weights, diff, rest, loss_target):
    with _jax.named_scope("forward"):
        args = {**rest, TWIN_DIFF_INPUT: diff, **{k: w.astype(_WEIGHT_DTYPES[k]) for k, w in weights.items()}}
        y = _forward(args)
    with _jax.named_scope("loss_head"):
        err = _jnp.square(y.astype(_jnp.float32) - loss_target)
        return 0.5 * _jnp.sum(_jnp.mean(err, axis=-1)) if err.ndim else 0.5 * err


def _adamw(w, g, m, v):
    m = ADAM_B1 * m + (1.0 - ADAM_B1) * g
    v = ADAM_B2 * v + (1.0 - ADAM_B2) * _jnp.square(g)
    m_hat = m / (1.0 - ADAM_B1 ** ADAM_STEP)
    v_hat = v / (1.0 - ADAM_B2 ** ADAM_STEP)
    delta = -ADAM_LR * (m_hat / (_jnp.sqrt(v_hat) + ADAM_EPS) + ADAM_WD * w)
    return delta, m, v


def reference(x, c, ctx, c_ctx, mod_w, mod_b, norm_g, ffn_w_in, ffn_w_out, even_w_in, even_w_out, attn_qk_norm_g, attn_sink, hgrn_out_norm_g, hgrn_lb, odd_w_in, odd_w_out, loss_target, m_c_ctx, m_mod_w, m_mod_b, m_norm_g, m_ffn_w_in, m_ffn_w_out, m_even_w_in, m_even_w_out, m_attn_qk_norm_g, m_attn_sink, m_hgrn_out_norm_g, m_hgrn_lb, m_odd_w_in, m_odd_w_out, v_c_ctx, v_mod_w, v_mod_b, v_norm_g, v_ffn_w_in, v_ffn_w_out, v_even_w_in, v_even_w_out, v_attn_qk_norm_g, v_attn_sink, v_hgrn_out_norm_g, v_hgrn_lb, v_odd_w_in, v_odd_w_out):
    given = dict(x=x, c=c, ctx=ctx, c_ctx=c_ctx, mod_w=mod_w, mod_b=mod_b, norm_g=norm_g, ffn_w_in=ffn_w_in, ffn_w_out=ffn_w_out, even_w_in=even_w_in, even_w_out=even_w_out, attn_qk_norm_g=attn_qk_norm_g, attn_sink=attn_sink, hgrn_out_norm_g=hgrn_out_norm_g, hgrn_lb=hgrn_lb, odd_w_in=odd_w_in, odd_w_out=odd_w_out, loss_target=loss_target, m_c_ctx=m_c_ctx, m_mod_w=m_mod_w, m_mod_b=m_mod_b, m_norm_g=m_norm_g, m_ffn_w_in=m_ffn_w_in, m_ffn_w_out=m_ffn_w_out, m_even_w_in=m_even_w_in, m_even_w_out=m_even_w_out, m_attn_qk_norm_g=m_attn_qk_norm_g, m_attn_sink=m_attn_sink, m_hgrn_out_norm_g=m_hgrn_out_norm_g, m_hgrn_lb=m_hgrn_lb, m_odd_w_in=m_odd_w_in, m_odd_w_out=m_odd_w_out, v_c_ctx=v_c_ctx, v_mod_w=v_mod_w, v_mod_b=v_mod_b, v_norm_g=v_norm_g, v_ffn_w_in=v_ffn_w_in, v_ffn_w_out=v_ffn_w_out, v_even_w_in=v_even_w_in, v_even_w_out=v_even_w_out, v_attn_qk_norm_g=v_attn_qk_norm_g, v_attn_sink=v_attn_sink, v_hgrn_out_norm_g=v_hgrn_out_norm_g, v_hgrn_lb=v_hgrn_lb, v_odd_w_in=v_odd_w_in, v_odd_w_out=v_odd_w_out)
    weights = {n: given[n] for n in TWIN_WEIGHTS}
    shared = {n: given[n] for n in SHARED_INPUTS}
    per_example = {n: given[n] for n in ['x', 'c', 'ctx']}
    grad_fn = _jax.value_and_grad(_loss, argnums=(0, 1))

    def one_microbatch(ex, loss_target):
        ex = dict(ex)
        diff = ex.pop(TWIN_DIFF_INPUT)
        return grad_fn(weights, diff, {**shared, **ex}, loss_target)

    if N_MICROBATCH == 1:
        loss, (grad_w, grad_x) = one_microbatch(per_example, given["loss_target"])
    else:
        def body(carry, xs):
            loss_sum, grad_sum = carry
            l_k, (gw_k, gx_k) = one_microbatch(xs[0], xs[1])
            with _jax.named_scope("update"):
                return (loss_sum + l_k, _jax.tree.map(_jnp.add, grad_sum, gw_k)), gx_k

        init = (_jnp.zeros((), _jnp.float32), _jax.tree.map(_jnp.zeros_like, weights))
        (loss, grad_w), grad_x = _jax.lax.scan(body, init, (per_example, given["loss_target"]))
    with _jax.named_scope("update"):
        delta_w, new_m, new_v = {}, {}, {}
        for n in TWIN_WEIGHTS:
            delta_w[n], new_m[n], new_v[n] = _adamw(weights[n], grad_w[n], given["m_" + n], given["v_" + n])
    return (loss, grad_x, *[grad_w[n] for n in TWIN_WEIGHTS], *[delta_w[n] for n in TWIN_WEIGHTS],
            *[new_m[n] for n in TWIN_WEIGHTS], *[new_v[n] for n in TWIN_WEIGHTS])
```

```python
import functools
import math

import jax
import jax.numpy as jnp
from jax import lax
from jax.experimental import pallas as pl
from jax.experimental.pallas import tpu as pltpu

F32 = jnp.float32
BF16 = jnp.bfloat16
EPS = 1e-6
N_DEV = 8
MESH = pl.DeviceIdType.MESH

HEAD_DIM = 64
ATTN_HEADS = 8
ATTN_KV = 2
ATTN_BLOCK = 128
WINDOW = 128
GRID_W = 64
HG_HEADS = 4
HG_D = 128
HG_CHUNK = 64
RET_HEADS = 4
RET_DK = 256
RET_DV = 512
RET_CHUNK = 64
NEG = -1e30

ADAM_LR = 0.001
ADAM_B1 = 0.9
ADAM_B2 = 0.999
ADAM_EPS = 1e-08
ADAM_WD = 0.01
ADAM_STEP = 10

VMEM_LIMIT = 60 * 1024 * 1024


def _cp(*sem):
    return pltpu.CompilerParams(dimension_semantics=sem, vmem_limit_bytes=VMEM_LIMIT)


def _nn(a, b):
    return jnp.dot(a, b, preferred_element_type=F32)


def _nt(a, b):
    return lax.dot_general(a, b, (((1,), (1,)), ((), ())), preferred_element_type=F32)


def _tn(a, b):
    return lax.dot_general(a, b, (((0,), (0,)), ((), ())), preferred_element_type=F32)


ACT = BF16


def _bf(a):
    return a.astype(ACT)


def _sig(x):
    return jax.nn.sigmoid(x)


def _split3(x):
    h = x.astype(BF16)
    r = x - h.astype(F32)
    m = r.astype(BF16)
    lo = (r - m.astype(F32)).astype(BF16)
    return h, m, lo


def _nn3(m01, x):
    h, m, lo = _split3(x)
    return _nn(m01, h) + _nn(m01, m) + _nn(m01, lo)


def _nn3r(x, m01):
    h, m, lo = _split3(x)
    return _nn(h, m01) + _nn(m, m01) + _nn(lo, m01)


def _full(shape):
    nd = len(shape)
    return pl.BlockSpec(shape, lambda *a: (0,) * nd, pipeline_mode=pl.Buffered(1))


def _whole(shape):
    nd = len(shape)
    return pl.BlockSpec(shape, lambda *a: (0,) * nd)


def _rows(tm, width):
    return pl.BlockSpec((tm, width), lambda i: (i, 0))


def _ctx_lat(width):
    return pl.BlockSpec((1, 1, width), lambda i: (jnp.minimum(i, 1), 0, 0))


def _acc_ctx_lat(ref, i, val):
    @pl.when(i <= 1)
    def _():
        ref[...] = val.reshape(ref.shape)

    @pl.when(i > 1)
    def _():
        ref[...] += val.reshape(ref.shape)


def _acc_all(ref, i, val):
    @pl.when(i == 0)
    def _():
        ref[...] = val.reshape(ref.shape)

    @pl.when(i > 0)
    def _():
        ref[...] += val.reshape(ref.shape)


def _tile(n, cap):
    best = None
    for t in range(128, min(n, cap) + 1, 128):
        if n % t == 0:
            best = t
    return n if best is None else best


def _norm_mod(xv, g, shift, scale):
    r = lax.rsqrt(jnp.mean(xv * xv, axis=-1, keepdims=True) + EPS)
    xhat = xv * r
    n = xhat * g
    return r, xhat, n, n * (1.0 + scale) + shift


def _norm_mod_bwd(dh, r, xhat, n, g, scale):
    dshift = jnp.sum(dh, axis=0, keepdims=True)
    dscale = jnp.sum(dh * n, axis=0, keepdims=True)
    dn = dh * (1.0 + scale)
    dg = jnp.sum(dn * xhat, axis=0, keepdims=True)
    dxh = dn * g
    dx = r * (dxh - xhat * jnp.mean(dxh * xhat, axis=-1, keepdims=True))
    return dx, dshift, dscale, dg


def _pre_fwd(x, gain, ms, w, splits, tm, name):
    T, dm = x.shape

    def body(x_ref, g_ref, ms_ref, w_ref, *outs):
        ms_v = ms_ref[0]
        h = _norm_mod(x_ref[...], g_ref[...], ms_v[:, :dm], ms_v[:, dm:])[3]
        hb = _bf(h)
        for (s, e), o_ref in zip(splits, outs):
            o_ref[...] = _nn(hb, w_ref[:, s:e])

    return pl.pallas_call(
        body, name=name, grid=(T // tm,),
        in_specs=[_rows(tm, dm), _full((1, dm)), _ctx_lat(2 * dm), _full(w.shape)],
        out_specs=[_rows(tm, e - s) for s, e in splits],
        out_shape=[jax.ShapeDtypeStruct((T, e - s), F32) for s, e in splits],
        compiler_params=_cp("arbitrary"),
    )(x, gain, ms, w)


def _pre_bwd(x, dx_in, gain, ms, w, pieces, tm, name):
    T, dm = x.shape
    n_out = w.shape[1]
    flat = [a for _, arrs in pieces for a in arrs]

    def body(x_ref, dxin_ref, g_ref, ms_ref, w_ref, *rest):
        p_refs = rest[:len(flat)]
        dx_ref, h_ref, dp_ref, dms_ref, dg_ref = rest[len(flat):]
        i = pl.program_id(0)
        ms_v = ms_ref[0]
        g = g_ref[...]
        scale = ms_v[:, dm:]
        r, xhat, n, h = _norm_mod(x_ref[...], g, ms_v[:, :dm], scale)
        h_ref[...] = _bf(h)
        dh = jnp.zeros((tm, dm), F32)
        k = 0
        for s, arrs in pieces:
            v = p_refs[k][...].astype(F32)
            for j in range(1, len(arrs)):
                v = v + p_refs[k + j][...].astype(F32)
            k += len(arrs)
            vb = _bf(v)
            wd = vb.shape[1]
            dp_ref[:, s:s + wd] = vb
            dh = dh + _nt(vb, w_ref[:, s:s + wd])
        dx, dshift, dscale, dg = _norm_mod_bwd(dh, r, xhat, n, g, scale)
        dx_ref[...] = dxin_ref[...] + dx
        _acc_ctx_lat(dms_ref, i, jnp.concatenate([dshift, dscale], axis=1))
        _acc_all(dg_ref, i, dg)

    return pl.pallas_call(
        body, name=name, grid=(T // tm,),
        in_specs=[_rows(tm, dm), _rows(tm, dm), _full((1, dm)), _ctx_lat(2 * dm), _full(w.shape)]
        + [_rows(tm, a.shape[1]) for a in flat],
        out_specs=[_rows(tm, dm), _rows(tm, dm), _rows(tm, n_out), _ctx_lat(2 * dm), _whole((1, dm))],
        out_shape=[jax.ShapeDtypeStruct((T, dm), F32), jax.ShapeDtypeStruct((T, dm), ACT),
                   jax.ShapeDtypeStruct((T, n_out), ACT), jax.ShapeDtypeStruct((2, 1, 2 * dm), F32),
                   jax.ShapeDtypeStruct((1, dm), F32)],
        compiler_params=_cp("arbitrary"),
    )(x, dx_in, gain, ms, w, *flat)


def _ffn_fwd(x1, gain, ms, w_in, w_out, tm, name):
    T, dm = x1.shape
    fh = w_out.shape[0]

    def body(x_ref, g_ref, ms_ref, wi_ref, wo_ref, x2_ref, u_ref, f_ref):
        ms_v = ms_ref[0]
        xv = x_ref[...]
        h = _norm_mod(xv, g_ref[...], ms_v[:, :dm], ms_v[:, dm:2 * dm])[3]
        u = _nn(_bf(h), wi_ref[...])
        u_ref[...] = _bf(u)
        gt = u[:, :fh]
        act = gt * _sig(gt) * u[:, fh:]
        f = _nn(_bf(act), wo_ref[...])
        f_ref[...] = _bf(f)
        x2_ref[...] = xv + ms_v[:, 2 * dm:] * f

    return pl.pallas_call(
        body, name=name, grid=(T // tm,),
        in_specs=[_rows(tm, dm), _full((1, dm)), _ctx_lat(3 * dm), _full(w_in.shape), _full(w_out.shape)],
        out_specs=[_rows(tm, dm), _rows(tm, 2 * fh), _rows(tm, dm)],
        out_shape=[jax.ShapeDtypeStruct((T, dm), F32), jax.ShapeDtypeStruct((T, 2 * fh), ACT),
                   jax.ShapeDtypeStruct((T, dm), ACT)],
        compiler_params=_cp("arbitrary"),
    )(x1, gain, ms, w_in, w_out)


def _ffn_bwd(x1, dx2, u, f, gain, ms, w_in, w_out, tm, name):
    T, dm = x1.shape
    fh = w_out.shape[0]

    def body(x_ref, dx2_ref, u_ref, f_ref, g_ref, ms_ref, wi_ref, wo_ref,
             dx1_ref, h_ref, du_ref, act_ref, df_ref, dms_ref, dg_ref):
        i = pl.program_id(0)
        ms_v = ms_ref[0]
        g = g_ref[...]
        scale = ms_v[:, dm:2 * dm]
        gate = ms_v[:, 2 * dm:]
        r, xhat, n, h = _norm_mod(x_ref[...], g, ms_v[:, :dm], scale)
        h_ref[...] = _bf(h)
        dx2 = dx2_ref[...]
        dgate = jnp.sum(dx2 * f_ref[...].astype(F32), axis=0, keepdims=True)
        dfb = _bf(dx2 * gate)
        df_ref[...] = dfb
        da = _nt(dfb, wo_ref[...])
        uv = u_ref[...].astype(F32)
        gt = uv[:, :fh]
        up = uv[:, fh:]
        s = _sig(gt)
        sg = gt * s
        act_ref[...] = _bf(sg * up)
        dgt = _bf(da * up * (s * (1.0 + gt * (1.0 - s))))
        dup = _bf(da * sg)
        du_ref[:, :fh] = dgt
        du_ref[:, fh:] = dup
        dh = _nt(dgt, wi_ref[:, :fh]) + _nt(dup, wi_ref[:, fh:])
        dx, dshift, dscale, dg = _norm_mod_bwd(dh, r, xhat, n, g, scale)
        dx1_ref[...] = dx2 + dx
        _acc_ctx_lat(dms_ref, i, jnp.concatenate([dshift, dscale, dgate], axis=1))
        _acc_all(dg_ref, i, dg)

    return pl.pallas_call(
        body, name=name, grid=(T // tm,),
        in_specs=[_rows(tm, dm), _rows(tm, dm), _rows(tm, 2 * fh), _rows(tm, dm), _full((1, dm)), _ctx_lat(3 * dm),
                  _full(w_in.shape), _full(w_out.shape)],
        out_specs=[_rows(tm, dm), _rows(tm, dm), _rows(tm, 2 * fh), _rows(tm, fh), _rows(tm, dm),
                   _ctx_lat(3 * dm), _whole((1, dm))],
        out_shape=[jax.ShapeDtypeStruct((T, dm), F32), jax.ShapeDtypeStruct((T, dm), ACT),
                   jax.ShapeDtypeStruct((T, 2 * fh), ACT), jax.ShapeDtypeStruct((T, fh), ACT),
                   jax.ShapeDtypeStruct((T, dm), ACT), jax.ShapeDtypeStruct((2, 1, 3 * dm), F32),
                   jax.ShapeDtypeStruct((1, dm), F32)],
        compiler_params=_cp("arbitrary"),
    )(x1, dx2, u, f, gain, ms, w_in, w_out)


def _wgrad(a, b, name):
    T, K = a.shape
    N = b.shape[1]
    tk, tn, tt = _tile(K, 1024), _tile(N, 1024), _tile(T, 1024)
    nt = T // tt

    def body(a_ref, b_ref, o_ref, acc_ref):
        t = pl.program_id(2)
        part = _tn(a_ref[...], b_ref[...])

        @pl.when(t == 0)
        def _():
            acc_ref[...] = part

        @pl.when(t > 0)
        def _():
            acc_ref[...] += part

        @pl.when(t == nt - 1)
        def _():
            o_ref[...] = acc_ref[...].astype(o_ref.dtype)

    return pl.pallas_call(
        body, name=name, grid=(K // tk, N // tn, nt),
        in_specs=[pl.BlockSpec((tt, tk), lambda i, j, t: (t, i)), pl.BlockSpec((tt, tn), lambda i, j, t: (t, j))],
        out_specs=pl.BlockSpec((tk, tn), lambda i, j, t: (i, j)),
        out_shape=jax.ShapeDtypeStruct((K, N), ACT),
        scratch_shapes=[pltpu.VMEM((tk, tn), F32)],
        compiler_params=_cp("parallel", "parallel", "arbitrary"),
    )(a, b)


def _post_fwd(x, o_fw, o_bw, g_src, g_blk, gain, a, w_out, ms, dvh, tm, name):
    T, dm = x.shape
    hv = o_fw.shape[1]
    aw = 0 if a is None else a.shape[1]
    has_gain = gain is not None

    def body(*refs):
        refs = list(refs)
        x_ref, of_ref, ob_ref, g_ref = refs[:4]
        k = 4
        gain_ref = a_ref = None
        if has_gain:
            gain_ref = refs[k]
            k += 1
        if aw:
            a_ref = refs[k]
            k += 1
        w_ref, ms_ref, x1_ref, z_ref = refs[k:k + 4]
        o = of_ref[...] + ob_ref[...]
        gr = g_ref[...]
        if aw:
            z_ref[:, :aw] = _bf(a_ref[...])
        for hd in range(hv // dvh):
            sl = slice(hd * dvh, (hd + 1) * dvh)
            oh = o[:, sl]
            gh = gr[:, sl]
            r = lax.rsqrt(jnp.mean(oh * oh, axis=-1, keepdims=True) + EPS)
            y = oh * r
            if has_gain:
                y = y * gain_ref[...]
            y = y * (gh * _sig(gh))
            z_ref[:, aw + hd * dvh:aw + (hd + 1) * dvh] = _bf(y)
        yp = _nn(z_ref[...], w_ref[...])
        x1_ref[...] = x_ref[...] + ms_ref[0] * yp

    ins = [x, o_fw, o_bw, g_src]
    specs = [_rows(tm, dm), _rows(tm, hv), _rows(tm, hv), pl.BlockSpec((tm, hv), lambda i: (i, g_blk))]
    if has_gain:
        ins.append(gain)
        specs.append(_full(gain.shape))
    if aw:
        ins.append(a)
        specs.append(_rows(tm, aw))
    ins += [w_out, ms]
    specs += [_full(w_out.shape), _ctx_lat(dm)]
    return pl.pallas_call(
        body, name=name, grid=(T // tm,), in_specs=specs,
        out_specs=[_rows(tm, dm), _rows(tm, aw + hv)],
        out_shape=[jax.ShapeDtypeStruct((T, dm), F32), jax.ShapeDtypeStruct((T, aw + hv), ACT)],
        compiler_params=_cp("arbitrary"),
    )(*ins)


def _post_bwd(dx1, z, o_fw, o_bw, g_src, g_blk, gain, w_out, ms, aw, dvh, tm, name):
    T, dm = dx1.shape
    hv = o_fw.shape[1]
    has_gain = gain is not None

    def body(*refs):
        refs = list(refs)
        dx1_ref, z_ref, of_ref, ob_ref, g_ref = refs[:5]
        k = 5
        gain_ref = None
        if has_gain:
            gain_ref = refs[k]
            k += 1
        w_ref, ms_ref = refs[k:k + 2]
        k += 2
        do_ref, dgr_ref = refs[k:k + 2]
        k += 2
        da_ref = None
        if aw:
            da_ref = refs[k]
            k += 1
        dy_ref, dgate_ref, dgain_ref = refs[k:k + 3]
        i = pl.program_id(0)
        dx1v = dx1_ref[...]
        zb = z_ref[...]
        yp = _nn(zb, w_ref[...])
        _acc_ctx_lat(dgate_ref, i, jnp.sum(dx1v * yp, axis=0, keepdims=True))
        dyb = _bf(dx1v * ms_ref[0])
        dy_ref[...] = dyb
        dz = _nt(dyb, w_ref[...])
        if aw:
            da_ref[...] = dz[:, :aw]
        o = of_ref[...] + ob_ref[...]
        gr = g_ref[...]
        dgain = jnp.zeros((1, dvh), F32)
        for hd in range(hv // dvh):
            sl = slice(hd * dvh, (hd + 1) * dvh)
            oh = o[:, sl]
            gh = gr[:, sl]
            dyh = dz[:, aw + hd * dvh:aw + (hd + 1) * dvh]
            r = lax.rsqrt(jnp.mean(oh * oh, axis=-1, keepdims=True) + EPS)
            n = oh * r
            s = _sig(gh)
            sl_g = gh * s
            gn = gain_ref[...] if has_gain else 1.0
            dgr_ref[:, sl] = dyh * n * gn * (s * (1.0 + gh * (1.0 - s)))
            dn = dyh * gn * sl_g
            dgain = dgain + jnp.sum(dyh * n * sl_g, axis=0, keepdims=True)
            do_ref[:, sl] = r * (dn - n * jnp.mean(dn * n, axis=-1, keepdims=True))
        _acc_all(dgain_ref, i, dgain)

    ins = [dx1, z, o_fw, o_bw, g_src]
    specs = [_rows(tm, dm), _rows(tm, aw + hv), _rows(tm, hv), _rows(tm, hv),
             pl.BlockSpec((tm, hv), lambda i: (i, g_blk))]
    if has_gain:
        ins.append(gain)
        specs.append(_full(gain.shape))
    ins += [w_out, ms]
    specs += [_full(w_out.shape), _ctx_lat(dm)]
    out_specs = [_rows(tm, hv), _rows(tm, hv)]
    out_shape = [jax.ShapeDtypeStruct((T, hv), F32), jax.ShapeDtypeStruct((T, hv), F32)]
    if aw:
        out_specs.append(_rows(tm, aw))
        out_shape.append(jax.ShapeDtypeStruct((T, aw), F32))
    out_specs += [_rows(tm, dm), _ctx_lat(dm), _whole((1, dvh))]
    out_shape += [jax.ShapeDtypeStruct((T, dm), ACT), jax.ShapeDtypeStruct((2, 1, dm), F32),
                  jax.ShapeDtypeStruct((1, dvh), F32)]
    return pl.pallas_call(
        body, name=name, grid=(T // tm,), in_specs=specs, out_specs=out_specs, out_shape=out_shape,
        compiler_params=_cp("arbitrary"),
    )(*ins)


def _loss_bwd(x, target, tm, name):
    T, dm = x.shape

    def body(x_ref, t_ref, dx_ref, loss_ref):
        i = pl.program_id(0)

        @pl.when(i == 0)
        def _():
            dx_ref[...] = jnp.zeros_like(dx_ref)
            loss_ref[...] = jnp.zeros_like(loss_ref)

        @pl.when(i > 0)
        def _():
            e = x_ref[...] - t_ref[...]
            dx_ref[...] = e * (1.0 / dm)
            loss_ref[...] += jnp.sum(e * e) * (0.5 / dm)

    return pl.pallas_call(
        body, name=name, grid=(T // tm,),
        in_specs=[_rows(tm, dm), pl.BlockSpec((tm, dm), lambda i: (jnp.maximum(i - 1, 0), 0))],
        out_specs=[_rows(tm, dm), _whole((1, 1))],
        out_shape=[jax.ShapeDtypeStruct((T, dm), F32), jax.ShapeDtypeStruct((1, 1), F32)],
        compiler_params=_cp("arbitrary"),
    )(x, target)


def _swap_matrix():
    r = lax.broadcasted_iota(jnp.int32, (HEAD_DIM, HEAD_DIM), 0)
    c = lax.broadcasted_iota(jnp.int32, (HEAD_DIM, HEAD_DIM), 1)
    return jnp.where((r + HEAD_DIM // 2) % HEAD_DIM == c, 1.0, 0.0).astype(BF16)


def _qk_prep_fwd(raw, gains, cos2, sin2, tq, name):
    nh, T, hd = raw.shape

    def body(x_ref, g_ref, c_ref, s_ref, o_ref):
        hidx = pl.program_id(0)
        xv = x_ref[0]
        r = lax.rsqrt(jnp.mean(xv * xv, axis=-1, keepdims=True) + EPS)
        n = xv * r * g_ref[0]
        y = n * c_ref[...] + _nn3r(n, _swap_matrix()) * s_ref[...]
        sc = jnp.where(hidx < ATTN_HEADS, HEAD_DIM ** -0.5, 1.0)
        o_ref[0] = _bf(y * sc)

    return pl.pallas_call(
        body, name=name, grid=(nh, T // tq),
        in_specs=[pl.BlockSpec((1, tq, hd), lambda h, i: (h, i, 0)), pl.BlockSpec((1, 1, hd), lambda h, i: (h, 0, 0)),
                  pl.BlockSpec((tq, hd), lambda h, i: (i, 0)), pl.BlockSpec((tq, hd), lambda h, i: (i, 0))],
        out_specs=pl.BlockSpec((1, tq, hd), lambda h, i: (h, i, 0)),
        out_shape=jax.ShapeDtypeStruct((nh, T, hd), ACT),
        compiler_params=_cp("arbitrary", "arbitrary"),
    )(raw, gains, cos2, sin2)


def _qk_prep_bwd(dy, raw, gains, cos2, sin2, tq, name):
    nh, T, hd = raw.shape

    def body(dy_ref, x_ref, g_ref, c_ref, s_ref, dx_ref, dg_ref):
        hidx = pl.program_id(0)
        i = pl.program_id(1)
        xv = x_ref[0]
        g = g_ref[0]
        r = lax.rsqrt(jnp.mean(xv * xv, axis=-1, keepdims=True) + EPS)
        xhat = xv * r
        sc = jnp.where(hidx < ATTN_HEADS, HEAD_DIM ** -0.5, 1.0)
        dyv = dy_ref[0] * sc
        dn = dyv * c_ref[...] + _nn3r(dyv * s_ref[...], _swap_matrix())
        _acc_all(dg_ref, i, jnp.sum(dn * xhat, axis=0, keepdims=True))
        dxh = dn * g
        dx_ref[0] = r * (dxh - xhat * jnp.mean(dxh * xhat, axis=-1, keepdims=True))

    return pl.pallas_call(
        body, name=name, grid=(nh, T // tq),
        in_specs=[pl.BlockSpec((1, tq, hd), lambda h, i: (h, i, 0)), pl.BlockSpec((1, tq, hd), lambda h, i: (h, i, 0)),
                  pl.BlockSpec((1, 1, hd), lambda h, i: (h, 0, 0)),
                  pl.BlockSpec((tq, hd), lambda h, i: (i, 0)), pl.BlockSpec((tq, hd), lambda h, i: (i, 0))],
        out_specs=[pl.BlockSpec((1, tq, hd), lambda h, i: (h, i, 0)), pl.BlockSpec((1, 1, hd), lambda h, i: (h, 0, 0))],
        out_shape=[jax.ShapeDtypeStruct((nh, T, hd), F32), jax.ShapeDtypeStruct((nh, 1, hd), F32)],
        compiler_params=_cp("arbitrary", "arbitrary"),
    )(dy, raw, gains, cos2, sin2)


def _attn_scores(q, k_ref, i, lc, T, sink):
    blk = ATTN_BLOCK
    kc = k_ref[0, pl.ds(blk, lc), :]
    kw = k_ref[0, pl.ds(pl.multiple_of(i * blk, blk), 3 * blk), :]
    s_c = _nt(q, kc)
    s_w = _nt(q, kw)
    row = lax.broadcasted_iota(jnp.int32, (4 * blk, 1), 0)
    qpos = i * blk + (row & (blk - 1))
    kpos = (i - 1) * blk + lax.broadcasted_iota(jnp.int32, (1, 3 * blk), 1)
    valid = (qpos >= lc) & (kpos >= lc) & (kpos < T) & (jnp.abs(kpos - qpos) <= WINDOW)
    s_w = jnp.where(valid, s_w, NEG)
    return kc, kw, s_c, s_w


def _attn_fwd(qt, kp, vp, sinkb, lc, name):
    nh, T, hd = qt.shape
    blk = ATTN_BLOCK
    g = nh // ATTN_KV

    def body(q_ref, k_ref, v_ref, sink_ref, o_ref, lse_ref):
        i = pl.program_id(1)
        q = q_ref[...].reshape(g * blk, hd)
        sink = sink_ref[0]
        kc, kw, s_c, s_w = _attn_scores(q, k_ref, i, lc, T, sink)
        m = jnp.maximum(jnp.maximum(jnp.max(s_c, axis=-1, keepdims=True), jnp.max(s_w, axis=-1, keepdims=True)), sink)
        e_c = jnp.exp(s_c - m)
        e_w = jnp.exp(s_w - m)
        den = jnp.exp(sink - m) + jnp.sum(e_c, axis=-1, keepdims=True) + jnp.sum(e_w, axis=-1, keepdims=True)
        inv = 1.0 / den
        vc = v_ref[0, pl.ds(blk, lc), :]
        vw = v_ref[0, pl.ds(pl.multiple_of(i * blk, blk), 3 * blk), :]
        o = _nn(_bf(e_c * inv), vc) + _nn(_bf(e_w * inv), vw)
        o_ref[...] = o.reshape(g, blk, hd)
        lse_ref[...] = (m + jnp.log(den)).reshape(g, blk, 1)

    return pl.pallas_call(
        body, name=name, grid=(ATTN_KV, T // blk),
        in_specs=[pl.BlockSpec((g, blk, hd), lambda kv, i: (kv, i, 0)),
                  pl.BlockSpec((1, T + 2 * blk, hd), lambda kv, i: (kv, 0, 0)),
                  pl.BlockSpec((1, T + 2 * blk, hd), lambda kv, i: (kv, 0, 0)),
                  pl.BlockSpec((1, g * blk, 1), lambda kv, i: (kv, 0, 0))],
        out_specs=[pl.BlockSpec((g, blk, hd), lambda kv, i: (kv, i, 0)),
                   pl.BlockSpec((g, blk, 1), lambda kv, i: (kv, i, 0))],
        out_shape=[jax.ShapeDtypeStruct((nh, T, hd), F32), jax.ShapeDtypeStruct((nh, T, 1), F32)],
        compiler_params=_cp("arbitrary", "arbitrary"),
    )(qt, kp, vp, sinkb)


def _attn_bwd(qt, kp, vp, sinkb, o, lse, do, lc, name):
    nh, T, hd = qt.shape
    blk = ATTN_BLOCK
    g = nh // ATTN_KV

    def body(q_ref, k_ref, v_ref, sink_ref, o_ref, lse_ref, do_ref, dq_ref, dk_ref, dv_ref, ds_ref):
        i = pl.program_id(1)

        @pl.when(i == 0)
        def _():
            dk_ref[...] = jnp.zeros_like(dk_ref)
            dv_ref[...] = jnp.zeros_like(dv_ref)
            ds_ref[...] = jnp.zeros_like(ds_ref)

        q = q_ref[...].reshape(g * blk, hd)
        sink = sink_ref[0]
        lse = lse_ref[...].reshape(g * blk, 1)
        dov = do_ref[...].reshape(g * blk, hd)
        delta = jnp.sum(dov * o_ref[...].reshape(g * blk, hd), axis=-1, keepdims=True)
        kc, kw, s_c, s_w = _attn_scores(q, k_ref, i, lc, T, sink)
        p_c = jnp.exp(s_c - lse)
        p_w = jnp.exp(s_w - lse)
        win = pl.ds(pl.multiple_of(i * blk, blk), 3 * blk)
        vc = v_ref[0, pl.ds(blk, lc), :]
        vw = v_ref[0, win, :]
        dob = _bf(dov)
        ds_c = _bf(p_c * (_nt(dob, vc) - delta))
        ds_w = _bf(p_w * (_nt(dob, vw) - delta))
        dsr = -jnp.exp(sink - lse) * delta
        for hh in range(g):
            ds_ref[0, hh:hh + 1, :] += jnp.sum(dsr[hh * blk:(hh + 1) * blk, :], axis=0, keepdims=True)
        dq_ref[...] = (_nn(ds_c, kc) + _nn(ds_w, kw)).reshape(g, blk, hd)
        dk_ref[0, pl.ds(blk, lc), :] += _tn(ds_c, q)
        dk_ref[0, win, :] += _tn(ds_w, q)
        dv_ref[0, pl.ds(blk, lc), :] += _tn(_bf(p_c), dob)
        dv_ref[0, win, :] += _tn(_bf(p_w), dob)

    qspec = pl.BlockSpec((g, blk, hd), lambda kv, i: (kv, i, 0))
    kspec = pl.BlockSpec((1, T + 2 * blk, hd), lambda kv, i: (kv, 0, 0))
    lspec = pl.BlockSpec((g, blk, 1), lambda kv, i: (kv, i, 0))
    return pl.pallas_call(
        body, name=name, grid=(ATTN_KV, T // blk),
        in_specs=[qspec, kspec, kspec, pl.BlockSpec((1, g * blk, 1), lambda kv, i: (kv, 0, 0)), qspec, lspec, qspec],
        out_specs=[qspec, kspec, kspec, pl.BlockSpec((1, g, 1), lambda kv, i: (kv, 0, 0))],
        out_shape=[jax.ShapeDtypeStruct((nh, T, hd), F32), jax.ShapeDtypeStruct((ATTN_KV, T + 2 * blk, hd), F32),
                   jax.ShapeDtypeStruct((ATTN_KV, T + 2 * blk, hd), F32), jax.ShapeDtypeStruct((ATTN_KV, g, 1), F32)],
        compiler_params=_cp("arbitrary", "arbitrary"),
    )(qt, kp, vp, sinkb, o, lse, do)


def _fw_chunk(s, nc, nt):
    return s


def _bw_chunk(s, nc, nt):
    return jnp.where(s < nc, nc - 1 - s, nt - 1 - (s - nc))


def _tri(c, rev):
    r = lax.broadcasted_iota(jnp.int32, (c, c), 0)
    k = lax.broadcasted_iota(jnp.int32, (c, c), 1)
    return (k >= r) if rev else (k <= r)


def _gla_gates(z, lb, rev):
    c = HG_CHUNK
    sg = _sig(z)
    f = lb + (1.0 - lb) * sg
    cum = _nn3(jnp.where(_tri(c, rev), 1.0, 0.0).astype(BF16), jnp.log(f))
    mid = c - 1 - c // 2 if rev else c // 2
    last = 0 if rev else c - 1
    return sg, f, cum, cum[mid:mid + 1], cum[last:last + 1], last


def _lower_bound(lbraw_ref):
    lr = lbraw_ref[...]
    return _sig(lr[0:1] - lr[1:2])


def _gla_fwd(pb, lbraw, lc, name):
    T = pb.shape[0]
    c, hw, d = HG_CHUNK, HG_HEADS * HG_D, HG_D
    nt, nc = T // c, lc // c
    orders = (_fw_chunk, _bw_chunk)

    def body(qf, zf, vf, qb, zb, vb, lb_ref, of_ref, ob_ref, sf_ref, sb_ref, st_ref):
        @pl.when(pl.program_id(0) == 0)
        def _():
            st_ref[...] = jnp.zeros_like(st_ref)

        lb_all = _lower_bound(lb_ref)
        for dr, (q_ref, z_ref, v_ref, o_ref, s_ref) in enumerate(((qf, zf, vf, of_ref, sf_ref), (qb, zb, vb, ob_ref, sb_ref))):
            rev = dr == 1
            mask = _tri(c, rev)
            for h in range(HG_HEADS):
                sl = slice(h * d, (h + 1) * d)
                qr = q_ref[:, sl]
                q = qr * _sig(qr)
                v = _bf(v_ref[:, sl])
                _, f, cum, ref, last, _ = _gla_gates(z_ref[:, sl], lb_all[:, sl], rev)
                k = 1.0 - f
                a = jnp.where(mask, _nt(_bf(q * jnp.exp(cum - ref)), _bf(k * jnp.exp(ref - cum))), 0.0)
                st = st_ref[dr, h]
                stb = _bf(st)
                s_ref[0, h] = stb
                o_ref[:, sl] = _nn(_bf(a), v) + _nt(_bf(q * jnp.exp(cum)), stb)
                st_ref[dr, h] = st * jnp.exp(last) + _tn(v, _bf(k * jnp.exp(last - cum)))

    def col(order, blkcol):
        return pl.BlockSpec((c, hw), lambda s: (order(s, nc, nt), blkcol))

    def st_spec(order):
        return pl.BlockSpec((1, HG_HEADS, d, d), lambda s: (order(s, nc, nt), 0, 0, 0))

    in_specs = []
    for dr, order in enumerate(orders):
        in_specs += [col(order, 0), col(order, 1 + dr), col(order, 3)]
    in_specs.append(_full(lbraw.shape))
    return pl.pallas_call(
        body, name=name, grid=(nt,), in_specs=in_specs,
        out_specs=[col(_fw_chunk, 0), col(_bw_chunk, 0), st_spec(_fw_chunk), st_spec(_bw_chunk)],
        out_shape=[jax.ShapeDtypeStruct((T, hw), F32), jax.ShapeDtypeStruct((T, hw), F32),
                   jax.ShapeDtypeStruct((nt, HG_HEADS, d, d), ACT), jax.ShapeDtypeStruct((nt, HG_HEADS, d, d), ACT)],
        scratch_shapes=[pltpu.VMEM((2, HG_HEADS, d, d), F32)],
        compiler_params=_cp("arbitrary"),
    )(pb, pb, pb, pb, pb, pb, lbraw)


def _gla_bwd(pb, lbraw, s_fw, s_bw, do, lc, name):
    T = pb.shape[0]
    c, hw, d = HG_CHUNK, HG_HEADS * HG_D, HG_D
    nt, nc = T // c, lc // c

    def rfw(s, nc_, nt_):
        return _fw_chunk(nt_ - 1 - s, nc_, nt_)

    def rbw(s, nc_, nt_):
        return _bw_chunk(nt_ - 1 - s, nc_, nt_)

    def body(qf, zf, vf, sf, dof, qb, zb, vb, sb, dob_, lb_ref,
             dqf, dzf, dvf, dqb, dzb, dvb, dlb_ref, dst_ref):
        step = pl.program_id(0)

        @pl.when(step == 0)
        def _():
            dst_ref[...] = jnp.zeros_like(dst_ref)

        lb_all = _lower_bound(lb_ref)
        dlb_parts = []
        sets = ((qf, zf, vf, sf, dof, dqf, dzf, dvf), (qb, zb, vb, sb, dob_, dqb, dzb, dvb))
        for dr, (q_ref, z_ref, v_ref, s_ref, do_ref, dq_ref, dz_ref, dv_ref) in enumerate(sets):
            rev = dr == 1
            mask = _tri(c, rev)
            acc_t = jnp.where(_tri(c, not rev), 1.0, 0.0).astype(BF16)
            dlb_heads = []
            for h in range(HG_HEADS):
                sl = slice(h * d, (h + 1) * d)
                lb = lb_all[:, sl]
                qr = q_ref[:, sl]
                sq = _sig(qr)
                q = qr * sq
                vbf = _bf(v_ref[:, sl])
                sg, f, cum, ref, last, last_row = _gla_gates(z_ref[:, sl], lb, rev)
                k = 1.0 - f
                e_qr = jnp.exp(cum - ref)
                e_kr = jnp.exp(ref - cum)
                e_q = jnp.exp(cum)
                e_kl = jnp.exp(last - cum)
                el = jnp.exp(last)
                q1 = q * e_qr
                k1 = k * e_kr
                q2 = q * e_q
                k2 = k * e_kl
                q1b, k1b, q2b, k2b = _bf(q1), _bf(k1), _bf(q2), _bf(k2)
                a = jnp.where(mask, _nt(q1b, k1b), 0.0)
                dob = _bf(do_ref[:, sl])
                stb = s_ref[0, h]
                dst = dst_ref[dr, h]
                dstb = _bf(dst)
                da = _bf(jnp.where(mask, _nt(dob, vbf), 0.0))
                dv_ref[:, sl] = _tn(_bf(a), dob) + _nt(k2b, dstb)
                dq1 = _nn(da, k1b)
                dk1 = _tn(da, q1b)
                dq2 = _nn(dob, stb)
                dk2 = _nn(vbf, dstb)
                dst_ref[dr, h] = _tn(dob, q2b) + dst * el
                dq = dq1 * e_qr + dq2 * e_q
                dk = dk1 * e_kr + dk2 * e_kl
                dcum = dq1 * q1 - dk1 * k1 + dq2 * q2 - dk2 * k2
                dlast = (jnp.sum(dk2 * k2, axis=0, keepdims=True)
                         + jnp.sum(dst * stb.astype(F32), axis=0, keepdims=True) * el)
                rowid = lax.broadcasted_iota(jnp.int32, (c, 1), 0)
                dcum = dcum + jnp.where(rowid == last_row, dlast, 0.0)
                dlf = _nn3(acc_t, dcum)
                df = dlf / f - dk
                dz_ref[:, sl] = df * (1.0 - lb) * sg * (1.0 - sg)
                dlb_heads.append(jnp.sum(df * (1.0 - sg), axis=0, keepdims=True))
                dq_ref[:, sl] = dq * (sq * (1.0 + qr * (1.0 - sq)))
            dlb_parts.append(jnp.concatenate(dlb_heads, axis=1))
        _acc_all(dlb_ref, step, dlb_parts[0] + dlb_parts[1])

    def col(order, blkcol):
        return pl.BlockSpec((c, hw), lambda s: (order(s, nc, nt), blkcol))

    def st_spec(order):
        return pl.BlockSpec((1, HG_HEADS, d, d), lambda s: (order(s, nc, nt), 0, 0, 0))

    in_specs = []
    for dr, order in enumerate((rfw, rbw)):
        in_specs += [col(order, 0), col(order, 1 + dr), col(order, 3), st_spec(order), col(order, 0)]
    in_specs.append(_full(lbraw.shape))
    out_specs = [col(rfw, 0)] * 3 + [col(rbw, 0)] * 3 + [_whole((1, hw))]
    out_shape = [jax.ShapeDtypeStruct((T, hw), F32)] * 6 + [jax.ShapeDtypeStruct((1, hw), F32)]
    return pl.pallas_call(
        body, name=name, grid=(nt,), in_specs=in_specs, out_specs=out_specs, out_shape=out_shape,
        scratch_shapes=[pltpu.VMEM((2, HG_HEADS, d, d), F32)],
        compiler_params=_cp("arbitrary"),
    )(pb, pb, pb, s_fw, do, pb, pb, pb, s_bw, do, lbraw)


def _ret_log_gamma(h, rev):
    hh = RET_HEADS - 1 - h if rev else h
    return math.log(1.0 - 2.0 ** (-5.0 - hh))


def _rope(x, cos, sin):
    half = x.shape[1] // 2
    x1, x2 = x[:, :half], x[:, half:]
    return jnp.concatenate([x1 * cos - x2 * sin, x2 * cos + x1 * sin], axis=1)


def _unrope(dy, cos, sin):
    half = dy.shape[1] // 2
    d1, d2 = dy[:, :half], dy[:, half:]
    return jnp.concatenate([d1 * cos + d2 * sin, d2 * cos - d1 * sin], axis=1)


def _ret_decays(lg, rev):
    c = RET_CHUNK
    r = lax.broadcasted_iota(jnp.int32, (c, c), 0)
    k = lax.broadcasted_iota(jnp.int32, (c, c), 1)
    rel = (k - r) if rev else (r - k)
    dm = jnp.where(rel >= 0, jnp.exp(lg * jnp.maximum(rel, 0).astype(F32)), 0.0)
    pos = lax.broadcasted_iota(jnp.int32, (c, 1), 0).astype(F32)
    if rev:
        qdec = jnp.exp(lg * (c - pos))
        kdec = jnp.exp(lg * pos)
    else:
        qdec = jnp.exp(lg * (pos + 1.0))
        kdec = jnp.exp(lg * (c - 1.0 - pos))
    return dm, qdec, kdec


def _ret_fwd(q, k, v, cos, sin, lc, name):
    T = q.shape[0]
    c, dk, dv = RET_CHUNK, RET_DK, RET_DV
    nt, nc = T // c, lc // c
    kscale = dk ** -0.5

    def body(qf, kf, vf, cf, sf_, qb, kb, vb, cb, sb_, of_ref, ob_ref, stf_ref, stb_ref, st_ref):
        @pl.when(pl.program_id(0) == 0)
        def _():
            st_ref[...] = jnp.zeros_like(st_ref)

        sets = ((qf, kf, vf, cf, sf_, of_ref, stf_ref), (qb, kb, vb, cb, sb_, ob_ref, stb_ref))
        for dr, (q_ref, k_ref, v_ref, c_ref, s_ref, o_ref, so_ref) in enumerate(sets):
            rev = dr == 1
            cos_v, sin_v = c_ref[...], s_ref[...]
            for h in range(RET_HEADS):
                lg = _ret_log_gamma(h, rev)
                dm, qdec, kdec = _ret_decays(lg, rev)
                qh = _rope(q_ref[:, h * dk:(h + 1) * dk], cos_v, sin_v)
                kh = _rope(k_ref[:, h * dk:(h + 1) * dk], cos_v, sin_v) * kscale
                vh = _bf(v_ref[:, h * dv:(h + 1) * dv])
                st = st_ref[dr, h]
                stb = _bf(st)
                so_ref[0, h] = stb
                sc = _nt(_bf(qh), _bf(kh)) * dm
                o_ref[:, h * dv:(h + 1) * dv] = _nn(_bf(sc), vh) + _nt(_bf(qh * qdec), stb)
                st_ref[dr, h] = st * math.exp(lg * c) + _tn(vh, _bf(kh * kdec))

    def spec(order, width):
        return pl.BlockSpec((c, width), lambda s: (order(s, nc, nt), 0))

    def st_spec(order):
        return pl.BlockSpec((1, RET_HEADS, dv, dk), lambda s: (order(s, nc, nt), 0, 0, 0))

    in_specs = []
    for order in (_fw_chunk, _bw_chunk):
        in_specs += [spec(order, RET_HEADS * dk), spec(order, RET_HEADS * dk), spec(order, RET_HEADS * dv),
                     spec(order, dk // 2), spec(order, dk // 2)]
    return pl.pallas_call(
        body, name=name, grid=(nt,), in_specs=in_specs,
        out_specs=[spec(_fw_chunk, RET_HEADS * dv), spec(_bw_chunk, RET_HEADS * dv), st_spec(_fw_chunk), st_spec(_bw_chunk)],
        out_shape=[jax.ShapeDtypeStruct((T, RET_HEADS * dv), F32), jax.ShapeDtypeStruct((T, RET_HEADS * dv), F32),
                   jax.ShapeDtypeStruct((nt, RET_HEADS, dv, dk), ACT), jax.ShapeDtypeStruct((nt, RET_HEADS, dv, dk), ACT)],
        scratch_shapes=[pltpu.VMEM((2, RET_HEADS, dv, dk), F32)],
        compiler_params=_cp("arbitrary"),
    )(q, k, v, cos, sin, q, k, v, cos, sin)


def _ret_bwd(q, k, v, cos, sin, s_fw, s_bw, do, lc, name):
    T = q.shape[0]
    c, dk, dv = RET_CHUNK, RET_DK, RET_DV
    nt, nc = T // c, lc // c
    kscale = dk ** -0.5

    def rfw(s, nc_, nt_):
        return _fw_chunk(nt_ - 1 - s, nc_, nt_)

    def rbw(s, nc_, nt_):
        return _bw_chunk(nt_ - 1 - s, nc_, nt_)

    def body(qf, kf, vf, cf, sf_, stf, dof, qb, kb, vb, cb, sb_, stb_, dob_,
             dqf, dkf, dvf, dqb, dkb, dvb, dst_ref):
        @pl.when(pl.program_id(0) == 0)
        def _():
            dst_ref[...] = jnp.zeros_like(dst_ref)

        sets = ((qf, kf, vf, cf, sf_, stf, dof, dqf, dkf, dvf), (qb, kb, vb, cb, sb_, stb_, dob_, dqb, dkb, dvb))
        for dr, (q_ref, k_ref, v_ref, c_ref, s_ref, st_in, do_ref, dq_ref, dk_ref, dv_ref) in enumerate(sets):
            rev = dr == 1
            cos_v, sin_v = c_ref[...], s_ref[...]
            for h in range(RET_HEADS):
                lg = _ret_log_gamma(h, rev)
                dm, qdec, kdec = _ret_decays(lg, rev)
                qh = _rope(q_ref[:, h * dk:(h + 1) * dk], cos_v, sin_v)
                kh = _rope(k_ref[:, h * dk:(h + 1) * dk], cos_v, sin_v) * kscale
                vh = _bf(v_ref[:, h * dv:(h + 1) * dv])
                qb16, kb16 = _bf(qh), _bf(kh)
                qinb, kinb = _bf(qh * qdec), _bf(kh * kdec)
                dob = _bf(do_ref[:, h * dv:(h + 1) * dv])
                stb = st_in[0, h]
                dst = dst_ref[dr, h]
                dstb = _bf(dst)
                sc = _bf(_nt(qb16, kb16) * dm)
                dsc = _bf(_nt(dob, vh) * dm)
                dq_r = _nn(dsc, kb16) + _nn(dob, stb) * qdec
                dk_r = _tn(dsc, qb16) + _nn(vh, dstb) * kdec
                dv_ref[:, h * dv:(h + 1) * dv] = _tn(sc, dob) + _nt(kinb, dstb)
                dst_ref[dr, h] = _tn(dob, qinb) + dst * math.exp(lg * c)
                dq_ref[:, h * dk:(h + 1) * dk] = _unrope(dq_r, cos_v, sin_v)
                dk_ref[:, h * dk:(h + 1) * dk] = _unrope(dk_r * kscale, cos_v, sin_v)

    def spec(order, width):
        return pl.BlockSpec((c, width), lambda s: (order(s, nc, nt), 0))

    def st_spec(order):
        return pl.BlockSpec((1, RET_HEADS, dv, dk), lambda s: (order(s, nc, nt), 0, 0, 0))

    in_specs = []
    for order in (rfw, rbw):
        in_specs += [spec(order, RET_HEADS * dk), spec(order, RET_HEADS * dk), spec(order, RET_HEADS * dv),
                     spec(order, dk // 2), spec(order, dk // 2), st_spec(order), spec(order, RET_HEADS * dv)]
    out_specs, out_shape = [], []
    for order in (rfw, rbw):
        out_specs += [spec(order, RET_HEADS * dk), spec(order, RET_HEADS * dk), spec(order, RET_HEADS * dv)]
        out_shape += [jax.ShapeDtypeStruct((T, RET_HEADS * dk), F32), jax.ShapeDtypeStruct((T, RET_HEADS * dk), F32),
                      jax.ShapeDtypeStruct((T, RET_HEADS * dv), F32)]
    return pl.pallas_call(
        body, name=name, grid=(nt,), in_specs=in_specs, out_specs=out_specs, out_shape=out_shape,
        scratch_shapes=[pltpu.VMEM((2, RET_HEADS, dv, dk), F32)],
        compiler_params=_cp("arbitrary"),
    )(q, k, v, cos, sin, s_fw, do, q, k, v, cos, sin, s_bw, do)


def _attn_rope_tables(lc, l):
    t = jnp.arange(l)
    row = (t // GRID_W).astype(F32)
    colp = (t % GRID_W).astype(F32)
    n_freq = HEAD_DIM // 4
    inv = 10000.0 ** (-jnp.arange(n_freq, dtype=F32) / n_freq)
    ang = jnp.concatenate([row[:, None] * inv, colp[:, None] * inv], axis=-1)
    cos = jnp.concatenate([jnp.ones((lc, HEAD_DIM // 2), F32), jnp.cos(ang)], axis=0)
    sin = jnp.concatenate([jnp.zeros((lc, HEAD_DIM // 2), F32), jnp.sin(ang)], axis=0)
    return jnp.concatenate([cos, cos], axis=1), jnp.concatenate([-sin, sin], axis=1)


def _ret_rope_tables(lc, l):
    theta = 1.0 / (10000.0 ** jnp.linspace(0.0, 1.0, RET_DK // 2, dtype=F32))
    ang = jnp.arange(l, dtype=F32)[:, None] * theta
    cos = jnp.concatenate([jnp.ones((lc, RET_DK // 2), F32), jnp.cos(ang)], axis=0)
    sin = jnp.concatenate([jnp.zeros((lc, RET_DK // 2), F32), jnp.sin(ang)], axis=0)
    return cos, sin


def _heads_major(slab, n_heads):
    t = slab.shape[0]
    return slab.reshape(t, n_heads, HEAD_DIM).transpose(1, 0, 2)


def _slab(hm):
    nh, t, hd = hm.shape
    return hm.transpose(1, 0, 2).reshape(t, nh * hd)


def _local_step(xs, target, mv, norm_g, w, qk_g, sink, hg_out_g, lbraw, lc):
    T, dm = xs.shape
    l = T - lc
    tm = lc
    blk = ATTN_BLOCK
    d2, d3 = 2 * dm, 3 * dm

    def ms(layer, a, b):
        return mv[layer, :, :, a:b]

    g00, g01, g10, g11 = (norm_g[i, j][None, :] for i in (0, 1) for j in (0, 1))

    pa, pb = _pre_fwd(xs, g00, ms(0, 0, d2), w["even_in"], ((0, 768), (768, 3328)), tm, "pre0_fwd")
    raw10 = jnp.concatenate([_heads_major(pa[:, :512], 8), _heads_major(pa[:, 512:640], 2)], axis=0)
    v_hm = _heads_major(pa[:, 640:768], 2)
    cos2, sin2 = _attn_rope_tables(lc, l)
    gains10 = jnp.concatenate([jnp.broadcast_to(qk_g[0], (8, HEAD_DIM)), jnp.broadcast_to(qk_g[1], (2, HEAD_DIM))])[:, None, :]
    qk_t = _qk_prep_fwd(raw10, gains10, cos2, sin2, tm, "qk_prep_fwd")
    pad = ((0, 0), (blk, blk), (0, 0))
    qt = qk_t[:8]
    kp = jnp.pad(qk_t[8:], pad)
    vp = jnp.pad(_bf(v_hm), pad)
    sinkb = jnp.broadcast_to(sink.reshape(ATTN_KV, 4, 1, 1), (ATTN_KV, 4, blk, 1)).reshape(ATTN_KV, 4 * blk, 1)
    o_att, lse = _attn_fwd(qt, kp, vp, sinkb, lc, "attn_fwd")
    a_slab = _slab(o_att)
    hg_of, hg_ob, hg_sf, hg_sb = _gla_fwd(pb, lbraw, lc, "hgrn_fwd")
    x01, z0 = _post_fwd(xs, hg_of, hg_ob, pb, 4, hg_out_g, a_slab, w["even_out"], ms(0, d2, d3), HG_D, tm, "post0_fwd")
    x02, u0, f0 = _ffn_fwd(x01, g01, ms(0, d3, 6 * dm), w["ffn_in0"], w["ffn_out0"], tm, "ffn0_fwd")

    rq, rk, rv, rg = _pre_fwd(x02, g10, ms(1, 0, d2), w["odd_in"],
                              ((0, 1024), (1024, 2048), (2048, 4096), (4096, 6144)), tm, "pre1_fwd")
    rcos, rsin = _ret_rope_tables(lc, l)
    rt_of, rt_ob, rt_sf, rt_sb = _ret_fwd(rq, rk, rv, rcos, rsin, lc, "ret_fwd")
    x11, z1 = _post_fwd(x02, rt_of, rt_ob, rg, 0, None, None, w["odd_out"], ms(1, d2, d3), RET_DV, tm, "post1_fwd")
    x12, u1, f1 = _ffn_fwd(x11, g11, ms(1, d3, 6 * dm), w["ffn_in1"], w["ffn_out1"], tm, "ffn1_fwd")

    dx, loss = _loss_bwd(x12, target, tm, "loss_bwd")

    dx, h, du, act, df, dms_f1, dg11 = _ffn_bwd(x11, dx, u1, f1, g11, ms(1, d3, 6 * dm), w["ffn_in1"], w["ffn_out1"], tm, "ffn1_bwd")
    gw = {"ffn_in1": _wgrad(h, du, "wg_ffn_in1"), "ffn_out1": _wgrad(act, df, "wg_ffn_out1")}
    do1, dgr1, dy1, dgate_p1, _ = _post_bwd(dx, z1, rt_of, rt_ob, rg, 0, None, w["odd_out"], ms(1, d2, d3), 0, RET_DV, tm, "post1_bwd")
    gw["odd_out"] = _wgrad(z1, dy1, "wg_odd_out")
    dqf, dkf, dvf, dqb, dkb, dvb = _ret_bwd(rq, rk, rv, rcos, rsin, rt_sf, rt_sb, do1, lc, "ret_bwd")
    dx, h, dp, dms_p1, dg10 = _pre_bwd(x02, dx, g10, ms(1, 0, d2), w["odd_in"],
                                       [(0, [dqf, dqb]), (1024, [dkf, dkb]), (2048, [dvf, dvb]), (4096, [dgr1])], tm, "pre1_bwd")
    gw["odd_in"] = _wgrad(h, dp, "wg_odd_in")

    dx, h, du, act, df, dms_f0, dg01 = _ffn_bwd(x01, dx, u0, f0, g01, ms(0, d3, 6 * dm), w["ffn_in0"], w["ffn_out0"], tm, "ffn0_bwd")
    gw["ffn_in0"] = _wgrad(h, du, "wg_ffn_in0")
    gw["ffn_out0"] = _wgrad(act, df, "wg_ffn_out0")
    do0, dgr0, da0, dy0, dgate_p0, d_hg_gain = _post_bwd(dx, z0, hg_of, hg_ob, pb, 4, hg_out_g, w["even_out"], ms(0, d2, d3),
                                                        512, HG_D, tm, "post0_bwd")
    gw["even_out"] = _wgrad(z0, dy0, "wg_even_out")
    hq_f, hz_f, hv_f, hq_b, hz_b, hv_b, dlb = _gla_bwd(pb, lbraw, hg_sf, hg_sb, do0, lc, "hgrn_bwd")
    dq_att, dkp, dvp, dsink = _attn_bwd(qt, kp, vp, sinkb, o_att, lse, _heads_major(da0, 8), lc, "attn_bwd")
    dy10 = jnp.concatenate([dq_att, dkp[:, blk:blk + T]], axis=0)
    draw10, dgain10 = _qk_prep_bwd(dy10, raw10, gains10, cos2, sin2, tm, "qk_prep_bwd")
    pieces0 = [(0, [_slab(draw10[:8])]), (512, [_slab(draw10[8:])]), (640, [_slab(dvp[:, blk:blk + T])]),
               (768, [hq_f, hq_b]), (1280, [hz_f]), (1792, [hz_b]), (2304, [hv_f, hv_b]), (2816, [dgr0])]
    dx, h, dp, dms_p0, dg00 = _pre_bwd(xs, dx, g00, ms(0, 0, d2), w["even_in"], pieces0, tm, "pre0_bwd")
    gw["even_in"] = _wgrad(h, dp, "wg_even_in")

    dmv = jnp.stack([jnp.concatenate([dms_p0, dgate_p0, dms_f0], axis=2), jnp.concatenate([dms_p1, dgate_p1, dms_f1], axis=2)])
    small = {
        "dmv": dmv,
        "norm_g": jnp.stack([jnp.stack([dg00[0], dg01[0]]), jnp.stack([dg10[0], dg11[0]])]),
        "qk_g": jnp.stack([jnp.sum(dgain10[:8, 0], axis=0), jnp.sum(dgain10[8:, 0], axis=0)]),
        "sink": dsink.reshape(ATTN_HEADS),
        "hg_out_g": d_hg_gain[0],
        "lb": dlb[0],
        "loss": loss[0, 0],
    }
    return loss, dx, gw, small


HBM_SPEC = pl.BlockSpec(memory_space=pltpu.HBM)


def _my_index():
    return 4 * lax.axis_index("x") + 2 * lax.axis_index("y") + lax.axis_index("c")


def _peer(k):
    pos = []
    for axis, bit in (("x", 4), ("y", 2), ("c", 1)):
        a = lax.axis_index(axis)
        pos.append(1 - a if k & bit else a)
    return tuple(pos)


def _peer_index(k):
    px, py, pc = _peer(k)
    return 4 * px + 2 * py + pc


def _exchange_body(src_of, v_ref, out_ref, send_sems, recv_sems, local_sem):
    me = _my_index()
    mine = pltpu.make_async_copy(src_of(me), out_ref.at[me], local_sem)
    mine.start()
    for k in range(1, N_DEV):
        pltpu.make_async_remote_copy(src_of(_peer_index(k)), out_ref.at[me], send_sems.at[k - 1], recv_sems.at[k - 1],
                                     device_id=_peer(k), device_id_type=MESH).start()
    for k in range(1, N_DEV):
        landed = pltpu.make_async_remote_copy(src_of(_peer_index(k)), out_ref.at[_peer_index(k)], send_sems.at[k - 1],
                                              recv_sems.at[k - 1], device_id=_peer(k), device_id_type=MESH)
        landed.wait_send()
        landed.wait_recv()
    mine.wait()


def _exchange_call(body, v, out_struct, name):
    return pl.pallas_call(
        body, name=name, in_specs=[HBM_SPEC], out_specs=HBM_SPEC, out_shape=out_struct,
        scratch_shapes=[pltpu.SemaphoreType.DMA((N_DEV - 1,)), pltpu.SemaphoreType.DMA((N_DEV - 1,)),
                        pltpu.SemaphoreType.DMA(())],
    )(v)


def _all_gather(v, name):
    def body(v_ref, out_ref, send_sems, recv_sems, local_sem):
        _exchange_body(lambda j: v_ref, v_ref, out_ref, send_sems, recv_sems, local_sem)

    return _exchange_call(body, v, jax.ShapeDtypeStruct((N_DEV,) + v.shape, v.dtype), name)


def _all_to_all(v, name):
    def body(v_ref, out_ref, send_sems, recv_sems, local_sem):
        _exchange_body(lambda j: v_ref.at[j], v_ref, out_ref, send_sems, recv_sems, local_sem)

    return _exchange_call(body, v, jax.ShapeDtypeStruct(v.shape, v.dtype), name)


def _mod_fwd(call, mod_w, bias, name):
    nl, dm, n = mod_w.shape

    def body(c_ref, w_ref, b_ref, o_ref):
        cv = c_ref[...]
        cond = _bf(cv * _sig(cv))
        for layer in range(nl):
            o_ref[layer] = _nn(cond, _bf(w_ref[layer])) + b_ref[layer]

    return pl.pallas_call(
        body, name=name, out_shape=jax.ShapeDtypeStruct((nl, call.shape[0], n), F32),
        compiler_params=pltpu.CompilerParams(vmem_limit_bytes=VMEM_LIMIT),
    )(call, mod_w, bias)


def _mod_bwd(call, dm_all, mod_w, name):
    nl, dm, n = mod_w.shape

    def body(c_ref, d_ref, w_ref, gw_ref, dc_ref):
        cv = c_ref[...]
        cond = _bf(cv * _sig(cv))
        dc = jnp.zeros(cv.shape, F32)
        for layer in range(nl):
            db = _bf(d_ref[layer])
            gw_ref[layer] = _tn(cond, db)
            dc = dc + _nt(db, _bf(w_ref[layer]))
        dc_ref[...] = dc

    return pl.pallas_call(
        body, name=name,
        out_shape=[jax.ShapeDtypeStruct(mod_w.shape, F32), jax.ShapeDtypeStruct(call.shape, F32)],
        compiler_params=pltpu.CompilerParams(vmem_limit_bytes=VMEM_LIMIT),
    )(call, dm_all, mod_w)


def _sum_parts(g, name):
    def body(g_ref, o_ref):
        acc = g_ref[0]
        for j in range(1, g.shape[0]):
            acc = acc + g_ref[j]
        o_ref[...] = acc

    return pl.pallas_call(body, name=name, out_shape=jax.ShapeDtypeStruct(g.shape[1:], g.dtype))(g)


def _small_finish(dcond_g, c_ctx, dlb, lbraw, dm_ctx, dm_lat, name):
    def body(dc_ref, c_ref, dlb_ref, lb_ref, mc_ref, ml_ref, gc_ref, glb_ref, gb_ref):
        acc = dc_ref[0, 0:1, :]
        for j in range(1, N_DEV):
            acc = acc + dc_ref[j, 0:1, :]
        cv = c_ref[...]
        s = _sig(cv)
        gc_ref[...] = acc * (s * (1.0 + cv * (1.0 - s)))
        lb = _lower_bound(lb_ref)
        d0 = dlb_ref[...] * lb * (1.0 - lb)
        glb_ref[0:1, :] = d0
        glb_ref[1:2, :] = -d0
        gb_ref[...] = mc_ref[...] + ml_ref[...]

    return pl.pallas_call(
        body, name=name,
        out_shape=[jax.ShapeDtypeStruct(c_ctx.shape, F32), jax.ShapeDtypeStruct(lbraw.shape, F32),
                   jax.ShapeDtypeStruct(dm_ctx.shape, F32)],
    )(dcond_g, c_ctx, dlb, lbraw, dm_ctx, dm_lat)


def _row_tile(r, cap, mult):
    best = r
    for t in range(mult, min(r, cap) + 1, mult):
        if r % t == 0:
            best = t
    return best


def _adam(g_parts, w, m, v, cap, name):
    p, r, cdim = g_parts.shape
    tr = _row_tile(r, cap, 16)

    def body(g_ref, w_ref, m_ref, v_ref, go_ref, d_ref, mo_ref, vo_ref):
        g = g_ref[0].astype(F32)
        for j in range(1, p):
            g = g + g_ref[j].astype(F32)
        m2 = ADAM_B1 * m_ref[...] + (1.0 - ADAM_B1) * g
        v2 = ADAM_B2 * v_ref[...] + (1.0 - ADAM_B2) * (g * g)
        m_hat = m2 / (1.0 - ADAM_B1 ** ADAM_STEP)
        v_hat = v2 / (1.0 - ADAM_B2 ** ADAM_STEP)
        go_ref[...] = g
        d_ref[...] = -ADAM_LR * (m_hat / (jnp.sqrt(v_hat) + ADAM_EPS) + ADAM_WD * w_ref[...])
        mo_ref[...] = m2
        vo_ref[...] = v2

    spec = pl.BlockSpec((tr, cdim), lambda i: (i, 0))
    return pl.pallas_call(
        body, name=name, grid=(r // tr,),
        in_specs=[pl.BlockSpec((p, tr, cdim), lambda i: (0, i, 0)), spec, spec, spec],
        out_specs=[spec] * 4, out_shape=[jax.ShapeDtypeStruct((r, cdim), F32)] * 4,
        compiler_params=_cp("parallel"),
    )(g_parts, w, m, v)


def _f32_as_rows(a, width):
    return lax.bitcast_convert_type(a.reshape(-1), BF16).reshape(-1, width)


def _rows_as_f32(rows):
    return lax.bitcast_convert_type(rows.reshape(rows.shape[:-2] + (-1, 2)), F32)


def _pad_rows(a, mult):
    r = (-a.shape[-2]) % mult
    if r == 0:
        return a
    widths = [(0, 0)] * (a.ndim - 2) + [(0, r), (0, 0)]
    return jnp.pad(a, widths)


def _pack_flat(parts, lane):
    flat = jnp.concatenate([p.reshape(-1).astype(F32) for p in parts])
    n = flat.shape[0]
    rows = -(-n // lane)
    rows += (-rows) % 8
    return jnp.pad(flat, (0, rows * lane - n)).reshape(rows, lane)


def _unpack_flat(packed, shapes):
    flat = packed.reshape(-1)
    out, off = [], 0
    for s in shapes:
        n = math.prod(s)
        out.append(flat[off:off + n].reshape(s))
        off += n
    return out


def kernel(x, c, ctx, c_ctx, mod_w, mod_b, norm_g, ffn_w_in, ffn_w_out, even_w_in, even_w_out, attn_qk_norm_g, attn_sink, hgrn_out_norm_g, hgrn_lb, odd_w_in, odd_w_out, loss_target, m_c_ctx, m_mod_w, m_mod_b, m_norm_g, m_ffn_w_in, m_ffn_w_out, m_even_w_in, m_even_w_out, m_attn_qk_norm_g, m_attn_sink, m_hgrn_out_norm_g, m_hgrn_lb, m_odd_w_in, m_odd_w_out, v_c_ctx, v_mod_w, v_mod_b, v_norm_g, v_ffn_w_in, v_ffn_w_out, v_even_w_in, v_even_w_out, v_attn_qk_norm_g, v_attn_sink, v_hgrn_out_norm_g, v_hgrn_lb, v_odd_w_in, v_odd_w_out):
    me = _my_index()
    lc, dm = ctx.shape[1], x.shape[2]
    nmod = mod_w.shape[2]
    big = (ffn_w_in, ffn_w_out, even_w_in, even_w_out, odd_w_in, odd_w_out)
    big_rows = [math.prod(a.shape) // dm for a in big]
    n_big = sum(big_rows)

    extra = _pad_rows(jnp.concatenate([_f32_as_rows(c, dm), _f32_as_rows(norm_g, dm)], axis=0), 16)
    pack = jnp.concatenate([a.reshape(-1, dm).astype(BF16) for a in big] + [extra], axis=0)
    gathered = _all_gather(pack, "gather_weights")
    offs = [0]
    for r in big_rows:
        offs.append(offs[-1] + r)
    sh = [gathered[:, offs[i]:offs[i + 1]].reshape((N_DEV,) + big[i].shape) for i in range(6)]

    def cols(s):
        return s.transpose(1, 2, 0, 3).reshape(s.shape[1], s.shape[2], -1)

    def rows(s):
        return s.transpose(1, 0, 2, 3).reshape(s.shape[1], -1, s.shape[3])

    ffn_in_f, ffn_out_f = cols(sh[0]), rows(sh[1])
    w = {"ffn_in0": ffn_in_f[0], "ffn_in1": ffn_in_f[1], "ffn_out0": ffn_out_f[0], "ffn_out1": ffn_out_f[1],
         "even_in": cols(sh[2])[0], "even_out": rows(sh[3])[0], "odd_in": cols(sh[4])[0], "odd_out": rows(sh[5])[0]}
    c_all = _rows_as_f32(gathered[:, n_big:n_big + 2])
    norm_g_all = _rows_as_f32(gathered[:, n_big + 2:n_big + 3]).reshape(N_DEV, 2, 2, -1)
    norm_g_full = norm_g_all.transpose(1, 2, 0, 3).reshape(2, 2, dm)

    call = jnp.concatenate([c_all, c_ctx[None, :], jnp.zeros((16 - N_DEV - 1, dm), F32)], axis=0)
    bias = lax.dynamic_slice_in_dim(mod_b, me * nmod, nmod, axis=1)[:, None, :]
    m_sh = _mod_fwd(call, mod_w, bias, "mod_fwd")
    m_g = _all_gather(m_sh.reshape(-1, nmod), "gather_mod").reshape(N_DEV, 2, 16, nmod)
    m_all = m_g.transpose(1, 2, 0, 3).reshape(2, 16, -1)
    m_lat = lax.dynamic_index_in_dim(m_all, me, axis=1, keepdims=False)
    mv = jnp.stack([m_all[:, N_DEV], m_lat], axis=1)[:, :, None, :]

    xs = jnp.concatenate([ctx[0], x[0]], axis=0)
    _, dxs, gw, small = _local_step(xs, loss_target[0], mv, norm_g_full, w, attn_qk_norm_g[0], attn_sink[0],
                                    hgrn_out_norm_g, hgrn_lb, lc)
    grad_x = dxs[lc:][None]

    def to_cols(g, n):
        k = g.shape[1]
        return g.reshape(n, k, N_DEV, -1).transpose(2, 0, 1, 3).reshape(N_DEV, -1, dm)

    def to_rows(g, n):
        return g.reshape(n, N_DEV, -1, dm).transpose(1, 0, 2, 3).reshape(N_DEV, -1, dm)

    send = jnp.concatenate([
        to_cols(jnp.stack([gw["ffn_in0"], gw["ffn_in1"]]), 2), to_rows(jnp.stack([gw["ffn_out0"], gw["ffn_out1"]]), 2),
        to_cols(gw["even_in"][None], 1), to_rows(gw["even_out"][None], 1),
        to_cols(gw["odd_in"][None], 1), to_rows(gw["odd_out"][None], 1)], axis=1)
    recv = _all_to_all(send, "scatter_grads")
    big_m = (m_ffn_w_in, m_ffn_w_out, m_even_w_in, m_even_w_out, m_odd_w_in, m_odd_w_out)
    big_v = (v_ffn_w_in, v_ffn_w_out, v_even_w_in, v_even_w_out, v_odd_w_in, v_odd_w_out)

    def pack_f32(arrs):
        return jnp.concatenate([a.reshape(-1, dm) for a in arrs], axis=0)

    big_out = _adam(recv, pack_f32(big), pack_f32(big_m), pack_f32(big_v), 192, "adam_big")
    big_res = [[o[offs[i]:offs[i + 1]].reshape(big[i].shape) for i in range(6)] for o in big_out]

    dmv = small["dmv"]
    small_shapes = [(2, 6 * dm), (2, 6 * dm), (2, 2, dm), (2, HEAD_DIM), (ATTN_HEADS,), (HG_D,), (HG_HEADS * HG_D,), (1,)]
    vec = _pack_flat([dmv[:, 0, 0], dmv[:, 1, 0], small["norm_g"], small["qk_g"], small["sink"], small["hg_out_g"],
                      small["lb"], small["loss"]], 128)
    vec_g = _all_gather(vec, "gather_small")
    tot = _unpack_flat(_sum_parts(vec_g, "sum_small"), small_shapes)
    dm_ctx_tot, dm_lat_tot, g_norm_full, g_qk, g_sink, g_hg, dlb_tot, loss_tot = tot
    dm_lat_each = vec_g.reshape(N_DEV, -1)[:, 12 * dm:24 * dm].reshape(N_DEV, 2, 6 * dm)
    dm_lat_mine = lax.dynamic_slice_in_dim(dm_lat_each, me * nmod, nmod, axis=2).transpose(1, 0, 2)
    dm_ctx_mine = lax.dynamic_slice_in_dim(dm_ctx_tot, me * nmod, nmod, axis=1)[:, None, :]
    dm_all = jnp.concatenate([dm_lat_mine, dm_ctx_mine, jnp.zeros((2, 16 - N_DEV - 1, nmod), F32)], axis=1)
    g_mod_w, dcond = _mod_bwd(call, dm_all, mod_w, "mod_bwd")
    dcond_g = _all_gather(dcond[N_DEV:], "gather_dcond")
    g_c_ctx, g_lb, g_mod_b = _small_finish(dcond_g, c_ctx[None, :], dlb_tot[None, :], hgrn_lb, dm_ctx_tot, dm_lat_tot,
                                           "small_finish")
    g_norm = lax.dynamic_slice_in_dim(g_norm_full, me * norm_g.shape[2], norm_g.shape[2], axis=2)

    mod_out = _adam(g_mod_w.reshape(1, -1, nmod), mod_w.reshape(-1, nmod), m_mod_w.reshape(-1, nmod),
                    v_mod_w.reshape(-1, nmod), 256, "adam_mod_w")
    mod_res = [o.reshape(mod_w.shape) for o in mod_out]

    sm_w = (c_ctx, mod_b, norm_g, attn_qk_norm_g, attn_sink, hgrn_out_norm_g, hgrn_lb)
    sm_m = (m_c_ctx, m_mod_b, m_norm_g, m_attn_qk_norm_g, m_attn_sink, m_hgrn_out_norm_g, m_hgrn_lb)
    sm_v = (v_c_ctx, v_mod_b, v_norm_g, v_attn_qk_norm_g, v_attn_sink, v_hgrn_out_norm_g, v_hgrn_lb)
    sm_g = (g_c_ctx, g_mod_b, g_norm, g_qk, g_sink, g_hg, g_lb)
    sm_shapes = [a.shape for a in sm_w]
    sm_out = _adam(_pack_flat(sm_g, 128)[None], _pack_flat(sm_w, 128), _pack_flat(sm_m, 128),
                   _pack_flat(sm_v, 128) + 0.0, 1024, "adam_small")
    sm_res = [_unpack_flat(o, sm_shapes) for o in sm_out]

    def ordered(k):
        s, b = sm_res[k], big_res[k]
        return [s[0], mod_res[k], s[1], s[2], b[0], b[1], b[2], b[3], s[3], s[4], s[5], s[6], b[4], b[5]]

    return (loss_tot[0], grad_x, *ordered(0), *ordered(1), *ordered(2), *ordered(3))
```

```python
import functools
import math

import jax
import jax.numpy as jnp
from jax import lax
from jax.experimental import pallas as pl
from jax.experimental.pallas import tpu as pltpu

F32 = jnp.float32
BF16 = jnp.bfloat16
EPS = 1e-6
N_DEV = 8
MESH = pl.DeviceIdType.MESH

HEAD_DIM = 64
ATTN_HEADS = 8
ATTN_KV = 2
ATTN_BLOCK = 128
WINDOW = 128
GRID_W = 64
HG_HEADS = 4
HG_D = 128
HG_CHUNK = 64
RET_HEADS = 4
RET_DK = 256
RET_DV = 512
RET_CHUNK = 64
NEG = -1e30

ADAM_LR = 0.001
ADAM_B1 = 0.9
ADAM_B2 = 0.999
ADAM_EPS = 1e-08
ADAM_WD = 0.01
ADAM_STEP = 10

VMEM_LIMIT = 60 * 1024 * 1024


def _cp(*sem):
    return pltpu.CompilerParams(dimension_semantics=sem, vmem_limit_bytes=VMEM_LIMIT)


def _nn(a, b):
    return jnp.dot(a, b, preferred_element_type=F32)


def _nt(a, b):
    return lax.dot_general(a, b, (((1,), (1,)), ((), ())), preferred_element_type=F32)


def _tn(a, b):
    return lax.dot_general(a, b, (((0,), (0,)), ((), ())), preferred_element_type=F32)


ACT = BF16


def _bf(a):
    return a.astype(ACT)


def _sig(x):
    return jax.nn.sigmoid(x)


def _split3(x):
    h = x.astype(BF16)
    r = x - h.astype(F32)
    m = r.astype(BF16)
    lo = (r - m.astype(F32)).astype(BF16)
    return h, m, lo


def _nn3(m01, x):
    h, m, lo = _split3(x)
    return _nn(m01, h) + _nn(m01, m) + _nn(m01, lo)


def _nn3r(x, m01):
    h, m, lo = _split3(x)
    return _nn(h, m01) + _nn(m, m01) + _nn(lo, m01)


def _full(shape):
    nd = len(shape)
    return pl.BlockSpec(shape, lambda *a: (0,) * nd, pipeline_mode=pl.Buffered(1))


def _whole(shape):
    nd = len(shape)
    return pl.BlockSpec(shape, lambda *a: (0,) * nd)


def _rows(tm, width):
    return pl.BlockSpec((tm, width), lambda i: (i, 0))


def _ctx_lat(width):
    return pl.BlockSpec((1, 1, width), lambda i: (jnp.minimum(i, 1), 0, 0))


def _acc_ctx_lat(ref, i, val):
    @pl.when(i <= 1)
    def _():
        ref[...] = val.reshape(ref.shape)

    @pl.when(i > 1)
    def _():
        ref[...] += val.reshape(ref.shape)


def _acc_all(ref, i, val):
    @pl.when(i == 0)
    def _():
        ref[...] = val.reshape(ref.shape)

    @pl.when(i > 0)
    def _():
        ref[...] += val.reshape(ref.shape)


def _tile(n, cap):
    best = None
    for t in range(128, min(n, cap) + 1, 128):
        if n % t == 0:
            best = t
    return n if best is None else best


def _norm_mod(xv, g, shift, scale):
    r = lax.rsqrt(jnp.mean(xv * xv, axis=-1, keepdims=True) + EPS)
    xhat = xv * r
    n = xhat * g
    return r, xhat, n, n * (1.0 + scale) + shift


def _norm_mod_bwd(dh, r, xhat, n, g, scale):
    dshift = jnp.sum(dh, axis=0, keepdims=True)
    dscale = jnp.sum(dh * n, axis=0, keepdims=True)
    dn = dh * (1.0 + scale)
    dg = jnp.sum(dn * xhat, axis=0, keepdims=True)
    dxh = dn * g
    dx = r * (dxh - xhat * jnp.mean(dxh * xhat, axis=-1, keepdims=True))
    return dx, dshift, dscale, dg


def _pre_fwd(x, gain, ms, w, splits, tm, name):
    T, dm = x.shape

    def body(x_ref, g_ref, ms_ref, w_ref, *outs):
        ms_v = ms_ref[0]
        h = _norm_mod(x_ref[...], g_ref[...], ms_v[:, :dm], ms_v[:, dm:])[3]
        hb = _bf(h)
        for (s, e), o_ref in zip(splits, outs):
            o_ref[...] = _nn(hb, w_ref[:, s:e])

    return pl.pallas_call(
        body, name=name, grid=(T // tm,),
        in_specs=[_rows(tm, dm), _full((1, dm)), _ctx_lat(2 * dm), _full(w.shape)],
        out_specs=[_rows(tm, e - s) for s, e in splits],
        out_shape=[jax.ShapeDtypeStruct((T, e - s), F32) for s, e in splits],
        compiler_params=_cp("arbitrary"),
    )(x, gain, ms, w)


def _pre_bwd(x, dx_in, gain, ms, w, pieces, tm, name):
    T, dm = x.shape
    n_out = w.shape[1]
    flat = [a for _, arrs in pieces for a in arrs]

    def body(x_ref, dxin_ref, g_ref, ms_ref, w_ref, *rest):
        p_refs = rest[:len(flat)]
        dx_ref, h_ref, dp_ref, dms_ref, dg_ref = rest[len(flat):]
        i = pl.program_id(0)
        ms_v = ms_ref[0]
        g = g_ref[...]
        scale = ms_v[:, dm:]
        r, xhat, n, h = _norm_mod(x_ref[...], g, ms_v[:, :dm], scale)
        h_ref[...] = _bf(h)
        dh = jnp.zeros((tm, dm), F32)
        k = 0
        for s, arrs in pieces:
            v = p_refs[k][...].astype(F32)
            for j in range(1, len(arrs)):
                v = v + p_refs[k + j][...].astype(F32)
            k += len(arrs)
            vb = _bf(v)
            wd = vb.shape[1]
            dp_ref[:, s:s + wd] = vb
            dh = dh + _nt(vb, w_ref[:, s:s + wd])
        dx, dshift, dscale, dg = _norm_mod_bwd(dh, r, xhat, n, g, scale)
        dx_ref[...] = dxin_ref[...] + dx
        _acc_ctx_lat(dms_ref, i, jnp.concatenate([dshift, dscale], axis=1))
        _acc_all(dg_ref, i, dg)

    return pl.pallas_call(
        body, name=name, grid=(T // tm,),
        in_specs=[_rows(tm, dm), _rows(tm, dm), _full((1, dm)), _ctx_lat(2 * dm), _full(w.shape)]
        + [_rows(tm, a.shape[1]) for a in flat],
        out_specs=[_rows(tm, dm), _rows(tm, dm), _rows(tm, n_out), _ctx_lat(2 * dm), _whole((1, dm))],
        out_shape=[jax.ShapeDtypeStruct((T, dm), F32), jax.ShapeDtypeStruct((T, dm), ACT),
                   jax.ShapeDtypeStruct((T, n_out), ACT), jax.ShapeDtypeStruct((2, 1, 2 * dm), F32),
                   jax.ShapeDtypeStruct((1, dm), F32)],
        compiler_params=_cp("arbitrary"),
    )(x, dx_in, gain, ms, w, *flat)


def _ffn_fwd(x1, gain, ms, w_in, w_out, tm, name):
    T, dm = x1.shape
    fh = w_out.shape[0]

    def body(x_ref, g_ref, ms_ref, wi_ref, wo_ref, x2_ref, u_ref, f_ref):
        ms_v = ms_ref[0]
        xv = x_ref[...]
        h = _norm_mod(xv, g_ref[...], ms_v[:, :dm], ms_v[:, dm:2 * dm])[3]
        u = _nn(_bf(h), wi_ref[...])
        u_ref[...] = _bf(u)
        gt = u[:, :fh]
        act = gt * _sig(gt) * u[:, fh:]
        f = _nn(_bf(act), wo_ref[...])
        f_ref[...] = _bf(f)
        x2_ref[...] = xv + ms_v[:, 2 * dm:] * f

    return pl.pallas_call(
        body, name=name, grid=(T // tm,),
        in_specs=[_rows(tm, dm), _full((1, dm)), _ctx_lat(3 * dm), _full(w_in.shape), _full(w_out.shape)],
        out_specs=[_rows(tm, dm), _rows(tm, 2 * fh), _rows(tm, dm)],
        out_shape=[jax.ShapeDtypeStruct((T, dm), F32), jax.ShapeDtypeStruct((T, 2 * fh), ACT),
                   jax.ShapeDtypeStruct((T, dm), ACT)],
        compiler_params=_cp("arbitrary"),
    )(x1, gain, ms, w_in, w_out)


def _ffn_bwd(x1, dx2, u, f, gain, ms, w_in, w_out, tm, name):
    T, dm = x1.shape
    fh = w_out.shape[0]

    def body(x_ref, dx2_ref, u_ref, f_ref, g_ref, ms_ref, wi_ref, wo_ref,
             dx1_ref, h_ref, du_ref, act_ref, df_ref, dms_ref, dg_ref):
        i = pl.program_id(0)
        ms_v = ms_ref[0]
        g = g_ref[...]
        scale = ms_v[:, dm:2 * dm]
        gate = ms_v[:, 2 * dm:]
        r, xhat, n, h = _norm_mod(x_ref[...], g, ms_v[:, :dm], scale)
        h_ref[...] = _bf(h)
        dx2 = dx2_ref[...]
        dgate = jnp.sum(dx2 * f_ref[...].astype(F32), axis=0, keepdims=True)
        dfb = _bf(dx2 * gate)
        df_ref[...] = dfb
        da = _nt(dfb, wo_ref[...])
        uv = u_ref[...].astype(F32)
        gt = uv[:, :fh]
        up = uv[:, fh:]
        s = _sig(gt)
        sg = gt * s
        act_ref[...] = _bf(sg * up)
        dgt = _bf(da * up * (s * (1.0 + gt * (1.0 - s))))
        dup = _bf(da * sg)
        du_ref[:, :fh] = dgt
        du_ref[:, fh:] = dup
        dh = _nt(dgt, wi_ref[:, :fh]) + _nt(dup, wi_ref[:, fh:])
        dx, dshift, dscale, dg = _norm_mod_bwd(dh, r, xhat, n, g, scale)
        dx1_ref[...] = dx2 + dx
        _acc_ctx_lat(dms_ref, i, jnp.concatenate([dshift, dscale, dgate], axis=1))
        _acc_all(dg_ref, i, dg)

    return pl.pallas_call(
        body, name=name, grid=(T // tm,),
        in_specs=[_rows(tm, dm), _rows(tm, dm), _rows(tm, 2 * fh), _rows(tm, dm), _full((1, dm)), _ctx_lat(3 * dm),
                  _full(w_in.shape), _full(w_out.shape)],
        out_specs=[_rows(tm, dm), _rows(tm, dm), _rows(tm, 2 * fh), _rows(tm, fh), _rows(tm, dm),
                   _ctx_lat(3 * dm), _whole((1, dm))],
        out_shape=[jax.ShapeDtypeStruct((T, dm), F32), jax.ShapeDtypeStruct((T, dm), ACT),
                   jax.ShapeDtypeStruct((T, 2 * fh), ACT), jax.ShapeDtypeStruct((T, fh), ACT),
                   jax.ShapeDtypeStruct((T, dm), ACT), jax.ShapeDtypeStruct((2, 1, 3 * dm), F32),
                   jax.ShapeDtypeStruct((1, dm), F32)],
        compiler_params=_cp("arbitrary"),
    )(x1, dx2, u, f, gain, ms, w_in, w_out)


def _wgrad(a, b, name):
    T, K = a.shape
    N = b.shape[1]
    tk, tn, tt = _tile(K, 1024), _tile(N, 1024), _tile(T, 1024)
    nt = T // tt

    def body(a_ref, b_ref, o_ref, acc_ref):
        t = pl.program_id(2)
        part = _tn(a_ref[...], b_ref[...])

        @pl.when(t == 0)
        def _():
            acc_ref[...] = part

        @pl.when(t > 0)
        def _():
            acc_ref[...] += part

        @pl.when(t == nt - 1)
        def _():
            o_ref[...] = acc_ref[...].astype(o_ref.dtype)

    return pl.pallas_call(
        body, name=name, grid=(K // tk, N // tn, nt),
        in_specs=[pl.BlockSpec((tt, tk), lambda i, j, t: (t, i)), pl.BlockSpec((tt, tn), lambda i, j, t: (t, j))],
        out_specs=pl.BlockSpec((tk, tn), lambda i, j, t: (i, j)),
        out_shape=jax.ShapeDtypeStruct((K, N), ACT),
        scratch_shapes=[pltpu.VMEM((tk, tn), F32)],
        compiler_params=_cp("parallel", "parallel", "arbitrary"),
    )(a, b)


def _post_fwd(x, o_fw, o_bw, g_src, g_blk, gain, a, w_out, ms, dvh, tm, name):
    T, dm = x.shape
    hv = o_fw.shape[1]
    aw = 0 if a is None else a.shape[1]
    has_gain = gain is not None

    def body(*refs):
        refs = list(refs)
        x_ref, of_ref, ob_ref, g_ref = refs[:4]
        k = 4
        gain_ref = a_ref = None
        if has_gain:
            gain_ref = refs[k]
            k += 1
        if aw:
            a_ref = refs[k]
            k += 1
        w_ref, ms_ref, x1_ref, z_ref = refs[k:k + 4]
        o = of_ref[...] + ob_ref[...]
        gr = g_ref[...]
        if aw:
            z_ref[:, :aw] = _bf(a_ref[...])
        for hd in range(hv // dvh):
            sl = slice(hd * dvh, (hd + 1) * dvh)
            oh = o[:, sl]
            gh = gr[:, sl]
            r = lax.rsqrt(jnp.mean(oh * oh, axis=-1, keepdims=True) + EPS)
            y = oh * r
            if has_gain:
                y = y * gain_ref[...]
            y = y * (gh * _sig(gh))
            z_ref[:, aw + hd * dvh:aw + (hd + 1) * dvh] = _bf(y)
        yp = _nn(z_ref[...], w_ref[...])
        x1_ref[...] = x_ref[...] + ms_ref[0] * yp

    ins = [x, o_fw, o_bw, g_src]
    specs = [_rows(tm, dm), _rows(tm, hv), _rows(tm, hv), pl.BlockSpec((tm, hv), lambda i: (i, g_blk))]
    if has_gain:
        ins.append(gain)
        specs.append(_full(gain.shape))
    if aw:
        ins.append(a)
        specs.append(_rows(tm, aw))
    ins += [w_out, ms]
    specs += [_full(w_out.shape), _ctx_lat(dm)]
    return pl.pallas_call(
        body, name=name, grid=(T // tm,), in_specs=specs,
        out_specs=[_rows(tm, dm), _rows(tm, aw + hv)],
        out_shape=[jax.ShapeDtypeStruct((T, dm), F32), jax.ShapeDtypeStruct((T, aw + hv), ACT)],
        compiler_params=_cp("arbitrary"),
    )(*ins)


def _post_bwd(dx1, z, o_fw, o_bw, g_src, g_blk, gain, w_out, ms, aw, dvh, tm, name):
    T, dm = dx1.shape
    hv = o_fw.shape[1]
    has_gain = gain is not None

    def body(*refs):
        refs = list(refs)
        dx1_ref, z_ref, of_ref, ob_ref, g_ref = refs[:5]
        k = 5
        gain_ref = None
        if has_gain:
            gain_ref = refs[k]
            k += 1
        w_ref, ms_ref = refs[k:k + 2]
        k += 2
        do_ref, dgr_ref = refs[k:k + 2]
        k += 2
        da_ref = None
        if aw:
            da_ref = refs[k]
            k += 1
        dy_ref, dgate_ref, dgain_ref = refs[k:k + 3]
        i = pl.program_id(0)
        dx1v = dx1_ref[...]
        zb = z_ref[...]
        yp = _nn(zb, w_ref[...])
        _acc_ctx_lat(dgate_ref, i, jnp.sum(dx1v * yp, axis=0, keepdims=True))
        dyb = _bf(dx1v * ms_ref[0])
        dy_ref[...] = dyb
        dz = _nt(dyb, w_ref[...])
        if aw:
            da_ref[...] = dz[:, :aw]
        o = of_ref[...] + ob_ref[...]
        gr = g_ref[...]
        dgain = jnp.zeros((1, dvh), F32)
        for hd in range(hv // dvh):
            sl = slice(hd * dvh, (hd + 1) * dvh)
            oh = o[:, sl]
            gh = gr[:, sl]
            dyh = dz[:, aw + hd * dvh:aw + (hd + 1) * dvh]
            r = lax.rsqrt(jnp.mean(oh * oh, axis=-1, keepdims=True) + EPS)
            n = oh * r
            s = _sig(gh)
            sl_g = gh * s
            gn = gain_ref[...] if has_gain else 1.0
            dgr_ref[:, sl] = dyh * n * gn * (s * (1.0 + gh * (1.0 - s)))
            dn = dyh * gn * sl_g
            dgain = dgain + jnp.sum(dyh * n * sl_g, axis=0, keepdims=True)
            do_ref[:, sl] = r * (dn - n * jnp.mean(dn * n, axis=-1, keepdims=True))
        _acc_all(dgain_ref, i, dgain)

    ins = [dx1, z, o_fw, o_bw, g_src]
    specs = [_rows(tm, dm), _rows(tm, aw + hv), _rows(tm, hv), _rows(tm, hv),
             pl.BlockSpec((tm, hv), lambda i: (i, g_blk))]
    if has_gain:
        ins.append(gain)
        specs.append(_full(gain.shape))
    ins += [w_out, ms]
    specs += [_full(w_out.shape), _ctx_lat(dm)]
    out_specs = [_rows(tm, hv), _rows(tm, hv)]
    out_shape = [jax.ShapeDtypeStruct((T, hv), F32), jax.ShapeDtypeStruct((T, hv), F32)]
    if aw:
        out_specs.append(_rows(tm, aw))
        out_shape.append(jax.ShapeDtypeStruct((T, aw), F32))
    out_specs += [_rows(tm, dm), _ctx_lat(dm), _whole((1, dvh))]
    out_shape += [jax.ShapeDtypeStruct((T, dm), ACT), jax.ShapeDtypeStruct((2, 1, dm), F32),
                  jax.ShapeDtypeStruct((1, dvh), F32)]
    return pl.pallas_call(
        body, name=name, grid=(T // tm,), in_specs=specs, out_specs=out_specs, out_shape=out_shape,
        compiler_params=_cp("arbitrary"),
    )(*ins)


def _loss_bwd(x, target, tm, name):
    T, dm = x.shape

    def body(x_ref, t_ref, dx_ref, loss_ref):
        i = pl.program_id(0)

        @pl.when(i == 0)
        def _():
            dx_ref[...] = jnp.zeros_like(dx_ref)
            loss_ref[...] = jnp.zeros_like(loss_ref)

        @pl.when(i > 0)
        def _():
            e = x_ref[...] - t_ref[...]
            dx_ref[...] = e * (1.0 / dm)
            loss_ref[...] += jnp.sum(e * e) * (0.5 / dm)

    return pl.pallas_call(
        body, name=name, grid=(T // tm,),
        in_specs=[_rows(tm, dm), pl.BlockSpec((tm, dm), lambda i: (jnp.maximum(i - 1, 0), 0))],
        out_specs=[_rows(tm, dm), _whole((1, 1))],
        out_shape=[jax.ShapeDtypeStruct((T, dm), F32), jax.ShapeDtypeStruct((1, 1), F32)],
        compiler_params=_cp("arbitrary"),
    )(x, target)


def _swap_matrix():
    r = lax.broadcasted_iota(jnp.int32, (HEAD_DIM, HEAD_DIM), 0)
    c = lax.broadcasted_iota(jnp.int32, (HEAD_DIM, HEAD_DIM), 1)
    return jnp.where((r + HEAD_DIM // 2) % HEAD_DIM == c, 1.0, 0.0).astype(BF16)


def _qk_prep_fwd(raw, gains, cos2, sin2, tq, name):
    nh, T, hd = raw.shape

    def body(x_ref, g_ref, c_ref, s_ref, o_ref):
        hidx = pl.program_id(0)
        xv = x_ref[0]
        r = lax.rsqrt(jnp.mean(xv * xv, axis=-1, keepdims=True) + EPS)
        n = xv * r * g_ref[0]
        y = n * c_ref[...] + _nn3r(n, _swap_matrix()) * s_ref[...]
        sc = jnp.where(hidx < ATTN_HEADS, HEAD_DIM ** -0.5, 1.0)
        o_ref[0] = _bf(y * sc)

    return pl.pallas_call(
        body, name=name, grid=(nh, T // tq),
        in_specs=[pl.BlockSpec((1, tq, hd), lambda h, i: (h, i, 0)), pl.BlockSpec((1, 1, hd), lambda h, i: (h, 0, 0)),
                  pl.BlockSpec((tq, hd), lambda h, i: (i, 0)), pl.BlockSpec((tq, hd), lambda h, i: (i, 0))],
        out_specs=pl.BlockSpec((1, tq, hd), lambda h, i: (h, i, 0)),
        out_shape=jax.ShapeDtypeStruct((nh, T, hd), ACT),
        compiler_params=_cp("arbitrary", "arbitrary"),
    )(raw, gains, cos2, sin2)


def _qk_prep_bwd(dy, raw, gains, cos2, sin2, tq, name):
    nh, T, hd = raw.shape

    def body(dy_ref, x_ref, g_ref, c_ref, s_ref, dx_ref, dg_ref):
        hidx = pl.program_id(0)
        i = pl.program_id(1)
        xv = x_ref[0]
        g = g_ref[0]
        r = lax.rsqrt(jnp.mean(xv * xv, axis=-1, keepdims=True) + EPS)
        xhat = xv * r
        sc = jnp.where(hidx < ATTN_HEADS, HEAD_DIM ** -0.5, 1.0)
        dyv = dy_ref[0] * sc
        dn = dyv * c_ref[...] + _nn3r(dyv * s_ref[...], _swap_matrix())
        _acc_all(dg_ref, i, jnp.sum(dn * xhat, axis=0, keepdims=True))
        dxh = dn * g
        dx_ref[0] = r * (dxh - xhat * jnp.mean(dxh * xhat, axis=-1, keepdims=True))

    return pl.pallas_call(
        body, name=name, grid=(nh, T // tq),
        in_specs=[pl.BlockSpec((1, tq, hd), lambda h, i: (h, i, 0)), pl.BlockSpec((1, tq, hd), lambda h, i: (h, i, 0)),
                  pl.BlockSpec((1, 1, hd), lambda h, i: (h, 0, 0)),
                  pl.BlockSpec((tq, hd), lambda h, i: (i, 0)), pl.BlockSpec((tq, hd), lambda h, i: (i, 0))],
        out_specs=[pl.BlockSpec((1, tq, hd), lambda h, i: (h, i, 0)), pl.BlockSpec((1, 1, hd), lambda h, i: (h, 0, 0))],
        out_shape=[jax.ShapeDtypeStruct((nh, T, hd), F32), jax.ShapeDtypeStruct((nh, 1, hd), F32)],
        compiler_params=_cp("arbitrary", "arbitrary"),
    )(dy, raw, gains, cos2, sin2)


def _attn_scores(q, k_ref, i, lc, T, sink):
    blk = ATTN_BLOCK
    kc = k_ref[0, pl.ds(blk, lc), :]
    kw = k_ref[0, pl.ds(pl.multiple_of(i * blk, blk), 3 * blk), :]
    s_c = _nt(q, kc)
    s_w = _nt(q, kw)
    row = lax.broadcasted_iota(jnp.int32, (4 * blk, 1), 0)
    qpos = i * blk + (row & (blk - 1))
    kpos = (i - 1) * blk + lax.broadcasted_iota(jnp.int32, (1, 3 * blk), 1)
    valid = (qpos >= lc) & (kpos >= lc) & (kpos < T) & (jnp.abs(kpos - qpos) <= WINDOW)
    s_w = jnp.where(valid, s_w, NEG)
    return kc, kw, s_c, s_w


def _attn_fwd(qt, kp, vp, sinkb, lc, name, ex=None):
    nh, T, hd = qt.shape
    blk = ATTN_BLOCK
    g = nh // ATTN_KV

    def body(q_ref, k_ref, v_ref, sink_ref, o_ref, lse_ref):
        i = pl.program_id(1)
        q = q_ref[...].reshape(g * blk, hd)
        sink = sink_ref[0]
        kc, kw, s_c, s_w = _attn_scores(q, k_ref, i, lc, T, sink)
        m = jnp.maximum(jnp.maximum(jnp.max(s_c, axis=-1, keepdims=True), jnp.max(s_w, axis=-1, keepdims=True)), sink)
        e_c = jnp.exp(s_c - m)
        e_w = jnp.exp(s_w - m)
        den = jnp.exp(sink - m) + jnp.sum(e_c, axis=-1, keepdims=True) + jnp.sum(e_w, axis=-1, keepdims=True)
        inv = 1.0 / den
        vc = v_ref[0, pl.ds(blk, lc), :]
        vw = v_ref[0, pl.ds(pl.multiple_of(i * blk, blk), 3 * blk), :]
        o = _nn(_bf(e_c * inv), vc) + _nn(_bf(e_w * inv), vw)
        o_ref[...] = o.reshape(g, blk, hd)
        lse_ref[...] = (m + jnp.log(den)).reshape(g, blk, 1)

    nb = T // blk
    return _host_call(
        body, ex, lambda: (pl.program_id(0) == 0) & (pl.program_id(1) == 0),
        lambda: (pl.program_id(0) == ATTN_KV - 1) & (pl.program_id(1) == nb - 1),
        name=name, grid=(ATTN_KV, nb),
        in_specs=[pl.BlockSpec((g, blk, hd), lambda kv, i: (kv, i, 0)),
                  pl.BlockSpec((1, T + 2 * blk, hd), lambda kv, i: (kv, 0, 0)),
                  pl.BlockSpec((1, T + 2 * blk, hd), lambda kv, i: (kv, 0, 0)),
                  pl.BlockSpec((1, g * blk, 1), lambda kv, i: (kv, 0, 0))],
        out_specs=[pl.BlockSpec((g, blk, hd), lambda kv, i: (kv, i, 0)),
                   pl.BlockSpec((g, blk, 1), lambda kv, i: (kv, i, 0))],
        out_shape=[jax.ShapeDtypeStruct((nh, T, hd), F32), jax.ShapeDtypeStruct((nh, T, 1), F32)],
        scratch_shapes=[], sem=("arbitrary", "arbitrary"), args=(qt, kp, vp, sinkb))


def _attn_bwd(qt, kp, vp, sinkb, o, lse, do, lc, name):
    nh, T, hd = qt.shape
    blk = ATTN_BLOCK
    g = nh // ATTN_KV

    def body(q_ref, k_ref, v_ref, sink_ref, o_ref, lse_ref, do_ref, dq_ref, dk_ref, dv_ref, ds_ref):
        i = pl.program_id(1)

        @pl.when(i == 0)
        def _():
            dk_ref[...] = jnp.zeros_like(dk_ref)
            dv_ref[...] = jnp.zeros_like(dv_ref)
            ds_ref[...] = jnp.zeros_like(ds_ref)

        q = q_ref[...].reshape(g * blk, hd)
        sink = sink_ref[0]
        lse = lse_ref[...].reshape(g * blk, 1)
        dov = do_ref[...].reshape(g * blk, hd)
        delta = jnp.sum(dov * o_ref[...].reshape(g * blk, hd), axis=-1, keepdims=True)
        kc, kw, s_c, s_w = _attn_scores(q, k_ref, i, lc, T, sink)
        p_c = jnp.exp(s_c - lse)
        p_w = jnp.exp(s_w - lse)
        win = pl.ds(pl.multiple_of(i * blk, blk), 3 * blk)
        vc = v_ref[0, pl.ds(blk, lc), :]
        vw = v_ref[0, win, :]
        dob = _bf(dov)
        ds_c = _bf(p_c * (_nt(dob, vc) - delta))
        ds_w = _bf(p_w * (_nt(dob, vw) - delta))
        dsr = -jnp.exp(sink - lse) * delta
        for hh in range(g):
            ds_ref[0, hh:hh + 1, :] += jnp.sum(dsr[hh * blk:(hh + 1) * blk, :], axis=0, keepdims=True)
        dq_ref[...] = (_nn(ds_c, kc) + _nn(ds_w, kw)).reshape(g, blk, hd)
        dk_ref[0, pl.ds(blk, lc), :] += _tn(ds_c, q)
        dk_ref[0, win, :] += _tn(ds_w, q)
        dv_ref[0, pl.ds(blk, lc), :] += _tn(_bf(p_c), dob)
        dv_ref[0, win, :] += _tn(_bf(p_w), dob)

    qspec = pl.BlockSpec((g, blk, hd), lambda kv, i: (kv, i, 0))
    kspec = pl.BlockSpec((1, T + 2 * blk, hd), lambda kv, i: (kv, 0, 0))
    lspec = pl.BlockSpec((g, blk, 1), lambda kv, i: (kv, i, 0))
    return pl.pallas_call(
        body, name=name, grid=(ATTN_KV, T // blk),
        in_specs=[qspec, kspec, kspec, pl.BlockSpec((1, g * blk, 1), lambda kv, i: (kv, 0, 0)), qspec, lspec, qspec],
        out_specs=[qspec, kspec, kspec, pl.BlockSpec((1, g, 1), lambda kv, i: (kv, 0, 0))],
        out_shape=[jax.ShapeDtypeStruct((nh, T, hd), F32), jax.ShapeDtypeStruct((ATTN_KV, T + 2 * blk, hd), F32),
                   jax.ShapeDtypeStruct((ATTN_KV, T + 2 * blk, hd), F32), jax.ShapeDtypeStruct((ATTN_KV, g, 1), F32)],
        compiler_params=_cp("arbitrary", "arbitrary"),
    )(qt, kp, vp, sinkb, o, lse, do)


def _fw_chunk(s, nc, nt):
    return s


def _bw_chunk(s, nc, nt):
    return jnp.where(s < nc, nc - 1 - s, nt - 1 - (s - nc))


def _tri(c, rev):
    r = lax.broadcasted_iota(jnp.int32, (c, c), 0)
    k = lax.broadcasted_iota(jnp.int32, (c, c), 1)
    return (k >= r) if rev else (k <= r)


def _gla_gates(z, lb, rev):
    c = HG_CHUNK
    sg = _sig(z)
    f = lb + (1.0 - lb) * sg
    cum = _nn3(jnp.where(_tri(c, rev), 1.0, 0.0).astype(BF16), jnp.log(f))
    mid = c - 1 - c // 2 if rev else c // 2
    last = 0 if rev else c - 1
    return sg, f, cum, cum[mid:mid + 1], cum[last:last + 1], last


def _lower_bound(lbraw_ref):
    lr = lbraw_ref[...]
    return _sig(lr[0:1] - lr[1:2])


def _gla_fwd(pb, lbraw, lc, name, ex=None):
    T = pb.shape[0]
    c, hw, d = HG_CHUNK, HG_HEADS * HG_D, HG_D
    nt, nc = T // c, lc // c
    orders = (_fw_chunk, _bw_chunk)

    def body(qf, zf, vf, qb, zb, vb, lb_ref, of_ref, ob_ref, sf_ref, sb_ref, st_ref):
        @pl.when(pl.program_id(0) == 0)
        def _():
            st_ref[...] = jnp.zeros_like(st_ref)

        lb_all = _lower_bound(lb_ref)
        for dr, (q_ref, z_ref, v_ref, o_ref, s_ref) in enumerate(((qf, zf, vf, of_ref, sf_ref), (qb, zb, vb, ob_ref, sb_ref))):
            rev = dr == 1
            mask = _tri(c, rev)
            for h in range(HG_HEADS):
                sl = slice(h * d, (h + 1) * d)
                qr = q_ref[:, sl]
                q = qr * _sig(qr)
                v = _bf(v_ref[:, sl])
                _, f, cum, ref, last, _ = _gla_gates(z_ref[:, sl], lb_all[:, sl], rev)
                k = 1.0 - f
                a = jnp.where(mask, _nt(_bf(q * jnp.exp(cum - ref)), _bf(k * jnp.exp(ref - cum))), 0.0)
                st = st_ref[dr, h]
                stb = _bf(st)
                s_ref[0, h] = stb
                o_ref[:, sl] = _nn(_bf(a), v) + _nt(_bf(q * jnp.exp(cum)), stb)
                st_ref[dr, h] = st * jnp.exp(last) + _tn(v, _bf(k * jnp.exp(last - cum)))

    def col(order, blkcol):
        return pl.BlockSpec((c, hw), lambda s: (order(s, nc, nt), blkcol))

    def st_spec(order):
        return pl.BlockSpec((1, HG_HEADS, d, d), lambda s: (order(s, nc, nt), 0, 0, 0))

    in_specs = []
    for dr, order in enumerate(orders):
        in_specs += [col(order, 0), col(order, 1 + dr), col(order, 3)]
    in_specs.append(_full(lbraw.shape))
    return _host_call(
        body, ex, lambda: pl.program_id(0) == 0, lambda: pl.program_id(0) == nt - 1,
        name=name, grid=(nt,), in_specs=in_specs,
        out_specs=[col(_fw_chunk, 0), col(_bw_chunk, 0), st_spec(_fw_chunk), st_spec(_bw_chunk)],
        out_shape=[jax.ShapeDtypeStruct((T, hw), F32), jax.ShapeDtypeStruct((T, hw), F32),
                   jax.ShapeDtypeStruct((nt, HG_HEADS, d, d), ACT), jax.ShapeDtypeStruct((nt, HG_HEADS, d, d), ACT)],
        scratch_shapes=[pltpu.VMEM((2, HG_HEADS, d, d), F32)], sem=("arbitrary",),
        args=(pb, pb, pb, pb, pb, pb, lbraw))


def _gla_bwd(pb, lbraw, s_fw, s_bw, do, lc, name, ex=None):
    T = pb.shape[0]
    c, hw, d = HG_CHUNK, HG_HEADS * HG_D, HG_D
    nt, nc = T // c, lc // c

    def rfw(s, nc_, nt_):
        return _fw_chunk(nt_ - 1 - s, nc_, nt_)

    def rbw(s, nc_, nt_):
        return _bw_chunk(nt_ - 1 - s, nc_, nt_)

    def body(qf, zf, vf, sf, dof, qb, zb, vb, sb, dob_, lb_ref,
             dqf, dzf, dvf, dqb, dzb, dvb, dlb_ref, dst_ref):
        step = pl.program_id(0)

        @pl.when(step == 0)
        def _():
            dst_ref[...] = jnp.zeros_like(dst_ref)

        lb_all = _lower_bound(lb_ref)
        dlb_parts = []
        sets = ((qf, zf, vf, sf, dof, dqf, dzf, dvf), (qb, zb, vb, sb, dob_, dqb, dzb, dvb))
        for dr, (q_ref, z_ref, v_ref, s_ref, do_ref, dq_ref, dz_ref, dv_ref) in enumerate(sets):
            rev = dr == 1
            mask = _tri(c, rev)
            acc_t = jnp.where(_tri(c, not rev), 1.0, 0.0).astype(BF16)
            dlb_heads = []
            for h in range(HG_HEADS):
                sl = slice(h * d, (h + 1) * d)
                lb = lb_all[:, sl]
                qr = q_ref[:, sl]
                sq = _sig(qr)
                q = qr * sq
                vbf = _bf(v_ref[:, sl])
                sg, f, cum, ref, last, last_row = _gla_gates(z_ref[:, sl], lb, rev)
                k = 1.0 - f
                e_qr = jnp.exp(cum - ref)
                e_kr = jnp.exp(ref - cum)
                e_q = jnp.exp(cum)
                e_kl = jnp.exp(last - cum)
                el = jnp.exp(last)
                q1 = q * e_qr
                k1 = k * e_kr
                q2 = q * e_q
                k2 = k * e_kl
                q1b, k1b, q2b, k2b = _bf(q1), _bf(k1), _bf(q2), _bf(k2)
                a = jnp.where(mask, _nt(q1b, k1b), 0.0)
                dob = _bf(do_ref[:, sl])
                stb = s_ref[0, h]
                dst = dst_ref[dr, h]
                dstb = _bf(dst)
                da = _bf(jnp.where(mask, _nt(dob, vbf), 0.0))
                dv_ref[:, sl] = _tn(_bf(a), dob) + _nt(k2b, dstb)
                dq1 = _nn(da, k1b)
                dk1 = _tn(da, q1b)
                dq2 = _nn(dob, stb)
                dk2 = _nn(vbf, dstb)
                dst_ref[dr, h] = _tn(dob, q2b) + dst * el
                dq = dq1 * e_qr + dq2 * e_q
                dk = dk1 * e_kr + dk2 * e_kl
                dcum = dq1 * q1 - dk1 * k1 + dq2 * q2 - dk2 * k2
                dlast = (jnp.sum(dk2 * k2, axis=0, keepdims=True)
                         + jnp.sum(dst * stb.astype(F32), axis=0, keepdims=True) * el)
                rowid = lax.broadcasted_iota(jnp.int32, (c, 1), 0)
                dcum = dcum + jnp.where(rowid == last_row, dlast, 0.0)
                dlf = _nn3(acc_t, dcum)
                df = dlf / f - dk
                dz_ref[:, sl] = df * (1.0 - lb) * sg * (1.0 - sg)
                dlb_heads.append(jnp.sum(df * (1.0 - sg), axis=0, keepdims=True))
                dq_ref[:, sl] = dq * (sq * (1.0 + qr * (1.0 - sq)))
            dlb_parts.append(jnp.concatenate(dlb_heads, axis=1))
        _acc_all(dlb_ref, step, dlb_parts[0] + dlb_parts[1])

    def col(order, blkcol):
        return pl.BlockSpec((c, hw), lambda s: (order(s, nc, nt), blkcol))

    def st_spec(order):
        return pl.BlockSpec((1, HG_HEADS, d, d), lambda s: (order(s, nc, nt), 0, 0, 0))

    in_specs = []
    for dr, order in enumerate((rfw, rbw)):
        in_specs += [col(order, 0), col(order, 1 + dr), col(order, 3), st_spec(order), col(order, 0)]
    in_specs.append(_full(lbraw.shape))
    out_specs = [col(rfw, 0)] * 3 + [col(rbw, 0)] * 3 + [_whole((1, hw))]
    out_shape = [jax.ShapeDtypeStruct((T, hw), F32)] * 6 + [jax.ShapeDtypeStruct((1, hw), F32)]
    return _host_call(
        body, ex, lambda: pl.program_id(0) == 0, lambda: pl.program_id(0) == nt - 1,
        name=name, grid=(nt,), in_specs=in_specs, out_specs=out_specs, out_shape=out_shape,
        scratch_shapes=[pltpu.VMEM((2, HG_HEADS, d, d), F32)], sem=("arbitrary",),
        args=(pb, pb, pb, s_fw, do, pb, pb, pb, s_bw, do, lbraw))


def _ret_log_gamma(h, rev):
    hh = RET_HEADS - 1 - h if rev else h
    return math.log(1.0 - 2.0 ** (-5.0 - hh))


def _rope(x, cos, sin):
    half = x.shape[1] // 2
    x1, x2 = x[:, :half], x[:, half:]
    return jnp.concatenate([x1 * cos - x2 * sin, x2 * cos + x1 * sin], axis=1)


def _unrope(dy, cos, sin):
    half = dy.shape[1] // 2
    d1, d2 = dy[:, :half], dy[:, half:]
    return jnp.concatenate([d1 * cos + d2 * sin, d2 * cos - d1 * sin], axis=1)


def _ret_decays(lg, rev):
    c = RET_CHUNK
    r = lax.broadcasted_iota(jnp.int32, (c, c), 0)
    k = lax.broadcasted_iota(jnp.int32, (c, c), 1)
    rel = (k - r) if rev else (r - k)
    dm = jnp.where(rel >= 0, jnp.exp(lg * jnp.maximum(rel, 0).astype(F32)), 0.0)
    pos = lax.broadcasted_iota(jnp.int32, (c, 1), 0).astype(F32)
    if rev:
        qdec = jnp.exp(lg * (c - pos))
        kdec = jnp.exp(lg * pos)
    else:
        qdec = jnp.exp(lg * (pos + 1.0))
        kdec = jnp.exp(lg * (c - 1.0 - pos))
    return dm, qdec, kdec


def _ret_fwd(q, k, v, cos, sin, lc, name, ex=None):
    T = q.shape[0]
    c, dk, dv = RET_CHUNK, RET_DK, RET_DV
    nt, nc = T // c, lc // c
    kscale = dk ** -0.5

    def body(qf, kf, vf, cf, sf_, qb, kb, vb, cb, sb_, of_ref, ob_ref, stf_ref, stb_ref, st_ref):
        @pl.when(pl.program_id(0) == 0)
        def _():
            st_ref[...] = jnp.zeros_like(st_ref)

        sets = ((qf, kf, vf, cf, sf_, of_ref, stf_ref), (qb, kb, vb, cb, sb_, ob_ref, stb_ref))
        for dr, (q_ref, k_ref, v_ref, c_ref, s_ref, o_ref, so_ref) in enumerate(sets):
            rev = dr == 1
            cos_v, sin_v = c_ref[...], s_ref[...]
            for h in range(RET_HEADS):
                lg = _ret_log_gamma(h, rev)
                dm, qdec, kdec = _ret_decays(lg, rev)
                qh = _rope(q_ref[:, h * dk:(h + 1) * dk], cos_v, sin_v)
                kh = _rope(k_ref[:, h * dk:(h + 1) * dk], cos_v, sin_v) * kscale
                vh = _bf(v_ref[:, h * dv:(h + 1) * dv])
                st = st_ref[dr, h]
                stb = _bf(st)
                so_ref[0, h] = stb
                sc = _nt(_bf(qh), _bf(kh)) * dm
                o_ref[:, h * dv:(h + 1) * dv] = _nn(_bf(sc), vh) + _nt(_bf(qh * qdec), stb)
                st_ref[dr, h] = st * math.exp(lg * c) + _tn(vh, _bf(kh * kdec))

    def spec(order, width):
        return pl.BlockSpec((c, width), lambda s: (order(s, nc, nt), 0))

    def st_spec(order):
        return pl.BlockSpec((1, RET_HEADS, dv, dk), lambda s: (order(s, nc, nt), 0, 0, 0))

    in_specs = []
    for order in (_fw_chunk, _bw_chunk):
        in_specs += [spec(order, RET_HEADS * dk), spec(order, RET_HEADS * dk), spec(order, RET_HEADS * dv),
                     spec(order, dk // 2), spec(order, dk // 2)]
    return _host_call(
        body, ex, lambda: pl.program_id(0) == 0, lambda: pl.program_id(0) == nt - 1,
        name=name, grid=(nt,), in_specs=in_specs,
        out_specs=[spec(_fw_chunk, RET_HEADS * dv), spec(_bw_chunk, RET_HEADS * dv), st_spec(_fw_chunk), st_spec(_bw_chunk)],
        out_shape=[jax.ShapeDtypeStruct((T, RET_HEADS * dv), F32), jax.ShapeDtypeStruct((T, RET_HEADS * dv), F32),
                   jax.ShapeDtypeStruct((nt, RET_HEADS, dv, dk), ACT), jax.ShapeDtypeStruct((nt, RET_HEADS, dv, dk), ACT)],
        scratch_shapes=[pltpu.VMEM((2, RET_HEADS, dv, dk), F32)], sem=("arbitrary",),
        args=(q, k, v, cos, sin, q, k, v, cos, sin))


def _ret_bwd(q, k, v, cos, sin, s_fw, s_bw, do, lc, name, ex=None):
    T = q.shape[0]
    c, dk, dv = RET_CHUNK, RET_DK, RET_DV
    nt, nc = T // c, lc // c
    kscale = dk ** -0.5

    def rfw(s, nc_, nt_):
        return _fw_chunk(nt_ - 1 - s, nc_, nt_)

    def rbw(s, nc_, nt_):
        return _bw_chunk(nt_ - 1 - s, nc_, nt_)

    def body(qf, kf, vf, cf, sf_, stf, dof, qb, kb, vb, cb, sb_, stb_, dob_,
             dqf, dkf, dvf, dqb, dkb, dvb, dst_ref):
        @pl.when(pl.program_id(0) == 0)
        def _():
            dst_ref[...] = jnp.zeros_like(dst_ref)

        sets = ((qf, kf, vf, cf, sf_, stf, dof, dqf, dkf, dvf), (qb, kb, vb, cb, sb_, stb_, dob_, dqb, dkb, dvb))
        for dr, (q_ref, k_ref, v_ref, c_ref, s_ref, st_in, do_ref, dq_ref, dk_ref, dv_ref) in enumerate(sets):
            rev = dr == 1
            cos_v, sin_v = c_ref[...], s_ref[...]
            for h in range(RET_HEADS):
                lg = _ret_log_gamma(h, rev)
                dm, qdec, kdec = _ret_decays(lg, rev)
                qh = _rope(q_ref[:, h * dk:(h + 1) * dk], cos_v, sin_v)
                kh = _rope(k_ref[:, h * dk:(h + 1) * dk], cos_v, sin_v) * kscale
                vh = _bf(v_ref[:, h * dv:(h + 1) * dv])
                qb16, kb16 = _bf(qh), _bf(kh)
                qinb, kinb = _bf(qh * qdec), _bf(kh * kdec)
                dob = _bf(do_ref[:, h * dv:(h + 1) * dv])
                stb = st_in[0, h]
                dst = dst_ref[dr, h]
                dstb = _bf(dst)
                sc = _bf(_nt(qb16, kb16) * dm)
                dsc = _bf(_nt(dob, vh) * dm)
                dq_r = _nn(dsc, kb16) + _nn(dob, stb) * qdec
                dk_r = _tn(dsc, qb16) + _nn(vh, dstb) * kdec
                dv_ref[:, h * dv:(h + 1) * dv] = _tn(sc, dob) + _nt(kinb, dstb)
                dst_ref[dr, h] = _tn(dob, qinb) + dst * math.exp(lg * c)
                dq_ref[:, h * dk:(h + 1) * dk] = _unrope(dq_r, cos_v, sin_v)
                dk_ref[:, h * dk:(h + 1) * dk] = _unrope(dk_r * kscale, cos_v, sin_v)

    def spec(order, width):
        return pl.BlockSpec((c, width), lambda s: (order(s, nc, nt), 0))

    def st_spec(order):
        return pl.BlockSpec((1, RET_HEADS, dv, dk), lambda s: (order(s, nc, nt), 0, 0, 0))

    in_specs = []
    for order in (rfw, rbw):
        in_specs += [spec(order, RET_HEADS * dk), spec(order, RET_HEADS * dk), spec(order, RET_HEADS * dv),
                     spec(order, dk // 2), spec(order, dk // 2), st_spec(order), spec(order, RET_HEADS * dv)]
    out_specs, out_shape = [], []
    for order in (rfw, rbw):
        out_specs += [spec(order, RET_HEADS * dk), spec(order, RET_HEADS * dk), spec(order, RET_HEADS * dv)]
        out_shape += [jax.ShapeDtypeStruct((T, RET_HEADS * dk), F32), jax.ShapeDtypeStruct((T, RET_HEADS * dk), F32),
                      jax.ShapeDtypeStruct((T, RET_HEADS * dv), F32)]
    return _host_call(
        body, ex, lambda: pl.program_id(0) == 0, lambda: pl.program_id(0) == nt - 1,
        name=name, grid=(nt,), in_specs=in_specs, out_specs=out_specs, out_shape=out_shape,
        scratch_shapes=[pltpu.VMEM((2, RET_HEADS, dv, dk), F32)], sem=("arbitrary",),
        args=(q, k, v, cos, sin, s_fw, do, q, k, v, cos, sin, s_bw, do))


def _attn_rope_tables(lc, l):
    t = jnp.arange(l)
    row = (t // GRID_W).astype(F32)
    colp = (t % GRID_W).astype(F32)
    n_freq = HEAD_DIM // 4
    inv = 10000.0 ** (-jnp.arange(n_freq, dtype=F32) / n_freq)
    ang = jnp.concatenate([row[:, None] * inv, colp[:, None] * inv], axis=-1)
    cos = jnp.concatenate([jnp.ones((lc, HEAD_DIM // 2), F32), jnp.cos(ang)], axis=0)
    sin = jnp.concatenate([jnp.zeros((lc, HEAD_DIM // 2), F32), jnp.sin(ang)], axis=0)
    return jnp.concatenate([cos, cos], axis=1), jnp.concatenate([-sin, sin], axis=1)


def _ret_rope_tables(lc, l):
    theta = 1.0 / (10000.0 ** jnp.linspace(0.0, 1.0, RET_DK // 2, dtype=F32))
    ang = jnp.arange(l, dtype=F32)[:, None] * theta
    cos = jnp.concatenate([jnp.ones((lc, RET_DK // 2), F32), jnp.cos(ang)], axis=0)
    sin = jnp.concatenate([jnp.zeros((lc, RET_DK // 2), F32), jnp.sin(ang)], axis=0)
    return cos, sin


def _heads_major(slab, n_heads):
    t = slab.shape[0]
    return slab.reshape(t, n_heads, HEAD_DIM).transpose(1, 0, 2)


def _slab(hm):
    nh, t, hd = hm.shape
    return hm.transpose(1, 0, 2).reshape(t, nh * hd)


COL_SHARDED = ("ffn_in0", "ffn_in1", "even_in", "odd_in")


def _full_weight(name, g):
    if name in COL_SHARDED:
        return g.transpose(1, 0, 2).reshape(g.shape[1], -1)
    return g.reshape(-1, g.shape[2])


def _shard_slots(name, g):
    if name in COL_SHARDED:
        return g.reshape(g.shape[0], N_DEV, -1).transpose(1, 0, 2)
    return g.reshape(N_DEV, -1, g.shape[1])


def _local_step(xs, target, mv, norm_g, w, qk_g, sink, hg_out_g, lbraw, lc, shards=None):
    T, dm = xs.shape
    l = T - lc
    tm = lc
    blk = ATTN_BLOCK
    d2, d3 = 2 * dm, 3 * dm
    w = dict(w)
    gw, recv = {}, {}

    def ms(layer, a, b):
        return mv[layer, :, :, a:b]

    def gather(names):
        return None if shards is None else _Exchange(GATHER, [shards[n] for n in names])

    def arrived(names, got):
        for n, g in zip(names, got):
            w[n] = _full_weight(n, g)

    def scatter(names):
        return None if shards is None else _Exchange(SCATTER, [_shard_slots(n, gw[n]) for n in names])

    def scattered(names, got):
        for n, g in zip(names, got):
            recv[n] = g

    g00, g01, g10, g11 = (norm_g[i, j][None, :] for i in (0, 1) for j in (0, 1))

    pa, pb = _pre_fwd(xs, g00, ms(0, 0, d2), w["even_in"], ((0, 768), (768, 3328)), tm, "pre0_fwd")
    raw10 = jnp.concatenate([_heads_major(pa[:, :512], 8), _heads_major(pa[:, 512:640], 2)], axis=0)
    v_hm = _heads_major(pa[:, 640:768], 2)
    cos2, sin2 = _attn_rope_tables(lc, l)
    gains10 = jnp.concatenate([jnp.broadcast_to(qk_g[0], (8, HEAD_DIM)), jnp.broadcast_to(qk_g[1], (2, HEAD_DIM))])[:, None, :]
    qk_t = _qk_prep_fwd(raw10, gains10, cos2, sin2, tm, "qk_prep_fwd")
    pad = ((0, 0), (blk, blk), (0, 0))
    qt = qk_t[:8]
    kp = jnp.pad(qk_t[8:], pad)
    vp = jnp.pad(_bf(v_hm), pad)
    sinkb = jnp.broadcast_to(sink.reshape(ATTN_KV, 4, 1, 1), (ATTN_KV, 4, blk, 1)).reshape(ATTN_KV, 4 * blk, 1)
    (o_att, lse), got = _attn_fwd(qt, kp, vp, sinkb, lc, "attn_fwd", gather(["ffn_in0", "ffn_out0"]))
    arrived(["ffn_in0", "ffn_out0"], got)
    a_slab = _slab(o_att)
    (hg_of, hg_ob, hg_sf, hg_sb), got = _gla_fwd(pb, lbraw, lc, "hgrn_fwd", gather(["odd_in", "odd_out"]))
    arrived(["odd_in", "odd_out"], got)
    x01, z0 = _post_fwd(xs, hg_of, hg_ob, pb, 4, hg_out_g, a_slab, w["even_out"], ms(0, d2, d3), HG_D, tm, "post0_fwd")
    x02, u0, f0 = _ffn_fwd(x01, g01, ms(0, d3, 6 * dm), w["ffn_in0"], w["ffn_out0"], tm, "ffn0_fwd")

    rq, rk, rv, rg = _pre_fwd(x02, g10, ms(1, 0, d2), w["odd_in"],
                              ((0, 1024), (1024, 2048), (2048, 4096), (4096, 6144)), tm, "pre1_fwd")
    rcos, rsin = _ret_rope_tables(lc, l)
    (rt_of, rt_ob, rt_sf, rt_sb), got = _ret_fwd(rq, rk, rv, rcos, rsin, lc, "ret_fwd", gather(["ffn_in1", "ffn_out1"]))
    arrived(["ffn_in1", "ffn_out1"], got)
    x11, z1 = _post_fwd(x02, rt_of, rt_ob, rg, 0, None, None, w["odd_out"], ms(1, d2, d3), RET_DV, tm, "post1_fwd")
    x12, u1, f1 = _ffn_fwd(x11, g11, ms(1, d3, 6 * dm), w["ffn_in1"], w["ffn_out1"], tm, "ffn1_fwd")

    dx, loss = _loss_bwd(x12, target, tm, "loss_bwd")

    dx, h, du, act, df, dms_f1, dg11 = _ffn_bwd(x11, dx, u1, f1, g11, ms(1, d3, 6 * dm), w["ffn_in1"], w["ffn_out1"], tm, "ffn1_bwd")
    gw["ffn_in1"] = _wgrad(h, du, "wg_ffn_in1")
    gw["ffn_out1"] = _wgrad(act, df, "wg_ffn_out1")
    do1, dgr1, dy1, dgate_p1, _ = _post_bwd(dx, z1, rt_of, rt_ob, rg, 0, None, w["odd_out"], ms(1, d2, d3), 0, RET_DV, tm, "post1_bwd")
    gw["odd_out"] = _wgrad(z1, dy1, "wg_odd_out")
    (dqf, dkf, dvf, dqb, dkb, dvb), got = _ret_bwd(rq, rk, rv, rcos, rsin, rt_sf, rt_sb, do1, lc, "ret_bwd",
                                                   scatter(["ffn_in1", "ffn_out1"]))
    scattered(["ffn_in1", "ffn_out1"], got)
    dx, h, dp, dms_p1, dg10 = _pre_bwd(x02, dx, g10, ms(1, 0, d2), w["odd_in"],
                                       [(0, [dqf, dqb]), (1024, [dkf, dkb]), (2048, [dvf, dvb]), (4096, [dgr1])], tm, "pre1_bwd")
    gw["odd_in"] = _wgrad(h, dp, "wg_odd_in")

    dx, h, du, act, df, dms_f0, dg01 = _ffn_bwd(x01, dx, u0, f0, g01, ms(0, d3, 6 * dm), w["ffn_in0"], w["ffn_out0"], tm, "ffn0_bwd")
    gw["ffn_in0"] = _wgrad(h, du, "wg_ffn_in0")
    gw["ffn_out0"] = _wgrad(act, df, "wg_ffn_out0")
    do0, dgr0, da0, dy0, dgate_p0, d_hg_gain = _post_bwd(dx, z0, hg_of, hg_ob, pb, 4, hg_out_g, w["even_out"], ms(0, d2, d3),
                                                        512, HG_D, tm, "post0_bwd")
    gw["even_out"] = _wgrad(z0, dy0, "wg_even_out")
    late = ["odd_in", "odd_out", "ffn_in0", "ffn_out0"]
    (hq_f, hz_f, hv_f, hq_b, hz_b, hv_b, dlb), got = _gla_bwd(pb, lbraw, hg_sf, hg_sb, do0, lc, "hgrn_bwd", scatter(late))
    scattered(late, got)
    dq_att, dkp, dvp, dsink = _attn_bwd(qt, kp, vp, sinkb, o_att, lse, _heads_major(da0, 8), lc, "attn_bwd")
    dy10 = jnp.concatenate([dq_att, dkp[:, blk:blk + T]], axis=0)
    draw10, dgain10 = _qk_prep_bwd(dy10, raw10, gains10, cos2, sin2, tm, "qk_prep_bwd")
    pieces0 = [(0, [_slab(draw10[:8])]), (512, [_slab(draw10[8:])]), (640, [_slab(dvp[:, blk:blk + T])]),
               (768, [hq_f, hq_b]), (1280, [hz_f]), (1792, [hz_b]), (2304, [hv_f, hv_b]), (2816, [dgr0])]
    dx, h, dp, dms_p0, dg00 = _pre_bwd(xs, dx, g00, ms(0, 0, d2), w["even_in"], pieces0, tm, "pre0_bwd")
    gw["even_in"] = _wgrad(h, dp, "wg_even_in")

    dmv = jnp.stack([jnp.concatenate([dms_p0, dgate_p0, dms_f0], axis=2), jnp.concatenate([dms_p1, dgate_p1, dms_f1], axis=2)])
    small = {
        "dmv": dmv,
        "norm_g": jnp.stack([jnp.stack([dg00[0], dg01[0]]), jnp.stack([dg10[0], dg11[0]])]),
        "qk_g": jnp.stack([jnp.sum(dgain10[:8, 0], axis=0), jnp.sum(dgain10[8:, 0], axis=0)]),
        "sink": dsink.reshape(ATTN_HEADS),
        "hg_out_g": d_hg_gain[0],
        "lb": dlb[0],
        "loss": loss[0, 0],
    }
    if shards is not None:
        gw = {n: recv.get(n, g) for n, g in gw.items()}
    return loss, dx, gw, small


HBM_SPEC = pl.BlockSpec(memory_space=pltpu.HBM)


def _my_index():
    return 4 * lax.axis_index("x") + 2 * lax.axis_index("y") + lax.axis_index("c")


def _peer(k):
    pos = []
    for axis, bit in (("x", 4), ("y", 2), ("c", 1)):
        a = lax.axis_index(axis)
        pos.append(1 - a if k & bit else a)
    return tuple(pos)


def _peer_index(k):
    px, py, pc = _peer(k)
    return 4 * px + 2 * py + pc


GATHER, SCATTER = "gather", "scatter"


class _Exchange:
    def __init__(self, mode, arrays):
        self.mode, self.arrays, self.n = mode, list(arrays), len(arrays)

    def out_shape(self):
        if self.mode == GATHER:
            return [jax.ShapeDtypeStruct((N_DEV,) + a.shape, a.dtype) for a in self.arrays]
        return [jax.ShapeDtypeStruct(a.shape, a.dtype) for a in self.arrays]

    def specs(self):
        return [HBM_SPEC] * self.n

    def scratch(self):
        return [pltpu.SemaphoreType.DMA((self.n, N_DEV - 1)), pltpu.SemaphoreType.DMA((self.n, N_DEV - 1)),
                pltpu.SemaphoreType.DMA((self.n,))]

    def _copies(self, in_refs, out_refs, send_sems, recv_sems, local_sems):
        me = _my_index()
        local, starts, waits = [], [], []
        for a, (src, dst) in enumerate(zip(in_refs, out_refs)):
            part = (lambda j, s=src: s) if self.mode == GATHER else (lambda j, s=src: s.at[j])
            local.append(pltpu.make_async_copy(part(me), dst.at[me], local_sems.at[a]))
            for k in range(1, N_DEV):
                pj = _peer_index(k)
                sems = dict(send_sem=send_sems.at[a, k - 1], recv_sem=recv_sems.at[a, k - 1], device_id=_peer(k),
                            device_id_type=MESH)
                starts.append(pltpu.make_async_remote_copy(src_ref=part(pj), dst_ref=dst.at[me], **sems))
                waits.append(pltpu.make_async_remote_copy(src_ref=part(pj), dst_ref=dst.at[pj], **sems))
        return local, starts, waits

    def start(self, in_refs, out_refs, sems):
        local, starts, _ = self._copies(in_refs, out_refs, *sems)
        for cp in local + starts:
            cp.start()

    def wait(self, in_refs, out_refs, sems):
        local, _, waits = self._copies(in_refs, out_refs, *sems)
        for cp in waits:
            cp.wait_send()
            cp.wait_recv()
        for cp in local:
            cp.wait()

    def ride(self, refs, n_in, n_out, first, last):
        refs = list(refs)
        n = self.n
        x_in = refs[n_in:n_in + n]
        x_out = refs[n_in + n + n_out:n_in + 2 * n + n_out]
        sems = refs[n_in + 2 * n + n_out:n_in + 2 * n + n_out + 3]

        @pl.when(first)
        def _():
            self.start(x_in, x_out, sems)

        @pl.when(last)
        def _():
            self.wait(x_in, x_out, sems)

        return refs[:n_in] + refs[n_in + n:n_in + n + n_out] + refs[n_in + 2 * n + n_out + 3:]

    def call(self, name):
        n = self.n

        def body(*refs):
            ins, outs, sems = refs[:n], refs[n:2 * n], refs[2 * n:]
            self.start(ins, outs, sems)
            self.wait(ins, outs, sems)

        return pl.pallas_call(body, name=name, in_specs=self.specs(), out_specs=self.specs(), out_shape=self.out_shape(),
                              scratch_shapes=self.scratch())(*self.arrays)


def _all_gather(v, name):
    return _Exchange(GATHER, [v]).call(name)[0]


def _hosted(kernel_body, ex, n_in, n_out, first, last):
    if ex is None:
        return kernel_body

    def body(*refs):
        kernel_body(*ex.ride(refs, n_in, n_out, first(), last()))

    return body


def _host_call(kernel_body, ex, first, last, name, grid, in_specs, out_specs, out_shape, scratch_shapes, sem, args):
    n_in, n_out = len(in_specs), len(out_specs)
    if ex is None:
        outs = pl.pallas_call(kernel_body, name=name, grid=grid, in_specs=in_specs, out_specs=out_specs, out_shape=out_shape,
                              scratch_shapes=scratch_shapes, compiler_params=_cp(*sem))(*args)
        return list(outs), []
    outs = pl.pallas_call(
        _hosted(kernel_body, ex, n_in, n_out, first, last), name=name, grid=grid,
        in_specs=list(in_specs) + ex.specs(), out_specs=list(out_specs) + ex.specs(),
        out_shape=list(out_shape) + ex.out_shape(), scratch_shapes=ex.scratch() + list(scratch_shapes),
        compiler_params=_cp(*sem))(*args, *ex.arrays)
    return list(outs[:n_out]), list(outs[n_out:])


def _mod_fwd(call, mod_w, bias, name):
    nl, dm, n = mod_w.shape

    def body(c_ref, w_ref, b_ref, o_ref):
        cv = c_ref[...]
        cond = _bf(cv * _sig(cv))
        for layer in range(nl):
            o_ref[layer] = _nn(cond, _bf(w_ref[layer])) + b_ref[layer]

    return pl.pallas_call(
        body, name=name, out_shape=jax.ShapeDtypeStruct((nl, call.shape[0], n), F32),
        compiler_params=pltpu.CompilerParams(vmem_limit_bytes=VMEM_LIMIT),
    )(call, mod_w, bias)


def _mod_bwd(call, dm_all, mod_w, name):
    nl, dm, n = mod_w.shape

    def body(c_ref, d_ref, w_ref, gw_ref, dc_ref):
        cv = c_ref[...]
        cond = _bf(cv * _sig(cv))
        dc = jnp.zeros(cv.shape, F32)
        for layer in range(nl):
            db = _bf(d_ref[layer])
            gw_ref[layer] = _tn(cond, db)
            dc = dc + _nt(db, _bf(w_ref[layer]))
        dc_ref[...] = dc

    return pl.pallas_call(
        body, name=name,
        out_shape=[jax.ShapeDtypeStruct(mod_w.shape, F32), jax.ShapeDtypeStruct(call.shape, F32)],
        compiler_params=pltpu.CompilerParams(vmem_limit_bytes=VMEM_LIMIT),
    )(call, dm_all, mod_w)


def _sum_parts(g, name):
    def body(g_ref, o_ref):
        acc = g_ref[0]
        for j in range(1, g.shape[0]):
            acc = acc + g_ref[j]
        o_ref[...] = acc

    return pl.pallas_call(body, name=name, out_shape=jax.ShapeDtypeStruct(g.shape[1:], g.dtype))(g)


def _small_finish(dcond_g, c_ctx, dlb, lbraw, dm_ctx, dm_lat, name):
    def body(dc_ref, c_ref, dlb_ref, lb_ref, mc_ref, ml_ref, gc_ref, glb_ref, gb_ref):
        acc = dc_ref[0, 0:1, :]
        for j in range(1, N_DEV):
            acc = acc + dc_ref[j, 0:1, :]
        cv = c_ref[...]
        s = _sig(cv)
        gc_ref[...] = acc * (s * (1.0 + cv * (1.0 - s)))
        lb = _lower_bound(lb_ref)
        d0 = dlb_ref[...] * lb * (1.0 - lb)
        glb_ref[0:1, :] = d0
        glb_ref[1:2, :] = -d0
        gb_ref[...] = mc_ref[...] + ml_ref[...]

    return pl.pallas_call(
        body, name=name,
        out_shape=[jax.ShapeDtypeStruct(c_ctx.shape, F32), jax.ShapeDtypeStruct(lbraw.shape, F32),
                   jax.ShapeDtypeStruct(dm_ctx.shape, F32)],
    )(dcond_g, c_ctx, dlb, lbraw, dm_ctx, dm_lat)


def _row_tile(r, cap, mult):
    best = r
    for t in range(mult, min(r, cap) + 1, mult):
        if r % t == 0:
            best = t
    return best


def _adam(g_list, w, m, v, name):
    nl, r, cdim = w.shape
    p = g_list[0].shape[0]
    tr = _row_tile(r, 128, 16)
    ni = r // tr

    def body(*refs):
        g_refs = refs[:nl]
        w_ref, m_ref, v_ref, go_ref, d_ref, mo_ref, vo_ref = refs[nl:]
        layer = pl.program_id(0)

        def total(g_ref):
            acc = g_ref[0].astype(F32)
            for j in range(1, p):
                acc = acc + g_ref[j].astype(F32)
            return acc

        g = total(g_refs[0])
        for k in range(1, nl):
            g = jnp.where(layer == k, total(g_refs[k]), g)
        m2 = ADAM_B1 * m_ref[0] + (1.0 - ADAM_B1) * g
        v2 = ADAM_B2 * v_ref[0] + (1.0 - ADAM_B2) * (g * g)
        m_hat = m2 / (1.0 - ADAM_B1 ** ADAM_STEP)
        v_hat = v2 / (1.0 - ADAM_B2 ** ADAM_STEP)
        go_ref[0] = g
        d_ref[0] = -ADAM_LR * (m_hat / (jnp.sqrt(v_hat) + ADAM_EPS) + ADAM_WD * w_ref[0])
        mo_ref[0] = m2
        vo_ref[0] = v2

    def g_spec(k):
        return pl.BlockSpec((p, tr, cdim), lambda la, i: (0, jnp.where(la == k, i, jnp.where(la < k, 0, ni - 1)), 0))

    spec = pl.BlockSpec((1, tr, cdim), lambda la, i: (la, i, 0))
    return pl.pallas_call(
        body, name=name, grid=(nl, ni),
        in_specs=[g_spec(k) for k in range(nl)] + [spec, spec, spec],
        out_specs=[spec] * 4, out_shape=[jax.ShapeDtypeStruct((nl, r, cdim), F32)] * 4,
        compiler_params=_cp("arbitrary", "arbitrary"),
    )(*g_list, w, m, v)


def _f32_as_rows(a, width):
    return lax.bitcast_convert_type(a.reshape(-1), BF16).reshape(-1, width)


def _rows_as_f32(rows):
    return lax.bitcast_convert_type(rows.reshape(rows.shape[:-2] + (-1, 2)), F32)


def _pad_rows(a, mult):
    r = (-a.shape[-2]) % mult
    if r == 0:
        return a
    widths = [(0, 0)] * (a.ndim - 2) + [(0, r), (0, 0)]
    return jnp.pad(a, widths)


def _pack_flat(parts, lane):
    flat = jnp.concatenate([p.reshape(-1).astype(F32) for p in parts])
    n = flat.shape[0]
    rows = -(-n // lane)
    rows += (-rows) % 8
    return jnp.pad(flat, (0, rows * lane - n)).reshape(rows, lane)


def _unpack_flat(packed, shapes):
    flat = packed.reshape(-1)
    out, off = [], 0
    for s in shapes:
        n = math.prod(s)
        out.append(flat[off:off + n].reshape(s))
        off += n
    return out


def kernel(x, c, ctx, c_ctx, mod_w, mod_b, norm_g, ffn_w_in, ffn_w_out, even_w_in, even_w_out, attn_qk_norm_g, attn_sink, hgrn_out_norm_g, hgrn_lb, odd_w_in, odd_w_out, loss_target, m_c_ctx, m_mod_w, m_mod_b, m_norm_g, m_ffn_w_in, m_ffn_w_out, m_even_w_in, m_even_w_out, m_attn_qk_norm_g, m_attn_sink, m_hgrn_out_norm_g, m_hgrn_lb, m_odd_w_in, m_odd_w_out, v_c_ctx, v_mod_w, v_mod_b, v_norm_g, v_ffn_w_in, v_ffn_w_out, v_even_w_in, v_even_w_out, v_attn_qk_norm_g, v_attn_sink, v_hgrn_out_norm_g, v_hgrn_lb, v_odd_w_in, v_odd_w_out):
    me = _my_index()
    lc, dm = ctx.shape[1], x.shape[2]
    nmod = mod_w.shape[2]
    big = (ffn_w_in, ffn_w_out, even_w_in, even_w_out, odd_w_in, odd_w_out)

    extra = _pad_rows(jnp.concatenate([_f32_as_rows(c, dm), _f32_as_rows(norm_g, dm)], axis=0), 16)
    shards = {"ffn_in0": ffn_w_in[0], "ffn_in1": ffn_w_in[1], "ffn_out0": ffn_w_out[0], "ffn_out1": ffn_w_out[1],
              "even_in": even_w_in[0], "even_out": even_w_out[0], "odd_in": odd_w_in[0], "odd_out": odd_w_out[0]}
    shards = {n: a.astype(BF16) for n, a in shards.items()}
    first = _Exchange(GATHER, [shards["even_in"], shards["even_out"], extra]).call("gather_first")
    w = {"even_in": _full_weight("even_in", first[0]), "even_out": _full_weight("even_out", first[1])}
    c_all = _rows_as_f32(first[2][:, 0:2])
    norm_g_all = _rows_as_f32(first[2][:, 2:3]).reshape(N_DEV, 2, 2, -1)
    norm_g_full = norm_g_all.transpose(1, 2, 0, 3).reshape(2, 2, dm)

    call = jnp.concatenate([c_all, c_ctx[None, :], jnp.zeros((16 - N_DEV - 1, dm), F32)], axis=0)
    bias = lax.dynamic_slice_in_dim(mod_b, me * nmod, nmod, axis=1)[:, None, :]
    m_sh = _mod_fwd(call, mod_w, bias, "mod_fwd")
    m_g = _all_gather(m_sh.reshape(-1, nmod), "gather_mod").reshape(N_DEV, 2, 16, nmod)
    m_all = m_g.transpose(1, 2, 0, 3).reshape(2, 16, -1)
    m_lat = lax.dynamic_index_in_dim(m_all, me, axis=1, keepdims=False)
    mv = jnp.stack([m_all[:, N_DEV], m_lat], axis=1)[:, :, None, :]

    xs = jnp.concatenate([ctx[0], x[0]], axis=0)
    _, dxs, gw, small = _local_step(xs, loss_target[0], mv, norm_g_full, w, attn_qk_norm_g[0], attn_sink[0],
                                    hgrn_out_norm_g, hgrn_lb, lc, shards)
    grad_x = dxs[lc:][None]

    last = _Exchange(SCATTER, [_shard_slots(n, gw[n]) for n in ("even_in", "even_out")]).call("scatter_last")
    gw["even_in"], gw["even_out"] = last
    big_g = ([gw["ffn_in0"], gw["ffn_in1"]], [gw["ffn_out0"], gw["ffn_out1"]], [gw["even_in"]], [gw["even_out"]],
             [gw["odd_in"]], [gw["odd_out"]])
    big_m = (m_ffn_w_in, m_ffn_w_out, m_even_w_in, m_even_w_out, m_odd_w_in, m_odd_w_out)
    big_v = (v_ffn_w_in, v_ffn_w_out, v_even_w_in, v_even_w_out, v_odd_w_in, v_odd_w_out)
    big_names = ("ffn_w_in", "ffn_w_out", "even_w_in", "even_w_out", "odd_w_in", "odd_w_out")
    big_out = [_adam(big_g[i], big[i], big_m[i], big_v[i], "adam_" + big_names[i]) for i in range(6)]
    big_res = [[big_out[i][k] for i in range(6)] for k in range(4)]

    dmv = small["dmv"]
    small_shapes = [(2, 6 * dm), (2, 6 * dm), (2, 2, dm), (2, HEAD_DIM), (ATTN_HEADS,), (HG_D,), (HG_HEADS * HG_D,), (1,)]
    vec = _pack_flat([dmv[:, 0, 0], dmv[:, 1, 0], small["norm_g"], small["qk_g"], small["sink"], small["hg_out_g"],
                      small["lb"], small["loss"]], 128)
    vec_g = _all_gather(vec, "gather_small")
    tot = _unpack_flat(_sum_parts(vec_g, "sum_small"), small_shapes)
    dm_ctx_tot, dm_lat_tot, g_norm_full, g_qk, g_sink, g_hg, dlb_tot, loss_tot = tot
    dm_lat_each = vec_g.reshape(N_DEV, -1)[:, 12 * dm:24 * dm].reshape(N_DEV, 2, 6 * dm)
    dm_lat_mine = lax.dynamic_slice_in_dim(dm_lat_each, me * nmod, nmod, axis=2).transpose(1, 0, 2)
    dm_ctx_mine = lax.dynamic_slice_in_dim(dm_ctx_tot, me * nmod, nmod, axis=1)[:, None, :]
    dm_all = jnp.concatenate([dm_lat_mine, dm_ctx_mine, jnp.zeros((2, 16 - N_DEV - 1, nmod), F32)], axis=1)
    g_mod_w, dcond = _mod_bwd(call, dm_all, mod_w, "mod_bwd")
    dcond_g = _all_gather(dcond[N_DEV:], "gather_dcond")
    g_c_ctx, g_lb, g_mod_b = _small_finish(dcond_g, c_ctx[None, :], dlb_tot[None, :], hgrn_lb, dm_ctx_tot, dm_lat_tot,
                                           "small_finish")
    g_norm = lax.dynamic_slice_in_dim(g_norm_full, me * norm_g.shape[2], norm_g.shape[2], axis=2)

    mod_res = _adam([g_mod_w[0][None], g_mod_w[1][None]], mod_w, m_mod_w, v_mod_w, "adam_mod_w")

    sm_w = (c_ctx, mod_b, norm_g, attn_qk_norm_g, attn_sink, hgrn_out_norm_g, hgrn_lb)
    sm_m = (m_c_ctx, m_mod_b, m_norm_g, m_attn_qk_norm_g, m_attn_sink, m_hgrn_out_norm_g, m_hgrn_lb)
    sm_v = (v_c_ctx, v_mod_b, v_norm_g, v_attn_qk_norm_g, v_attn_sink, v_hgrn_out_norm_g, v_hgrn_lb)
    sm_g = (g_c_ctx, g_mod_b, g_norm, g_qk, g_sink, g_hg, g_lb)
    sm_shapes = [a.shape for a in sm_w]
    sm_out = _adam([_pack_flat(sm_g, 128)[None]],_pack_flat(sm_w, 128)[None], _pack_flat(sm_m, 128)[None],
                   _pack_flat(sm_v, 128)[None], "adam_small")
    sm_res = [_unpack_flat(o, sm_shapes) for o in sm_out]

    def ordered(k):
        s, b = sm_res[k], big_res[k]
        return [s[0], mod_res[k], s[1], s[2], b[0], b[1], b[2], b[3], s[3], s[4], s[5], s[6], b[4], b[5]]

    return (loss_tot[0], grad_x, *ordered(0), *ordered(1), *ordered(2), *ordered(3))
```

```python
import functools
import math

import jax
import jax.numpy as jnp
from jax import lax
from jax.experimental import pallas as pl
from jax.experimental.pallas import tpu as pltpu

F32 = jnp.float32
BF16 = jnp.bfloat16
EPS = 1e-6
N_DEV = 8
MESH = pl.DeviceIdType.MESH

HEAD_DIM = 64
ATTN_HEADS = 8
ATTN_KV = 2
ATTN_BLOCK = 128
WINDOW = 128
GRID_W = 64
HG_HEADS = 4
HG_D = 128
HG_CHUNK = 64
RET_HEADS = 4
RET_DK = 256
RET_DV = 512
RET_CHUNK = 128
NEG = -1e30

ADAM_LR = 0.001
ADAM_B1 = 0.9
ADAM_B2 = 0.999
ADAM_EPS = 1e-08
ADAM_WD = 0.01
ADAM_STEP = 10

VMEM_LIMIT = 60 * 1024 * 1024


def _cp(*sem):
    return pltpu.CompilerParams(dimension_semantics=sem, vmem_limit_bytes=VMEM_LIMIT)


def _nn(a, b):
    return jnp.dot(a, b, preferred_element_type=F32)


def _nt(a, b):
    return lax.dot_general(a, b, (((1,), (1,)), ((), ())), preferred_element_type=F32)


def _tn(a, b):
    return lax.dot_general(a, b, (((0,), (0,)), ((), ())), preferred_element_type=F32)


ACT = BF16


def _bf(a):
    return a.astype(ACT)


def _sig(x):
    return jax.nn.sigmoid(x)


def _split3(x):
    h = x.astype(BF16)
    r = x - h.astype(F32)
    m = r.astype(BF16)
    lo = (r - m.astype(F32)).astype(BF16)
    return h, m, lo


def _nn3(m01, x):
    h, m, lo = _split3(x)
    return _nn(m01, h) + _nn(m01, m) + _nn(m01, lo)


def _nn3r(x, m01):
    h, m, lo = _split3(x)
    return _nn(h, m01) + _nn(m, m01) + _nn(lo, m01)


def _full(shape):
    nd = len(shape)
    return pl.BlockSpec(shape, lambda *a: (0,) * nd, pipeline_mode=pl.Buffered(1))


def _whole(shape):
    nd = len(shape)
    return pl.BlockSpec(shape, lambda *a: (0,) * nd)


def _rows(tm, width):
    return pl.BlockSpec((tm, width), lambda i: (i, 0))


def _ctx_lat(width):
    return pl.BlockSpec((1, 1, width), lambda i: (jnp.minimum(i, 1), 0, 0))


def _acc_ctx_lat(ref, i, val):
    @pl.when(i <= 1)
    def _():
        ref[...] = val.reshape(ref.shape)

    @pl.when(i > 1)
    def _():
        ref[...] += val.reshape(ref.shape)


def _acc_all(ref, i, val):
    @pl.when(i == 0)
    def _():
        ref[...] = val.reshape(ref.shape)

    @pl.when(i > 0)
    def _():
        ref[...] += val.reshape(ref.shape)


def _tile(n, cap):
    best = None
    for t in range(128, min(n, cap) + 1, 128):
        if n % t == 0:
            best = t
    return n if best is None else best


def _norm_mod(xv, g, shift, scale):
    r = lax.rsqrt(jnp.mean(xv * xv, axis=-1, keepdims=True) + EPS)
    xhat = xv * r
    n = xhat * g
    return r, xhat, n, n * (1.0 + scale) + shift


def _norm_mod_bwd(dh, r, xhat, n, g, scale):
    dshift = jnp.sum(dh, axis=0, keepdims=True)
    dscale = jnp.sum(dh * n, axis=0, keepdims=True)
    dn = dh * (1.0 + scale)
    dg = jnp.sum(dn * xhat, axis=0, keepdims=True)
    dxh = dn * g
    dx = r * (dxh - xhat * jnp.mean(dxh * xhat, axis=-1, keepdims=True))
    return dx, dshift, dscale, dg


def _pre_fwd(x, gain, ms, w, splits, tm, name):
    T, dm = x.shape

    def body(x_ref, g_ref, ms_ref, w_ref, *outs):
        ms_v = ms_ref[0]
        h = _norm_mod(x_ref[...], g_ref[...], ms_v[:, :dm], ms_v[:, dm:])[3]
        hb = _bf(h)
        for (s, e), o_ref in zip(splits, outs):
            o_ref[...] = _nn(hb, w_ref[:, s:e])

    return pl.pallas_call(
        body, name=name, grid=(T // tm,),
        in_specs=[_rows(tm, dm), _full((1, dm)), _ctx_lat(2 * dm), _full(w.shape)],
        out_specs=[_rows(tm, e - s) for s, e in splits],
        out_shape=[jax.ShapeDtypeStruct((T, e - s), F32) for s, e in splits],
        compiler_params=_cp("arbitrary"),
    )(x, gain, ms, w)


def _pre_bwd(x, dx_in, gain, ms, w, pieces, tm, name):
    T, dm = x.shape
    n_out = w.shape[1]
    flat = [a for _, arrs in pieces for a in arrs]

    def body(x_ref, dxin_ref, g_ref, ms_ref, w_ref, *rest):
        p_refs = rest[:len(flat)]
        dx_ref, h_ref, dp_ref, dms_ref, dg_ref = rest[len(flat):]
        i = pl.program_id(0)
        ms_v = ms_ref[0]
        g = g_ref[...]
        scale = ms_v[:, dm:]
        r, xhat, n, h = _norm_mod(x_ref[...], g, ms_v[:, :dm], scale)
        h_ref[...] = _bf(h)
        dh = jnp.zeros((tm, dm), F32)
        k = 0
        for s, arrs in pieces:
            v = p_refs[k][...].astype(F32)
            for j in range(1, len(arrs)):
                v = v + p_refs[k + j][...].astype(F32)
            k += len(arrs)
            vb = _bf(v)
            wd = vb.shape[1]
            dp_ref[:, s:s + wd] = vb
            dh = dh + _nt(vb, w_ref[:, s:s + wd])
        dx, dshift, dscale, dg = _norm_mod_bwd(dh, r, xhat, n, g, scale)
        dx_ref[...] = dxin_ref[...] + dx
        _acc_ctx_lat(dms_ref, i, jnp.concatenate([dshift, dscale], axis=1))
        _acc_all(dg_ref, i, dg)

    return pl.pallas_call(
        body, name=name, grid=(T // tm,),
        in_specs=[_rows(tm, dm), _rows(tm, dm), _full((1, dm)), _ctx_lat(2 * dm), _full(w.shape)]
        + [_rows(tm, a.shape[1]) for a in flat],
        out_specs=[_rows(tm, dm), _rows(tm, dm), _rows(tm, n_out), _ctx_lat(2 * dm), _whole((1, dm))],
        out_shape=[jax.ShapeDtypeStruct((T, dm), F32), jax.ShapeDtypeStruct((T, dm), ACT),
                   jax.ShapeDtypeStruct((T, n_out), ACT), jax.ShapeDtypeStruct((2, 1, 2 * dm), F32),
                   jax.ShapeDtypeStruct((1, dm), F32)],
        compiler_params=_cp("arbitrary"),
    )(x, dx_in, gain, ms, w, *flat)


def _ffn_fwd(x1, gain, ms, w_in, w_out, tm, name):
    T, dm = x1.shape
    fh = w_out.shape[0]

    def body(x_ref, g_ref, ms_ref, wi_ref, wo_ref, x2_ref, u_ref, f_ref):
        ms_v = ms_ref[0]
        xv = x_ref[...]
        h = _norm_mod(xv, g_ref[...], ms_v[:, :dm], ms_v[:, dm:2 * dm])[3]
        u = _nn(_bf(h), wi_ref[...])
        u_ref[...] = _bf(u)
        gt = u[:, :fh]
        act = gt * _sig(gt) * u[:, fh:]
        f = _nn(_bf(act), wo_ref[...])
        f_ref[...] = _bf(f)
        x2_ref[...] = xv + ms_v[:, 2 * dm:] * f

    return pl.pallas_call(
        body, name=name, grid=(T // tm,),
        in_specs=[_rows(tm, dm), _full((1, dm)), _ctx_lat(3 * dm), _full(w_in.shape), _full(w_out.shape)],
        out_specs=[_rows(tm, dm), _rows(tm, 2 * fh), _rows(tm, dm)],
        out_shape=[jax.ShapeDtypeStruct((T, dm), F32), jax.ShapeDtypeStruct((T, 2 * fh), ACT),
                   jax.ShapeDtypeStruct((T, dm), ACT)],
        compiler_params=_cp("arbitrary"),
    )(x1, gain, ms, w_in, w_out)


def _ffn_bwd(x1, dx2, u, f, gain, ms, w_in, w_out, tm, name):
    T, dm = x1.shape
    fh = w_out.shape[0]

    def body(x_ref, dx2_ref, u_ref, f_ref, g_ref, ms_ref, wi_ref, wo_ref,
             dx1_ref, h_ref, du_ref, act_ref, df_ref, dms_ref, dg_ref):
        i = pl.program_id(0)
        ms_v = ms_ref[0]
        g = g_ref[...]
        scale = ms_v[:, dm:2 * dm]
        gate = ms_v[:, 2 * dm:]
        r, xhat, n, h = _norm_mod(x_ref[...], g, ms_v[:, :dm], scale)
        h_ref[...] = _bf(h)
        dx2 = dx2_ref[...]
        dgate = jnp.sum(dx2 * f_ref[...].astype(F32), axis=0, keepdims=True)
        dfb = _bf(dx2 * gate)
        df_ref[...] = dfb
        da = _nt(dfb, wo_ref[...])
        uv = u_ref[...].astype(F32)
        gt = uv[:, :fh]
        up = uv[:, fh:]
        s = _sig(gt)
        sg = gt * s
        act_ref[...] = _bf(sg * up)
        dgt = _bf(da * up * (s * (1.0 + gt * (1.0 - s))))
        dup = _bf(da * sg)
        du_ref[:, :fh] = dgt
        du_ref[:, fh:] = dup
        dh = _nt(dgt, wi_ref[:, :fh]) + _nt(dup, wi_ref[:, fh:])
        dx, dshift, dscale, dg = _norm_mod_bwd(dh, r, xhat, n, g, scale)
        dx1_ref[...] = dx2 + dx
        _acc_ctx_lat(dms_ref, i, jnp.concatenate([dshift, dscale, dgate], axis=1))
        _acc_all(dg_ref, i, dg)

    return pl.pallas_call(
        body, name=name, grid=(T // tm,),
        in_specs=[_rows(tm, dm), _rows(tm, dm), _rows(tm, 2 * fh), _rows(tm, dm), _full((1, dm)), _ctx_lat(3 * dm),
                  _full(w_in.shape), _full(w_out.shape)],
        out_specs=[_rows(tm, dm), _rows(tm, dm), _rows(tm, 2 * fh), _rows(tm, fh), _rows(tm, dm),
                   _ctx_lat(3 * dm), _whole((1, dm))],
        out_shape=[jax.ShapeDtypeStruct((T, dm), F32), jax.ShapeDtypeStruct((T, dm), ACT),
                   jax.ShapeDtypeStruct((T, 2 * fh), ACT), jax.ShapeDtypeStruct((T, fh), ACT),
                   jax.ShapeDtypeStruct((T, dm), ACT), jax.ShapeDtypeStruct((2, 1, 3 * dm), F32),
                   jax.ShapeDtypeStruct((1, dm), F32)],
        compiler_params=_cp("arbitrary"),
    )(x1, dx2, u, f, gain, ms, w_in, w_out)


def _wgrad(a, b, name):
    T, K = a.shape
    N = b.shape[1]
    tk, tn, tt = _tile(K, 1024), _tile(N, 1024), _tile(T, 1408)
    nt = T // tt

    def body(a_ref, b_ref, o_ref, acc_ref):
        t = pl.program_id(2)
        part = _tn(a_ref[...], b_ref[...])

        @pl.when(t == 0)
        def _():
            acc_ref[...] = part

        @pl.when(t > 0)
        def _():
            acc_ref[...] += part

        @pl.when(t == nt - 1)
        def _():
            o_ref[...] = acc_ref[...].astype(o_ref.dtype)

    return pl.pallas_call(
        body, name=name, grid=(K // tk, N // tn, nt),
        in_specs=[pl.BlockSpec((tt, tk), lambda i, j, t: (t, i)), pl.BlockSpec((tt, tn), lambda i, j, t: (t, j))],
        out_specs=pl.BlockSpec((tk, tn), lambda i, j, t: (i, j)),
        out_shape=jax.ShapeDtypeStruct((K, N), ACT),
        scratch_shapes=[pltpu.VMEM((tk, tn), F32)],
        compiler_params=_cp("parallel", "parallel", "arbitrary"),
    )(a, b)


def _post_fwd(x, o_fw, o_bw, g_src, g_blk, gain, a, w_out, ms, dvh, tm, name):
    T, dm = x.shape
    hv = o_fw.shape[1]
    aw = 0 if a is None else a.shape[1]
    has_gain = gain is not None

    def body(*refs):
        refs = list(refs)
        x_ref, of_ref, ob_ref, g_ref = refs[:4]
        k = 4
        gain_ref = a_ref = None
        if has_gain:
            gain_ref = refs[k]
            k += 1
        if aw:
            a_ref = refs[k]
            k += 1
        w_ref, ms_ref, x1_ref, z_ref = refs[k:k + 4]
        o = of_ref[...] + ob_ref[...]
        gr = g_ref[...]
        if aw:
            z_ref[:, :aw] = _bf(a_ref[...])
        for hd in range(hv // dvh):
            sl = slice(hd * dvh, (hd + 1) * dvh)
            oh = o[:, sl]
            gh = gr[:, sl]
            r = lax.rsqrt(jnp.mean(oh * oh, axis=-1, keepdims=True) + EPS)
            y = oh * r
            if has_gain:
                y = y * gain_ref[...]
            y = y * (gh * _sig(gh))
            z_ref[:, aw + hd * dvh:aw + (hd + 1) * dvh] = _bf(y)
        yp = _nn(z_ref[...], w_ref[...])
        x1_ref[...] = x_ref[...] + ms_ref[0] * yp

    ins = [x, o_fw, o_bw, g_src]
    specs = [_rows(tm, dm), _rows(tm, hv), _rows(tm, hv), pl.BlockSpec((tm, hv), lambda i: (i, g_blk))]
    if has_gain:
        ins.append(gain)
        specs.append(_full(gain.shape))
    if aw:
        ins.append(a)
        specs.append(_rows(tm, aw))
    ins += [w_out, ms]
    specs += [_full(w_out.shape), _ctx_lat(dm)]
    return pl.pallas_call(
        body, name=name, grid=(T // tm,), in_specs=specs,
        out_specs=[_rows(tm, dm), _rows(tm, aw + hv)],
        out_shape=[jax.ShapeDtypeStruct((T, dm), F32), jax.ShapeDtypeStruct((T, aw + hv), ACT)],
        compiler_params=_cp("arbitrary"),
    )(*ins)


def _post_bwd(dx1, z, o_fw, o_bw, g_src, g_blk, gain, w_out, ms, aw, dvh, tm, name):
    T, dm = dx1.shape
    hv = o_fw.shape[1]
    has_gain = gain is not None

    def body(*refs):
        refs = list(refs)
        dx1_ref, z_ref, of_ref, ob_ref, g_ref = refs[:5]
        k = 5
        gain_ref = None
        if has_gain:
            gain_ref = refs[k]
            k += 1
        w_ref, ms_ref = refs[k:k + 2]
        k += 2
        do_ref, dgr_ref = refs[k:k + 2]
        k += 2
        da_ref = None
        if aw:
            da_ref = refs[k]
            k += 1
        dy_ref, dgate_ref, dgain_ref = refs[k:k + 3]
        i = pl.program_id(0)
        dx1v = dx1_ref[...]
        zb = z_ref[...]
        yp = _nn(zb, w_ref[...])
        _acc_ctx_lat(dgate_ref, i, jnp.sum(dx1v * yp, axis=0, keepdims=True))
        dyb = _bf(dx1v * ms_ref[0])
        dy_ref[...] = dyb
        dz = _nt(dyb, w_ref[...])
        if aw:
            da_ref[...] = dz[:, :aw]
        o = of_ref[...] + ob_ref[...]
        gr = g_ref[...]
        dgain = jnp.zeros((1, dvh), F32)
        for hd in range(hv // dvh):
            sl = slice(hd * dvh, (hd + 1) * dvh)
            oh = o[:, sl]
            gh = gr[:, sl]
            dyh = dz[:, aw + hd * dvh:aw + (hd + 1) * dvh]
            r = lax.rsqrt(jnp.mean(oh * oh, axis=-1, keepdims=True) + EPS)
            n = oh * r
            s = _sig(gh)
            sl_g = gh * s
            gn = gain_ref[...] if has_gain else 1.0
            dgr_ref[:, sl] = dyh * n * gn * (s * (1.0 + gh * (1.0 - s)))
            dn = dyh * gn * sl_g
            dgain = dgain + jnp.sum(dyh * n * sl_g, axis=0, keepdims=True)
            do_ref[:, sl] = r * (dn - n * jnp.mean(dn * n, axis=-1, keepdims=True))
        _acc_all(dgain_ref, i, dgain)

    ins = [dx1, z, o_fw, o_bw, g_src]
    specs = [_rows(tm, dm), _rows(tm, aw + hv), _rows(tm, hv), _rows(tm, hv),
             pl.BlockSpec((tm, hv), lambda i: (i, g_blk))]
    if has_gain:
        ins.append(gain)
        specs.append(_full(gain.shape))
    ins += [w_out, ms]
    specs += [_full(w_out.shape), _ctx_lat(dm)]
    out_specs = [_rows(tm, hv), _rows(tm, hv)]
    out_shape = [jax.ShapeDtypeStruct((T, hv), F32), jax.ShapeDtypeStruct((T, hv), F32)]
    if aw:
        out_specs.append(_rows(tm, aw))
        out_shape.append(jax.ShapeDtypeStruct((T, aw), F32))
    out_specs += [_rows(tm, dm), _ctx_lat(dm), _whole((1, dvh))]
    out_shape += [jax.ShapeDtypeStruct((T, dm), ACT), jax.ShapeDtypeStruct((2, 1, dm), F32),
                  jax.ShapeDtypeStruct((1, dvh), F32)]
    return pl.pallas_call(
        body, name=name, grid=(T // tm,), in_specs=specs, out_specs=out_specs, out_shape=out_shape,
        compiler_params=_cp("arbitrary"),
    )(*ins)


def _loss_bwd(x, target, tm, name):
    T, dm = x.shape

    def body(x_ref, t_ref, dx_ref, loss_ref):
        i = pl.program_id(0)

        @pl.when(i == 0)
        def _():
            dx_ref[...] = jnp.zeros_like(dx_ref)
            loss_ref[...] = jnp.zeros_like(loss_ref)

        @pl.when(i > 0)
        def _():
            e = x_ref[...] - t_ref[...]
            dx_ref[...] = e * (1.0 / dm)
            loss_ref[...] += jnp.sum(e * e) * (0.5 / dm)

    return pl.pallas_call(
        body, name=name, grid=(T // tm,),
        in_specs=[_rows(tm, dm), pl.BlockSpec((tm, dm), lambda i: (jnp.maximum(i - 1, 0), 0))],
        out_specs=[_rows(tm, dm), _whole((1, 1))],
        out_shape=[jax.ShapeDtypeStruct((T, dm), F32), jax.ShapeDtypeStruct((1, 1), F32)],
        compiler_params=_cp("arbitrary"),
    )(x, target)


def _swap_matrix():
    r = lax.broadcasted_iota(jnp.int32, (HEAD_DIM, HEAD_DIM), 0)
    c = lax.broadcasted_iota(jnp.int32, (HEAD_DIM, HEAD_DIM), 1)
    return jnp.where((r + HEAD_DIM // 2) % HEAD_DIM == c, 1.0, 0.0).astype(BF16)


def _qk_prep_fwd(raw, gains, cos2, sin2, tq, name):
    nh, T, hd = raw.shape

    def body(x_ref, g_ref, c_ref, s_ref, o_ref):
        hidx = pl.program_id(0)
        xv = x_ref[0]
        r = lax.rsqrt(jnp.mean(xv * xv, axis=-1, keepdims=True) + EPS)
        n = xv * r * g_ref[0]
        y = n * c_ref[...] + _nn3r(n, _swap_matrix()) * s_ref[...]
        sc = jnp.where(hidx < ATTN_HEADS, HEAD_DIM ** -0.5, 1.0)
        o_ref[0] = _bf(y * sc)

    return pl.pallas_call(
        body, name=name, grid=(nh, T // tq),
        in_specs=[pl.BlockSpec((1, tq, hd), lambda h, i: (h, i, 0)), pl.BlockSpec((1, 1, hd), lambda h, i: (h, 0, 0)),
                  pl.BlockSpec((tq, hd), lambda h, i: (i, 0)), pl.BlockSpec((tq, hd), lambda h, i: (i, 0))],
        out_specs=pl.BlockSpec((1, tq, hd), lambda h, i: (h, i, 0)),
        out_shape=jax.ShapeDtypeStruct((nh, T, hd), ACT),
        compiler_params=_cp("arbitrary", "arbitrary"),
    )(raw, gains, cos2, sin2)


def _qk_prep_bwd(dy, raw, gains, cos2, sin2, tq, name):
    nh, T, hd = raw.shape

    def body(dy_ref, x_ref, g_ref, c_ref, s_ref, dx_ref, dg_ref):
        hidx = pl.program_id(0)
        i = pl.program_id(1)
        xv = x_ref[0]
        g = g_ref[0]
        r = lax.rsqrt(jnp.mean(xv * xv, axis=-1, keepdims=True) + EPS)
        xhat = xv * r
        sc = jnp.where(hidx < ATTN_HEADS, HEAD_DIM ** -0.5, 1.0)
        dyv = dy_ref[0] * sc
        dn = dyv * c_ref[...] + _nn3r(dyv * s_ref[...], _swap_matrix())
        _acc_all(dg_ref, i, jnp.sum(dn * xhat, axis=0, keepdims=True))
        dxh = dn * g
        dx_ref[0] = r * (dxh - xhat * jnp.mean(dxh * xhat, axis=-1, keepdims=True))

    return pl.pallas_call(
        body, name=name, grid=(nh, T // tq),
        in_specs=[pl.BlockSpec((1, tq, hd), lambda h, i: (h, i, 0)), pl.BlockSpec((1, tq, hd), lambda h, i: (h, i, 0)),
                  pl.BlockSpec((1, 1, hd), lambda h, i: (h, 0, 0)),
                  pl.BlockSpec((tq, hd), lambda h, i: (i, 0)), pl.BlockSpec((tq, hd), lambda h, i: (i, 0))],
        out_specs=[pl.BlockSpec((1, tq, hd), lambda h, i: (h, i, 0)), pl.BlockSpec((1, 1, hd), lambda h, i: (h, 0, 0))],
        out_shape=[jax.ShapeDtypeStruct((nh, T, hd), F32), jax.ShapeDtypeStruct((nh, 1, hd), F32)],
        compiler_params=_cp("arbitrary", "arbitrary"),
    )(dy, raw, gains, cos2, sin2)


def _attn_scores(q, k_ref, i, lc, T, sink):
    blk = ATTN_BLOCK
    kc = k_ref[0, pl.ds(blk, lc), :]
    kw = k_ref[0, pl.ds(pl.multiple_of(i * blk, blk), 3 * blk), :]
    s_c = _nt(q, kc)
    s_w = _nt(q, kw)
    row = lax.broadcasted_iota(jnp.int32, (4 * blk, 1), 0)
    qpos = i * blk + (row & (blk - 1))
    kpos = (i - 1) * blk + lax.broadcasted_iota(jnp.int32, (1, 3 * blk), 1)
    valid = (qpos >= lc) & (kpos >= lc) & (kpos < T) & (jnp.abs(kpos - qpos) <= WINDOW)
    s_w = jnp.where(valid, s_w, NEG)
    return kc, kw, s_c, s_w


def _attn_fwd(qt, kp, vp, sinkb, lc, name, ex=None):
    nh, T, hd = qt.shape
    blk = ATTN_BLOCK
    g = nh // ATTN_KV

    def body(q_ref, k_ref, v_ref, sink_ref, o_ref, lse_ref):
        i = pl.program_id(1)
        q = q_ref[...].reshape(g * blk, hd)
        sink = sink_ref[0]
        kc, kw, s_c, s_w = _attn_scores(q, k_ref, i, lc, T, sink)
        m = jnp.maximum(jnp.maximum(jnp.max(s_c, axis=-1, keepdims=True), jnp.max(s_w, axis=-1, keepdims=True)), sink)
        e_c = jnp.exp(s_c - m)
        e_w = jnp.exp(s_w - m)
        den = jnp.exp(sink - m) + jnp.sum(e_c, axis=-1, keepdims=True) + jnp.sum(e_w, axis=-1, keepdims=True)
        inv = 1.0 / den
        vc = v_ref[0, pl.ds(blk, lc), :]
        vw = v_ref[0, pl.ds(pl.multiple_of(i * blk, blk), 3 * blk), :]
        o = _nn(_bf(e_c * inv), vc) + _nn(_bf(e_w * inv), vw)
        o_ref[...] = o.reshape(g, blk, hd)
        lse_ref[...] = (m + jnp.log(den)).reshape(g, blk, 1)

    nb = T // blk
    return _host_call(
        body, ex, lambda: (pl.program_id(0) == 0) & (pl.program_id(1) == 0),
        lambda: (pl.program_id(0) == ATTN_KV - 1) & (pl.program_id(1) == nb - 1),
        name=name, grid=(ATTN_KV, nb),
        in_specs=[pl.BlockSpec((g, blk, hd), lambda kv, i: (kv, i, 0)),
                  pl.BlockSpec((1, T + 2 * blk, hd), lambda kv, i: (kv, 0, 0)),
                  pl.BlockSpec((1, T + 2 * blk, hd), lambda kv, i: (kv, 0, 0)),
                  pl.BlockSpec((1, g * blk, 1), lambda kv, i: (kv, 0, 0))],
        out_specs=[pl.BlockSpec((g, blk, hd), lambda kv, i: (kv, i, 0)),
                   pl.BlockSpec((g, blk, 1), lambda kv, i: (kv, i, 0))],
        out_shape=[jax.ShapeDtypeStruct((nh, T, hd), F32), jax.ShapeDtypeStruct((nh, T, 1), F32)],
        scratch_shapes=[], sem=("arbitrary", "arbitrary"), args=(qt, kp, vp, sinkb))


def _attn_bwd(qt, kp, vp, sinkb, o, lse, do, lc, name):
    nh, T, hd = qt.shape
    blk = ATTN_BLOCK
    g = nh // ATTN_KV

    def body(q_ref, k_ref, v_ref, sink_ref, o_ref, lse_ref, do_ref, dq_ref, dk_ref, dv_ref, ds_ref):
        i = pl.program_id(1)

        @pl.when(i == 0)
        def _():
            dk_ref[...] = jnp.zeros_like(dk_ref)
            dv_ref[...] = jnp.zeros_like(dv_ref)
            ds_ref[...] = jnp.zeros_like(ds_ref)

        q = q_ref[...].reshape(g * blk, hd)
        sink = sink_ref[0]
        lse = lse_ref[...].reshape(g * blk, 1)
        dov = do_ref[...].reshape(g * blk, hd)
        delta = jnp.sum(dov * o_ref[...].reshape(g * blk, hd), axis=-1, keepdims=True)
        kc, kw, s_c, s_w = _attn_scores(q, k_ref, i, lc, T, sink)
        p_c = jnp.exp(s_c - lse)
        p_w = jnp.exp(s_w - lse)
        win = pl.ds(pl.multiple_of(i * blk, blk), 3 * blk)
        vc = v_ref[0, pl.ds(blk, lc), :]
        vw = v_ref[0, win, :]
        dob = _bf(dov)
        ds_c = _bf(p_c * (_nt(dob, vc) - delta))
        ds_w = _bf(p_w * (_nt(dob, vw) - delta))
        dsr = -jnp.exp(sink - lse) * delta
        for hh in range(g):
            ds_ref[0, hh:hh + 1, :] += jnp.sum(dsr[hh * blk:(hh + 1) * blk, :], axis=0, keepdims=True)
        dq_ref[...] = (_nn(ds_c, kc) + _nn(ds_w, kw)).reshape(g, blk, hd)
        dk_ref[0, pl.ds(blk, lc), :] += _tn(ds_c, q)
        dk_ref[0, win, :] += _tn(ds_w, q)
        dv_ref[0, pl.ds(blk, lc), :] += _tn(_bf(p_c), dob)
        dv_ref[0, win, :] += _tn(_bf(p_w), dob)

    qspec = pl.BlockSpec((g, blk, hd), lambda kv, i: (kv, i, 0))
    kspec = pl.BlockSpec((1, T + 2 * blk, hd), lambda kv, i: (kv, 0, 0))
    lspec = pl.BlockSpec((g, blk, 1), lambda kv, i: (kv, i, 0))
    return pl.pallas_call(
        body, name=name, grid=(ATTN_KV, T // blk),
        in_specs=[qspec, kspec, kspec, pl.BlockSpec((1, g * blk, 1), lambda kv, i: (kv, 0, 0)), qspec, lspec, qspec],
        out_specs=[qspec, kspec, kspec, pl.BlockSpec((1, g, 1), lambda kv, i: (kv, 0, 0))],
        out_shape=[jax.ShapeDtypeStruct((nh, T, hd), F32), jax.ShapeDtypeStruct((ATTN_KV, T + 2 * blk, hd), F32),
                   jax.ShapeDtypeStruct((ATTN_KV, T + 2 * blk, hd), F32), jax.ShapeDtypeStruct((ATTN_KV, g, 1), F32)],
        compiler_params=_cp("arbitrary", "arbitrary"),
    )(qt, kp, vp, sinkb, o, lse, do)


PAIR = 2 * HEAD_DIM
N_PAIRS = (ATTN_HEADS + ATTN_KV) // 2


def _lanes():
    return lax.broadcasted_iota(jnp.int32, (1, PAIR), 1)


def _swap32(v):
    first_half = (_lanes() & (HEAD_DIM // 2)) == 0
    return jnp.where(first_half, pltpu.roll(v, PAIR - HEAD_DIM // 2, 1), pltpu.roll(v, HEAD_DIM // 2, 1))


def _head_mean(v):
    r = lax.broadcasted_iota(jnp.int32, (PAIR, PAIR), 0)
    c = lax.broadcasted_iota(jnp.int32, (PAIR, PAIR), 1)
    same = jnp.where((r >= HEAD_DIM) == (c >= HEAD_DIM), 1.0, 0.0).astype(BF16)
    return _nn3r(v, same) * (1.0 / HEAD_DIM)


def _qk_slab_fwd(pa, gains, cosp, sinp, tm, name):
    T = pa.shape[0]
    qw = ATTN_HEADS * HEAD_DIM

    def body(pa_ref, g_ref, c_ref, s_ref, q_ref, k_ref, v_ref):
        cosv, sinv = c_ref[...], s_ref[...]
        for p in range(N_PAIRS):
            xv = pa_ref[:, p * PAIR:(p + 1) * PAIR]
            n = xv * lax.rsqrt(_head_mean(xv * xv) + EPS) * g_ref[p]
            y = n * cosv + _swap32(n) * sinv
            if p < N_PAIRS - 1:
                q_ref[:, p * PAIR:(p + 1) * PAIR] = _bf(y * HEAD_DIM ** -0.5)
            else:
                k_ref[...] = _bf(y)
        v_ref[...] = _bf(pa_ref[:, qw + PAIR:])

    return pl.pallas_call(
        body, name=name, grid=(T // tm,),
        in_specs=[_rows(tm, pa.shape[1]), _full(gains.shape), _rows(tm, PAIR), _rows(tm, PAIR)],
        out_specs=[_rows(tm, qw), _rows(tm, PAIR), _rows(tm, PAIR)],
        out_shape=[jax.ShapeDtypeStruct((T, qw), ACT), jax.ShapeDtypeStruct((T, PAIR), ACT),
                   jax.ShapeDtypeStruct((T, PAIR), ACT)],
        compiler_params=_cp("arbitrary"),
    )(pa, gains, cosp, sinp)


def _qk_slab_bwd(dq, dk, pa, gains, cosp, sinp, tm, name):
    T = pa.shape[0]
    qw = ATTN_HEADS * HEAD_DIM

    def body(dq_ref, dk_ref, pa_ref, g_ref, c_ref, s_ref, dx_ref, dg_ref):
        i = pl.program_id(0)
        cosv, sinv = c_ref[...], s_ref[...]
        for p in range(N_PAIRS):
            sl = slice(p * PAIR, (p + 1) * PAIR)
            xv = pa_ref[:, sl]
            r = lax.rsqrt(_head_mean(xv * xv) + EPS)
            xhat = xv * r
            dy = dq_ref[:, sl] * HEAD_DIM ** -0.5 if p < N_PAIRS - 1 else dk_ref[...]
            dn = dy * cosv + _swap32(dy * sinv)
            _acc_all(dg_ref.at[p], i, jnp.sum(dn * xhat, axis=0, keepdims=True))
            dxh = dn * g_ref[p]
            dx_ref[:, sl] = r * (dxh - xhat * _head_mean(dxh * xhat))

    return pl.pallas_call(
        body, name=name, grid=(T // tm,),
        in_specs=[_rows(tm, qw), _rows(tm, PAIR), _rows(tm, qw + PAIR), _full(gains.shape), _rows(tm, PAIR), _rows(tm, PAIR)],
        out_specs=[_rows(tm, qw + PAIR), _whole(gains.shape)],
        out_shape=[jax.ShapeDtypeStruct((T, qw + PAIR), F32), jax.ShapeDtypeStruct(gains.shape, F32)],
        compiler_params=_cp("arbitrary"),
    )(dq, dk, pa, gains, cosp, sinp)


def _attn_window(ref, i, nb):
    blk = ATTN_BLOCK
    starts = [pl.multiple_of(jnp.clip(i + d, 0, nb - 1) * blk, blk) for d in (-1, 0, 1)]
    return starts, jnp.concatenate([ref[pl.ds(s, blk), :] for s in starts], axis=0)


def _attn_mask(i, lc, T):
    blk = ATTN_BLOCK
    row = lax.broadcasted_iota(jnp.int32, (4 * blk, 1), 0)
    qpos = i * blk + (row & (blk - 1))
    kpos = (i - 1) * blk + lax.broadcasted_iota(jnp.int32, (1, 3 * blk), 1)
    return (qpos >= lc) & (kpos >= lc) & (kpos < T) & (jnp.abs(kpos - qpos) <= WINDOW)


def _to_kv_half(v, head, kv):
    return v if head % 2 == kv else pltpu.roll(v, HEAD_DIM, 1)


def _attn_slab_fwd(qt, ks, vs, sinkb, lc, name, ex=None):
    T = qt.shape[0]
    blk = ATTN_BLOCK
    nb = T // blk
    g = ATTN_HEADS // ATTN_KV

    def body(q_ref, k_ref, v_ref, sink_ref, o_ref, lse_ref):
        i = pl.program_id(0)
        lane = _lanes()
        valid = _attn_mask(i, lc, T)
        kc_all, vc = k_ref[0:lc, :], v_ref[0:lc, :]
        _, kw_all = _attn_window(k_ref, i, nb)
        _, vw = _attn_window(v_ref, i, nb)
        placed = [None] * ATTN_HEADS
        for kv in range(ATTN_KV):
            mine = (lane >= kv * HEAD_DIM) & (lane < (kv + 1) * HEAD_DIM)
            kc = jnp.where(mine, kc_all, jnp.zeros_like(kc_all))
            kw = jnp.where(mine, kw_all, jnp.zeros_like(kw_all))
            heads = [kv * g + j for j in range(g)]
            q4 = jnp.concatenate([_to_kv_half(q_ref[:, (h // 2) * PAIR:(h // 2 + 1) * PAIR], h, kv) for h in heads], axis=0)
            sink = sink_ref[kv]
            s_c = _nt(q4, kc)
            s_w = jnp.where(valid, _nt(q4, kw), NEG)
            m = jnp.maximum(jnp.maximum(jnp.max(s_c, axis=-1, keepdims=True), jnp.max(s_w, axis=-1, keepdims=True)), sink)
            e_c = jnp.exp(s_c - m)
            e_w = jnp.exp(s_w - m)
            den = jnp.exp(sink - m) + jnp.sum(e_c, axis=-1, keepdims=True) + jnp.sum(e_w, axis=-1, keepdims=True)
            inv = 1.0 / den
            o4 = _nn(_bf(e_c * inv), vc) + _nn(_bf(e_w * inv), vw)
            lse_ref[kv * g:(kv + 1) * g] = (m + jnp.log(den)).reshape(g, blk, 1)
            for j, h in enumerate(heads):
                placed[h] = _to_kv_half(o4[j * blk:(j + 1) * blk], h, kv)
        for p in range(ATTN_HEADS // 2):
            o_ref[:, p * PAIR:(p + 1) * PAIR] = jnp.where(lane < HEAD_DIM, placed[2 * p], placed[2 * p + 1])

    qw = ATTN_HEADS * HEAD_DIM
    return _host_call(
        body, ex, lambda: pl.program_id(0) == 0, lambda: pl.program_id(0) == nb - 1,
        name=name, grid=(nb,),
        in_specs=[_rows(blk, qw), _full((T, PAIR)), _full((T, PAIR)), _full(sinkb.shape)],
        out_specs=[_rows(blk, qw), pl.BlockSpec((ATTN_HEADS, blk, 1), lambda i: (0, i, 0))],
        out_shape=[jax.ShapeDtypeStruct((T, qw), F32), jax.ShapeDtypeStruct((ATTN_HEADS, T, 1), F32)],
        scratch_shapes=[], sem=("arbitrary",), args=(qt, ks, vs, sinkb))


def _attn_slab_bwd(qt, ks, vs, sinkb, o, lse, do, lc, name):
    T = qt.shape[0]
    blk = ATTN_BLOCK
    nb = T // blk
    g = ATTN_HEADS // ATTN_KV

    def body(q_ref, k_ref, v_ref, sink_ref, o_ref, lse_ref, do_ref, dq_ref, dk_ref, dv_ref, ds_ref):
        i = pl.program_id(0)

        @pl.when(i == 0)
        def _():
            dk_ref[...] = jnp.zeros_like(dk_ref)
            dv_ref[...] = jnp.zeros_like(dv_ref)
            ds_ref[...] = jnp.zeros_like(ds_ref)

        lane = _lanes()
        valid = _attn_mask(i, lc, T)
        kc_all, vc_all = k_ref[0:lc, :], v_ref[0:lc, :]
        starts, kw_all = _attn_window(k_ref, i, nb)
        _, vw_all = _attn_window(v_ref, i, nb)
        dq_pairs = [jnp.zeros((blk, PAIR), F32) for _ in range(ATTN_HEADS // 2)]
        for kv in range(ATTN_KV):
            mine = (lane >= kv * HEAD_DIM) & (lane < (kv + 1) * HEAD_DIM)

            def only(v):
                return jnp.where(mine, v, jnp.zeros_like(v))

            kc, kw, vc, vw = only(kc_all), only(kw_all), only(vc_all), only(vw_all)
            heads = [kv * g + j for j in range(g)]
            qs, dos, deltas = [], [], []
            for h in heads:
                sl = slice((h // 2) * PAIR, (h // 2 + 1) * PAIR)
                dov = do_ref[:, sl]
                qs.append(_to_kv_half(q_ref[:, sl], h, kv))
                dos.append(_bf(_to_kv_half(dov, h, kv)))
                own = (lane < HEAD_DIM) if h % 2 == 0 else (lane >= HEAD_DIM)
                deltas.append(jnp.sum(jnp.where(own, dov * o_ref[:, sl], 0.0), axis=-1, keepdims=True))
            q4, do4, delta = jnp.concatenate(qs, axis=0), jnp.concatenate(dos, axis=0), jnp.concatenate(deltas, axis=0)
            sink = sink_ref[kv]
            lse = lse_ref[kv * g:(kv + 1) * g].reshape(g * blk, 1)
            p_c = jnp.exp(_nt(q4, kc) - lse)
            p_w = jnp.exp(jnp.where(valid, _nt(q4, kw), NEG) - lse)
            ds_c = _bf(p_c * (_nt(do4, vc) - delta))
            ds_w = _bf(p_w * (_nt(do4, vw) - delta))
            dsr = -jnp.exp(sink - lse) * delta
            dq4 = _nn(ds_c, kc) + _nn(ds_w, kw)
            for j, h in enumerate(heads):
                ds_ref[h:h + 1, :] += jnp.sum(dsr[j * blk:(j + 1) * blk, :], axis=0, keepdims=True)
                dq_pairs[h // 2] = dq_pairs[h // 2] + _to_kv_half(dq4[j * blk:(j + 1) * blk], h, kv)
            dk_ref[0:lc, :] += only(_tn(ds_c, q4))
            dv_ref[0:lc, :] += only(_tn(_bf(p_c), do4))
            dkw = only(_tn(ds_w, q4))
            dvw = only(_tn(_bf(p_w), do4))
            for b, s in enumerate(starts):
                dk_ref[pl.ds(s, blk), :] += dkw[b * blk:(b + 1) * blk]
                dv_ref[pl.ds(s, blk), :] += dvw[b * blk:(b + 1) * blk]
        for p in range(ATTN_HEADS // 2):
            dq_ref[:, p * PAIR:(p + 1) * PAIR] = dq_pairs[p]

    qw = ATTN_HEADS * HEAD_DIM
    lspec = pl.BlockSpec((ATTN_HEADS, blk, 1), lambda i: (0, i, 0))
    return pl.pallas_call(
        body, name=name, grid=(nb,),
        in_specs=[_rows(blk, qw), _full((T, PAIR)), _full((T, PAIR)), _full(sinkb.shape), _rows(blk, qw), lspec,
                  _rows(blk, qw)],
        out_specs=[_rows(blk, qw), _whole((T, PAIR)), _whole((T, PAIR)), _whole((ATTN_HEADS, 1))],
        out_shape=[jax.ShapeDtypeStruct((T, qw), F32), jax.ShapeDtypeStruct((T, PAIR), F32),
                   jax.ShapeDtypeStruct((T, PAIR), F32), jax.ShapeDtypeStruct((ATTN_HEADS, 1), F32)],
        compiler_params=_cp("arbitrary"),
    )(qt, ks, vs, sinkb, o, lse, do)


def _fw_chunk(s, nc, nt):
    return s


def _bw_chunk(s, nc, nt):
    return jnp.where(s < nc, nc - 1 - s, nt - 1 - (s - nc))


def _tri(c, rev):
    r = lax.broadcasted_iota(jnp.int32, (c, c), 0)
    k = lax.broadcasted_iota(jnp.int32, (c, c), 1)
    return (k >= r) if rev else (k <= r)


def _gla_gates(z, lb, rev):
    c = HG_CHUNK
    sg = _sig(z)
    f = lb + (1.0 - lb) * sg
    cum = _nn3(jnp.where(_tri(c, rev), 1.0, 0.0).astype(BF16), jnp.log(f))
    mid = c - 1 - c // 2 if rev else c // 2
    last = 0 if rev else c - 1
    return sg, f, cum, cum[mid:mid + 1], cum[last:last + 1], last


def _lower_bound(lbraw_ref):
    lr = lbraw_ref[...]
    return _sig(lr[0:1] - lr[1:2])


def _gla_fwd(pb, lbraw, lc, name, ex=None):
    T = pb.shape[0]
    c, hw, d = HG_CHUNK, HG_HEADS * HG_D, HG_D
    nt, nc = T // c, lc // c
    orders = (_fw_chunk, _bw_chunk)

    def body(qf, zf, vf, qb, zb, vb, lb_ref, of_ref, ob_ref, sf_ref, sb_ref, st_ref):
        @pl.when(pl.program_id(0) == 0)
        def _():
            st_ref[...] = jnp.zeros_like(st_ref)

        lb_all = _lower_bound(lb_ref)
        for dr, (q_ref, z_ref, v_ref, o_ref, s_ref) in enumerate(((qf, zf, vf, of_ref, sf_ref), (qb, zb, vb, ob_ref, sb_ref))):
            rev = dr == 1
            mask = _tri(c, rev)
            for h in range(HG_HEADS):
                sl = slice(h * d, (h + 1) * d)
                qr = q_ref[:, sl]
                q = qr * _sig(qr)
                v = _bf(v_ref[:, sl])
                _, f, cum, ref, last, _ = _gla_gates(z_ref[:, sl], lb_all[:, sl], rev)
                k = 1.0 - f
                a = jnp.where(mask, _nt(_bf(q * jnp.exp(cum - ref)), _bf(k * jnp.exp(ref - cum))), 0.0)
                st = st_ref[dr, h]
                stb = _bf(st)
                s_ref[0, h] = stb
                o_ref[:, sl] = _nn(_bf(a), v) + _nt(_bf(q * jnp.exp(cum)), stb)
                st_ref[dr, h] = st * jnp.exp(last) + _tn(v, _bf(k * jnp.exp(last - cum)))

    def col(order, blkcol):
        return pl.BlockSpec((c, hw), lambda s: (order(s, nc, nt), blkcol))

    def st_spec(order):
        return pl.BlockSpec((1, HG_HEADS, d, d), lambda s: (order(s, nc, nt), 0, 0, 0))

    in_specs = []
    for dr, order in enumerate(orders):
        in_specs += [col(order, 0), col(order, 1 + dr), col(order, 3)]
    in_specs.append(_full(lbraw.shape))
    return _host_call(
        body, ex, lambda: pl.program_id(0) == 0, lambda: pl.program_id(0) == nt - 1,
        name=name, grid=(nt,), in_specs=in_specs,
        out_specs=[col(_fw_chunk, 0), col(_bw_chunk, 0), st_spec(_fw_chunk), st_spec(_bw_chunk)],
        out_shape=[jax.ShapeDtypeStruct((T, hw), F32), jax.ShapeDtypeStruct((T, hw), F32),
                   jax.ShapeDtypeStruct((nt, HG_HEADS, d, d), ACT), jax.ShapeDtypeStruct((nt, HG_HEADS, d, d), ACT)],
        scratch_shapes=[pltpu.VMEM((2, HG_HEADS, d, d), F32)], sem=("arbitrary",),
        args=(pb, pb, pb, pb, pb, pb, lbraw))


def _gla_bwd(pb, lbraw, s_fw, s_bw, do, lc, name, ex=None):
    T = pb.shape[0]
    c, hw, d = HG_CHUNK, HG_HEADS * HG_D, HG_D
    nt, nc = T // c, lc // c

    def rfw(s, nc_, nt_):
        return _fw_chunk(nt_ - 1 - s, nc_, nt_)

    def rbw(s, nc_, nt_):
        return _bw_chunk(nt_ - 1 - s, nc_, nt_)

    def body(qf, zf, vf, sf, dof, qb, zb, vb, sb, dob_, lb_ref,
             dqf, dzf, dvf, dqb, dzb, dvb, dlb_ref, dst_ref):
        step = pl.program_id(0)

        @pl.when(step == 0)
        def _():
            dst_ref[...] = jnp.zeros_like(dst_ref)

        lb_all = _lower_bound(lb_ref)
        dlb_parts = []
        sets = ((qf, zf, vf, sf, dof, dqf, dzf, dvf), (qb, zb, vb, sb, dob_, dqb, dzb, dvb))
        for dr, (q_ref, z_ref, v_ref, s_ref, do_ref, dq_ref, dz_ref, dv_ref) in enumerate(sets):
            rev = dr == 1
            mask = _tri(c, rev)
            acc_t = jnp.where(_tri(c, not rev), 1.0, 0.0).astype(BF16)
            dlb_heads = []
            for h in range(HG_HEADS):
                sl = slice(h * d, (h + 1) * d)
                lb = lb_all[:, sl]
                qr = q_ref[:, sl]
                sq = _sig(qr)
                q = qr * sq
                vbf = _bf(v_ref[:, sl])
                sg, f, cum, ref, last, last_row = _gla_gates(z_ref[:, sl], lb, rev)
                k = 1.0 - f
                e_qr = jnp.exp(cum - ref)
                e_kr = jnp.exp(ref - cum)
                e_q = jnp.exp(cum)
                e_kl = jnp.exp(last - cum)
                el = jnp.exp(last)
                q1 = q * e_qr
                k1 = k * e_kr
                q2 = q * e_q
                k2 = k * e_kl
                q1b, k1b, q2b, k2b = _bf(q1), _bf(k1), _bf(q2), _bf(k2)
                a = jnp.where(mask, _nt(q1b, k1b), 0.0)
                dob = _bf(do_ref[:, sl])
                stb = s_ref[0, h]
                dst = dst_ref[dr, h]
                dstb = _bf(dst)
                da = _bf(jnp.where(mask, _nt(dob, vbf), 0.0))
                dv_ref[:, sl] = _tn(_bf(a), dob) + _nt(k2b, dstb)
                dq1 = _nn(da, k1b)
                dk1 = _tn(da, q1b)
                dq2 = _nn(dob, stb)
                dk2 = _nn(vbf, dstb)
                dst_ref[dr, h] = _tn(dob, q2b) + dst * el
                dq = dq1 * e_qr + dq2 * e_q
                dk = dk1 * e_kr + dk2 * e_kl
                dcum = dq1 * q1 - dk1 * k1 + dq2 * q2 - dk2 * k2
                dlast = (jnp.sum(dk2 * k2, axis=0, keepdims=True)
                         + jnp.sum(dst * stb.astype(F32), axis=0, keepdims=True) * el)
                rowid = lax.broadcasted_iota(jnp.int32, (c, 1), 0)
                dcum = dcum + jnp.where(rowid == last_row, dlast, 0.0)
                dlf = _nn3(acc_t, dcum)
                df = dlf / f - dk
                dz_ref[:, sl] = df * (1.0 - lb) * sg * (1.0 - sg)
                dlb_heads.append(jnp.sum(df * (1.0 - sg), axis=0, keepdims=True))
                dq_ref[:, sl] = dq * (sq * (1.0 + qr * (1.0 - sq)))
            dlb_parts.append(jnp.concatenate(dlb_heads, axis=1))
        _acc_all(dlb_ref, step, dlb_parts[0] + dlb_parts[1])

    def col(order, blkcol):
        return pl.BlockSpec((c, hw), lambda s: (order(s, nc, nt), blkcol))

    def st_spec(order):
        return pl.BlockSpec((1, HG_HEADS, d, d), lambda s: (order(s, nc, nt), 0, 0, 0))

    in_specs = []
    for dr, order in enumerate((rfw, rbw)):
        in_specs += [col(order, 0), col(order, 1 + dr), col(order, 3), st_spec(order), col(order, 0)]
    in_specs.append(_full(lbraw.shape))
    out_specs = [col(rfw, 0)] * 3 + [col(rbw, 0)] * 3 + [_whole((1, hw))]
    out_shape = [jax.ShapeDtypeStruct((T, hw), F32)] * 6 + [jax.ShapeDtypeStruct((1, hw), F32)]
    return _host_call(
        body, ex, lambda: pl.program_id(0) == 0, lambda: pl.program_id(0) == nt - 1,
        name=name, grid=(nt,), in_specs=in_specs, out_specs=out_specs, out_shape=out_shape,
        scratch_shapes=[pltpu.VMEM((2, HG_HEADS, d, d), F32)], sem=("arbitrary",),
        args=(pb, pb, pb, s_fw, do, pb, pb, pb, s_bw, do, lbraw))


def _ret_log_gamma(h, rev):
    hh = RET_HEADS - 1 - h if rev else h
    return math.log(1.0 - 2.0 ** (-5.0 - hh))


def _rope(x, cos, sin):
    half = x.shape[1] // 2
    x1, x2 = x[:, :half], x[:, half:]
    return jnp.concatenate([x1 * cos - x2 * sin, x2 * cos + x1 * sin], axis=1)


def _unrope(dy, cos, sin):
    half = dy.shape[1] // 2
    d1, d2 = dy[:, :half], dy[:, half:]
    return jnp.concatenate([d1 * cos + d2 * sin, d2 * cos - d1 * sin], axis=1)


def _ret_decays(lg, rev):
    c = RET_CHUNK
    r = lax.broadcasted_iota(jnp.int32, (c, c), 0)
    k = lax.broadcasted_iota(jnp.int32, (c, c), 1)
    rel = (k - r) if rev else (r - k)
    dm = jnp.where(rel >= 0, jnp.exp(lg * jnp.maximum(rel, 0).astype(F32)), 0.0)
    pos = lax.broadcasted_iota(jnp.int32, (c, 1), 0).astype(F32)
    if rev:
        qdec = jnp.exp(lg * (c - pos))
        kdec = jnp.exp(lg * pos)
    else:
        qdec = jnp.exp(lg * (pos + 1.0))
        kdec = jnp.exp(lg * (c - 1.0 - pos))
    return dm, qdec, kdec


def _ret_fwd(q, k, v, cos, sin, lc, name, ex=None):
    T = q.shape[0]
    c, dk, dv = RET_CHUNK, RET_DK, RET_DV
    nt, nc = T // c, lc // c
    kscale = dk ** -0.5

    def body(qf, kf, vf, cf, sf_, qb, kb, vb, cb, sb_, of_ref, ob_ref, stf_ref, stb_ref, st_ref):
        @pl.when(pl.program_id(0) == 0)
        def _():
            st_ref[...] = jnp.zeros_like(st_ref)

        sets = ((qf, kf, vf, cf, sf_, of_ref, stf_ref), (qb, kb, vb, cb, sb_, ob_ref, stb_ref))
        for dr, (q_ref, k_ref, v_ref, c_ref, s_ref, o_ref, so_ref) in enumerate(sets):
            rev = dr == 1
            cos_v, sin_v = c_ref[...], s_ref[...]
            for h in range(RET_HEADS):
                lg = _ret_log_gamma(h, rev)
                dm, qdec, kdec = _ret_decays(lg, rev)
                qh = _rope(q_ref[:, h * dk:(h + 1) * dk], cos_v, sin_v)
                kh = _rope(k_ref[:, h * dk:(h + 1) * dk], cos_v, sin_v) * kscale
                vh = _bf(v_ref[:, h * dv:(h + 1) * dv])
                st = st_ref[dr, h]
                stb = _bf(st)
                so_ref[0, h] = stb
                sc = _nt(_bf(qh), _bf(kh)) * dm
                o_ref[:, h * dv:(h + 1) * dv] = _nn(_bf(sc), vh) + _nt(_bf(qh * qdec), stb)
                st_ref[dr, h] = st * math.exp(lg * c) + _tn(vh, _bf(kh * kdec))

    def spec(order, width):
        return pl.BlockSpec((c, width), lambda s: (order(s, nc, nt), 0))

    def st_spec(order):
        return pl.BlockSpec((1, RET_HEADS, dv, dk), lambda s: (order(s, nc, nt), 0, 0, 0))

    in_specs = []
    for order in (_fw_chunk, _bw_chunk):
        in_specs += [spec(order, RET_HEADS * dk), spec(order, RET_HEADS * dk), spec(order, RET_HEADS * dv),
                     spec(order, dk // 2), spec(order, dk // 2)]
    return _host_call(
        body, ex, lambda: pl.program_id(0) == 0, lambda: pl.program_id(0) == nt - 1,
        name=name, grid=(nt,), in_specs=in_specs,
        out_specs=[spec(_fw_chunk, RET_HEADS * dv), spec(_bw_chunk, RET_HEADS * dv), st_spec(_fw_chunk), st_spec(_bw_chunk)],
        out_shape=[jax.ShapeDtypeStruct((T, RET_HEADS * dv), F32), jax.ShapeDtypeStruct((T, RET_HEADS * dv), F32),
                   jax.ShapeDtypeStruct((nt, RET_HEADS, dv, dk), ACT), jax.ShapeDtypeStruct((nt, RET_HEADS, dv, dk), ACT)],
        scratch_shapes=[pltpu.VMEM((2, RET_HEADS, dv, dk), F32)], sem=("arbitrary",),
        args=(q, k, v, cos, sin, q, k, v, cos, sin))


def _ret_bwd(q, k, v, cos, sin, s_fw, s_bw, do, lc, name, ex=None):
    T = q.shape[0]
    c, dk, dv = RET_CHUNK, RET_DK, RET_DV
    nt, nc = T // c, lc // c
    kscale = dk ** -0.5

    def rfw(s, nc_, nt_):
        return _fw_chunk(nt_ - 1 - s, nc_, nt_)

    def rbw(s, nc_, nt_):
        return _bw_chunk(nt_ - 1 - s, nc_, nt_)

    def body(qf, kf, vf, cf, sf_, stf, dof, qb, kb, vb, cb, sb_, stb_, dob_,
             dqf, dkf, dvf, dqb, dkb, dvb, dst_ref):
        @pl.when(pl.program_id(0) == 0)
        def _():
            dst_ref[...] = jnp.zeros_like(dst_ref)

        sets = ((qf, kf, vf, cf, sf_, stf, dof, dqf, dkf, dvf), (qb, kb, vb, cb, sb_, stb_, dob_, dqb, dkb, dvb))
        for dr, (q_ref, k_ref, v_ref, c_ref, s_ref, st_in, do_ref, dq_ref, dk_ref, dv_ref) in enumerate(sets):
            rev = dr == 1
            cos_v, sin_v = c_ref[...], s_ref[...]
            for h in range(RET_HEADS):
                lg = _ret_log_gamma(h, rev)
                dm, qdec, kdec = _ret_decays(lg, rev)
                qh = _rope(q_ref[:, h * dk:(h + 1) * dk], cos_v, sin_v)
                kh = _rope(k_ref[:, h * dk:(h + 1) * dk], cos_v, sin_v) * kscale
                vh = _bf(v_ref[:, h * dv:(h + 1) * dv])
                qb16, kb16 = _bf(qh), _bf(kh)
                qinb, kinb = _bf(qh * qdec), _bf(kh * kdec)
                dob = _bf(do_ref[:, h * dv:(h + 1) * dv])
                stb = st_in[0, h]
                dst = dst_ref[dr, h]
                dstb = _bf(dst)
                sc = _bf(_nt(qb16, kb16) * dm)
                dsc = _bf(_nt(dob, vh) * dm)
                dq_r = _nn(dsc, kb16) + _nn(dob, stb) * qdec
                dk_r = _tn(dsc, qb16) + _nn(vh, dstb) * kdec
                dv_ref[:, h * dv:(h + 1) * dv] = _tn(sc, dob) + _nt(kinb, dstb)
                dst_ref[dr, h] = _tn(dob, qinb) + dst * math.exp(lg * c)
                dq_ref[:, h * dk:(h + 1) * dk] = _unrope(dq_r, cos_v, sin_v)
                dk_ref[:, h * dk:(h + 1) * dk] = _unrope(dk_r * kscale, cos_v, sin_v)

    def spec(order, width):
        return pl.BlockSpec((c, width), lambda s: (order(s, nc, nt), 0))

    def st_spec(order):
        return pl.BlockSpec((1, RET_HEADS, dv, dk), lambda s: (order(s, nc, nt), 0, 0, 0))

    in_specs = []
    for order in (rfw, rbw):
        in_specs += [spec(order, RET_HEADS * dk), spec(order, RET_HEADS * dk), spec(order, RET_HEADS * dv),
                     spec(order, dk // 2), spec(order, dk // 2), st_spec(order), spec(order, RET_HEADS * dv)]
    out_specs, out_shape = [], []
    for order in (rfw, rbw):
        out_specs += [spec(order, RET_HEADS * dk), spec(order, RET_HEADS * dk), spec(order, RET_HEADS * dv)]
        out_shape += [jax.ShapeDtypeStruct((T, RET_HEADS * dk), F32), jax.ShapeDtypeStruct((T, RET_HEADS * dk), F32),
                      jax.ShapeDtypeStruct((T, RET_HEADS * dv), F32)]
    return _host_call(
        body, ex, lambda: pl.program_id(0) == 0, lambda: pl.program_id(0) == nt - 1,
        name=name, grid=(nt,), in_specs=in_specs, out_specs=out_specs, out_shape=out_shape,
        scratch_shapes=[pltpu.VMEM((2, RET_HEADS, dv, dk), F32)], sem=("arbitrary",),
        args=(q, k, v, cos, sin, s_fw, do, q, k, v, cos, sin, s_bw, do))


def _attn_rope_tables(lc, l):
    t = jnp.arange(l)
    row = (t // GRID_W).astype(F32)
    colp = (t % GRID_W).astype(F32)
    n_freq = HEAD_DIM // 4
    inv = 10000.0 ** (-jnp.arange(n_freq, dtype=F32) / n_freq)
    ang = jnp.concatenate([row[:, None] * inv, colp[:, None] * inv], axis=-1)
    cos = jnp.concatenate([jnp.ones((lc, HEAD_DIM // 2), F32), jnp.cos(ang)], axis=0)
    sin = jnp.concatenate([jnp.zeros((lc, HEAD_DIM // 2), F32), jnp.sin(ang)], axis=0)
    return jnp.concatenate([cos, cos], axis=1), jnp.concatenate([-sin, sin], axis=1)


def _ret_rope_tables(lc, l):
    theta = 1.0 / (10000.0 ** jnp.linspace(0.0, 1.0, RET_DK // 2, dtype=F32))
    ang = jnp.arange(l, dtype=F32)[:, None] * theta
    cos = jnp.concatenate([jnp.ones((lc, RET_DK // 2), F32), jnp.cos(ang)], axis=0)
    sin = jnp.concatenate([jnp.zeros((lc, RET_DK // 2), F32), jnp.sin(ang)], axis=0)
    return cos, sin


def _heads_major(slab, n_heads):
    t = slab.shape[0]
    return slab.reshape(t, n_heads, HEAD_DIM).transpose(1, 0, 2)


def _slab(hm):
    nh, t, hd = hm.shape
    return hm.transpose(1, 0, 2).reshape(t, nh * hd)


COL_SHARDED = ("ffn_in0", "ffn_in1", "even_in", "odd_in")


def _full_weight(name, g):
    if name in COL_SHARDED:
        return g.transpose(1, 0, 2).reshape(g.shape[1], -1)
    return g.reshape(-1, g.shape[2])


def _shard_slots(name, g):
    if name in COL_SHARDED:
        return g.reshape(g.shape[0], N_DEV, -1).transpose(1, 0, 2)
    return g.reshape(N_DEV, -1, g.shape[1])


def _local_step(xs, target, mv, norm_g, w, qk_g, sink, hg_out_g, lbraw, lc, shards=None):
    T, dm = xs.shape
    l = T - lc
    tm = lc
    blk = ATTN_BLOCK
    d2, d3 = 2 * dm, 3 * dm
    w = dict(w)
    gw, recv = {}, {}

    def ms(layer, a, b):
        return mv[layer, :, :, a:b]

    def gather(names):
        return None if shards is None else _Exchange(GATHER, [shards[n] for n in names])

    def arrived(names, got):
        for n, g in zip(names, got):
            w[n] = _full_weight(n, g)

    def scatter(names):
        return None if shards is None else _Exchange(SCATTER, [_shard_slots(n, gw[n]) for n in names])

    def scattered(names, got):
        for n, g in zip(names, got):
            recv[n] = g

    g00, g01, g10, g11 = (norm_g[i, j][None, :] for i in (0, 1) for j in (0, 1))

    pa, pb = _pre_fwd(xs, g00, ms(0, 0, d2), w["even_in"], ((0, 768), (768, 3328)), tm, "pre0_fwd")
    cos2, sin2 = _attn_rope_tables(lc, l)
    cosp, sinp = jnp.concatenate([cos2, cos2], axis=1), jnp.concatenate([sin2, sin2], axis=1)
    gains5 = jnp.concatenate([jnp.broadcast_to(jnp.tile(qk_g[0], 2), (N_PAIRS - 1, PAIR)), jnp.tile(qk_g[1], 2)[None]])[:, None, :]
    qt, ks, vs = _qk_slab_fwd(pa, gains5, cosp, sinp, tm, "qk_prep_fwd")
    sinkb = jnp.broadcast_to(sink.reshape(ATTN_KV, 4, 1, 1), (ATTN_KV, 4, blk, 1)).reshape(ATTN_KV, 4 * blk, 1)
    (a_slab, lse), got = _attn_slab_fwd(qt, ks, vs, sinkb, lc, "attn_fwd", gather(["ffn_in0", "ffn_out0"]))
    arrived(["ffn_in0", "ffn_out0"], got)
    (hg_of, hg_ob, hg_sf, hg_sb), got = _gla_fwd(pb, lbraw, lc, "hgrn_fwd", gather(["odd_in", "odd_out"]))
    arrived(["odd_in", "odd_out"], got)
    x01, z0 = _post_fwd(xs, hg_of, hg_ob, pb, 4, hg_out_g, a_slab, w["even_out"], ms(0, d2, d3), HG_D, tm, "post0_fwd")
    x02, u0, f0 = _ffn_fwd(x01, g01, ms(0, d3, 6 * dm), w["ffn_in0"], w["ffn_out0"], tm, "ffn0_fwd")

    rq, rk, rv, rg = _pre_fwd(x02, g10, ms(1, 0, d2), w["odd_in"],
                              ((0, 1024), (1024, 2048), (2048, 4096), (4096, 6144)), tm, "pre1_fwd")
    rcos, rsin = _ret_rope_tables(lc, l)
    (rt_of, rt_ob, rt_sf, rt_sb), got = _ret_fwd(rq, rk, rv, rcos, rsin, lc, "ret_fwd", gather(["ffn_in1", "ffn_out1"]))
    arrived(["ffn_in1", "ffn_out1"], got)
    x11, z1 = _post_fwd(x02, rt_of, rt_ob, rg, 0, None, None, w["odd_out"], ms(1, d2, d3), RET_DV, tm, "post1_fwd")
    x12, u1, f1 = _ffn_fwd(x11, g11, ms(1, d3, 6 * dm), w["ffn_in1"], w["ffn_out1"], tm, "ffn1_fwd")

    dx, loss = _loss_bwd(x12, target, tm, "loss_bwd")

    dx, h, du, act, df, dms_f1, dg11 = _ffn_bwd(x11, dx, u1, f1, g11, ms(1, d3, 6 * dm), w["ffn_in1"], w["ffn_out1"], tm, "ffn1_bwd")
    gw["ffn_in1"] = _wgrad(h, du, "wg_ffn_in1")
    gw["ffn_out1"] = _wgrad(act, df, "wg_ffn_out1")
    do1, dgr1, dy1, dgate_p1, _ = _post_bwd(dx, z1, rt_of, rt_ob, rg, 0, None, w["odd_out"], ms(1, d2, d3), 0, RET_DV, tm, "post1_bwd")
    gw["odd_out"] = _wgrad(z1, dy1, "wg_odd_out")
    (dqf, dkf, dvf, dqb, dkb, dvb), got = _ret_bwd(rq, rk, rv, rcos, rsin, rt_sf, rt_sb, do1, lc, "ret_bwd",
                                                   scatter(["ffn_in1", "ffn_out1"]))
    scattered(["ffn_in1", "ffn_out1"], got)
    dx, h, dp, dms_p1, dg10 = _pre_bwd(x02, dx, g10, ms(1, 0, d2), w["odd_in"],
                                       [(0, [dqf, dqb]), (1024, [dkf, dkb]), (2048, [dvf, dvb]), (4096, [dgr1])], tm, "pre1_bwd")
    gw["odd_in"] = _wgrad(h, dp, "wg_odd_in")

    dx, h, du, act, df, dms_f0, dg01 = _ffn_bwd(x01, dx, u0, f0, g01, ms(0, d3, 6 * dm), w["ffn_in0"], w["ffn_out0"], tm, "ffn0_bwd")
    gw["ffn_in0"] = _wgrad(h, du, "wg_ffn_in0")
    gw["ffn_out0"] = _wgrad(act, df, "wg_ffn_out0")
    do0, dgr0, da0, dy0, dgate_p0, d_hg_gain = _post_bwd(dx, z0, hg_of, hg_ob, pb, 4, hg_out_g, w["even_out"], ms(0, d2, d3),
                                                        512, HG_D, tm, "post0_bwd")
    gw["even_out"] = _wgrad(z0, dy0, "wg_even_out")
    late = ["odd_in", "odd_out", "ffn_in0", "ffn_out0"]
    (hq_f, hz_f, hv_f, hq_b, hz_b, hv_b, dlb), got = _gla_bwd(pb, lbraw, hg_sf, hg_sb, do0, lc, "hgrn_bwd", scatter(late))
    scattered(late, got)
    dq_att, dk_att, dv_att, dsink = _attn_slab_bwd(qt, ks, vs, sinkb, a_slab, lse, da0, lc, "attn_bwd")
    dqk_raw, dgain5 = _qk_slab_bwd(dq_att, dk_att, pa, gains5, cosp, sinp, tm, "qk_prep_bwd")
    pieces0 = [(0, [dqk_raw]), (640, [dv_att]),
               (768, [hq_f, hq_b]), (1280, [hz_f]), (1792, [hz_b]), (2304, [hv_f, hv_b]), (2816, [dgr0])]
    dx, h, dp, dms_p0, dg00 = _pre_bwd(xs, dx, g00, ms(0, 0, d2), w["even_in"], pieces0, tm, "pre0_bwd")
    gw["even_in"] = _wgrad(h, dp, "wg_even_in")

    dmv = jnp.stack([jnp.concatenate([dms_p0, dgate_p0, dms_f0], axis=2), jnp.concatenate([dms_p1, dgate_p1, dms_f1], axis=2)])
    small = {
        "dmv": dmv,
        "norm_g": jnp.stack([jnp.stack([dg00[0], dg01[0]]), jnp.stack([dg10[0], dg11[0]])]),
        "qk_g": jnp.stack([jnp.sum(dgain5[:N_PAIRS - 1, 0].reshape(-1, HEAD_DIM), axis=0),
                           jnp.sum(dgain5[N_PAIRS - 1, 0].reshape(-1, HEAD_DIM), axis=0)]),
        "sink": dsink.reshape(ATTN_HEADS),
        "hg_out_g": d_hg_gain[0],
        "lb": dlb[0],
        "loss": loss[0, 0],
    }
    if shards is not None:
        gw = {n: recv.get(n, g) for n, g in gw.items()}
    return loss, dx, gw, small


HBM_SPEC = pl.BlockSpec(memory_space=pltpu.HBM)


def _my_index():
    return 4 * lax.axis_index("x") + 2 * lax.axis_index("y") + lax.axis_index("c")


def _peer(k):
    pos = []
    for axis, bit in (("x", 4), ("y", 2), ("c", 1)):
        a = lax.axis_index(axis)
        pos.append(1 - a if k & bit else a)
    return tuple(pos)


def _peer_index(k):
    px, py, pc = _peer(k)
    return 4 * px + 2 * py + pc


GATHER, SCATTER = "gather", "scatter"


class _Exchange:
    def __init__(self, mode, arrays):
        self.mode, self.arrays, self.n = mode, list(arrays), len(arrays)

    def out_shape(self):
        if self.mode == GATHER:
            return [jax.ShapeDtypeStruct((N_DEV,) + a.shape, a.dtype) for a in self.arrays]
        return [jax.ShapeDtypeStruct(a.shape, a.dtype) for a in self.arrays]

    def specs(self):
        return [HBM_SPEC] * self.n

    def scratch(self):
        return [pltpu.SemaphoreType.DMA((self.n, N_DEV - 1)), pltpu.SemaphoreType.DMA((self.n, N_DEV - 1)),
                pltpu.SemaphoreType.DMA((self.n,))]

    def _copies(self, in_refs, out_refs, send_sems, recv_sems, local_sems):
        me = _my_index()
        local, starts, waits = [], [], []
        for a, (src, dst) in enumerate(zip(in_refs, out_refs)):
            part = (lambda j, s=src: s) if self.mode == GATHER else (lambda j, s=src: s.at[j])
            local.append(pltpu.make_async_copy(part(me), dst.at[me], local_sems.at[a]))
            for k in range(1, N_DEV):
                pj = _peer_index(k)
                sems = dict(send_sem=send_sems.at[a, k - 1], recv_sem=recv_sems.at[a, k - 1], device_id=_peer(k),
                            device_id_type=MESH)
                starts.append(pltpu.make_async_remote_copy(src_ref=part(pj), dst_ref=dst.at[me], **sems))
                waits.append(pltpu.make_async_remote_copy(src_ref=part(pj), dst_ref=dst.at[pj], **sems))
        return local, starts, waits

    def start(self, in_refs, out_refs, sems):
        local, starts, _ = self._copies(in_refs, out_refs, *sems)
        for cp in local + starts:
            cp.start()

    def wait(self, in_refs, out_refs, sems):
        local, _, waits = self._copies(in_refs, out_refs, *sems)
        for cp in waits:
            cp.wait_send()
            cp.wait_recv()
        for cp in local:
            cp.wait()

    def ride(self, refs, n_in, n_out, first, last):
        refs = list(refs)
        n = self.n
        x_in = refs[n_in:n_in + n]
        x_out = refs[n_in + n + n_out:n_in + 2 * n + n_out]
        sems = refs[n_in + 2 * n + n_out:n_in + 2 * n + n_out + 3]

        @pl.when(first)
        def _():
            self.start(x_in, x_out, sems)

        @pl.when(last)
        def _():
            self.wait(x_in, x_out, sems)

        return refs[:n_in] + refs[n_in + n:n_in + n + n_out] + refs[n_in + 2 * n + n_out + 3:]

    def call(self, name):
        n = self.n

        def body(*refs):
            ins, outs, sems = refs[:n], refs[n:2 * n], refs[2 * n:]
            self.start(ins, outs, sems)
            self.wait(ins, outs, sems)

        return pl.pallas_call(body, name=name, in_specs=self.specs(), out_specs=self.specs(), out_shape=self.out_shape(),
                              scratch_shapes=self.scratch())(*self.arrays)


def _all_gather(v, name):
    return _Exchange(GATHER, [v]).call(name)[0]


def _hosted(kernel_body, ex, n_in, n_out, first, last):
    if ex is None:
        return kernel_body

    def body(*refs):
        kernel_body(*ex.ride(refs, n_in, n_out, first(), last()))

    return body


def _host_call(kernel_body, ex, first, last, name, grid, in_specs, out_specs, out_shape, scratch_shapes, sem, args):
    n_in, n_out = len(in_specs), len(out_specs)
    if ex is None:
        outs = pl.pallas_call(kernel_body, name=name, grid=grid, in_specs=in_specs, out_specs=out_specs, out_shape=out_shape,
                              scratch_shapes=scratch_shapes, compiler_params=_cp(*sem))(*args)
        return list(outs), []
    outs = pl.pallas_call(
        _hosted(kernel_body, ex, n_in, n_out, first, last), name=name, grid=grid,
        in_specs=list(in_specs) + ex.specs(), out_specs=list(out_specs) + ex.specs(),
        out_shape=list(out_shape) + ex.out_shape(), scratch_shapes=ex.scratch() + list(scratch_shapes),
        compiler_params=_cp(*sem))(*args, *ex.arrays)
    return list(outs[:n_out]), list(outs[n_out:])


def _mod_fwd(call, mod_w, bias, name):
    nl, dm, n = mod_w.shape

    def body(c_ref, w_ref, b_ref, o_ref):
        cv = c_ref[...]
        cond = _bf(cv * _sig(cv))
        for layer in range(nl):
            o_ref[layer] = _nn(cond, _bf(w_ref[layer])) + b_ref[layer]

    return pl.pallas_call(
        body, name=name, out_shape=jax.ShapeDtypeStruct((nl, call.shape[0], n), F32),
        compiler_params=pltpu.CompilerParams(vmem_limit_bytes=VMEM_LIMIT),
    )(call, mod_w, bias)


def _mod_bwd(call, dm_all, mod_w, name):
    nl, dm, n = mod_w.shape

    def body(c_ref, d_ref, w_ref, gw_ref, dc_ref):
        cv = c_ref[...]
        cond = _bf(cv * _sig(cv))
        dc = jnp.zeros(cv.shape, F32)
        for layer in range(nl):
            db = _bf(d_ref[layer])
            gw_ref[layer] = _tn(cond, db)
            dc = dc + _nt(db, _bf(w_ref[layer]))
        dc_ref[...] = dc

    return pl.pallas_call(
        body, name=name,
        out_shape=[jax.ShapeDtypeStruct(mod_w.shape, F32), jax.ShapeDtypeStruct(call.shape, F32)],
        compiler_params=pltpu.CompilerParams(vmem_limit_bytes=VMEM_LIMIT),
    )(call, dm_all, mod_w)


def _sum_parts(g, name):
    def body(g_ref, o_ref):
        acc = g_ref[0]
        for j in range(1, g.shape[0]):
            acc = acc + g_ref[j]
        o_ref[...] = acc

    return pl.pallas_call(body, name=name, out_shape=jax.ShapeDtypeStruct(g.shape[1:], g.dtype))(g)


def _small_finish(dcond_g, c_ctx, dlb, lbraw, dm_ctx, dm_lat, name):
    def body(dc_ref, c_ref, dlb_ref, lb_ref, mc_ref, ml_ref, gc_ref, glb_ref, gb_ref):
        acc = dc_ref[0, 0:1, :]
        for j in range(1, N_DEV):
            acc = acc + dc_ref[j, 0:1, :]
        cv = c_ref[...]
        s = _sig(cv)
        gc_ref[...] = acc * (s * (1.0 + cv * (1.0 - s)))
        lb = _lower_bound(lb_ref)
        d0 = dlb_ref[...] * lb * (1.0 - lb)
        glb_ref[0:1, :] = d0
        glb_ref[1:2, :] = -d0
        gb_ref[...] = mc_ref[...] + ml_ref[...]

    return pl.pallas_call(
        body, name=name,
        out_shape=[jax.ShapeDtypeStruct(c_ctx.shape, F32), jax.ShapeDtypeStruct(lbraw.shape, F32),
                   jax.ShapeDtypeStruct(dm_ctx.shape, F32)],
    )(dcond_g, c_ctx, dlb, lbraw, dm_ctx, dm_lat)


def _row_tile(r, cap, mult):
    best = r
    for t in range(mult, min(r, cap) + 1, mult):
        if r % t == 0:
            best = t
    return best


def _adam(g_list, w, m, v, name):
    nl, r, cdim = w.shape
    p = g_list[0].shape[0]
    tr = _row_tile(r, 128, 16)
    ni = r // tr

    def body(*refs):
        g_refs = refs[:nl]
        w_ref, m_ref, v_ref, go_ref, d_ref, mo_ref, vo_ref = refs[nl:]
        layer = pl.program_id(0)

        def total(g_ref):
            acc = g_ref[0].astype(F32)
            for j in range(1, p):
                acc = acc + g_ref[j].astype(F32)
            return acc

        g = total(g_refs[0])
        for k in range(1, nl):
            g = jnp.where(layer == k, total(g_refs[k]), g)
        m2 = ADAM_B1 * m_ref[0] + (1.0 - ADAM_B1) * g
        v2 = ADAM_B2 * v_ref[0] + (1.0 - ADAM_B2) * (g * g)
        m_hat = m2 / (1.0 - ADAM_B1 ** ADAM_STEP)
        v_hat = v2 / (1.0 - ADAM_B2 ** ADAM_STEP)
        go_ref[0] = g
        d_ref[0] = -ADAM_LR * (m_hat / (jnp.sqrt(v_hat) + ADAM_EPS) + ADAM_WD * w_ref[0])
        mo_ref[0] = m2
        vo_ref[0] = v2

    def g_spec(k):
        return pl.BlockSpec((p, tr, cdim), lambda la, i: (0, jnp.where(la == k, i, jnp.where(la < k, 0, ni - 1)), 0))

    spec = pl.BlockSpec((1, tr, cdim), lambda la, i: (la, i, 0))
    return pl.pallas_call(
        body, name=name, grid=(nl, ni),
        in_specs=[g_spec(k) for k in range(nl)] + [spec, spec, spec],
        out_specs=[spec] * 4, out_shape=[jax.ShapeDtypeStruct((nl, r, cdim), F32)] * 4,
        compiler_params=_cp("arbitrary", "arbitrary"),
    )(*g_list, w, m, v)


def _f32_as_rows(a, width):
    return lax.bitcast_convert_type(a.reshape(-1), BF16).reshape(-1, width)


def _rows_as_f32(rows):
    return lax.bitcast_convert_type(rows.reshape(rows.shape[:-2] + (-1, 2)), F32)


def _pad_rows(a, mult):
    r = (-a.shape[-2]) % mult
    if r == 0:
        return a
    widths = [(0, 0)] * (a.ndim - 2) + [(0, r), (0, 0)]
    return jnp.pad(a, widths)


def _pack_flat(parts, lane):
    flat = jnp.concatenate([p.reshape(-1).astype(F32) for p in parts])
    n = flat.shape[0]
    rows = -(-n // lane)
    rows += (-rows) % 8
    return jnp.pad(flat, (0, rows * lane - n)).reshape(rows, lane)


def _unpack_flat(packed, shapes):
    flat = packed.reshape(-1)
    out, off = [], 0
    for s in shapes:
        n = math.prod(s)
        out.append(flat[off:off + n].reshape(s))
        off += n
    return out


def kernel(x, c, ctx, c_ctx, mod_w, mod_b, norm_g, ffn_w_in, ffn_w_out, even_w_in, even_w_out, attn_qk_norm_g, attn_sink, hgrn_out_norm_g, hgrn_lb, odd_w_in, odd_w_out, loss_target, m_c_ctx, m_mod_w, m_mod_b, m_norm_g, m_ffn_w_in, m_ffn_w_out, m_even_w_in, m_even_w_out, m_attn_qk_norm_g, m_attn_sink, m_hgrn_out_norm_g, m_hgrn_lb, m_odd_w_in, m_odd_w_out, v_c_ctx, v_mod_w, v_mod_b, v_norm_g, v_ffn_w_in, v_ffn_w_out, v_even_w_in, v_even_w_out, v_attn_qk_norm_g, v_attn_sink, v_hgrn_out_norm_g, v_hgrn_lb, v_odd_w_in, v_odd_w_out):
    me = _my_index()
    lc, dm = ctx.shape[1], x.shape[2]
    nmod = mod_w.shape[2]
    big = (ffn_w_in, ffn_w_out, even_w_in, even_w_out, odd_w_in, odd_w_out)

    extra = _pad_rows(jnp.concatenate([_f32_as_rows(c, dm), _f32_as_rows(norm_g, dm)], axis=0), 16)
    shards = {"ffn_in0": ffn_w_in[0], "ffn_in1": ffn_w_in[1], "ffn_out0": ffn_w_out[0], "ffn_out1": ffn_w_out[1],
              "even_in": even_w_in[0], "even_out": even_w_out[0], "odd_in": odd_w_in[0], "odd_out": odd_w_out[0]}
    shards = {n: a.astype(BF16) for n, a in shards.items()}
    first = _Exchange(GATHER, [shards["even_in"], shards["even_out"], extra]).call("gather_first")
    w = {"even_in": _full_weight("even_in", first[0]), "even_out": _full_weight("even_out", first[1])}
    c_all = _rows_as_f32(first[2][:, 0:2])
    norm_g_all = _rows_as_f32(first[2][:, 2:3]).reshape(N_DEV, 2, 2, -1)
    norm_g_full = norm_g_all.transpose(1, 2, 0, 3).reshape(2, 2, dm)

    call = jnp.concatenate([c_all, c_ctx[None, :], jnp.zeros((16 - N_DEV - 1, dm), F32)], axis=0)
    bias = lax.dynamic_slice_in_dim(mod_b, me * nmod, nmod, axis=1)[:, None, :]
    m_sh = _mod_fwd(call, mod_w, bias, "mod_fwd")
    m_g = _all_gather(m_sh.reshape(-1, nmod), "gather_mod").reshape(N_DEV, 2, 16, nmod)
    m_all = m_g.transpose(1, 2, 0, 3).reshape(2, 16, -1)
    m_lat = lax.dynamic_index_in_dim(m_all, me, axis=1, keepdims=False)
    mv = jnp.stack([m_all[:, N_DEV], m_lat], axis=1)[:, :, None, :]

    xs = jnp.concatenate([ctx[0], x[0]], axis=0)
    _, dxs, gw, small = _local_step(xs, loss_target[0], mv, norm_g_full, w, attn_qk_norm_g[0], attn_sink[0],
                                    hgrn_out_norm_g, hgrn_lb, lc, shards)
    grad_x = dxs[lc:][None]

    last = _Exchange(SCATTER, [_shard_slots(n, gw[n]) for n in ("even_in", "even_out")]).call("scatter_last")
    gw["even_in"], gw["even_out"] = last
    big_g = ([gw["ffn_in0"], gw["ffn_in1"]], [gw["ffn_out0"], gw["ffn_out1"]], [gw["even_in"]], [gw["even_out"]],
             [gw["odd_in"]], [gw["odd_out"]])
    big_m = (m_ffn_w_in, m_ffn_w_out, m_even_w_in, m_even_w_out, m_odd_w_in, m_odd_w_out)
    big_v = (v_ffn_w_in, v_ffn_w_out, v_even_w_in, v_even_w_out, v_odd_w_in, v_odd_w_out)
    big_names = ("ffn_w_in", "ffn_w_out", "even_w_in", "even_w_out", "odd_w_in", "odd_w_out")
    big_out = [_adam(big_g[i], big[i], big_m[i], big_v[i], "adam_" + big_names[i]) for i in range(6)]
    big_res = [[big_out[i][k] for i in range(6)] for k in range(4)]

    dmv = small["dmv"]
    small_shapes = [(2, 6 * dm), (2, 6 * dm), (2, 2, dm), (2, HEAD_DIM), (ATTN_HEADS,), (HG_D,), (HG_HEADS * HG_D,), (1,)]
    vec = _pack_flat([dmv[:, 0, 0], dmv[:, 1, 0], small["norm_g"], small["qk_g"], small["sink"], small["hg_out_g"],
                      small["lb"], small["loss"]], 128)
    vec_g = _all_gather(vec, "gather_small")
    tot = _unpack_flat(_sum_parts(vec_g, "sum_small"), small_shapes)
    dm_ctx_tot, dm_lat_tot, g_norm_full, g_qk, g_sink, g_hg, dlb_tot, loss_tot = tot
    dm_lat_each = vec_g.reshape(N_DEV, -1)[:, 12 * dm:24 * dm].reshape(N_DEV, 2, 6 * dm)
    dm_lat_mine = lax.dynamic_slice_in_dim(dm_lat_each, me * nmod, nmod, axis=2).transpose(1, 0, 2)
    dm_ctx_mine = lax.dynamic_slice_in_dim(dm_ctx_tot, me * nmod, nmod, axis=1)[:, None, :]
    dm_all = jnp.concatenate([dm_lat_mine, dm_ctx_mine, jnp.zeros((2, 16 - N_DEV - 1, nmod), F32)], axis=1)
    g_mod_w, dcond = _mod_bwd(call, dm_all, mod_w, "mod_bwd")
    dcond_g = _all_gather(dcond[N_DEV:], "gather_dcond")
    g_c_ctx, g_lb, g_mod_b = _small_finish(dcond_g, c_ctx[None, :], dlb_tot[None, :], hgrn_lb, dm_ctx_tot, dm_lat_tot,
                                           "small_finish")
    g_norm = lax.dynamic_slice_in_dim(g_norm_full, me * norm_g.shape[2], norm_g.shape[2], axis=2)

    mod_res = _adam([g_mod_w[0][None], g_mod_w[1][None]], mod_w, m_mod_w, v_mod_w, "adam_mod_w")

    sm_w = (c_ctx, mod_b, norm_g, attn_qk_norm_g, attn_sink, hgrn_out_norm_g, hgrn_lb)
    sm_m = (m_c_ctx, m_mod_b, m_norm_g, m_attn_qk_norm_g, m_attn_sink, m_hgrn_out_norm_g, m_hgrn_lb)
    sm_v = (v_c_ctx, v_mod_b, v_norm_g, v_attn_qk_norm_g, v_attn_sink, v_hgrn_out_norm_g, v_hgrn_lb)
    sm_g = (g_c_ctx, g_mod_b, g_norm, g_qk, g_sink, g_hg, g_lb)
    sm_shapes = [a.shape for a in sm_w]
    sm_out = _adam([_pack_flat(sm_g, 128)[None]],_pack_flat(sm_w, 128)[None], _pack_flat(sm_m, 128)[None],
                   _pack_flat(sm_v, 128)[None], "adam_small")
    sm_res = [_unpack_flat(o, sm_shapes) for o in sm_out]

    def ordered(k):
        s, b = sm_res[k], big_res[k]
        return [s[0], mod_res[k], s[1], s[2], b[0], b[1], b[2], b[3], s[3], s[4], s[5], s[6], b[4], b[5]]

    return (loss_tot[0], grad_x, *ordered(0), *ordered(1), *ordered(2), *ordered(3))
```

```python
import functools
import math

import jax
import jax.numpy as jnp
from jax import lax
from jax.experimental import pallas as pl
from jax.experimental.pallas import tpu as pltpu

F32 = jnp.float32
BF16 = jnp.bfloat16
EPS = 1e-6
N_DEV = 8
MESH = pl.DeviceIdType.MESH

HEAD_DIM = 64
ATTN_HEADS = 8
ATTN_KV = 2
ATTN_BLOCK = 128
WINDOW = 128
GRID_W = 64
HG_HEADS = 4
HG_D = 128
HG_CHUNK = 64
RET_HEADS = 4
RET_DK = 256
RET_DV = 512
RET_CHUNK = 128
NEG = -1e30

ADAM_LR = 0.001
ADAM_B1 = 0.9
ADAM_B2 = 0.999
ADAM_EPS = 1e-08
ADAM_WD = 0.01
ADAM_STEP = 10

VMEM_LIMIT = 60 * 1024 * 1024


def _cp(*sem):
    return pltpu.CompilerParams(dimension_semantics=sem, vmem_limit_bytes=VMEM_LIMIT)


def _nn(a, b):
    return jnp.dot(a, b, preferred_element_type=F32)


def _nt(a, b):
    return lax.dot_general(a, b, (((1,), (1,)), ((), ())), preferred_element_type=F32)


def _tn(a, b):
    return lax.dot_general(a, b, (((0,), (0,)), ((), ())), preferred_element_type=F32)


ACT = BF16


def _bf(a):
    return a.astype(ACT)


def _sig(x):
    return jax.nn.sigmoid(x)


def _split3(x):
    h = x.astype(BF16)
    r = x - h.astype(F32)
    m = r.astype(BF16)
    lo = (r - m.astype(F32)).astype(BF16)
    return h, m, lo


def _nn3(m01, x):
    h, m, lo = _split3(x)
    return _nn(m01, h) + _nn(m01, m) + _nn(m01, lo)


def _nn3r(x, m01):
    h, m, lo = _split3(x)
    return _nn(h, m01) + _nn(m, m01) + _nn(lo, m01)


def _full(shape):
    nd = len(shape)
    return pl.BlockSpec(shape, lambda *a: (0,) * nd, pipeline_mode=pl.Buffered(1))


def _whole(shape):
    nd = len(shape)
    return pl.BlockSpec(shape, lambda *a: (0,) * nd)


def _rows(tm, width):
    return pl.BlockSpec((tm, width), lambda i: (i, 0))


def _cols(height, tm):
    return pl.BlockSpec((height, tm), lambda i: (0, i))


def _ctx_lat(width):
    return pl.BlockSpec((1, 1, width), lambda i: (jnp.minimum(i, 1), 0, 0))


def _acc_ctx_lat(ref, i, val):
    @pl.when(i <= 1)
    def _():
        ref[...] = val.reshape(ref.shape)

    @pl.when(i > 1)
    def _():
        ref[...] += val.reshape(ref.shape)


def _acc_all(ref, i, val):
    @pl.when(i == 0)
    def _():
        ref[...] = val.reshape(ref.shape)

    @pl.when(i > 0)
    def _():
        ref[...] += val.reshape(ref.shape)


def _tile(n, cap):
    best = None
    for t in range(128, min(n, cap) + 1, 128):
        if n % t == 0:
            best = t
    return n if best is None else best


def _norm_mod(xv, g, shift, scale):
    r = lax.rsqrt(jnp.mean(xv * xv, axis=-1, keepdims=True) + EPS)
    xhat = xv * r
    n = xhat * g
    return r, xhat, n, n * (1.0 + scale) + shift


def _norm_mod_bwd(dh, r, xhat, n, g, scale):
    dshift = jnp.sum(dh, axis=0, keepdims=True)
    dscale = jnp.sum(dh * n, axis=0, keepdims=True)
    dn = dh * (1.0 + scale)
    dg = jnp.sum(dn * xhat, axis=0, keepdims=True)
    dxh = dn * g
    dx = r * (dxh - xhat * jnp.mean(dxh * xhat, axis=-1, keepdims=True))
    return dx, dshift, dscale, dg


def _pre_fwd(x, gain, ms, w, splits, tm, name, ex=None):
    T, dm = x.shape
    nt = T // tm

    def body(x_ref, g_ref, ms_ref, w_ref, *outs):
        ms_v = ms_ref[0]
        h = _norm_mod(x_ref[...], g_ref[...], ms_v[:, :dm], ms_v[:, dm:])[3]
        hb = _bf(h)
        for (s, e), o_ref in zip(splits, outs):
            o_ref[...] = _nn(hb, w_ref[:, s:e])

    return _host_call(
        body, ex, lambda: pl.program_id(0) == 0, lambda: pl.program_id(0) == nt - 1,
        name=name, grid=(nt,),
        in_specs=[_rows(tm, dm), _full((1, dm)), _ctx_lat(2 * dm), _full(w.shape)],
        out_specs=[_rows(tm, e - s) for s, e in splits],
        out_shape=[jax.ShapeDtypeStruct((T, e - s), F32) for s, e in splits],
        scratch_shapes=[], sem=("arbitrary",), args=(x, gain, ms, w))


def _pre_bwd(x, dx_in, gain, ms, w, pieces, tm, name):
    T, dm = x.shape
    n_out = w.shape[1]
    flat = [a for _, arrs in pieces for a in arrs]

    def body(x_ref, dxin_ref, g_ref, ms_ref, w_ref, *rest):
        p_refs = rest[:len(flat)]
        dx_ref, h_ref, dp_ref, dms_ref, dg_ref = rest[len(flat):]
        i = pl.program_id(0)
        ms_v = ms_ref[0]
        g = g_ref[...]
        scale = ms_v[:, dm:]
        r, xhat, n, h = _norm_mod(x_ref[...], g, ms_v[:, :dm], scale)
        h_ref[...] = _bf(h).T
        dh = jnp.zeros((tm, dm), F32)
        k = 0
        for s, arrs in pieces:
            v = p_refs[k][...].astype(F32)
            for j in range(1, len(arrs)):
                v = v + p_refs[k + j][...].astype(F32)
            k += len(arrs)
            vb = _bf(v)
            wd = vb.shape[1]
            dp_ref[:, s:s + wd] = vb
            dh = dh + _nt(vb, w_ref[:, s:s + wd])
        dx, dshift, dscale, dg = _norm_mod_bwd(dh, r, xhat, n, g, scale)
        dx_ref[...] = dxin_ref[...] + dx
        _acc_ctx_lat(dms_ref, i, jnp.concatenate([dshift, dscale], axis=1))
        _acc_all(dg_ref, i, dg)

    return pl.pallas_call(
        body, name=name, grid=(T // tm,),
        in_specs=[_rows(tm, dm), _rows(tm, dm), _full((1, dm)), _ctx_lat(2 * dm), _full(w.shape)]
        + [_rows(tm, a.shape[1]) for a in flat],
        out_specs=[_rows(tm, dm), _cols(dm, tm), _rows(tm, n_out), _ctx_lat(2 * dm), _whole((1, dm))],
        out_shape=[jax.ShapeDtypeStruct((T, dm), F32), jax.ShapeDtypeStruct((dm, T), ACT),
                   jax.ShapeDtypeStruct((T, n_out), ACT), jax.ShapeDtypeStruct((2, 1, 2 * dm), F32),
                   jax.ShapeDtypeStruct((1, dm), F32)],
        compiler_params=_cp("arbitrary"),
    )(x, dx_in, gain, ms, w, *flat)


def _ffn_fwd(x1, gain, ms, w_in, w_out, tm, name):
    T, dm = x1.shape
    fh = w_out.shape[0]

    def body(x_ref, g_ref, ms_ref, wi_ref, wo_ref, x2_ref, u_ref, f_ref):
        ms_v = ms_ref[0]
        xv = x_ref[...]
        h = _norm_mod(xv, g_ref[...], ms_v[:, :dm], ms_v[:, dm:2 * dm])[3]
        u = _nn(_bf(h), wi_ref[...])
        u_ref[...] = _bf(u)
        gt = u[:, :fh]
        act = gt * _sig(gt) * u[:, fh:]
        f = _nn(_bf(act), wo_ref[...])
        f_ref[...] = _bf(f)
        x2_ref[...] = xv + ms_v[:, 2 * dm:] * f

    return pl.pallas_call(
        body, name=name, grid=(T // tm,),
        in_specs=[_rows(tm, dm), _full((1, dm)), _ctx_lat(3 * dm), _full(w_in.shape), _full(w_out.shape)],
        out_specs=[_rows(tm, dm), _rows(tm, 2 * fh), _rows(tm, dm)],
        out_shape=[jax.ShapeDtypeStruct((T, dm), F32), jax.ShapeDtypeStruct((T, 2 * fh), ACT),
                   jax.ShapeDtypeStruct((T, dm), ACT)],
        compiler_params=_cp("arbitrary"),
    )(x1, gain, ms, w_in, w_out)


def _ffn_bwd(x1, dx2, u, f, gain, ms, w_in, w_out, tm, name):
    T, dm = x1.shape
    fh = w_out.shape[0]

    def body(x_ref, dx2_ref, u_ref, f_ref, g_ref, ms_ref, wi_ref, wo_ref,
             dx1_ref, h_ref, du_ref, act_ref, df_ref, dms_ref, dg_ref):
        i = pl.program_id(0)
        ms_v = ms_ref[0]
        g = g_ref[...]
        scale = ms_v[:, dm:2 * dm]
        gate = ms_v[:, 2 * dm:]
        r, xhat, n, h = _norm_mod(x_ref[...], g, ms_v[:, :dm], scale)
        h_ref[...] = _bf(h).T
        dx2 = dx2_ref[...]
        dgate = jnp.sum(dx2 * f_ref[...].astype(F32), axis=0, keepdims=True)
        dfb = _bf(dx2 * gate)
        df_ref[...] = dfb
        da = _nt(dfb, wo_ref[...])
        uv = u_ref[...].astype(F32)
        gt = uv[:, :fh]
        up = uv[:, fh:]
        s = _sig(gt)
        sg = gt * s
        act_ref[...] = _bf(sg * up).T
        dgt = _bf(da * up * (s * (1.0 + gt * (1.0 - s))))
        dup = _bf(da * sg)
        du_ref[:, :fh] = dgt
        du_ref[:, fh:] = dup
        dh = _nt(dgt, wi_ref[:, :fh]) + _nt(dup, wi_ref[:, fh:])
        dx, dshift, dscale, dg = _norm_mod_bwd(dh, r, xhat, n, g, scale)
        dx1_ref[...] = dx2 + dx
        _acc_ctx_lat(dms_ref, i, jnp.concatenate([dshift, dscale, dgate], axis=1))
        _acc_all(dg_ref, i, dg)

    return pl.pallas_call(
        body, name=name, grid=(T // tm,),
        in_specs=[_rows(tm, dm), _rows(tm, dm), _rows(tm, 2 * fh), _rows(tm, dm), _full((1, dm)), _ctx_lat(3 * dm),
                  _full(w_in.shape), _full(w_out.shape)],
        out_specs=[_rows(tm, dm), _cols(dm, tm), _rows(tm, 2 * fh), _cols(fh, tm), _rows(tm, dm),
                   _ctx_lat(3 * dm), _whole((1, dm))],
        out_shape=[jax.ShapeDtypeStruct((T, dm), F32), jax.ShapeDtypeStruct((dm, T), ACT),
                   jax.ShapeDtypeStruct((T, 2 * fh), ACT), jax.ShapeDtypeStruct((fh, T), ACT),
                   jax.ShapeDtypeStruct((T, dm), ACT), jax.ShapeDtypeStruct((2, 1, 3 * dm), F32),
                   jax.ShapeDtypeStruct((1, dm), F32)],
        compiler_params=_cp("arbitrary"),
    )(x1, dx2, u, f, gain, ms, w_in, w_out)


def _wgrad(a_t, b, name):
    K, T = a_t.shape
    N = b.shape[1]
    tk, tn, tt = _tile(K, 1024), _tile(N, 1024), _tile(T, 2816)
    nt = T // tt

    def body(a_ref, b_ref, o_ref, acc_ref):
        t = pl.program_id(2)
        part = _nn(a_ref[...], b_ref[...])

        @pl.when(t == 0)
        def _():
            acc_ref[...] = part

        @pl.when(t > 0)
        def _():
            acc_ref[...] += part

        @pl.when(t == nt - 1)
        def _():
            o_ref[...] = acc_ref[...].astype(o_ref.dtype)

    return pl.pallas_call(
        body, name=name, grid=(K // tk, N // tn, nt),
        in_specs=[pl.BlockSpec((tk, tt), lambda i, j, t: (i, t)), pl.BlockSpec((tt, tn), lambda i, j, t: (t, j))],
        out_specs=pl.BlockSpec((tk, tn), lambda i, j, t: (i, j)),
        out_shape=jax.ShapeDtypeStruct((K, N), ACT),
        scratch_shapes=[pltpu.VMEM((tk, tn), F32)],
        compiler_params=_cp("parallel", "parallel", "arbitrary"),
    )(a_t, b)


def _post_fwd(x, o_fw, o_bw, g_src, g_blk, gain, a, w_out, ms, dvh, tm, name):
    T, dm = x.shape
    hv = o_fw.shape[1]
    aw = 0 if a is None else a.shape[1]
    has_gain = gain is not None

    def body(*refs):
        refs = list(refs)
        x_ref, of_ref, ob_ref, g_ref = refs[:4]
        k = 4
        gain_ref = a_ref = None
        if has_gain:
            gain_ref = refs[k]
            k += 1
        if aw:
            a_ref = refs[k]
            k += 1
        w_ref, ms_ref, x1_ref, z_ref = refs[k:k + 4]
        o = of_ref[...] + ob_ref[...]
        gr = g_ref[...]
        if aw:
            z_ref[:, :aw] = _bf(a_ref[...])
        for hd in range(hv // dvh):
            sl = slice(hd * dvh, (hd + 1) * dvh)
            oh = o[:, sl]
            gh = gr[:, sl]
            r = lax.rsqrt(jnp.mean(oh * oh, axis=-1, keepdims=True) + EPS)
            y = oh * r
            if has_gain:
                y = y * gain_ref[...]
            y = y * (gh * _sig(gh))
            z_ref[:, aw + hd * dvh:aw + (hd + 1) * dvh] = _bf(y)
        yp = _nn(z_ref[...], w_ref[...])
        x1_ref[...] = x_ref[...] + ms_ref[0] * yp

    ins = [x, o_fw, o_bw, g_src]
    specs = [_rows(tm, dm), _rows(tm, hv), _rows(tm, hv), pl.BlockSpec((tm, hv), lambda i: (i, g_blk))]
    if has_gain:
        ins.append(gain)
        specs.append(_full(gain.shape))
    if aw:
        ins.append(a)
        specs.append(_rows(tm, aw))
    ins += [w_out, ms]
    specs += [_full(w_out.shape), _ctx_lat(dm)]
    return pl.pallas_call(
        body, name=name, grid=(T // tm,), in_specs=specs,
        out_specs=[_rows(tm, dm), _rows(tm, aw + hv)],
        out_shape=[jax.ShapeDtypeStruct((T, dm), F32), jax.ShapeDtypeStruct((T, aw + hv), ACT)],
        compiler_params=_cp("arbitrary"),
    )(*ins)


def _post_bwd(dx1, z, o_fw, o_bw, g_src, g_blk, gain, w_out, ms, aw, dvh, tm, name):
    T, dm = dx1.shape
    hv = o_fw.shape[1]
    has_gain = gain is not None

    def body(*refs):
        refs = list(refs)
        dx1_ref, z_ref, of_ref, ob_ref, g_ref = refs[:5]
        k = 5
        gain_ref = None
        if has_gain:
            gain_ref = refs[k]
            k += 1
        w_ref, ms_ref = refs[k:k + 2]
        k += 2
        do_ref, dgr_ref = refs[k:k + 2]
        k += 2
        da_ref = None
        if aw:
            da_ref = refs[k]
            k += 1
        dy_ref, zt_ref, dgate_ref, dgain_ref = refs[k:k + 4]
        i = pl.program_id(0)
        dx1v = dx1_ref[...]
        zb = z_ref[...]
        zt_ref[...] = zb.T
        yp = _nn(zb, w_ref[...])
        _acc_ctx_lat(dgate_ref, i, jnp.sum(dx1v * yp, axis=0, keepdims=True))
        dyb = _bf(dx1v * ms_ref[0])
        dy_ref[...] = dyb
        dz = _nt(dyb, w_ref[...])
        if aw:
            da_ref[...] = dz[:, :aw]
        o = of_ref[...] + ob_ref[...]
        gr = g_ref[...]
        dgain = jnp.zeros((1, dvh), F32)
        for hd in range(hv // dvh):
            sl = slice(hd * dvh, (hd + 1) * dvh)
            oh = o[:, sl]
            gh = gr[:, sl]
            dyh = dz[:, aw + hd * dvh:aw + (hd + 1) * dvh]
            r = lax.rsqrt(jnp.mean(oh * oh, axis=-1, keepdims=True) + EPS)
            n = oh * r
            s = _sig(gh)
            sl_g = gh * s
            gn = gain_ref[...] if has_gain else 1.0
            dgr_ref[:, sl] = dyh * n * gn * (s * (1.0 + gh * (1.0 - s)))
            dn = dyh * gn * sl_g
            dgain = dgain + jnp.sum(dyh * n * sl_g, axis=0, keepdims=True)
            do_ref[:, sl] = r * (dn - n * jnp.mean(dn * n, axis=-1, keepdims=True))
        _acc_all(dgain_ref, i, dgain)

    ins = [dx1, z, o_fw, o_bw, g_src]
    specs = [_rows(tm, dm), _rows(tm, aw + hv), _rows(tm, hv), _rows(tm, hv),
             pl.BlockSpec((tm, hv), lambda i: (i, g_blk))]
    if has_gain:
        ins.append(gain)
        specs.append(_full(gain.shape))
    ins += [w_out, ms]
    specs += [_full(w_out.shape), _ctx_lat(dm)]
    out_specs = [_rows(tm, hv), _rows(tm, hv)]
    out_shape = [jax.ShapeDtypeStruct((T, hv), F32), jax.ShapeDtypeStruct((T, hv), F32)]
    if aw:
        out_specs.append(_rows(tm, aw))
        out_shape.append(jax.ShapeDtypeStruct((T, aw), F32))
    out_specs += [_rows(tm, dm), _cols(aw + hv, tm), _ctx_lat(dm), _whole((1, dvh))]
    out_shape += [jax.ShapeDtypeStruct((T, dm), ACT), jax.ShapeDtypeStruct((aw + hv, T), ACT),
                  jax.ShapeDtypeStruct((2, 1, dm), F32), jax.ShapeDtypeStruct((1, dvh), F32)]
    return pl.pallas_call(
        body, name=name, grid=(T // tm,), in_specs=specs, out_specs=out_specs, out_shape=out_shape,
        compiler_params=_cp("arbitrary"),
    )(*ins)


def _loss_bwd(x, target, tm, name):
    T, dm = x.shape

    def body(x_ref, t_ref, dx_ref, loss_ref):
        i = pl.program_id(0)

        @pl.when(i == 0)
        def _():
            dx_ref[...] = jnp.zeros_like(dx_ref)
            loss_ref[...] = jnp.zeros_like(loss_ref)

        @pl.when(i > 0)
        def _():
            e = x_ref[...] - t_ref[...]
            dx_ref[...] = e * (1.0 / dm)
            loss_ref[...] += jnp.sum(e * e) * (0.5 / dm)

    return pl.pallas_call(
        body, name=name, grid=(T // tm,),
        in_specs=[_rows(tm, dm), pl.BlockSpec((tm, dm), lambda i: (jnp.maximum(i - 1, 0), 0))],
        out_specs=[_rows(tm, dm), _whole((1, 1))],
        out_shape=[jax.ShapeDtypeStruct((T, dm), F32), jax.ShapeDtypeStruct((1, 1), F32)],
        compiler_params=_cp("arbitrary"),
    )(x, target)


def _swap_matrix():
    r = lax.broadcasted_iota(jnp.int32, (HEAD_DIM, HEAD_DIM), 0)
    c = lax.broadcasted_iota(jnp.int32, (HEAD_DIM, HEAD_DIM), 1)
    return jnp.where((r + HEAD_DIM // 2) % HEAD_DIM == c, 1.0, 0.0).astype(BF16)


def _qk_prep_fwd(raw, gains, cos2, sin2, tq, name):
    nh, T, hd = raw.shape

    def body(x_ref, g_ref, c_ref, s_ref, o_ref):
        hidx = pl.program_id(0)
        xv = x_ref[0]
        r = lax.rsqrt(jnp.mean(xv * xv, axis=-1, keepdims=True) + EPS)
        n = xv * r * g_ref[0]
        y = n * c_ref[...] + _nn3r(n, _swap_matrix()) * s_ref[...]
        sc = jnp.where(hidx < ATTN_HEADS, HEAD_DIM ** -0.5, 1.0)
        o_ref[0] = _bf(y * sc)

    return pl.pallas_call(
        body, name=name, grid=(nh, T // tq),
        in_specs=[pl.BlockSpec((1, tq, hd), lambda h, i: (h, i, 0)), pl.BlockSpec((1, 1, hd), lambda h, i: (h, 0, 0)),
                  pl.BlockSpec((tq, hd), lambda h, i: (i, 0)), pl.BlockSpec((tq, hd), lambda h, i: (i, 0))],
        out_specs=pl.BlockSpec((1, tq, hd), lambda h, i: (h, i, 0)),
        out_shape=jax.ShapeDtypeStruct((nh, T, hd), ACT),
        compiler_params=_cp("arbitrary", "arbitrary"),
    )(raw, gains, cos2, sin2)


def _qk_prep_bwd(dy, raw, gains, cos2, sin2, tq, name):
    nh, T, hd = raw.shape

    def body(dy_ref, x_ref, g_ref, c_ref, s_ref, dx_ref, dg_ref):
        hidx = pl.program_id(0)
        i = pl.program_id(1)
        xv = x_ref[0]
        g = g_ref[0]
        r = lax.rsqrt(jnp.mean(xv * xv, axis=-1, keepdims=True) + EPS)
        xhat = xv * r
        sc = jnp.where(hidx < ATTN_HEADS, HEAD_DIM ** -0.5, 1.0)
        dyv = dy_ref[0] * sc
        dn = dyv * c_ref[...] + _nn3r(dyv * s_ref[...], _swap_matrix())
        _acc_all(dg_ref, i, jnp.sum(dn * xhat, axis=0, keepdims=True))
        dxh = dn * g
        dx_ref[0] = r * (dxh - xhat * jnp.mean(dxh * xhat, axis=-1, keepdims=True))

    return pl.pallas_call(
        body, name=name, grid=(nh, T // tq),
        in_specs=[pl.BlockSpec((1, tq, hd), lambda h, i: (h, i, 0)), pl.BlockSpec((1, tq, hd), lambda h, i: (h, i, 0)),
                  pl.BlockSpec((1, 1, hd), lambda h, i: (h, 0, 0)),
                  pl.BlockSpec((tq, hd), lambda h, i: (i, 0)), pl.BlockSpec((tq, hd), lambda h, i: (i, 0))],
        out_specs=[pl.BlockSpec((1, tq, hd), lambda h, i: (h, i, 0)), pl.BlockSpec((1, 1, hd), lambda h, i: (h, 0, 0))],
        out_shape=[jax.ShapeDtypeStruct((nh, T, hd), F32), jax.ShapeDtypeStruct((nh, 1, hd), F32)],
        compiler_params=_cp("arbitrary", "arbitrary"),
    )(dy, raw, gains, cos2, sin2)


def _attn_scores(q, k_ref, i, lc, T, sink):
    blk = ATTN_BLOCK
    kc = k_ref[0, pl.ds(blk, lc), :]
    kw = k_ref[0, pl.ds(pl.multiple_of(i * blk, blk), 3 * blk), :]
    s_c = _nt(q, kc)
    s_w = _nt(q, kw)
    row = lax.broadcasted_iota(jnp.int32, (4 * blk, 1), 0)
    qpos = i * blk + (row & (blk - 1))
    kpos = (i - 1) * blk + lax.broadcasted_iota(jnp.int32, (1, 3 * blk), 1)
    valid = (qpos >= lc) & (kpos >= lc) & (kpos < T) & (jnp.abs(kpos - qpos) <= WINDOW)
    s_w = jnp.where(valid, s_w, NEG)
    return kc, kw, s_c, s_w


def _attn_fwd(qt, kp, vp, sinkb, lc, name, ex=None):
    nh, T, hd = qt.shape
    blk = ATTN_BLOCK
    g = nh // ATTN_KV

    def body(q_ref, k_ref, v_ref, sink_ref, o_ref, lse_ref):
        i = pl.program_id(1)
        q = q_ref[...].reshape(g * blk, hd)
        sink = sink_ref[0]
        kc, kw, s_c, s_w = _attn_scores(q, k_ref, i, lc, T, sink)
        m = jnp.maximum(jnp.maximum(jnp.max(s_c, axis=-1, keepdims=True), jnp.max(s_w, axis=-1, keepdims=True)), sink)
        e_c = jnp.exp(s_c - m)
        e_w = jnp.exp(s_w - m)
        den = jnp.exp(sink - m) + jnp.sum(e_c, axis=-1, keepdims=True) + jnp.sum(e_w, axis=-1, keepdims=True)
        inv = 1.0 / den
        vc = v_ref[0, pl.ds(blk, lc), :]
        vw = v_ref[0, pl.ds(pl.multiple_of(i * blk, blk), 3 * blk), :]
        o = _nn(_bf(e_c * inv), vc) + _nn(_bf(e_w * inv), vw)
        o_ref[...] = o.reshape(g, blk, hd)
        lse_ref[...] = (m + jnp.log(den)).reshape(g, blk, 1)

    nb = T // blk
    return _host_call(
        body, ex, lambda: (pl.program_id(0) == 0) & (pl.program_id(1) == 0),
        lambda: (pl.program_id(0) == ATTN_KV - 1) & (pl.program_id(1) == nb - 1),
        name=name, grid=(ATTN_KV, nb),
        in_specs=[pl.BlockSpec((g, blk, hd), lambda kv, i: (kv, i, 0)),
                  pl.BlockSpec((1, T + 2 * blk, hd), lambda kv, i: (kv, 0, 0)),
                  pl.BlockSpec((1, T + 2 * blk, hd), lambda kv, i: (kv, 0, 0)),
                  pl.BlockSpec((1, g * blk, 1), lambda kv, i: (kv, 0, 0))],
        out_specs=[pl.BlockSpec((g, blk, hd), lambda kv, i: (kv, i, 0)),
                   pl.BlockSpec((g, blk, 1), lambda kv, i: (kv, i, 0))],
        out_shape=[jax.ShapeDtypeStruct((nh, T, hd), F32), jax.ShapeDtypeStruct((nh, T, 1), F32)],
        scratch_shapes=[], sem=("arbitrary", "arbitrary"), args=(qt, kp, vp, sinkb))


def _attn_bwd(qt, kp, vp, sinkb, o, lse, do, lc, name):
    nh, T, hd = qt.shape
    blk = ATTN_BLOCK
    g = nh // ATTN_KV

    def body(q_ref, k_ref, v_ref, sink_ref, o_ref, lse_ref, do_ref, dq_ref, dk_ref, dv_ref, ds_ref):
        i = pl.program_id(1)

        @pl.when(i == 0)
        def _():
            dk_ref[...] = jnp.zeros_like(dk_ref)
            dv_ref[...] = jnp.zeros_like(dv_ref)
            ds_ref[...] = jnp.zeros_like(ds_ref)

        q = q_ref[...].reshape(g * blk, hd)
        sink = sink_ref[0]
        lse = lse_ref[...].reshape(g * blk, 1)
        dov = do_ref[...].reshape(g * blk, hd)
        delta = jnp.sum(dov * o_ref[...].reshape(g * blk, hd), axis=-1, keepdims=True)
        kc, kw, s_c, s_w = _attn_scores(q, k_ref, i, lc, T, sink)
        p_c = jnp.exp(s_c - lse)
        p_w = jnp.exp(s_w - lse)
        win = pl.ds(pl.multiple_of(i * blk, blk), 3 * blk)
        vc = v_ref[0, pl.ds(blk, lc), :]
        vw = v_ref[0, win, :]
        dob = _bf(dov)
        ds_c = _bf(p_c * (_nt(dob, vc) - delta))
        ds_w = _bf(p_w * (_nt(dob, vw) - delta))
        dsr = -jnp.exp(sink - lse) * delta
        for hh in range(g):
            ds_ref[0, hh:hh + 1, :] += jnp.sum(dsr[hh * blk:(hh + 1) * blk, :], axis=0, keepdims=True)
        dq_ref[...] = (_nn(ds_c, kc) + _nn(ds_w, kw)).reshape(g, blk, hd)
        dk_ref[0, pl.ds(blk, lc), :] += _tn(ds_c, q)
        dk_ref[0, win, :] += _tn(ds_w, q)
        dv_ref[0, pl.ds(blk, lc), :] += _tn(_bf(p_c), dob)
        dv_ref[0, win, :] += _tn(_bf(p_w), dob)

    qspec = pl.BlockSpec((g, blk, hd), lambda kv, i: (kv, i, 0))
    kspec = pl.BlockSpec((1, T + 2 * blk, hd), lambda kv, i: (kv, 0, 0))
    lspec = pl.BlockSpec((g, blk, 1), lambda kv, i: (kv, i, 0))
    return pl.pallas_call(
        body, name=name, grid=(ATTN_KV, T // blk),
        in_specs=[qspec, kspec, kspec, pl.BlockSpec((1, g * blk, 1), lambda kv, i: (kv, 0, 0)), qspec, lspec, qspec],
        out_specs=[qspec, kspec, kspec, pl.BlockSpec((1, g, 1), lambda kv, i: (kv, 0, 0))],
        out_shape=[jax.ShapeDtypeStruct((nh, T, hd), F32), jax.ShapeDtypeStruct((ATTN_KV, T + 2 * blk, hd), F32),
                   jax.ShapeDtypeStruct((ATTN_KV, T + 2 * blk, hd), F32), jax.ShapeDtypeStruct((ATTN_KV, g, 1), F32)],
        compiler_params=_cp("arbitrary", "arbitrary"),
    )(qt, kp, vp, sinkb, o, lse, do)


PAIR = 2 * HEAD_DIM
N_PAIRS = (ATTN_HEADS + ATTN_KV) // 2


def _lanes():
    return lax.broadcasted_iota(jnp.int32, (1, PAIR), 1)


def _swap32(v):
    first_half = (_lanes() & (HEAD_DIM // 2)) == 0
    return jnp.where(first_half, pltpu.roll(v, PAIR - HEAD_DIM // 2, 1), pltpu.roll(v, HEAD_DIM // 2, 1))


def _head_mean(v):
    r = lax.broadcasted_iota(jnp.int32, (PAIR, PAIR), 0)
    c = lax.broadcasted_iota(jnp.int32, (PAIR, PAIR), 1)
    same = jnp.where((r >= HEAD_DIM) == (c >= HEAD_DIM), 1.0, 0.0).astype(BF16)
    return _nn3r(v, same) * (1.0 / HEAD_DIM)


def _qk_slab_fwd(pa, gains, cosp, sinp, tm, name):
    T = pa.shape[0]
    qw = ATTN_HEADS * HEAD_DIM

    def body(pa_ref, g_ref, c_ref, s_ref, q_ref, k_ref, v_ref):
        cosv, sinv = c_ref[...], s_ref[...]
        for p in range(N_PAIRS):
            xv = pa_ref[:, p * PAIR:(p + 1) * PAIR]
            n = xv * lax.rsqrt(_head_mean(xv * xv) + EPS) * g_ref[p]
            y = n * cosv + _swap32(n) * sinv
            if p < N_PAIRS - 1:
                q_ref[:, p * PAIR:(p + 1) * PAIR] = _bf(y * HEAD_DIM ** -0.5)
            else:
                k_ref[...] = _bf(y)
        v_ref[...] = _bf(pa_ref[:, qw + PAIR:])

    return pl.pallas_call(
        body, name=name, grid=(T // tm,),
        in_specs=[_rows(tm, pa.shape[1]), _full(gains.shape), _rows(tm, PAIR), _rows(tm, PAIR)],
        out_specs=[_rows(tm, qw), _rows(tm, PAIR), _rows(tm, PAIR)],
        out_shape=[jax.ShapeDtypeStruct((T, qw), ACT), jax.ShapeDtypeStruct((T, PAIR), ACT),
                   jax.ShapeDtypeStruct((T, PAIR), ACT)],
        compiler_params=_cp("arbitrary"),
    )(pa, gains, cosp, sinp)


def _qk_slab_bwd(dq, dk, pa, gains, cosp, sinp, tm, name):
    T = pa.shape[0]
    qw = ATTN_HEADS * HEAD_DIM

    def body(dq_ref, dk_ref, pa_ref, g_ref, c_ref, s_ref, dx_ref, dg_ref):
        i = pl.program_id(0)
        cosv, sinv = c_ref[...], s_ref[...]
        for p in range(N_PAIRS):
            sl = slice(p * PAIR, (p + 1) * PAIR)
            xv = pa_ref[:, sl]
            r = lax.rsqrt(_head_mean(xv * xv) + EPS)
            xhat = xv * r
            dy = dq_ref[:, sl] * HEAD_DIM ** -0.5 if p < N_PAIRS - 1 else dk_ref[...]
            dn = dy * cosv + _swap32(dy * sinv)
            _acc_all(dg_ref.at[p], i, jnp.sum(dn * xhat, axis=0, keepdims=True))
            dxh = dn * g_ref[p]
            dx_ref[:, sl] = r * (dxh - xhat * _head_mean(dxh * xhat))

    return pl.pallas_call(
        body, name=name, grid=(T // tm,),
        in_specs=[_rows(tm, qw), _rows(tm, PAIR), _rows(tm, qw + PAIR), _full(gains.shape), _rows(tm, PAIR), _rows(tm, PAIR)],
        out_specs=[_rows(tm, qw + PAIR), _whole(gains.shape)],
        out_shape=[jax.ShapeDtypeStruct((T, qw + PAIR), F32), jax.ShapeDtypeStruct(gains.shape, F32)],
        compiler_params=_cp("arbitrary"),
    )(dq, dk, pa, gains, cosp, sinp)


def _attn_window(ref, i, nb):
    blk = ATTN_BLOCK
    starts = [pl.multiple_of(jnp.clip(i + d, 0, nb - 1) * blk, blk) for d in (-1, 0, 1)]
    return starts, jnp.concatenate([ref[pl.ds(s, blk), :] for s in starts], axis=0)


def _attn_mask(i, lc, T):
    blk = ATTN_BLOCK
    row = lax.broadcasted_iota(jnp.int32, (4 * blk, 1), 0)
    qpos = i * blk + (row & (blk - 1))
    kpos = (i - 1) * blk + lax.broadcasted_iota(jnp.int32, (1, 3 * blk), 1)
    return (qpos >= lc) & (kpos >= lc) & (kpos < T) & (jnp.abs(kpos - qpos) <= WINDOW)


def _to_kv_half(v, head, kv):
    return v if head % 2 == kv else pltpu.roll(v, HEAD_DIM, 1)


def _attn_slab_fwd(qt, ks, vs, sinkb, lc, name, ex=None):
    T = qt.shape[0]
    blk = ATTN_BLOCK
    nb = T // blk
    g = ATTN_HEADS // ATTN_KV

    def body(q_ref, k_ref, v_ref, sink_ref, o_ref, lse_ref):
        i = pl.program_id(0)
        lane = _lanes()
        valid = _attn_mask(i, lc, T)
        kc_all, vc = k_ref[0:lc, :], v_ref[0:lc, :]
        _, kw_all = _attn_window(k_ref, i, nb)
        _, vw = _attn_window(v_ref, i, nb)
        placed = [None] * ATTN_HEADS
        for kv in range(ATTN_KV):
            mine = (lane >= kv * HEAD_DIM) & (lane < (kv + 1) * HEAD_DIM)
            kc = jnp.where(mine, kc_all, jnp.zeros_like(kc_all))
            kw = jnp.where(mine, kw_all, jnp.zeros_like(kw_all))
            heads = [kv * g + j for j in range(g)]
            q4 = jnp.concatenate([_to_kv_half(q_ref[:, (h // 2) * PAIR:(h // 2 + 1) * PAIR], h, kv) for h in heads], axis=0)
            sink = sink_ref[kv]
            s_c = _nt(q4, kc)
            s_w = jnp.where(valid, _nt(q4, kw), NEG)
            m = jnp.maximum(jnp.maximum(jnp.max(s_c, axis=-1, keepdims=True), jnp.max(s_w, axis=-1, keepdims=True)), sink)
            e_c = jnp.exp(s_c - m)
            e_w = jnp.exp(s_w - m)
            den = jnp.exp(sink - m) + jnp.sum(e_c, axis=-1, keepdims=True) + jnp.sum(e_w, axis=-1, keepdims=True)
            inv = 1.0 / den
            o4 = _nn(_bf(e_c * inv), vc) + _nn(_bf(e_w * inv), vw)
            lse_ref[kv * g:(kv + 1) * g] = (m + jnp.log(den)).reshape(g, blk, 1)
            for j, h in enumerate(heads):
                placed[h] = _to_kv_half(o4[j * blk:(j + 1) * blk], h, kv)
        for p in range(ATTN_HEADS // 2):
            o_ref[:, p * PAIR:(p + 1) * PAIR] = jnp.where(lane < HEAD_DIM, placed[2 * p], placed[2 * p + 1])

    qw = ATTN_HEADS * HEAD_DIM
    return _host_call(
        body, ex, lambda: pl.program_id(0) == 0, lambda: pl.program_id(0) == nb - 1,
        name=name, grid=(nb,),
        in_specs=[_rows(blk, qw), _full((T, PAIR)), _full((T, PAIR)), _full(sinkb.shape)],
        out_specs=[_rows(blk, qw), pl.BlockSpec((ATTN_HEADS, blk, 1), lambda i: (0, i, 0))],
        out_shape=[jax.ShapeDtypeStruct((T, qw), F32), jax.ShapeDtypeStruct((ATTN_HEADS, T, 1), F32)],
        scratch_shapes=[], sem=("arbitrary",), args=(qt, ks, vs, sinkb))


def _attn_slab_bwd(qt, ks, vs, sinkb, o, lse, do, lc, name, ex=None):
    T = qt.shape[0]
    blk = ATTN_BLOCK
    nb = T // blk
    g = ATTN_HEADS // ATTN_KV

    def body(q_ref, k_ref, v_ref, sink_ref, o_ref, lse_ref, do_ref, dq_ref, dk_ref, dv_ref, ds_ref):
        i = pl.program_id(0)

        @pl.when(i == 0)
        def _():
            dk_ref[...] = jnp.zeros_like(dk_ref)
            dv_ref[...] = jnp.zeros_like(dv_ref)
            ds_ref[...] = jnp.zeros_like(ds_ref)

        lane = _lanes()
        valid = _attn_mask(i, lc, T)
        kc_all, vc_all = k_ref[0:lc, :], v_ref[0:lc, :]
        starts, kw_all = _attn_window(k_ref, i, nb)
        _, vw_all = _attn_window(v_ref, i, nb)
        dq_pairs = [jnp.zeros((blk, PAIR), F32) for _ in range(ATTN_HEADS // 2)]
        for kv in range(ATTN_KV):
            mine = (lane >= kv * HEAD_DIM) & (lane < (kv + 1) * HEAD_DIM)

            def only(v):
                return jnp.where(mine, v, jnp.zeros_like(v))

            kc, kw, vc, vw = only(kc_all), only(kw_all), only(vc_all), only(vw_all)
            heads = [kv * g + j for j in range(g)]
            qs, dos, deltas = [], [], []
            for h in heads:
                sl = slice((h // 2) * PAIR, (h // 2 + 1) * PAIR)
                dov = do_ref[:, sl]
                qs.append(_to_kv_half(q_ref[:, sl], h, kv))
                dos.append(_bf(_to_kv_half(dov, h, kv)))
                own = (lane < HEAD_DIM) if h % 2 == 0 else (lane >= HEAD_DIM)
                deltas.append(jnp.sum(jnp.where(own, dov * o_ref[:, sl], 0.0), axis=-1, keepdims=True))
            q4, do4, delta = jnp.concatenate(qs, axis=0), jnp.concatenate(dos, axis=0), jnp.concatenate(deltas, axis=0)
            sink = sink_ref[kv]
            lse = lse_ref[kv * g:(kv + 1) * g].reshape(g * blk, 1)
            p_c = jnp.exp(_nt(q4, kc) - lse)
            p_w = jnp.exp(jnp.where(valid, _nt(q4, kw), NEG) - lse)
            ds_c = _bf(p_c * (_nt(do4, vc) - delta))
            ds_w = _bf(p_w * (_nt(do4, vw) - delta))
            dsr = -jnp.exp(sink - lse) * delta
            dq4 = _nn(ds_c, kc) + _nn(ds_w, kw)
            for j, h in enumerate(heads):
                ds_ref[h:h + 1, :] += jnp.sum(dsr[j * blk:(j + 1) * blk, :], axis=0, keepdims=True)
                dq_pairs[h // 2] = dq_pairs[h // 2] + _to_kv_half(dq4[j * blk:(j + 1) * blk], h, kv)
            dk_ref[0:lc, :] += only(_tn(ds_c, q4))
            dv_ref[0:lc, :] += only(_tn(_bf(p_c), do4))
            dkw = only(_tn(ds_w, q4))
            dvw = only(_tn(_bf(p_w), do4))
            for b, s in enumerate(starts):
                dk_ref[pl.ds(s, blk), :] += dkw[b * blk:(b + 1) * blk]
                dv_ref[pl.ds(s, blk), :] += dvw[b * blk:(b + 1) * blk]
        for p in range(ATTN_HEADS // 2):
            dq_ref[:, p * PAIR:(p + 1) * PAIR] = dq_pairs[p]

    qw = ATTN_HEADS * HEAD_DIM
    lspec = pl.BlockSpec((ATTN_HEADS, blk, 1), lambda i: (0, i, 0))
    return _host_call(
        body, ex, lambda: pl.program_id(0) == 0, lambda: pl.program_id(0) == nb - 1,
        name=name, grid=(nb,),
        in_specs=[_rows(blk, qw), _full((T, PAIR)), _full((T, PAIR)), _full(sinkb.shape), _rows(blk, qw), lspec,
                  _rows(blk, qw)],
        out_specs=[_rows(blk, qw), _whole((T, PAIR)), _whole((T, PAIR)), _whole((ATTN_HEADS, 1))],
        out_shape=[jax.ShapeDtypeStruct((T, qw), F32), jax.ShapeDtypeStruct((T, PAIR), F32),
                   jax.ShapeDtypeStruct((T, PAIR), F32), jax.ShapeDtypeStruct((ATTN_HEADS, 1), F32)],
        scratch_shapes=[], sem=("arbitrary",), args=(qt, ks, vs, sinkb, o, lse, do))


def _fw_chunk(s, nc, nt):
    return s


def _bw_chunk(s, nc, nt):
    return jnp.where(s < nc, nc - 1 - s, nt - 1 - (s - nc))


def _tri(c, rev):
    r = lax.broadcasted_iota(jnp.int32, (c, c), 0)
    k = lax.broadcasted_iota(jnp.int32, (c, c), 1)
    return (k >= r) if rev else (k <= r)


def _gla_gates(z, lb, rev):
    c = HG_CHUNK
    sg = _sig(z)
    f = lb + (1.0 - lb) * sg
    cum = _nn3(jnp.where(_tri(c, rev), 1.0, 0.0).astype(BF16), jnp.log(f))
    mid = c - 1 - c // 2 if rev else c // 2
    last = 0 if rev else c - 1
    return sg, f, cum, cum[mid:mid + 1], cum[last:last + 1], last


def _lower_bound(lbraw_ref):
    lr = lbraw_ref[...]
    return _sig(lr[0:1] - lr[1:2])


def _gla_fwd(pb, lbraw, lc, name, ex=None):
    T = pb.shape[0]
    c, hw, d = HG_CHUNK, HG_HEADS * HG_D, HG_D
    nt, nc = T // c, lc // c
    orders = (_fw_chunk, _bw_chunk)

    def body(qf, zf, vf, qb, zb, vb, lb_ref, of_ref, ob_ref, sf_ref, sb_ref, st_ref):
        @pl.when(pl.program_id(0) == 0)
        def _():
            st_ref[...] = jnp.zeros_like(st_ref)

        lb = _lower_bound(lb_ref)
        dirs = ((qf, zf, vf, of_ref, sf_ref), (qb, zb, vb, ob_ref, sb_ref))
        combos = [(dr, h, slice(h * d, (h + 1) * d)) for dr in range(2) for h in range(HG_HEADS)]
        prep = []
        for dr, (q_ref, z_ref, v_ref, _, _) in enumerate(dirs):
            rev = dr == 1
            qr = q_ref[...]
            q = qr * _sig(qr)
            _, f, cum, ref, last, _ = _gla_gates(z_ref[...], lb, rev)
            k = 1.0 - f
            prep.append(dict(q1=_bf(q * jnp.exp(cum - ref)), k1=_bf(k * jnp.exp(ref - cum)), q2=_bf(q * jnp.exp(cum)),
                             k2=_bf(k * jnp.exp(last - cum)), el=jnp.exp(last), v=_bf(v_ref[...]), mask=_tri(c, rev)))
        a = [_bf(jnp.where(prep[dr]["mask"], _nt(prep[dr]["q1"][:, sl], prep[dr]["k1"][:, sl]), 0.0)) for dr, _, sl in combos]
        for (dr, h, sl), a_h in zip(combos, a):
            p = prep[dr]
            o_ref, s_ref = dirs[dr][3], dirs[dr][4]
            st = st_ref[dr, h]
            stb = _bf(st)
            s_ref[0, h] = stb
            o_ref[:, sl] = _nn(a_h, p["v"][:, sl]) + _nt(p["q2"][:, sl], stb)
            st_ref[dr, h] = st * p["el"][:, sl] + _tn(p["v"][:, sl], p["k2"][:, sl])

    def col(order, blkcol):
        return pl.BlockSpec((c, hw), lambda s: (order(s, nc, nt), blkcol))

    def st_spec(order):
        return pl.BlockSpec((1, HG_HEADS, d, d), lambda s: (order(s, nc, nt), 0, 0, 0))

    in_specs = []
    for dr, order in enumerate(orders):
        in_specs += [col(order, 0), col(order, 1 + dr), col(order, 3)]
    in_specs.append(_full(lbraw.shape))
    return _host_call(
        body, ex, lambda: pl.program_id(0) == 0, lambda: pl.program_id(0) == nt - 1,
        name=name, grid=(nt,), in_specs=in_specs,
        out_specs=[col(_fw_chunk, 0), col(_bw_chunk, 0), st_spec(_fw_chunk), st_spec(_bw_chunk)],
        out_shape=[jax.ShapeDtypeStruct((T, hw), F32), jax.ShapeDtypeStruct((T, hw), F32),
                   jax.ShapeDtypeStruct((nt, HG_HEADS, d, d), ACT), jax.ShapeDtypeStruct((nt, HG_HEADS, d, d), ACT)],
        scratch_shapes=[pltpu.VMEM((2, HG_HEADS, d, d), F32)], sem=("arbitrary",),
        args=(pb, pb, pb, pb, pb, pb, lbraw))


def _gla_bwd(pb, lbraw, s_fw, s_bw, do, lc, name, ex=None):
    T = pb.shape[0]
    c, hw, d = HG_CHUNK, HG_HEADS * HG_D, HG_D
    nt, nc = T // c, lc // c

    def rfw(s, nc_, nt_):
        return _fw_chunk(nt_ - 1 - s, nc_, nt_)

    def rbw(s, nc_, nt_):
        return _bw_chunk(nt_ - 1 - s, nc_, nt_)

    def body(qf, zf, vf, sf, dof, qb, zb, vb, sb, dob_, lb_ref,
             dqf, dzf, dvf, dqb, dzb, dvb, dlb_ref, dst_ref):
        step = pl.program_id(0)

        @pl.when(step == 0)
        def _():
            dst_ref[...] = jnp.zeros_like(dst_ref)

        lb = _lower_bound(lb_ref)
        sets = ((qf, zf, vf, sf, dof, dqf, dzf, dvf), (qb, zb, vb, sb, dob_, dqb, dzb, dvb))
        combos = [(dr, h, slice(h * d, (h + 1) * d)) for dr in range(2) for h in range(HG_HEADS)]
        prep = []
        for dr, (q_ref, z_ref, v_ref, _, do_ref, _, _, _) in enumerate(sets):
            rev = dr == 1
            qr = q_ref[...]
            sq = _sig(qr)
            q = qr * sq
            sg, f, cum, ref, last, last_row = _gla_gates(z_ref[...], lb, rev)
            k = 1.0 - f
            e_qr, e_kr, e_q, e_kl = jnp.exp(cum - ref), jnp.exp(ref - cum), jnp.exp(cum), jnp.exp(last - cum)
            q1, k1, q2, k2 = q * e_qr, k * e_kr, q * e_q, k * e_kl
            prep.append(dict(qr=qr, sq=sq, sg=sg, f=f, e_qr=e_qr, e_kr=e_kr, e_q=e_q, e_kl=e_kl, el=jnp.exp(last),
                             q1=q1, k1=k1, q2=q2, k2=k2, q1b=_bf(q1), k1b=_bf(k1), q2b=_bf(q2), k2b=_bf(k2),
                             vb=_bf(v_ref[...]), dob=_bf(do_ref[...]), mask=_tri(c, rev), last_row=last_row,
                             acc_t=jnp.where(_tri(c, not rev), 1.0, 0.0).astype(BF16)))
        a = [_bf(jnp.where(prep[dr]["mask"], _nt(prep[dr]["q1b"][:, sl], prep[dr]["k1b"][:, sl]), 0.0)) for dr, _, sl in combos]
        da = [_bf(jnp.where(prep[dr]["mask"], _nt(prep[dr]["dob"][:, sl], prep[dr]["vb"][:, sl]), 0.0)) for dr, _, sl in combos]
        parts = [dict(dq1=[], dk1=[], dq2=[], dk2=[], dls=[]) for _ in range(2)]
        for (dr, h, sl), a_h, da_h in zip(combos, a, da):
            p = prep[dr]
            s_ref, dv_ref = sets[dr][3], sets[dr][7]
            stb = s_ref[0, h]
            dst = dst_ref[dr, h]
            dstb = _bf(dst)
            dob_h, vb_h = p["dob"][:, sl], p["vb"][:, sl]
            dv_ref[:, sl] = _tn(a_h, dob_h) + _nt(p["k2b"][:, sl], dstb)
            parts[dr]["dq1"].append(_nn(da_h, p["k1b"][:, sl]))
            parts[dr]["dk1"].append(_tn(da_h, p["q1b"][:, sl]))
            parts[dr]["dq2"].append(_nn(dob_h, stb))
            parts[dr]["dk2"].append(_nn(vb_h, dstb))
            el_h = p["el"][:, sl]
            dst_ref[dr, h] = _tn(dob_h, p["q2b"][:, sl]) + dst * el_h
            parts[dr]["dls"].append(jnp.sum(dst * stb.astype(F32), axis=0, keepdims=True) * el_h)
        dlb_tot = jnp.zeros((1, hw), F32)
        for dr in range(2):
            p = prep[dr]
            dq_ref, dz_ref = sets[dr][5], sets[dr][6]
            dq1, dk1, dq2, dk2, dls = (jnp.concatenate(parts[dr][n], axis=1) for n in ("dq1", "dk1", "dq2", "dk2", "dls"))
            dq = dq1 * p["e_qr"] + dq2 * p["e_q"]
            dk = dk1 * p["e_kr"] + dk2 * p["e_kl"]
            dcum = dq1 * p["q1"] - dk1 * p["k1"] + dq2 * p["q2"] - dk2 * p["k2"]
            dlast = jnp.sum(dk2 * p["k2"], axis=0, keepdims=True) + dls
            rowid = lax.broadcasted_iota(jnp.int32, (c, 1), 0)
            dcum = dcum + jnp.where(rowid == p["last_row"], dlast, 0.0)
            df = _nn3(p["acc_t"], dcum) / p["f"] - dk
            sg = p["sg"]
            dz_ref[...] = df * (1.0 - lb) * sg * (1.0 - sg)
            dlb_tot = dlb_tot + jnp.sum(df * (1.0 - sg), axis=0, keepdims=True)
            dq_ref[...] = dq * (p["sq"] * (1.0 + p["qr"] * (1.0 - p["sq"])))
        _acc_all(dlb_ref, step, dlb_tot)

    def col(order, blkcol):
        return pl.BlockSpec((c, hw), lambda s: (order(s, nc, nt), blkcol))

    def st_spec(order):
        return pl.BlockSpec((1, HG_HEADS, d, d), lambda s: (order(s, nc, nt), 0, 0, 0))

    in_specs = []
    for dr, order in enumerate((rfw, rbw)):
        in_specs += [col(order, 0), col(order, 1 + dr), col(order, 3), st_spec(order), col(order, 0)]
    in_specs.append(_full(lbraw.shape))
    out_specs = [col(rfw, 0)] * 3 + [col(rbw, 0)] * 3 + [_whole((1, hw))]
    out_shape = [jax.ShapeDtypeStruct((T, hw), F32)] * 6 + [jax.ShapeDtypeStruct((1, hw), F32)]
    return _host_call(
        body, ex, lambda: pl.program_id(0) == 0, lambda: pl.program_id(0) == nt - 1,
        name=name, grid=(nt,), in_specs=in_specs, out_specs=out_specs, out_shape=out_shape,
        scratch_shapes=[pltpu.VMEM((2, HG_HEADS, d, d), F32)], sem=("arbitrary",),
        args=(pb, pb, pb, s_fw, do, pb, pb, pb, s_bw, do, lbraw))


def _ret_log_gamma(h, rev):
    hh = RET_HEADS - 1 - h if rev else h
    return math.log(1.0 - 2.0 ** (-5.0 - hh))


def _rope(x, cos, sin):
    half = x.shape[1] // 2
    x1, x2 = x[:, :half], x[:, half:]
    return jnp.concatenate([x1 * cos - x2 * sin, x2 * cos + x1 * sin], axis=1)


def _unrope(dy, cos, sin):
    half = dy.shape[1] // 2
    d1, d2 = dy[:, :half], dy[:, half:]
    return jnp.concatenate([d1 * cos + d2 * sin, d2 * cos - d1 * sin], axis=1)


def _ret_decays(lg, rev):
    c = RET_CHUNK
    r = lax.broadcasted_iota(jnp.int32, (c, c), 0)
    k = lax.broadcasted_iota(jnp.int32, (c, c), 1)
    rel = (k - r) if rev else (r - k)
    dm = jnp.where(rel >= 0, jnp.exp(lg * jnp.maximum(rel, 0).astype(F32)), 0.0)
    pos = lax.broadcasted_iota(jnp.int32, (c, 1), 0).astype(F32)
    if rev:
        qdec = jnp.exp(lg * (c - pos))
        kdec = jnp.exp(lg * pos)
    else:
        qdec = jnp.exp(lg * (pos + 1.0))
        kdec = jnp.exp(lg * (c - 1.0 - pos))
    return dm, qdec, kdec


def _ret_fwd(q, k, v, cos, sin, lc, name, ex=None):
    T = q.shape[0]
    c, dk, dv = RET_CHUNK, RET_DK, RET_DV
    nt, nc = T // c, lc // c
    kscale = dk ** -0.5

    def body(qf, kf, vf, cf, sf_, qb, kb, vb, cb, sb_, of_ref, ob_ref, stf_ref, stb_ref, st_ref):
        @pl.when(pl.program_id(0) == 0)
        def _():
            st_ref[...] = jnp.zeros_like(st_ref)

        sets = ((qf, kf, vf, cf, sf_, of_ref, stf_ref), (qb, kb, vb, cb, sb_, ob_ref, stb_ref))
        for dr, (q_ref, k_ref, v_ref, c_ref, s_ref, o_ref, so_ref) in enumerate(sets):
            rev = dr == 1
            cos_v, sin_v = c_ref[...], s_ref[...]
            for h in range(RET_HEADS):
                lg = _ret_log_gamma(h, rev)
                dm, qdec, kdec = _ret_decays(lg, rev)
                qh = _rope(q_ref[:, h * dk:(h + 1) * dk], cos_v, sin_v)
                kh = _rope(k_ref[:, h * dk:(h + 1) * dk], cos_v, sin_v) * kscale
                vh = _bf(v_ref[:, h * dv:(h + 1) * dv])
                st = st_ref[dr, h]
                stb = _bf(st)
                so_ref[0, h] = stb
                sc = _nt(_bf(qh), _bf(kh)) * dm
                o_ref[:, h * dv:(h + 1) * dv] = _nn(_bf(sc), vh) + _nt(_bf(qh * qdec), stb)
                st_ref[dr, h] = st * math.exp(lg * c) + _tn(vh, _bf(kh * kdec))

    def spec(order, width):
        return pl.BlockSpec((c, width), lambda s: (order(s, nc, nt), 0))

    def st_spec(order):
        return pl.BlockSpec((1, RET_HEADS, dv, dk), lambda s: (order(s, nc, nt), 0, 0, 0))

    in_specs = []
    for order in (_fw_chunk, _bw_chunk):
        in_specs += [spec(order, RET_HEADS * dk), spec(order, RET_HEADS * dk), spec(order, RET_HEADS * dv),
                     spec(order, dk // 2), spec(order, dk // 2)]
    return _host_call(
        body, ex, lambda: pl.program_id(0) == 0, lambda: pl.program_id(0) == nt - 1,
        name=name, grid=(nt,), in_specs=in_specs,
        out_specs=[spec(_fw_chunk, RET_HEADS * dv), spec(_bw_chunk, RET_HEADS * dv), st_spec(_fw_chunk), st_spec(_bw_chunk)],
        out_shape=[jax.ShapeDtypeStruct((T, RET_HEADS * dv), F32), jax.ShapeDtypeStruct((T, RET_HEADS * dv), F32),
                   jax.ShapeDtypeStruct((nt, RET_HEADS, dv, dk), ACT), jax.ShapeDtypeStruct((nt, RET_HEADS, dv, dk), ACT)],
        scratch_shapes=[pltpu.VMEM((2, RET_HEADS, dv, dk), F32)], sem=("arbitrary",),
        args=(q, k, v, cos, sin, q, k, v, cos, sin))


def _ret_bwd(q, k, v, cos, sin, s_fw, s_bw, do, lc, name, ex=None):
    T = q.shape[0]
    c, dk, dv = RET_CHUNK, RET_DK, RET_DV
    nt, nc = T // c, lc // c
    kscale = dk ** -0.5

    def rfw(s, nc_, nt_):
        return _fw_chunk(nt_ - 1 - s, nc_, nt_)

    def rbw(s, nc_, nt_):
        return _bw_chunk(nt_ - 1 - s, nc_, nt_)

    def body(qf, kf, vf, cf, sf_, stf, dof, qb, kb, vb, cb, sb_, stb_, dob_,
             dqf, dkf, dvf, dqb, dkb, dvb, dst_ref):
        @pl.when(pl.program_id(0) == 0)
        def _():
            dst_ref[...] = jnp.zeros_like(dst_ref)

        sets = ((qf, kf, vf, cf, sf_, stf, dof, dqf, dkf, dvf), (qb, kb, vb, cb, sb_, stb_, dob_, dqb, dkb, dvb))
        for dr, (q_ref, k_ref, v_ref, c_ref, s_ref, st_in, do_ref, dq_ref, dk_ref, dv_ref) in enumerate(sets):
            rev = dr == 1
            cos_v, sin_v = c_ref[...], s_ref[...]
            for h in range(RET_HEADS):
                lg = _ret_log_gamma(h, rev)
                dm, qdec, kdec = _ret_decays(lg, rev)
                qh = _rope(q_ref[:, h * dk:(h + 1) * dk], cos_v, sin_v)
                kh = _rope(k_ref[:, h * dk:(h + 1) * dk], cos_v, sin_v) * kscale
                vh = _bf(v_ref[:, h * dv:(h + 1) * dv])
                qb16, kb16 = _bf(qh), _bf(kh)
                qinb, kinb = _bf(qh * qdec), _bf(kh * kdec)
                dob = _bf(do_ref[:, h * dv:(h + 1) * dv])
                stb = st_in[0, h]
                dst = dst_ref[dr, h]
                dstb = _bf(dst)
                sc = _bf(_nt(qb16, kb16) * dm)
                dsc = _bf(_nt(dob, vh) * dm)
                dq_r = _nn(dsc, kb16) + _nn(dob, stb) * qdec
                dk_r = _tn(dsc, qb16) + _nn(vh, dstb) * kdec
                dv_ref[:, h * dv:(h + 1) * dv] = _tn(sc, dob) + _nt(kinb, dstb)
                dst_ref[dr, h] = _tn(dob, qinb) + dst * math.exp(lg * c)
                dq_ref[:, h * dk:(h + 1) * dk] = _unrope(dq_r, cos_v, sin_v)
                dk_ref[:, h * dk:(h + 1) * dk] = _unrope(dk_r * kscale, cos_v, sin_v)

    def spec(order, width):
        return pl.BlockSpec((c, width), lambda s: (order(s, nc, nt), 0))

    def st_spec(order):
        return pl.BlockSpec((1, RET_HEADS, dv, dk), lambda s: (order(s, nc, nt), 0, 0, 0))

    in_specs = []
    for order in (rfw, rbw):
        in_specs += [spec(order, RET_HEADS * dk), spec(order, RET_HEADS * dk), spec(order, RET_HEADS * dv),
                     spec(order, dk // 2), spec(order, dk // 2), st_spec(order), spec(order, RET_HEADS * dv)]
    out_specs, out_shape = [], []
    for order in (rfw, rbw):
        out_specs += [spec(order, RET_HEADS * dk), spec(order, RET_HEADS * dk), spec(order, RET_HEADS * dv)]
        out_shape += [jax.ShapeDtypeStruct((T, RET_HEADS * dk), F32), jax.ShapeDtypeStruct((T, RET_HEADS * dk), F32),
                      jax.ShapeDtypeStruct((T, RET_HEADS * dv), F32)]
    return _host_call(
        body, ex, lambda: pl.program_id(0) == 0, lambda: pl.program_id(0) == nt - 1,
        name=name, grid=(nt,), in_specs=in_specs, out_specs=out_specs, out_shape=out_shape,
        scratch_shapes=[pltpu.VMEM((2, RET_HEADS, dv, dk), F32)], sem=("arbitrary",),
        args=(q, k, v, cos, sin, s_fw, do, q, k, v, cos, sin, s_bw, do))


def _attn_rope_tables(lc, l):
    t = jnp.arange(l)
    row = (t // GRID_W).astype(F32)
    colp = (t % GRID_W).astype(F32)
    n_freq = HEAD_DIM // 4
    inv = 10000.0 ** (-jnp.arange(n_freq, dtype=F32) / n_freq)
    ang = jnp.concatenate([row[:, None] * inv, colp[:, None] * inv], axis=-1)
    cos = jnp.concatenate([jnp.ones((lc, HEAD_DIM // 2), F32), jnp.cos(ang)], axis=0)
    sin = jnp.concatenate([jnp.zeros((lc, HEAD_DIM // 2), F32), jnp.sin(ang)], axis=0)
    return jnp.concatenate([cos, cos], axis=1), jnp.concatenate([-sin, sin], axis=1)


def _ret_rope_tables(lc, l):
    theta = 1.0 / (10000.0 ** jnp.linspace(0.0, 1.0, RET_DK // 2, dtype=F32))
    ang = jnp.arange(l, dtype=F32)[:, None] * theta
    cos = jnp.concatenate([jnp.ones((lc, RET_DK // 2), F32), jnp.cos(ang)], axis=0)
    sin = jnp.concatenate([jnp.zeros((lc, RET_DK // 2), F32), jnp.sin(ang)], axis=0)
    return cos, sin


def _heads_major(slab, n_heads):
    t = slab.shape[0]
    return slab.reshape(t, n_heads, HEAD_DIM).transpose(1, 0, 2)


def _slab(hm):
    nh, t, hd = hm.shape
    return hm.transpose(1, 0, 2).reshape(t, nh * hd)


COL_SHARDED = ("ffn_in0", "ffn_in1", "even_in", "odd_in")


def _full_weight(name, g):
    if name in COL_SHARDED:
        return g.transpose(1, 0, 2).reshape(g.shape[1], -1)
    return g.reshape(-1, g.shape[2])


def _shard_slots(name, g):
    if name in COL_SHARDED:
        return g.reshape(g.shape[0], N_DEV, -1).transpose(1, 0, 2)
    return g.reshape(N_DEV, -1, g.shape[1])


def _local_step(xs, target, mv, norm_g, w, qk_g, sink, hg_out_g, lbraw, lc, shards=None):
    T, dm = xs.shape
    l = T - lc
    tm = lc
    blk = ATTN_BLOCK
    d2, d3 = 2 * dm, 3 * dm
    w = dict(w)
    gw, recv = {}, {}

    def ms(layer, a, b):
        return mv[layer, :, :, a:b]

    def gather(names):
        return None if shards is None else _Exchange(GATHER, [shards[n] for n in names])

    def arrived(names, got):
        for n, g in zip(names, got):
            w[n] = _full_weight(n, g)

    def scatter(names):
        return None if shards is None else _Exchange(SCATTER, [_shard_slots(n, gw[n]) for n in names])

    def scattered(names, got):
        for n, g in zip(names, got):
            recv[n] = g

    g00, g01, g10, g11 = (norm_g[i, j][None, :] for i in (0, 1) for j in (0, 1))

    riding = ["even_out", "ffn_out0"]
    (pa, pb), got = _pre_fwd(xs, g00, ms(0, 0, d2), w["even_in"], ((0, 768), (768, 3328)), tm, "pre0_fwd", gather(riding))
    arrived(riding, got)
    cos2, sin2 = _attn_rope_tables(lc, l)
    cosp, sinp = jnp.concatenate([cos2, cos2], axis=1), jnp.concatenate([sin2, sin2], axis=1)
    gains5 = jnp.concatenate([jnp.broadcast_to(jnp.tile(qk_g[0], 2), (N_PAIRS - 1, PAIR)), jnp.tile(qk_g[1], 2)[None]])[:, None, :]
    qt, ks, vs = _qk_slab_fwd(pa, gains5, cosp, sinp, tm, "qk_prep_fwd")
    sinkb = jnp.broadcast_to(sink.reshape(ATTN_KV, 4, 1, 1), (ATTN_KV, 4, blk, 1)).reshape(ATTN_KV, 4 * blk, 1)
    riding = ["ffn_in0", "odd_out"]
    (a_slab, lse), got = _attn_slab_fwd(qt, ks, vs, sinkb, lc, "attn_fwd", gather(riding))
    arrived(riding, got)
    (hg_of, hg_ob, hg_sf, hg_sb), got = _gla_fwd(pb, lbraw, lc, "hgrn_fwd", gather(["odd_in"]))
    arrived(["odd_in"], got)
    x01, z0 = _post_fwd(xs, hg_of, hg_ob, pb, 4, hg_out_g, a_slab, w["even_out"], ms(0, d2, d3), HG_D, tm, "post0_fwd")
    x02, u0, f0 = _ffn_fwd(x01, g01, ms(0, d3, 6 * dm), w["ffn_in0"], w["ffn_out0"], tm, "ffn0_fwd")

    (rq, rk, rv, rg), _ = _pre_fwd(x02, g10, ms(1, 0, d2), w["odd_in"],
                                   ((0, 1024), (1024, 2048), (2048, 4096), (4096, 6144)), tm, "pre1_fwd")
    rcos, rsin = _ret_rope_tables(lc, l)
    (rt_of, rt_ob, rt_sf, rt_sb), got = _ret_fwd(rq, rk, rv, rcos, rsin, lc, "ret_fwd", gather(["ffn_in1", "ffn_out1"]))
    arrived(["ffn_in1", "ffn_out1"], got)
    x11, z1 = _post_fwd(x02, rt_of, rt_ob, rg, 0, None, None, w["odd_out"], ms(1, d2, d3), RET_DV, tm, "post1_fwd")
    x12, u1, f1 = _ffn_fwd(x11, g11, ms(1, d3, 6 * dm), w["ffn_in1"], w["ffn_out1"], tm, "ffn1_fwd")

    dx, loss = _loss_bwd(x12, target, tm, "loss_bwd")

    dx, h, du, act, df, dms_f1, dg11 = _ffn_bwd(x11, dx, u1, f1, g11, ms(1, d3, 6 * dm), w["ffn_in1"], w["ffn_out1"], tm, "ffn1_bwd")
    gw["ffn_in1"] = _wgrad(h, du, "wg_ffn_in1")
    gw["ffn_out1"] = _wgrad(act, df, "wg_ffn_out1")
    do1, dgr1, dy1, z1_t, dgate_p1, _ = _post_bwd(dx, z1, rt_of, rt_ob, rg, 0, None, w["odd_out"], ms(1, d2, d3), 0, RET_DV, tm,
                                                  "post1_bwd")
    gw["odd_out"] = _wgrad(z1_t, dy1, "wg_odd_out")
    riding = ["ffn_in1", "ffn_out1", "odd_out"]
    (dqf, dkf, dvf, dqb, dkb, dvb), got = _ret_bwd(rq, rk, rv, rcos, rsin, rt_sf, rt_sb, do1, lc, "ret_bwd", scatter(riding))
    scattered(riding, got)
    dx, h, dp, dms_p1, dg10 = _pre_bwd(x02, dx, g10, ms(1, 0, d2), w["odd_in"],
                                       [(0, [dqf, dqb]), (1024, [dkf, dkb]), (2048, [dvf, dvb]), (4096, [dgr1])], tm, "pre1_bwd")
    gw["odd_in"] = _wgrad(h, dp, "wg_odd_in")

    dx, h, du, act, df, dms_f0, dg01 = _ffn_bwd(x01, dx, u0, f0, g01, ms(0, d3, 6 * dm), w["ffn_in0"], w["ffn_out0"], tm, "ffn0_bwd")
    gw["ffn_in0"] = _wgrad(h, du, "wg_ffn_in0")
    gw["ffn_out0"] = _wgrad(act, df, "wg_ffn_out0")
    do0, dgr0, da0, dy0, z0_t, dgate_p0, d_hg_gain = _post_bwd(dx, z0, hg_of, hg_ob, pb, 4, hg_out_g, w["even_out"],
                                                              ms(0, d2, d3), 512, HG_D, tm, "post0_bwd")
    gw["even_out"] = _wgrad(z0_t, dy0, "wg_even_out")
    riding = ["odd_in", "ffn_in0"]
    (hq_f, hz_f, hv_f, hq_b, hz_b, hv_b, dlb), got = _gla_bwd(pb, lbraw, hg_sf, hg_sb, do0, lc, "hgrn_bwd", scatter(riding))
    scattered(riding, got)
    riding = ["ffn_out0", "even_out"]
    (dq_att, dk_att, dv_att, dsink), got = _attn_slab_bwd(qt, ks, vs, sinkb, a_slab, lse, da0, lc, "attn_bwd", scatter(riding))
    scattered(riding, got)
    dqk_raw, dgain5 = _qk_slab_bwd(dq_att, dk_att, pa, gains5, cosp, sinp, tm, "qk_prep_bwd")
    pieces0 = [(0, [dqk_raw]), (640, [dv_att]),
               (768, [hq_f, hq_b]), (1280, [hz_f]), (1792, [hz_b]), (2304, [hv_f, hv_b]), (2816, [dgr0])]
    dx, h, dp, dms_p0, dg00 = _pre_bwd(xs, dx, g00, ms(0, 0, d2), w["even_in"], pieces0, tm, "pre0_bwd")
    gw["even_in"] = _wgrad(h, dp, "wg_even_in")

    dmv = jnp.stack([jnp.concatenate([dms_p0, dgate_p0, dms_f0], axis=2), jnp.concatenate([dms_p1, dgate_p1, dms_f1], axis=2)])
    small = {
        "dmv": dmv,
        "norm_g": jnp.stack([jnp.stack([dg00[0], dg01[0]]), jnp.stack([dg10[0], dg11[0]])]),
        "qk_g": jnp.stack([jnp.sum(dgain5[:N_PAIRS - 1, 0].reshape(-1, HEAD_DIM), axis=0),
                           jnp.sum(dgain5[N_PAIRS - 1, 0].reshape(-1, HEAD_DIM), axis=0)]),
        "sink": dsink.reshape(ATTN_HEADS),
        "hg_out_g": d_hg_gain[0],
        "lb": dlb[0],
        "loss": loss[0, 0],
    }
    if shards is not None:
        gw = {n: recv.get(n, g) for n, g in gw.items()}
    return loss, dx, gw, small


HBM_SPEC = pl.BlockSpec(memory_space=pltpu.HBM)


def _my_index():
    return 4 * lax.axis_index("x") + 2 * lax.axis_index("y") + lax.axis_index("c")


def _peer(k):
    pos = []
    for axis, bit in (("x", 4), ("y", 2), ("c", 1)):
        a = lax.axis_index(axis)
        pos.append(1 - a if k & bit else a)
    return tuple(pos)


def _peer_index(k):
    px, py, pc = _peer(k)
    return 4 * px + 2 * py + pc


GATHER, SCATTER = "gather", "scatter"


class _Exchange:
    def __init__(self, mode, arrays):
        self.mode, self.arrays, self.n = mode, list(arrays), len(arrays)

    def out_shape(self):
        if self.mode == GATHER:
            return [jax.ShapeDtypeStruct((N_DEV,) + a.shape, a.dtype) for a in self.arrays]
        return [jax.ShapeDtypeStruct(a.shape, a.dtype) for a in self.arrays]

    def specs(self):
        return [HBM_SPEC] * self.n

    def scratch(self):
        return [pltpu.SemaphoreType.DMA((self.n, N_DEV - 1)), pltpu.SemaphoreType.DMA((self.n, N_DEV - 1)),
                pltpu.SemaphoreType.DMA((self.n,))]

    def _copies(self, in_refs, out_refs, send_sems, recv_sems, local_sems, landing):
        me = _my_index()
        local, remote = [], []
        for a, (src, dst) in enumerate(zip(in_refs, out_refs)):
            part = (lambda j, s=src: s) if self.mode == GATHER else (lambda j, s=src: s.at[j])
            local.append(pltpu.make_async_copy(part(me), dst.at[me], local_sems.at[a]))
            for k in range(1, N_DEV):
                pj = _peer_index(k)
                remote.append(pltpu.make_async_remote_copy(
                    src_ref=part(pj), dst_ref=dst.at[pj if landing else me], send_sem=send_sems.at[a, k - 1],
                    recv_sem=recv_sems.at[a, k - 1], device_id=_peer(k), device_id_type=MESH))
        return local, remote

    def start(self, in_refs, out_refs, sems):
        local, remote = self._copies(in_refs, out_refs, *sems, landing=False)
        for cp in local + remote:
            cp.start()

    def wait(self, in_refs, out_refs, sems):
        local, remote = self._copies(in_refs, out_refs, *sems, landing=True)
        for cp in remote:
            cp.wait_send()
            cp.wait_recv()
        for cp in local:
            cp.wait()

    def ride(self, refs, n_in, n_out, first, last):
        refs = list(refs)
        n = self.n
        x_in = refs[n_in:n_in + n]
        x_out = refs[n_in + n + n_out:n_in + 2 * n + n_out]
        sems = refs[n_in + 2 * n + n_out:n_in + 2 * n + n_out + 3]

        @pl.when(first)
        def _():
            self.start(x_in, x_out, sems)

        @pl.when(last)
        def _():
            self.wait(x_in, x_out, sems)

        return refs[:n_in] + refs[n_in + n:n_in + n + n_out] + refs[n_in + 2 * n + n_out + 3:]

    def call(self, name):
        n = self.n

        def body(*refs):
            ins, outs, sems = refs[:n], refs[n:2 * n], refs[2 * n:]
            self.start(ins, outs, sems)
            self.wait(ins, outs, sems)

        return pl.pallas_call(body, name=name, in_specs=self.specs(), out_specs=self.specs(), out_shape=self.out_shape(),
                              scratch_shapes=self.scratch())(*self.arrays)


def _all_gather(v, name):
    return _Exchange(GATHER, [v]).call(name)[0]


def _hosted(kernel_body, ex, n_in, n_out, first, last):
    if ex is None:
        return kernel_body

    def body(*refs):
        kernel_body(*ex.ride(refs, n_in, n_out, first(), last()))

    return body


def _host_call(kernel_body, ex, first, last, name, grid, in_specs, out_specs, out_shape, scratch_shapes, sem, args):
    n_in, n_out = len(in_specs), len(out_specs)
    if ex is None:
        outs = pl.pallas_call(kernel_body, name=name, grid=grid, in_specs=in_specs, out_specs=out_specs, out_shape=out_shape,
                              scratch_shapes=scratch_shapes, compiler_params=_cp(*sem))(*args)
        return list(outs), []
    outs = pl.pallas_call(
        _hosted(kernel_body, ex, n_in, n_out, first, last), name=name, grid=grid,
        in_specs=list(in_specs) + ex.specs(), out_specs=list(out_specs) + ex.specs(),
        out_shape=list(out_shape) + ex.out_shape(), scratch_shapes=ex.scratch() + list(scratch_shapes),
        compiler_params=_cp(*sem))(*args, *ex.arrays)
    return list(outs[:n_out]), list(outs[n_out:])


def _mod_fwd(call, mod_w, bias, name):
    nl, dm, n = mod_w.shape

    def body(c_ref, w_ref, b_ref, o_ref):
        cv = c_ref[...]
        cond = _bf(cv * _sig(cv))
        for layer in range(nl):
            o_ref[layer] = _nn(cond, _bf(w_ref[layer])) + b_ref[layer]

    return pl.pallas_call(
        body, name=name, out_shape=jax.ShapeDtypeStruct((nl, call.shape[0], n), F32),
        compiler_params=pltpu.CompilerParams(vmem_limit_bytes=VMEM_LIMIT),
    )(call, mod_w, bias)


def _mod_bwd(call, dm_all, mod_w, name):
    nl, dm, n = mod_w.shape

    def body(c_ref, d_ref, w_ref, gw_ref, dc_ref):
        cv = c_ref[...]
        cond = _bf(cv * _sig(cv))
        dc = jnp.zeros(cv.shape, F32)
        for layer in range(nl):
            db = _bf(d_ref[layer])
            gw_ref[layer] = _tn(cond, db)
            dc = dc + _nt(db, _bf(w_ref[layer]))
        dc_ref[...] = dc

    return pl.pallas_call(
        body, name=name,
        out_shape=[jax.ShapeDtypeStruct(mod_w.shape, F32), jax.ShapeDtypeStruct(call.shape, F32)],
        compiler_params=pltpu.CompilerParams(vmem_limit_bytes=VMEM_LIMIT),
    )(call, dm_all, mod_w)


def _sum_parts(g, name):
    def body(g_ref, o_ref):
        acc = g_ref[0]
        for j in range(1, g.shape[0]):
            acc = acc + g_ref[j]
        o_ref[...] = acc

    return pl.pallas_call(body, name=name, out_shape=jax.ShapeDtypeStruct(g.shape[1:], g.dtype))(g)


def _small_finish(dcond_g, c_ctx, dlb, lbraw, dm_ctx, dm_lat, name):
    def body(dc_ref, c_ref, dlb_ref, lb_ref, mc_ref, ml_ref, gc_ref, glb_ref, gb_ref):
        acc = dc_ref[0, 0:1, :]
        for j in range(1, N_DEV):
            acc = acc + dc_ref[j, 0:1, :]
        cv = c_ref[...]
        s = _sig(cv)
        gc_ref[...] = acc * (s * (1.0 + cv * (1.0 - s)))
        lb = _lower_bound(lb_ref)
        d0 = dlb_ref[...] * lb * (1.0 - lb)
        glb_ref[0:1, :] = d0
        glb_ref[1:2, :] = -d0
        gb_ref[...] = mc_ref[...] + ml_ref[...]

    return pl.pallas_call(
        body, name=name,
        out_shape=[jax.ShapeDtypeStruct(c_ctx.shape, F32), jax.ShapeDtypeStruct(lbraw.shape, F32),
                   jax.ShapeDtypeStruct(dm_ctx.shape, F32)],
    )(dcond_g, c_ctx, dlb, lbraw, dm_ctx, dm_lat)


def _row_tile(r, cap, mult):
    best = r
    for t in range(mult, min(r, cap) + 1, mult):
        if r % t == 0:
            best = t
    return best


def _adam(g_list, w, m, v, name, ex=None):
    nl, r, cdim = w.shape
    p = g_list[0].shape[0]
    tr = _row_tile(r, 128, 16)
    ni = r // tr

    def body(*refs):
        g_refs = refs[:nl]
        w_ref, m_ref, v_ref, go_ref, d_ref, mo_ref, vo_ref = refs[nl:]
        layer = pl.program_id(0)

        def total(g_ref):
            acc = g_ref[0].astype(F32)
            for j in range(1, p):
                acc = acc + g_ref[j].astype(F32)
            return acc

        g = total(g_refs[0])
        for k in range(1, nl):
            g = jnp.where(layer == k, total(g_refs[k]), g)
        m2 = ADAM_B1 * m_ref[0] + (1.0 - ADAM_B1) * g
        v2 = ADAM_B2 * v_ref[0] + (1.0 - ADAM_B2) * (g * g)
        m_hat = m2 / (1.0 - ADAM_B1 ** ADAM_STEP)
        v_hat = v2 / (1.0 - ADAM_B2 ** ADAM_STEP)
        go_ref[0] = g
        d_ref[0] = -ADAM_LR * (m_hat / (jnp.sqrt(v_hat) + ADAM_EPS) + ADAM_WD * w_ref[0])
        mo_ref[0] = m2
        vo_ref[0] = v2

    def g_spec(k):
        return pl.BlockSpec((p, tr, cdim), lambda la, i: (0, jnp.where(la == k, i, jnp.where(la < k, 0, ni - 1)), 0))

    spec = pl.BlockSpec((1, tr, cdim), lambda la, i: (la, i, 0))
    return _host_call(
        body, ex, lambda: (pl.program_id(0) == 0) & (pl.program_id(1) == 0),
        lambda: (pl.program_id(0) == nl - 1) & (pl.program_id(1) == ni - 1),
        name=name, grid=(nl, ni),
        in_specs=[g_spec(k) for k in range(nl)] + [spec, spec, spec],
        out_specs=[spec] * 4, out_shape=[jax.ShapeDtypeStruct((nl, r, cdim), F32)] * 4,
        scratch_shapes=[], sem=("arbitrary", "arbitrary"), args=(*g_list, w, m, v))


def _f32_as_rows(a, width):
    return lax.bitcast_convert_type(a.reshape(-1), BF16).reshape(-1, width)


def _rows_as_f32(rows):
    return lax.bitcast_convert_type(rows.reshape(rows.shape[:-2] + (-1, 2)), F32)


def _pad_rows(a, mult):
    r = (-a.shape[-2]) % mult
    if r == 0:
        return a
    widths = [(0, 0)] * (a.ndim - 2) + [(0, r), (0, 0)]
    return jnp.pad(a, widths)


def _pack_flat(parts, lane):
    flat = jnp.concatenate([p.reshape(-1).astype(F32) for p in parts])
    n = flat.shape[0]
    rows = -(-n // lane)
    rows += (-rows) % 8
    return jnp.pad(flat, (0, rows * lane - n)).reshape(rows, lane)


def _unpack_flat(packed, shapes):
    flat = packed.reshape(-1)
    out, off = [], 0
    for s in shapes:
        n = math.prod(s)
        out.append(flat[off:off + n].reshape(s))
        off += n
    return out


def kernel(x, c, ctx, c_ctx, mod_w, mod_b, norm_g, ffn_w_in, ffn_w_out, even_w_in, even_w_out, attn_qk_norm_g, attn_sink, hgrn_out_norm_g, hgrn_lb, odd_w_in, odd_w_out, loss_target, m_c_ctx, m_mod_w, m_mod_b, m_norm_g, m_ffn_w_in, m_ffn_w_out, m_even_w_in, m_even_w_out, m_attn_qk_norm_g, m_attn_sink, m_hgrn_out_norm_g, m_hgrn_lb, m_odd_w_in, m_odd_w_out, v_c_ctx, v_mod_w, v_mod_b, v_norm_g, v_ffn_w_in, v_ffn_w_out, v_even_w_in, v_even_w_out, v_attn_qk_norm_g, v_attn_sink, v_hgrn_out_norm_g, v_hgrn_lb, v_odd_w_in, v_odd_w_out):
    me = _my_index()
    lc, dm = ctx.shape[1], x.shape[2]
    nmod = mod_w.shape[2]
    big = (ffn_w_in, ffn_w_out, even_w_in, even_w_out, odd_w_in, odd_w_out)

    extra = _pad_rows(jnp.concatenate([_f32_as_rows(c, dm), _f32_as_rows(norm_g, dm)], axis=0), 16)
    shards = {"ffn_in0": ffn_w_in[0], "ffn_in1": ffn_w_in[1], "ffn_out0": ffn_w_out[0], "ffn_out1": ffn_w_out[1],
              "even_in": even_w_in[0], "even_out": even_w_out[0], "odd_in": odd_w_in[0], "odd_out": odd_w_out[0]}
    shards = {n: a.astype(BF16) for n, a in shards.items()}
    first = _Exchange(GATHER, [shards["even_in"], extra]).call("gather_first")
    w = {"even_in": _full_weight("even_in", first[0])}
    c_all = _rows_as_f32(first[1][:, 0:2])
    norm_g_all = _rows_as_f32(first[1][:, 2:3]).reshape(N_DEV, 2, 2, -1)
    norm_g_full = norm_g_all.transpose(1, 2, 0, 3).reshape(2, 2, dm)

    call = jnp.concatenate([c_all, c_ctx[None, :], jnp.zeros((16 - N_DEV - 1, dm), F32)], axis=0)
    bias = lax.dynamic_slice_in_dim(mod_b, me * nmod, nmod, axis=1)[:, None, :]
    m_sh = _mod_fwd(call, mod_w, bias, "mod_fwd")
    m_g = _all_gather(m_sh.reshape(-1, nmod), "gather_mod").reshape(N_DEV, 2, 16, nmod)
    m_all = m_g.transpose(1, 2, 0, 3).reshape(2, 16, -1)
    m_lat = lax.dynamic_index_in_dim(m_all, me, axis=1, keepdims=False)
    mv = jnp.stack([m_all[:, N_DEV], m_lat], axis=1)[:, :, None, :]

    xs = jnp.concatenate([ctx[0], x[0]], axis=0)
    _, dxs, gw, small = _local_step(xs, loss_target[0], mv, norm_g_full, w, attn_qk_norm_g[0], attn_sink[0],
                                    hgrn_out_norm_g, hgrn_lb, lc, shards)
    grad_x = dxs[lc:][None]

    last = _Exchange(SCATTER, [_shard_slots("even_in", gw["even_in"])])
    big_g = [[gw["ffn_in0"], gw["ffn_in1"]], [gw["ffn_out0"], gw["ffn_out1"]], None, [gw["even_out"]],
             [gw["odd_in"]], [gw["odd_out"]]]
    big_m = (m_ffn_w_in, m_ffn_w_out, m_even_w_in, m_even_w_out, m_odd_w_in, m_odd_w_out)
    big_v = (v_ffn_w_in, v_ffn_w_out, v_even_w_in, v_even_w_out, v_odd_w_in, v_odd_w_out)
    big_names = ("ffn_w_in", "ffn_w_out", "even_w_in", "even_w_out", "odd_w_in", "odd_w_out")
    big_out = [None] * 6
    for i in (0, 1, 3, 4, 5, 2):
        big_out[i], got = _adam(big_g[i], big[i], big_m[i], big_v[i], "adam_" + big_names[i], last if i == 0 else None)
        if i == 0:
            big_g[2] = [got[0]]
    big_res = [[big_out[i][k] for i in range(6)] for k in range(4)]

    dmv = small["dmv"]
    small_shapes = [(2, 6 * dm), (2, 6 * dm), (2, 2, dm), (2, HEAD_DIM), (ATTN_HEADS,), (HG_D,), (HG_HEADS * HG_D,), (1,)]
    vec = _pack_flat([dmv[:, 0, 0], dmv[:, 1, 0], small["norm_g"], small["qk_g"], small["sink"], small["hg_out_g"],
                      small["lb"], small["loss"]], 128)
    vec_g = _all_gather(vec, "gather_small")
    tot = _unpack_flat(_sum_parts(vec_g, "sum_small"), small_shapes)
    dm_ctx_tot, dm_lat_tot, g_norm_full, g_qk, g_sink, g_hg, dlb_tot, loss_tot = tot
    dm_lat_each = vec_g.reshape(N_DEV, -1)[:, 12 * dm:24 * dm].reshape(N_DEV, 2, 6 * dm)
    dm_lat_mine = lax.dynamic_slice_in_dim(dm_lat_each, me * nmod, nmod, axis=2).transpose(1, 0, 2)
    dm_ctx_mine = lax.dynamic_slice_in_dim(dm_ctx_tot, me * nmod, nmod, axis=1)[:, None, :]
    dm_all = jnp.concatenate([dm_lat_mine, dm_ctx_mine, jnp.zeros((2, 16 - N_DEV - 1, nmod), F32)], axis=1)
    g_mod_w, dcond = _mod_bwd(call, dm_all, mod_w, "mod_bwd")
    dcond_g = _all_gather(dcond[N_DEV:], "gather_dcond")
    g_c_ctx, g_lb, g_mod_b = _small_finish(dcond_g, c_ctx[None, :], dlb_tot[None, :], hgrn_lb, dm_ctx_tot, dm_lat_tot,
                                           "small_finish")
    g_norm = lax.dynamic_slice_in_dim(g_norm_full, me * norm_g.shape[2], norm_g.shape[2], axis=2)

    mod_res, _ = _adam([g_mod_w[0][None], g_mod_w[1][None]], mod_w, m_mod_w, v_mod_w, "adam_mod_w")

    sm_w = (c_ctx, mod_b, norm_g, attn_qk_norm_g, attn_sink, hgrn_out_norm_g, hgrn_lb)
    sm_m = (m_c_ctx, m_mod_b, m_norm_g, m_attn_qk_norm_g, m_attn_sink, m_hgrn_out_norm_g, m_hgrn_lb)
    sm_v = (v_c_ctx, v_mod_b, v_norm_g, v_attn_qk_norm_g, v_attn_sink, v_hgrn_out_norm_g, v_hgrn_lb)
    sm_g = (g_c_ctx, g_mod_b, g_norm, g_qk, g_sink, g_hg, g_lb)
    sm_shapes = [a.shape for a in sm_w]
    sm_out, _ = _adam([_pack_flat(sm_g, 128)[None]], _pack_flat(sm_w, 128)[None], _pack_flat(sm_m, 128)[None],
                      _pack_flat(sm_v, 128)[None], "adam_small")
    sm_res = [_unpack_flat(o, sm_shapes) for o in sm_out]

    def ordered(k):
        s, b = sm_res[k], big_res[k]
        return [s[0], mod_res[k], s[1], s[2], b[0], b[1], b[2], b[3], s[3], s[4], s[5], s[6], b[4], b[5]]

    return (loss_tot[0], grad_x, *ordered(0), *ordered(1), *ordered(2), *ordered(3))
```

```python
import functools
import math

import jax
import jax.numpy as jnp
from jax import lax
from jax.experimental import pallas as pl
from jax.experimental.pallas import tpu as pltpu

F32 = jnp.float32
BF16 = jnp.bfloat16
EPS = 1e-6
N_DEV = 8
MESH = pl.DeviceIdType.MESH

HEAD_DIM = 64
ATTN_HEADS = 8
ATTN_KV = 2
ATTN_BLOCK = 128
WINDOW = 128
GRID_W = 64
HG_HEADS = 4
HG_D = 128
HG_CHUNK = 64
RET_HEADS = 4
RET_DK = 256
RET_DV = 512
RET_CHUNK = 128
NEG = -1e30

ADAM_LR = 0.001
ADAM_B1 = 0.9
ADAM_B2 = 0.999
ADAM_EPS = 1e-08
ADAM_WD = 0.01
ADAM_STEP = 10

VMEM_LIMIT = 60 * 1024 * 1024


def _cp(*sem):
    return pltpu.CompilerParams(dimension_semantics=sem, vmem_limit_bytes=VMEM_LIMIT)


def _nn(a, b):
    return jnp.dot(a, b, preferred_element_type=F32)


def _nt(a, b):
    return lax.dot_general(a, b, (((1,), (1,)), ((), ())), preferred_element_type=F32)


def _tn(a, b):
    return lax.dot_general(a, b, (((0,), (0,)), ((), ())), preferred_element_type=F32)


ACT = BF16


def _bf(a):
    return a.astype(ACT)


def _sig(x):
    return jax.nn.sigmoid(x)


def _split3(x):
    h = x.astype(BF16)
    r = x - h.astype(F32)
    m = r.astype(BF16)
    lo = (r - m.astype(F32)).astype(BF16)
    return h, m, lo


def _nn3(m01, x):
    h, m, lo = _split3(x)
    return _nn(m01, h) + _nn(m01, m) + _nn(m01, lo)


def _nn3r(x, m01):
    h, m, lo = _split3(x)
    return _nn(h, m01) + _nn(m, m01) + _nn(lo, m01)


def _full(shape):
    nd = len(shape)
    return pl.BlockSpec(shape, lambda *a: (0,) * nd, pipeline_mode=pl.Buffered(1))


def _whole(shape):
    nd = len(shape)
    return pl.BlockSpec(shape, lambda *a: (0,) * nd)


def _rows(tm, width):
    return pl.BlockSpec((tm, width), lambda i: (i, 0))


def _cols(height, tm):
    return pl.BlockSpec((height, tm), lambda i: (0, i))


def _ctx_lat(width):
    return pl.BlockSpec((1, 1, width), lambda i: (jnp.minimum(i, 1), 0, 0))


def _acc_ctx_lat(ref, i, val):
    @pl.when(i <= 1)
    def _():
        ref[...] = val.reshape(ref.shape)

    @pl.when(i > 1)
    def _():
        ref[...] += val.reshape(ref.shape)


def _acc_all(ref, i, val):
    @pl.when(i == 0)
    def _():
        ref[...] = val.reshape(ref.shape)

    @pl.when(i > 0)
    def _():
        ref[...] += val.reshape(ref.shape)


def _tile(n, cap):
    best = None
    for t in range(128, min(n, cap) + 1, 128):
        if n % t == 0:
            best = t
    return n if best is None else best


def _norm_mod(xv, g, shift, scale):
    r = lax.rsqrt(jnp.mean(xv * xv, axis=-1, keepdims=True) + EPS)
    xhat = xv * r
    n = xhat * g
    return r, xhat, n, n * (1.0 + scale) + shift


def _norm_mod_bwd(dh, r, xhat, n, g, scale):
    dshift = jnp.sum(dh, axis=0, keepdims=True)
    dscale = jnp.sum(dh * n, axis=0, keepdims=True)
    dn = dh * (1.0 + scale)
    dg = jnp.sum(dn * xhat, axis=0, keepdims=True)
    dxh = dn * g
    dx = r * (dxh - xhat * jnp.mean(dxh * xhat, axis=-1, keepdims=True))
    return dx, dshift, dscale, dg


def _pre_fwd(x, gain, ms, w, splits, tm, name, ex=None):
    T, dm = x.shape
    nt = T // tm

    def body(x_ref, g_ref, ms_ref, w_ref, *outs):
        ms_v = ms_ref[0]
        h = _norm_mod(x_ref[...], g_ref[...], ms_v[:, :dm], ms_v[:, dm:])[3]
        hb = _bf(h)
        for (s, e), o_ref in zip(splits, outs):
            o_ref[...] = _nn(hb, w_ref[:, s:e])

    return _host_call(
        body, ex, lambda: pl.program_id(0) == 0, lambda: pl.program_id(0) == nt - 1,
        name=name, grid=(nt,),
        in_specs=[_rows(tm, dm), _full((1, dm)), _ctx_lat(2 * dm), _full(w.shape)],
        out_specs=[_rows(tm, e - s) for s, e in splits],
        out_shape=[jax.ShapeDtypeStruct((T, e - s), F32) for s, e in splits],
        scratch_shapes=[], sem=("arbitrary",), args=(x, gain, ms, w))


def _pre_bwd(x, dx_in, gain, ms, w, pieces, tm, name, latent_dx=False, ex=None):
    T, dm = x.shape
    dx_spec = pl.BlockSpec((tm, dm), lambda i: (jnp.maximum(i - 1, 0), 0)) if latent_dx else _rows(tm, dm)
    dx_rows = T - tm if latent_dx else T
    n_out = w.shape[1]
    flat = [a for _, arrs in pieces for a in arrs]

    def body(x_ref, dxin_ref, g_ref, ms_ref, w_ref, *rest):
        p_refs = rest[:len(flat)]
        dx_ref, h_ref, dp_ref, dms_ref, dg_ref = rest[len(flat):]
        i = pl.program_id(0)
        ms_v = ms_ref[0]
        g = g_ref[...]
        scale = ms_v[:, dm:]
        r, xhat, n, h = _norm_mod(x_ref[...], g, ms_v[:, :dm], scale)
        h_ref[...] = _bf(h).T
        dh = jnp.zeros((tm, dm), F32)
        k = 0
        for s, arrs in pieces:
            v = p_refs[k][...].astype(F32)
            for j in range(1, len(arrs)):
                v = v + p_refs[k + j][...].astype(F32)
            k += len(arrs)
            vb = _bf(v)
            wd = vb.shape[1]
            dp_ref[:, s:s + wd] = vb
            dh = dh + _nt(vb, w_ref[:, s:s + wd])
        dx, dshift, dscale, dg = _norm_mod_bwd(dh, r, xhat, n, g, scale)
        dx_ref[...] = dxin_ref[...] + dx
        _acc_ctx_lat(dms_ref, i, jnp.concatenate([dshift, dscale], axis=1))
        _acc_all(dg_ref, i, dg)

    nt = T // tm
    return _host_call(
        body, ex, lambda: pl.program_id(0) == 0, lambda: pl.program_id(0) == nt - 1,
        name=name, grid=(nt,),
        in_specs=[_rows(tm, dm), _rows(tm, dm), _full((1, dm)), _ctx_lat(2 * dm), _full(w.shape)]
        + [_rows(tm, a.shape[1]) for a in flat],
        out_specs=[dx_spec, _cols(dm, tm), _rows(tm, n_out), _ctx_lat(2 * dm), _whole((1, dm))],
        out_shape=[jax.ShapeDtypeStruct((dx_rows, dm), F32), jax.ShapeDtypeStruct((dm, T), ACT),
                   jax.ShapeDtypeStruct((T, n_out), ACT), jax.ShapeDtypeStruct((2, 1, 2 * dm), F32),
                   jax.ShapeDtypeStruct((1, dm), F32)],
        scratch_shapes=[], sem=("arbitrary",), args=(x, dx_in, gain, ms, w, *flat))


def _ffn_fwd(x1, gain, ms, w_in, w_out, tm, name, target=None, ex=None):
    T, dm = x1.shape
    fh = w_out.shape[0]
    head = target is not None

    def body(*refs):
        if head:
            x_ref, g_ref, ms_ref, wi_ref, wo_ref, t_ref, x2_ref, u_ref, f_ref, loss_ref = refs
        else:
            x_ref, g_ref, ms_ref, wi_ref, wo_ref, x2_ref, u_ref, f_ref = refs
        ms_v = ms_ref[0]
        xv = x_ref[...]
        h = _norm_mod(xv, g_ref[...], ms_v[:, :dm], ms_v[:, dm:2 * dm])[3]
        u = _nn(_bf(h), wi_ref[...])
        u_ref[...] = _bf(u)
        gt = u[:, :fh]
        act = gt * _sig(gt) * u[:, fh:]
        f = _nn(_bf(act), wo_ref[...])
        f_ref[...] = _bf(f)
        x2 = xv + ms_v[:, 2 * dm:] * f
        if head:
            i = pl.program_id(0)
            e = x2 - t_ref[...]
            x2_ref[...] = jnp.where(i > 0, e * (1.0 / dm), 0.0)
            _acc_all(loss_ref, i, jnp.where(i > 0, jnp.sum(e * e) * (0.5 / dm), 0.0))
        else:
            x2_ref[...] = x2

    ins = [x1, gain, ms, w_in, w_out]
    in_specs = [_rows(tm, dm), _full((1, dm)), _ctx_lat(3 * dm), _full(w_in.shape), _full(w_out.shape)]
    out_specs = [_rows(tm, dm), _rows(tm, 2 * fh), _rows(tm, dm)]
    out_shape = [jax.ShapeDtypeStruct((T, dm), F32), jax.ShapeDtypeStruct((T, 2 * fh), ACT), jax.ShapeDtypeStruct((T, dm), ACT)]
    if head:
        ins.append(target)
        in_specs.append(pl.BlockSpec((tm, dm), lambda i: (jnp.maximum(i - 1, 0), 0)))
        out_specs.append(_whole((1, 1)))
        out_shape.append(jax.ShapeDtypeStruct((1, 1), F32))
    nt = T // tm
    return _host_call(
        body, ex, lambda: pl.program_id(0) == 0, lambda: pl.program_id(0) == nt - 1,
        name=name, grid=(nt,), in_specs=in_specs, out_specs=out_specs, out_shape=out_shape,
        scratch_shapes=[], sem=("arbitrary",), args=tuple(ins))


def _ffn_bwd(x1, dx2, u, f, gain, ms, w_in, w_out, tm, name, ex=None):
    T, dm = x1.shape
    fh = w_out.shape[0]

    def body(x_ref, dx2_ref, u_ref, f_ref, g_ref, ms_ref, wi_ref, wo_ref,
             dx1_ref, h_ref, du_ref, act_ref, df_ref, dms_ref, dg_ref):
        i = pl.program_id(0)
        ms_v = ms_ref[0]
        g = g_ref[...]
        scale = ms_v[:, dm:2 * dm]
        gate = ms_v[:, 2 * dm:]
        r, xhat, n, h = _norm_mod(x_ref[...], g, ms_v[:, :dm], scale)
        h_ref[...] = _bf(h).T
        dx2 = dx2_ref[...]
        dgate = jnp.sum(dx2 * f_ref[...].astype(F32), axis=0, keepdims=True)
        dfb = _bf(dx2 * gate)
        df_ref[...] = dfb
        da = _nt(dfb, wo_ref[...])
        uv = u_ref[...].astype(F32)
        gt = uv[:, :fh]
        up = uv[:, fh:]
        s = _sig(gt)
        sg = gt * s
        act_ref[...] = _bf(sg * up).T
        dgt = _bf(da * up * (s * (1.0 + gt * (1.0 - s))))
        dup = _bf(da * sg)
        du_ref[:, :fh] = dgt
        du_ref[:, fh:] = dup
        dh = _nt(dgt, wi_ref[:, :fh]) + _nt(dup, wi_ref[:, fh:])
        dx, dshift, dscale, dg = _norm_mod_bwd(dh, r, xhat, n, g, scale)
        dx1_ref[...] = dx2 + dx
        _acc_ctx_lat(dms_ref, i, jnp.concatenate([dshift, dscale, dgate], axis=1))
        _acc_all(dg_ref, i, dg)

    nt = T // tm
    return _host_call(
        body, ex, lambda: pl.program_id(0) == 0, lambda: pl.program_id(0) == nt - 1,
        name=name, grid=(nt,),
        in_specs=[_rows(tm, dm), _rows(tm, dm), _rows(tm, 2 * fh), _rows(tm, dm), _full((1, dm)), _ctx_lat(3 * dm),
                  _full(w_in.shape), _full(w_out.shape)],
        out_specs=[_rows(tm, dm), _cols(dm, tm), _rows(tm, 2 * fh), _cols(fh, tm), _rows(tm, dm),
                   _ctx_lat(3 * dm), _whole((1, dm))],
        out_shape=[jax.ShapeDtypeStruct((T, dm), F32), jax.ShapeDtypeStruct((dm, T), ACT),
                   jax.ShapeDtypeStruct((T, 2 * fh), ACT), jax.ShapeDtypeStruct((fh, T), ACT),
                   jax.ShapeDtypeStruct((T, dm), ACT), jax.ShapeDtypeStruct((2, 1, 3 * dm), F32),
                   jax.ShapeDtypeStruct((1, dm), F32)],
        scratch_shapes=[], sem=("arbitrary",), args=(x1, dx2, u, f, gain, ms, w_in, w_out))


def _wgrad(a_t, b, name):
    K, T = a_t.shape
    N = b.shape[1]
    tk, tn, tt = _tile(K, 1024), _tile(N, 1024), _tile(T, 2816)
    nt = T // tt

    def body(a_ref, b_ref, o_ref, acc_ref):
        t = pl.program_id(2)
        part = _nn(a_ref[...], b_ref[...])

        @pl.when(t == 0)
        def _():
            acc_ref[...] = part

        @pl.when(t > 0)
        def _():
            acc_ref[...] += part

        @pl.when(t == nt - 1)
        def _():
            o_ref[...] = acc_ref[...].astype(o_ref.dtype)

    return pl.pallas_call(
        body, name=name, grid=(K // tk, N // tn, nt),
        in_specs=[pl.BlockSpec((tk, tt), lambda i, j, t: (i, t)), pl.BlockSpec((tt, tn), lambda i, j, t: (t, j))],
        out_specs=pl.BlockSpec((tk, tn), lambda i, j, t: (i, j)),
        out_shape=jax.ShapeDtypeStruct((K, N), ACT),
        scratch_shapes=[pltpu.VMEM((tk, tn), F32)],
        compiler_params=_cp("parallel", "parallel", "arbitrary"),
    )(a_t, b)


def _post_fwd(x, o_fw, o_bw, g_src, g_blk, gain, a, w_out, ms, dvh, tm, name):
    T, dm = x.shape
    hv = o_fw.shape[1]
    aw = 0 if a is None else a.shape[1]
    has_gain = gain is not None

    def body(*refs):
        refs = list(refs)
        x_ref, of_ref, ob_ref, g_ref = refs[:4]
        k = 4
        gain_ref = a_ref = None
        if has_gain:
            gain_ref = refs[k]
            k += 1
        if aw:
            a_ref = refs[k]
            k += 1
        w_ref, ms_ref, x1_ref, z_ref = refs[k:k + 4]
        o = of_ref[...] + ob_ref[...]
        gr = g_ref[...]
        if aw:
            z_ref[:, :aw] = _bf(a_ref[...])
        for hd in range(hv // dvh):
            sl = slice(hd * dvh, (hd + 1) * dvh)
            oh = o[:, sl]
            gh = gr[:, sl]
            r = lax.rsqrt(jnp.mean(oh * oh, axis=-1, keepdims=True) + EPS)
            y = oh * r
            if has_gain:
                y = y * gain_ref[...]
            y = y * (gh * _sig(gh))
            z_ref[:, aw + hd * dvh:aw + (hd + 1) * dvh] = _bf(y)
        yp = _nn(z_ref[...], w_ref[...])
        x1_ref[...] = x_ref[...] + ms_ref[0] * yp

    ins = [x, o_fw, o_bw, g_src]
    specs = [_rows(tm, dm), _rows(tm, hv), _rows(tm, hv), pl.BlockSpec((tm, hv), lambda i: (i, g_blk))]
    if has_gain:
        ins.append(gain)
        specs.append(_full(gain.shape))
    if aw:
        ins.append(a)
        specs.append(_rows(tm, aw))
    ins += [w_out, ms]
    specs += [_full(w_out.shape), _ctx_lat(dm)]
    return pl.pallas_call(
        body, name=name, grid=(T // tm,), in_specs=specs,
        out_specs=[_rows(tm, dm), _rows(tm, aw + hv)],
        out_shape=[jax.ShapeDtypeStruct((T, dm), F32), jax.ShapeDtypeStruct((T, aw + hv), ACT)],
        compiler_params=_cp("arbitrary"),
    )(*ins)


def _post_bwd(dx1, z, o_fw, o_bw, g_src, g_blk, gain, w_out, ms, aw, dvh, tm, name):
    T, dm = dx1.shape
    hv = o_fw.shape[1]
    has_gain = gain is not None

    def body(*refs):
        refs = list(refs)
        dx1_ref, z_ref, of_ref, ob_ref, g_ref = refs[:5]
        k = 5
        gain_ref = None
        if has_gain:
            gain_ref = refs[k]
            k += 1
        w_ref, ms_ref = refs[k:k + 2]
        k += 2
        do_ref, dgr_ref = refs[k:k + 2]
        k += 2
        da_ref = None
        if aw:
            da_ref = refs[k]
            k += 1
        dy_ref, zt_ref, dgate_ref, dgain_ref = refs[k:k + 4]
        i = pl.program_id(0)
        dx1v = dx1_ref[...]
        zb = z_ref[...]
        zt_ref[...] = zb.T
        yp = _nn(zb, w_ref[...])
        _acc_ctx_lat(dgate_ref, i, jnp.sum(dx1v * yp, axis=0, keepdims=True))
        dyb = _bf(dx1v * ms_ref[0])
        dy_ref[...] = dyb
        dz = _nt(dyb, w_ref[...])
        if aw:
            da_ref[...] = dz[:, :aw]
        o = of_ref[...] + ob_ref[...]
        gr = g_ref[...]
        dgain = jnp.zeros((1, dvh), F32)
        for hd in range(hv // dvh):
            sl = slice(hd * dvh, (hd + 1) * dvh)
            oh = o[:, sl]
            gh = gr[:, sl]
            dyh = dz[:, aw + hd * dvh:aw + (hd + 1) * dvh]
            r = lax.rsqrt(jnp.mean(oh * oh, axis=-1, keepdims=True) + EPS)
            n = oh * r
            s = _sig(gh)
            sl_g = gh * s
            gn = gain_ref[...] if has_gain else 1.0
            dgr_ref[:, sl] = _bf(dyh * n * gn * (s * (1.0 + gh * (1.0 - s))))
            dn = dyh * gn * sl_g
            dgain = dgain + jnp.sum(dyh * n * sl_g, axis=0, keepdims=True)
            do_ref[:, sl] = _bf(r * (dn - n * jnp.mean(dn * n, axis=-1, keepdims=True)))
        _acc_all(dgain_ref, i, dgain)

    ins = [dx1, z, o_fw, o_bw, g_src]
    specs = [_rows(tm, dm), _rows(tm, aw + hv), _rows(tm, hv), _rows(tm, hv),
             pl.BlockSpec((tm, hv), lambda i: (i, g_blk))]
    if has_gain:
        ins.append(gain)
        specs.append(_full(gain.shape))
    ins += [w_out, ms]
    specs += [_full(w_out.shape), _ctx_lat(dm)]
    out_specs = [_rows(tm, hv), _rows(tm, hv)]
    out_shape = [jax.ShapeDtypeStruct((T, hv), ACT), jax.ShapeDtypeStruct((T, hv), ACT)]
    if aw:
        out_specs.append(_rows(tm, aw))
        out_shape.append(jax.ShapeDtypeStruct((T, aw), F32))
    out_specs += [_rows(tm, dm), _cols(aw + hv, tm), _ctx_lat(dm), _whole((1, dvh))]
    out_shape += [jax.ShapeDtypeStruct((T, dm), ACT), jax.ShapeDtypeStruct((aw + hv, T), ACT),
                  jax.ShapeDtypeStruct((2, 1, dm), F32), jax.ShapeDtypeStruct((1, dvh), F32)]
    return pl.pallas_call(
        body, name=name, grid=(T // tm,), in_specs=specs, out_specs=out_specs, out_shape=out_shape,
        compiler_params=_cp("arbitrary"),
    )(*ins)


def _loss_bwd(x, target, tm, name):
    T, dm = x.shape

    def body(x_ref, t_ref, dx_ref, loss_ref):
        i = pl.program_id(0)

        @pl.when(i == 0)
        def _():
            dx_ref[...] = jnp.zeros_like(dx_ref)
            loss_ref[...] = jnp.zeros_like(loss_ref)

        @pl.when(i > 0)
        def _():
            e = x_ref[...] - t_ref[...]
            dx_ref[...] = e * (1.0 / dm)
            loss_ref[...] += jnp.sum(e * e) * (0.5 / dm)

    return pl.pallas_call(
        body, name=name, grid=(T // tm,),
        in_specs=[_rows(tm, dm), pl.BlockSpec((tm, dm), lambda i: (jnp.maximum(i - 1, 0), 0))],
        out_specs=[_rows(tm, dm), _whole((1, 1))],
        out_shape=[jax.ShapeDtypeStruct((T, dm), F32), jax.ShapeDtypeStruct((1, 1), F32)],
        compiler_params=_cp("arbitrary"),
    )(x, target)


def _swap_matrix():
    r = lax.broadcasted_iota(jnp.int32, (HEAD_DIM, HEAD_DIM), 0)
    c = lax.broadcasted_iota(jnp.int32, (HEAD_DIM, HEAD_DIM), 1)
    return jnp.where((r + HEAD_DIM // 2) % HEAD_DIM == c, 1.0, 0.0).astype(BF16)


def _qk_prep_fwd(raw, gains, cos2, sin2, tq, name):
    nh, T, hd = raw.shape

    def body(x_ref, g_ref, c_ref, s_ref, o_ref):
        hidx = pl.program_id(0)
        xv = x_ref[0]
        r = lax.rsqrt(jnp.mean(xv * xv, axis=-1, keepdims=True) + EPS)
        n = xv * r * g_ref[0]
        y = n * c_ref[...] + _nn3r(n, _swap_matrix()) * s_ref[...]
        sc = jnp.where(hidx < ATTN_HEADS, HEAD_DIM ** -0.5, 1.0)
        o_ref[0] = _bf(y * sc)

    return pl.pallas_call(
        body, name=name, grid=(nh, T // tq),
        in_specs=[pl.BlockSpec((1, tq, hd), lambda h, i: (h, i, 0)), pl.BlockSpec((1, 1, hd), lambda h, i: (h, 0, 0)),
                  pl.BlockSpec((tq, hd), lambda h, i: (i, 0)), pl.BlockSpec((tq, hd), lambda h, i: (i, 0))],
        out_specs=pl.BlockSpec((1, tq, hd), lambda h, i: (h, i, 0)),
        out_shape=jax.ShapeDtypeStruct((nh, T, hd), ACT),
        compiler_params=_cp("arbitrary", "arbitrary"),
    )(raw, gains, cos2, sin2)


def _qk_prep_bwd(dy, raw, gains, cos2, sin2, tq, name):
    nh, T, hd = raw.shape

    def body(dy_ref, x_ref, g_ref, c_ref, s_ref, dx_ref, dg_ref):
        hidx = pl.program_id(0)
        i = pl.program_id(1)
        xv = x_ref[0]
        g = g_ref[0]
        r = lax.rsqrt(jnp.mean(xv * xv, axis=-1, keepdims=True) + EPS)
        xhat = xv * r
        sc = jnp.where(hidx < ATTN_HEADS, HEAD_DIM ** -0.5, 1.0)
        dyv = dy_ref[0] * sc
        dn = dyv * c_ref[...] + _nn3r(dyv * s_ref[...], _swap_matrix())
        _acc_all(dg_ref, i, jnp.sum(dn * xhat, axis=0, keepdims=True))
        dxh = dn * g
        dx_ref[0] = r * (dxh - xhat * jnp.mean(dxh * xhat, axis=-1, keepdims=True))

    return pl.pallas_call(
        body, name=name, grid=(nh, T // tq),
        in_specs=[pl.BlockSpec((1, tq, hd), lambda h, i: (h, i, 0)), pl.BlockSpec((1, tq, hd), lambda h, i: (h, i, 0)),
                  pl.BlockSpec((1, 1, hd), lambda h, i: (h, 0, 0)),
                  pl.BlockSpec((tq, hd), lambda h, i: (i, 0)), pl.BlockSpec((tq, hd), lambda h, i: (i, 0))],
        out_specs=[pl.BlockSpec((1, tq, hd), lambda h, i: (h, i, 0)), pl.BlockSpec((1, 1, hd), lambda h, i: (h, 0, 0))],
        out_shape=[jax.ShapeDtypeStruct((nh, T, hd), F32), jax.ShapeDtypeStruct((nh, 1, hd), F32)],
        compiler_params=_cp("arbitrary", "arbitrary"),
    )(dy, raw, gains, cos2, sin2)


def _attn_scores(q, k_ref, i, lc, T, sink):
    blk = ATTN_BLOCK
    kc = k_ref[0, pl.ds(blk, lc), :]
    kw = k_ref[0, pl.ds(pl.multiple_of(i * blk, blk), 3 * blk), :]
    s_c = _nt(q, kc)
    s_w = _nt(q, kw)
    row = lax.broadcasted_iota(jnp.int32, (4 * blk, 1), 0)
    qpos = i * blk + (row & (blk - 1))
    kpos = (i - 1) * blk + lax.broadcasted_iota(jnp.int32, (1, 3 * blk), 1)
    valid = (qpos >= lc) & (kpos >= lc) & (kpos < T) & (jnp.abs(kpos - qpos) <= WINDOW)
    s_w = jnp.where(valid, s_w, NEG)
    return kc, kw, s_c, s_w


def _attn_fwd(qt, kp, vp, sinkb, lc, name, ex=None):
    nh, T, hd = qt.shape
    blk = ATTN_BLOCK
    g = nh // ATTN_KV

    def body(q_ref, k_ref, v_ref, sink_ref, o_ref, lse_ref):
        i = pl.program_id(1)
        q = q_ref[...].reshape(g * blk, hd)
        sink = sink_ref[0]
        kc, kw, s_c, s_w = _attn_scores(q, k_ref, i, lc, T, sink)
        m = jnp.maximum(jnp.maximum(jnp.max(s_c, axis=-1, keepdims=True), jnp.max(s_w, axis=-1, keepdims=True)), sink)
        e_c = jnp.exp(s_c - m)
        e_w = jnp.exp(s_w - m)
        den = jnp.exp(sink - m) + jnp.sum(e_c, axis=-1, keepdims=True) + jnp.sum(e_w, axis=-1, keepdims=True)
        inv = 1.0 / den
        vc = v_ref[0, pl.ds(blk, lc), :]
        vw = v_ref[0, pl.ds(pl.multiple_of(i * blk, blk), 3 * blk), :]
        o = _nn(_bf(e_c * inv), vc) + _nn(_bf(e_w * inv), vw)
        o_ref[...] = o.reshape(g, blk, hd)
        lse_ref[...] = (m + jnp.log(den)).reshape(g, blk, 1)

    nb = T // blk
    return _host_call(
        body, ex, lambda: (pl.program_id(0) == 0) & (pl.program_id(1) == 0),
        lambda: (pl.program_id(0) == ATTN_KV - 1) & (pl.program_id(1) == nb - 1),
        name=name, grid=(ATTN_KV, nb),
        in_specs=[pl.BlockSpec((g, blk, hd), lambda kv, i: (kv, i, 0)),
                  pl.BlockSpec((1, T + 2 * blk, hd), lambda kv, i: (kv, 0, 0)),
                  pl.BlockSpec((1, T + 2 * blk, hd), lambda kv, i: (kv, 0, 0)),
                  pl.BlockSpec((1, g * blk, 1), lambda kv, i: (kv, 0, 0))],
        out_specs=[pl.BlockSpec((g, blk, hd), lambda kv, i: (kv, i, 0)),
                   pl.BlockSpec((g, blk, 1), lambda kv, i: (kv, i, 0))],
        out_shape=[jax.ShapeDtypeStruct((nh, T, hd), F32), jax.ShapeDtypeStruct((nh, T, 1), F32)],
        scratch_shapes=[], sem=("arbitrary", "arbitrary"), args=(qt, kp, vp, sinkb))


def _attn_bwd(qt, kp, vp, sinkb, o, lse, do, lc, name):
    nh, T, hd = qt.shape
    blk = ATTN_BLOCK
    g = nh // ATTN_KV

    def body(q_ref, k_ref, v_ref, sink_ref, o_ref, lse_ref, do_ref, dq_ref, dk_ref, dv_ref, ds_ref):
        i = pl.program_id(1)

        @pl.when(i == 0)
        def _():
            dk_ref[...] = jnp.zeros_like(dk_ref)
            dv_ref[...] = jnp.zeros_like(dv_ref)
            ds_ref[...] = jnp.zeros_like(ds_ref)

        q = q_ref[...].reshape(g * blk, hd)
        sink = sink_ref[0]
        lse = lse_ref[...].reshape(g * blk, 1)
        dov = do_ref[...].reshape(g * blk, hd)
        delta = jnp.sum(dov * o_ref[...].reshape(g * blk, hd), axis=-1, keepdims=True)
        kc, kw, s_c, s_w = _attn_scores(q, k_ref, i, lc, T, sink)
        p_c = jnp.exp(s_c - lse)
        p_w = jnp.exp(s_w - lse)
        win = pl.ds(pl.multiple_of(i * blk, blk), 3 * blk)
        vc = v_ref[0, pl.ds(blk, lc), :]
        vw = v_ref[0, win, :]
        dob = _bf(dov)
        ds_c = _bf(p_c * (_nt(dob, vc) - delta))
        ds_w = _bf(p_w * (_nt(dob, vw) - delta))
        dsr = -jnp.exp(sink - lse) * delta
        for hh in range(g):
            ds_ref[0, hh:hh + 1, :] += jnp.sum(dsr[hh * blk:(hh + 1) * blk, :], axis=0, keepdims=True)
        dq_ref[...] = (_nn(ds_c, kc) + _nn(ds_w, kw)).reshape(g, blk, hd)
        dk_ref[0, pl.ds(blk, lc), :] += _tn(ds_c, q)
        dk_ref[0, win, :] += _tn(ds_w, q)
        dv_ref[0, pl.ds(blk, lc), :] += _tn(_bf(p_c), dob)
        dv_ref[0, win, :] += _tn(_bf(p_w), dob)

    qspec = pl.BlockSpec((g, blk, hd), lambda kv, i: (kv, i, 0))
    kspec = pl.BlockSpec((1, T + 2 * blk, hd), lambda kv, i: (kv, 0, 0))
    lspec = pl.BlockSpec((g, blk, 1), lambda kv, i: (kv, i, 0))
    return pl.pallas_call(
        body, name=name, grid=(ATTN_KV, T // blk),
        in_specs=[qspec, kspec, kspec, pl.BlockSpec((1, g * blk, 1), lambda kv, i: (kv, 0, 0)), qspec, lspec, qspec],
        out_specs=[qspec, kspec, kspec, pl.BlockSpec((1, g, 1), lambda kv, i: (kv, 0, 0))],
        out_shape=[jax.ShapeDtypeStruct((nh, T, hd), F32), jax.ShapeDtypeStruct((ATTN_KV, T + 2 * blk, hd), F32),
                   jax.ShapeDtypeStruct((ATTN_KV, T + 2 * blk, hd), F32), jax.ShapeDtypeStruct((ATTN_KV, g, 1), F32)],
        compiler_params=_cp("arbitrary", "arbitrary"),
    )(qt, kp, vp, sinkb, o, lse, do)


PAIR = 2 * HEAD_DIM
N_PAIRS = (ATTN_HEADS + ATTN_KV) // 2


def _lanes():
    return lax.broadcasted_iota(jnp.int32, (1, PAIR), 1)


def _swap32(v):
    first_half = (_lanes() & (HEAD_DIM // 2)) == 0
    return jnp.where(first_half, pltpu.roll(v, PAIR - HEAD_DIM // 2, 1), pltpu.roll(v, HEAD_DIM // 2, 1))


def _head_mean(v):
    r = lax.broadcasted_iota(jnp.int32, (PAIR, PAIR), 0)
    c = lax.broadcasted_iota(jnp.int32, (PAIR, PAIR), 1)
    same = jnp.where((r >= HEAD_DIM) == (c >= HEAD_DIM), 1.0, 0.0).astype(BF16)
    return _nn3r(v, same) * (1.0 / HEAD_DIM)


def _qk_slab_fwd(pa, gains, cosp, sinp, tm, name):
    T = pa.shape[0]
    qw = ATTN_HEADS * HEAD_DIM

    def body(pa_ref, g_ref, c_ref, s_ref, q_ref, k_ref, v_ref):
        cosv, sinv = c_ref[...], s_ref[...]
        for p in range(N_PAIRS):
            xv = pa_ref[:, p * PAIR:(p + 1) * PAIR]
            n = xv * lax.rsqrt(_head_mean(xv * xv) + EPS) * g_ref[p]
            y = n * cosv + _swap32(n) * sinv
            if p < N_PAIRS - 1:
                q_ref[:, p * PAIR:(p + 1) * PAIR] = _bf(y * HEAD_DIM ** -0.5)
            else:
                k_ref[...] = _bf(y)
        v_ref[...] = _bf(pa_ref[:, qw + PAIR:])

    return pl.pallas_call(
        body, name=name, grid=(T // tm,),
        in_specs=[_rows(tm, pa.shape[1]), _full(gains.shape), _rows(tm, PAIR), _rows(tm, PAIR)],
        out_specs=[_rows(tm, qw), _rows(tm, PAIR), _rows(tm, PAIR)],
        out_shape=[jax.ShapeDtypeStruct((T, qw), ACT), jax.ShapeDtypeStruct((T, PAIR), ACT),
                   jax.ShapeDtypeStruct((T, PAIR), ACT)],
        compiler_params=_cp("arbitrary"),
    )(pa, gains, cosp, sinp)


def _qk_slab_bwd(dq, dk, pa, gains, cosp, sinp, tm, name):
    T = pa.shape[0]
    qw = ATTN_HEADS * HEAD_DIM

    def body(dq_ref, dk_ref, pa_ref, g_ref, c_ref, s_ref, dx_ref, dg_ref):
        i = pl.program_id(0)
        cosv, sinv = c_ref[...], s_ref[...]
        for p in range(N_PAIRS):
            sl = slice(p * PAIR, (p + 1) * PAIR)
            xv = pa_ref[:, sl]
            r = lax.rsqrt(_head_mean(xv * xv) + EPS)
            xhat = xv * r
            dy = dq_ref[:, sl] * HEAD_DIM ** -0.5 if p < N_PAIRS - 1 else dk_ref[...]
            dn = dy * cosv + _swap32(dy * sinv)
            _acc_all(dg_ref.at[p], i, jnp.sum(dn * xhat, axis=0, keepdims=True))
            dxh = dn * g_ref[p]
            dx_ref[:, sl] = r * (dxh - xhat * _head_mean(dxh * xhat))

    return pl.pallas_call(
        body, name=name, grid=(T // tm,),
        in_specs=[_rows(tm, qw), _rows(tm, PAIR), _rows(tm, qw + PAIR), _full(gains.shape), _rows(tm, PAIR), _rows(tm, PAIR)],
        out_specs=[_rows(tm, qw + PAIR), _whole(gains.shape)],
        out_shape=[jax.ShapeDtypeStruct((T, qw + PAIR), F32), jax.ShapeDtypeStruct(gains.shape, F32)],
        compiler_params=_cp("arbitrary"),
    )(dq, dk, pa, gains, cosp, sinp)


def _attn_window(ref, i, nb):
    blk = ATTN_BLOCK
    starts = [pl.multiple_of(jnp.clip(i + d, 0, nb - 1) * blk, blk) for d in (-1, 0, 1)]
    return starts, jnp.concatenate([ref[pl.ds(s, blk), :] for s in starts], axis=0)


def _attn_mask(i, lc, T):
    blk = ATTN_BLOCK
    row = lax.broadcasted_iota(jnp.int32, (4 * blk, 1), 0)
    qpos = i * blk + (row & (blk - 1))
    kpos = (i - 1) * blk + lax.broadcasted_iota(jnp.int32, (1, 3 * blk), 1)
    return (qpos >= lc) & (kpos >= lc) & (kpos < T) & (jnp.abs(kpos - qpos) <= WINDOW)


def _to_kv_half(v, head, kv):
    return v if head % 2 == kv else pltpu.roll(v, HEAD_DIM, 1)


def _attn_slab_fwd(qt, ks, vs, sinkb, lc, name, ex=None):
    T = qt.shape[0]
    blk = ATTN_BLOCK
    nb = T // blk
    g = ATTN_HEADS // ATTN_KV

    def body(q_ref, k_ref, v_ref, sink_ref, o_ref, lse_ref):
        i = pl.program_id(0)
        lane = _lanes()
        valid = _attn_mask(i, lc, T)
        kc_all, vc = k_ref[0:lc, :], v_ref[0:lc, :]
        _, kw_all = _attn_window(k_ref, i, nb)
        _, vw = _attn_window(v_ref, i, nb)
        placed = [None] * ATTN_HEADS
        for kv in range(ATTN_KV):
            mine = (lane >= kv * HEAD_DIM) & (lane < (kv + 1) * HEAD_DIM)
            kc = jnp.where(mine, kc_all, jnp.zeros_like(kc_all))
            kw = jnp.where(mine, kw_all, jnp.zeros_like(kw_all))
            heads = [kv * g + j for j in range(g)]
            q4 = jnp.concatenate([_to_kv_half(q_ref[:, (h // 2) * PAIR:(h // 2 + 1) * PAIR], h, kv) for h in heads], axis=0)
            sink = sink_ref[kv]
            s_c = _nt(q4, kc)
            s_w = jnp.where(valid, _nt(q4, kw), NEG)
            m = jnp.maximum(jnp.maximum(jnp.max(s_c, axis=-1, keepdims=True), jnp.max(s_w, axis=-1, keepdims=True)), sink)
            e_c = jnp.exp(s_c - m)
            e_w = jnp.exp(s_w - m)
            den = jnp.exp(sink - m) + jnp.sum(e_c, axis=-1, keepdims=True) + jnp.sum(e_w, axis=-1, keepdims=True)
            inv = 1.0 / den
            o4 = _nn(_bf(e_c * inv), vc) + _nn(_bf(e_w * inv), vw)
            lse_ref[kv * g:(kv + 1) * g] = (m + jnp.log(den)).reshape(g, blk, 1)
            for j, h in enumerate(heads):
                placed[h] = _to_kv_half(o4[j * blk:(j + 1) * blk], h, kv)
        for p in range(ATTN_HEADS // 2):
            o_ref[:, p * PAIR:(p + 1) * PAIR] = jnp.where(lane < HEAD_DIM, placed[2 * p], placed[2 * p + 1])

    qw = ATTN_HEADS * HEAD_DIM
    return _host_call(
        body, ex, lambda: pl.program_id(0) == 0, lambda: pl.program_id(0) == nb - 1,
        name=name, grid=(nb,),
        in_specs=[_rows(blk, qw), _full((T, PAIR)), _full((T, PAIR)), _full(sinkb.shape)],
        out_specs=[_rows(blk, qw), pl.BlockSpec((ATTN_HEADS, blk, 1), lambda i: (0, i, 0))],
        out_shape=[jax.ShapeDtypeStruct((T, qw), F32), jax.ShapeDtypeStruct((ATTN_HEADS, T, 1), F32)],
        scratch_shapes=[], sem=("arbitrary",), args=(qt, ks, vs, sinkb))


def _attn_slab_bwd(qt, ks, vs, sinkb, o, lse, do, lc, name, ex=None):
    T = qt.shape[0]
    blk = ATTN_BLOCK
    nb = T // blk
    g = ATTN_HEADS // ATTN_KV

    def body(q_ref, k_ref, v_ref, sink_ref, o_ref, lse_ref, do_ref, dq_ref, dk_ref, dv_ref, ds_ref):
        i = pl.program_id(0)

        @pl.when(i == 0)
        def _():
            dk_ref[...] = jnp.zeros_like(dk_ref)
            dv_ref[...] = jnp.zeros_like(dv_ref)
            ds_ref[...] = jnp.zeros_like(ds_ref)

        lane = _lanes()
        valid = _attn_mask(i, lc, T)
        kc_all, vc_all = k_ref[0:lc, :], v_ref[0:lc, :]
        starts, kw_all = _attn_window(k_ref, i, nb)
        _, vw_all = _attn_window(v_ref, i, nb)
        dq_pairs = [jnp.zeros((blk, PAIR), F32) for _ in range(ATTN_HEADS // 2)]
        for kv in range(ATTN_KV):
            mine = (lane >= kv * HEAD_DIM) & (lane < (kv + 1) * HEAD_DIM)

            def only(v):
                return jnp.where(mine, v, jnp.zeros_like(v))

            kc, kw, vc, vw = only(kc_all), only(kw_all), only(vc_all), only(vw_all)
            heads = [kv * g + j for j in range(g)]
            qs, dos, deltas = [], [], []
            for h in heads:
                sl = slice((h // 2) * PAIR, (h // 2 + 1) * PAIR)
                dov = do_ref[:, sl]
                qs.append(_to_kv_half(q_ref[:, sl], h, kv))
                dos.append(_bf(_to_kv_half(dov, h, kv)))
                own = (lane < HEAD_DIM) if h % 2 == 0 else (lane >= HEAD_DIM)
                deltas.append(jnp.sum(jnp.where(own, dov * o_ref[:, sl], 0.0), axis=-1, keepdims=True))
            q4, do4, delta = jnp.concatenate(qs, axis=0), jnp.concatenate(dos, axis=0), jnp.concatenate(deltas, axis=0)
            sink = sink_ref[kv]
            lse = lse_ref[kv * g:(kv + 1) * g].reshape(g * blk, 1)
            p_c = jnp.exp(_nt(q4, kc) - lse)
            p_w = jnp.exp(jnp.where(valid, _nt(q4, kw), NEG) - lse)
            ds_c = _bf(p_c * (_nt(do4, vc) - delta))
            ds_w = _bf(p_w * (_nt(do4, vw) - delta))
            dsr = -jnp.exp(sink - lse) * delta
            dq4 = _nn(ds_c, kc) + _nn(ds_w, kw)
            for j, h in enumerate(heads):
                ds_ref[h:h + 1, :] += jnp.sum(dsr[j * blk:(j + 1) * blk, :], axis=0, keepdims=True)
                dq_pairs[h // 2] = dq_pairs[h // 2] + _to_kv_half(dq4[j * blk:(j + 1) * blk], h, kv)
            dk_ref[0:lc, :] += only(_tn(ds_c, q4))
            dv_ref[0:lc, :] += only(_tn(_bf(p_c), do4))
            dkw = only(_tn(ds_w, q4))
            dvw = only(_tn(_bf(p_w), do4))
            for b, s in enumerate(starts):
                dk_ref[pl.ds(s, blk), :] += dkw[b * blk:(b + 1) * blk]
                dv_ref[pl.ds(s, blk), :] += dvw[b * blk:(b + 1) * blk]
        for p in range(ATTN_HEADS // 2):
            dq_ref[:, p * PAIR:(p + 1) * PAIR] = dq_pairs[p]

    qw = ATTN_HEADS * HEAD_DIM
    lspec = pl.BlockSpec((ATTN_HEADS, blk, 1), lambda i: (0, i, 0))
    return _host_call(
        body, ex, lambda: pl.program_id(0) == 0, lambda: pl.program_id(0) == nb - 1,
        name=name, grid=(nb,),
        in_specs=[_rows(blk, qw), _full((T, PAIR)), _full((T, PAIR)), _full(sinkb.shape), _rows(blk, qw), lspec,
                  _rows(blk, qw)],
        out_specs=[_rows(blk, qw), _whole((T, PAIR)), _whole((T, PAIR)), _whole((ATTN_HEADS, 1))],
        out_shape=[jax.ShapeDtypeStruct((T, qw), F32), jax.ShapeDtypeStruct((T, PAIR), F32),
                   jax.ShapeDtypeStruct((T, PAIR), F32), jax.ShapeDtypeStruct((ATTN_HEADS, 1), F32)],
        scratch_shapes=[], sem=("arbitrary",), args=(qt, ks, vs, sinkb, o, lse, do))


def _fw_chunk(s, nc, nt):
    return s


def _bw_chunk(s, nc, nt):
    return jnp.where(s < nc, nc - 1 - s, nt - 1 - (s - nc))


def _tri(c, rev):
    r = lax.broadcasted_iota(jnp.int32, (c, c), 0)
    k = lax.broadcasted_iota(jnp.int32, (c, c), 1)
    return (k >= r) if rev else (k <= r)


def _gla_gates(z, lb, rev):
    c = HG_CHUNK
    sg = _sig(z)
    f = lb + (1.0 - lb) * sg
    cum = _nn3(jnp.where(_tri(c, rev), 1.0, 0.0).astype(BF16), jnp.log(f))
    mid = c - 1 - c // 2 if rev else c // 2
    last = 0 if rev else c - 1
    return sg, f, cum, cum[mid:mid + 1], cum[last:last + 1], last


def _lower_bound(lbraw_ref):
    lr = lbraw_ref[...]
    return _sig(lr[0:1] - lr[1:2])


def _gla_fwd(pb, lbraw, lc, name, ex=None):
    T = pb.shape[0]
    c, hw, d = HG_CHUNK, HG_HEADS * HG_D, HG_D
    nt, nc = T // c, lc // c
    orders = (_fw_chunk, _bw_chunk)

    def body(qf, zf, vf, qb, zb, vb, lb_ref, of_ref, ob_ref, sf_ref, sb_ref, st_ref):
        @pl.when(pl.program_id(0) == 0)
        def _():
            st_ref[...] = jnp.zeros_like(st_ref)

        lb = _lower_bound(lb_ref)
        dirs = ((qf, zf, vf, of_ref, sf_ref), (qb, zb, vb, ob_ref, sb_ref))
        combos = [(dr, h, slice(h * d, (h + 1) * d)) for dr in range(2) for h in range(HG_HEADS)]
        prep = []
        for dr, (q_ref, z_ref, v_ref, _, _) in enumerate(dirs):
            rev = dr == 1
            qr = q_ref[...]
            q = qr * _sig(qr)
            _, f, cum, ref, last, _ = _gla_gates(z_ref[...], lb, rev)
            k = 1.0 - f
            prep.append(dict(q1=_bf(q * jnp.exp(cum - ref)), k1=_bf(k * jnp.exp(ref - cum)), q2=_bf(q * jnp.exp(cum)),
                             k2=_bf(k * jnp.exp(last - cum)), el=jnp.exp(last), v=_bf(v_ref[...]), mask=_tri(c, rev)))
        a = [_bf(jnp.where(prep[dr]["mask"], _nt(prep[dr]["q1"][:, sl], prep[dr]["k1"][:, sl]), 0.0)) for dr, _, sl in combos]
        for (dr, h, sl), a_h in zip(combos, a):
            p = prep[dr]
            o_ref, s_ref = dirs[dr][3], dirs[dr][4]
            st = st_ref[dr, h]
            stb = _bf(st)
            s_ref[0, h] = stb
            o_ref[:, sl] = _nn(a_h, p["v"][:, sl]) + _nt(p["q2"][:, sl], stb)
            st_ref[dr, h] = st * p["el"][:, sl] + _tn(p["v"][:, sl], p["k2"][:, sl])

    def col(order, blkcol):
        return pl.BlockSpec((c, hw), lambda s: (order(s, nc, nt), blkcol))

    def st_spec(order):
        return pl.BlockSpec((1, HG_HEADS, d, d), lambda s: (order(s, nc, nt), 0, 0, 0))

    in_specs = []
    for dr, order in enumerate(orders):
        in_specs += [col(order, 0), col(order, 1 + dr), col(order, 3)]
    in_specs.append(_full(lbraw.shape))
    return _host_call(
        body, ex, lambda: pl.program_id(0) == 0, lambda: pl.program_id(0) == nt - 1,
        name=name, grid=(nt,), in_specs=in_specs,
        out_specs=[col(_fw_chunk, 0), col(_bw_chunk, 0), st_spec(_fw_chunk), st_spec(_bw_chunk)],
        out_shape=[jax.ShapeDtypeStruct((T, hw), F32), jax.ShapeDtypeStruct((T, hw), F32),
                   jax.ShapeDtypeStruct((nt, HG_HEADS, d, d), ACT), jax.ShapeDtypeStruct((nt, HG_HEADS, d, d), ACT)],
        scratch_shapes=[pltpu.VMEM((2, HG_HEADS, d, d), F32)], sem=("arbitrary",),
        args=(pb, pb, pb, pb, pb, pb, lbraw))


def _gla_bwd(pb, lbraw, s_fw, s_bw, do, lc, name, ex=None):
    T = pb.shape[0]
    c, hw, d = HG_CHUNK, HG_HEADS * HG_D, HG_D
    nt, nc = T // c, lc // c

    def rfw(s, nc_, nt_):
        return _fw_chunk(nt_ - 1 - s, nc_, nt_)

    def rbw(s, nc_, nt_):
        return _bw_chunk(nt_ - 1 - s, nc_, nt_)

    def body(qf, zf, vf, sf, dof, qb, zb, vb, sb, dob_, lb_ref,
             dqf, dzf, dvf, dqb, dzb, dvb, dlb_ref, dst_ref):
        step = pl.program_id(0)

        @pl.when(step == 0)
        def _():
            dst_ref[...] = jnp.zeros_like(dst_ref)

        lb = _lower_bound(lb_ref)
        sets = ((qf, zf, vf, sf, dof, dqf, dzf, dvf), (qb, zb, vb, sb, dob_, dqb, dzb, dvb))
        combos = [(dr, h, slice(h * d, (h + 1) * d)) for dr in range(2) for h in range(HG_HEADS)]
        prep = []
        for dr, (q_ref, z_ref, v_ref, _, do_ref, _, _, _) in enumerate(sets):
            rev = dr == 1
            qr = q_ref[...]
            sq = _sig(qr)
            q = qr * sq
            sg, f, cum, ref, last, last_row = _gla_gates(z_ref[...], lb, rev)
            k = 1.0 - f
            e_qr, e_kr, e_q, e_kl = jnp.exp(cum - ref), jnp.exp(ref - cum), jnp.exp(cum), jnp.exp(last - cum)
            q1, k1, q2, k2 = q * e_qr, k * e_kr, q * e_q, k * e_kl
            prep.append(dict(qr=qr, sq=sq, sg=sg, f=f, e_qr=e_qr, e_kr=e_kr, e_q=e_q, e_kl=e_kl, el=jnp.exp(last),
                             q1=q1, k1=k1, q2=q2, k2=k2, q1b=_bf(q1), k1b=_bf(k1), q2b=_bf(q2), k2b=_bf(k2),
                             vb=_bf(v_ref[...]), dob=_bf(do_ref[...]), mask=_tri(c, rev), last_row=last_row,
                             acc_t=jnp.where(_tri(c, not rev), 1.0, 0.0).astype(BF16)))
        a = [_bf(jnp.where(prep[dr]["mask"], _nt(prep[dr]["q1b"][:, sl], prep[dr]["k1b"][:, sl]), 0.0)) for dr, _, sl in combos]
        da = [_bf(jnp.where(prep[dr]["mask"], _nt(prep[dr]["dob"][:, sl], prep[dr]["vb"][:, sl]), 0.0)) for dr, _, sl in combos]
        parts = [dict(dq1=[], dk1=[], dq2=[], dk2=[], dls=[]) for _ in range(2)]
        for (dr, h, sl), a_h, da_h in zip(combos, a, da):
            p = prep[dr]
            s_ref, dv_ref = sets[dr][3], sets[dr][7]
            stb = s_ref[0, h]
            dst = dst_ref[dr, h]
            dstb = _bf(dst)
            dob_h, vb_h = p["dob"][:, sl], p["vb"][:, sl]
            dv_ref[:, sl] = _bf(_tn(a_h, dob_h) + _nt(p["k2b"][:, sl], dstb))
            parts[dr]["dq1"].append(_nn(da_h, p["k1b"][:, sl]))
            parts[dr]["dk1"].append(_tn(da_h, p["q1b"][:, sl]))
            parts[dr]["dq2"].append(_nn(dob_h, stb))
            parts[dr]["dk2"].append(_nn(vb_h, dstb))
            el_h = p["el"][:, sl]
            dst_ref[dr, h] = _tn(dob_h, p["q2b"][:, sl]) + dst * el_h
            parts[dr]["dls"].append(jnp.sum(dst * stb.astype(F32), axis=0, keepdims=True) * el_h)
        dlb_tot = jnp.zeros((1, hw), F32)
        for dr in range(2):
            p = prep[dr]
            dq_ref, dz_ref = sets[dr][5], sets[dr][6]
            dq1, dk1, dq2, dk2, dls = (jnp.concatenate(parts[dr][n], axis=1) for n in ("dq1", "dk1", "dq2", "dk2", "dls"))
            dq = dq1 * p["e_qr"] + dq2 * p["e_q"]
            dk = dk1 * p["e_kr"] + dk2 * p["e_kl"]
            dcum = dq1 * p["q1"] - dk1 * p["k1"] + dq2 * p["q2"] - dk2 * p["k2"]
            dlast = jnp.sum(dk2 * p["k2"], axis=0, keepdims=True) + dls
            rowid = lax.broadcasted_iota(jnp.int32, (c, 1), 0)
            dcum = dcum + jnp.where(rowid == p["last_row"], dlast, 0.0)
            df = _nn3(p["acc_t"], dcum) / p["f"] - dk
            sg = p["sg"]
            dz_ref[...] = _bf(df * (1.0 - lb) * sg * (1.0 - sg))
            dlb_tot = dlb_tot + jnp.sum(df * (1.0 - sg), axis=0, keepdims=True)
            dq_ref[...] = _bf(dq * (p["sq"] * (1.0 + p["qr"] * (1.0 - p["sq"]))))
        _acc_all(dlb_ref, step, dlb_tot)

    def col(order, blkcol):
        return pl.BlockSpec((c, hw), lambda s: (order(s, nc, nt), blkcol))

    def st_spec(order):
        return pl.BlockSpec((1, HG_HEADS, d, d), lambda s: (order(s, nc, nt), 0, 0, 0))

    in_specs = []
    for dr, order in enumerate((rfw, rbw)):
        in_specs += [col(order, 0), col(order, 1 + dr), col(order, 3), st_spec(order), col(order, 0)]
    in_specs.append(_full(lbraw.shape))
    out_specs = [col(rfw, 0)] * 3 + [col(rbw, 0)] * 3 + [_whole((1, hw))]
    out_shape = [jax.ShapeDtypeStruct((T, hw), ACT)] * 6 + [jax.ShapeDtypeStruct((1, hw), F32)]
    return _host_call(
        body, ex, lambda: pl.program_id(0) == 0, lambda: pl.program_id(0) == nt - 1,
        name=name, grid=(nt,), in_specs=in_specs, out_specs=out_specs, out_shape=out_shape,
        scratch_shapes=[pltpu.VMEM((2, HG_HEADS, d, d), F32)], sem=("arbitrary",),
        args=(pb, pb, pb, s_fw, do, pb, pb, pb, s_bw, do, lbraw))


def _ret_log_gamma(h, rev):
    hh = RET_HEADS - 1 - h if rev else h
    return math.log(1.0 - 2.0 ** (-5.0 - hh))


def _rope(x, cos, sin):
    half = x.shape[1] // 2
    x1, x2 = x[:, :half], x[:, half:]
    return jnp.concatenate([x1 * cos - x2 * sin, x2 * cos + x1 * sin], axis=1)


def _unrope(dy, cos, sin):
    half = dy.shape[1] // 2
    d1, d2 = dy[:, :half], dy[:, half:]
    return jnp.concatenate([d1 * cos + d2 * sin, d2 * cos - d1 * sin], axis=1)


def _ret_decays(lg, rev):
    c = RET_CHUNK
    r = lax.broadcasted_iota(jnp.int32, (c, c), 0)
    k = lax.broadcasted_iota(jnp.int32, (c, c), 1)
    rel = (k - r) if rev else (r - k)
    dm = jnp.where(rel >= 0, jnp.exp(lg * jnp.maximum(rel, 0).astype(F32)), 0.0)
    pos = lax.broadcasted_iota(jnp.int32, (c, 1), 0).astype(F32)
    if rev:
        qdec = jnp.exp(lg * (c - pos))
        kdec = jnp.exp(lg * pos)
    else:
        qdec = jnp.exp(lg * (pos + 1.0))
        kdec = jnp.exp(lg * (c - 1.0 - pos))
    return dm, qdec, kdec


def _ret_fwd(q, k, v, cos, sin, lc, name, ex=None):
    T = q.shape[0]
    c, dk, dv = RET_CHUNK, RET_DK, RET_DV
    nt, nc = T // c, lc // c
    kscale = dk ** -0.5

    def body(qf, kf, vf, cf, sf_, qb, kb, vb, cb, sb_, of_ref, ob_ref, stf_ref, stb_ref, st_ref):
        @pl.when(pl.program_id(0) == 0)
        def _():
            st_ref[...] = jnp.zeros_like(st_ref)

        sets = ((qf, kf, vf, cf, sf_, of_ref, stf_ref), (qb, kb, vb, cb, sb_, ob_ref, stb_ref))
        combos = [(dr, h) for dr in range(2) for h in range(RET_HEADS)]
        prep = {}
        for dr, (q_ref, k_ref, v_ref, c_ref, s_ref, _, _) in enumerate(sets):
            rev = dr == 1
            cos_v, sin_v = c_ref[...], s_ref[...]
            for h in range(RET_HEADS):
                lg = _ret_log_gamma(h, rev)
                dm, qdec, kdec = _ret_decays(lg, rev)
                qh = _rope(q_ref[:, h * dk:(h + 1) * dk], cos_v, sin_v)
                kh = _rope(k_ref[:, h * dk:(h + 1) * dk], cos_v, sin_v) * kscale
                prep[dr, h] = dict(qb=_bf(qh), kb=_bf(kh), qin=_bf(qh * qdec), kin=_bf(kh * kdec),
                                   v=_bf(v_ref[:, h * dv:(h + 1) * dv]), dm=dm, decay=math.exp(lg * c))
        sc = {ch: _bf(_nt(prep[ch]["qb"], prep[ch]["kb"]) * prep[ch]["dm"]) for ch in combos}
        for dr, h in combos:
            p = prep[dr, h]
            o_ref, so_ref = sets[dr][5], sets[dr][6]
            st = st_ref[dr, h]
            stb = _bf(st)
            so_ref[0, h] = stb
            o_ref[:, h * dv:(h + 1) * dv] = _nn(sc[dr, h], p["v"]) + _nt(p["qin"], stb)
            st_ref[dr, h] = st * p["decay"] + _tn(p["v"], p["kin"])

    def spec(order, width):
        return pl.BlockSpec((c, width), lambda s: (order(s, nc, nt), 0))

    def st_spec(order):
        return pl.BlockSpec((1, RET_HEADS, dv, dk), lambda s: (order(s, nc, nt), 0, 0, 0))

    in_specs = []
    for order in (_fw_chunk, _bw_chunk):
        in_specs += [spec(order, RET_HEADS * dk), spec(order, RET_HEADS * dk), spec(order, RET_HEADS * dv),
                     spec(order, dk // 2), spec(order, dk // 2)]
    return _host_call(
        body, ex, lambda: pl.program_id(0) == 0, lambda: pl.program_id(0) == nt - 1,
        name=name, grid=(nt,), in_specs=in_specs,
        out_specs=[spec(_fw_chunk, RET_HEADS * dv), spec(_bw_chunk, RET_HEADS * dv), st_spec(_fw_chunk), st_spec(_bw_chunk)],
        out_shape=[jax.ShapeDtypeStruct((T, RET_HEADS * dv), F32), jax.ShapeDtypeStruct((T, RET_HEADS * dv), F32),
                   jax.ShapeDtypeStruct((nt, RET_HEADS, dv, dk), ACT), jax.ShapeDtypeStruct((nt, RET_HEADS, dv, dk), ACT)],
        scratch_shapes=[pltpu.VMEM((2, RET_HEADS, dv, dk), F32)], sem=("arbitrary",),
        args=(q, k, v, cos, sin, q, k, v, cos, sin))


def _ret_bwd(q, k, v, cos, sin, s_fw, s_bw, do, lc, name, ex=None):
    T = q.shape[0]
    c, dk, dv = RET_CHUNK, RET_DK, RET_DV
    nt, nc = T // c, lc // c
    kscale = dk ** -0.5

    def rfw(s, nc_, nt_):
        return _fw_chunk(nt_ - 1 - s, nc_, nt_)

    def rbw(s, nc_, nt_):
        return _bw_chunk(nt_ - 1 - s, nc_, nt_)

    def body(qf, kf, vf, cf, sf_, stf, dof, qb, kb, vb, cb, sb_, stb_, dob_,
             dqf, dkf, dvf, dqb, dkb, dvb, dst_ref):
        @pl.when(pl.program_id(0) == 0)
        def _():
            dst_ref[...] = jnp.zeros_like(dst_ref)

        sets = ((qf, kf, vf, cf, sf_, stf, dof, dqf, dkf, dvf), (qb, kb, vb, cb, sb_, stb_, dob_, dqb, dkb, dvb))
        combos = [(dr, h) for dr in range(2) for h in range(RET_HEADS)]
        prep = {}
        for dr, (q_ref, k_ref, v_ref, c_ref, s_ref, _, do_ref, _, _, _) in enumerate(sets):
            rev = dr == 1
            cos_v, sin_v = c_ref[...], s_ref[...]
            for h in range(RET_HEADS):
                lg = _ret_log_gamma(h, rev)
                dm, qdec, kdec = _ret_decays(lg, rev)
                qh = _rope(q_ref[:, h * dk:(h + 1) * dk], cos_v, sin_v)
                kh = _rope(k_ref[:, h * dk:(h + 1) * dk], cos_v, sin_v) * kscale
                prep[dr, h] = dict(qb=_bf(qh), kb=_bf(kh), qin=_bf(qh * qdec), kin=_bf(kh * kdec),
                                   v=_bf(v_ref[:, h * dv:(h + 1) * dv]), dob=_bf(do_ref[:, h * dv:(h + 1) * dv]),
                                   dm=dm, qdec=qdec, kdec=kdec, decay=math.exp(lg * c), cos=cos_v, sin=sin_v)
        sc = {ch: _bf(_nt(prep[ch]["qb"], prep[ch]["kb"]) * prep[ch]["dm"]) for ch in combos}
        dsc = {ch: _bf(_nt(prep[ch]["dob"], prep[ch]["v"]) * prep[ch]["dm"]) for ch in combos}
        carried = {}
        for dr, h in combos:
            p = prep[dr, h]
            dv_ref = sets[dr][9]
            dst = dst_ref[dr, h]
            dstb = _bf(dst)
            carried[dr, h] = dstb
            dv_ref[:, h * dv:(h + 1) * dv] = _bf(_tn(sc[dr, h], p["dob"]) + _nt(p["kin"], dstb))
            dst_ref[dr, h] = _tn(p["dob"], p["qin"]) + dst * p["decay"]
        for dr, h in combos:
            p = prep[dr, h]
            st_in, dq_ref, dk_ref = sets[dr][5], sets[dr][7], sets[dr][8]
            dq_r = _nn(dsc[dr, h], p["kb"]) + _nn(p["dob"], st_in[0, h]) * p["qdec"]
            dk_r = _tn(dsc[dr, h], p["qb"]) + _nn(p["v"], carried[dr, h]) * p["kdec"]
            dq_ref[:, h * dk:(h + 1) * dk] = _bf(_unrope(dq_r, p["cos"], p["sin"]))
            dk_ref[:, h * dk:(h + 1) * dk] = _bf(_unrope(dk_r * kscale, p["cos"], p["sin"]))

    def spec(order, width):
        return pl.BlockSpec((c, width), lambda s: (order(s, nc, nt), 0))

    def st_spec(order):
        return pl.BlockSpec((1, RET_HEADS, dv, dk), lambda s: (order(s, nc, nt), 0, 0, 0))

    in_specs = []
    for order in (rfw, rbw):
        in_specs += [spec(order, RET_HEADS * dk), spec(order, RET_HEADS * dk), spec(order, RET_HEADS * dv),
                     spec(order, dk // 2), spec(order, dk // 2), st_spec(order), spec(order, RET_HEADS * dv)]
    out_specs, out_shape = [], []
    for order in (rfw, rbw):
        out_specs += [spec(order, RET_HEADS * dk), spec(order, RET_HEADS * dk), spec(order, RET_HEADS * dv)]
        out_shape += [jax.ShapeDtypeStruct((T, RET_HEADS * dk), ACT), jax.ShapeDtypeStruct((T, RET_HEADS * dk), ACT),
                      jax.ShapeDtypeStruct((T, RET_HEADS * dv), ACT)]
    return _host_call(
        body, ex, lambda: pl.program_id(0) == 0, lambda: pl.program_id(0) == nt - 1,
        name=name, grid=(nt,), in_specs=in_specs, out_specs=out_specs, out_shape=out_shape,
        scratch_shapes=[pltpu.VMEM((2, RET_HEADS, dv, dk), F32)], sem=("arbitrary",),
        args=(q, k, v, cos, sin, s_fw, do, q, k, v, cos, sin, s_bw, do))


def _attn_rope_tables(lc, l):
    t = jnp.arange(l)
    row = (t // GRID_W).astype(F32)
    colp = (t % GRID_W).astype(F32)
    n_freq = HEAD_DIM // 4
    inv = 10000.0 ** (-jnp.arange(n_freq, dtype=F32) / n_freq)
    ang = jnp.concatenate([row[:, None] * inv, colp[:, None] * inv], axis=-1)
    cos = jnp.concatenate([jnp.ones((lc, HEAD_DIM // 2), F32), jnp.cos(ang)], axis=0)
    sin = jnp.concatenate([jnp.zeros((lc, HEAD_DIM // 2), F32), jnp.sin(ang)], axis=0)
    return jnp.concatenate([cos, cos], axis=1), jnp.concatenate([-sin, sin], axis=1)


def _ret_rope_tables(lc, l):
    theta = 1.0 / (10000.0 ** jnp.linspace(0.0, 1.0, RET_DK // 2, dtype=F32))
    ang = jnp.arange(l, dtype=F32)[:, None] * theta
    cos = jnp.concatenate([jnp.ones((lc, RET_DK // 2), F32), jnp.cos(ang)], axis=0)
    sin = jnp.concatenate([jnp.zeros((lc, RET_DK // 2), F32), jnp.sin(ang)], axis=0)
    return cos, sin


def _heads_major(slab, n_heads):
    t = slab.shape[0]
    return slab.reshape(t, n_heads, HEAD_DIM).transpose(1, 0, 2)


def _slab(hm):
    nh, t, hd = hm.shape
    return hm.transpose(1, 0, 2).reshape(t, nh * hd)


COL_SHARDED = ("ffn_in0", "ffn_in1", "even_in", "odd_in")


def _full_weight(name, g):
    if name in COL_SHARDED:
        return g.transpose(1, 0, 2).reshape(g.shape[1], -1)
    return g.reshape(-1, g.shape[2])


def _shard_slots(name, g):
    if name in COL_SHARDED:
        return g.reshape(g.shape[0], N_DEV, -1).transpose(1, 0, 2)
    return g.reshape(N_DEV, -1, g.shape[1])


def _local_step(xs, target, mv, norm_g, w, qk_g, sink, hg_out_g, lbraw, lc, shards=None):
    T, dm = xs.shape
    l = T - lc
    tm = lc
    blk = ATTN_BLOCK
    d2, d3 = 2 * dm, 3 * dm
    w = dict(w)
    gw, recv = {}, {}

    def ms(layer, a, b):
        return mv[layer, :, :, a:b]

    def gather(names):
        return None if shards is None else _Exchange(GATHER, [shards[n] for n in names])

    def arrived(names, got):
        for n, g in zip(names, got):
            w[n] = _full_weight(n, g)

    def scatter(names):
        return None if shards is None else _Exchange(SCATTER, [_shard_slots(n, gw[n]) for n in names])

    def scattered(names, got):
        for n, g in zip(names, got):
            recv[n] = g

    g00, g01, g10, g11 = (norm_g[i, j][None, :] for i in (0, 1) for j in (0, 1))

    riding = ["even_out"]
    (pa, pb), got = _pre_fwd(xs, g00, ms(0, 0, d2), w["even_in"], ((0, 768), (768, 3328)), tm, "pre0_fwd", gather(riding))
    arrived(riding, got)
    cos2, sin2 = _attn_rope_tables(lc, l)
    cosp, sinp = jnp.concatenate([cos2, cos2], axis=1), jnp.concatenate([sin2, sin2], axis=1)
    gains5 = jnp.concatenate([jnp.broadcast_to(jnp.tile(qk_g[0], 2), (N_PAIRS - 1, PAIR)), jnp.tile(qk_g[1], 2)[None]])[:, None, :]
    qt, ks, vs = _qk_slab_fwd(pa, gains5, cosp, sinp, tm, "qk_prep_fwd")
    sinkb = jnp.broadcast_to(sink.reshape(ATTN_KV, 4, 1, 1), (ATTN_KV, 4, blk, 1)).reshape(ATTN_KV, 4 * blk, 1)
    riding = ["ffn_in0"]
    (a_slab, lse), got = _attn_slab_fwd(qt, ks, vs, sinkb, lc, "attn_fwd", gather(riding))
    arrived(riding, got)
    riding = ["ffn_out0", "odd_out"]
    (hg_of, hg_ob, hg_sf, hg_sb), got = _gla_fwd(pb, lbraw, lc, "hgrn_fwd", gather(riding))
    arrived(riding, got)
    x01, z0 = _post_fwd(xs, hg_of, hg_ob, pb, 4, hg_out_g, a_slab, w["even_out"], ms(0, d2, d3), HG_D, tm, "post0_fwd")
    riding = ["odd_in"]
    (x02, u0, f0), got = _ffn_fwd(x01, g01, ms(0, d3, 6 * dm), w["ffn_in0"], w["ffn_out0"], tm, "ffn0_fwd", ex=gather(riding))
    arrived(riding, got)

    riding = ["ffn_out1"]
    (rq, rk, rv, rg), got = _pre_fwd(x02, g10, ms(1, 0, d2), w["odd_in"],
                                     ((0, 1024), (1024, 2048), (2048, 4096), (4096, 6144)), tm, "pre1_fwd", gather(riding))
    arrived(riding, got)
    rcos, rsin = _ret_rope_tables(lc, l)
    riding = ["ffn_in1"]
    (rt_of, rt_ob, rt_sf, rt_sb), got = _ret_fwd(rq, rk, rv, rcos, rsin, lc, "ret_fwd", gather(riding))
    arrived(riding, got)
    x11, z1 = _post_fwd(x02, rt_of, rt_ob, rg, 0, None, None, w["odd_out"], ms(1, d2, d3), RET_DV, tm, "post1_fwd")
    (dx, u1, f1, loss), _ = _ffn_fwd(x11, g11, ms(1, d3, 6 * dm), w["ffn_in1"], w["ffn_out1"], tm, "ffn1_fwd", target)

    (dx, h, du, act, df, dms_f1, dg11), _ = _ffn_bwd(x11, dx, u1, f1, g11, ms(1, d3, 6 * dm), w["ffn_in1"], w["ffn_out1"], tm,
                                                     "ffn1_bwd")
    gw["ffn_in1"] = _wgrad(h, du, "wg_ffn_in1")
    gw["ffn_out1"] = _wgrad(act, df, "wg_ffn_out1")
    do1, dgr1, dy1, z1_t, dgate_p1, _ = _post_bwd(dx, z1, rt_of, rt_ob, rg, 0, None, w["odd_out"], ms(1, d2, d3), 0, RET_DV, tm,
                                                  "post1_bwd")
    gw["odd_out"] = _wgrad(z1_t, dy1, "wg_odd_out")
    riding = ["ffn_in1", "ffn_out1"]
    (dqf, dkf, dvf, dqb, dkb, dvb), got = _ret_bwd(rq, rk, rv, rcos, rsin, rt_sf, rt_sb, do1, lc, "ret_bwd", scatter(riding))
    scattered(riding, got)
    riding = ["odd_out"]
    (dx, h, dp, dms_p1, dg10), got = _pre_bwd(x02, dx, g10, ms(1, 0, d2), w["odd_in"],
                                              [(0, [dqf, dqb]), (1024, [dkf, dkb]), (2048, [dvf, dvb]), (4096, [dgr1])], tm,
                                              "pre1_bwd", ex=scatter(riding))
    scattered(riding, got)
    gw["odd_in"] = _wgrad(h, dp, "wg_odd_in")

    riding = ["odd_in"]
    (dx, h, du, act, df, dms_f0, dg01), got = _ffn_bwd(x01, dx, u0, f0, g01, ms(0, d3, 6 * dm), w["ffn_in0"], w["ffn_out0"], tm,
                                                       "ffn0_bwd", scatter(riding))
    scattered(riding, got)
    gw["ffn_in0"] = _wgrad(h, du, "wg_ffn_in0")
    gw["ffn_out0"] = _wgrad(act, df, "wg_ffn_out0")
    do0, dgr0, da0, dy0, z0_t, dgate_p0, d_hg_gain = _post_bwd(dx, z0, hg_of, hg_ob, pb, 4, hg_out_g, w["even_out"],
                                                              ms(0, d2, d3), 512, HG_D, tm, "post0_bwd")
    gw["even_out"] = _wgrad(z0_t, dy0, "wg_even_out")
    riding = ["ffn_in0", "ffn_out0"]
    (hq_f, hz_f, hv_f, hq_b, hz_b, hv_b, dlb), got = _gla_bwd(pb, lbraw, hg_sf, hg_sb, do0, lc, "hgrn_bwd", scatter(riding))
    scattered(riding, got)
    riding = ["even_out"]
    (dq_att, dk_att, dv_att, dsink), got = _attn_slab_bwd(qt, ks, vs, sinkb, a_slab, lse, da0, lc, "attn_bwd", scatter(riding))
    scattered(riding, got)
    dqk_raw, dgain5 = _qk_slab_bwd(dq_att, dk_att, pa, gains5, cosp, sinp, tm, "qk_prep_bwd")
    pieces0 = [(0, [dqk_raw]), (640, [dv_att]),
               (768, [hq_f, hq_b]), (1280, [hz_f]), (1792, [hz_b]), (2304, [hv_f, hv_b]), (2816, [dgr0])]
    (dx, h, dp, dms_p0, dg00), _ = _pre_bwd(xs, dx, g00, ms(0, 0, d2), w["even_in"], pieces0, tm, "pre0_bwd",
                                            latent_dx=shards is not None)
    gw["even_in"] = _wgrad(h, dp, "wg_even_in")

    dmv = jnp.stack([jnp.concatenate([dms_p0, dgate_p0, dms_f0], axis=2), jnp.concatenate([dms_p1, dgate_p1, dms_f1], axis=2)])
    small = {
        "dmv": dmv,
        "norm_g": jnp.stack([jnp.stack([dg00[0], dg01[0]]), jnp.stack([dg10[0], dg11[0]])]),
        "qk_g": jnp.stack([jnp.sum(dgain5[:N_PAIRS - 1, 0].reshape(-1, HEAD_DIM), axis=0),
                           jnp.sum(dgain5[N_PAIRS - 1, 0].reshape(-1, HEAD_DIM), axis=0)]),
        "sink": dsink.reshape(ATTN_HEADS),
        "hg_out_g": d_hg_gain[0],
        "lb": dlb[0],
        "loss": loss[0, 0],
    }
    if shards is not None:
        gw = {n: recv.get(n, g) for n, g in gw.items()}
    return loss, dx, gw, small


HBM_SPEC = pl.BlockSpec(memory_space=pltpu.HBM)


def _my_index():
    return 4 * lax.axis_index("x") + 2 * lax.axis_index("y") + lax.axis_index("c")


def _peer(k):
    pos = []
    for axis, bit in (("x", 4), ("y", 2), ("c", 1)):
        a = lax.axis_index(axis)
        pos.append(1 - a if k & bit else a)
    return tuple(pos)


def _peer_index(k):
    px, py, pc = _peer(k)
    return 4 * px + 2 * py + pc


GATHER, SCATTER = "gather", "scatter"


class _Exchange:
    def __init__(self, mode, arrays):
        self.mode, self.arrays, self.n = mode, list(arrays), len(arrays)

    def out_shape(self):
        if self.mode == GATHER:
            return [jax.ShapeDtypeStruct((N_DEV,) + a.shape, a.dtype) for a in self.arrays]
        return [jax.ShapeDtypeStruct(a.shape, a.dtype) for a in self.arrays]

    def specs(self):
        return [HBM_SPEC] * self.n

    def scratch(self):
        return [pltpu.SemaphoreType.DMA((self.n, N_DEV - 1)), pltpu.SemaphoreType.DMA((self.n, N_DEV - 1)),
                pltpu.SemaphoreType.DMA((self.n,))]

    def _copies(self, in_refs, out_refs, send_sems, recv_sems, local_sems, landing):
        me = _my_index()
        local, remote = [], []
        for a, (src, dst) in enumerate(zip(in_refs, out_refs)):
            part = (lambda j, s=src: s) if self.mode == GATHER else (lambda j, s=src: s.at[j])
            local.append(pltpu.make_async_copy(part(me), dst.at[me], local_sems.at[a]))
            for k in range(1, N_DEV):
                pj = _peer_index(k)
                remote.append(pltpu.make_async_remote_copy(
                    src_ref=part(pj), dst_ref=dst.at[pj if landing else me], send_sem=send_sems.at[a, k - 1],
                    recv_sem=recv_sems.at[a, k - 1], device_id=_peer(k), device_id_type=MESH))
        return local, remote

    def start(self, in_refs, out_refs, sems):
        local, remote = self._copies(in_refs, out_refs, *sems, landing=False)
        for cp in local + remote:
            cp.start()

    def wait(self, in_refs, out_refs, sems):
        local, remote = self._copies(in_refs, out_refs, *sems, landing=True)
        for cp in remote:
            cp.wait_send()
            cp.wait_recv()
        for cp in local:
            cp.wait()

    def ride(self, refs, n_in, n_out, first, last):
        refs = list(refs)
        n = self.n
        x_in = refs[n_in:n_in + n]
        x_out = refs[n_in + n + n_out:n_in + 2 * n + n_out]
        sems = refs[n_in + 2 * n + n_out:n_in + 2 * n + n_out + 3]

        @pl.when(first)
        def _():
            self.start(x_in, x_out, sems)

        @pl.when(last)
        def _():
            self.wait(x_in, x_out, sems)

        return refs[:n_in] + refs[n_in + n:n_in + n + n_out] + refs[n_in + 2 * n + n_out + 3:]

    def call(self, name):
        n = self.n

        def body(*refs):
            ins, outs, sems = refs[:n], refs[n:2 * n], refs[2 * n:]
            self.start(ins, outs, sems)
            self.wait(ins, outs, sems)

        return pl.pallas_call(body, name=name, in_specs=self.specs(), out_specs=self.specs(), out_shape=self.out_shape(),
                              scratch_shapes=self.scratch())(*self.arrays)


def _all_gather(v, name):
    return _Exchange(GATHER, [v]).call(name)[0]


def _hosted(kernel_body, ex, n_in, n_out, first, last):
    if ex is None:
        return kernel_body

    def body(*refs):
        kernel_body(*ex.ride(refs, n_in, n_out, first(), last()))

    return body


def _host_call(kernel_body, ex, first, last, name, grid, in_specs, out_specs, out_shape, scratch_shapes, sem, args):
    n_in, n_out = len(in_specs), len(out_specs)
    if ex is None:
        outs = pl.pallas_call(kernel_body, name=name, grid=grid, in_specs=in_specs, out_specs=out_specs, out_shape=out_shape,
                              scratch_shapes=scratch_shapes, compiler_params=_cp(*sem))(*args)
        return list(outs), []
    outs = pl.pallas_call(
        _hosted(kernel_body, ex, n_in, n_out, first, last), name=name, grid=grid,
        in_specs=list(in_specs) + ex.specs(), out_specs=list(out_specs) + ex.specs(),
        out_shape=list(out_shape) + ex.out_shape(), scratch_shapes=ex.scratch() + list(scratch_shapes),
        compiler_params=_cp(*sem))(*args, *ex.arrays)
    return list(outs[:n_out]), list(outs[n_out:])


def _mod_fwd(call, mod_w, bias, name):
    nl, dm, n = mod_w.shape

    def body(c_ref, w_ref, b_ref, o_ref):
        cv = c_ref[...]
        cond = _bf(cv * _sig(cv))
        for layer in range(nl):
            o_ref[layer] = _nn(cond, _bf(w_ref[layer])) + b_ref[layer]

    return pl.pallas_call(
        body, name=name, out_shape=jax.ShapeDtypeStruct((nl, call.shape[0], n), F32),
        compiler_params=pltpu.CompilerParams(vmem_limit_bytes=VMEM_LIMIT),
    )(call, mod_w, bias)


def _mod_bwd(call, dm_all, mod_w, name):
    nl, dm, n = mod_w.shape

    def body(c_ref, d_ref, w_ref, gw_ref, dc_ref):
        cv = c_ref[...]
        cond = _bf(cv * _sig(cv))
        dc = jnp.zeros(cv.shape, F32)
        for layer in range(nl):
            db = _bf(d_ref[layer])
            gw_ref[layer] = _tn(cond, db)
            dc = dc + _nt(db, _bf(w_ref[layer]))
        dc_ref[...] = dc

    return pl.pallas_call(
        body, name=name,
        out_shape=[jax.ShapeDtypeStruct(mod_w.shape, F32), jax.ShapeDtypeStruct(call.shape, F32)],
        compiler_params=pltpu.CompilerParams(vmem_limit_bytes=VMEM_LIMIT),
    )(call, dm_all, mod_w)


def _sum_parts(g, name):
    def body(g_ref, o_ref):
        acc = g_ref[0]
        for j in range(1, g.shape[0]):
            acc = acc + g_ref[j]
        o_ref[...] = acc

    return pl.pallas_call(body, name=name, out_shape=jax.ShapeDtypeStruct(g.shape[1:], g.dtype))(g)


def _small_finish(dcond_g, c_ctx, dlb, lbraw, dm_ctx, dm_lat, name):
    def body(dc_ref, c_ref, dlb_ref, lb_ref, mc_ref, ml_ref, gc_ref, glb_ref, gb_ref):
        acc = dc_ref[0, 0:1, :]
        for j in range(1, N_DEV):
            acc = acc + dc_ref[j, 0:1, :]
        cv = c_ref[...]
        s = _sig(cv)
        gc_ref[...] = acc * (s * (1.0 + cv * (1.0 - s)))
        lb = _lower_bound(lb_ref)
        d0 = dlb_ref[...] * lb * (1.0 - lb)
        glb_ref[0:1, :] = d0
        glb_ref[1:2, :] = -d0
        gb_ref[...] = mc_ref[...] + ml_ref[...]

    return pl.pallas_call(
        body, name=name,
        out_shape=[jax.ShapeDtypeStruct(c_ctx.shape, F32), jax.ShapeDtypeStruct(lbraw.shape, F32),
                   jax.ShapeDtypeStruct(dm_ctx.shape, F32)],
    )(dcond_g, c_ctx, dlb, lbraw, dm_ctx, dm_lat)


def _row_tile(r, cap, mult):
    best = r
    for t in range(mult, min(r, cap) + 1, mult):
        if r % t == 0:
            best = t
    return best


def _adam(g_list, w, m, v, name, ex=None):
    nl, r, cdim = w.shape
    p = g_list[0].shape[0]
    tr = _row_tile(r, 128, 16)
    ni = r // tr

    def body(*refs):
        g_refs = refs[:nl]
        w_ref, m_ref, v_ref, go_ref, d_ref, mo_ref, vo_ref = refs[nl:]
        layer = pl.program_id(0)

        def total(g_ref):
            acc = g_ref[0].astype(F32)
            for j in range(1, p):
                acc = acc + g_ref[j].astype(F32)
            return acc

        g = total(g_refs[0])
        for k in range(1, nl):
            g = jnp.where(layer == k, total(g_refs[k]), g)
        m2 = ADAM_B1 * m_ref[0] + (1.0 - ADAM_B1) * g
        v2 = ADAM_B2 * v_ref[0] + (1.0 - ADAM_B2) * (g * g)
        m_hat = m2 / (1.0 - ADAM_B1 ** ADAM_STEP)
        v_hat = v2 / (1.0 - ADAM_B2 ** ADAM_STEP)
        go_ref[0] = g
        d_ref[0] = -ADAM_LR * (m_hat / (jnp.sqrt(v_hat) + ADAM_EPS) + ADAM_WD * w_ref[0])
        mo_ref[0] = m2
        vo_ref[0] = v2

    def g_spec(k):
        return pl.BlockSpec((p, tr, cdim), lambda la, i: (0, jnp.where(la == k, i, jnp.where(la < k, 0, ni - 1)), 0))

    spec = pl.BlockSpec((1, tr, cdim), lambda la, i: (la, i, 0))
    return _host_call(
        body, ex, lambda: (pl.program_id(0) == 0) & (pl.program_id(1) == 0),
        lambda: (pl.program_id(0) == nl - 1) & (pl.program_id(1) == ni - 1),
        name=name, grid=(nl, ni),
        in_specs=[g_spec(k) for k in range(nl)] + [spec, spec, spec],
        out_specs=[spec] * 4, out_shape=[jax.ShapeDtypeStruct((nl, r, cdim), F32)] * 4,
        scratch_shapes=[], sem=("arbitrary", "arbitrary"), args=(*g_list, w, m, v))


def _f32_as_rows(a, width):
    return lax.bitcast_convert_type(a.reshape(-1), BF16).reshape(-1, width)


def _rows_as_f32(rows):
    return lax.bitcast_convert_type(rows.reshape(rows.shape[:-2] + (-1, 2)), F32)


def _pad_rows(a, mult):
    r = (-a.shape[-2]) % mult
    if r == 0:
        return a
    widths = [(0, 0)] * (a.ndim - 2) + [(0, r), (0, 0)]
    return jnp.pad(a, widths)


def _pack_flat(parts, lane):
    flat = jnp.concatenate([p.reshape(-1).astype(F32) for p in parts])
    n = flat.shape[0]
    rows = -(-n // lane)
    rows += (-rows) % 8
    return jnp.pad(flat, (0, rows * lane - n)).reshape(rows, lane)


def _unpack_flat(packed, shapes):
    flat = packed.reshape(-1)
    out, off = [], 0
    for s in shapes:
        n = math.prod(s)
        out.append(flat[off:off + n].reshape(s))
        off += n
    return out


def kernel(x, c, ctx, c_ctx, mod_w, mod_b, norm_g, ffn_w_in, ffn_w_out, even_w_in, even_w_out, attn_qk_norm_g, attn_sink, hgrn_out_norm_g, hgrn_lb, odd_w_in, odd_w_out, loss_target, m_c_ctx, m_mod_w, m_mod_b, m_norm_g, m_ffn_w_in, m_ffn_w_out, m_even_w_in, m_even_w_out, m_attn_qk_norm_g, m_attn_sink, m_hgrn_out_norm_g, m_hgrn_lb, m_odd_w_in, m_odd_w_out, v_c_ctx, v_mod_w, v_mod_b, v_norm_g, v_ffn_w_in, v_ffn_w_out, v_even_w_in, v_even_w_out, v_attn_qk_norm_g, v_attn_sink, v_hgrn_out_norm_g, v_hgrn_lb, v_odd_w_in, v_odd_w_out):
    me = _my_index()
    lc, dm = ctx.shape[1], x.shape[2]
    nmod = mod_w.shape[2]
    big = (ffn_w_in, ffn_w_out, even_w_in, even_w_out, odd_w_in, odd_w_out)

    extra = _pad_rows(jnp.concatenate([_f32_as_rows(c, dm), _f32_as_rows(norm_g, dm)], axis=0), 16)
    shards = {"ffn_in0": ffn_w_in[0], "ffn_in1": ffn_w_in[1], "ffn_out0": ffn_w_out[0], "ffn_out1": ffn_w_out[1],
              "even_in": even_w_in[0], "even_out": even_w_out[0], "odd_in": odd_w_in[0], "odd_out": odd_w_out[0]}
    shards = {n: a.astype(BF16) for n, a in shards.items()}
    first = _Exchange(GATHER, [shards["even_in"], extra]).call("gather_first")
    w = {"even_in": _full_weight("even_in", first[0])}
    c_all = _rows_as_f32(first[1][:, 0:2])
    norm_g_all = _rows_as_f32(first[1][:, 2:3]).reshape(N_DEV, 2, 2, -1)
    norm_g_full = norm_g_all.transpose(1, 2, 0, 3).reshape(2, 2, dm)

    call = jnp.concatenate([c_all, c_ctx[None, :], jnp.zeros((16 - N_DEV - 1, dm), F32)], axis=0)
    bias = lax.dynamic_slice_in_dim(mod_b, me * nmod, nmod, axis=1)[:, None, :]
    m_sh = _mod_fwd(call, mod_w, bias, "mod_fwd")
    m_g = _all_gather(m_sh.reshape(-1, nmod), "gather_mod").reshape(N_DEV, 2, 16, nmod)
    m_all = m_g.transpose(1, 2, 0, 3).reshape(2, 16, -1)
    m_lat = lax.dynamic_index_in_dim(m_all, me, axis=1, keepdims=False)
    mv = jnp.stack([m_all[:, N_DEV], m_lat], axis=1)[:, :, None, :]

    xs = jnp.concatenate([ctx[0], x[0]], axis=0)
    _, dxs, gw, small = _local_step(xs, loss_target[0], mv, norm_g_full, w, attn_qk_norm_g[0], attn_sink[0],
                                    hgrn_out_norm_g, hgrn_lb, lc, shards)
    grad_x = dxs[None]

    last = _Exchange(SCATTER, [_shard_slots("even_in", gw["even_in"])])
    big_g = [[gw["ffn_in0"], gw["ffn_in1"]], [gw["ffn_out0"], gw["ffn_out1"]], None, [gw["even_out"]],
             [gw["odd_in"]], [gw["odd_out"]]]
    big_m = (m_ffn_w_in, m_ffn_w_out, m_even_w_in, m_even_w_out, m_odd_w_in, m_odd_w_out)
    big_v = (v_ffn_w_in, v_ffn_w_out, v_even_w_in, v_even_w_out, v_odd_w_in, v_odd_w_out)
    big_names = ("ffn_w_in", "ffn_w_out", "even_w_in", "even_w_out", "odd_w_in", "odd_w_out")
    big_out = [None] * 6
    for i in (0, 1, 3, 4, 5, 2):
        big_out[i], got = _adam(big_g[i], big[i], big_m[i], big_v[i], "adam_" + big_names[i], last if i == 0 else None)
        if i == 0:
            big_g[2] = [got[0]]
    big_res = [[big_out[i][k] for i in range(6)] for k in range(4)]

    dmv = small["dmv"]
    small_shapes = [(2, 6 * dm), (2, 6 * dm), (2, 2, dm), (2, HEAD_DIM), (ATTN_HEADS,), (HG_D,), (HG_HEADS * HG_D,), (1,)]
    vec = _pack_flat([dmv[:, 0, 0], dmv[:, 1, 0], small["norm_g"], small["qk_g"], small["sink"], small["hg_out_g"],
                      small["lb"], small["loss"]], 128)
    vec_g = _all_gather(vec, "gather_small")
    tot = _unpack_flat(_sum_parts(vec_g, "sum_small"), small_shapes)
    dm_ctx_tot, dm_lat_tot, g_norm_full, g_qk, g_sink, g_hg, dlb_tot, loss_tot = tot
    dm_lat_each = vec_g.reshape(N_DEV, -1)[:, 12 * dm:24 * dm].reshape(N_DEV, 2, 6 * dm)
    dm_lat_mine = lax.dynamic_slice_in_dim(dm_lat_each, me * nmod, nmod, axis=2).transpose(1, 0, 2)
    dm_ctx_mine = lax.dynamic_slice_in_dim(dm_ctx_tot, me * nmod, nmod, axis=1)[:, None, :]
    dm_all = jnp.concatenate([dm_lat_mine, dm_ctx_mine, jnp.zeros((2, 16 - N_DEV - 1, nmod), F32)], axis=1)
    g_mod_w, dcond = _mod_bwd(call, dm_all, mod_w, "mod_bwd")
    dcond_g = _all_gather(dcond[N_DEV:], "gather_dcond")
    g_c_ctx, g_lb, g_mod_b = _small_finish(dcond_g, c_ctx[None, :], dlb_tot[None, :], hgrn_lb, dm_ctx_tot, dm_lat_tot,
                                           "small_finish")
    g_norm = lax.dynamic_slice_in_dim(g_norm_full, me * norm_g.shape[2], norm_g.shape[2], axis=2)

    mod_res, _ = _adam([g_mod_w[0][None], g_mod_w[1][None]], mod_w, m_mod_w, v_mod_w, "adam_mod_w")

    sm_w = (c_ctx, mod_b, norm_g, attn_qk_norm_g, attn_sink, hgrn_out_norm_g, hgrn_lb)
    sm_m = (m_c_ctx, m_mod_b, m_norm_g, m_attn_qk_norm_g, m_attn_sink, m_hgrn_out_norm_g, m_hgrn_lb)
    sm_v = (v_c_ctx, v_mod_b, v_norm_g, v_attn_qk_norm_g, v_attn_sink, v_hgrn_out_norm_g, v_hgrn_lb)
    sm_g = (g_c_ctx, g_mod_b, g_norm, g_qk, g_sink, g_hg, g_lb)
    sm_shapes = [a.shape for a in sm_w]
    sm_out, _ = _adam([_pack_flat(sm_g, 128)[None]], _pack_flat(sm_w, 128)[None], _pack_flat(sm_m, 128)[None],
                      _pack_flat(sm_v, 128)[None], "adam_small")
    sm_res = [_unpack_flat(o, sm_shapes) for o in sm_out]

    def ordered(k):
        s, b = sm_res[k], big_res[k]
        return [s[0], mod_res[k], s[1], s[2], b[0], b[1], b[2], b[3], s[3], s[4], s[5], s[6], b[4], b[5]]

    return (loss_tot[0], grad_x, *ordered(0), *ordered(1), *ordered(2), *ordered(3))
```

```python
import functools
import math

import jax
import jax.numpy as jnp
from jax import lax
from jax.experimental import pallas as pl
from jax.experimental.pallas import tpu as pltpu

F32 = jnp.float32
BF16 = jnp.bfloat16
EPS = 1e-6
N_DEV = 8
MESH = pl.DeviceIdType.MESH

HEAD_DIM = 64
ATTN_HEADS = 8
ATTN_KV = 2
ATTN_BLOCK = 128
WINDOW = 128
GRID_W = 64
HG_HEADS = 4
HG_D = 128
HG_CHUNK = 64
RET_HEADS = 4
RET_DK = 256
RET_DV = 512
RET_CHUNK = 128
NEG = -1e30

ADAM_LR = 0.001
ADAM_B1 = 0.9
ADAM_B2 = 0.999
ADAM_EPS = 1e-08
ADAM_WD = 0.01
ADAM_STEP = 10

VMEM_LIMIT = 60 * 1024 * 1024


def _cp(*sem):
    return pltpu.CompilerParams(dimension_semantics=sem, vmem_limit_bytes=VMEM_LIMIT)


def _nn(a, b):
    return jnp.dot(a, b, preferred_element_type=F32)


def _nt(a, b):
    return lax.dot_general(a, b, (((1,), (1,)), ((), ())), preferred_element_type=F32)


def _tn(a, b):
    return lax.dot_general(a, b, (((0,), (0,)), ((), ())), preferred_element_type=F32)


ACT = BF16


def _bf(a):
    return a.astype(ACT)


def _sig(x):
    return jax.nn.sigmoid(x)


def _split3(x):
    h = x.astype(BF16)
    r = x - h.astype(F32)
    m = r.astype(BF16)
    lo = (r - m.astype(F32)).astype(BF16)
    return h, m, lo


def _nn3(m01, x):
    h, m, lo = _split3(x)
    return _nn(m01, h) + _nn(m01, m) + _nn(m01, lo)


def _nn3r(x, m01):
    h, m, lo = _split3(x)
    return _nn(h, m01) + _nn(m, m01) + _nn(lo, m01)


def _full(shape):
    nd = len(shape)
    return pl.BlockSpec(shape, lambda *a: (0,) * nd, pipeline_mode=pl.Buffered(1))


def _whole(shape):
    nd = len(shape)
    return pl.BlockSpec(shape, lambda *a: (0,) * nd)


def _rows(tm, width):
    return pl.BlockSpec((tm, width), lambda i: (i, 0))


def _cols(height, tm):
    return pl.BlockSpec((height, tm), lambda i: (0, i))


def _ctx_lat(width):
    return pl.BlockSpec((1, 1, width), lambda i: (jnp.minimum(i, 1), 0, 0))


def _acc_ctx_lat(ref, i, val):
    @pl.when(i <= 1)
    def _():
        ref[...] = val.reshape(ref.shape)

    @pl.when(i > 1)
    def _():
        ref[...] += val.reshape(ref.shape)


def _acc_all(ref, i, val):
    @pl.when(i == 0)
    def _():
        ref[...] = val.reshape(ref.shape)

    @pl.when(i > 0)
    def _():
        ref[...] += val.reshape(ref.shape)


def _tile(n, cap):
    best = None
    for t in range(128, min(n, cap) + 1, 128):
        if n % t == 0:
            best = t
    return n if best is None else best


def _norm_mod(xv, g, shift, scale):
    r = lax.rsqrt(jnp.mean(xv * xv, axis=-1, keepdims=True) + EPS)
    xhat = xv * r
    n = xhat * g
    return r, xhat, n, n * (1.0 + scale) + shift


def _norm_mod_bwd(dh, r, xhat, n, g, scale):
    dshift = jnp.sum(dh, axis=0, keepdims=True)
    dscale = jnp.sum(dh * n, axis=0, keepdims=True)
    dn = dh * (1.0 + scale)
    dg = jnp.sum(dn * xhat, axis=0, keepdims=True)
    dxh = dn * g
    dx = r * (dxh - xhat * jnp.mean(dxh * xhat, axis=-1, keepdims=True))
    return dx, dshift, dscale, dg


def _pre_fwd(x, gain, ms, w, splits, tm, name, ex=None, out_dtype=F32):
    T, dm = x.shape
    nt = T // tm

    def body(x_ref, g_ref, ms_ref, w_ref, *outs):
        ms_v = ms_ref[0]
        h = _norm_mod(x_ref[...], g_ref[...], ms_v[:, :dm], ms_v[:, dm:])[3]
        hb = _bf(h)
        for (s, e), o_ref in zip(splits, outs):
            o_ref[...] = _nn(hb, w_ref[:, s:e]).astype(o_ref.dtype)

    return _host_call(
        body, ex, lambda: pl.program_id(0) == 0, lambda: pl.program_id(0) == nt - 1,
        name=name, grid=(nt,),
        in_specs=[_rows(tm, dm), _full((1, dm)), _ctx_lat(2 * dm), _full(w.shape)],
        out_specs=[_rows(tm, e - s) for s, e in splits],
        out_shape=[jax.ShapeDtypeStruct((T, e - s), out_dtype) for s, e in splits],
        scratch_shapes=[], sem=("arbitrary",), args=(x, gain, ms, w))


def _pre_bwd(x, dx_in, gain, ms, w, pieces, tm, name, latent_dx=False, ex=None):
    T, dm = x.shape
    dx_spec = pl.BlockSpec((tm, dm), lambda i: (jnp.maximum(i - 1, 0), 0)) if latent_dx else _rows(tm, dm)
    dx_rows = T - tm if latent_dx else T
    n_out = w.shape[1]
    flat = [a for _, arrs in pieces for a in arrs]

    def body(x_ref, dxin_ref, g_ref, ms_ref, w_ref, *rest):
        p_refs = rest[:len(flat)]
        dx_ref, h_ref, dp_ref, dms_ref, dg_ref = rest[len(flat):]
        i = pl.program_id(0)
        ms_v = ms_ref[0]
        g = g_ref[...]
        scale = ms_v[:, dm:]
        r, xhat, n, h = _norm_mod(x_ref[...], g, ms_v[:, :dm], scale)
        h_ref[...] = _bf(h).T
        dh = jnp.zeros((tm, dm), F32)
        k = 0
        for s, arrs in pieces:
            v = p_refs[k][...].astype(F32)
            for j in range(1, len(arrs)):
                v = v + p_refs[k + j][...].astype(F32)
            k += len(arrs)
            vb = _bf(v)
            wd = vb.shape[1]
            dp_ref[:, s:s + wd] = vb
            dh = dh + _nt(vb, w_ref[:, s:s + wd])
        dx, dshift, dscale, dg = _norm_mod_bwd(dh, r, xhat, n, g, scale)
        dx_ref[...] = dxin_ref[...] + dx
        _acc_ctx_lat(dms_ref, i, jnp.concatenate([dshift, dscale], axis=1))
        _acc_all(dg_ref, i, dg)

    nt = T // tm
    return _host_call(
        body, ex, lambda: pl.program_id(0) == 0, lambda: pl.program_id(0) == nt - 1,
        name=name, grid=(nt,),
        in_specs=[_rows(tm, dm), _rows(tm, dm), _full((1, dm)), _ctx_lat(2 * dm), _full(w.shape)]
        + [_rows(tm, a.shape[1]) for a in flat],
        out_specs=[dx_spec, _cols(dm, tm), _rows(tm, n_out), _ctx_lat(2 * dm), _whole((1, dm))],
        out_shape=[jax.ShapeDtypeStruct((dx_rows, dm), F32), jax.ShapeDtypeStruct((dm, T), ACT),
                   jax.ShapeDtypeStruct((T, n_out), ACT), jax.ShapeDtypeStruct((2, 1, 2 * dm), F32),
                   jax.ShapeDtypeStruct((1, dm), F32)],
        scratch_shapes=[], sem=("arbitrary",), args=(x, dx_in, gain, ms, w, *flat))


def _ffn_fwd(x1, gain, ms, w_in, w_out, tm, name, target=None, ex=None):
    T, dm = x1.shape
    fh = w_out.shape[0]
    head = target is not None

    def body(*refs):
        if head:
            x_ref, g_ref, ms_ref, wi_ref, wo_ref, t_ref, x2_ref, u_ref, f_ref, loss_ref = refs
        else:
            x_ref, g_ref, ms_ref, wi_ref, wo_ref, x2_ref, u_ref, f_ref = refs
        ms_v = ms_ref[0]
        xv = x_ref[...]
        h = _norm_mod(xv, g_ref[...], ms_v[:, :dm], ms_v[:, dm:2 * dm])[3]
        u = _nn(_bf(h), wi_ref[...])
        u_ref[...] = _bf(u)
        gt = u[:, :fh]
        act = gt * _sig(gt) * u[:, fh:]
        f = _nn(_bf(act), wo_ref[...])
        f_ref[...] = _bf(f)
        x2 = xv + ms_v[:, 2 * dm:] * f
        if head:
            i = pl.program_id(0)
            e = x2 - t_ref[...]
            x2_ref[...] = jnp.where(i > 0, e * (1.0 / dm), 0.0)
            _acc_all(loss_ref, i, jnp.where(i > 0, jnp.sum(e * e) * (0.5 / dm), 0.0))
        else:
            x2_ref[...] = x2

    ins = [x1, gain, ms, w_in, w_out]
    in_specs = [_rows(tm, dm), _full((1, dm)), _ctx_lat(3 * dm), _full(w_in.shape), _full(w_out.shape)]
    out_specs = [_rows(tm, dm), _rows(tm, 2 * fh), _rows(tm, dm)]
    out_shape = [jax.ShapeDtypeStruct((T, dm), F32), jax.ShapeDtypeStruct((T, 2 * fh), ACT), jax.ShapeDtypeStruct((T, dm), ACT)]
    if head:
        ins.append(target)
        in_specs.append(pl.BlockSpec((tm, dm), lambda i: (jnp.maximum(i - 1, 0), 0)))
        out_specs.append(_whole((1, 1)))
        out_shape.append(jax.ShapeDtypeStruct((1, 1), F32))
    nt = T // tm
    return _host_call(
        body, ex, lambda: pl.program_id(0) == 0, lambda: pl.program_id(0) == nt - 1,
        name=name, grid=(nt,), in_specs=in_specs, out_specs=out_specs, out_shape=out_shape,
        scratch_shapes=[], sem=("arbitrary",), args=tuple(ins))


def _ffn_bwd(x1, dx2, u, f, gain, ms, w_in, w_out, tm, name, ex=None):
    T, dm = x1.shape
    fh = w_out.shape[0]

    def body(x_ref, dx2_ref, u_ref, f_ref, g_ref, ms_ref, wi_ref, wo_ref,
             dx1_ref, h_ref, du_ref, act_ref, df_ref, dms_ref, dg_ref):
        i = pl.program_id(0)
        ms_v = ms_ref[0]
        g = g_ref[...]
        scale = ms_v[:, dm:2 * dm]
        gate = ms_v[:, 2 * dm:]
        r, xhat, n, h = _norm_mod(x_ref[...], g, ms_v[:, :dm], scale)
        h_ref[...] = _bf(h).T
        dx2 = dx2_ref[...]
        dgate = jnp.sum(dx2 * f_ref[...].astype(F32), axis=0, keepdims=True)
        dfb = _bf(dx2 * gate)
        df_ref[...] = dfb
        da = _nt(dfb, wo_ref[...])
        uv = u_ref[...].astype(F32)
        gt = uv[:, :fh]
        up = uv[:, fh:]
        s = _sig(gt)
        sg = gt * s
        act_ref[...] = _bf(sg * up).T
        dgt = _bf(da * up * (s * (1.0 + gt * (1.0 - s))))
        dup = _bf(da * sg)
        du_ref[:, :fh] = dgt
        du_ref[:, fh:] = dup
        dh = _nt(dgt, wi_ref[:, :fh]) + _nt(dup, wi_ref[:, fh:])
        dx, dshift, dscale, dg = _norm_mod_bwd(dh, r, xhat, n, g, scale)
        dx1_ref[...] = dx2 + dx
        _acc_ctx_lat(dms_ref, i, jnp.concatenate([dshift, dscale, dgate], axis=1))
        _acc_all(dg_ref, i, dg)

    nt = T // tm
    return _host_call(
        body, ex, lambda: pl.program_id(0) == 0, lambda: pl.program_id(0) == nt - 1,
        name=name, grid=(nt,),
        in_specs=[_rows(tm, dm), _rows(tm, dm), _rows(tm, 2 * fh), _rows(tm, dm), _full((1, dm)), _ctx_lat(3 * dm),
                  _full(w_in.shape), _full(w_out.shape)],
        out_specs=[_rows(tm, dm), _cols(dm, tm), _rows(tm, 2 * fh), _cols(fh, tm), _rows(tm, dm),
                   _ctx_lat(3 * dm), _whole((1, dm))],
        out_shape=[jax.ShapeDtypeStruct((T, dm), F32), jax.ShapeDtypeStruct((dm, T), ACT),
                   jax.ShapeDtypeStruct((T, 2 * fh), ACT), jax.ShapeDtypeStruct((fh, T), ACT),
                   jax.ShapeDtypeStruct((T, dm), ACT), jax.ShapeDtypeStruct((2, 1, 3 * dm), F32),
                   jax.ShapeDtypeStruct((1, dm), F32)],
        scratch_shapes=[], sem=("arbitrary",), args=(x1, dx2, u, f, gain, ms, w_in, w_out))


def _wgrad(a_t, b, name):
    K, T = a_t.shape
    N = b.shape[1]
    tk, tn, tt = _tile(K, 1408), _tile(N, 1664), _tile(T, 1408)
    nt = T // tt

    def body(a_ref, b_ref, o_ref, acc_ref):
        t = pl.program_id(2)
        part = _nn(a_ref[...], b_ref[...])

        @pl.when(t == 0)
        def _():
            acc_ref[...] = part

        @pl.when(t > 0)
        def _():
            acc_ref[...] += part

        @pl.when(t == nt - 1)
        def _():
            o_ref[...] = acc_ref[...].astype(o_ref.dtype)

    return pl.pallas_call(
        body, name=name, grid=(K // tk, N // tn, nt),
        in_specs=[pl.BlockSpec((tk, tt), lambda i, j, t: (i, t)), pl.BlockSpec((tt, tn), lambda i, j, t: (t, j))],
        out_specs=pl.BlockSpec((tk, tn), lambda i, j, t: (i, j)),
        out_shape=jax.ShapeDtypeStruct((K, N), ACT),
        scratch_shapes=[pltpu.VMEM((tk, tn), F32)],
        compiler_params=_cp("parallel", "parallel", "arbitrary"),
    )(a_t, b)


def _post_fwd(x, o_fw, o_bw, g_src, g_blk, gain, a, w_out, ms, dvh, tm, name):
    T, dm = x.shape
    hv = o_fw.shape[1]
    aw = 0 if a is None else a.shape[1]
    has_gain = gain is not None

    def body(*refs):
        refs = list(refs)
        x_ref, of_ref, ob_ref, g_ref = refs[:4]
        k = 4
        gain_ref = a_ref = None
        if has_gain:
            gain_ref = refs[k]
            k += 1
        if aw:
            a_ref = refs[k]
            k += 1
        w_ref, ms_ref, x1_ref, z_ref = refs[k:k + 4]
        o = of_ref[...].astype(F32) + ob_ref[...].astype(F32)
        gr = g_ref[...].astype(F32)
        if aw:
            z_ref[:, :aw] = _bf(a_ref[...])
        for hd in range(hv // dvh):
            sl = slice(hd * dvh, (hd + 1) * dvh)
            oh = o[:, sl]
            gh = gr[:, sl]
            r = lax.rsqrt(jnp.mean(oh * oh, axis=-1, keepdims=True) + EPS)
            y = oh * r
            if has_gain:
                y = y * gain_ref[...]
            y = y * (gh * _sig(gh))
            z_ref[:, aw + hd * dvh:aw + (hd + 1) * dvh] = _bf(y)
        yp = _nn(z_ref[...], w_ref[...])
        x1_ref[...] = x_ref[...] + ms_ref[0] * yp

    ins = [x, o_fw, o_bw, g_src]
    specs = [_rows(tm, dm), _rows(tm, hv), _rows(tm, hv), pl.BlockSpec((tm, hv), lambda i: (i, g_blk))]
    if has_gain:
        ins.append(gain)
        specs.append(_full(gain.shape))
    if aw:
        ins.append(a)
        specs.append(_rows(tm, aw))
    ins += [w_out, ms]
    specs += [_full(w_out.shape), _ctx_lat(dm)]
    return pl.pallas_call(
        body, name=name, grid=(T // tm,), in_specs=specs,
        out_specs=[_rows(tm, dm), _rows(tm, aw + hv)],
        out_shape=[jax.ShapeDtypeStruct((T, dm), F32), jax.ShapeDtypeStruct((T, aw + hv), ACT)],
        compiler_params=_cp("arbitrary"),
    )(*ins)


def _post_bwd(dx1, z, o_fw, o_bw, g_src, g_blk, gain, w_out, ms, aw, dvh, tm, name):
    T, dm = dx1.shape
    hv = o_fw.shape[1]
    has_gain = gain is not None

    def body(*refs):
        refs = list(refs)
        dx1_ref, z_ref, of_ref, ob_ref, g_ref = refs[:5]
        k = 5
        gain_ref = None
        if has_gain:
            gain_ref = refs[k]
            k += 1
        w_ref, ms_ref = refs[k:k + 2]
        k += 2
        do_ref, dgr_ref = refs[k:k + 2]
        k += 2
        da_ref = None
        if aw:
            da_ref = refs[k]
            k += 1
        dy_ref, zt_ref, dgate_ref, dgain_ref = refs[k:k + 4]
        i = pl.program_id(0)
        dx1v = dx1_ref[...]
        zb = z_ref[...]
        zt_ref[...] = zb.T
        yp = _nn(zb, w_ref[...])
        _acc_ctx_lat(dgate_ref, i, jnp.sum(dx1v * yp, axis=0, keepdims=True))
        dyb = _bf(dx1v * ms_ref[0])
        dy_ref[...] = dyb
        dz = _nt(dyb, w_ref[...])
        if aw:
            da_ref[...] = dz[:, :aw]
        o = of_ref[...].astype(F32) + ob_ref[...].astype(F32)
        gr = g_ref[...].astype(F32)
        dgain = jnp.zeros((1, dvh), F32)
        for hd in range(hv // dvh):
            sl = slice(hd * dvh, (hd + 1) * dvh)
            oh = o[:, sl]
            gh = gr[:, sl]
            dyh = dz[:, aw + hd * dvh:aw + (hd + 1) * dvh]
            r = lax.rsqrt(jnp.mean(oh * oh, axis=-1, keepdims=True) + EPS)
            n = oh * r
            s = _sig(gh)
            sl_g = gh * s
            gn = gain_ref[...] if has_gain else 1.0
            dgr_ref[:, sl] = _bf(dyh * n * gn * (s * (1.0 + gh * (1.0 - s))))
            dn = dyh * gn * sl_g
            dgain = dgain + jnp.sum(dyh * n * sl_g, axis=0, keepdims=True)
            do_ref[:, sl] = _bf(r * (dn - n * jnp.mean(dn * n, axis=-1, keepdims=True)))
        _acc_all(dgain_ref, i, dgain)

    ins = [dx1, z, o_fw, o_bw, g_src]
    specs = [_rows(tm, dm), _rows(tm, aw + hv), _rows(tm, hv), _rows(tm, hv),
             pl.BlockSpec((tm, hv), lambda i: (i, g_blk))]
    if has_gain:
        ins.append(gain)
        specs.append(_full(gain.shape))
    ins += [w_out, ms]
    specs += [_full(w_out.shape), _ctx_lat(dm)]
    out_specs = [_rows(tm, hv), _rows(tm, hv)]
    out_shape = [jax.ShapeDtypeStruct((T, hv), ACT), jax.ShapeDtypeStruct((T, hv), ACT)]
    if aw:
        out_specs.append(_rows(tm, aw))
        out_shape.append(jax.ShapeDtypeStruct((T, aw), F32))
    out_specs += [_rows(tm, dm), _cols(aw + hv, tm), _ctx_lat(dm), _whole((1, dvh))]
    out_shape += [jax.ShapeDtypeStruct((T, dm), ACT), jax.ShapeDtypeStruct((aw + hv, T), ACT),
                  jax.ShapeDtypeStruct((2, 1, dm), F32), jax.ShapeDtypeStruct((1, dvh), F32)]
    return pl.pallas_call(
        body, name=name, grid=(T // tm,), in_specs=specs, out_specs=out_specs, out_shape=out_shape,
        compiler_params=_cp("arbitrary"),
    )(*ins)


def _loss_bwd(x, target, tm, name):
    T, dm = x.shape

    def body(x_ref, t_ref, dx_ref, loss_ref):
        i = pl.program_id(0)

        @pl.when(i == 0)
        def _():
            dx_ref[...] = jnp.zeros_like(dx_ref)
            loss_ref[...] = jnp.zeros_like(loss_ref)

        @pl.when(i > 0)
        def _():
            e = x_ref[...] - t_ref[...]
            dx_ref[...] = e * (1.0 / dm)
            loss_ref[...] += jnp.sum(e * e) * (0.5 / dm)

    return pl.pallas_call(
        body, name=name, grid=(T // tm,),
        in_specs=[_rows(tm, dm), pl.BlockSpec((tm, dm), lambda i: (jnp.maximum(i - 1, 0), 0))],
        out_specs=[_rows(tm, dm), _whole((1, 1))],
        out_shape=[jax.ShapeDtypeStruct((T, dm), F32), jax.ShapeDtypeStruct((1, 1), F32)],
        compiler_params=_cp("arbitrary"),
    )(x, target)


def _swap_matrix():
    r = lax.broadcasted_iota(jnp.int32, (HEAD_DIM, HEAD_DIM), 0)
    c = lax.broadcasted_iota(jnp.int32, (HEAD_DIM, HEAD_DIM), 1)
    return jnp.where((r + HEAD_DIM // 2) % HEAD_DIM == c, 1.0, 0.0).astype(BF16)


def _qk_prep_fwd(raw, gains, cos2, sin2, tq, name):
    nh, T, hd = raw.shape

    def body(x_ref, g_ref, c_ref, s_ref, o_ref):
        hidx = pl.program_id(0)
        xv = x_ref[0]
        r = lax.rsqrt(jnp.mean(xv * xv, axis=-1, keepdims=True) + EPS)
        n = xv * r * g_ref[0]
        y = n * c_ref[...] + _nn3r(n, _swap_matrix()) * s_ref[...]
        sc = jnp.where(hidx < ATTN_HEADS, HEAD_DIM ** -0.5, 1.0)
        o_ref[0] = _bf(y * sc)

    return pl.pallas_call(
        body, name=name, grid=(nh, T // tq),
        in_specs=[pl.BlockSpec((1, tq, hd), lambda h, i: (h, i, 0)), pl.BlockSpec((1, 1, hd), lambda h, i: (h, 0, 0)),
                  pl.BlockSpec((tq, hd), lambda h, i: (i, 0)), pl.BlockSpec((tq, hd), lambda h, i: (i, 0))],
        out_specs=pl.BlockSpec((1, tq, hd), lambda h, i: (h, i, 0)),
        out_shape=jax.ShapeDtypeStruct((nh, T, hd), ACT),
        compiler_params=_cp("arbitrary", "arbitrary"),
    )(raw, gains, cos2, sin2)


def _qk_prep_bwd(dy, raw, gains, cos2, sin2, tq, name):
    nh, T, hd = raw.shape

    def body(dy_ref, x_ref, g_ref, c_ref, s_ref, dx_ref, dg_ref):
        hidx = pl.program_id(0)
        i = pl.program_id(1)
        xv = x_ref[0]
        g = g_ref[0]
        r = lax.rsqrt(jnp.mean(xv * xv, axis=-1, keepdims=True) + EPS)
        xhat = xv * r
        sc = jnp.where(hidx < ATTN_HEADS, HEAD_DIM ** -0.5, 1.0)
        dyv = dy_ref[0] * sc
        dn = dyv * c_ref[...] + _nn3r(dyv * s_ref[...], _swap_matrix())
        _acc_all(dg_ref, i, jnp.sum(dn * xhat, axis=0, keepdims=True))
        dxh = dn * g
        dx_ref[0] = r * (dxh - xhat * jnp.mean(dxh * xhat, axis=-1, keepdims=True))

    return pl.pallas_call(
        body, name=name, grid=(nh, T // tq),
        in_specs=[pl.BlockSpec((1, tq, hd), lambda h, i: (h, i, 0)), pl.BlockSpec((1, tq, hd), lambda h, i: (h, i, 0)),
                  pl.BlockSpec((1, 1, hd), lambda h, i: (h, 0, 0)),
                  pl.BlockSpec((tq, hd), lambda h, i: (i, 0)), pl.BlockSpec((tq, hd), lambda h, i: (i, 0))],
        out_specs=[pl.BlockSpec((1, tq, hd), lambda h, i: (h, i, 0)), pl.BlockSpec((1, 1, hd), lambda h, i: (h, 0, 0))],
        out_shape=[jax.ShapeDtypeStruct((nh, T, hd), F32), jax.ShapeDtypeStruct((nh, 1, hd), F32)],
        compiler_params=_cp("arbitrary", "arbitrary"),
    )(dy, raw, gains, cos2, sin2)


def _attn_scores(q, k_ref, i, lc, T, sink):
    blk = ATTN_BLOCK
    kc = k_ref[0, pl.ds(blk, lc), :]
    kw = k_ref[0, pl.ds(pl.multiple_of(i * blk, blk), 3 * blk), :]
    s_c = _nt(q, kc)
    s_w = _nt(q, kw)
    row = lax.broadcasted_iota(jnp.int32, (4 * blk, 1), 0)
    qpos = i * blk + (row & (blk - 1))
    kpos = (i - 1) * blk + lax.broadcasted_iota(jnp.int32, (1, 3 * blk), 1)
    valid = (qpos >= lc) & (kpos >= lc) & (kpos < T) & (jnp.abs(kpos - qpos) <= WINDOW)
    s_w = jnp.where(valid, s_w, NEG)
    return kc, kw, s_c, s_w


def _attn_fwd(qt, kp, vp, sinkb, lc, name, ex=None):
    nh, T, hd = qt.shape
    blk = ATTN_BLOCK
    g = nh // ATTN_KV

    def body(q_ref, k_ref, v_ref, sink_ref, o_ref, lse_ref):
        i = pl.program_id(1)
        q = q_ref[...].reshape(g * blk, hd)
        sink = sink_ref[0]
        kc, kw, s_c, s_w = _attn_scores(q, k_ref, i, lc, T, sink)
        m = jnp.maximum(jnp.maximum(jnp.max(s_c, axis=-1, keepdims=True), jnp.max(s_w, axis=-1, keepdims=True)), sink)
        e_c = jnp.exp(s_c - m)
        e_w = jnp.exp(s_w - m)
        den = jnp.exp(sink - m) + jnp.sum(e_c, axis=-1, keepdims=True) + jnp.sum(e_w, axis=-1, keepdims=True)
        inv = 1.0 / den
        vc = v_ref[0, pl.ds(blk, lc), :]
        vw = v_ref[0, pl.ds(pl.multiple_of(i * blk, blk), 3 * blk), :]
        o = _nn(_bf(e_c * inv), vc) + _nn(_bf(e_w * inv), vw)
        o_ref[...] = o.reshape(g, blk, hd)
        lse_ref[...] = (m + jnp.log(den)).reshape(g, blk, 1)

    nb = T // blk
    return _host_call(
        body, ex, lambda: (pl.program_id(0) == 0) & (pl.program_id(1) == 0),
        lambda: (pl.program_id(0) == ATTN_KV - 1) & (pl.program_id(1) == nb - 1),
        name=name, grid=(ATTN_KV, nb),
        in_specs=[pl.BlockSpec((g, blk, hd), lambda kv, i: (kv, i, 0)),
                  pl.BlockSpec((1, T + 2 * blk, hd), lambda kv, i: (kv, 0, 0)),
                  pl.BlockSpec((1, T + 2 * blk, hd), lambda kv, i: (kv, 0, 0)),
                  pl.BlockSpec((1, g * blk, 1), lambda kv, i: (kv, 0, 0))],
        out_specs=[pl.BlockSpec((g, blk, hd), lambda kv, i: (kv, i, 0)),
                   pl.BlockSpec((g, blk, 1), lambda kv, i: (kv, i, 0))],
        out_shape=[jax.ShapeDtypeStruct((nh, T, hd), F32), jax.ShapeDtypeStruct((nh, T, 1), F32)],
        scratch_shapes=[], sem=("arbitrary", "arbitrary"), args=(qt, kp, vp, sinkb))


def _attn_bwd(qt, kp, vp, sinkb, o, lse, do, lc, name):
    nh, T, hd = qt.shape
    blk = ATTN_BLOCK
    g = nh // ATTN_KV

    def body(q_ref, k_ref, v_ref, sink_ref, o_ref, lse_ref, do_ref, dq_ref, dk_ref, dv_ref, ds_ref):
        i = pl.program_id(1)

        @pl.when(i == 0)
        def _():
            dk_ref[...] = jnp.zeros_like(dk_ref)
            dv_ref[...] = jnp.zeros_like(dv_ref)
            ds_ref[...] = jnp.zeros_like(ds_ref)

        q = q_ref[...].reshape(g * blk, hd)
        sink = sink_ref[0]
        lse = lse_ref[...].reshape(g * blk, 1)
        dov = do_ref[...].reshape(g * blk, hd)
        delta = jnp.sum(dov * o_ref[...].reshape(g * blk, hd), axis=-1, keepdims=True)
        kc, kw, s_c, s_w = _attn_scores(q, k_ref, i, lc, T, sink)
        p_c = jnp.exp(s_c - lse)
        p_w = jnp.exp(s_w - lse)
        win = pl.ds(pl.multiple_of(i * blk, blk), 3 * blk)
        vc = v_ref[0, pl.ds(blk, lc), :]
        vw = v_ref[0, win, :]
        dob = _bf(dov)
        ds_c = _bf(p_c * (_nt(dob, vc) - delta))
        ds_w = _bf(p_w * (_nt(dob, vw) - delta))
        dsr = -jnp.exp(sink - lse) * delta
        for hh in range(g):
            ds_ref[0, hh:hh + 1, :] += jnp.sum(dsr[hh * blk:(hh + 1) * blk, :], axis=0, keepdims=True)
        dq_ref[...] = (_nn(ds_c, kc) + _nn(ds_w, kw)).reshape(g, blk, hd)
        dk_ref[0, pl.ds(blk, lc), :] += _tn(ds_c, q)
        dk_ref[0, win, :] += _tn(ds_w, q)
        dv_ref[0, pl.ds(blk, lc), :] += _tn(_bf(p_c), dob)
        dv_ref[0, win, :] += _tn(_bf(p_w), dob)

    qspec = pl.BlockSpec((g, blk, hd), lambda kv, i: (kv, i, 0))
    kspec = pl.BlockSpec((1, T + 2 * blk, hd), lambda kv, i: (kv, 0, 0))
    lspec = pl.BlockSpec((g, blk, 1), lambda kv, i: (kv, i, 0))
    return pl.pallas_call(
        body, name=name, grid=(ATTN_KV, T // blk),
        in_specs=[qspec, kspec, kspec, pl.BlockSpec((1, g * blk, 1), lambda kv, i: (kv, 0, 0)), qspec, lspec, qspec],
        out_specs=[qspec, kspec, kspec, pl.BlockSpec((1, g, 1), lambda kv, i: (kv, 0, 0))],
        out_shape=[jax.ShapeDtypeStruct((nh, T, hd), F32), jax.ShapeDtypeStruct((ATTN_KV, T + 2 * blk, hd), F32),
                   jax.ShapeDtypeStruct((ATTN_KV, T + 2 * blk, hd), F32), jax.ShapeDtypeStruct((ATTN_KV, g, 1), F32)],
        compiler_params=_cp("arbitrary", "arbitrary"),
    )(qt, kp, vp, sinkb, o, lse, do)


PAIR = 2 * HEAD_DIM
N_PAIRS = (ATTN_HEADS + ATTN_KV) // 2


def _lanes():
    return lax.broadcasted_iota(jnp.int32, (1, PAIR), 1)


def _swap32(v):
    first_half = (_lanes() & (HEAD_DIM // 2)) == 0
    return jnp.where(first_half, pltpu.roll(v, PAIR - HEAD_DIM // 2, 1), pltpu.roll(v, HEAD_DIM // 2, 1))


def _head_mean(v):
    r = lax.broadcasted_iota(jnp.int32, (PAIR, PAIR), 0)
    c = lax.broadcasted_iota(jnp.int32, (PAIR, PAIR), 1)
    same = jnp.where((r >= HEAD_DIM) == (c >= HEAD_DIM), 1.0, 0.0).astype(BF16)
    return _nn3r(v, same) * (1.0 / HEAD_DIM)


def _qk_slab_fwd(pa, gains, cosp, sinp, tm, name):
    T = pa.shape[0]
    qw = ATTN_HEADS * HEAD_DIM

    def body(pa_ref, g_ref, c_ref, s_ref, q_ref, k_ref, v_ref):
        cosv, sinv = c_ref[...], s_ref[...]
        for p in range(N_PAIRS):
            xv = pa_ref[:, p * PAIR:(p + 1) * PAIR]
            n = xv * lax.rsqrt(_head_mean(xv * xv) + EPS) * g_ref[p]
            y = n * cosv + _swap32(n) * sinv
            if p < N_PAIRS - 1:
                q_ref[:, p * PAIR:(p + 1) * PAIR] = _bf(y * HEAD_DIM ** -0.5)
            else:
                k_ref[...] = _bf(y)
        v_ref[...] = _bf(pa_ref[:, qw + PAIR:])

    return pl.pallas_call(
        body, name=name, grid=(T // tm,),
        in_specs=[_rows(tm, pa.shape[1]), _full(gains.shape), _rows(tm, PAIR), _rows(tm, PAIR)],
        out_specs=[_rows(tm, qw), _rows(tm, PAIR), _rows(tm, PAIR)],
        out_shape=[jax.ShapeDtypeStruct((T, qw), ACT), jax.ShapeDtypeStruct((T, PAIR), ACT),
                   jax.ShapeDtypeStruct((T, PAIR), ACT)],
        compiler_params=_cp("arbitrary"),
    )(pa, gains, cosp, sinp)


def _qk_slab_bwd(dq, dk, pa, gains, cosp, sinp, tm, name):
    T = pa.shape[0]
    qw = ATTN_HEADS * HEAD_DIM

    def body(dq_ref, dk_ref, pa_ref, g_ref, c_ref, s_ref, dx_ref, dg_ref):
        i = pl.program_id(0)
        cosv, sinv = c_ref[...], s_ref[...]
        for p in range(N_PAIRS):
            sl = slice(p * PAIR, (p + 1) * PAIR)
            xv = pa_ref[:, sl]
            r = lax.rsqrt(_head_mean(xv * xv) + EPS)
            xhat = xv * r
            dy = dq_ref[:, sl] * HEAD_DIM ** -0.5 if p < N_PAIRS - 1 else dk_ref[...]
            dn = dy * cosv + _swap32(dy * sinv)
            _acc_all(dg_ref.at[p], i, jnp.sum(dn * xhat, axis=0, keepdims=True))
            dxh = dn * g_ref[p]
            dx_ref[:, sl] = r * (dxh - xhat * _head_mean(dxh * xhat))

    return pl.pallas_call(
        body, name=name, grid=(T // tm,),
        in_specs=[_rows(tm, qw), _rows(tm, PAIR), _rows(tm, qw + PAIR), _full(gains.shape), _rows(tm, PAIR), _rows(tm, PAIR)],
        out_specs=[_rows(tm, qw + PAIR), _whole(gains.shape)],
        out_shape=[jax.ShapeDtypeStruct((T, qw + PAIR), F32), jax.ShapeDtypeStruct(gains.shape, F32)],
        compiler_params=_cp("arbitrary"),
    )(dq, dk, pa, gains, cosp, sinp)


def _attn_window(ref, i, nb):
    blk = ATTN_BLOCK
    starts = [pl.multiple_of(jnp.clip(i + d, 0, nb - 1) * blk, blk) for d in (-1, 0, 1)]
    return starts, jnp.concatenate([ref[pl.ds(s, blk), :] for s in starts], axis=0)


def _attn_mask(i, lc, T):
    blk = ATTN_BLOCK
    row = lax.broadcasted_iota(jnp.int32, (4 * blk, 1), 0)
    qpos = i * blk + (row & (blk - 1))
    kpos = (i - 1) * blk + lax.broadcasted_iota(jnp.int32, (1, 3 * blk), 1)
    return (qpos >= lc) & (kpos >= lc) & (kpos < T) & (jnp.abs(kpos - qpos) <= WINDOW)


def _to_kv_half(v, head, kv):
    return v if head % 2 == kv else pltpu.roll(v, HEAD_DIM, 1)


def _attn_slab_fwd(qt, ks, vs, sinkb, lc, name, ex=None):
    T = qt.shape[0]
    blk = ATTN_BLOCK
    nb = T // blk
    g = ATTN_HEADS // ATTN_KV

    def body(q_ref, k_ref, v_ref, sink_ref, o_ref, lse_ref):
        i = pl.program_id(0)
        lane = _lanes()
        valid = _attn_mask(i, lc, T)
        kc_all, vc = k_ref[0:lc, :], v_ref[0:lc, :]
        _, kw_all = _attn_window(k_ref, i, nb)
        _, vw = _attn_window(v_ref, i, nb)
        placed = [None] * ATTN_HEADS
        for kv in range(ATTN_KV):
            mine = (lane >= kv * HEAD_DIM) & (lane < (kv + 1) * HEAD_DIM)
            kc = jnp.where(mine, kc_all, jnp.zeros_like(kc_all))
            kw = jnp.where(mine, kw_all, jnp.zeros_like(kw_all))
            heads = [kv * g + j for j in range(g)]
            q4 = jnp.concatenate([_to_kv_half(q_ref[:, (h // 2) * PAIR:(h // 2 + 1) * PAIR], h, kv) for h in heads], axis=0)
            sink = sink_ref[kv]
            s_c = _nt(q4, kc)
            s_w = jnp.where(valid, _nt(q4, kw), NEG)
            m = jnp.maximum(jnp.maximum(jnp.max(s_c, axis=-1, keepdims=True), jnp.max(s_w, axis=-1, keepdims=True)), sink)
            e_c = jnp.exp(s_c - m)
            e_w = jnp.exp(s_w - m)
            den = jnp.exp(sink - m) + jnp.sum(e_c, axis=-1, keepdims=True) + jnp.sum(e_w, axis=-1, keepdims=True)
            inv = 1.0 / den
            o4 = _nn(_bf(e_c * inv), vc) + _nn(_bf(e_w * inv), vw)
            lse_ref[kv * g:(kv + 1) * g] = (m + jnp.log(den)).reshape(g, blk, 1)
            for j, h in enumerate(heads):
                placed[h] = _to_kv_half(o4[j * blk:(j + 1) * blk], h, kv)
        for p in range(ATTN_HEADS // 2):
            o_ref[:, p * PAIR:(p + 1) * PAIR] = jnp.where(lane < HEAD_DIM, placed[2 * p], placed[2 * p + 1])

    qw = ATTN_HEADS * HEAD_DIM
    return _host_call(
        body, ex, lambda: pl.program_id(0) == 0, lambda: pl.program_id(0) == nb - 1,
        name=name, grid=(nb,),
        in_specs=[_rows(blk, qw), _full((T, PAIR)), _full((T, PAIR)), _full(sinkb.shape)],
        out_specs=[_rows(blk, qw), pl.BlockSpec((ATTN_HEADS, blk, 1), lambda i: (0, i, 0))],
        out_shape=[jax.ShapeDtypeStruct((T, qw), F32), jax.ShapeDtypeStruct((ATTN_HEADS, T, 1), F32)],
        scratch_shapes=[], sem=("arbitrary",), args=(qt, ks, vs, sinkb))


def _attn_slab_bwd(qt, ks, vs, sinkb, o, lse, do, lc, name, ex=None):
    T = qt.shape[0]
    blk = ATTN_BLOCK
    nb = T // blk
    g = ATTN_HEADS // ATTN_KV

    def body(q_ref, k_ref, v_ref, sink_ref, o_ref, lse_ref, do_ref, dq_ref, dk_ref, dv_ref, ds_ref):
        i = pl.program_id(0)

        @pl.when(i == 0)
        def _():
            dk_ref[...] = jnp.zeros_like(dk_ref)
            dv_ref[...] = jnp.zeros_like(dv_ref)
            ds_ref[...] = jnp.zeros_like(ds_ref)

        lane = _lanes()
        valid = _attn_mask(i, lc, T)
        kc_all, vc_all = k_ref[0:lc, :], v_ref[0:lc, :]
        starts, kw_all = _attn_window(k_ref, i, nb)
        _, vw_all = _attn_window(v_ref, i, nb)
        dq_pairs = [jnp.zeros((blk, PAIR), F32) for _ in range(ATTN_HEADS // 2)]
        for kv in range(ATTN_KV):
            mine = (lane >= kv * HEAD_DIM) & (lane < (kv + 1) * HEAD_DIM)

            def only(v):
                return jnp.where(mine, v, jnp.zeros_like(v))

            kc, kw, vc, vw = only(kc_all), only(kw_all), only(vc_all), only(vw_all)
            heads = [kv * g + j for j in range(g)]
            qs, dos, deltas = [], [], []
            for h in heads:
                sl = slice((h // 2) * PAIR, (h // 2 + 1) * PAIR)
                dov = do_ref[:, sl]
                qs.append(_to_kv_half(q_ref[:, sl], h, kv))
                dos.append(_bf(_to_kv_half(dov, h, kv)))
                own = (lane < HEAD_DIM) if h % 2 == 0 else (lane >= HEAD_DIM)
                deltas.append(jnp.sum(jnp.where(own, dov * o_ref[:, sl], 0.0), axis=-1, keepdims=True))
            q4, do4, delta = jnp.concatenate(qs, axis=0), jnp.concatenate(dos, axis=0), jnp.concatenate(deltas, axis=0)
            sink = sink_ref[kv]
            lse = lse_ref[kv * g:(kv + 1) * g].reshape(g * blk, 1)
            p_c = jnp.exp(_nt(q4, kc) - lse)
            p_w = jnp.exp(jnp.where(valid, _nt(q4, kw), NEG) - lse)
            ds_c = _bf(p_c * (_nt(do4, vc) - delta))
            ds_w = _bf(p_w * (_nt(do4, vw) - delta))
            dsr = -jnp.exp(sink - lse) * delta
            dq4 = _nn(ds_c, kc) + _nn(ds_w, kw)
            for j, h in enumerate(heads):
                ds_ref[h:h + 1, :] += jnp.sum(dsr[j * blk:(j + 1) * blk, :], axis=0, keepdims=True)
                dq_pairs[h // 2] = dq_pairs[h // 2] + _to_kv_half(dq4[j * blk:(j + 1) * blk], h, kv)
            dk_ref[0:lc, :] += only(_tn(ds_c, q4))
            dv_ref[0:lc, :] += only(_tn(_bf(p_c), do4))
            dkw = only(_tn(ds_w, q4))
            dvw = only(_tn(_bf(p_w), do4))
            for b, s in enumerate(starts):
                dk_ref[pl.ds(s, blk), :] += dkw[b * blk:(b + 1) * blk]
                dv_ref[pl.ds(s, blk), :] += dvw[b * blk:(b + 1) * blk]
        for p in range(ATTN_HEADS // 2):
            dq_ref[:, p * PAIR:(p + 1) * PAIR] = dq_pairs[p]

    qw = ATTN_HEADS * HEAD_DIM
    lspec = pl.BlockSpec((ATTN_HEADS, blk, 1), lambda i: (0, i, 0))
    return _host_call(
        body, ex, lambda: pl.program_id(0) == 0, lambda: pl.program_id(0) == nb - 1,
        name=name, grid=(nb,),
        in_specs=[_rows(blk, qw), _full((T, PAIR)), _full((T, PAIR)), _full(sinkb.shape), _rows(blk, qw), lspec,
                  _rows(blk, qw)],
        out_specs=[_rows(blk, qw), _whole((T, PAIR)), _whole((T, PAIR)), _whole((ATTN_HEADS, 1))],
        out_shape=[jax.ShapeDtypeStruct((T, qw), F32), jax.ShapeDtypeStruct((T, PAIR), F32),
                   jax.ShapeDtypeStruct((T, PAIR), F32), jax.ShapeDtypeStruct((ATTN_HEADS, 1), F32)],
        scratch_shapes=[], sem=("arbitrary",), args=(qt, ks, vs, sinkb, o, lse, do))


def _fw_chunk(s, nc, nt):
    return s


def _bw_chunk(s, nc, nt):
    return jnp.where(s < nc, nc - 1 - s, nt - 1 - (s - nc))


def _tri(c, rev):
    r = lax.broadcasted_iota(jnp.int32, (c, c), 0)
    k = lax.broadcasted_iota(jnp.int32, (c, c), 1)
    return (k >= r) if rev else (k <= r)


def _gla_gates(z, lb, rev):
    c = HG_CHUNK
    sg = _sig(z)
    f = lb + (1.0 - lb) * sg
    cum = _nn3(jnp.where(_tri(c, rev), 1.0, 0.0).astype(BF16), jnp.log(f))
    mid = c - 1 - c // 2 if rev else c // 2
    last = 0 if rev else c - 1
    return sg, f, cum, cum[mid:mid + 1], cum[last:last + 1], last


def _lower_bound(lbraw_ref):
    lr = lbraw_ref[...]
    return _sig(lr[0:1] - lr[1:2])


def _gla_fwd(pb, lbraw, lc, name, ex=None):
    T = pb.shape[0]
    c, hw, d = HG_CHUNK, HG_HEADS * HG_D, HG_D
    nt, nc = T // c, lc // c
    orders = (_fw_chunk, _bw_chunk)

    def body(qf, zf, vf, qb, zb, vb, lb_ref, of_ref, ob_ref, sf_ref, sb_ref, st_ref):
        @pl.when(pl.program_id(0) == 0)
        def _():
            st_ref[...] = jnp.zeros_like(st_ref)

        lb = _lower_bound(lb_ref)
        dirs = ((qf, zf, vf, of_ref, sf_ref), (qb, zb, vb, ob_ref, sb_ref))
        combos = [(dr, h, slice(h * d, (h + 1) * d)) for dr in range(2) for h in range(HG_HEADS)]
        prep = []
        for dr, (q_ref, z_ref, v_ref, _, _) in enumerate(dirs):
            rev = dr == 1
            qr = q_ref[...]
            q = qr * _sig(qr)
            _, f, cum, ref, last, _ = _gla_gates(z_ref[...], lb, rev)
            k = 1.0 - f
            prep.append(dict(q1=_bf(q * jnp.exp(cum - ref)), k1=_bf(k * jnp.exp(ref - cum)), q2=_bf(q * jnp.exp(cum)),
                             k2=_bf(k * jnp.exp(last - cum)), el=jnp.exp(last), v=_bf(v_ref[...]), mask=_tri(c, rev)))
        a = [_bf(jnp.where(prep[dr]["mask"], _nt(prep[dr]["q1"][:, sl], prep[dr]["k1"][:, sl]), 0.0)) for dr, _, sl in combos]
        for (dr, h, sl), a_h in zip(combos, a):
            p = prep[dr]
            o_ref, s_ref = dirs[dr][3], dirs[dr][4]
            st = st_ref[dr, h]
            stb = _bf(st)
            s_ref[0, h] = stb
            o_ref[:, sl] = _nn(a_h, p["v"][:, sl]) + _nt(p["q2"][:, sl], stb)
            st_ref[dr, h] = st * p["el"][:, sl] + _tn(p["v"][:, sl], p["k2"][:, sl])

    def col(order, blkcol):
        return pl.BlockSpec((c, hw), lambda s: (order(s, nc, nt), blkcol))

    def st_spec(order):
        return pl.BlockSpec((1, HG_HEADS, d, d), lambda s: (order(s, nc, nt), 0, 0, 0))

    in_specs = []
    for dr, order in enumerate(orders):
        in_specs += [col(order, 0), col(order, 1 + dr), col(order, 3)]
    in_specs.append(_full(lbraw.shape))
    return _host_call(
        body, ex, lambda: pl.program_id(0) == 0, lambda: pl.program_id(0) == nt - 1,
        name=name, grid=(nt,), in_specs=in_specs,
        out_specs=[col(_fw_chunk, 0), col(_bw_chunk, 0), st_spec(_fw_chunk), st_spec(_bw_chunk)],
        out_shape=[jax.ShapeDtypeStruct((T, hw), F32), jax.ShapeDtypeStruct((T, hw), F32),
                   jax.ShapeDtypeStruct((nt, HG_HEADS, d, d), ACT), jax.ShapeDtypeStruct((nt, HG_HEADS, d, d), ACT)],
        scratch_shapes=[pltpu.VMEM((2, HG_HEADS, d, d), F32)], sem=("arbitrary",),
        args=(pb, pb, pb, pb, pb, pb, lbraw))


def _gla_bwd(pb, lbraw, s_fw, s_bw, do, lc, name, ex=None):
    T = pb.shape[0]
    c, hw, d = HG_CHUNK, HG_HEADS * HG_D, HG_D
    nt, nc = T // c, lc // c

    def rfw(s, nc_, nt_):
        return _fw_chunk(nt_ - 1 - s, nc_, nt_)

    def rbw(s, nc_, nt_):
        return _bw_chunk(nt_ - 1 - s, nc_, nt_)

    def body(qf, zf, vf, sf, dof, qb, zb, vb, sb, dob_, lb_ref,
             dqf, dzf, dvf, dqb, dzb, dvb, dlb_ref, dst_ref):
        step = pl.program_id(0)

        @pl.when(step == 0)
        def _():
            dst_ref[...] = jnp.zeros_like(dst_ref)

        lb = _lower_bound(lb_ref)
        sets = ((qf, zf, vf, sf, dof, dqf, dzf, dvf), (qb, zb, vb, sb, dob_, dqb, dzb, dvb))
        combos = [(dr, h, slice(h * d, (h + 1) * d)) for dr in range(2) for h in range(HG_HEADS)]
        prep = []
        for dr, (q_ref, z_ref, v_ref, _, do_ref, _, _, _) in enumerate(sets):
            rev = dr == 1
            qr = q_ref[...]
            sq = _sig(qr)
            q = qr * sq
            sg, f, cum, ref, last, last_row = _gla_gates(z_ref[...], lb, rev)
            k = 1.0 - f
            e_qr, e_kr, e_q, e_kl = jnp.exp(cum - ref), jnp.exp(ref - cum), jnp.exp(cum), jnp.exp(last - cum)
            q1, k1, q2, k2 = q * e_qr, k * e_kr, q * e_q, k * e_kl
            prep.append(dict(qr=qr, sq=sq, sg=sg, f=f, e_qr=e_qr, e_kr=e_kr, e_q=e_q, e_kl=e_kl, el=jnp.exp(last),
                             q1=q1, k1=k1, q2=q2, k2=k2, q1b=_bf(q1), k1b=_bf(k1), q2b=_bf(q2), k2b=_bf(k2),
                             vb=_bf(v_ref[...]), dob=_bf(do_ref[...]), mask=_tri(c, rev), last_row=last_row,
                             acc_t=jnp.where(_tri(c, not rev), 1.0, 0.0).astype(BF16)))
        a = [_bf(jnp.where(prep[dr]["mask"], _nt(prep[dr]["q1b"][:, sl], prep[dr]["k1b"][:, sl]), 0.0)) for dr, _, sl in combos]
        da = [_bf(jnp.where(prep[dr]["mask"], _nt(prep[dr]["dob"][:, sl], prep[dr]["vb"][:, sl]), 0.0)) for dr, _, sl in combos]
        parts = [dict(dq1=[], dk1=[], dq2=[], dk2=[], dls=[]) for _ in range(2)]
        for (dr, h, sl), a_h, da_h in zip(combos, a, da):
            p = prep[dr]
            s_ref, dv_ref = sets[dr][3], sets[dr][7]
            stb = s_ref[0, h]
            dst = dst_ref[dr, h]
            dstb = _bf(dst)
            dob_h, vb_h = p["dob"][:, sl], p["vb"][:, sl]
            dv_ref[:, sl] = _bf(_tn(a_h, dob_h) + _nt(p["k2b"][:, sl], dstb))
            parts[dr]["dq1"].append(_nn(da_h, p["k1b"][:, sl]))
            parts[dr]["dk1"].append(_tn(da_h, p["q1b"][:, sl]))
            parts[dr]["dq2"].append(_nn(dob_h, stb))
            parts[dr]["dk2"].append(_nn(vb_h, dstb))
            el_h = p["el"][:, sl]
            dst_ref[dr, h] = _tn(dob_h, p["q2b"][:, sl]) + dst * el_h
            parts[dr]["dls"].append(jnp.sum(dst * stb.astype(F32), axis=0, keepdims=True) * el_h)
        dlb_tot = jnp.zeros((1, hw), F32)
        for dr in range(2):
            p = prep[dr]
            dq_ref, dz_ref = sets[dr][5], sets[dr][6]
            dq1, dk1, dq2, dk2, dls = (jnp.concatenate(parts[dr][n], axis=1) for n in ("dq1", "dk1", "dq2", "dk2", "dls"))
            dq = dq1 * p["e_qr"] + dq2 * p["e_q"]
            dk = dk1 * p["e_kr"] + dk2 * p["e_kl"]
            dcum = dq1 * p["q1"] - dk1 * p["k1"] + dq2 * p["q2"] - dk2 * p["k2"]
            dlast = jnp.sum(dk2 * p["k2"], axis=0, keepdims=True) + dls
            rowid = lax.broadcasted_iota(jnp.int32, (c, 1), 0)
            dcum = dcum + jnp.where(rowid == p["last_row"], dlast, 0.0)
            df = _nn3(p["acc_t"], dcum) / p["f"] - dk
            sg = p["sg"]
            dz_ref[...] = _bf(df * (1.0 - lb) * sg * (1.0 - sg))
            dlb_tot = dlb_tot + jnp.sum(df * (1.0 - sg), axis=0, keepdims=True)
            dq_ref[...] = _bf(dq * (p["sq"] * (1.0 + p["qr"] * (1.0 - p["sq"]))))
        _acc_all(dlb_ref, step, dlb_tot)

    def col(order, blkcol):
        return pl.BlockSpec((c, hw), lambda s: (order(s, nc, nt), blkcol))

    def st_spec(order):
        return pl.BlockSpec((1, HG_HEADS, d, d), lambda s: (order(s, nc, nt), 0, 0, 0))

    in_specs = []
    for dr, order in enumerate((rfw, rbw)):
        in_specs += [col(order, 0), col(order, 1 + dr), col(order, 3), st_spec(order), col(order, 0)]
    in_specs.append(_full(lbraw.shape))
    out_specs = [col(rfw, 0)] * 3 + [col(rbw, 0)] * 3 + [_whole((1, hw))]
    out_shape = [jax.ShapeDtypeStruct((T, hw), ACT)] * 6 + [jax.ShapeDtypeStruct((1, hw), F32)]
    return _host_call(
        body, ex, lambda: pl.program_id(0) == 0, lambda: pl.program_id(0) == nt - 1,
        name=name, grid=(nt,), in_specs=in_specs, out_specs=out_specs, out_shape=out_shape,
        scratch_shapes=[pltpu.VMEM((2, HG_HEADS, d, d), F32)], sem=("arbitrary",),
        args=(pb, pb, pb, s_fw, do, pb, pb, pb, s_bw, do, lbraw))


def _ret_log_gamma(h, rev):
    hh = RET_HEADS - 1 - h if rev else h
    return math.log(1.0 - 2.0 ** (-5.0 - hh))


def _rope(x, cos, sin):
    half = x.shape[1] // 2
    x1, x2 = x[:, :half], x[:, half:]
    return jnp.concatenate([x1 * cos - x2 * sin, x2 * cos + x1 * sin], axis=1)


def _unrope(dy, cos, sin):
    half = dy.shape[1] // 2
    d1, d2 = dy[:, :half], dy[:, half:]
    return jnp.concatenate([d1 * cos + d2 * sin, d2 * cos - d1 * sin], axis=1)


def _ret_decays(lg, rev):
    c = RET_CHUNK
    r = lax.broadcasted_iota(jnp.int32, (c, c), 0)
    k = lax.broadcasted_iota(jnp.int32, (c, c), 1)
    rel = (k - r) if rev else (r - k)
    dm = jnp.where(rel >= 0, jnp.exp(lg * jnp.maximum(rel, 0).astype(F32)), 0.0)
    pos = lax.broadcasted_iota(jnp.int32, (c, 1), 0).astype(F32)
    if rev:
        qdec = jnp.exp(lg * (c - pos))
        kdec = jnp.exp(lg * pos)
    else:
        qdec = jnp.exp(lg * (pos + 1.0))
        kdec = jnp.exp(lg * (c - 1.0 - pos))
    return dm, qdec, kdec


def _ret_fwd(q, k, v, cos, sin, lc, name, ex=None):
    T = q.shape[0]
    c, dk, dv = RET_CHUNK, RET_DK, RET_DV
    nt, nc = T // c, lc // c
    kscale = dk ** -0.5

    def body(qf, kf, vf, cf, sf_, qb, kb, vb, cb, sb_, of_ref, ob_ref, stf_ref, stb_ref, st_ref):
        @pl.when(pl.program_id(0) == 0)
        def _():
            st_ref[...] = jnp.zeros_like(st_ref)

        sets = ((qf, kf, vf, cf, sf_, of_ref, stf_ref), (qb, kb, vb, cb, sb_, ob_ref, stb_ref))
        combos = [(dr, h) for dr in range(2) for h in range(RET_HEADS)]
        prep = {}
        for dr, (q_ref, k_ref, v_ref, c_ref, s_ref, _, _) in enumerate(sets):
            rev = dr == 1
            cos_v, sin_v = c_ref[...], s_ref[...]
            for h in range(RET_HEADS):
                lg = _ret_log_gamma(h, rev)
                dm, qdec, kdec = _ret_decays(lg, rev)
                qh = _rope(q_ref[:, h * dk:(h + 1) * dk].astype(F32), cos_v, sin_v)
                kh = _rope(k_ref[:, h * dk:(h + 1) * dk].astype(F32), cos_v, sin_v) * kscale
                prep[dr, h] = dict(qb=_bf(qh), kb=_bf(kh), qin=_bf(qh * qdec), kin=_bf(kh * kdec),
                                   v=_bf(v_ref[:, h * dv:(h + 1) * dv]), dm=dm, decay=math.exp(lg * c))
        sc = {ch: _bf(_nt(prep[ch]["qb"], prep[ch]["kb"]) * prep[ch]["dm"]) for ch in combos}
        for dr, h in combos:
            p = prep[dr, h]
            o_ref, so_ref = sets[dr][5], sets[dr][6]
            st = st_ref[dr, h]
            stb = _bf(st)
            so_ref[0, h] = stb
            o_ref[:, h * dv:(h + 1) * dv] = _bf(_nn(sc[dr, h], p["v"]) + _nt(p["qin"], stb))
            st_ref[dr, h] = st * p["decay"] + _tn(p["v"], p["kin"])

    def spec(order, width):
        return pl.BlockSpec((c, width), lambda s: (order(s, nc, nt), 0))

    def st_spec(order):
        return pl.BlockSpec((1, RET_HEADS, dv, dk), lambda s: (order(s, nc, nt), 0, 0, 0))

    in_specs = []
    for order in (_fw_chunk, _bw_chunk):
        in_specs += [spec(order, RET_HEADS * dk), spec(order, RET_HEADS * dk), spec(order, RET_HEADS * dv),
                     spec(order, dk // 2), spec(order, dk // 2)]
    return _host_call(
        body, ex, lambda: pl.program_id(0) == 0, lambda: pl.program_id(0) == nt - 1,
        name=name, grid=(nt,), in_specs=in_specs,
        out_specs=[spec(_fw_chunk, RET_HEADS * dv), spec(_bw_chunk, RET_HEADS * dv), st_spec(_fw_chunk), st_spec(_bw_chunk)],
        out_shape=[jax.ShapeDtypeStruct((T, RET_HEADS * dv), ACT), jax.ShapeDtypeStruct((T, RET_HEADS * dv), ACT),
                   jax.ShapeDtypeStruct((nt, RET_HEADS, dv, dk), ACT), jax.ShapeDtypeStruct((nt, RET_HEADS, dv, dk), ACT)],
        scratch_shapes=[pltpu.VMEM((2, RET_HEADS, dv, dk), F32)], sem=("arbitrary",),
        args=(q, k, v, cos, sin, q, k, v, cos, sin))


def _ret_bwd(q, k, v, cos, sin, s_fw, s_bw, do, lc, name, ex=None):
    T = q.shape[0]
    c, dk, dv = RET_CHUNK, RET_DK, RET_DV
    nt, nc = T // c, lc // c
    kscale = dk ** -0.5

    def rfw(s, nc_, nt_):
        return _fw_chunk(nt_ - 1 - s, nc_, nt_)

    def rbw(s, nc_, nt_):
        return _bw_chunk(nt_ - 1 - s, nc_, nt_)

    def body(qf, kf, vf, cf, sf_, stf, dof, qb, kb, vb, cb, sb_, stb_, dob_,
             dqf, dkf, dvf, dqb, dkb, dvb, dst_ref):
        @pl.when(pl.program_id(0) == 0)
        def _():
            dst_ref[...] = jnp.zeros_like(dst_ref)

        sets = ((qf, kf, vf, cf, sf_, stf, dof, dqf, dkf, dvf), (qb, kb, vb, cb, sb_, stb_, dob_, dqb, dkb, dvb))
        combos = [(dr, h) for dr in range(2) for h in range(RET_HEADS)]
        prep = {}
        for dr, (q_ref, k_ref, v_ref, c_ref, s_ref, _, do_ref, _, _, _) in enumerate(sets):
            rev = dr == 1
            cos_v, sin_v = c_ref[...], s_ref[...]
            for h in range(RET_HEADS):
                lg = _ret_log_gamma(h, rev)
                dm, qdec, kdec = _ret_decays(lg, rev)
                qh = _rope(q_ref[:, h * dk:(h + 1) * dk].astype(F32), cos_v, sin_v)
                kh = _rope(k_ref[:, h * dk:(h + 1) * dk].astype(F32), cos_v, sin_v) * kscale
                prep[dr, h] = dict(qb=_bf(qh), kb=_bf(kh), qin=_bf(qh * qdec), kin=_bf(kh * kdec),
                                   v=_bf(v_ref[:, h * dv:(h + 1) * dv]), dob=_bf(do_ref[:, h * dv:(h + 1) * dv]),
                                   dm=dm, qdec=qdec, kdec=kdec, decay=math.exp(lg * c), cos=cos_v, sin=sin_v)
        sc = {ch: _bf(_nt(prep[ch]["qb"], prep[ch]["kb"]) * prep[ch]["dm"]) for ch in combos}
        dsc = {ch: _bf(_nt(prep[ch]["dob"], prep[ch]["v"]) * prep[ch]["dm"]) for ch in combos}
        carried = {}
        for dr, h in combos:
            p = prep[dr, h]
            dv_ref = sets[dr][9]
            dst = dst_ref[dr, h]
            dstb = _bf(dst)
            carried[dr, h] = dstb
            dv_ref[:, h * dv:(h + 1) * dv] = _bf(_tn(sc[dr, h], p["dob"]) + _nt(p["kin"], dstb))
            dst_ref[dr, h] = _tn(p["dob"], p["qin"]) + dst * p["decay"]
        for dr, h in combos:
            p = prep[dr, h]
            st_in, dq_ref, dk_ref = sets[dr][5], sets[dr][7], sets[dr][8]
            dq_r = _nn(dsc[dr, h], p["kb"]) + _nn(p["dob"], st_in[0, h]) * p["qdec"]
            dk_r = _tn(dsc[dr, h], p["qb"]) + _nn(p["v"], carried[dr, h]) * p["kdec"]
            dq_ref[:, h * dk:(h + 1) * dk] = _bf(_unrope(dq_r, p["cos"], p["sin"]))
            dk_ref[:, h * dk:(h + 1) * dk] = _bf(_unrope(dk_r * kscale, p["cos"], p["sin"]))

    def spec(order, width):
        return pl.BlockSpec((c, width), lambda s: (order(s, nc, nt), 0))

    def st_spec(order):
        return pl.BlockSpec((1, RET_HEADS, dv, dk), lambda s: (order(s, nc, nt), 0, 0, 0))

    in_specs = []
    for order in (rfw, rbw):
        in_specs += [spec(order, RET_HEADS * dk), spec(order, RET_HEADS * dk), spec(order, RET_HEADS * dv),
                     spec(order, dk // 2), spec(order, dk // 2), st_spec(order), spec(order, RET_HEADS * dv)]
    out_specs, out_shape = [], []
    for order in (rfw, rbw):
        out_specs += [spec(order, RET_HEADS * dk), spec(order, RET_HEADS * dk), spec(order, RET_HEADS * dv)]
        out_shape += [jax.ShapeDtypeStruct((T, RET_HEADS * dk), ACT), jax.ShapeDtypeStruct((T, RET_HEADS * dk), ACT),
                      jax.ShapeDtypeStruct((T, RET_HEADS * dv), ACT)]
    return _host_call(
        body, ex, lambda: pl.program_id(0) == 0, lambda: pl.program_id(0) == nt - 1,
        name=name, grid=(nt,), in_specs=in_specs, out_specs=out_specs, out_shape=out_shape,
        scratch_shapes=[pltpu.VMEM((2, RET_HEADS, dv, dk), F32)], sem=("arbitrary",),
        args=(q, k, v, cos, sin, s_fw, do, q, k, v, cos, sin, s_bw, do))


def _attn_rope_tables(lc, l):
    t = jnp.arange(l)
    row = (t // GRID_W).astype(F32)
    colp = (t % GRID_W).astype(F32)
    n_freq = HEAD_DIM // 4
    inv = 10000.0 ** (-jnp.arange(n_freq, dtype=F32) / n_freq)
    ang = jnp.concatenate([row[:, None] * inv, colp[:, None] * inv], axis=-1)
    cos = jnp.concatenate([jnp.ones((lc, HEAD_DIM // 2), F32), jnp.cos(ang)], axis=0)
    sin = jnp.concatenate([jnp.zeros((lc, HEAD_DIM // 2), F32), jnp.sin(ang)], axis=0)
    return jnp.concatenate([cos, cos], axis=1), jnp.concatenate([-sin, sin], axis=1)


def _ret_rope_tables(lc, l):
    theta = 1.0 / (10000.0 ** jnp.linspace(0.0, 1.0, RET_DK // 2, dtype=F32))
    ang = jnp.arange(l, dtype=F32)[:, None] * theta
    cos = jnp.concatenate([jnp.ones((lc, RET_DK // 2), F32), jnp.cos(ang)], axis=0)
    sin = jnp.concatenate([jnp.zeros((lc, RET_DK // 2), F32), jnp.sin(ang)], axis=0)
    return cos, sin


def _heads_major(slab, n_heads):
    t = slab.shape[0]
    return slab.reshape(t, n_heads, HEAD_DIM).transpose(1, 0, 2)


def _slab(hm):
    nh, t, hd = hm.shape
    return hm.transpose(1, 0, 2).reshape(t, nh * hd)


COL_SHARDED = ("ffn_in0", "ffn_in1", "even_in", "odd_in")


def _full_weight(name, g):
    if name in COL_SHARDED:
        return g.transpose(1, 0, 2).reshape(g.shape[1], -1)
    return g.reshape(-1, g.shape[2])


def _shard_slots(name, g):
    if name in COL_SHARDED:
        return g.reshape(g.shape[0], N_DEV, -1).transpose(1, 0, 2)
    return g.reshape(N_DEV, -1, g.shape[1])


def _local_step(xs, target, mv, norm_g, w, qk_g, sink, hg_out_g, lbraw, lc, shards=None):
    T, dm = xs.shape
    l = T - lc
    tm = lc
    blk = ATTN_BLOCK
    d2, d3 = 2 * dm, 3 * dm
    w = dict(w)
    gw, recv = {}, {}

    def ms(layer, a, b):
        return mv[layer, :, :, a:b]

    def gather(names):
        return None if shards is None else _Exchange(GATHER2, [shards[n] for n in names])

    def arrived(names, got):
        for n, g in zip(names, got):
            w[n] = _full_weight(n, g)

    def scatter(names):
        return None if shards is None else _Exchange(SCATTER, [_shard_slots(n, gw[n]) for n in names])

    def scattered(names, got):
        for n, g in zip(names, got):
            recv[n] = g

    g00, g01, g10, g11 = (norm_g[i, j][None, :] for i in (0, 1) for j in (0, 1))

    riding = ["even_out"]
    (pa, pb), got = _pre_fwd(xs, g00, ms(0, 0, d2), w["even_in"], ((0, 768), (768, 3328)), tm, "pre0_fwd", gather(riding))
    arrived(riding, got)
    cos2, sin2 = _attn_rope_tables(lc, l)
    cosp, sinp = jnp.concatenate([cos2, cos2], axis=1), jnp.concatenate([sin2, sin2], axis=1)
    gains5 = jnp.concatenate([jnp.broadcast_to(jnp.tile(qk_g[0], 2), (N_PAIRS - 1, PAIR)), jnp.tile(qk_g[1], 2)[None]])[:, None, :]
    qt, ks, vs = _qk_slab_fwd(pa, gains5, cosp, sinp, tm, "qk_prep_fwd")
    sinkb = jnp.broadcast_to(sink.reshape(ATTN_KV, 4, 1, 1), (ATTN_KV, 4, blk, 1)).reshape(ATTN_KV, 4 * blk, 1)
    riding = ["ffn_in0"]
    (a_slab, lse), got = _attn_slab_fwd(qt, ks, vs, sinkb, lc, "attn_fwd", gather(riding))
    arrived(riding, got)
    riding = ["ffn_out0", "odd_out"]
    (hg_of, hg_ob, hg_sf, hg_sb), got = _gla_fwd(pb, lbraw, lc, "hgrn_fwd", gather(riding))
    arrived(riding, got)
    x01, z0 = _post_fwd(xs, hg_of, hg_ob, pb, 4, hg_out_g, a_slab, w["even_out"], ms(0, d2, d3), HG_D, tm, "post0_fwd")
    riding = ["odd_in"]
    (x02, u0, f0), got = _ffn_fwd(x01, g01, ms(0, d3, 6 * dm), w["ffn_in0"], w["ffn_out0"], tm, "ffn0_fwd", ex=gather(riding))
    arrived(riding, got)

    riding = ["ffn_out1"]
    (rq, rk, rv, rg), got = _pre_fwd(x02, g10, ms(1, 0, d2), w["odd_in"],
                                     ((0, 1024), (1024, 2048), (2048, 4096), (4096, 6144)), tm, "pre1_fwd", gather(riding),
                                     out_dtype=ACT)
    arrived(riding, got)
    rcos, rsin = _ret_rope_tables(lc, l)
    riding = ["ffn_in1"]
    (rt_of, rt_ob, rt_sf, rt_sb), got = _ret_fwd(rq, rk, rv, rcos, rsin, lc, "ret_fwd", gather(riding))
    arrived(riding, got)
    x11, z1 = _post_fwd(x02, rt_of, rt_ob, rg, 0, None, None, w["odd_out"], ms(1, d2, d3), RET_DV, tm, "post1_fwd")
    (dx, u1, f1, loss), _ = _ffn_fwd(x11, g11, ms(1, d3, 6 * dm), w["ffn_in1"], w["ffn_out1"], tm, "ffn1_fwd", target)

    (dx, h, du, act, df, dms_f1, dg11), _ = _ffn_bwd(x11, dx, u1, f1, g11, ms(1, d3, 6 * dm), w["ffn_in1"], w["ffn_out1"], tm,
                                                     "ffn1_bwd")
    gw["ffn_in1"] = _wgrad(h, du, "wg_ffn_in1")
    gw["ffn_out1"] = _wgrad(act, df, "wg_ffn_out1")
    do1, dgr1, dy1, z1_t, dgate_p1, _ = _post_bwd(dx, z1, rt_of, rt_ob, rg, 0, None, w["odd_out"], ms(1, d2, d3), 0, RET_DV, tm,
                                                  "post1_bwd")
    gw["odd_out"] = _wgrad(z1_t, dy1, "wg_odd_out")
    riding = ["ffn_in1", "ffn_out1"]
    (dqf, dkf, dvf, dqb, dkb, dvb), got = _ret_bwd(rq, rk, rv, rcos, rsin, rt_sf, rt_sb, do1, lc, "ret_bwd", scatter(riding))
    scattered(riding, got)
    riding = ["odd_out"]
    (dx, h, dp, dms_p1, dg10), got = _pre_bwd(x02, dx, g10, ms(1, 0, d2), w["odd_in"],
                                              [(0, [dqf, dqb]), (1024, [dkf, dkb]), (2048, [dvf, dvb]), (4096, [dgr1])], tm,
                                              "pre1_bwd", ex=scatter(riding))
    scattered(riding, got)
    gw["odd_in"] = _wgrad(h, dp, "wg_odd_in")

    riding = ["odd_in"]
    (dx, h, du, act, df, dms_f0, dg01), got = _ffn_bwd(x01, dx, u0, f0, g01, ms(0, d3, 6 * dm), w["ffn_in0"], w["ffn_out0"], tm,
                                                       "ffn0_bwd", scatter(riding))
    scattered(riding, got)
    gw["ffn_in0"] = _wgrad(h, du, "wg_ffn_in0")
    gw["ffn_out0"] = _wgrad(act, df, "wg_ffn_out0")
    do0, dgr0, da0, dy0, z0_t, dgate_p0, d_hg_gain = _post_bwd(dx, z0, hg_of, hg_ob, pb, 4, hg_out_g, w["even_out"],
                                                              ms(0, d2, d3), 512, HG_D, tm, "post0_bwd")
    gw["even_out"] = _wgrad(z0_t, dy0, "wg_even_out")
    riding = ["ffn_in0", "ffn_out0"]
    (hq_f, hz_f, hv_f, hq_b, hz_b, hv_b, dlb), got = _gla_bwd(pb, lbraw, hg_sf, hg_sb, do0, lc, "hgrn_bwd", scatter(riding))
    scattered(riding, got)
    riding = ["even_out"]
    (dq_att, dk_att, dv_att, dsink), got = _attn_slab_bwd(qt, ks, vs, sinkb, a_slab, lse, da0, lc, "attn_bwd", scatter(riding))
    scattered(riding, got)
    dqk_raw, dgain5 = _qk_slab_bwd(dq_att, dk_att, pa, gains5, cosp, sinp, tm, "qk_prep_bwd")
    pieces0 = [(0, [dqk_raw]), (640, [dv_att]),
               (768, [hq_f, hq_b]), (1280, [hz_f]), (1792, [hz_b]), (2304, [hv_f, hv_b]), (2816, [dgr0])]
    (dx, h, dp, dms_p0, dg00), _ = _pre_bwd(xs, dx, g00, ms(0, 0, d2), w["even_in"], pieces0, tm, "pre0_bwd",
                                            latent_dx=shards is not None)
    gw["even_in"] = _wgrad(h, dp, "wg_even_in")

    dmv = jnp.stack([jnp.concatenate([dms_p0, dgate_p0, dms_f0], axis=2), jnp.concatenate([dms_p1, dgate_p1, dms_f1], axis=2)])
    small = {
        "dmv": dmv,
        "norm_g": jnp.stack([jnp.stack([dg00[0], dg01[0]]), jnp.stack([dg10[0], dg11[0]])]),
        "qk_g": jnp.stack([jnp.sum(dgain5[:N_PAIRS - 1, 0].reshape(-1, HEAD_DIM), axis=0),
                           jnp.sum(dgain5[N_PAIRS - 1, 0].reshape(-1, HEAD_DIM), axis=0)]),
        "sink": dsink.reshape(ATTN_HEADS),
        "hg_out_g": d_hg_gain[0],
        "lb": dlb[0],
        "loss": loss[0, 0],
    }
    if shards is not None:
        gw = {n: recv.get(n, g) for n, g in gw.items()}
    return loss, dx, gw, small


HBM_SPEC = pl.BlockSpec(memory_space=pltpu.HBM)


def _my_index():
    return 4 * lax.axis_index("x") + 2 * lax.axis_index("y") + lax.axis_index("c")


def _peer(k):
    pos = []
    for axis, bit in (("x", 4), ("y", 2), ("c", 1)):
        a = lax.axis_index(axis)
        pos.append(1 - a if k & bit else a)
    return tuple(pos)


def _peer_index(k):
    px, py, pc = _peer(k)
    return 4 * px + 2 * py + pc


GATHER, SCATTER = "gather", "scatter"
GATHER2 = "gather over ICI once per chip"
SIBLING = 1
OTHER_CHIPS = (2, 4, 6)


class _Exchange:
    def __init__(self, mode, arrays):
        self.mode, self.arrays, self.n = mode, list(arrays), len(arrays)

    def out_shape(self):
        if self.mode in (GATHER, GATHER2):
            return [jax.ShapeDtypeStruct((N_DEV,) + a.shape, a.dtype) for a in self.arrays]
        return [jax.ShapeDtypeStruct(a.shape, a.dtype) for a in self.arrays]

    def specs(self):
        return [HBM_SPEC] * self.n

    def scratch(self):
        return [pltpu.SemaphoreType.DMA((self.n, N_DEV - 1)), pltpu.SemaphoreType.DMA((self.n, N_DEV - 1)),
                pltpu.SemaphoreType.DMA((self.n,))]

    def _copies(self, in_refs, out_refs, send_sems, recv_sems, local_sems, landing):
        me = _my_index()
        local, remote = [], []
        for a, (src, dst) in enumerate(zip(in_refs, out_refs)):
            part = (lambda j, s=src: s) if self.mode == GATHER else (lambda j, s=src: s.at[j])
            local.append(pltpu.make_async_copy(part(me), dst.at[me], local_sems.at[a]))
            for k in range(1, N_DEV):
                pj = _peer_index(k)
                remote.append(pltpu.make_async_remote_copy(
                    src_ref=part(pj), dst_ref=dst.at[pj if landing else me], send_sem=send_sems.at[a, k - 1],
                    recv_sem=recv_sems.at[a, k - 1], device_id=_peer(k), device_id_type=MESH))
        return local, remote

    def _copy2(self, a, src, dst, sems, slot, relation, to):
        send_sems, recv_sems, _ = sems
        return pltpu.make_async_remote_copy(src_ref=src, dst_ref=dst.at[slot], send_sem=send_sems.at[a, relation - 1],
                                            recv_sem=recv_sems.at[a, relation - 1], device_id=_peer(to), device_id_type=MESH)

    def start(self, in_refs, out_refs, sems):
        if self.mode == GATHER2:
            me = _my_index()
            for a, (src, dst) in enumerate(zip(in_refs, out_refs)):
                pltpu.make_async_copy(src, dst.at[me], sems[2].at[a]).start()
                for k in (SIBLING,) + OTHER_CHIPS:
                    self._copy2(a, src, dst, sems, me, k, k).start()
            return
        local, remote = self._copies(in_refs, out_refs, *sems, landing=False)
        for cp in local + remote:
            cp.start()

    def forward(self, in_refs, out_refs, sems):
        for a, (src, dst) in enumerate(zip(in_refs, out_refs)):
            for r in OTHER_CHIPS:
                pj = _peer_index(r)
                self._copy2(a, src, dst, sems, pj, r, r).wait_recv()
                self._copy2(a, dst.at[pj], dst, sems, pj, r ^ SIBLING, SIBLING).start()

    def wait(self, in_refs, out_refs, sems):
        if self.mode == GATHER2:
            me = _my_index()
            for a, (src, dst) in enumerate(zip(in_refs, out_refs)):
                for k in (SIBLING,) + OTHER_CHIPS:
                    self._copy2(a, src, dst, sems, me, k, k).wait_send()
                self._copy2(a, src, dst, sems, _peer_index(SIBLING), SIBLING, SIBLING).wait_recv()
                for r in OTHER_CHIPS:
                    passed = self._copy2(a, src, dst, sems, _peer_index(r ^ SIBLING), r ^ SIBLING, SIBLING)
                    passed.wait_send()
                    passed.wait_recv()
                pltpu.make_async_copy(src, dst.at[me], sems[2].at[a]).wait()
            return
        local, remote = self._copies(in_refs, out_refs, *sems, landing=True)
        for cp in remote:
            cp.wait_send()
            cp.wait_recv()
        for cp in local:
            cp.wait()

    def ride(self, refs, n_in, n_out, first, mid, last):
        refs = list(refs)
        n = self.n
        x_in = refs[n_in:n_in + n]
        x_out = refs[n_in + n + n_out:n_in + 2 * n + n_out]
        sems = refs[n_in + 2 * n + n_out:n_in + 2 * n + n_out + 3]

        @pl.when(first)
        def _():
            self.start(x_in, x_out, sems)

        if self.mode == GATHER2:
            @pl.when(mid)
            def _():
                self.forward(x_in, x_out, sems)

        @pl.when(last)
        def _():
            self.wait(x_in, x_out, sems)

        return refs[:n_in] + refs[n_in + n:n_in + n + n_out] + refs[n_in + 2 * n + n_out + 3:]

    def call(self, name):
        n = self.n

        def body(*refs):
            ins, outs, sems = refs[:n], refs[n:2 * n], refs[2 * n:]
            self.start(ins, outs, sems)
            if self.mode == GATHER2:
                self.forward(ins, outs, sems)
            self.wait(ins, outs, sems)

        return pl.pallas_call(body, name=name, in_specs=self.specs(), out_specs=self.specs(), out_shape=self.out_shape(),
                              scratch_shapes=self.scratch())(*self.arrays)


def _all_gather(v, name):
    return _Exchange(GATHER, [v]).call(name)[0]


def _hosted(kernel_body, ex, n_in, n_out, first, last, grid):
    if ex is None:
        return kernel_body

    def body(*refs):
        mid = pl.program_id(0) == (2 * grid[0]) // 3 if len(grid) == 1 else None
        kernel_body(*ex.ride(refs, n_in, n_out, first(), mid, last()))

    return body


def _host_call(kernel_body, ex, first, last, name, grid, in_specs, out_specs, out_shape, scratch_shapes, sem, args):
    n_in, n_out = len(in_specs), len(out_specs)
    if ex is None:
        outs = pl.pallas_call(kernel_body, name=name, grid=grid, in_specs=in_specs, out_specs=out_specs, out_shape=out_shape,
                              scratch_shapes=scratch_shapes, compiler_params=_cp(*sem))(*args)
        return list(outs), []
    outs = pl.pallas_call(
        _hosted(kernel_body, ex, n_in, n_out, first, last, grid), name=name, grid=grid,
        in_specs=list(in_specs) + ex.specs(), out_specs=list(out_specs) + ex.specs(),
        out_shape=list(out_shape) + ex.out_shape(), scratch_shapes=ex.scratch() + list(scratch_shapes),
        compiler_params=_cp(*sem))(*args, *ex.arrays)
    return list(outs[:n_out]), list(outs[n_out:])


def _mod_fwd(call, mod_w, bias, name):
    nl, dm, n = mod_w.shape

    def body(c_ref, w_ref, b_ref, o_ref):
        cv = c_ref[...]
        cond = _bf(cv * _sig(cv))
        for layer in range(nl):
            o_ref[layer] = _nn(cond, _bf(w_ref[layer])) + b_ref[layer]

    return pl.pallas_call(
        body, name=name, out_shape=jax.ShapeDtypeStruct((nl, call.shape[0], n), F32),
        compiler_params=pltpu.CompilerParams(vmem_limit_bytes=VMEM_LIMIT),
    )(call, mod_w, bias)


def _mod_bwd(call, dm_all, mod_w, name):
    nl, dm, n = mod_w.shape

    def body(c_ref, d_ref, w_ref, gw_ref, dc_ref):
        cv = c_ref[...]
        cond = _bf(cv * _sig(cv))
        dc = jnp.zeros(cv.shape, F32)
        for layer in range(nl):
            db = _bf(d_ref[layer])
            gw_ref[layer] = _tn(cond, db)
            dc = dc + _nt(db, _bf(w_ref[layer]))
        dc_ref[...] = dc

    return pl.pallas_call(
        body, name=name,
        out_shape=[jax.ShapeDtypeStruct(mod_w.shape, F32), jax.ShapeDtypeStruct(call.shape, F32)],
        compiler_params=pltpu.CompilerParams(vmem_limit_bytes=VMEM_LIMIT),
    )(call, dm_all, mod_w)


def _sum_parts(g, name):
    def body(g_ref, o_ref):
        acc = g_ref[0]
        for j in range(1, g.shape[0]):
            acc = acc + g_ref[j]
        o_ref[...] = acc

    return pl.pallas_call(body, name=name, out_shape=jax.ShapeDtypeStruct(g.shape[1:], g.dtype))(g)


def _small_finish(dcond_g, c_ctx, dlb, lbraw, dm_ctx, dm_lat, name):
    def body(dc_ref, c_ref, dlb_ref, lb_ref, mc_ref, ml_ref, gc_ref, glb_ref, gb_ref):
        acc = dc_ref[0, 0:1, :]
        for j in range(1, N_DEV):
            acc = acc + dc_ref[j, 0:1, :]
        cv = c_ref[...]
        s = _sig(cv)
        gc_ref[...] = acc * (s * (1.0 + cv * (1.0 - s)))
        lb = _lower_bound(lb_ref)
        d0 = dlb_ref[...] * lb * (1.0 - lb)
        glb_ref[0:1, :] = d0
        glb_ref[1:2, :] = -d0
        gb_ref[...] = mc_ref[...] + ml_ref[...]

    return pl.pallas_call(
        body, name=name,
        out_shape=[jax.ShapeDtypeStruct(c_ctx.shape, F32), jax.ShapeDtypeStruct(lbraw.shape, F32),
                   jax.ShapeDtypeStruct(dm_ctx.shape, F32)],
    )(dcond_g, c_ctx, dlb, lbraw, dm_ctx, dm_lat)


def _row_tile(r, cap, mult):
    best = r
    for t in range(mult, min(r, cap) + 1, mult):
        if r % t == 0:
            best = t
    return best


def _adam(g_list, w, m, v, name, ex=None):
    nl, r, cdim = w.shape
    p = g_list[0].shape[0]
    tr = _row_tile(r, 128, 16)
    ni = r // tr

    def body(*refs):
        g_refs = refs[:nl]
        w_ref, m_ref, v_ref, go_ref, d_ref, mo_ref, vo_ref = refs[nl:]
        layer = pl.program_id(0)

        def total(g_ref):
            acc = g_ref[0].astype(F32)
            for j in range(1, p):
                acc = acc + g_ref[j].astype(F32)
            return acc

        g = total(g_refs[0])
        for k in range(1, nl):
            g = jnp.where(layer == k, total(g_refs[k]), g)
        m2 = ADAM_B1 * m_ref[0] + (1.0 - ADAM_B1) * g
        v2 = ADAM_B2 * v_ref[0] + (1.0 - ADAM_B2) * (g * g)
        m_hat = m2 / (1.0 - ADAM_B1 ** ADAM_STEP)
        v_hat = v2 / (1.0 - ADAM_B2 ** ADAM_STEP)
        go_ref[0] = g
        d_ref[0] = -ADAM_LR * (m_hat / (jnp.sqrt(v_hat) + ADAM_EPS) + ADAM_WD * w_ref[0])
        mo_ref[0] = m2
        vo_ref[0] = v2

    def g_spec(k):
        return pl.BlockSpec((p, tr, cdim), lambda la, i: (0, jnp.where(la == k, i, jnp.where(la < k, 0, ni - 1)), 0))

    spec = pl.BlockSpec((1, tr, cdim), lambda la, i: (la, i, 0))
    return _host_call(
        body, ex, lambda: (pl.program_id(0) == 0) & (pl.program_id(1) == 0),
        lambda: (pl.program_id(0) == nl - 1) & (pl.program_id(1) == ni - 1),
        name=name, grid=(nl, ni),
        in_specs=[g_spec(k) for k in range(nl)] + [spec, spec, spec],
        out_specs=[spec] * 4, out_shape=[jax.ShapeDtypeStruct((nl, r, cdim), F32)] * 4,
        scratch_shapes=[], sem=("arbitrary", "arbitrary"), args=(*g_list, w, m, v))


def _f32_as_rows(a, width):
    return lax.bitcast_convert_type(a.reshape(-1), BF16).reshape(-1, width)


def _rows_as_f32(rows):
    return lax.bitcast_convert_type(rows.reshape(rows.shape[:-2] + (-1, 2)), F32)


def _pad_rows(a, mult):
    r = (-a.shape[-2]) % mult
    if r == 0:
        return a
    widths = [(0, 0)] * (a.ndim - 2) + [(0, r), (0, 0)]
    return jnp.pad(a, widths)


def _pack_flat(parts, lane):
    flat = jnp.concatenate([p.reshape(-1).astype(F32) for p in parts])
    n = flat.shape[0]
    rows = -(-n // lane)
    rows += (-rows) % 8
    return jnp.pad(flat, (0, rows * lane - n)).reshape(rows, lane)


def _unpack_flat(packed, shapes):
    flat = packed.reshape(-1)
    out, off = [], 0
    for s in shapes:
        n = math.prod(s)
        out.append(flat[off:off + n].reshape(s))
        off += n
    return out


def kernel(x, c, ctx, c_ctx, mod_w, mod_b, norm_g, ffn_w_in, ffn_w_out, even_w_in, even_w_out, attn_qk_norm_g, attn_sink, hgrn_out_norm_g, hgrn_lb, odd_w_in, odd_w_out, loss_target, m_c_ctx, m_mod_w, m_mod_b, m_norm_g, m_ffn_w_in, m_ffn_w_out, m_even_w_in, m_even_w_out, m_attn_qk_norm_g, m_attn_sink, m_hgrn_out_norm_g, m_hgrn_lb, m_odd_w_in, m_odd_w_out, v_c_ctx, v_mod_w, v_mod_b, v_norm_g, v_ffn_w_in, v_ffn_w_out, v_even_w_in, v_even_w_out, v_attn_qk_norm_g, v_attn_sink, v_hgrn_out_norm_g, v_hgrn_lb, v_odd_w_in, v_odd_w_out):
    me = _my_index()
    lc, dm = ctx.shape[1], x.shape[2]
    nmod = mod_w.shape[2]
    big = (ffn_w_in, ffn_w_out, even_w_in, even_w_out, odd_w_in, odd_w_out)

    extra = _pad_rows(jnp.concatenate([_f32_as_rows(c, dm), _f32_as_rows(norm_g, dm)], axis=0), 16)
    shards = {"ffn_in0": ffn_w_in[0], "ffn_in1": ffn_w_in[1], "ffn_out0": ffn_w_out[0], "ffn_out1": ffn_w_out[1],
              "even_in": even_w_in[0], "even_out": even_w_out[0], "odd_in": odd_w_in[0], "odd_out": odd_w_out[0]}
    shards = {n: a.astype(BF16) for n, a in shards.items()}
    first = _Exchange(GATHER2, [shards["even_in"], extra]).call("gather_first")
    w = {"even_in": _full_weight("even_in", first[0])}
    c_all = _rows_as_f32(first[1][:, 0:2])
    norm_g_all = _rows_as_f32(first[1][:, 2:3]).reshape(N_DEV, 2, 2, -1)
    norm_g_full = norm_g_all.transpose(1, 2, 0, 3).reshape(2, 2, dm)

    call = jnp.concatenate([c_all, c_ctx[None, :], jnp.zeros((16 - N_DEV - 1, dm), F32)], axis=0)
    bias = lax.dynamic_slice_in_dim(mod_b, me * nmod, nmod, axis=1)[:, None, :]
    m_sh = _mod_fwd(call, mod_w, bias, "mod_fwd")
    m_g = _all_gather(m_sh.reshape(-1, nmod), "gather_mod").reshape(N_DEV, 2, 16, nmod)
    m_all = m_g.transpose(1, 2, 0, 3).reshape(2, 16, -1)
    m_lat = lax.dynamic_index_in_dim(m_all, me, axis=1, keepdims=False)
    mv = jnp.stack([m_all[:, N_DEV], m_lat], axis=1)[:, :, None, :]

    xs = jnp.concatenate([ctx[0], x[0]], axis=0)
    _, dxs, gw, small = _local_step(xs, loss_target[0], mv, norm_g_full, w, attn_qk_norm_g[0], attn_sink[0],
                                    hgrn_out_norm_g, hgrn_lb, lc, shards)
    grad_x = dxs[None]

    last = _Exchange(SCATTER, [_shard_slots("even_in", gw["even_in"])])
    big_g = [[gw["ffn_in0"], gw["ffn_in1"]], [gw["ffn_out0"], gw["ffn_out1"]], None, [gw["even_out"]],
             [gw["odd_in"]], [gw["odd_out"]]]
    big_m = (m_ffn_w_in, m_ffn_w_out, m_even_w_in, m_even_w_out, m_odd_w_in, m_odd_w_out)
    big_v = (v_ffn_w_in, v_ffn_w_out, v_even_w_in, v_even_w_out, v_odd_w_in, v_odd_w_out)
    big_names = ("ffn_w_in", "ffn_w_out", "even_w_in", "even_w_out", "odd_w_in", "odd_w_out")
    big_out = [None] * 6
    for i in (0, 1, 3, 4, 5, 2):
        big_out[i], got = _adam(big_g[i], big[i], big_m[i], big_v[i], "adam_" + big_names[i], last if i == 0 else None)
        if i == 0:
            big_g[2] = [got[0]]
    big_res = [[big_out[i][k] for i in range(6)] for k in range(4)]

    dmv = small["dmv"]
    small_shapes = [(2, 6 * dm), (2, 6 * dm), (2, 2, dm), (2, HEAD_DIM), (ATTN_HEADS,), (HG_D,), (HG_HEADS * HG_D,), (1,)]
    vec = _pack_flat([dmv[:, 0, 0], dmv[:, 1, 0], small["norm_g"], small["qk_g"], small["sink"], small["hg_out_g"],
                      small["lb"], small["loss"]], 128)
    vec_g = _all_gather(vec, "gather_small")
    tot = _unpack_flat(_sum_parts(vec_g, "sum_small"), small_shapes)
    dm_ctx_tot, dm_lat_tot, g_norm_full, g_qk, g_sink, g_hg, dlb_tot, loss_tot = tot
    dm_lat_each = vec_g.reshape(N_DEV, -1)[:, 12 * dm:24 * dm].reshape(N_DEV, 2, 6 * dm)
    dm_lat_mine = lax.dynamic_slice_in_dim(dm_lat_each, me * nmod, nmod, axis=2).transpose(1, 0, 2)
    dm_ctx_mine = lax.dynamic_slice_in_dim(dm_ctx_tot, me * nmod, nmod, axis=1)[:, None, :]
    dm_all = jnp.concatenate([dm_lat_mine, dm_ctx_mine, jnp.zeros((2, 16 - N_DEV - 1, nmod), F32)], axis=1)
    g_mod_w, dcond = _mod_bwd(call, dm_all, mod_w, "mod_bwd")
    dcond_g = _all_gather(dcond[N_DEV:], "gather_dcond")
    g_c_ctx, g_lb, g_mod_b = _small_finish(dcond_g, c_ctx[None, :], dlb_tot[None, :], hgrn_lb, dm_ctx_tot, dm_lat_tot,
                                           "small_finish")
    g_norm = lax.dynamic_slice_in_dim(g_norm_full, me * norm_g.shape[2], norm_g.shape[2], axis=2)

    mod_res, _ = _adam([g_mod_w[0][None], g_mod_w[1][None]], mod_w, m_mod_w, v_mod_w, "adam_mod_w")

    sm_w = (c_ctx, mod_b, norm_g, attn_qk_norm_g, attn_sink, hgrn_out_norm_g, hgrn_lb)
    sm_m = (m_c_ctx, m_mod_b, m_norm_g, m_attn_qk_norm_g, m_attn_sink, m_hgrn_out_norm_g, m_hgrn_lb)
    sm_v = (v_c_ctx, v_mod_b, v_norm_g, v_attn_qk_norm_g, v_attn_sink, v_hgrn_out_norm_g, v_hgrn_lb)
    sm_g = (g_c_ctx, g_mod_b, g_norm, g_qk, g_sink, g_hg, g_lb)
    sm_shapes = [a.shape for a in sm_w]
    sm_out, _ = _adam([_pack_flat(sm_g, 128)[None]], _pack_flat(sm_w, 128)[None], _pack_flat(sm_m, 128)[None],
                      _pack_flat(sm_v, 128)[None], "adam_small")
    sm_res = [_unpack_flat(o, sm_shapes) for o in sm_out]

    def ordered(k):
        s, b = sm_res[k], big_res[k]
        return [s[0], mod_res[k], s[1], s[2], b[0], b[1], b[2], b[3], s[3], s[4], s[5], s[6], b[4], b[5]]

    return (loss_tot[0], grad_x, *ordered(0), *ordered(1), *ordered(2), *ordered(3))
```

```python
import functools
import math

import jax
import jax.numpy as jnp
from jax import lax
from jax.experimental import pallas as pl
from jax.experimental.pallas import tpu as pltpu

F32 = jnp.float32
BF16 = jnp.bfloat16
EPS = 1e-6
N_DEV = 8
MESH = pl.DeviceIdType.MESH

HEAD_DIM = 64
ATTN_HEADS = 8
ATTN_KV = 2
ATTN_BLOCK = 128
WINDOW = 128
GRID_W = 64
HG_HEADS = 4
HG_D = 128
HG_CHUNK = 64
RET_HEADS = 4
RET_DK = 256
RET_DV = 512
RET_CHUNK = 128
NEG = -1e30

ADAM_LR = 0.001
ADAM_B1 = 0.9
ADAM_B2 = 0.999
ADAM_EPS = 1e-08
ADAM_WD = 0.01
ADAM_STEP = 10

VMEM_LIMIT = 60 * 1024 * 1024


def _cp(*sem):
    return pltpu.CompilerParams(dimension_semantics=sem, vmem_limit_bytes=VMEM_LIMIT)


def _nn(a, b):
    return jnp.dot(a, b, preferred_element_type=F32)


def _nt(a, b):
    return lax.dot_general(a, b, (((1,), (1,)), ((), ())), preferred_element_type=F32)


def _tn(a, b):
    return lax.dot_general(a, b, (((0,), (0,)), ((), ())), preferred_element_type=F32)


ACT = BF16


def _bf(a):
    return a.astype(ACT)


def _sig(x):
    return jax.nn.sigmoid(x)


def _split3(x):
    h = x.astype(BF16)
    r = x - h.astype(F32)
    m = r.astype(BF16)
    lo = (r - m.astype(F32)).astype(BF16)
    return h, m, lo


def _nn3(m01, x):
    h, m, lo = _split3(x)
    return _nn(m01, h) + _nn(m01, m) + _nn(m01, lo)


def _nn3r(x, m01):
    h, m, lo = _split3(x)
    return _nn(h, m01) + _nn(m, m01) + _nn(lo, m01)


def _full(shape):
    nd = len(shape)
    return pl.BlockSpec(shape, lambda *a: (0,) * nd, pipeline_mode=pl.Buffered(1))


def _whole(shape):
    nd = len(shape)
    return pl.BlockSpec(shape, lambda *a: (0,) * nd)


def _rows(tm, width):
    return pl.BlockSpec((tm, width), lambda i: (i, 0))


def _cols(height, tm):
    return pl.BlockSpec((height, tm), lambda i: (0, i))


def _ctx_lat(width):
    return pl.BlockSpec((1, 1, width), lambda i: (jnp.minimum(i, 1), 0, 0))


def _acc_ctx_lat(ref, i, val):
    @pl.when(i <= 1)
    def _():
        ref[...] = val.reshape(ref.shape)

    @pl.when(i > 1)
    def _():
        ref[...] += val.reshape(ref.shape)


def _acc_all(ref, i, val):
    @pl.when(i == 0)
    def _():
        ref[...] = val.reshape(ref.shape)

    @pl.when(i > 0)
    def _():
        ref[...] += val.reshape(ref.shape)


def _tile(n, cap):
    best = None
    for t in range(128, min(n, cap) + 1, 128):
        if n % t == 0:
            best = t
    return n if best is None else best


def _norm_mod(xv, g, shift, scale):
    r = lax.rsqrt(jnp.mean(xv * xv, axis=-1, keepdims=True) + EPS)
    xhat = xv * r
    n = xhat * g
    return r, xhat, n, n * (1.0 + scale) + shift


def _norm_mod_bwd(dh, r, xhat, n, g, scale):
    dshift = jnp.sum(dh, axis=0, keepdims=True)
    dscale = jnp.sum(dh * n, axis=0, keepdims=True)
    dn = dh * (1.0 + scale)
    dg = jnp.sum(dn * xhat, axis=0, keepdims=True)
    dxh = dn * g
    dx = r * (dxh - xhat * jnp.mean(dxh * xhat, axis=-1, keepdims=True))
    return dx, dshift, dscale, dg


def _pre_fwd(x, gain, ms, w, splits, tm, name, ex=None, out_dtype=F32):
    T, dm = x.shape
    nt = T // tm

    def body(x_ref, g_ref, ms_ref, w_ref, *outs):
        ms_v = ms_ref[0]
        h = _norm_mod(x_ref[...], g_ref[...], ms_v[:, :dm], ms_v[:, dm:])[3]
        hb = _bf(h)
        for (s, e), o_ref in zip(splits, outs):
            o_ref[...] = _nn(hb, w_ref[:, s:e]).astype(o_ref.dtype)

    return _host_call(
        body, ex, lambda: pl.program_id(0) == 0, lambda: pl.program_id(0) == nt - 1,
        name=name, grid=(nt,),
        in_specs=[_rows(tm, dm), _full((1, dm)), _ctx_lat(2 * dm), _full(w.shape)],
        out_specs=[_rows(tm, e - s) for s, e in splits],
        out_shape=[jax.ShapeDtypeStruct((T, e - s), out_dtype) for s, e in splits],
        scratch_shapes=[], sem=("arbitrary",), args=(x, gain, ms, w))


def _pre_bwd(x, dx_in, gain, ms, w, pieces, tm, name, latent_dx=False, ex=None):
    T, dm = x.shape
    dx_spec = pl.BlockSpec((tm, dm), lambda i: (jnp.maximum(i - 1, 0), 0)) if latent_dx else _rows(tm, dm)
    dx_rows = T - tm if latent_dx else T
    n_out = w.shape[1]
    flat = [a for _, arrs in pieces for a in arrs]

    def body(x_ref, dxin_ref, g_ref, ms_ref, w_ref, *rest):
        p_refs = rest[:len(flat)]
        dx_ref, h_ref, dp_ref, dms_ref, dg_ref = rest[len(flat):]
        i = pl.program_id(0)
        ms_v = ms_ref[0]
        g = g_ref[...]
        scale = ms_v[:, dm:]
        r, xhat, n, h = _norm_mod(x_ref[...], g, ms_v[:, :dm], scale)
        h_ref[...] = _bf(h).T
        dh = jnp.zeros((tm, dm), F32)
        k = 0
        for s, arrs in pieces:
            v = p_refs[k][...].astype(F32)
            for j in range(1, len(arrs)):
                v = v + p_refs[k + j][...].astype(F32)
            k += len(arrs)
            vb = _bf(v)
            wd = vb.shape[1]
            dp_ref[:, s:s + wd] = vb
            dh = dh + _nt(vb, w_ref[:, s:s + wd])
        dx, dshift, dscale, dg = _norm_mod_bwd(dh, r, xhat, n, g, scale)
        dx_ref[...] = dxin_ref[...] + dx
        _acc_ctx_lat(dms_ref, i, jnp.concatenate([dshift, dscale], axis=1))
        _acc_all(dg_ref, i, dg)

    nt = T // tm
    return _host_call(
        body, ex, lambda: pl.program_id(0) == 0, lambda: pl.program_id(0) == nt - 1,
        name=name, grid=(nt,),
        in_specs=[_rows(tm, dm), _rows(tm, dm), _full((1, dm)), _ctx_lat(2 * dm), _full(w.shape)]
        + [_rows(tm, a.shape[1]) for a in flat],
        out_specs=[dx_spec, _cols(dm, tm), _rows(tm, n_out), _ctx_lat(2 * dm), _whole((1, dm))],
        out_shape=[jax.ShapeDtypeStruct((dx_rows, dm), F32), jax.ShapeDtypeStruct((dm, T), ACT),
                   jax.ShapeDtypeStruct((T, n_out), ACT), jax.ShapeDtypeStruct((2, 1, 2 * dm), F32),
                   jax.ShapeDtypeStruct((1, dm), F32)],
        scratch_shapes=[], sem=("arbitrary",), args=(x, dx_in, gain, ms, w, *flat))


def _ffn_fwd(x1, gain, ms, w_in, w_out, tm, name, target=None, ex=None):
    T, dm = x1.shape
    fh = w_out.shape[0]
    head = target is not None

    def body(*refs):
        if head:
            x_ref, g_ref, ms_ref, wi_ref, wo_ref, t_ref, x2_ref, u_ref, f_ref, loss_ref = refs
        else:
            x_ref, g_ref, ms_ref, wi_ref, wo_ref, x2_ref, u_ref, f_ref = refs
        ms_v = ms_ref[0]
        xv = x_ref[...]
        h = _norm_mod(xv, g_ref[...], ms_v[:, :dm], ms_v[:, dm:2 * dm])[3]
        u = _nn(_bf(h), wi_ref[...])
        u_ref[...] = _bf(u)
        gt = u[:, :fh]
        act = gt * _sig(gt) * u[:, fh:]
        f = _nn(_bf(act), wo_ref[...])
        f_ref[...] = _bf(f)
        x2 = xv + ms_v[:, 2 * dm:] * f
        if head:
            i = pl.program_id(0)
            e = x2 - t_ref[...]
            x2_ref[...] = jnp.where(i > 0, e * (1.0 / dm), 0.0)
            _acc_all(loss_ref, i, jnp.where(i > 0, jnp.sum(e * e) * (0.5 / dm), 0.0))
        else:
            x2_ref[...] = x2

    ins = [x1, gain, ms, w_in, w_out]
    in_specs = [_rows(tm, dm), _full((1, dm)), _ctx_lat(3 * dm), _full(w_in.shape), _full(w_out.shape)]
    out_specs = [_rows(tm, dm), _rows(tm, 2 * fh), _rows(tm, dm)]
    out_shape = [jax.ShapeDtypeStruct((T, dm), F32), jax.ShapeDtypeStruct((T, 2 * fh), ACT), jax.ShapeDtypeStruct((T, dm), ACT)]
    if head:
        ins.append(target)
        in_specs.append(pl.BlockSpec((tm, dm), lambda i: (jnp.maximum(i - 1, 0), 0)))
        out_specs.append(_whole((1, 1)))
        out_shape.append(jax.ShapeDtypeStruct((1, 1), F32))
    nt = T // tm
    return _host_call(
        body, ex, lambda: pl.program_id(0) == 0, lambda: pl.program_id(0) == nt - 1,
        name=name, grid=(nt,), in_specs=in_specs, out_specs=out_specs, out_shape=out_shape,
        scratch_shapes=[], sem=("arbitrary",), args=tuple(ins))


def _ffn_bwd(x1, dx2, u, f, gain, ms, w_in, w_out, tm, name, ex=None):
    T, dm = x1.shape
    fh = w_out.shape[0]

    def body(x_ref, dx2_ref, u_ref, f_ref, g_ref, ms_ref, wi_ref, wo_ref,
             dx1_ref, h_ref, du_ref, act_ref, df_ref, dms_ref, dg_ref):
        i = pl.program_id(0)
        ms_v = ms_ref[0]
        g = g_ref[...]
        scale = ms_v[:, dm:2 * dm]
        gate = ms_v[:, 2 * dm:]
        r, xhat, n, h = _norm_mod(x_ref[...], g, ms_v[:, :dm], scale)
        h_ref[...] = _bf(h).T
        dx2 = dx2_ref[...]
        dgate = jnp.sum(dx2 * f_ref[...].astype(F32), axis=0, keepdims=True)
        dfb = _bf(dx2 * gate)
        df_ref[...] = dfb
        da = _nt(dfb, wo_ref[...])
        uv = u_ref[...].astype(F32)
        gt = uv[:, :fh]
        up = uv[:, fh:]
        s = _sig(gt)
        sg = gt * s
        act_ref[...] = _bf(sg * up).T
        dgt = _bf(da * up * (s * (1.0 + gt * (1.0 - s))))
        dup = _bf(da * sg)
        du_ref[:, :fh] = dgt
        du_ref[:, fh:] = dup
        dh = _nt(dgt, wi_ref[:, :fh]) + _nt(dup, wi_ref[:, fh:])
        dx, dshift, dscale, dg = _norm_mod_bwd(dh, r, xhat, n, g, scale)
        dx1_ref[...] = dx2 + dx
        _acc_ctx_lat(dms_ref, i, jnp.concatenate([dshift, dscale, dgate], axis=1))
        _acc_all(dg_ref, i, dg)

    nt = T // tm
    return _host_call(
        body, ex, lambda: pl.program_id(0) == 0, lambda: pl.program_id(0) == nt - 1,
        name=name, grid=(nt,),
        in_specs=[_rows(tm, dm), _rows(tm, dm), _rows(tm, 2 * fh), _rows(tm, dm), _full((1, dm)), _ctx_lat(3 * dm),
                  _full(w_in.shape), _full(w_out.shape)],
        out_specs=[_rows(tm, dm), _cols(dm, tm), _rows(tm, 2 * fh), _cols(fh, tm), _rows(tm, dm),
                   _ctx_lat(3 * dm), _whole((1, dm))],
        out_shape=[jax.ShapeDtypeStruct((T, dm), F32), jax.ShapeDtypeStruct((dm, T), ACT),
                   jax.ShapeDtypeStruct((T, 2 * fh), ACT), jax.ShapeDtypeStruct((fh, T), ACT),
                   jax.ShapeDtypeStruct((T, dm), ACT), jax.ShapeDtypeStruct((2, 1, 3 * dm), F32),
                   jax.ShapeDtypeStruct((1, dm), F32)],
        scratch_shapes=[], sem=("arbitrary",), args=(x1, dx2, u, f, gain, ms, w_in, w_out))


def _wgrad(a_t, b, name, rows=None, ex=None):
    T = a_t.shape[1]
    r0, K = (0, a_t.shape[0]) if rows is None else rows
    N = b.shape[1]
    tk, tn, tt = _tile(K, 1408), _tile(N, 1664), _tile(T, 2816)
    nt = T // tt
    assert r0 % tk == 0
    off = r0 // tk
    nk, nn = K // tk, N // tn

    def body(a_ref, b_ref, o_ref, acc_ref):
        t = pl.program_id(2)
        part = _nn(a_ref[...], b_ref[...])

        @pl.when(t == 0)
        def _():
            acc_ref[...] = part

        @pl.when(t > 0)
        def _():
            acc_ref[...] += part

        @pl.when(t == nt - 1)
        def _():
            o_ref[...] = acc_ref[...].astype(o_ref.dtype)

    def at(i, j, t):
        return (pl.program_id(0) == i) & (pl.program_id(1) == j) & (pl.program_id(2) == t)

    outs, got = _host_call(
        body, ex, lambda: at(0, 0, 0), lambda: at(nk - 1, nn - 1, nt - 1),
        name=name, grid=(nk, nn, nt),
        in_specs=[pl.BlockSpec((tk, tt), lambda i, j, t: (i + off, t)), pl.BlockSpec((tt, tn), lambda i, j, t: (t, j))],
        out_specs=[pl.BlockSpec((tk, tn), lambda i, j, t: (i, j))],
        out_shape=[jax.ShapeDtypeStruct((K, N), ACT)],
        scratch_shapes=[pltpu.VMEM((tk, tn), F32)], sem=("arbitrary", "arbitrary", "arbitrary"), args=(a_t, b))
    return outs[0] if ex is None else (outs[0], got)


def _post_fwd(x, o_fw, o_bw, g_src, g_blk, gain, a, w_out, ms, dvh, tm, name):
    T, dm = x.shape
    hv = o_fw.shape[1]
    aw = 0 if a is None else a.shape[1]
    has_gain = gain is not None

    def body(*refs):
        refs = list(refs)
        x_ref, of_ref, ob_ref, g_ref = refs[:4]
        k = 4
        gain_ref = a_ref = None
        if has_gain:
            gain_ref = refs[k]
            k += 1
        if aw:
            a_ref = refs[k]
            k += 1
        w_ref, ms_ref, x1_ref, z_ref, yp_ref = refs[k:k + 5]
        o = of_ref[...].astype(F32) + ob_ref[...].astype(F32)
        gr = g_ref[...].astype(F32)
        if aw:
            z_ref[:, :aw] = _bf(a_ref[...])
        for hd in range(hv // dvh):
            sl = slice(hd * dvh, (hd + 1) * dvh)
            oh = o[:, sl]
            gh = gr[:, sl]
            r = lax.rsqrt(jnp.mean(oh * oh, axis=-1, keepdims=True) + EPS)
            y = oh * r
            if has_gain:
                y = y * gain_ref[...]
            y = y * (gh * _sig(gh))
            z_ref[:, aw + hd * dvh:aw + (hd + 1) * dvh] = _bf(y)
        yp = _nn(z_ref[...], w_ref[...])
        yp_ref[...] = _bf(yp)
        x1_ref[...] = x_ref[...] + ms_ref[0] * yp

    ins = [x, o_fw, o_bw, g_src]
    specs = [_rows(tm, dm), _rows(tm, hv), _rows(tm, hv), pl.BlockSpec((tm, hv), lambda i: (i, g_blk))]
    if has_gain:
        ins.append(gain)
        specs.append(_full(gain.shape))
    if aw:
        ins.append(a)
        specs.append(_rows(tm, aw))
    ins += [w_out, ms]
    specs += [_full(w_out.shape), _ctx_lat(dm)]
    return pl.pallas_call(
        body, name=name, grid=(T // tm,), in_specs=specs,
        out_specs=[_rows(tm, dm), _rows(tm, aw + hv), _rows(tm, dm)],
        out_shape=[jax.ShapeDtypeStruct((T, dm), F32), jax.ShapeDtypeStruct((T, aw + hv), ACT),
                   jax.ShapeDtypeStruct((T, dm), ACT)],
        compiler_params=_cp("arbitrary"),
    )(*ins)


def _post_bwd(dx1, z, yp, o_fw, o_bw, g_src, g_blk, gain, w_out, ms, aw, dvh, tm, name):
    T, dm = dx1.shape
    hv = o_fw.shape[1]
    has_gain = gain is not None

    def body(*refs):
        refs = list(refs)
        dx1_ref, z_ref, yp_ref, of_ref, ob_ref, g_ref = refs[:6]
        k = 6
        gain_ref = None
        if has_gain:
            gain_ref = refs[k]
            k += 1
        w_ref, ms_ref = refs[k:k + 2]
        k += 2
        do_ref, dgr_ref = refs[k:k + 2]
        k += 2
        da_ref = None
        if aw:
            da_ref = refs[k]
            k += 1
        dy_ref, zt_ref, dgate_ref, dgain_ref = refs[k:k + 4]
        i = pl.program_id(0)
        dx1v = dx1_ref[...]
        zt_ref[...] = z_ref[...].T
        _acc_ctx_lat(dgate_ref, i, jnp.sum(dx1v * yp_ref[...].astype(F32), axis=0, keepdims=True))
        dyb = _bf(dx1v * ms_ref[0])
        dy_ref[...] = dyb
        dz = _nt(dyb, w_ref[...])
        if aw:
            da_ref[...] = dz[:, :aw]
        o = of_ref[...].astype(F32) + ob_ref[...].astype(F32)
        gr = g_ref[...].astype(F32)
        dgain = jnp.zeros((1, dvh), F32)
        for hd in range(hv // dvh):
            sl = slice(hd * dvh, (hd + 1) * dvh)
            oh = o[:, sl]
            gh = gr[:, sl]
            dyh = dz[:, aw + hd * dvh:aw + (hd + 1) * dvh]
            r = lax.rsqrt(jnp.mean(oh * oh, axis=-1, keepdims=True) + EPS)
            n = oh * r
            s = _sig(gh)
            sl_g = gh * s
            gn = gain_ref[...] if has_gain else 1.0
            dgr_ref[:, sl] = _bf(dyh * n * gn * (s * (1.0 + gh * (1.0 - s))))
            dn = dyh * gn * sl_g
            dgain = dgain + jnp.sum(dyh * n * sl_g, axis=0, keepdims=True)
            do_ref[:, sl] = _bf(r * (dn - n * jnp.mean(dn * n, axis=-1, keepdims=True)))
        _acc_all(dgain_ref, i, dgain)

    ins = [dx1, z, yp, o_fw, o_bw, g_src]
    specs = [_rows(tm, dm), _rows(tm, aw + hv), _rows(tm, dm), _rows(tm, hv), _rows(tm, hv),
             pl.BlockSpec((tm, hv), lambda i: (i, g_blk))]
    if has_gain:
        ins.append(gain)
        specs.append(_full(gain.shape))
    ins += [w_out, ms]
    specs += [_full(w_out.shape), _ctx_lat(dm)]
    out_specs = [_rows(tm, hv), _rows(tm, hv)]
    out_shape = [jax.ShapeDtypeStruct((T, hv), ACT), jax.ShapeDtypeStruct((T, hv), ACT)]
    if aw:
        out_specs.append(_rows(tm, aw))
        out_shape.append(jax.ShapeDtypeStruct((T, aw), F32))
    out_specs += [_rows(tm, dm), _cols(aw + hv, tm), _ctx_lat(dm), _whole((1, dvh))]
    out_shape += [jax.ShapeDtypeStruct((T, dm), ACT), jax.ShapeDtypeStruct((aw + hv, T), ACT),
                  jax.ShapeDtypeStruct((2, 1, dm), F32), jax.ShapeDtypeStruct((1, dvh), F32)]
    return pl.pallas_call(
        body, name=name, grid=(T // tm,), in_specs=specs, out_specs=out_specs, out_shape=out_shape,
        compiler_params=_cp("arbitrary"),
    )(*ins)


def _loss_bwd(x, target, tm, name):
    T, dm = x.shape

    def body(x_ref, t_ref, dx_ref, loss_ref):
        i = pl.program_id(0)

        @pl.when(i == 0)
        def _():
            dx_ref[...] = jnp.zeros_like(dx_ref)
            loss_ref[...] = jnp.zeros_like(loss_ref)

        @pl.when(i > 0)
        def _():
            e = x_ref[...] - t_ref[...]
            dx_ref[...] = e * (1.0 / dm)
            loss_ref[...] += jnp.sum(e * e) * (0.5 / dm)

    return pl.pallas_call(
        body, name=name, grid=(T // tm,),
        in_specs=[_rows(tm, dm), pl.BlockSpec((tm, dm), lambda i: (jnp.maximum(i - 1, 0), 0))],
        out_specs=[_rows(tm, dm), _whole((1, 1))],
        out_shape=[jax.ShapeDtypeStruct((T, dm), F32), jax.ShapeDtypeStruct((1, 1), F32)],
        compiler_params=_cp("arbitrary"),
    )(x, target)


def _swap_matrix():
    r = lax.broadcasted_iota(jnp.int32, (HEAD_DIM, HEAD_DIM), 0)
    c = lax.broadcasted_iota(jnp.int32, (HEAD_DIM, HEAD_DIM), 1)
    return jnp.where((r + HEAD_DIM // 2) % HEAD_DIM == c, 1.0, 0.0).astype(BF16)


def _qk_prep_fwd(raw, gains, cos2, sin2, tq, name):
    nh, T, hd = raw.shape

    def body(x_ref, g_ref, c_ref, s_ref, o_ref):
        hidx = pl.program_id(0)
        xv = x_ref[0]
        r = lax.rsqrt(jnp.mean(xv * xv, axis=-1, keepdims=True) + EPS)
        n = xv * r * g_ref[0]
        y = n * c_ref[...] + _nn3r(n, _swap_matrix()) * s_ref[...]
        sc = jnp.where(hidx < ATTN_HEADS, HEAD_DIM ** -0.5, 1.0)
        o_ref[0] = _bf(y * sc)

    return pl.pallas_call(
        body, name=name, grid=(nh, T // tq),
        in_specs=[pl.BlockSpec((1, tq, hd), lambda h, i: (h, i, 0)), pl.BlockSpec((1, 1, hd), lambda h, i: (h, 0, 0)),
                  pl.BlockSpec((tq, hd), lambda h, i: (i, 0)), pl.BlockSpec((tq, hd), lambda h, i: (i, 0))],
        out_specs=pl.BlockSpec((1, tq, hd), lambda h, i: (h, i, 0)),
        out_shape=jax.ShapeDtypeStruct((nh, T, hd), ACT),
        compiler_params=_cp("arbitrary", "arbitrary"),
    )(raw, gains, cos2, sin2)


def _qk_prep_bwd(dy, raw, gains, cos2, sin2, tq, name):
    nh, T, hd = raw.shape

    def body(dy_ref, x_ref, g_ref, c_ref, s_ref, dx_ref, dg_ref):
        hidx = pl.program_id(0)
        i = pl.program_id(1)
        xv = x_ref[0]
        g = g_ref[0]
        r = lax.rsqrt(jnp.mean(xv * xv, axis=-1, keepdims=True) + EPS)
        xhat = xv * r
        sc = jnp.where(hidx < ATTN_HEADS, HEAD_DIM ** -0.5, 1.0)
        dyv = dy_ref[0] * sc
        dn = dyv * c_ref[...] + _nn3r(dyv * s_ref[...], _swap_matrix())
        _acc_all(dg_ref, i, jnp.sum(dn * xhat, axis=0, keepdims=True))
        dxh = dn * g
        dx_ref[0] = r * (dxh - xhat * jnp.mean(dxh * xhat, axis=-1, keepdims=True))

    return pl.pallas_call(
        body, name=name, grid=(nh, T // tq),
        in_specs=[pl.BlockSpec((1, tq, hd), lambda h, i: (h, i, 0)), pl.BlockSpec((1, tq, hd), lambda h, i: (h, i, 0)),
                  pl.BlockSpec((1, 1, hd), lambda h, i: (h, 0, 0)),
                  pl.BlockSpec((tq, hd), lambda h, i: (i, 0)), pl.BlockSpec((tq, hd), lambda h, i: (i, 0))],
        out_specs=[pl.BlockSpec((1, tq, hd), lambda h, i: (h, i, 0)), pl.BlockSpec((1, 1, hd), lambda h, i: (h, 0, 0))],
        out_shape=[jax.ShapeDtypeStruct((nh, T, hd), F32), jax.ShapeDtypeStruct((nh, 1, hd), F32)],
        compiler_params=_cp("arbitrary", "arbitrary"),
    )(dy, raw, gains, cos2, sin2)


def _attn_scores(q, k_ref, i, lc, T, sink):
    blk = ATTN_BLOCK
    kc = k_ref[0, pl.ds(blk, lc), :]
    kw = k_ref[0, pl.ds(pl.multiple_of(i * blk, blk), 3 * blk), :]
    s_c = _nt(q, kc)
    s_w = _nt(q, kw)
    row = lax.broadcasted_iota(jnp.int32, (4 * blk, 1), 0)
    qpos = i * blk + (row & (blk - 1))
    kpos = (i - 1) * blk + lax.broadcasted_iota(jnp.int32, (1, 3 * blk), 1)
    valid = (qpos >= lc) & (kpos >= lc) & (kpos < T) & (jnp.abs(kpos - qpos) <= WINDOW)
    s_w = jnp.where(valid, s_w, NEG)
    return kc, kw, s_c, s_w


def _attn_fwd(qt, kp, vp, sinkb, lc, name, ex=None):
    nh, T, hd = qt.shape
    blk = ATTN_BLOCK
    g = nh // ATTN_KV

    def body(q_ref, k_ref, v_ref, sink_ref, o_ref, lse_ref):
        i = pl.program_id(1)
        q = q_ref[...].reshape(g * blk, hd)
        sink = sink_ref[0]
        kc, kw, s_c, s_w = _attn_scores(q, k_ref, i, lc, T, sink)
        m = jnp.maximum(jnp.maximum(jnp.max(s_c, axis=-1, keepdims=True), jnp.max(s_w, axis=-1, keepdims=True)), sink)
        e_c = jnp.exp(s_c - m)
        e_w = jnp.exp(s_w - m)
        den = jnp.exp(sink - m) + jnp.sum(e_c, axis=-1, keepdims=True) + jnp.sum(e_w, axis=-1, keepdims=True)
        inv = 1.0 / den
        vc = v_ref[0, pl.ds(blk, lc), :]
        vw = v_ref[0, pl.ds(pl.multiple_of(i * blk, blk), 3 * blk), :]
        o = _nn(_bf(e_c * inv), vc) + _nn(_bf(e_w * inv), vw)
        o_ref[...] = o.reshape(g, blk, hd)
        lse_ref[...] = (m + jnp.log(den)).reshape(g, blk, 1)

    nb = T // blk
    return _host_call(
        body, ex, lambda: (pl.program_id(0) == 0) & (pl.program_id(1) == 0),
        lambda: (pl.program_id(0) == ATTN_KV - 1) & (pl.program_id(1) == nb - 1),
        name=name, grid=(ATTN_KV, nb),
        in_specs=[pl.BlockSpec((g, blk, hd), lambda kv, i: (kv, i, 0)),
                  pl.BlockSpec((1, T + 2 * blk, hd), lambda kv, i: (kv, 0, 0)),
                  pl.BlockSpec((1, T + 2 * blk, hd), lambda kv, i: (kv, 0, 0)),
                  pl.BlockSpec((1, g * blk, 1), lambda kv, i: (kv, 0, 0))],
        out_specs=[pl.BlockSpec((g, blk, hd), lambda kv, i: (kv, i, 0)),
                   pl.BlockSpec((g, blk, 1), lambda kv, i: (kv, i, 0))],
        out_shape=[jax.ShapeDtypeStruct((nh, T, hd), F32), jax.ShapeDtypeStruct((nh, T, 1), F32)],
        scratch_shapes=[], sem=("arbitrary", "arbitrary"), args=(qt, kp, vp, sinkb))


def _attn_bwd(qt, kp, vp, sinkb, o, lse, do, lc, name):
    nh, T, hd = qt.shape
    blk = ATTN_BLOCK
    g = nh // ATTN_KV

    def body(q_ref, k_ref, v_ref, sink_ref, o_ref, lse_ref, do_ref, dq_ref, dk_ref, dv_ref, ds_ref):
        i = pl.program_id(1)

        @pl.when(i == 0)
        def _():
            dk_ref[...] = jnp.zeros_like(dk_ref)
            dv_ref[...] = jnp.zeros_like(dv_ref)
            ds_ref[...] = jnp.zeros_like(ds_ref)

        q = q_ref[...].reshape(g * blk, hd)
        sink = sink_ref[0]
        lse = lse_ref[...].reshape(g * blk, 1)
        dov = do_ref[...].reshape(g * blk, hd)
        delta = jnp.sum(dov * o_ref[...].reshape(g * blk, hd), axis=-1, keepdims=True)
        kc, kw, s_c, s_w = _attn_scores(q, k_ref, i, lc, T, sink)
        p_c = jnp.exp(s_c - lse)
        p_w = jnp.exp(s_w - lse)
        win = pl.ds(pl.multiple_of(i * blk, blk), 3 * blk)
        vc = v_ref[0, pl.ds(blk, lc), :]
        vw = v_ref[0, win, :]
        dob = _bf(dov)
        ds_c = _bf(p_c * (_nt(dob, vc) - delta))
        ds_w = _bf(p_w * (_nt(dob, vw) - delta))
        dsr = -jnp.exp(sink - lse) * delta
        for hh in range(g):
            ds_ref[0, hh:hh + 1, :] += jnp.sum(dsr[hh * blk:(hh + 1) * blk, :], axis=0, keepdims=True)
        dq_ref[...] = (_nn(ds_c, kc) + _nn(ds_w, kw)).reshape(g, blk, hd)
        dk_ref[0, pl.ds(blk, lc), :] += _tn(ds_c, q)
        dk_ref[0, win, :] += _tn(ds_w, q)
        dv_ref[0, pl.ds(blk, lc), :] += _tn(_bf(p_c), dob)
        dv_ref[0, win, :] += _tn(_bf(p_w), dob)

    qspec = pl.BlockSpec((g, blk, hd), lambda kv, i: (kv, i, 0))
    kspec = pl.BlockSpec((1, T + 2 * blk, hd), lambda kv, i: (kv, 0, 0))
    lspec = pl.BlockSpec((g, blk, 1), lambda kv, i: (kv, i, 0))
    return pl.pallas_call(
        body, name=name, grid=(ATTN_KV, T // blk),
        in_specs=[qspec, kspec, kspec, pl.BlockSpec((1, g * blk, 1), lambda kv, i: (kv, 0, 0)), qspec, lspec, qspec],
        out_specs=[qspec, kspec, kspec, pl.BlockSpec((1, g, 1), lambda kv, i: (kv, 0, 0))],
        out_shape=[jax.ShapeDtypeStruct((nh, T, hd), F32), jax.ShapeDtypeStruct((ATTN_KV, T + 2 * blk, hd), F32),
                   jax.ShapeDtypeStruct((ATTN_KV, T + 2 * blk, hd), F32), jax.ShapeDtypeStruct((ATTN_KV, g, 1), F32)],
        compiler_params=_cp("arbitrary", "arbitrary"),
    )(qt, kp, vp, sinkb, o, lse, do)


PAIR = 2 * HEAD_DIM
N_PAIRS = (ATTN_HEADS + ATTN_KV) // 2


def _lanes():
    return lax.broadcasted_iota(jnp.int32, (1, PAIR), 1)


def _swap32(v):
    first_half = (_lanes() & (HEAD_DIM // 2)) == 0
    return jnp.where(first_half, pltpu.roll(v, PAIR - HEAD_DIM // 2, 1), pltpu.roll(v, HEAD_DIM // 2, 1))


def _head_mean(v):
    r = lax.broadcasted_iota(jnp.int32, (PAIR, PAIR), 0)
    c = lax.broadcasted_iota(jnp.int32, (PAIR, PAIR), 1)
    same = jnp.where((r >= HEAD_DIM) == (c >= HEAD_DIM), 1.0, 0.0).astype(BF16)
    return _nn3r(v, same) * (1.0 / HEAD_DIM)


def _qk_slab_fwd(pa, gains, cosp, sinp, tm, name):
    T = pa.shape[0]
    qw = ATTN_HEADS * HEAD_DIM

    def body(pa_ref, g_ref, c_ref, s_ref, q_ref, k_ref, v_ref):
        cosv, sinv = c_ref[...], s_ref[...]
        for p in range(N_PAIRS):
            xv = pa_ref[:, p * PAIR:(p + 1) * PAIR]
            n = xv * lax.rsqrt(_head_mean(xv * xv) + EPS) * g_ref[p]
            y = n * cosv + _swap32(n) * sinv
            if p < N_PAIRS - 1:
                q_ref[:, p * PAIR:(p + 1) * PAIR] = _bf(y * HEAD_DIM ** -0.5)
            else:
                k_ref[...] = _bf(y)
        v_ref[...] = _bf(pa_ref[:, qw + PAIR:])

    return pl.pallas_call(
        body, name=name, grid=(T // tm,),
        in_specs=[_rows(tm, pa.shape[1]), _full(gains.shape), _rows(tm, PAIR), _rows(tm, PAIR)],
        out_specs=[_rows(tm, qw), _rows(tm, PAIR), _rows(tm, PAIR)],
        out_shape=[jax.ShapeDtypeStruct((T, qw), ACT), jax.ShapeDtypeStruct((T, PAIR), ACT),
                   jax.ShapeDtypeStruct((T, PAIR), ACT)],
        compiler_params=_cp("arbitrary"),
    )(pa, gains, cosp, sinp)


def _qk_slab_bwd(dq, dk, pa, gains, cosp, sinp, tm, name):
    T = pa.shape[0]
    qw = ATTN_HEADS * HEAD_DIM

    def body(dq_ref, dk_ref, pa_ref, g_ref, c_ref, s_ref, dx_ref, dg_ref):
        i = pl.program_id(0)
        cosv, sinv = c_ref[...], s_ref[...]
        for p in range(N_PAIRS):
            sl = slice(p * PAIR, (p + 1) * PAIR)
            xv = pa_ref[:, sl]
            r = lax.rsqrt(_head_mean(xv * xv) + EPS)
            xhat = xv * r
            dy = dq_ref[:, sl] * HEAD_DIM ** -0.5 if p < N_PAIRS - 1 else dk_ref[...]
            dn = dy * cosv + _swap32(dy * sinv)
            _acc_all(dg_ref.at[p], i, jnp.sum(dn * xhat, axis=0, keepdims=True))
            dxh = dn * g_ref[p]
            dx_ref[:, sl] = r * (dxh - xhat * _head_mean(dxh * xhat))

    return pl.pallas_call(
        body, name=name, grid=(T // tm,),
        in_specs=[_rows(tm, qw), _rows(tm, PAIR), _rows(tm, qw + PAIR), _full(gains.shape), _rows(tm, PAIR), _rows(tm, PAIR)],
        out_specs=[_rows(tm, qw + PAIR), _whole(gains.shape)],
        out_shape=[jax.ShapeDtypeStruct((T, qw + PAIR), F32), jax.ShapeDtypeStruct(gains.shape, F32)],
        compiler_params=_cp("arbitrary"),
    )(dq, dk, pa, gains, cosp, sinp)


def _attn_window(ref, i, nb):
    blk = ATTN_BLOCK
    starts = [pl.multiple_of(jnp.clip(i + d, 0, nb - 1) * blk, blk) for d in (-1, 0, 1)]
    return starts, jnp.concatenate([ref[pl.ds(s, blk), :] for s in starts], axis=0)


GROUP_HEADS = 2


def _head_groups(n):
    g = ATTN_HEADS // ATTN_KV
    return [(kv, [kv * g + s + j for j in range(n)]) for kv in range(ATTN_KV) for s in range(0, g, n)]


def _attn_mask(i, lc, T, rows):
    blk = ATTN_BLOCK
    row = lax.broadcasted_iota(jnp.int32, (rows, 1), 0)
    qpos = i * blk + (row & (blk - 1))
    kpos = (i - 1) * blk + lax.broadcasted_iota(jnp.int32, (1, 3 * blk), 1)
    return (qpos >= lc) & (kpos >= lc) & (kpos < T) & (jnp.abs(kpos - qpos) <= WINDOW)


def _to_kv_half(v, head, kv):
    return v if head % 2 == kv else pltpu.roll(v, HEAD_DIM, 1)


def _attn_slab_fwd(qt, ks, vs, sinkb, lc, name, ex=None):
    T = qt.shape[0]
    blk = ATTN_BLOCK
    nb = T // blk
    g = ATTN_HEADS // ATTN_KV

    def body(q_ref, k_ref, v_ref, sink_ref, o_ref, lse_ref):
        i = pl.program_id(0)
        lane = _lanes()
        valid = _attn_mask(i, lc, T, GROUP_HEADS * blk)
        kc_all, vc = k_ref[0:lc, :], v_ref[0:lc, :]
        _, kw_all = _attn_window(k_ref, i, nb)
        _, vw = _attn_window(v_ref, i, nb)
        kc, kw = [], []
        for kv in range(ATTN_KV):
            mine = (lane >= kv * HEAD_DIM) & (lane < (kv + 1) * HEAD_DIM)
            kc.append(jnp.where(mine, kc_all, jnp.zeros_like(kc_all)))
            kw.append(jnp.where(mine, kw_all, jnp.zeros_like(kw_all)))
        groups = _head_groups(GROUP_HEADS)
        qg = [jnp.concatenate([_to_kv_half(q_ref[:, (h // 2) * PAIR:(h // 2 + 1) * PAIR], h, kv) for h in heads], axis=0)
              for kv, heads in groups]
        sinks = [sink_ref[kv, (heads[0] - kv * g) * blk:(heads[-1] + 1 - kv * g) * blk] for kv, heads in groups]
        s_c = [_nt(q, kc[kv]) for q, (kv, _) in zip(qg, groups)]
        s_w = [jnp.where(valid, _nt(q, kw[kv]), NEG) for q, (kv, _) in zip(qg, groups)]
        m = [jnp.maximum(jnp.maximum(jnp.max(a, axis=-1, keepdims=True), jnp.max(b, axis=-1, keepdims=True)), s)
             for a, b, s in zip(s_c, s_w, sinks)]
        e_c = [jnp.exp(a - mm) for a, mm in zip(s_c, m)]
        e_w = [jnp.exp(b - mm) for b, mm in zip(s_w, m)]
        den = [jnp.exp(s - mm) + jnp.sum(a, axis=-1, keepdims=True) + jnp.sum(b, axis=-1, keepdims=True)
               for s, mm, a, b in zip(sinks, m, e_c, e_w)]
        inv = [1.0 / d for d in den]
        og = [_nn(_bf(a * r), vc) + _nn(_bf(b * r), vw) for a, b, r in zip(e_c, e_w, inv)]
        placed = [None] * ATTN_HEADS
        for (kv, heads), o2, mm, d in zip(groups, og, m, den):
            lse_ref[heads[0]:heads[-1] + 1] = (mm + jnp.log(d)).reshape(len(heads), blk, 1)
            for j, h in enumerate(heads):
                placed[h] = _to_kv_half(o2[j * blk:(j + 1) * blk], h, kv)
        for p in range(ATTN_HEADS // 2):
            o_ref[:, p * PAIR:(p + 1) * PAIR] = jnp.where(lane < HEAD_DIM, placed[2 * p], placed[2 * p + 1])

    qw = ATTN_HEADS * HEAD_DIM
    return _host_call(
        body, ex, lambda: pl.program_id(0) == 0, lambda: pl.program_id(0) == nb - 1,
        name=name, grid=(nb,),
        in_specs=[_rows(blk, qw), _full((T, PAIR)), _full((T, PAIR)), _full(sinkb.shape)],
        out_specs=[_rows(blk, qw), pl.BlockSpec((ATTN_HEADS, blk, 1), lambda i: (0, i, 0))],
        out_shape=[jax.ShapeDtypeStruct((T, qw), F32), jax.ShapeDtypeStruct((ATTN_HEADS, T, 1), F32)],
        scratch_shapes=[], sem=("arbitrary",), args=(qt, ks, vs, sinkb))


def _attn_slab_bwd(qt, ks, vs, sinkb, o, lse, do, lc, name, ex=None):
    T = qt.shape[0]
    blk = ATTN_BLOCK
    nb = T // blk
    g = ATTN_HEADS // ATTN_KV

    def body(q_ref, k_ref, v_ref, sink_ref, o_ref, lse_ref, do_ref, dq_ref, dk_ref, dv_ref, ds_ref):
        i = pl.program_id(0)

        @pl.when(i == 0)
        def _():
            dk_ref[...] = jnp.zeros_like(dk_ref)
            dv_ref[...] = jnp.zeros_like(dv_ref)
            ds_ref[...] = jnp.zeros_like(ds_ref)

        lane = _lanes()
        valid = _attn_mask(i, lc, T, g * blk)
        kc_all, vc_all = k_ref[0:lc, :], v_ref[0:lc, :]
        starts, kw_all = _attn_window(k_ref, i, nb)
        _, vw_all = _attn_window(v_ref, i, nb)
        dq_pairs = [jnp.zeros((blk, PAIR), F32) for _ in range(ATTN_HEADS // 2)]
        for kv in range(ATTN_KV):
            mine = (lane >= kv * HEAD_DIM) & (lane < (kv + 1) * HEAD_DIM)

            def only(v):
                return jnp.where(mine, v, jnp.zeros_like(v))

            kc, kw, vc, vw = only(kc_all), only(kw_all), only(vc_all), only(vw_all)
            heads = [kv * g + j for j in range(g)]
            qs, dos, deltas = [], [], []
            for h in heads:
                sl = slice((h // 2) * PAIR, (h // 2 + 1) * PAIR)
                dov = do_ref[:, sl]
                qs.append(_to_kv_half(q_ref[:, sl], h, kv))
                dos.append(_bf(_to_kv_half(dov, h, kv)))
                own = (lane < HEAD_DIM) if h % 2 == 0 else (lane >= HEAD_DIM)
                deltas.append(jnp.sum(jnp.where(own, dov * o_ref[:, sl], 0.0), axis=-1, keepdims=True))
            q4, do4, delta = jnp.concatenate(qs, axis=0), jnp.concatenate(dos, axis=0), jnp.concatenate(deltas, axis=0)
            sink = sink_ref[kv]
            lse = lse_ref[kv * g:(kv + 1) * g].reshape(g * blk, 1)
            p_c = jnp.exp(_nt(q4, kc) - lse)
            p_w = jnp.exp(jnp.where(valid, _nt(q4, kw), NEG) - lse)
            ds_c = _bf(p_c * (_nt(do4, vc) - delta))
            ds_w = _bf(p_w * (_nt(do4, vw) - delta))
            dsr = -jnp.exp(sink - lse) * delta
            dq4 = _nn(ds_c, kc) + _nn(ds_w, kw)
            for j, h in enumerate(heads):
                ds_ref[h:h + 1, :] += jnp.sum(dsr[j * blk:(j + 1) * blk, :], axis=0, keepdims=True)
                dq_pairs[h // 2] = dq_pairs[h // 2] + _to_kv_half(dq4[j * blk:(j + 1) * blk], h, kv)
            dk_ref[0:lc, :] += only(_tn(ds_c, q4))
            dv_ref[0:lc, :] += only(_tn(_bf(p_c), do4))
            dkw = only(_tn(ds_w, q4))
            dvw = only(_tn(_bf(p_w), do4))
            for b, s in enumerate(starts):
                dk_ref[pl.ds(s, blk), :] += dkw[b * blk:(b + 1) * blk]
                dv_ref[pl.ds(s, blk), :] += dvw[b * blk:(b + 1) * blk]
        for p in range(ATTN_HEADS // 2):
            dq_ref[:, p * PAIR:(p + 1) * PAIR] = dq_pairs[p]

    qw = ATTN_HEADS * HEAD_DIM
    lspec = pl.BlockSpec((ATTN_HEADS, blk, 1), lambda i: (0, i, 0))
    return _host_call(
        body, ex, lambda: pl.program_id(0) == 0, lambda: pl.program_id(0) == nb - 1,
        name=name, grid=(nb,),
        in_specs=[_rows(blk, qw), _full((T, PAIR)), _full((T, PAIR)), _full(sinkb.shape), _rows(blk, qw), lspec,
                  _rows(blk, qw)],
        out_specs=[_rows(blk, qw), _whole((T, PAIR)), _whole((T, PAIR)), _whole((ATTN_HEADS, 1))],
        out_shape=[jax.ShapeDtypeStruct((T, qw), F32), jax.ShapeDtypeStruct((T, PAIR), F32),
                   jax.ShapeDtypeStruct((T, PAIR), F32), jax.ShapeDtypeStruct((ATTN_HEADS, 1), F32)],
        scratch_shapes=[], sem=("arbitrary",), args=(qt, ks, vs, sinkb, o, lse, do))


def _fw_chunk(s, nc, nt):
    return s


def _bw_chunk(s, nc, nt):
    return jnp.where(s < nc, nc - 1 - s, nt - 1 - (s - nc))


def _tri(c, rev):
    r = lax.broadcasted_iota(jnp.int32, (c, c), 0)
    k = lax.broadcasted_iota(jnp.int32, (c, c), 1)
    return (k >= r) if rev else (k <= r)


def _gla_gates(z, lb, rev):
    c = HG_CHUNK
    sg = _sig(z)
    f = lb + (1.0 - lb) * sg
    cum = _nn3(jnp.where(_tri(c, rev), 1.0, 0.0).astype(BF16), jnp.log(f))
    mid = c - 1 - c // 2 if rev else c // 2
    last = 0 if rev else c - 1
    return sg, f, cum, cum[mid:mid + 1], cum[last:last + 1], last


def _lower_bound(lbraw_ref):
    lr = lbraw_ref[...]
    return _sig(lr[0:1] - lr[1:2])


def _gla_fwd(pb, lbraw, lc, name, ex=None):
    T = pb.shape[0]
    c, hw, d = HG_CHUNK, HG_HEADS * HG_D, HG_D
    nt, nc = T // c, lc // c
    orders = (_fw_chunk, _bw_chunk)

    def body(qf, zf, vf, qb, zb, vb, lb_ref, of_ref, ob_ref, sf_ref, sb_ref, st_ref):
        @pl.when(pl.program_id(0) == 0)
        def _():
            st_ref[...] = jnp.zeros_like(st_ref)

        lb = _lower_bound(lb_ref)
        dirs = ((qf, zf, vf, of_ref, sf_ref), (qb, zb, vb, ob_ref, sb_ref))
        combos = [(dr, h, slice(h * d, (h + 1) * d)) for dr in range(2) for h in range(HG_HEADS)]
        prep = []
        for dr, (q_ref, z_ref, v_ref, _, _) in enumerate(dirs):
            rev = dr == 1
            qr = q_ref[...]
            q = qr * _sig(qr)
            _, f, cum, ref, last, _ = _gla_gates(z_ref[...], lb, rev)
            k = 1.0 - f
            prep.append(dict(q1=_bf(q * jnp.exp(cum - ref)), k1=_bf(k * jnp.exp(ref - cum)), q2=_bf(q * jnp.exp(cum)),
                             k2=_bf(k * jnp.exp(last - cum)), el=jnp.exp(last), v=_bf(v_ref[...]), mask=_tri(c, rev)))
        a = [_bf(jnp.where(prep[dr]["mask"], _nt(prep[dr]["q1"][:, sl], prep[dr]["k1"][:, sl]), 0.0)) for dr, _, sl in combos]
        for (dr, h, sl), a_h in zip(combos, a):
            p = prep[dr]
            o_ref, s_ref = dirs[dr][3], dirs[dr][4]
            st = st_ref[dr, h]
            stb = _bf(st)
            s_ref[0, h] = stb
            o_ref[:, sl] = _nn(a_h, p["v"][:, sl]) + _nt(p["q2"][:, sl], stb)
            st_ref[dr, h] = st * p["el"][:, sl] + _tn(p["v"][:, sl], p["k2"][:, sl])

    def col(order, blkcol):
        return pl.BlockSpec((c, hw), lambda s: (order(s, nc, nt), blkcol))

    def st_spec(order):
        return pl.BlockSpec((1, HG_HEADS, d, d), lambda s: (order(s, nc, nt), 0, 0, 0))

    in_specs = []
    for dr, order in enumerate(orders):
        in_specs += [col(order, 0), col(order, 1 + dr), col(order, 3)]
    in_specs.append(_full(lbraw.shape))
    return _host_call(
        body, ex, lambda: pl.program_id(0) == 0, lambda: pl.program_id(0) == nt - 1,
        name=name, grid=(nt,), in_specs=in_specs,
        out_specs=[col(_fw_chunk, 0), col(_bw_chunk, 0), st_spec(_fw_chunk), st_spec(_bw_chunk)],
        out_shape=[jax.ShapeDtypeStruct((T, hw), F32), jax.ShapeDtypeStruct((T, hw), F32),
                   jax.ShapeDtypeStruct((nt, HG_HEADS, d, d), ACT), jax.ShapeDtypeStruct((nt, HG_HEADS, d, d), ACT)],
        scratch_shapes=[pltpu.VMEM((2, HG_HEADS, d, d), F32)], sem=("arbitrary",),
        args=(pb, pb, pb, pb, pb, pb, lbraw))


def _gla_bwd(pb, lbraw, s_fw, s_bw, do, lc, name, ex=None):
    T = pb.shape[0]
    c, hw, d = HG_CHUNK, HG_HEADS * HG_D, HG_D
    nt, nc = T // c, lc // c

    def rfw(s, nc_, nt_):
        return _fw_chunk(nt_ - 1 - s, nc_, nt_)

    def rbw(s, nc_, nt_):
        return _bw_chunk(nt_ - 1 - s, nc_, nt_)

    def body(qf, zf, vf, sf, dof, qb, zb, vb, sb, dob_, lb_ref,
             dqf, dzf, dvf, dqb, dzb, dvb, dlb_ref, dst_ref):
        step = pl.program_id(0)

        @pl.when(step == 0)
        def _():
            dst_ref[...] = jnp.zeros_like(dst_ref)

        lb = _lower_bound(lb_ref)
        sets = ((qf, zf, vf, sf, dof, dqf, dzf, dvf), (qb, zb, vb, sb, dob_, dqb, dzb, dvb))
        combos = [(dr, h, slice(h * d, (h + 1) * d)) for dr in range(2) for h in range(HG_HEADS)]
        prep = []
        for dr, (q_ref, z_ref, v_ref, _, do_ref, _, _, _) in enumerate(sets):
            rev = dr == 1
            qr = q_ref[...]
            sq = _sig(qr)
            q = qr * sq
            sg, f, cum, ref, last, last_row = _gla_gates(z_ref[...], lb, rev)
            k = 1.0 - f
            e_qr, e_kr, e_q, e_kl = jnp.exp(cum - ref), jnp.exp(ref - cum), jnp.exp(cum), jnp.exp(last - cum)
            q1, k1, q2, k2 = q * e_qr, k * e_kr, q * e_q, k * e_kl
            prep.append(dict(qr=qr, sq=sq, sg=sg, f=f, e_qr=e_qr, e_kr=e_kr, e_q=e_q, e_kl=e_kl, el=jnp.exp(last),
                             q1=q1, k1=k1, q2=q2, k2=k2, q1b=_bf(q1), k1b=_bf(k1), q2b=_bf(q2), k2b=_bf(k2),
                             vb=_bf(v_ref[...]), dob=_bf(do_ref[...]), mask=_tri(c, rev), last_row=last_row,
                             acc_t=jnp.where(_tri(c, not rev), 1.0, 0.0).astype(BF16)))
        a = [_bf(jnp.where(prep[dr]["mask"], _nt(prep[dr]["q1b"][:, sl], prep[dr]["k1b"][:, sl]), 0.0)) for dr, _, sl in combos]
        da = [_bf(jnp.where(prep[dr]["mask"], _nt(prep[dr]["dob"][:, sl], prep[dr]["vb"][:, sl]), 0.0)) for dr, _, sl in combos]
        parts = [dict(dq1=[], dk1=[], dq2=[], dk2=[], dls=[]) for _ in range(2)]
        for (dr, h, sl), a_h, da_h in zip(combos, a, da):
            p = prep[dr]
            s_ref, dv_ref = sets[dr][3], sets[dr][7]
            stb = s_ref[0, h]
            dst = dst_ref[dr, h]
            dstb = _bf(dst)
            dob_h, vb_h = p["dob"][:, sl], p["vb"][:, sl]
            dv_ref[:, sl] = _bf(_tn(a_h, dob_h) + _nt(p["k2b"][:, sl], dstb))
            parts[dr]["dq1"].append(_nn(da_h, p["k1b"][:, sl]))
            parts[dr]["dk1"].append(_tn(da_h, p["q1b"][:, sl]))
            parts[dr]["dq2"].append(_nn(dob_h, stb))
            parts[dr]["dk2"].append(_nn(vb_h, dstb))
            el_h = p["el"][:, sl]
            dst_ref[dr, h] = _tn(dob_h, p["q2b"][:, sl]) + dst * el_h
            parts[dr]["dls"].append(jnp.sum(dst * stb.astype(F32), axis=0, keepdims=True) * el_h)
        dlb_tot = jnp.zeros((1, hw), F32)
        for dr in range(2):
            p = prep[dr]
            dq_ref, dz_ref = sets[dr][5], sets[dr][6]
            dq1, dk1, dq2, dk2, dls = (jnp.concatenate(parts[dr][n], axis=1) for n in ("dq1", "dk1", "dq2", "dk2", "dls"))
            dq = dq1 * p["e_qr"] + dq2 * p["e_q"]
            dk = dk1 * p["e_kr"] + dk2 * p["e_kl"]
            dcum = dq1 * p["q1"] - dk1 * p["k1"] + dq2 * p["q2"] - dk2 * p["k2"]
            dlast = jnp.sum(dk2 * p["k2"], axis=0, keepdims=True) + dls
            rowid = lax.broadcasted_iota(jnp.int32, (c, 1), 0)
            dcum = dcum + jnp.where(rowid == p["last_row"], dlast, 0.0)
            df = _nn3(p["acc_t"], dcum) / p["f"] - dk
            sg = p["sg"]
            dz_ref[...] = _bf(df * (1.0 - lb) * sg * (1.0 - sg))
            dlb_tot = dlb_tot + jnp.sum(df * (1.0 - sg), axis=0, keepdims=True)
            dq_ref[...] = _bf(dq * (p["sq"] * (1.0 + p["qr"] * (1.0 - p["sq"]))))
        _acc_all(dlb_ref, step, dlb_tot)

    def col(order, blkcol):
        return pl.BlockSpec((c, hw), lambda s: (order(s, nc, nt), blkcol))

    def st_spec(order):
        return pl.BlockSpec((1, HG_HEADS, d, d), lambda s: (order(s, nc, nt), 0, 0, 0))

    in_specs = []
    for dr, order in enumerate((rfw, rbw)):
        in_specs += [col(order, 0), col(order, 1 + dr), col(order, 3), st_spec(order), col(order, 0)]
    in_specs.append(_full(lbraw.shape))
    out_specs = [col(rfw, 0)] * 3 + [col(rbw, 0)] * 3 + [_whole((1, hw))]
    out_shape = [jax.ShapeDtypeStruct((T, hw), ACT)] * 6 + [jax.ShapeDtypeStruct((1, hw), F32)]
    return _host_call(
        body, ex, lambda: pl.program_id(0) == 0, lambda: pl.program_id(0) == nt - 1,
        name=name, grid=(nt,), in_specs=in_specs, out_specs=out_specs, out_shape=out_shape,
        scratch_shapes=[pltpu.VMEM((2, HG_HEADS, d, d), F32)], sem=("arbitrary",),
        args=(pb, pb, pb, s_fw, do, pb, pb, pb, s_bw, do, lbraw))


def _ret_log_gamma(h, rev):
    hh = RET_HEADS - 1 - h if rev else h
    return math.log(1.0 - 2.0 ** (-5.0 - hh))


def _rope(x, cos, sin):
    half = x.shape[1] // 2
    x1, x2 = x[:, :half], x[:, half:]
    return jnp.concatenate([x1 * cos - x2 * sin, x2 * cos + x1 * sin], axis=1)


def _unrope(dy, cos, sin):
    half = dy.shape[1] // 2
    d1, d2 = dy[:, :half], dy[:, half:]
    return jnp.concatenate([d1 * cos + d2 * sin, d2 * cos - d1 * sin], axis=1)


def _ret_decays(lg, rev):
    c = RET_CHUNK
    r = lax.broadcasted_iota(jnp.int32, (c, c), 0)
    k = lax.broadcasted_iota(jnp.int32, (c, c), 1)
    rel = (k - r) if rev else (r - k)
    dm = jnp.where(rel >= 0, jnp.exp(lg * jnp.maximum(rel, 0).astype(F32)), 0.0)
    pos = lax.broadcasted_iota(jnp.int32, (c, 1), 0).astype(F32)
    if rev:
        qdec = jnp.exp(lg * (c - pos))
        kdec = jnp.exp(lg * pos)
    else:
        qdec = jnp.exp(lg * (pos + 1.0))
        kdec = jnp.exp(lg * (c - 1.0 - pos))
    return dm, qdec, kdec


def _ret_fwd(q, k, v, cos, sin, lc, name, ex=None):
    T = q.shape[0]
    c, dk, dv = RET_CHUNK, RET_DK, RET_DV
    nt, nc = T // c, lc // c
    kscale = dk ** -0.5

    def body(qf, kf, vf, cf, sf_, qb, kb, vb, cb, sb_, of_ref, ob_ref, stf_ref, stb_ref, st_ref):
        @pl.when(pl.program_id(0) == 0)
        def _():
            st_ref[...] = jnp.zeros_like(st_ref)

        sets = ((qf, kf, vf, cf, sf_, of_ref, stf_ref), (qb, kb, vb, cb, sb_, ob_ref, stb_ref))
        combos = [(dr, h) for dr in range(2) for h in range(RET_HEADS)]
        prep = {}
        for dr, (q_ref, k_ref, v_ref, c_ref, s_ref, _, _) in enumerate(sets):
            rev = dr == 1
            cos_v, sin_v = c_ref[...], s_ref[...]
            for h in range(RET_HEADS):
                lg = _ret_log_gamma(h, rev)
                dm, qdec, kdec = _ret_decays(lg, rev)
                qh = _rope(q_ref[:, h * dk:(h + 1) * dk].astype(F32), cos_v, sin_v)
                kh = _rope(k_ref[:, h * dk:(h + 1) * dk].astype(F32), cos_v, sin_v) * kscale
                prep[dr, h] = dict(qb=_bf(qh), kb=_bf(kh), qin=_bf(qh * qdec), kin=_bf(kh * kdec),
                                   v=_bf(v_ref[:, h * dv:(h + 1) * dv]), dm=dm, decay=math.exp(lg * c))
        sc = {ch: _bf(_nt(prep[ch]["qb"], prep[ch]["kb"]) * prep[ch]["dm"]) for ch in combos}
        for dr, h in combos:
            p = prep[dr, h]
            o_ref, so_ref = sets[dr][5], sets[dr][6]
            st = st_ref[dr, h]
            stb = _bf(st)
            so_ref[0, h] = stb
            o_ref[:, h * dv:(h + 1) * dv] = _bf(_nn(sc[dr, h], p["v"]) + _nt(p["qin"], stb))
            st_ref[dr, h] = st * p["decay"] + _tn(p["v"], p["kin"])

    def spec(order, width):
        return pl.BlockSpec((c, width), lambda s: (order(s, nc, nt), 0))

    def st_spec(order):
        return pl.BlockSpec((1, RET_HEADS, dv, dk), lambda s: (order(s, nc, nt), 0, 0, 0))

    in_specs = []
    for order in (_fw_chunk, _bw_chunk):
        in_specs += [spec(order, RET_HEADS * dk), spec(order, RET_HEADS * dk), spec(order, RET_HEADS * dv),
                     spec(order, dk // 2), spec(order, dk // 2)]
    return _host_call(
        body, ex, lambda: pl.program_id(0) == 0, lambda: pl.program_id(0) == nt - 1,
        name=name, grid=(nt,), in_specs=in_specs,
        out_specs=[spec(_fw_chunk, RET_HEADS * dv), spec(_bw_chunk, RET_HEADS * dv), st_spec(_fw_chunk), st_spec(_bw_chunk)],
        out_shape=[jax.ShapeDtypeStruct((T, RET_HEADS * dv), ACT), jax.ShapeDtypeStruct((T, RET_HEADS * dv), ACT),
                   jax.ShapeDtypeStruct((nt, RET_HEADS, dv, dk), ACT), jax.ShapeDtypeStruct((nt, RET_HEADS, dv, dk), ACT)],
        scratch_shapes=[pltpu.VMEM((2, RET_HEADS, dv, dk), F32)], sem=("arbitrary",),
        args=(q, k, v, cos, sin, q, k, v, cos, sin))


def _ret_bwd(q, k, v, cos, sin, s_fw, s_bw, do, lc, name, ex=None):
    T = q.shape[0]
    c, dk, dv = RET_CHUNK, RET_DK, RET_DV
    nt, nc = T // c, lc // c
    kscale = dk ** -0.5

    def rfw(s, nc_, nt_):
        return _fw_chunk(nt_ - 1 - s, nc_, nt_)

    def rbw(s, nc_, nt_):
        return _bw_chunk(nt_ - 1 - s, nc_, nt_)

    def body(qf, kf, vf, cf, sf_, stf, dof, qb, kb, vb, cb, sb_, stb_, dob_,
             dqf, dkf, dvf, dqb, dkb, dvb, dst_ref):
        @pl.when(pl.program_id(0) == 0)
        def _():
            dst_ref[...] = jnp.zeros_like(dst_ref)

        sets = ((qf, kf, vf, cf, sf_, stf, dof, dqf, dkf, dvf), (qb, kb, vb, cb, sb_, stb_, dob_, dqb, dkb, dvb))
        combos = [(dr, h) for dr in range(2) for h in range(RET_HEADS)]
        prep = {}
        for dr, (q_ref, k_ref, v_ref, c_ref, s_ref, _, do_ref, _, _, _) in enumerate(sets):
            rev = dr == 1
            cos_v, sin_v = c_ref[...], s_ref[...]
            for h in range(RET_HEADS):
                lg = _ret_log_gamma(h, rev)
                dm, qdec, kdec = _ret_decays(lg, rev)
                qh = _rope(q_ref[:, h * dk:(h + 1) * dk].astype(F32), cos_v, sin_v)
                kh = _rope(k_ref[:, h * dk:(h + 1) * dk].astype(F32), cos_v, sin_v) * kscale
                prep[dr, h] = dict(qb=_bf(qh), kb=_bf(kh), qin=_bf(qh * qdec), kin=_bf(kh * kdec),
                                   v=_bf(v_ref[:, h * dv:(h + 1) * dv]), dob=_bf(do_ref[:, h * dv:(h + 1) * dv]),
                                   dm=dm, qdec=qdec, kdec=kdec, decay=math.exp(lg * c), cos=cos_v, sin=sin_v)
        sc = {ch: _bf(_nt(prep[ch]["qb"], prep[ch]["kb"]) * prep[ch]["dm"]) for ch in combos}
        dsc = {ch: _bf(_nt(prep[ch]["dob"], prep[ch]["v"]) * prep[ch]["dm"]) for ch in combos}
        carried = {}
        for dr, h in combos:
            p = prep[dr, h]
            dv_ref = sets[dr][9]
            dst = dst_ref[dr, h]
            dstb = _bf(dst)
            carried[dr, h] = dstb
            dv_ref[:, h * dv:(h + 1) * dv] = _bf(_tn(sc[dr, h], p["dob"]) + _nt(p["kin"], dstb))
            dst_ref[dr, h] = _tn(p["dob"], p["qin"]) + dst * p["decay"]
        for dr, h in combos:
            p = prep[dr, h]
            st_in, dq_ref, dk_ref = sets[dr][5], sets[dr][7], sets[dr][8]
            dq_r = _nn(dsc[dr, h], p["kb"]) + _nn(p["dob"], st_in[0, h]) * p["qdec"]
            dk_r = _tn(dsc[dr, h], p["qb"]) + _nn(p["v"], carried[dr, h]) * p["kdec"]
            dq_ref[:, h * dk:(h + 1) * dk] = _bf(_unrope(dq_r, p["cos"], p["sin"]))
            dk_ref[:, h * dk:(h + 1) * dk] = _bf(_unrope(dk_r * kscale, p["cos"], p["sin"]))

    def spec(order, width):
        return pl.BlockSpec((c, width), lambda s: (order(s, nc, nt), 0))

    def st_spec(order):
        return pl.BlockSpec((1, RET_HEADS, dv, dk), lambda s: (order(s, nc, nt), 0, 0, 0))

    in_specs = []
    for order in (rfw, rbw):
        in_specs += [spec(order, RET_HEADS * dk), spec(order, RET_HEADS * dk), spec(order, RET_HEADS * dv),
                     spec(order, dk // 2), spec(order, dk // 2), st_spec(order), spec(order, RET_HEADS * dv)]
    out_specs, out_shape = [], []
    for order in (rfw, rbw):
        out_specs += [spec(order, RET_HEADS * dk), spec(order, RET_HEADS * dk), spec(order, RET_HEADS * dv)]
        out_shape += [jax.ShapeDtypeStruct((T, RET_HEADS * dk), ACT), jax.ShapeDtypeStruct((T, RET_HEADS * dk), ACT),
                      jax.ShapeDtypeStruct((T, RET_HEADS * dv), ACT)]
    return _host_call(
        body, ex, lambda: pl.program_id(0) == 0, lambda: pl.program_id(0) == nt - 1,
        name=name, grid=(nt,), in_specs=in_specs, out_specs=out_specs, out_shape=out_shape,
        scratch_shapes=[pltpu.VMEM((2, RET_HEADS, dv, dk), F32)], sem=("arbitrary",),
        args=(q, k, v, cos, sin, s_fw, do, q, k, v, cos, sin, s_bw, do))


def _attn_rope_tables(lc, l):
    t = jnp.arange(l)
    row = (t // GRID_W).astype(F32)
    colp = (t % GRID_W).astype(F32)
    n_freq = HEAD_DIM // 4
    inv = 10000.0 ** (-jnp.arange(n_freq, dtype=F32) / n_freq)
    ang = jnp.concatenate([row[:, None] * inv, colp[:, None] * inv], axis=-1)
    cos = jnp.concatenate([jnp.ones((lc, HEAD_DIM // 2), F32), jnp.cos(ang)], axis=0)
    sin = jnp.concatenate([jnp.zeros((lc, HEAD_DIM // 2), F32), jnp.sin(ang)], axis=0)
    return jnp.concatenate([cos, cos], axis=1), jnp.concatenate([-sin, sin], axis=1)


def _ret_rope_tables(lc, l):
    theta = 1.0 / (10000.0 ** jnp.linspace(0.0, 1.0, RET_DK // 2, dtype=F32))
    ang = jnp.arange(l, dtype=F32)[:, None] * theta
    cos = jnp.concatenate([jnp.ones((lc, RET_DK // 2), F32), jnp.cos(ang)], axis=0)
    sin = jnp.concatenate([jnp.zeros((lc, RET_DK // 2), F32), jnp.sin(ang)], axis=0)
    return cos, sin


def _heads_major(slab, n_heads):
    t = slab.shape[0]
    return slab.reshape(t, n_heads, HEAD_DIM).transpose(1, 0, 2)


def _slab(hm):
    nh, t, hd = hm.shape
    return hm.transpose(1, 0, 2).reshape(t, nh * hd)


COL_SHARDED = ("ffn_in0", "ffn_in1", "even_in", "even_in_a", "even_in_b", "odd_in")


def _full_weight(name, g):
    if name in COL_SHARDED:
        return g.transpose(1, 0, 2).reshape(g.shape[1], -1)
    return g.reshape(-1, g.shape[2])


def _shard_slots(name, g):
    if name in COL_SHARDED:
        return g.reshape(g.shape[0], N_DEV, -1).transpose(1, 0, 2)
    return g.reshape(N_DEV, -1, g.shape[1])


def _local_step(xs, target, mv, norm_g, w, qk_g, sink, hg_out_g, lbraw, lc, shards=None):
    T, dm = xs.shape
    l = T - lc
    tm = lc
    blk = ATTN_BLOCK
    d2, d3 = 2 * dm, 3 * dm
    w = dict(w)
    gw, recv = {}, {}

    def ms(layer, a, b):
        return mv[layer, :, :, a:b]

    def gather(names):
        return None if shards is None else _Exchange(GATHER2, [shards[n] for n in names])

    def arrived(names, got):
        for n, g in zip(names, got):
            w[n] = _full_weight(n, g)

    def scatter(names):
        return None if shards is None else _Exchange(SCATTER, [_shard_slots(n, gw[n]) for n in names])

    def scattered(names, got):
        for n, g in zip(names, got):
            recv[n] = g

    g00, g01, g10, g11 = (norm_g[i, j][None, :] for i in (0, 1) for j in (0, 1))

    riding = ["even_out"]
    (pa, pb), got = _pre_fwd(xs, g00, ms(0, 0, d2), w["even_in"], ((0, 768), (768, 3328)), tm, "pre0_fwd", gather(riding))
    arrived(riding, got)
    cos2, sin2 = _attn_rope_tables(lc, l)
    cosp, sinp = jnp.concatenate([cos2, cos2], axis=1), jnp.concatenate([sin2, sin2], axis=1)
    gains5 = jnp.concatenate([jnp.broadcast_to(jnp.tile(qk_g[0], 2), (N_PAIRS - 1, PAIR)), jnp.tile(qk_g[1], 2)[None]])[:, None, :]
    qt, ks, vs = _qk_slab_fwd(pa, gains5, cosp, sinp, tm, "qk_prep_fwd")
    sinkb = jnp.broadcast_to(sink.reshape(ATTN_KV, 4, 1, 1), (ATTN_KV, 4, blk, 1)).reshape(ATTN_KV, 4 * blk, 1)
    riding = ["ffn_in0"]
    (a_slab, lse), got = _attn_slab_fwd(qt, ks, vs, sinkb, lc, "attn_fwd", gather(riding))
    arrived(riding, got)
    riding = ["ffn_out0", "odd_out"]
    (hg_of, hg_ob, hg_sf, hg_sb), got = _gla_fwd(pb, lbraw, lc, "hgrn_fwd", gather(riding))
    arrived(riding, got)
    x01, z0, yp0 = _post_fwd(xs, hg_of, hg_ob, pb, 4, hg_out_g, a_slab, w["even_out"], ms(0, d2, d3), HG_D, tm, "post0_fwd")
    riding = ["odd_in"]
    (x02, u0, f0), got = _ffn_fwd(x01, g01, ms(0, d3, 6 * dm), w["ffn_in0"], w["ffn_out0"], tm, "ffn0_fwd", ex=gather(riding))
    arrived(riding, got)

    riding = ["ffn_out1"]
    (rq, rk, rv, rg), got = _pre_fwd(x02, g10, ms(1, 0, d2), w["odd_in"],
                                     ((0, 1024), (1024, 2048), (2048, 4096), (4096, 6144)), tm, "pre1_fwd", gather(riding),
                                     out_dtype=ACT)
    arrived(riding, got)
    rcos, rsin = _ret_rope_tables(lc, l)
    riding = ["ffn_in1"]
    (rt_of, rt_ob, rt_sf, rt_sb), got = _ret_fwd(rq, rk, rv, rcos, rsin, lc, "ret_fwd", gather(riding))
    arrived(riding, got)
    x11, z1, yp1 = _post_fwd(x02, rt_of, rt_ob, rg, 0, None, None, w["odd_out"], ms(1, d2, d3), RET_DV, tm, "post1_fwd")
    (dx, u1, f1, loss), _ = _ffn_fwd(x11, g11, ms(1, d3, 6 * dm), w["ffn_in1"], w["ffn_out1"], tm, "ffn1_fwd", target)

    (dx, h, du, act, df, dms_f1, dg11), _ = _ffn_bwd(x11, dx, u1, f1, g11, ms(1, d3, 6 * dm), w["ffn_in1"], w["ffn_out1"], tm,
                                                     "ffn1_bwd")
    gw["ffn_in1"] = _wgrad(h, du, "wg_ffn_in1")
    gw["ffn_out1"] = _wgrad(act, df, "wg_ffn_out1")
    do1, dgr1, dy1, z1_t, dgate_p1, _ = _post_bwd(dx, z1, yp1, rt_of, rt_ob, rg, 0, None, w["odd_out"], ms(1, d2, d3), 0, RET_DV, tm,
                                                  "post1_bwd")
    gw["odd_out"] = _wgrad(z1_t, dy1, "wg_odd_out")
    riding = ["ffn_in1", "ffn_out1"]
    (dqf, dkf, dvf, dqb, dkb, dvb), got = _ret_bwd(rq, rk, rv, rcos, rsin, rt_sf, rt_sb, do1, lc, "ret_bwd", scatter(riding))
    scattered(riding, got)
    riding = ["odd_out"]
    (dx, h, dp, dms_p1, dg10), got = _pre_bwd(x02, dx, g10, ms(1, 0, d2), w["odd_in"],
                                              [(0, [dqf, dqb]), (1024, [dkf, dkb]), (2048, [dvf, dvb]), (4096, [dgr1])], tm,
                                              "pre1_bwd", ex=scatter(riding))
    scattered(riding, got)
    gw["odd_in"] = _wgrad(h, dp, "wg_odd_in")

    riding = ["odd_in"]
    (dx, h, du, act, df, dms_f0, dg01), got = _ffn_bwd(x01, dx, u0, f0, g01, ms(0, d3, 6 * dm), w["ffn_in0"], w["ffn_out0"], tm,
                                                       "ffn0_bwd", scatter(riding))
    scattered(riding, got)
    gw["ffn_in0"] = _wgrad(h, du, "wg_ffn_in0")
    gw["ffn_out0"] = _wgrad(act, df, "wg_ffn_out0")
    do0, dgr0, da0, dy0, z0_t, dgate_p0, d_hg_gain = _post_bwd(dx, z0, yp0, hg_of, hg_ob, pb, 4, hg_out_g, w["even_out"],
                                                              ms(0, d2, d3), 512, HG_D, tm, "post0_bwd")
    gw["even_out"] = _wgrad(z0_t, dy0, "wg_even_out")
    riding = ["ffn_in0", "ffn_out0"]
    (hq_f, hz_f, hv_f, hq_b, hz_b, hv_b, dlb), got = _gla_bwd(pb, lbraw, hg_sf, hg_sb, do0, lc, "hgrn_bwd", scatter(riding))
    scattered(riding, got)
    riding = ["even_out"]
    (dq_att, dk_att, dv_att, dsink), got = _attn_slab_bwd(qt, ks, vs, sinkb, a_slab, lse, da0, lc, "attn_bwd", scatter(riding))
    scattered(riding, got)
    dqk_raw, dgain5 = _qk_slab_bwd(dq_att, dk_att, pa, gains5, cosp, sinp, tm, "qk_prep_bwd")
    pieces0 = [(0, [dqk_raw]), (640, [dv_att]),
               (768, [hq_f, hq_b]), (1280, [hz_f]), (1792, [hz_b]), (2304, [hv_f, hv_b]), (2816, [dgr0])]
    (dx, h, dp, dms_p0, dg00), _ = _pre_bwd(xs, dx, g00, ms(0, 0, d2), w["even_in"], pieces0, tm, "pre0_bwd",
                                            latent_dx=shards is not None)
    if shards is None:
        gw["even_in"] = _wgrad(h, dp, "wg_even_in")
    else:
        half = dm // 2
        gw["even_in_a"] = _wgrad(h, dp, "wg_even_in_a", rows=(0, half))
        gw["even_in_b"], got = _wgrad(h, dp, "wg_even_in_b", rows=(half, half), ex=scatter(["even_in_a"]))
        scattered(["even_in_a"], got)

    dmv = jnp.stack([jnp.concatenate([dms_p0, dgate_p0, dms_f0], axis=2), jnp.concatenate([dms_p1, dgate_p1, dms_f1], axis=2)])
    small = {
        "dmv": dmv,
        "norm_g": jnp.stack([jnp.stack([dg00[0], dg01[0]]), jnp.stack([dg10[0], dg11[0]])]),
        "qk_g": jnp.stack([jnp.sum(dgain5[:N_PAIRS - 1, 0].reshape(-1, HEAD_DIM), axis=0),
                           jnp.sum(dgain5[N_PAIRS - 1, 0].reshape(-1, HEAD_DIM), axis=0)]),
        "sink": dsink.reshape(ATTN_HEADS),
        "hg_out_g": d_hg_gain[0],
        "lb": dlb[0],
        "loss": loss[0, 0],
    }
    if shards is not None:
        gw = {n: recv.get(n, g) for n, g in gw.items()}
    return loss, dx, gw, small


HBM_SPEC = pl.BlockSpec(memory_space=pltpu.HBM)


def _my_index():
    return 4 * lax.axis_index("x") + 2 * lax.axis_index("y") + lax.axis_index("c")


def _peer(k):
    pos = []
    for axis, bit in (("x", 4), ("y", 2), ("c", 1)):
        a = lax.axis_index(axis)
        pos.append(1 - a if k & bit else a)
    return tuple(pos)


def _peer_index(k):
    px, py, pc = _peer(k)
    return 4 * px + 2 * py + pc


GATHER, SCATTER = "gather", "scatter"
GATHER2 = "gather over ICI once per chip"
SIBLING = 1
OTHER_CHIPS = (2, 4, 6)


class _Exchange:
    def __init__(self, mode, arrays):
        self.mode, self.arrays, self.n = mode, list(arrays), len(arrays)

    def out_shape(self):
        if self.mode in (GATHER, GATHER2):
            return [jax.ShapeDtypeStruct((N_DEV,) + a.shape, a.dtype) for a in self.arrays]
        return [jax.ShapeDtypeStruct(a.shape, a.dtype) for a in self.arrays]

    def specs(self):
        return [HBM_SPEC] * self.n

    def scratch(self):
        return [pltpu.SemaphoreType.DMA((self.n, N_DEV - 1)), pltpu.SemaphoreType.DMA((self.n, N_DEV - 1)),
                pltpu.SemaphoreType.DMA((self.n,))]

    def _copies(self, in_refs, out_refs, send_sems, recv_sems, local_sems, landing):
        me = _my_index()
        local, remote = [], []
        for a, (src, dst) in enumerate(zip(in_refs, out_refs)):
            part = (lambda j, s=src: s) if self.mode == GATHER else (lambda j, s=src: s.at[j])
            local.append(pltpu.make_async_copy(part(me), dst.at[me], local_sems.at[a]))
            for k in range(1, N_DEV):
                pj = _peer_index(k)
                remote.append(pltpu.make_async_remote_copy(
                    src_ref=part(pj), dst_ref=dst.at[pj if landing else me], send_sem=send_sems.at[a, k - 1],
                    recv_sem=recv_sems.at[a, k - 1], device_id=_peer(k), device_id_type=MESH))
        return local, remote

    def _copy2(self, a, src, dst, sems, slot, relation, to):
        send_sems, recv_sems, _ = sems
        return pltpu.make_async_remote_copy(src_ref=src, dst_ref=dst.at[slot], send_sem=send_sems.at[a, relation - 1],
                                            recv_sem=recv_sems.at[a, relation - 1], device_id=_peer(to), device_id_type=MESH)

    def start(self, in_refs, out_refs, sems):
        if self.mode == GATHER2:
            me = _my_index()
            for a, (src, dst) in enumerate(zip(in_refs, out_refs)):
                pltpu.make_async_copy(src, dst.at[me], sems[2].at[a]).start()
                for k in (SIBLING,) + OTHER_CHIPS:
                    self._copy2(a, src, dst, sems, me, k, k).start()
            return
        local, remote = self._copies(in_refs, out_refs, *sems, landing=False)
        for cp in local + remote:
            cp.start()

    def forward(self, in_refs, out_refs, sems):
        for a, (src, dst) in enumerate(zip(in_refs, out_refs)):
            for r in OTHER_CHIPS:
                pj = _peer_index(r)
                self._copy2(a, src, dst, sems, pj, r, r).wait_recv()
                self._copy2(a, dst.at[pj], dst, sems, pj, r ^ SIBLING, SIBLING).start()

    def wait(self, in_refs, out_refs, sems):
        if self.mode == GATHER2:
            me = _my_index()
            for a, (src, dst) in enumerate(zip(in_refs, out_refs)):
                for k in (SIBLING,) + OTHER_CHIPS:
                    self._copy2(a, src, dst, sems, me, k, k).wait_send()
                self._copy2(a, src, dst, sems, _peer_index(SIBLING), SIBLING, SIBLING).wait_recv()
                for r in OTHER_CHIPS:
                    passed = self._copy2(a, src, dst, sems, _peer_index(r ^ SIBLING), r ^ SIBLING, SIBLING)
                    passed.wait_send()
                    passed.wait_recv()
                pltpu.make_async_copy(src, dst.at[me], sems[2].at[a]).wait()
            return
        local, remote = self._copies(in_refs, out_refs, *sems, landing=True)
        for cp in remote:
            cp.wait_send()
            cp.wait_recv()
        for cp in local:
            cp.wait()

    def ride(self, refs, n_in, n_out, first, mid, last):
        refs = list(refs)
        n = self.n
        x_in = refs[n_in:n_in + n]
        x_out = refs[n_in + n + n_out:n_in + 2 * n + n_out]
        sems = refs[n_in + 2 * n + n_out:n_in + 2 * n + n_out + 3]

        @pl.when(first)
        def _():
            self.start(x_in, x_out, sems)

        if self.mode == GATHER2:
            @pl.when(mid)
            def _():
                self.forward(x_in, x_out, sems)

        @pl.when(last)
        def _():
            self.wait(x_in, x_out, sems)

        return refs[:n_in] + refs[n_in + n:n_in + n + n_out] + refs[n_in + 2 * n + n_out + 3:]

    def call(self, name):
        n = self.n

        def body(*refs):
            ins, outs, sems = refs[:n], refs[n:2 * n], refs[2 * n:]
            self.start(ins, outs, sems)
            if self.mode == GATHER2:
                self.forward(ins, outs, sems)
            self.wait(ins, outs, sems)

        return pl.pallas_call(body, name=name, in_specs=self.specs(), out_specs=self.specs(), out_shape=self.out_shape(),
                              scratch_shapes=self.scratch())(*self.arrays)


def _all_gather(v, name):
    return _Exchange(GATHER, [v]).call(name)[0]


def _hosted(kernel_body, ex, n_in, n_out, first, last, grid):
    if ex is None:
        return kernel_body

    def body(*refs):
        mid = pl.program_id(0) == (2 * grid[0]) // 3 if len(grid) == 1 else None
        kernel_body(*ex.ride(refs, n_in, n_out, first(), mid, last()))

    return body


def _host_call(kernel_body, ex, first, last, name, grid, in_specs, out_specs, out_shape, scratch_shapes, sem, args):
    n_in, n_out = len(in_specs), len(out_specs)
    if ex is None:
        outs = pl.pallas_call(kernel_body, name=name, grid=grid, in_specs=in_specs, out_specs=out_specs, out_shape=out_shape,
                              scratch_shapes=scratch_shapes, compiler_params=_cp(*sem))(*args)
        return list(outs), []
    outs = pl.pallas_call(
        _hosted(kernel_body, ex, n_in, n_out, first, last, grid), name=name, grid=grid,
        in_specs=list(in_specs) + ex.specs(), out_specs=list(out_specs) + ex.specs(),
        out_shape=list(out_shape) + ex.out_shape(), scratch_shapes=ex.scratch() + list(scratch_shapes),
        compiler_params=_cp(*sem))(*args, *ex.arrays)
    return list(outs[:n_out]), list(outs[n_out:])


def _mod_fwd(call, mod_w, bias, name):
    nl, dm, n = mod_w.shape

    def body(c_ref, w_ref, b_ref, o_ref):
        cv = c_ref[...]
        cond = _bf(cv * _sig(cv))
        for layer in range(nl):
            o_ref[layer] = _nn(cond, _bf(w_ref[layer])) + b_ref[layer]

    return pl.pallas_call(
        body, name=name, out_shape=jax.ShapeDtypeStruct((nl, call.shape[0], n), F32),
        compiler_params=pltpu.CompilerParams(vmem_limit_bytes=VMEM_LIMIT),
    )(call, mod_w, bias)


def _mod_bwd(call, dm_all, mod_w, name):
    nl, dm, n = mod_w.shape

    def body(c_ref, d_ref, w_ref, gw_ref, dc_ref):
        cv = c_ref[...]
        cond = _bf(cv * _sig(cv))
        dc = jnp.zeros(cv.shape, F32)
        for layer in range(nl):
            db = _bf(d_ref[layer])
            gw_ref[layer] = _tn(cond, db)
            dc = dc + _nt(db, _bf(w_ref[layer]))
        dc_ref[...] = dc

    return pl.pallas_call(
        body, name=name,
        out_shape=[jax.ShapeDtypeStruct(mod_w.shape, F32), jax.ShapeDtypeStruct(call.shape, F32)],
        compiler_params=pltpu.CompilerParams(vmem_limit_bytes=VMEM_LIMIT),
    )(call, dm_all, mod_w)


def _sum_parts(g, name):
    def body(g_ref, o_ref):
        acc = g_ref[0]
        for j in range(1, g.shape[0]):
            acc = acc + g_ref[j]
        o_ref[...] = acc

    return pl.pallas_call(body, name=name, out_shape=jax.ShapeDtypeStruct(g.shape[1:], g.dtype))(g)


def _small_finish(dcond_g, c_ctx, dlb, lbraw, dm_ctx, dm_lat, name):
    def body(dc_ref, c_ref, dlb_ref, lb_ref, mc_ref, ml_ref, gc_ref, glb_ref, gb_ref):
        acc = dc_ref[0, 0:1, :]
        for j in range(1, N_DEV):
            acc = acc + dc_ref[j, 0:1, :]
        cv = c_ref[...]
        s = _sig(cv)
        gc_ref[...] = acc * (s * (1.0 + cv * (1.0 - s)))
        lb = _lower_bound(lb_ref)
        d0 = dlb_ref[...] * lb * (1.0 - lb)
        glb_ref[0:1, :] = d0
        glb_ref[1:2, :] = -d0
        gb_ref[...] = mc_ref[...] + ml_ref[...]

    return pl.pallas_call(
        body, name=name,
        out_shape=[jax.ShapeDtypeStruct(c_ctx.shape, F32), jax.ShapeDtypeStruct(lbraw.shape, F32),
                   jax.ShapeDtypeStruct(dm_ctx.shape, F32)],
    )(dcond_g, c_ctx, dlb, lbraw, dm_ctx, dm_lat)


def _row_tile(r, cap, mult):
    best = r
    for t in range(mult, min(r, cap) + 1, mult):
        if r % t == 0:
            best = t
    return best


def _adam(g_list, w, m, v, name, ex=None):
    nl, r, cdim = w.shape
    p = g_list[0].shape[0]
    tr = _row_tile(r, 128, 16)
    ni = r // tr

    def body(*refs):
        g_refs = refs[:nl]
        w_ref, m_ref, v_ref, go_ref, d_ref, mo_ref, vo_ref = refs[nl:]
        layer = pl.program_id(0)

        def total(g_ref):
            acc = g_ref[0].astype(F32)
            for j in range(1, p):
                acc = acc + g_ref[j].astype(F32)
            return acc

        g = total(g_refs[0])
        for k in range(1, nl):
            g = jnp.where(layer == k, total(g_refs[k]), g)
        m2 = ADAM_B1 * m_ref[0] + (1.0 - ADAM_B1) * g
        v2 = ADAM_B2 * v_ref[0] + (1.0 - ADAM_B2) * (g * g)
        m_hat = m2 / (1.0 - ADAM_B1 ** ADAM_STEP)
        v_hat = v2 / (1.0 - ADAM_B2 ** ADAM_STEP)
        go_ref[0] = g
        d_ref[0] = -ADAM_LR * (m_hat / (jnp.sqrt(v_hat) + ADAM_EPS) + ADAM_WD * w_ref[0])
        mo_ref[0] = m2
        vo_ref[0] = v2

    def g_spec(k):
        return pl.BlockSpec((p, tr, cdim), lambda la, i: (0, jnp.where(la == k, i, jnp.where(la < k, 0, ni - 1)), 0))

    spec = pl.BlockSpec((1, tr, cdim), lambda la, i: (la, i, 0))
    return _host_call(
        body, ex, lambda: (pl.program_id(0) == 0) & (pl.program_id(1) == 0),
        lambda: (pl.program_id(0) == nl - 1) & (pl.program_id(1) == ni - 1),
        name=name, grid=(nl, ni),
        in_specs=[g_spec(k) for k in range(nl)] + [spec, spec, spec],
        out_specs=[spec] * 4, out_shape=[jax.ShapeDtypeStruct((nl, r, cdim), F32)] * 4,
        scratch_shapes=[], sem=("arbitrary", "arbitrary"), args=(*g_list, w, m, v))


def _f32_as_rows(a, width):
    return lax.bitcast_convert_type(a.reshape(-1), BF16).reshape(-1, width)


def _rows_as_f32(rows):
    return lax.bitcast_convert_type(rows.reshape(rows.shape[:-2] + (-1, 2)), F32)


def _pad_rows(a, mult):
    r = (-a.shape[-2]) % mult
    if r == 0:
        return a
    widths = [(0, 0)] * (a.ndim - 2) + [(0, r), (0, 0)]
    return jnp.pad(a, widths)


def _pack_flat(parts, lane):
    flat = jnp.concatenate([p.reshape(-1).astype(F32) for p in parts])
    n = flat.shape[0]
    rows = -(-n // lane)
    rows += (-rows) % 8
    return jnp.pad(flat, (0, rows * lane - n)).reshape(rows, lane)


def _unpack_flat(packed, shapes):
    flat = packed.reshape(-1)
    out, off = [], 0
    for s in shapes:
        n = math.prod(s)
        out.append(flat[off:off + n].reshape(s))
        off += n
    return out


def kernel(x, c, ctx, c_ctx, mod_w, mod_b, norm_g, ffn_w_in, ffn_w_out, even_w_in, even_w_out, attn_qk_norm_g, attn_sink, hgrn_out_norm_g, hgrn_lb, odd_w_in, odd_w_out, loss_target, m_c_ctx, m_mod_w, m_mod_b, m_norm_g, m_ffn_w_in, m_ffn_w_out, m_even_w_in, m_even_w_out, m_attn_qk_norm_g, m_attn_sink, m_hgrn_out_norm_g, m_hgrn_lb, m_odd_w_in, m_odd_w_out, v_c_ctx, v_mod_w, v_mod_b, v_norm_g, v_ffn_w_in, v_ffn_w_out, v_even_w_in, v_even_w_out, v_attn_qk_norm_g, v_attn_sink, v_hgrn_out_norm_g, v_hgrn_lb, v_odd_w_in, v_odd_w_out):
    me = _my_index()
    lc, dm = ctx.shape[1], x.shape[2]
    nmod = mod_w.shape[2]
    big = (ffn_w_in, ffn_w_out, even_w_in, even_w_out, odd_w_in, odd_w_out)

    extra = _pad_rows(jnp.concatenate([_f32_as_rows(c, dm), _f32_as_rows(norm_g, dm)], axis=0), 16)
    shards = {"ffn_in0": ffn_w_in[0], "ffn_in1": ffn_w_in[1], "ffn_out0": ffn_w_out[0], "ffn_out1": ffn_w_out[1],
              "even_in": even_w_in[0], "even_out": even_w_out[0], "odd_in": odd_w_in[0], "odd_out": odd_w_out[0]}
    shards = {n: a.astype(BF16) for n, a in shards.items()}
    first = _Exchange(GATHER2, [shards["even_in"], extra]).call("gather_first")
    w = {"even_in": _full_weight("even_in", first[0])}
    c_all = _rows_as_f32(first[1][:, 0:2])
    norm_g_all = _rows_as_f32(first[1][:, 2:3]).reshape(N_DEV, 2, 2, -1)
    norm_g_full = norm_g_all.transpose(1, 2, 0, 3).reshape(2, 2, dm)

    call = jnp.concatenate([c_all, c_ctx[None, :], jnp.zeros((16 - N_DEV - 1, dm), F32)], axis=0)
    bias = lax.dynamic_slice_in_dim(mod_b, me * nmod, nmod, axis=1)[:, None, :]
    m_sh = _mod_fwd(call, mod_w, bias, "mod_fwd")
    m_g = _all_gather(m_sh.reshape(-1, nmod), "gather_mod").reshape(N_DEV, 2, 16, nmod)
    m_all = m_g.transpose(1, 2, 0, 3).reshape(2, 16, -1)
    m_lat = lax.dynamic_index_in_dim(m_all, me, axis=1, keepdims=False)
    mv = jnp.stack([m_all[:, N_DEV], m_lat], axis=1)[:, :, None, :]

    xs = jnp.concatenate([ctx[0], x[0]], axis=0)
    _, dxs, gw, small = _local_step(xs, loss_target[0], mv, norm_g_full, w, attn_qk_norm_g[0], attn_sink[0],
                                    hgrn_out_norm_g, hgrn_lb, lc, shards)
    grad_x = dxs[None]

    last = _Exchange(SCATTER, [_shard_slots("even_in_b", gw["even_in_b"])])
    big_g = [[gw["ffn_in0"], gw["ffn_in1"]], [gw["ffn_out0"], gw["ffn_out1"]], None, [gw["even_out"]],
             [gw["odd_in"]], [gw["odd_out"]]]
    halves = (2, even_w_in.shape[1] // 2, even_w_in.shape[2])
    big_w = (ffn_w_in, ffn_w_out, even_w_in.reshape(halves), even_w_out, odd_w_in, odd_w_out)
    big_m = (m_ffn_w_in, m_ffn_w_out, m_even_w_in.reshape(halves), m_even_w_out, m_odd_w_in, m_odd_w_out)
    big_v = (v_ffn_w_in, v_ffn_w_out, v_even_w_in.reshape(halves), v_even_w_out, v_odd_w_in, v_odd_w_out)
    big_names = ("ffn_w_in", "ffn_w_out", "even_w_in", "even_w_out", "odd_w_in", "odd_w_out")
    big_out = [None] * 6
    for i in (0, 1, 3, 4, 5, 2):
        big_out[i], got = _adam(big_g[i], big_w[i], big_m[i], big_v[i], "adam_" + big_names[i], last if i == 0 else None)
        if i == 0:
            big_g[2] = [gw["even_in_a"], got[0]]
    big_out[2] = [o.reshape(even_w_in.shape) for o in big_out[2]]
    big_res = [[big_out[i][k] for i in range(6)] for k in range(4)]

    dmv = small["dmv"]
    small_shapes = [(2, 6 * dm), (2, 6 * dm), (2, 2, dm), (2, HEAD_DIM), (ATTN_HEADS,), (HG_D,), (HG_HEADS * HG_D,), (1,)]
    vec = _pack_flat([dmv[:, 0, 0], dmv[:, 1, 0], small["norm_g"], small["qk_g"], small["sink"], small["hg_out_g"],
                      small["lb"], small["loss"]], 128)
    vec_g = _all_gather(vec, "gather_small")
    tot = _unpack_flat(_sum_parts(vec_g, "sum_small"), small_shapes)
    dm_ctx_tot, dm_lat_tot, g_norm_full, g_qk, g_sink, g_hg, dlb_tot, loss_tot = tot
    dm_lat_each = vec_g.reshape(N_DEV, -1)[:, 12 * dm:24 * dm].reshape(N_DEV, 2, 6 * dm)
    dm_lat_mine = lax.dynamic_slice_in_dim(dm_lat_each, me * nmod, nmod, axis=2).transpose(1, 0, 2)
    dm_ctx_mine = lax.dynamic_slice_in_dim(dm_ctx_tot, me * nmod, nmod, axis=1)[:, None, :]
    dm_all = jnp.concatenate([dm_lat_mine, dm_ctx_mine, jnp.zeros((2, 16 - N_DEV - 1, nmod), F32)], axis=1)
    g_mod_w, dcond = _mod_bwd(call, dm_all, mod_w, "mod_bwd")
    dcond_g = _all_gather(dcond[N_DEV:], "gather_dcond")
    g_c_ctx, g_lb, g_mod_b = _small_finish(dcond_g, c_ctx[None, :], dlb_tot[None, :], hgrn_lb, dm_ctx_tot, dm_lat_tot,
                                           "small_finish")
    g_norm = lax.dynamic_slice_in_dim(g_norm_full, me * norm_g.shape[2], norm_g.shape[2], axis=2)

    mod_res, _ = _adam([g_mod_w[0][None], g_mod_w[1][None]], mod_w, m_mod_w, v_mod_w, "adam_mod_w")

    sm_w = (c_ctx, mod_b, norm_g, attn_qk_norm_g, attn_sink, hgrn_out_norm_g, hgrn_lb)
    sm_m = (m_c_ctx, m_mod_b, m_norm_g, m_attn_qk_norm_g, m_attn_sink, m_hgrn_out_norm_g, m_hgrn_lb)
    sm_v = (v_c_ctx, v_mod_b, v_norm_g, v_attn_qk_norm_g, v_attn_sink, v_hgrn_out_norm_g, v_hgrn_lb)
    sm_g = (g_c_ctx, g_mod_b, g_norm, g_qk, g_sink, g_hg, g_lb)
    sm_shapes = [a.shape for a in sm_w]
    sm_out, _ = _adam([_pack_flat(sm_g, 128)[None]], _pack_flat(sm_w, 128)[None], _pack_flat(sm_m, 128)[None],
                      _pack_flat(sm_v, 128)[None], "adam_small")
    sm_res = [_unpack_flat(o, sm_shapes) for o in sm_out]

    def ordered(k):
        s, b = sm_res[k], big_res[k]
        return [s[0], mod_res[k], s[1], s[2], b[0], b[1], b[2], b[3], s[3], s[4], s[5], s[6], b[4], b[5]]

    return (loss_tot[0], grad_x, *ordered(0), *ordered(1), *ordered(2), *ordered(3))
```

```python
import functools
import math

import jax
import jax.numpy as jnp
from jax import lax
from jax.experimental import pallas as pl
from jax.experimental.pallas import tpu as pltpu

F32 = jnp.float32
BF16 = jnp.bfloat16
EPS = 1e-6
N_DEV = 8
MESH = pl.DeviceIdType.MESH

HEAD_DIM = 64
ATTN_HEADS = 8
ATTN_KV = 2
ATTN_BLOCK = 128
WINDOW = 128
GRID_W = 64
HG_HEADS = 4
HG_D = 128
HG_CHUNK = 64
RET_HEADS = 4
RET_DK = 256
RET_DV = 512
RET_CHUNK = 128
NEG = -1e30

ADAM_LR = 0.001
ADAM_B1 = 0.9
ADAM_B2 = 0.999
ADAM_EPS = 1e-08
ADAM_WD = 0.01
ADAM_STEP = 10

VMEM_LIMIT = 60 * 1024 * 1024


def _cp(*sem):
    return pltpu.CompilerParams(dimension_semantics=sem, vmem_limit_bytes=VMEM_LIMIT)


def _nn(a, b):
    return jnp.dot(a, b, preferred_element_type=F32)


def _nt(a, b):
    return lax.dot_general(a, b, (((1,), (1,)), ((), ())), preferred_element_type=F32)


def _tn(a, b):
    return lax.dot_general(a, b, (((0,), (0,)), ((), ())), preferred_element_type=F32)


ACT = BF16


def _bf(a):
    return a.astype(ACT)


def _sig(x):
    return jax.nn.sigmoid(x)


def _split3(x):
    h = x.astype(BF16)
    r = x - h.astype(F32)
    m = r.astype(BF16)
    lo = (r - m.astype(F32)).astype(BF16)
    return h, m, lo


def _nn3(m01, x):
    h, m, lo = _split3(x)
    return _nn(m01, h) + _nn(m01, m) + _nn(m01, lo)


def _nn3r(x, m01):
    h, m, lo = _split3(x)
    return _nn(h, m01) + _nn(m, m01) + _nn(lo, m01)


def _full(shape):
    nd = len(shape)
    return pl.BlockSpec(shape, lambda *a: (0,) * nd, pipeline_mode=pl.Buffered(1))


def _whole(shape):
    nd = len(shape)
    return pl.BlockSpec(shape, lambda *a: (0,) * nd)


def _rows(tm, width):
    return pl.BlockSpec((tm, width), lambda i: (i, 0))


def _cols(height, tm):
    return pl.BlockSpec((height, tm), lambda i: (0, i))


def _ctx_lat(width):
    return pl.BlockSpec((1, 1, width), lambda i: (jnp.minimum(i, 1), 0, 0))


def _acc_ctx_lat(ref, i, val):
    @pl.when(i <= 1)
    def _():
        ref[...] = val.reshape(ref.shape)

    @pl.when(i > 1)
    def _():
        ref[...] += val.reshape(ref.shape)


def _acc_all(ref, i, val):
    @pl.when(i == 0)
    def _():
        ref[...] = val.reshape(ref.shape)

    @pl.when(i > 0)
    def _():
        ref[...] += val.reshape(ref.shape)


def _tile(n, cap):
    best = None
    for t in range(128, min(n, cap) + 1, 128):
        if n % t == 0:
            best = t
    return n if best is None else best


def _norm_mod(xv, g, shift, scale):
    r = lax.rsqrt(jnp.mean(xv * xv, axis=-1, keepdims=True) + EPS)
    xhat = xv * r
    n = xhat * g
    return r, xhat, n, n * (1.0 + scale) + shift


def _norm_mod_bwd(dh, r, xhat, n, g, scale):
    dshift = jnp.sum(dh, axis=0, keepdims=True)
    dscale = jnp.sum(dh * n, axis=0, keepdims=True)
    dn = dh * (1.0 + scale)
    dg = jnp.sum(dn * xhat, axis=0, keepdims=True)
    dxh = dn * g
    dx = r * (dxh - xhat * jnp.mean(dxh * xhat, axis=-1, keepdims=True))
    return dx, dshift, dscale, dg


def _stream(x):
    if isinstance(x, tuple):
        return list(x), x[0].shape[0] + x[1].shape[0], x[0].shape[1]
    return [x], x.shape[0], x.shape[1]


def _stream_specs(x, tm, dm):
    if isinstance(x, tuple):
        return [pl.BlockSpec((tm, dm), lambda i: (0, 0)), pl.BlockSpec((tm, dm), lambda i: (jnp.maximum(i - 1, 0), 0))]
    return [_rows(tm, dm)]


def _stream_tile(refs):
    if len(refs) == 2:
        return jnp.where(pl.program_id(0) == 0, refs[0][...], refs[1][...])
    return refs[0][...]


def _pre_fwd(x, gain, ms, w, splits, tm, name, ex=None, out_dtype=F32, qk=None):
    xs, T, dm = _stream(x)
    nx = len(xs)
    nt = T // tm
    nq = 0 if qk is None else 3
    ns = len(splits)

    def body(*refs):
        g_ref, ms_ref, w_ref = refs[nx:nx + 3]
        outs = refs[nx + 3 + nq:]
        ms_v = ms_ref[0]
        h = _norm_mod(_stream_tile(refs[:nx]), g_ref[...], ms_v[:, :dm], ms_v[:, dm:])[3]
        hb = _bf(h)
        for k, ((s, e), o_ref) in enumerate(zip(splits, outs[:ns])):
            part = _nn(hb, w_ref[:, s:e])
            o_ref[...] = part.astype(o_ref.dtype)
            if k == 0 and qk is not None:
                gq_ref, c_ref, s_ref = refs[nx + 3:nx + 6]
                _qk_tile_fwd(part, gq_ref, c_ref[...], s_ref[...], *outs[ns:])

    in_specs = _stream_specs(x, tm, dm) + [_full((1, dm)), _ctx_lat(2 * dm), _full(w.shape)]
    out_specs = [_rows(tm, e - s) for s, e in splits]
    out_shape = [jax.ShapeDtypeStruct((T, e - s), out_dtype) for s, e in splits]
    args = [*xs, gain, ms, w]
    if qk is not None:
        qw = ATTN_HEADS * HEAD_DIM
        in_specs += [_full(qk[0].shape), _rows(tm, PAIR), _rows(tm, PAIR)]
        args += list(qk)
        out_specs += [_rows(tm, qw), _rows(tm, PAIR), _rows(tm, PAIR)]
        out_shape += [jax.ShapeDtypeStruct((T, qw), ACT), jax.ShapeDtypeStruct((T, PAIR), ACT), jax.ShapeDtypeStruct((T, PAIR), ACT)]
    return _host_call(
        body, ex, lambda: pl.program_id(0) == 0, lambda: pl.program_id(0) == nt - 1,
        name=name, grid=(nt,), in_specs=in_specs, out_specs=out_specs, out_shape=out_shape,
        scratch_shapes=[], sem=("arbitrary",), args=tuple(args))


def _pre_bwd(x, dx_in, gain, ms, w, pieces, tm, name, latent_dx=False, ex=None, qk=None):
    xs, T, dm = _stream(x)
    nx = len(xs)
    dx_spec = pl.BlockSpec((tm, dm), lambda i: (jnp.maximum(i - 1, 0), 0)) if latent_dx else _rows(tm, dm)
    dx_rows = T - tm if latent_dx else T
    n_out = w.shape[1]
    flat = [a for _, arrs in pieces for a in arrs]
    nq = 0 if qk is None else 6
    qkw = (ATTN_HEADS + ATTN_KV) * HEAD_DIM

    def body(*refs):
        dxin_ref, g_ref, ms_ref, w_ref = refs[nx:nx + 4]
        rest = refs[nx + 4:]
        p_refs = rest[:len(flat)]
        qk_refs = rest[len(flat):len(flat) + nq]
        dx_ref, h_ref, dp_ref, dms_ref, dg_ref = rest[len(flat) + nq:len(flat) + nq + 5]
        i = pl.program_id(0)
        ms_v = ms_ref[0]
        g = g_ref[...]
        scale = ms_v[:, dm:]
        r, xhat, n, h = _norm_mod(_stream_tile(refs[:nx]), g, ms_v[:, :dm], scale)
        h_ref[...] = _bf(h).T
        dh = jnp.zeros((tm, dm), F32)
        if qk is not None:
            dq_ref, dk_ref, pa_ref, gq_ref, c_ref, s_ref = qk_refs
            dqk, dgs = _qk_tile_bwd(dq_ref, dk_ref, pa_ref, gq_ref, c_ref[...], s_ref[...])
            dgq_ref = rest[len(flat) + nq + 5]
            for p, dgp in enumerate(dgs):
                _acc_all(dgq_ref.at[p], i, dgp)
            vb = _bf(dqk)
            dp_ref[:, :qkw] = vb
            dh = dh + _nt(vb, w_ref[:, :qkw])
        k = 0
        for s, arrs in pieces:
            v = p_refs[k][...].astype(F32)
            for j in range(1, len(arrs)):
                v = v + p_refs[k + j][...].astype(F32)
            k += len(arrs)
            vb = _bf(v)
            wd = vb.shape[1]
            dp_ref[:, s:s + wd] = vb
            dh = dh + _nt(vb, w_ref[:, s:s + wd])
        dx, dshift, dscale, dg = _norm_mod_bwd(dh, r, xhat, n, g, scale)
        dx_ref[...] = dxin_ref[...] + dx
        _acc_ctx_lat(dms_ref, i, jnp.concatenate([dshift, dscale], axis=1))
        _acc_all(dg_ref, i, dg)

    nt = T // tm
    in_specs = (_stream_specs(x, tm, dm) + [_rows(tm, dm), _full((1, dm)), _ctx_lat(2 * dm), _full(w.shape)]
                + [_rows(tm, a.shape[1]) for a in flat])
    out_specs = [dx_spec, _cols(dm, tm), _rows(tm, n_out), _ctx_lat(2 * dm), _whole((1, dm))]
    out_shape = [jax.ShapeDtypeStruct((dx_rows, dm), F32), jax.ShapeDtypeStruct((dm, T), ACT),
                 jax.ShapeDtypeStruct((T, n_out), ACT), jax.ShapeDtypeStruct((2, 1, 2 * dm), F32),
                 jax.ShapeDtypeStruct((1, dm), F32)]
    args = [*xs, dx_in, gain, ms, w, *flat]
    if qk is not None:
        dq, dk, pa, gains, cosp, sinp = qk
        in_specs += [_rows(tm, dq.shape[1]), _rows(tm, PAIR), _rows(tm, qkw), _full(gains.shape), _rows(tm, PAIR), _rows(tm, PAIR)]
        args += [dq, dk, pa, gains, cosp, sinp]
        out_specs.append(_whole(gains.shape))
        out_shape.append(jax.ShapeDtypeStruct(gains.shape, F32))
    return _host_call(
        body, ex, lambda: pl.program_id(0) == 0, lambda: pl.program_id(0) == nt - 1,
        name=name, grid=(nt,), in_specs=in_specs, out_specs=out_specs, out_shape=out_shape,
        scratch_shapes=[], sem=("arbitrary",), args=tuple(args))


def _ffn_fwd(x1, gain, ms, w_in, w_out, tm, name, target=None, ex=None):
    T, dm = x1.shape
    fh = w_out.shape[0]
    head = target is not None

    def body(*refs):
        if head:
            x_ref, g_ref, ms_ref, wi_ref, wo_ref, t_ref, x2_ref, u_ref, f_ref, loss_ref = refs
        else:
            x_ref, g_ref, ms_ref, wi_ref, wo_ref, x2_ref, u_ref, f_ref = refs
        ms_v = ms_ref[0]
        xv = x_ref[...]
        h = _norm_mod(xv, g_ref[...], ms_v[:, :dm], ms_v[:, dm:2 * dm])[3]
        u = _nn(_bf(h), wi_ref[...])
        u_ref[...] = _bf(u)
        gt = u[:, :fh]
        act = gt * _sig(gt) * u[:, fh:]
        f = _nn(_bf(act), wo_ref[...])
        f_ref[...] = _bf(f)
        x2 = xv + ms_v[:, 2 * dm:] * f
        if head:
            i = pl.program_id(0)
            e = x2 - t_ref[...]
            x2_ref[...] = jnp.where(i > 0, e * (1.0 / dm), 0.0)
            _acc_all(loss_ref, i, jnp.where(i > 0, jnp.sum(e * e) * (0.5 / dm), 0.0))
        else:
            x2_ref[...] = x2

    ins = [x1, gain, ms, w_in, w_out]
    in_specs = [_rows(tm, dm), _full((1, dm)), _ctx_lat(3 * dm), _full(w_in.shape), _full(w_out.shape)]
    out_specs = [_rows(tm, dm), _rows(tm, 2 * fh), _rows(tm, dm)]
    out_shape = [jax.ShapeDtypeStruct((T, dm), F32), jax.ShapeDtypeStruct((T, 2 * fh), ACT), jax.ShapeDtypeStruct((T, dm), ACT)]
    if head:
        ins.append(target)
        in_specs.append(pl.BlockSpec((tm, dm), lambda i: (jnp.maximum(i - 1, 0), 0)))
        out_specs.append(_whole((1, 1)))
        out_shape.append(jax.ShapeDtypeStruct((1, 1), F32))
    nt = T // tm
    return _host_call(
        body, ex, lambda: pl.program_id(0) == 0, lambda: pl.program_id(0) == nt - 1,
        name=name, grid=(nt,), in_specs=in_specs, out_specs=out_specs, out_shape=out_shape,
        scratch_shapes=[], sem=("arbitrary",), args=tuple(ins))


def _ffn_bwd(x1, dx2, u, f, gain, ms, w_in, w_out, tm, name, ex=None):
    T, dm = x1.shape
    fh = w_out.shape[0]

    def body(x_ref, dx2_ref, u_ref, f_ref, g_ref, ms_ref, wi_ref, wo_ref,
             dx1_ref, h_ref, du_ref, act_ref, df_ref, dms_ref, dg_ref):
        i = pl.program_id(0)
        ms_v = ms_ref[0]
        g = g_ref[...]
        scale = ms_v[:, dm:2 * dm]
        gate = ms_v[:, 2 * dm:]
        r, xhat, n, h = _norm_mod(x_ref[...], g, ms_v[:, :dm], scale)
        h_ref[...] = _bf(h).T
        dx2 = dx2_ref[...]
        dgate = jnp.sum(dx2 * f_ref[...].astype(F32), axis=0, keepdims=True)
        dfb = _bf(dx2 * gate)
        df_ref[...] = dfb
        da = _nt(dfb, wo_ref[...])
        uv = u_ref[...].astype(F32)
        gt = uv[:, :fh]
        up = uv[:, fh:]
        s = _sig(gt)
        sg = gt * s
        act_ref[...] = _bf(sg * up).T
        dgt = _bf(da * up * (s * (1.0 + gt * (1.0 - s))))
        dup = _bf(da * sg)
        du_ref[:, :fh] = dgt
        du_ref[:, fh:] = dup
        dh = _nt(dgt, wi_ref[:, :fh]) + _nt(dup, wi_ref[:, fh:])
        dx, dshift, dscale, dg = _norm_mod_bwd(dh, r, xhat, n, g, scale)
        dx1_ref[...] = dx2 + dx
        _acc_ctx_lat(dms_ref, i, jnp.concatenate([dshift, dscale, dgate], axis=1))
        _acc_all(dg_ref, i, dg)

    nt = T // tm
    return _host_call(
        body, ex, lambda: pl.program_id(0) == 0, lambda: pl.program_id(0) == nt - 1,
        name=name, grid=(nt,),
        in_specs=[_rows(tm, dm), _rows(tm, dm), _rows(tm, 2 * fh), _rows(tm, dm), _full((1, dm)), _ctx_lat(3 * dm),
                  _full(w_in.shape), _full(w_out.shape)],
        out_specs=[_rows(tm, dm), _cols(dm, tm), _rows(tm, 2 * fh), _cols(fh, tm), _rows(tm, dm),
                   _ctx_lat(3 * dm), _whole((1, dm))],
        out_shape=[jax.ShapeDtypeStruct((T, dm), F32), jax.ShapeDtypeStruct((dm, T), ACT),
                   jax.ShapeDtypeStruct((T, 2 * fh), ACT), jax.ShapeDtypeStruct((fh, T), ACT),
                   jax.ShapeDtypeStruct((T, dm), ACT), jax.ShapeDtypeStruct((2, 1, 3 * dm), F32),
                   jax.ShapeDtypeStruct((1, dm), F32)],
        scratch_shapes=[], sem=("arbitrary",), args=(x1, dx2, u, f, gain, ms, w_in, w_out))


def _wgrad(a_t, b, name, rows=None, ex=None):
    T = a_t.shape[1]
    r0, K = (0, a_t.shape[0]) if rows is None else rows
    N = b.shape[1]
    tk, tn, tt = _tile(K, 1408), _tile(N, 1664), _tile(T, 2816)
    nt = T // tt
    assert r0 % tk == 0
    off = r0 // tk
    nk, nn = K // tk, N // tn

    def body(a_ref, b_ref, o_ref, acc_ref):
        t = pl.program_id(2)
        part = _nn(a_ref[...], b_ref[...])

        @pl.when(t == 0)
        def _():
            acc_ref[...] = part

        @pl.when(t > 0)
        def _():
            acc_ref[...] += part

        @pl.when(t == nt - 1)
        def _():
            o_ref[...] = acc_ref[...].astype(o_ref.dtype)

    def at(i, j, t):
        return (pl.program_id(0) == i) & (pl.program_id(1) == j) & (pl.program_id(2) == t)

    outs, got = _host_call(
        body, ex, lambda: at(0, 0, 0), lambda: at(nk - 1, nn - 1, nt - 1),
        name=name, grid=(nk, nn, nt),
        in_specs=[pl.BlockSpec((tk, tt), lambda i, j, t: (i + off, t)), pl.BlockSpec((tt, tn), lambda i, j, t: (t, j))],
        out_specs=[pl.BlockSpec((tk, tn), lambda i, j, t: (i, j))],
        out_shape=[jax.ShapeDtypeStruct((K, N), ACT)],
        scratch_shapes=[pltpu.VMEM((tk, tn), F32)], sem=("arbitrary", "arbitrary", "arbitrary"), args=(a_t, b))
    return outs[0] if ex is None else (outs[0], got)


def _post_fwd(x, o_fw, o_bw, g_src, g_blk, gain, a, w_out, ms, dvh, tm, name):
    xs, T, dm = _stream(x)
    nx = len(xs)
    hv = o_fw.shape[1]
    aw = 0 if a is None else a.shape[1]
    has_gain = gain is not None

    def body(*refs):
        refs = list(refs)
        x_refs = refs[:nx]
        of_ref, ob_ref, g_ref = refs[nx:nx + 3]
        k = nx + 3
        gain_ref = a_ref = None
        if has_gain:
            gain_ref = refs[k]
            k += 1
        if aw:
            a_ref = refs[k]
            k += 1
        w_ref, ms_ref, x1_ref, z_ref, yp_ref = refs[k:k + 5]
        o = of_ref[...].astype(F32) + ob_ref[...].astype(F32)
        gr = g_ref[...].astype(F32)
        if aw:
            z_ref[:, :aw] = _bf(a_ref[...])
        for hd in range(hv // dvh):
            sl = slice(hd * dvh, (hd + 1) * dvh)
            oh = o[:, sl]
            gh = gr[:, sl]
            r = lax.rsqrt(jnp.mean(oh * oh, axis=-1, keepdims=True) + EPS)
            y = oh * r
            if has_gain:
                y = y * gain_ref[...]
            y = y * (gh * _sig(gh))
            z_ref[:, aw + hd * dvh:aw + (hd + 1) * dvh] = _bf(y)
        yp = _nn(z_ref[...], w_ref[...])
        yp_ref[...] = _bf(yp)
        x1_ref[...] = _stream_tile(x_refs) + ms_ref[0] * yp

    ins = xs + [o_fw, o_bw, g_src]
    specs = _stream_specs(x, tm, dm) + [_rows(tm, hv), _rows(tm, hv), pl.BlockSpec((tm, hv), lambda i: (i, g_blk))]
    if has_gain:
        ins.append(gain)
        specs.append(_full(gain.shape))
    if aw:
        ins.append(a)
        specs.append(_rows(tm, aw))
    ins += [w_out, ms]
    specs += [_full(w_out.shape), _ctx_lat(dm)]
    return pl.pallas_call(
        body, name=name, grid=(T // tm,), in_specs=specs,
        out_specs=[_rows(tm, dm), _rows(tm, aw + hv), _rows(tm, dm)],
        out_shape=[jax.ShapeDtypeStruct((T, dm), F32), jax.ShapeDtypeStruct((T, aw + hv), ACT),
                   jax.ShapeDtypeStruct((T, dm), ACT)],
        compiler_params=_cp("arbitrary"),
    )(*ins)


def _post_bwd(dx1, z, yp, o_fw, o_bw, g_src, g_blk, gain, w_out, ms, aw, dvh, tm, name):
    T, dm = dx1.shape
    hv = o_fw.shape[1]
    has_gain = gain is not None

    def body(*refs):
        refs = list(refs)
        dx1_ref, z_ref, yp_ref, of_ref, ob_ref, g_ref = refs[:6]
        k = 6
        gain_ref = None
        if has_gain:
            gain_ref = refs[k]
            k += 1
        w_ref, ms_ref = refs[k:k + 2]
        k += 2
        do_ref, dgr_ref = refs[k:k + 2]
        k += 2
        da_ref = None
        if aw:
            da_ref = refs[k]
            k += 1
        dy_ref, zt_ref, dgate_ref, dgain_ref = refs[k:k + 4]
        i = pl.program_id(0)
        dx1v = dx1_ref[...]
        zt_ref[...] = z_ref[...].T
        _acc_ctx_lat(dgate_ref, i, jnp.sum(dx1v * yp_ref[...].astype(F32), axis=0, keepdims=True))
        dyb = _bf(dx1v * ms_ref[0])
        dy_ref[...] = dyb
        dz = _nt(dyb, w_ref[...])
        if aw:
            da_ref[...] = dz[:, :aw]
        o = of_ref[...].astype(F32) + ob_ref[...].astype(F32)
        gr = g_ref[...].astype(F32)
        dgain = jnp.zeros((1, dvh), F32)
        for hd in range(hv // dvh):
            sl = slice(hd * dvh, (hd + 1) * dvh)
            oh = o[:, sl]
            gh = gr[:, sl]
            dyh = dz[:, aw + hd * dvh:aw + (hd + 1) * dvh]
            r = lax.rsqrt(jnp.mean(oh * oh, axis=-1, keepdims=True) + EPS)
            n = oh * r
            s = _sig(gh)
            sl_g = gh * s
            gn = gain_ref[...] if has_gain else 1.0
            dgr_ref[:, sl] = _bf(dyh * n * gn * (s * (1.0 + gh * (1.0 - s))))
            dn = dyh * gn * sl_g
            dgain = dgain + jnp.sum(dyh * n * sl_g, axis=0, keepdims=True)
            do_ref[:, sl] = _bf(r * (dn - n * jnp.mean(dn * n, axis=-1, keepdims=True)))
        _acc_all(dgain_ref, i, dgain)

    ins = [dx1, z, yp, o_fw, o_bw, g_src]
    specs = [_rows(tm, dm), _rows(tm, aw + hv), _rows(tm, dm), _rows(tm, hv), _rows(tm, hv),
             pl.BlockSpec((tm, hv), lambda i: (i, g_blk))]
    if has_gain:
        ins.append(gain)
        specs.append(_full(gain.shape))
    ins += [w_out, ms]
    specs += [_full(w_out.shape), _ctx_lat(dm)]
    out_specs = [_rows(tm, hv), _rows(tm, hv)]
    out_shape = [jax.ShapeDtypeStruct((T, hv), ACT), jax.ShapeDtypeStruct((T, hv), ACT)]
    if aw:
        out_specs.append(_rows(tm, aw))
        out_shape.append(jax.ShapeDtypeStruct((T, aw), F32))
    out_specs += [_rows(tm, dm), _cols(aw + hv, tm), _ctx_lat(dm), _whole((1, dvh))]
    out_shape += [jax.ShapeDtypeStruct((T, dm), ACT), jax.ShapeDtypeStruct((aw + hv, T), ACT),
                  jax.ShapeDtypeStruct((2, 1, dm), F32), jax.ShapeDtypeStruct((1, dvh), F32)]
    return pl.pallas_call(
        body, name=name, grid=(T // tm,), in_specs=specs, out_specs=out_specs, out_shape=out_shape,
        compiler_params=_cp("arbitrary"),
    )(*ins)


def _loss_bwd(x, target, tm, name):
    T, dm = x.shape

    def body(x_ref, t_ref, dx_ref, loss_ref):
        i = pl.program_id(0)

        @pl.when(i == 0)
        def _():
            dx_ref[...] = jnp.zeros_like(dx_ref)
            loss_ref[...] = jnp.zeros_like(loss_ref)

        @pl.when(i > 0)
        def _():
            e = x_ref[...] - t_ref[...]
            dx_ref[...] = e * (1.0 / dm)
            loss_ref[...] += jnp.sum(e * e) * (0.5 / dm)

    return pl.pallas_call(
        body, name=name, grid=(T // tm,),
        in_specs=[_rows(tm, dm), pl.BlockSpec((tm, dm), lambda i: (jnp.maximum(i - 1, 0), 0))],
        out_specs=[_rows(tm, dm), _whole((1, 1))],
        out_shape=[jax.ShapeDtypeStruct((T, dm), F32), jax.ShapeDtypeStruct((1, 1), F32)],
        compiler_params=_cp("arbitrary"),
    )(x, target)


def _swap_matrix():
    r = lax.broadcasted_iota(jnp.int32, (HEAD_DIM, HEAD_DIM), 0)
    c = lax.broadcasted_iota(jnp.int32, (HEAD_DIM, HEAD_DIM), 1)
    return jnp.where((r + HEAD_DIM // 2) % HEAD_DIM == c, 1.0, 0.0).astype(BF16)


def _qk_prep_fwd(raw, gains, cos2, sin2, tq, name):
    nh, T, hd = raw.shape

    def body(x_ref, g_ref, c_ref, s_ref, o_ref):
        hidx = pl.program_id(0)
        xv = x_ref[0]
        r = lax.rsqrt(jnp.mean(xv * xv, axis=-1, keepdims=True) + EPS)
        n = xv * r * g_ref[0]
        y = n * c_ref[...] + _nn3r(n, _swap_matrix()) * s_ref[...]
        sc = jnp.where(hidx < ATTN_HEADS, HEAD_DIM ** -0.5, 1.0)
        o_ref[0] = _bf(y * sc)

    return pl.pallas_call(
        body, name=name, grid=(nh, T // tq),
        in_specs=[pl.BlockSpec((1, tq, hd), lambda h, i: (h, i, 0)), pl.BlockSpec((1, 1, hd), lambda h, i: (h, 0, 0)),
                  pl.BlockSpec((tq, hd), lambda h, i: (i, 0)), pl.BlockSpec((tq, hd), lambda h, i: (i, 0))],
        out_specs=pl.BlockSpec((1, tq, hd), lambda h, i: (h, i, 0)),
        out_shape=jax.ShapeDtypeStruct((nh, T, hd), ACT),
        compiler_params=_cp("arbitrary", "arbitrary"),
    )(raw, gains, cos2, sin2)


def _qk_prep_bwd(dy, raw, gains, cos2, sin2, tq, name):
    nh, T, hd = raw.shape

    def body(dy_ref, x_ref, g_ref, c_ref, s_ref, dx_ref, dg_ref):
        hidx = pl.program_id(0)
        i = pl.program_id(1)
        xv = x_ref[0]
        g = g_ref[0]
        r = lax.rsqrt(jnp.mean(xv * xv, axis=-1, keepdims=True) + EPS)
        xhat = xv * r
        sc = jnp.where(hidx < ATTN_HEADS, HEAD_DIM ** -0.5, 1.0)
        dyv = dy_ref[0] * sc
        dn = dyv * c_ref[...] + _nn3r(dyv * s_ref[...], _swap_matrix())
        _acc_all(dg_ref, i, jnp.sum(dn * xhat, axis=0, keepdims=True))
        dxh = dn * g
        dx_ref[0] = r * (dxh - xhat * jnp.mean(dxh * xhat, axis=-1, keepdims=True))

    return pl.pallas_call(
        body, name=name, grid=(nh, T // tq),
        in_specs=[pl.BlockSpec((1, tq, hd), lambda h, i: (h, i, 0)), pl.BlockSpec((1, tq, hd), lambda h, i: (h, i, 0)),
                  pl.BlockSpec((1, 1, hd), lambda h, i: (h, 0, 0)),
                  pl.BlockSpec((tq, hd), lambda h, i: (i, 0)), pl.BlockSpec((tq, hd), lambda h, i: (i, 0))],
        out_specs=[pl.BlockSpec((1, tq, hd), lambda h, i: (h, i, 0)), pl.BlockSpec((1, 1, hd), lambda h, i: (h, 0, 0))],
        out_shape=[jax.ShapeDtypeStruct((nh, T, hd), F32), jax.ShapeDtypeStruct((nh, 1, hd), F32)],
        compiler_params=_cp("arbitrary", "arbitrary"),
    )(dy, raw, gains, cos2, sin2)


def _attn_scores(q, k_ref, i, lc, T, sink):
    blk = ATTN_BLOCK
    kc = k_ref[0, pl.ds(blk, lc), :]
    kw = k_ref[0, pl.ds(pl.multiple_of(i * blk, blk), 3 * blk), :]
    s_c = _nt(q, kc)
    s_w = _nt(q, kw)
    row = lax.broadcasted_iota(jnp.int32, (4 * blk, 1), 0)
    qpos = i * blk + (row & (blk - 1))
    kpos = (i - 1) * blk + lax.broadcasted_iota(jnp.int32, (1, 3 * blk), 1)
    valid = (qpos >= lc) & (kpos >= lc) & (kpos < T) & (jnp.abs(kpos - qpos) <= WINDOW)
    s_w = jnp.where(valid, s_w, NEG)
    return kc, kw, s_c, s_w


def _attn_fwd(qt, kp, vp, sinkb, lc, name, ex=None):
    nh, T, hd = qt.shape
    blk = ATTN_BLOCK
    g = nh // ATTN_KV

    def body(q_ref, k_ref, v_ref, sink_ref, o_ref, lse_ref):
        i = pl.program_id(1)
        q = q_ref[...].reshape(g * blk, hd)
        sink = sink_ref[0]
        kc, kw, s_c, s_w = _attn_scores(q, k_ref, i, lc, T, sink)
        m = jnp.maximum(jnp.maximum(jnp.max(s_c, axis=-1, keepdims=True), jnp.max(s_w, axis=-1, keepdims=True)), sink)
        e_c = jnp.exp(s_c - m)
        e_w = jnp.exp(s_w - m)
        den = jnp.exp(sink - m) + jnp.sum(e_c, axis=-1, keepdims=True) + jnp.sum(e_w, axis=-1, keepdims=True)
        inv = 1.0 / den
        vc = v_ref[0, pl.ds(blk, lc), :]
        vw = v_ref[0, pl.ds(pl.multiple_of(i * blk, blk), 3 * blk), :]
        o = _nn(_bf(e_c * inv), vc) + _nn(_bf(e_w * inv), vw)
        o_ref[...] = o.reshape(g, blk, hd)
        lse_ref[...] = (m + jnp.log(den)).reshape(g, blk, 1)

    nb = T // blk
    return _host_call(
        body, ex, lambda: (pl.program_id(0) == 0) & (pl.program_id(1) == 0),
        lambda: (pl.program_id(0) == ATTN_KV - 1) & (pl.program_id(1) == nb - 1),
        name=name, grid=(ATTN_KV, nb),
        in_specs=[pl.BlockSpec((g, blk, hd), lambda kv, i: (kv, i, 0)),
                  pl.BlockSpec((1, T + 2 * blk, hd), lambda kv, i: (kv, 0, 0)),
                  pl.BlockSpec((1, T + 2 * blk, hd), lambda kv, i: (kv, 0, 0)),
                  pl.BlockSpec((1, g * blk, 1), lambda kv, i: (kv, 0, 0))],
        out_specs=[pl.BlockSpec((g, blk, hd), lambda kv, i: (kv, i, 0)),
                   pl.BlockSpec((g, blk, 1), lambda kv, i: (kv, i, 0))],
        out_shape=[jax.ShapeDtypeStruct((nh, T, hd), F32), jax.ShapeDtypeStruct((nh, T, 1), F32)],
        scratch_shapes=[], sem=("arbitrary", "arbitrary"), args=(qt, kp, vp, sinkb))


def _attn_bwd(qt, kp, vp, sinkb, o, lse, do, lc, name):
    nh, T, hd = qt.shape
    blk = ATTN_BLOCK
    g = nh // ATTN_KV

    def body(q_ref, k_ref, v_ref, sink_ref, o_ref, lse_ref, do_ref, dq_ref, dk_ref, dv_ref, ds_ref):
        i = pl.program_id(1)

        @pl.when(i == 0)
        def _():
            dk_ref[...] = jnp.zeros_like(dk_ref)
            dv_ref[...] = jnp.zeros_like(dv_ref)
            ds_ref[...] = jnp.zeros_like(ds_ref)

        q = q_ref[...].reshape(g * blk, hd)
        sink = sink_ref[0]
        lse = lse_ref[...].reshape(g * blk, 1)
        dov = do_ref[...].reshape(g * blk, hd)
        delta = jnp.sum(dov * o_ref[...].reshape(g * blk, hd), axis=-1, keepdims=True)
        kc, kw, s_c, s_w = _attn_scores(q, k_ref, i, lc, T, sink)
        p_c = jnp.exp(s_c - lse)
        p_w = jnp.exp(s_w - lse)
        win = pl.ds(pl.multiple_of(i * blk, blk), 3 * blk)
        vc = v_ref[0, pl.ds(blk, lc), :]
        vw = v_ref[0, win, :]
        dob = _bf(dov)
        ds_c = _bf(p_c * (_nt(dob, vc) - delta))
        ds_w = _bf(p_w * (_nt(dob, vw) - delta))
        dsr = -jnp.exp(sink - lse) * delta
        for hh in range(g):
            ds_ref[0, hh:hh + 1, :] += jnp.sum(dsr[hh * blk:(hh + 1) * blk, :], axis=0, keepdims=True)
        dq_ref[...] = (_nn(ds_c, kc) + _nn(ds_w, kw)).reshape(g, blk, hd)
        dk_ref[0, pl.ds(blk, lc), :] += _tn(ds_c, q)
        dk_ref[0, win, :] += _tn(ds_w, q)
        dv_ref[0, pl.ds(blk, lc), :] += _tn(_bf(p_c), dob)
        dv_ref[0, win, :] += _tn(_bf(p_w), dob)

    qspec = pl.BlockSpec((g, blk, hd), lambda kv, i: (kv, i, 0))
    kspec = pl.BlockSpec((1, T + 2 * blk, hd), lambda kv, i: (kv, 0, 0))
    lspec = pl.BlockSpec((g, blk, 1), lambda kv, i: (kv, i, 0))
    return pl.pallas_call(
        body, name=name, grid=(ATTN_KV, T // blk),
        in_specs=[qspec, kspec, kspec, pl.BlockSpec((1, g * blk, 1), lambda kv, i: (kv, 0, 0)), qspec, lspec, qspec],
        out_specs=[qspec, kspec, kspec, pl.BlockSpec((1, g, 1), lambda kv, i: (kv, 0, 0))],
        out_shape=[jax.ShapeDtypeStruct((nh, T, hd), F32), jax.ShapeDtypeStruct((ATTN_KV, T + 2 * blk, hd), F32),
                   jax.ShapeDtypeStruct((ATTN_KV, T + 2 * blk, hd), F32), jax.ShapeDtypeStruct((ATTN_KV, g, 1), F32)],
        compiler_params=_cp("arbitrary", "arbitrary"),
    )(qt, kp, vp, sinkb, o, lse, do)


PAIR = 2 * HEAD_DIM
N_PAIRS = (ATTN_HEADS + ATTN_KV) // 2


def _lanes():
    return lax.broadcasted_iota(jnp.int32, (1, PAIR), 1)


def _swap32(v):
    first_half = (_lanes() & (HEAD_DIM // 2)) == 0
    return jnp.where(first_half, pltpu.roll(v, PAIR - HEAD_DIM // 2, 1), pltpu.roll(v, HEAD_DIM // 2, 1))


def _head_mean(v):
    r = lax.broadcasted_iota(jnp.int32, (PAIR, PAIR), 0)
    c = lax.broadcasted_iota(jnp.int32, (PAIR, PAIR), 1)
    same = jnp.where((r >= HEAD_DIM) == (c >= HEAD_DIM), 1.0, 0.0).astype(BF16)
    return _nn3r(v, same) * (1.0 / HEAD_DIM)


def _qk_tile_fwd(pa, g_ref, cosv, sinv, q_ref, k_ref, v_ref):
    qw = ATTN_HEADS * HEAD_DIM
    for p in range(N_PAIRS):
        xv = pa[:, p * PAIR:(p + 1) * PAIR]
        n = xv * lax.rsqrt(_head_mean(xv * xv) + EPS) * g_ref[p]
        y = n * cosv + _swap32(n) * sinv
        if p < N_PAIRS - 1:
            q_ref[:, p * PAIR:(p + 1) * PAIR] = _bf(y * HEAD_DIM ** -0.5)
        else:
            k_ref[...] = _bf(y)
    v_ref[...] = _bf(pa[:, qw + PAIR:])


def _qk_tile_bwd(dq_ref, dk_ref, pa_ref, g_ref, cosv, sinv):
    dxs, dgs = [], []
    for p in range(N_PAIRS):
        sl = slice(p * PAIR, (p + 1) * PAIR)
        xv = pa_ref[:, sl]
        r = lax.rsqrt(_head_mean(xv * xv) + EPS)
        xhat = xv * r
        dy = dq_ref[:, sl] * HEAD_DIM ** -0.5 if p < N_PAIRS - 1 else dk_ref[...]
        dn = dy * cosv + _swap32(dy * sinv)
        dgs.append(jnp.sum(dn * xhat, axis=0, keepdims=True))
        dxh = dn * g_ref[p]
        dxs.append(r * (dxh - xhat * _head_mean(dxh * xhat)))
    return jnp.concatenate(dxs, axis=1), dgs


def _qk_slab_fwd(pa, gains, cosp, sinp, tm, name):
    T = pa.shape[0]
    qw = ATTN_HEADS * HEAD_DIM

    def body(pa_ref, g_ref, c_ref, s_ref, q_ref, k_ref, v_ref):
        cosv, sinv = c_ref[...], s_ref[...]
        for p in range(N_PAIRS):
            xv = pa_ref[:, p * PAIR:(p + 1) * PAIR]
            n = xv * lax.rsqrt(_head_mean(xv * xv) + EPS) * g_ref[p]
            y = n * cosv + _swap32(n) * sinv
            if p < N_PAIRS - 1:
                q_ref[:, p * PAIR:(p + 1) * PAIR] = _bf(y * HEAD_DIM ** -0.5)
            else:
                k_ref[...] = _bf(y)
        v_ref[...] = _bf(pa_ref[:, qw + PAIR:])

    return pl.pallas_call(
        body, name=name, grid=(T // tm,),
        in_specs=[_rows(tm, pa.shape[1]), _full(gains.shape), _rows(tm, PAIR), _rows(tm, PAIR)],
        out_specs=[_rows(tm, qw), _rows(tm, PAIR), _rows(tm, PAIR)],
        out_shape=[jax.ShapeDtypeStruct((T, qw), ACT), jax.ShapeDtypeStruct((T, PAIR), ACT),
                   jax.ShapeDtypeStruct((T, PAIR), ACT)],
        compiler_params=_cp("arbitrary"),
    )(pa, gains, cosp, sinp)


def _qk_slab_bwd(dq, dk, pa, gains, cosp, sinp, tm, name):
    T = pa.shape[0]
    qw = ATTN_HEADS * HEAD_DIM

    def body(dq_ref, dk_ref, pa_ref, g_ref, c_ref, s_ref, dx_ref, dg_ref):
        i = pl.program_id(0)
        cosv, sinv = c_ref[...], s_ref[...]
        for p in range(N_PAIRS):
            sl = slice(p * PAIR, (p + 1) * PAIR)
            xv = pa_ref[:, sl]
            r = lax.rsqrt(_head_mean(xv * xv) + EPS)
            xhat = xv * r
            dy = dq_ref[:, sl] * HEAD_DIM ** -0.5 if p < N_PAIRS - 1 else dk_ref[...]
            dn = dy * cosv + _swap32(dy * sinv)
            _acc_all(dg_ref.at[p], i, jnp.sum(dn * xhat, axis=0, keepdims=True))
            dxh = dn * g_ref[p]
            dx_ref[:, sl] = r * (dxh - xhat * _head_mean(dxh * xhat))

    return pl.pallas_call(
        body, name=name, grid=(T // tm,),
        in_specs=[_rows(tm, qw), _rows(tm, PAIR), _rows(tm, qw + PAIR), _full(gains.shape), _rows(tm, PAIR), _rows(tm, PAIR)],
        out_specs=[_rows(tm, qw + PAIR), _whole(gains.shape)],
        out_shape=[jax.ShapeDtypeStruct((T, qw + PAIR), F32), jax.ShapeDtypeStruct(gains.shape, F32)],
        compiler_params=_cp("arbitrary"),
    )(dq, dk, pa, gains, cosp, sinp)


def _attn_window(ref, i, nb):
    blk = ATTN_BLOCK
    starts = [pl.multiple_of(jnp.clip(i + d, 0, nb - 1) * blk, blk) for d in (-1, 0, 1)]
    return starts, jnp.concatenate([ref[pl.ds(s, blk), :] for s in starts], axis=0)


GROUP_HEADS = 2


def _head_groups(n):
    g = ATTN_HEADS // ATTN_KV
    return [(kv, [kv * g + s + j for j in range(n)]) for kv in range(ATTN_KV) for s in range(0, g, n)]


def _attn_mask(i, lc, T, rows):
    blk = ATTN_BLOCK
    row = lax.broadcasted_iota(jnp.int32, (rows, 1), 0)
    qpos = i * blk + (row & (blk - 1))
    kpos = (i - 1) * blk + lax.broadcasted_iota(jnp.int32, (1, 3 * blk), 1)
    return (qpos >= lc) & (kpos >= lc) & (kpos < T) & (jnp.abs(kpos - qpos) <= WINDOW)


def _to_kv_half(v, head, kv):
    return v if head % 2 == kv else pltpu.roll(v, HEAD_DIM, 1)


def _attn_slab_fwd(qt, ks, vs, sinkb, lc, name, ex=None):
    T = qt.shape[0]
    blk = ATTN_BLOCK
    nb = T // blk
    g = ATTN_HEADS // ATTN_KV

    def body(q_ref, k_ref, v_ref, sink_ref, o_ref, lse_ref):
        i = pl.program_id(0)
        lane = _lanes()
        valid = _attn_mask(i, lc, T, GROUP_HEADS * blk)
        kc_all, vc = k_ref[0:lc, :], v_ref[0:lc, :]
        _, kw_all = _attn_window(k_ref, i, nb)
        _, vw = _attn_window(v_ref, i, nb)
        kc, kw = [], []
        for kv in range(ATTN_KV):
            mine = (lane >= kv * HEAD_DIM) & (lane < (kv + 1) * HEAD_DIM)
            kc.append(jnp.where(mine, kc_all, jnp.zeros_like(kc_all)))
            kw.append(jnp.where(mine, kw_all, jnp.zeros_like(kw_all)))
        groups = _head_groups(GROUP_HEADS)
        qg = [jnp.concatenate([_to_kv_half(q_ref[:, (h // 2) * PAIR:(h // 2 + 1) * PAIR], h, kv) for h in heads], axis=0)
              for kv, heads in groups]
        sinks = [sink_ref[kv, (heads[0] - kv * g) * blk:(heads[-1] + 1 - kv * g) * blk] for kv, heads in groups]
        s_c = [_nt(q, kc[kv]) for q, (kv, _) in zip(qg, groups)]
        s_w = [jnp.where(valid, _nt(q, kw[kv]), NEG) for q, (kv, _) in zip(qg, groups)]
        m = [jnp.maximum(jnp.maximum(jnp.max(a, axis=-1, keepdims=True), jnp.max(b, axis=-1, keepdims=True)), s)
             for a, b, s in zip(s_c, s_w, sinks)]
        e_c = [jnp.exp(a - mm) for a, mm in zip(s_c, m)]
        e_w = [jnp.exp(b - mm) for b, mm in zip(s_w, m)]
        den = [jnp.exp(s - mm) + jnp.sum(a, axis=-1, keepdims=True) + jnp.sum(b, axis=-1, keepdims=True)
               for s, mm, a, b in zip(sinks, m, e_c, e_w)]
        inv = [1.0 / d for d in den]
        og = [_nn(_bf(a * r), vc) + _nn(_bf(b * r), vw) for a, b, r in zip(e_c, e_w, inv)]
        placed = [None] * ATTN_HEADS
        for (kv, heads), o2, mm, d in zip(groups, og, m, den):
            lse_ref[heads[0]:heads[-1] + 1] = (mm + jnp.log(d)).reshape(len(heads), blk, 1)
            for j, h in enumerate(heads):
                placed[h] = _to_kv_half(o2[j * blk:(j + 1) * blk], h, kv)
        for p in range(ATTN_HEADS // 2):
            o_ref[:, p * PAIR:(p + 1) * PAIR] = jnp.where(lane < HEAD_DIM, placed[2 * p], placed[2 * p + 1])

    qw = ATTN_HEADS * HEAD_DIM
    return _host_call(
        body, ex, lambda: pl.program_id(0) == 0, lambda: pl.program_id(0) == nb - 1,
        name=name, grid=(nb,),
        in_specs=[_rows(blk, qw), _full((T, PAIR)), _full((T, PAIR)), _full(sinkb.shape)],
        out_specs=[_rows(blk, qw), pl.BlockSpec((ATTN_HEADS, blk, 1), lambda i: (0, i, 0))],
        out_shape=[jax.ShapeDtypeStruct((T, qw), F32), jax.ShapeDtypeStruct((ATTN_HEADS, T, 1), F32)],
        scratch_shapes=[], sem=("arbitrary",), args=(qt, ks, vs, sinkb))


def _attn_slab_bwd(qt, ks, vs, sinkb, o, lse, do, lc, name, ex=None):
    T = qt.shape[0]
    blk = ATTN_BLOCK
    nb = T // blk
    g = ATTN_HEADS // ATTN_KV

    def body(q_ref, k_ref, v_ref, sink_ref, o_ref, lse_ref, do_ref, dq_ref, dk_ref, dv_ref, ds_ref):
        i = pl.program_id(0)

        @pl.when(i == 0)
        def _():
            dk_ref[...] = jnp.zeros_like(dk_ref)
            dv_ref[...] = jnp.zeros_like(dv_ref)
            ds_ref[...] = jnp.zeros_like(ds_ref)

        lane = _lanes()
        valid = _attn_mask(i, lc, T, g * blk)
        kc_all, vc_all = k_ref[0:lc, :], v_ref[0:lc, :]
        starts, kw_all = _attn_window(k_ref, i, nb)
        _, vw_all = _attn_window(v_ref, i, nb)
        dq_pairs = [jnp.zeros((blk, PAIR), F32) for _ in range(ATTN_HEADS // 2)]
        for kv in range(ATTN_KV):
            mine = (lane >= kv * HEAD_DIM) & (lane < (kv + 1) * HEAD_DIM)

            def only(v):
                return jnp.where(mine, v, jnp.zeros_like(v))

            kc, kw, vc, vw = only(kc_all), only(kw_all), only(vc_all), only(vw_all)
            heads = [kv * g + j for j in range(g)]
            qs, dos, deltas = [], [], []
            for h in heads:
                sl = slice((h // 2) * PAIR, (h // 2 + 1) * PAIR)
                dov = do_ref[:, sl]
                qs.append(_to_kv_half(q_ref[:, sl], h, kv))
                dos.append(_bf(_to_kv_half(dov, h, kv)))
                own = (lane < HEAD_DIM) if h % 2 == 0 else (lane >= HEAD_DIM)
                deltas.append(jnp.sum(jnp.where(own, dov * o_ref[:, sl], 0.0), axis=-1, keepdims=True))
            q4, do4, delta = jnp.concatenate(qs, axis=0), jnp.concatenate(dos, axis=0), jnp.concatenate(deltas, axis=0)
            sink = sink_ref[kv]
            lse = lse_ref[kv * g:(kv + 1) * g].reshape(g * blk, 1)
            p_c = jnp.exp(_nt(q4, kc) - lse)
            p_w = jnp.exp(jnp.where(valid, _nt(q4, kw), NEG) - lse)
            ds_c = _bf(p_c * (_nt(do4, vc) - delta))
            ds_w = _bf(p_w * (_nt(do4, vw) - delta))
            dsr = -jnp.exp(sink - lse) * delta
            dq4 = _nn(ds_c, kc) + _nn(ds_w, kw)
            for j, h in enumerate(heads):
                ds_ref[h:h + 1, :] += jnp.sum(dsr[j * blk:(j + 1) * blk, :], axis=0, keepdims=True)
                dq_pairs[h // 2] = dq_pairs[h // 2] + _to_kv_half(dq4[j * blk:(j + 1) * blk], h, kv)
            dk_ref[0:lc, :] += only(_tn(ds_c, q4))
            dv_ref[0:lc, :] += only(_tn(_bf(p_c), do4))
            dkw = only(_tn(ds_w, q4))
            dvw = only(_tn(_bf(p_w), do4))
            for b, s in enumerate(starts):
                dk_ref[pl.ds(s, blk), :] += dkw[b * blk:(b + 1) * blk]
                dv_ref[pl.ds(s, blk), :] += dvw[b * blk:(b + 1) * blk]
        for p in range(ATTN_HEADS // 2):
            dq_ref[:, p * PAIR:(p + 1) * PAIR] = dq_pairs[p]

    qw = ATTN_HEADS * HEAD_DIM
    lspec = pl.BlockSpec((ATTN_HEADS, blk, 1), lambda i: (0, i, 0))
    return _host_call(
        body, ex, lambda: pl.program_id(0) == 0, lambda: pl.program_id(0) == nb - 1,
        name=name, grid=(nb,),
        in_specs=[_rows(blk, qw), _full((T, PAIR)), _full((T, PAIR)), _full(sinkb.shape), _rows(blk, qw), lspec,
                  _rows(blk, qw)],
        out_specs=[_rows(blk, qw), _whole((T, PAIR)), _whole((T, PAIR)), _whole((ATTN_HEADS, 1))],
        out_shape=[jax.ShapeDtypeStruct((T, qw), F32), jax.ShapeDtypeStruct((T, PAIR), F32),
                   jax.ShapeDtypeStruct((T, PAIR), F32), jax.ShapeDtypeStruct((ATTN_HEADS, 1), F32)],
        scratch_shapes=[], sem=("arbitrary",), args=(qt, ks, vs, sinkb, o, lse, do))


def _fw_chunk(s, nc, nt):
    return s


def _bw_chunk(s, nc, nt):
    return jnp.where(s < nc, nc - 1 - s, nt - 1 - (s - nc))


def _tri(c, rev):
    r = lax.broadcasted_iota(jnp.int32, (c, c), 0)
    k = lax.broadcasted_iota(jnp.int32, (c, c), 1)
    return (k >= r) if rev else (k <= r)


def _gla_gates(z, lb, rev):
    c = HG_CHUNK
    sg = _sig(z)
    f = lb + (1.0 - lb) * sg
    cum = _nn3(jnp.where(_tri(c, rev), 1.0, 0.0).astype(BF16), jnp.log(f))
    mid = c - 1 - c // 2 if rev else c // 2
    last = 0 if rev else c - 1
    return sg, f, cum, cum[mid:mid + 1], cum[last:last + 1], last


def _lower_bound(lbraw_ref):
    lr = lbraw_ref[...]
    return _sig(lr[0:1] - lr[1:2])


def _gla_fwd(pb, lbraw, lc, name, ex=None):
    T = pb.shape[0]
    c, hw, d = HG_CHUNK, HG_HEADS * HG_D, HG_D
    nt, nc = T // c, lc // c
    orders = (_fw_chunk, _bw_chunk)

    def body(qf, zf, vf, qb, zb, vb, lb_ref, of_ref, ob_ref, sf_ref, sb_ref, st_ref):
        @pl.when(pl.program_id(0) == 0)
        def _():
            st_ref[...] = jnp.zeros_like(st_ref)

        lb = _lower_bound(lb_ref)
        dirs = ((qf, zf, vf, of_ref, sf_ref), (qb, zb, vb, ob_ref, sb_ref))
        combos = [(dr, h, slice(h * d, (h + 1) * d)) for dr in range(2) for h in range(HG_HEADS)]
        prep = []
        for dr, (q_ref, z_ref, v_ref, _, _) in enumerate(dirs):
            rev = dr == 1
            qr = q_ref[...]
            q = qr * _sig(qr)
            _, f, cum, ref, last, _ = _gla_gates(z_ref[...], lb, rev)
            k = 1.0 - f
            prep.append(dict(q1=_bf(q * jnp.exp(cum - ref)), k1=_bf(k * jnp.exp(ref - cum)), q2=_bf(q * jnp.exp(cum)),
                             k2=_bf(k * jnp.exp(last - cum)), el=jnp.exp(last), v=_bf(v_ref[...]), mask=_tri(c, rev)))
        a = [_bf(jnp.where(prep[dr]["mask"], _nt(prep[dr]["q1"][:, sl], prep[dr]["k1"][:, sl]), 0.0)) for dr, _, sl in combos]
        for (dr, h, sl), a_h in zip(combos, a):
            p = prep[dr]
            o_ref, s_ref = dirs[dr][3], dirs[dr][4]
            st = st_ref[dr, h]
            stb = _bf(st)
            s_ref[0, h] = stb
            o_ref[:, sl] = _nn(a_h, p["v"][:, sl]) + _nt(p["q2"][:, sl], stb)
            st_ref[dr, h] = st * p["el"][:, sl] + _tn(p["v"][:, sl], p["k2"][:, sl])

    def col(order, blkcol):
        return pl.BlockSpec((c, hw), lambda s: (order(s, nc, nt), blkcol))

    def st_spec(order):
        return pl.BlockSpec((1, HG_HEADS, d, d), lambda s: (order(s, nc, nt), 0, 0, 0))

    in_specs = []
    for dr, order in enumerate(orders):
        in_specs += [col(order, 0), col(order, 1 + dr), col(order, 3)]
    in_specs.append(_full(lbraw.shape))
    return _host_call(
        body, ex, lambda: pl.program_id(0) == 0, lambda: pl.program_id(0) == nt - 1,
        name=name, grid=(nt,), in_specs=in_specs,
        out_specs=[col(_fw_chunk, 0), col(_bw_chunk, 0), st_spec(_fw_chunk), st_spec(_bw_chunk)],
        out_shape=[jax.ShapeDtypeStruct((T, hw), F32), jax.ShapeDtypeStruct((T, hw), F32),
                   jax.ShapeDtypeStruct((nt, HG_HEADS, d, d), ACT), jax.ShapeDtypeStruct((nt, HG_HEADS, d, d), ACT)],
        scratch_shapes=[pltpu.VMEM((2, HG_HEADS, d, d), F32)], sem=("arbitrary",),
        args=(pb, pb, pb, pb, pb, pb, lbraw))


def _gla_bwd(pb, lbraw, s_fw, s_bw, do, lc, name, ex=None):
    T = pb.shape[0]
    c, hw, d = HG_CHUNK, HG_HEADS * HG_D, HG_D
    nt, nc = T // c, lc // c

    def rfw(s, nc_, nt_):
        return _fw_chunk(nt_ - 1 - s, nc_, nt_)

    def rbw(s, nc_, nt_):
        return _bw_chunk(nt_ - 1 - s, nc_, nt_)

    def body(qf, zf, vf, sf, dof, qb, zb, vb, sb, dob_, lb_ref,
             dqf, dzf, dvf, dqb, dzb, dvb, dlb_ref, dst_ref):
        step = pl.program_id(0)

        @pl.when(step == 0)
        def _():
            dst_ref[...] = jnp.zeros_like(dst_ref)

        lb = _lower_bound(lb_ref)
        sets = ((qf, zf, vf, sf, dof, dqf, dzf, dvf), (qb, zb, vb, sb, dob_, dqb, dzb, dvb))
        combos = [(dr, h, slice(h * d, (h + 1) * d)) for dr in range(2) for h in range(HG_HEADS)]
        prep = []
        for dr, (q_ref, z_ref, v_ref, _, do_ref, _, _, _) in enumerate(sets):
            rev = dr == 1
            qr = q_ref[...]
            sq = _sig(qr)
            q = qr * sq
            sg, f, cum, ref, last, last_row = _gla_gates(z_ref[...], lb, rev)
            k = 1.0 - f
            e_qr, e_kr, e_q, e_kl = jnp.exp(cum - ref), jnp.exp(ref - cum), jnp.exp(cum), jnp.exp(last - cum)
            q1, k1, q2, k2 = q * e_qr, k * e_kr, q * e_q, k * e_kl
            prep.append(dict(qr=qr, sq=sq, sg=sg, f=f, e_qr=e_qr, e_kr=e_kr, e_q=e_q, e_kl=e_kl, el=jnp.exp(last),
                             q1=q1, k1=k1, q2=q2, k2=k2, q1b=_bf(q1), k1b=_bf(k1), q2b=_bf(q2), k2b=_bf(k2),
                             vb=_bf(v_ref[...]), dob=_bf(do_ref[...]), mask=_tri(c, rev), last_row=last_row,
                             acc_t=jnp.where(_tri(c, not rev), 1.0, 0.0).astype(BF16)))
        a = [_bf(jnp.where(prep[dr]["mask"], _nt(prep[dr]["q1b"][:, sl], prep[dr]["k1b"][:, sl]), 0.0)) for dr, _, sl in combos]
        da = [_bf(jnp.where(prep[dr]["mask"], _nt(prep[dr]["dob"][:, sl], prep[dr]["vb"][:, sl]), 0.0)) for dr, _, sl in combos]
        parts = [dict(dq1=[], dk1=[], dq2=[], dk2=[], dls=[]) for _ in range(2)]
        for (dr, h, sl), a_h, da_h in zip(combos, a, da):
            p = prep[dr]
            s_ref, dv_ref = sets[dr][3], sets[dr][7]
            stb = s_ref[0, h]
            dst = dst_ref[dr, h]
            dstb = _bf(dst)
            dob_h, vb_h = p["dob"][:, sl], p["vb"][:, sl]
            dv_ref[:, sl] = _bf(_tn(a_h, dob_h) + _nt(p["k2b"][:, sl], dstb))
            parts[dr]["dq1"].append(_nn(da_h, p["k1b"][:, sl]))
            parts[dr]["dk1"].append(_tn(da_h, p["q1b"][:, sl]))
            parts[dr]["dq2"].append(_nn(dob_h, stb))
            parts[dr]["dk2"].append(_nn(vb_h, dstb))
            el_h = p["el"][:, sl]
            dst_ref[dr, h] = _tn(dob_h, p["q2b"][:, sl]) + dst * el_h
            parts[dr]["dls"].append(jnp.sum(dst * stb.astype(F32), axis=0, keepdims=True) * el_h)
        dlb_tot = jnp.zeros((1, hw), F32)
        for dr in range(2):
            p = prep[dr]
            dq_ref, dz_ref = sets[dr][5], sets[dr][6]
            dq1, dk1, dq2, dk2, dls = (jnp.concatenate(parts[dr][n], axis=1) for n in ("dq1", "dk1", "dq2", "dk2", "dls"))
            dq = dq1 * p["e_qr"] + dq2 * p["e_q"]
            dk = dk1 * p["e_kr"] + dk2 * p["e_kl"]
            dcum = dq1 * p["q1"] - dk1 * p["k1"] + dq2 * p["q2"] - dk2 * p["k2"]
            dlast = jnp.sum(dk2 * p["k2"], axis=0, keepdims=True) + dls
            rowid = lax.broadcasted_iota(jnp.int32, (c, 1), 0)
            dcum = dcum + jnp.where(rowid == p["last_row"], dlast, 0.0)
            df = _nn3(p["acc_t"], dcum) / p["f"] - dk
            sg = p["sg"]
            dz_ref[...] = _bf(df * (1.0 - lb) * sg * (1.0 - sg))
            dlb_tot = dlb_tot + jnp.sum(df * (1.0 - sg), axis=0, keepdims=True)
            dq_ref[...] = _bf(dq * (p["sq"] * (1.0 + p["qr"] * (1.0 - p["sq"]))))
        _acc_all(dlb_ref, step, dlb_tot)

    def col(order, blkcol):
        return pl.BlockSpec((c, hw), lambda s: (order(s, nc, nt), blkcol))

    def st_spec(order):
        return pl.BlockSpec((1, HG_HEADS, d, d), lambda s: (order(s, nc, nt), 0, 0, 0))

    in_specs = []
    for dr, order in enumerate((rfw, rbw)):
        in_specs += [col(order, 0), col(order, 1 + dr), col(order, 3), st_spec(order), col(order, 0)]
    in_specs.append(_full(lbraw.shape))
    out_specs = [col(rfw, 0)] * 3 + [col(rbw, 0)] * 3 + [_whole((1, hw))]
    out_shape = [jax.ShapeDtypeStruct((T, hw), ACT)] * 6 + [jax.ShapeDtypeStruct((1, hw), F32)]
    return _host_call(
        body, ex, lambda: pl.program_id(0) == 0, lambda: pl.program_id(0) == nt - 1,
        name=name, grid=(nt,), in_specs=in_specs, out_specs=out_specs, out_shape=out_shape,
        scratch_shapes=[pltpu.VMEM((2, HG_HEADS, d, d), F32)], sem=("arbitrary",),
        args=(pb, pb, pb, s_fw, do, pb, pb, pb, s_bw, do, lbraw))


def _ret_log_gamma(h, rev):
    hh = RET_HEADS - 1 - h if rev else h
    return math.log(1.0 - 2.0 ** (-5.0 - hh))


def _rope(x, cos, sin):
    half = x.shape[1] // 2
    x1, x2 = x[:, :half], x[:, half:]
    return jnp.concatenate([x1 * cos - x2 * sin, x2 * cos + x1 * sin], axis=1)


def _unrope(dy, cos, sin):
    half = dy.shape[1] // 2
    d1, d2 = dy[:, :half], dy[:, half:]
    return jnp.concatenate([d1 * cos + d2 * sin, d2 * cos - d1 * sin], axis=1)


def _ret_decays(lg, rev):
    c = RET_CHUNK
    r = lax.broadcasted_iota(jnp.int32, (c, c), 0)
    k = lax.broadcasted_iota(jnp.int32, (c, c), 1)
    rel = (k - r) if rev else (r - k)
    dm = jnp.where(rel >= 0, jnp.exp(lg * jnp.maximum(rel, 0).astype(F32)), 0.0)
    pos = lax.broadcasted_iota(jnp.int32, (c, 1), 0).astype(F32)
    if rev:
        qdec = jnp.exp(lg * (c - pos))
        kdec = jnp.exp(lg * pos)
    else:
        qdec = jnp.exp(lg * (pos + 1.0))
        kdec = jnp.exp(lg * (c - 1.0 - pos))
    return dm, qdec, kdec


def _ret_fwd(q, k, v, cos, sin, lc, name, ex=None):
    T = q.shape[0]
    c, dk, dv = RET_CHUNK, RET_DK, RET_DV
    nt, nc = T // c, lc // c
    kscale = dk ** -0.5

    def body(qf, kf, vf, cf, sf_, qb, kb, vb, cb, sb_, of_ref, ob_ref, stf_ref, stb_ref, st_ref):
        @pl.when(pl.program_id(0) == 0)
        def _():
            st_ref[...] = jnp.zeros_like(st_ref)

        sets = ((qf, kf, vf, cf, sf_, of_ref, stf_ref), (qb, kb, vb, cb, sb_, ob_ref, stb_ref))
        combos = [(dr, h) for dr in range(2) for h in range(RET_HEADS)]
        prep = {}
        for dr, (q_ref, k_ref, v_ref, c_ref, s_ref, _, _) in enumerate(sets):
            rev = dr == 1
            cos_v, sin_v = c_ref[...], s_ref[...]
            for h in range(RET_HEADS):
                lg = _ret_log_gamma(h, rev)
                dm, qdec, kdec = _ret_decays(lg, rev)
                qh = _rope(q_ref[:, h * dk:(h + 1) * dk].astype(F32), cos_v, sin_v)
                kh = _rope(k_ref[:, h * dk:(h + 1) * dk].astype(F32), cos_v, sin_v) * kscale
                prep[dr, h] = dict(qb=_bf(qh), kb=_bf(kh), qin=_bf(qh * qdec), kin=_bf(kh * kdec),
                                   v=_bf(v_ref[:, h * dv:(h + 1) * dv]), dm=dm, decay=math.exp(lg * c))
        sc = {ch: _bf(_nt(prep[ch]["qb"], prep[ch]["kb"]) * prep[ch]["dm"]) for ch in combos}
        for dr, h in combos:
            p = prep[dr, h]
            o_ref, so_ref = sets[dr][5], sets[dr][6]
            st = st_ref[dr, h]
            stb = _bf(st)
            so_ref[0, h] = stb
            o_ref[:, h * dv:(h + 1) * dv] = _bf(_nn(sc[dr, h], p["v"]) + _nt(p["qin"], stb))
            st_ref[dr, h] = st * p["decay"] + _tn(p["v"], p["kin"])

    def spec(order, width):
        return pl.BlockSpec((c, width), lambda s: (order(s, nc, nt), 0))

    def st_spec(order):
        return pl.BlockSpec((1, RET_HEADS, dv, dk), lambda s: (order(s, nc, nt), 0, 0, 0))

    in_specs = []
    for order in (_fw_chunk, _bw_chunk):
        in_specs += [spec(order, RET_HEADS * dk), spec(order, RET_HEADS * dk), spec(order, RET_HEADS * dv),
                     spec(order, dk // 2), spec(order, dk // 2)]
    return _host_call(
        body, ex, lambda: pl.program_id(0) == 0, lambda: pl.program_id(0) == nt - 1,
        name=name, grid=(nt,), in_specs=in_specs,
        out_specs=[spec(_fw_chunk, RET_HEADS * dv), spec(_bw_chunk, RET_HEADS * dv), st_spec(_fw_chunk), st_spec(_bw_chunk)],
        out_shape=[jax.ShapeDtypeStruct((T, RET_HEADS * dv), ACT), jax.ShapeDtypeStruct((T, RET_HEADS * dv), ACT),
                   jax.ShapeDtypeStruct((nt, RET_HEADS, dv, dk), ACT), jax.ShapeDtypeStruct((nt, RET_HEADS, dv, dk), ACT)],
        scratch_shapes=[pltpu.VMEM((2, RET_HEADS, dv, dk), F32)], sem=("arbitrary",),
        args=(q, k, v, cos, sin, q, k, v, cos, sin))


def _ret_bwd(q, k, v, cos, sin, s_fw, s_bw, do, lc, name, ex=None):
    T = q.shape[0]
    c, dk, dv = RET_CHUNK, RET_DK, RET_DV
    nt, nc = T // c, lc // c
    kscale = dk ** -0.5

    def rfw(s, nc_, nt_):
        return _fw_chunk(nt_ - 1 - s, nc_, nt_)

    def rbw(s, nc_, nt_):
        return _bw_chunk(nt_ - 1 - s, nc_, nt_)

    def body(qf, kf, vf, cf, sf_, stf, dof, qb, kb, vb, cb, sb_, stb_, dob_,
             dqf, dkf, dvf, dqb, dkb, dvb, dst_ref):
        @pl.when(pl.program_id(0) == 0)
        def _():
            dst_ref[...] = jnp.zeros_like(dst_ref)

        sets = ((qf, kf, vf, cf, sf_, stf, dof, dqf, dkf, dvf), (qb, kb, vb, cb, sb_, stb_, dob_, dqb, dkb, dvb))
        combos = [(dr, h) for dr in range(2) for h in range(RET_HEADS)]
        prep = {}
        for dr, (q_ref, k_ref, v_ref, c_ref, s_ref, _, do_ref, _, _, _) in enumerate(sets):
            rev = dr == 1
            cos_v, sin_v = c_ref[...], s_ref[...]
            for h in range(RET_HEADS):
                lg = _ret_log_gamma(h, rev)
                dm, qdec, kdec = _ret_decays(lg, rev)
                qh = _rope(q_ref[:, h * dk:(h + 1) * dk].astype(F32), cos_v, sin_v)
                kh = _rope(k_ref[:, h * dk:(h + 1) * dk].astype(F32), cos_v, sin_v) * kscale
                prep[dr, h] = dict(qb=_bf(qh), kb=_bf(kh), qin=_bf(qh * qdec), kin=_bf(kh * kdec),
                                   v=_bf(v_ref[:, h * dv:(h + 1) * dv]), dob=_bf(do_ref[:, h * dv:(h + 1) * dv]),
                                   dm=dm, qdec=qdec, kdec=kdec, decay=math.exp(lg * c), cos=cos_v, sin=sin_v)
        sc = {ch: _bf(_nt(prep[ch]["qb"], prep[ch]["kb"]) * prep[ch]["dm"]) for ch in combos}
        dsc = {ch: _bf(_nt(prep[ch]["dob"], prep[ch]["v"]) * prep[ch]["dm"]) for ch in combos}
        carried = {}
        for dr, h in combos:
            p = prep[dr, h]
            dv_ref = sets[dr][9]
            dst = dst_ref[dr, h]
            dstb = _bf(dst)
            carried[dr, h] = dstb
            dv_ref[:, h * dv:(h + 1) * dv] = _bf(_tn(sc[dr, h], p["dob"]) + _nt(p["kin"], dstb))
            dst_ref[dr, h] = _tn(p["dob"], p["qin"]) + dst * p["decay"]
        for dr, h in combos:
            p = prep[dr, h]
            st_in, dq_ref, dk_ref = sets[dr][5], sets[dr][7], sets[dr][8]
            dq_r = _nn(dsc[dr, h], p["kb"]) + _nn(p["dob"], st_in[0, h]) * p["qdec"]
            dk_r = _tn(dsc[dr, h], p["qb"]) + _nn(p["v"], carried[dr, h]) * p["kdec"]
            dq_ref[:, h * dk:(h + 1) * dk] = _bf(_unrope(dq_r, p["cos"], p["sin"]))
            dk_ref[:, h * dk:(h + 1) * dk] = _bf(_unrope(dk_r * kscale, p["cos"], p["sin"]))

    def spec(order, width):
        return pl.BlockSpec((c, width), lambda s: (order(s, nc, nt), 0))

    def st_spec(order):
        return pl.BlockSpec((1, RET_HEADS, dv, dk), lambda s: (order(s, nc, nt), 0, 0, 0))

    in_specs = []
    for order in (rfw, rbw):
        in_specs += [spec(order, RET_HEADS * dk), spec(order, RET_HEADS * dk), spec(order, RET_HEADS * dv),
                     spec(order, dk // 2), spec(order, dk // 2), st_spec(order), spec(order, RET_HEADS * dv)]
    out_specs, out_shape = [], []
    for order in (rfw, rbw):
        out_specs += [spec(order, RET_HEADS * dk), spec(order, RET_HEADS * dk), spec(order, RET_HEADS * dv)]
        out_shape += [jax.ShapeDtypeStruct((T, RET_HEADS * dk), ACT), jax.ShapeDtypeStruct((T, RET_HEADS * dk), ACT),
                      jax.ShapeDtypeStruct((T, RET_HEADS * dv), ACT)]
    return _host_call(
        body, ex, lambda: pl.program_id(0) == 0, lambda: pl.program_id(0) == nt - 1,
        name=name, grid=(nt,), in_specs=in_specs, out_specs=out_specs, out_shape=out_shape,
        scratch_shapes=[pltpu.VMEM((2, RET_HEADS, dv, dk), F32)], sem=("arbitrary",),
        args=(q, k, v, cos, sin, s_fw, do, q, k, v, cos, sin, s_bw, do))


def _attn_rope_tables(lc, l):
    t = jnp.arange(l)
    row = (t // GRID_W).astype(F32)
    colp = (t % GRID_W).astype(F32)
    n_freq = HEAD_DIM // 4
    inv = 10000.0 ** (-jnp.arange(n_freq, dtype=F32) / n_freq)
    ang = jnp.concatenate([row[:, None] * inv, colp[:, None] * inv], axis=-1)
    cos = jnp.concatenate([jnp.ones((lc, HEAD_DIM // 2), F32), jnp.cos(ang)], axis=0)
    sin = jnp.concatenate([jnp.zeros((lc, HEAD_DIM // 2), F32), jnp.sin(ang)], axis=0)
    return jnp.concatenate([cos, cos], axis=1), jnp.concatenate([-sin, sin], axis=1)


def _ret_rope_tables(lc, l):
    theta = 1.0 / (10000.0 ** jnp.linspace(0.0, 1.0, RET_DK // 2, dtype=F32))
    ang = jnp.arange(l, dtype=F32)[:, None] * theta
    cos = jnp.concatenate([jnp.ones((lc, RET_DK // 2), F32), jnp.cos(ang)], axis=0)
    sin = jnp.concatenate([jnp.zeros((lc, RET_DK // 2), F32), jnp.sin(ang)], axis=0)
    return cos, sin


def _heads_major(slab, n_heads):
    t = slab.shape[0]
    return slab.reshape(t, n_heads, HEAD_DIM).transpose(1, 0, 2)


def _slab(hm):
    nh, t, hd = hm.shape
    return hm.transpose(1, 0, 2).reshape(t, nh * hd)


COL_SHARDED = ("ffn_in0", "ffn_in1", "even_in", "even_in_a", "even_in_b", "odd_in")


def _full_weight(name, g):
    if name in COL_SHARDED:
        return g.transpose(1, 0, 2).reshape(g.shape[1], -1)
    return g.reshape(-1, g.shape[2])


def _shard_slots(name, g):
    if name in COL_SHARDED:
        return g.reshape(g.shape[0], N_DEV, -1).transpose(1, 0, 2)
    return g.reshape(N_DEV, -1, g.shape[1])


def _local_step(xs, target, mv, norm_g, w, qk_g, sink, hg_out_g, lbraw, lc, shards=None):
    _, T, dm = _stream(xs)
    l = T - lc
    tm = lc
    blk = ATTN_BLOCK
    d2, d3 = 2 * dm, 3 * dm
    w = dict(w)
    gw, recv = {}, {}

    def ms(layer, a, b):
        return mv[layer, :, :, a:b]

    def gather(names):
        return None if shards is None else _Exchange(GATHER2, [shards[n] for n in names])

    def arrived(names, got):
        for n, g in zip(names, got):
            w[n] = _full_weight(n, g)

    def scatter(names):
        return None if shards is None else _Exchange(SCATTER, [_shard_slots(n, gw[n]) for n in names])

    def scattered(names, got):
        for n, g in zip(names, got):
            recv[n] = g

    g00, g01, g10, g11 = (norm_g[i, j][None, :] for i in (0, 1) for j in (0, 1))

    cos2, sin2 = _attn_rope_tables(lc, l)
    cosp, sinp = jnp.concatenate([cos2, cos2], axis=1), jnp.concatenate([sin2, sin2], axis=1)
    gains5 = jnp.concatenate([jnp.broadcast_to(jnp.tile(qk_g[0], 2), (N_PAIRS - 1, PAIR)), jnp.tile(qk_g[1], 2)[None]])[:, None, :]
    riding = ["even_out"]
    (pa, pb, qt, ks, vs), got = _pre_fwd(xs, g00, ms(0, 0, d2), w["even_in"], ((0, 768), (768, 3328)), tm, "pre0_fwd",
                                         gather(riding), qk=(gains5, cosp, sinp))
    arrived(riding, got)
    sinkb = jnp.broadcast_to(sink.reshape(ATTN_KV, 4, 1, 1), (ATTN_KV, 4, blk, 1)).reshape(ATTN_KV, 4 * blk, 1)
    riding = ["ffn_in0"]
    (a_slab, lse), got = _attn_slab_fwd(qt, ks, vs, sinkb, lc, "attn_fwd", gather(riding))
    arrived(riding, got)
    riding = ["ffn_out0", "odd_out"]
    (hg_of, hg_ob, hg_sf, hg_sb), got = _gla_fwd(pb, lbraw, lc, "hgrn_fwd", gather(riding))
    arrived(riding, got)
    x01, z0, yp0 = _post_fwd(xs, hg_of, hg_ob, pb, 4, hg_out_g, a_slab, w["even_out"], ms(0, d2, d3), HG_D, tm, "post0_fwd")
    riding = ["odd_in"]
    (x02, u0, f0), got = _ffn_fwd(x01, g01, ms(0, d3, 6 * dm), w["ffn_in0"], w["ffn_out0"], tm, "ffn0_fwd", ex=gather(riding))
    arrived(riding, got)

    riding = ["ffn_out1"]
    (rq, rk, rv, rg), got = _pre_fwd(x02, g10, ms(1, 0, d2), w["odd_in"],
                                     ((0, 1024), (1024, 2048), (2048, 4096), (4096, 6144)), tm, "pre1_fwd", gather(riding),
                                     out_dtype=ACT)
    arrived(riding, got)
    rcos, rsin = _ret_rope_tables(lc, l)
    riding = ["ffn_in1"]
    (rt_of, rt_ob, rt_sf, rt_sb), got = _ret_fwd(rq, rk, rv, rcos, rsin, lc, "ret_fwd", gather(riding))
    arrived(riding, got)
    x11, z1, yp1 = _post_fwd(x02, rt_of, rt_ob, rg, 0, None, None, w["odd_out"], ms(1, d2, d3), RET_DV, tm, "post1_fwd")
    (dx, u1, f1, loss), _ = _ffn_fwd(x11, g11, ms(1, d3, 6 * dm), w["ffn_in1"], w["ffn_out1"], tm, "ffn1_fwd", target)

    (dx, h, du, act, df, dms_f1, dg11), _ = _ffn_bwd(x11, dx, u1, f1, g11, ms(1, d3, 6 * dm), w["ffn_in1"], w["ffn_out1"], tm,
                                                     "ffn1_bwd")
    gw["ffn_in1"] = _wgrad(h, du, "wg_ffn_in1")
    gw["ffn_out1"] = _wgrad(act, df, "wg_ffn_out1")
    do1, dgr1, dy1, z1_t, dgate_p1, _ = _post_bwd(dx, z1, yp1, rt_of, rt_ob, rg, 0, None, w["odd_out"], ms(1, d2, d3), 0, RET_DV, tm,
                                                  "post1_bwd")
    gw["odd_out"] = _wgrad(z1_t, dy1, "wg_odd_out")
    riding = ["ffn_in1", "ffn_out1"]
    (dqf, dkf, dvf, dqb, dkb, dvb), got = _ret_bwd(rq, rk, rv, rcos, rsin, rt_sf, rt_sb, do1, lc, "ret_bwd", scatter(riding))
    scattered(riding, got)
    riding = ["odd_out"]
    (dx, h, dp, dms_p1, dg10), got = _pre_bwd(x02, dx, g10, ms(1, 0, d2), w["odd_in"],
                                              [(0, [dqf, dqb]), (1024, [dkf, dkb]), (2048, [dvf, dvb]), (4096, [dgr1])], tm,
                                              "pre1_bwd", ex=scatter(riding))
    scattered(riding, got)
    gw["odd_in"] = _wgrad(h, dp, "wg_odd_in")

    riding = ["odd_in"]
    (dx, h, du, act, df, dms_f0, dg01), got = _ffn_bwd(x01, dx, u0, f0, g01, ms(0, d3, 6 * dm), w["ffn_in0"], w["ffn_out0"], tm,
                                                       "ffn0_bwd", scatter(riding))
    scattered(riding, got)
    gw["ffn_in0"] = _wgrad(h, du, "wg_ffn_in0")
    gw["ffn_out0"] = _wgrad(act, df, "wg_ffn_out0")
    do0, dgr0, da0, dy0, z0_t, dgate_p0, d_hg_gain = _post_bwd(dx, z0, yp0, hg_of, hg_ob, pb, 4, hg_out_g, w["even_out"],
                                                              ms(0, d2, d3), 512, HG_D, tm, "post0_bwd")
    gw["even_out"] = _wgrad(z0_t, dy0, "wg_even_out")
    riding = ["ffn_in0", "ffn_out0"]
    (hq_f, hz_f, hv_f, hq_b, hz_b, hv_b, dlb), got = _gla_bwd(pb, lbraw, hg_sf, hg_sb, do0, lc, "hgrn_bwd", scatter(riding))
    scattered(riding, got)
    riding = ["even_out"]
    (dq_att, dk_att, dv_att, dsink), got = _attn_slab_bwd(qt, ks, vs, sinkb, a_slab, lse, da0, lc, "attn_bwd", scatter(riding))
    scattered(riding, got)
    pieces0 = [(640, [dv_att]), (768, [hq_f, hq_b]), (1280, [hz_f]), (1792, [hz_b]), (2304, [hv_f, hv_b]), (2816, [dgr0])]
    (dx, h, dp, dms_p0, dg00, dgain5), _ = _pre_bwd(xs, dx, g00, ms(0, 0, d2), w["even_in"], pieces0, tm, "pre0_bwd",
                                                    latent_dx=shards is not None,
                                                    qk=(dq_att, dk_att, pa, gains5, cosp, sinp))
    if shards is None:
        gw["even_in"] = _wgrad(h, dp, "wg_even_in")
    else:
        half = dm // 2
        gw["even_in_a"] = _wgrad(h, dp, "wg_even_in_a", rows=(0, half))
        gw["even_in_b"], got = _wgrad(h, dp, "wg_even_in_b", rows=(half, half), ex=scatter(["even_in_a"]))
        scattered(["even_in_a"], got)

    dmv = jnp.stack([jnp.concatenate([dms_p0, dgate_p0, dms_f0], axis=2), jnp.concatenate([dms_p1, dgate_p1, dms_f1], axis=2)])
    small = {
        "dmv": dmv,
        "norm_g": jnp.stack([jnp.stack([dg00[0], dg01[0]]), jnp.stack([dg10[0], dg11[0]])]),
        "qk_g": jnp.stack([jnp.sum(dgain5[:N_PAIRS - 1, 0].reshape(-1, HEAD_DIM), axis=0),
                           jnp.sum(dgain5[N_PAIRS - 1, 0].reshape(-1, HEAD_DIM), axis=0)]),
        "sink": dsink.reshape(ATTN_HEADS),
        "hg_out_g": d_hg_gain[0],
        "lb": dlb[0],
        "loss": loss[0, 0],
    }
    if shards is not None:
        gw = {n: recv.get(n, g) for n, g in gw.items()}
    return loss, dx, gw, small


HBM_SPEC = pl.BlockSpec(memory_space=pltpu.HBM)


def _my_index():
    return 4 * lax.axis_index("x") + 2 * lax.axis_index("y") + lax.axis_index("c")


def _peer(k):
    pos = []
    for axis, bit in (("x", 4), ("y", 2), ("c", 1)):
        a = lax.axis_index(axis)
        pos.append(1 - a if k & bit else a)
    return tuple(pos)


def _peer_index(k):
    px, py, pc = _peer(k)
    return 4 * px + 2 * py + pc


GATHER, SCATTER = "gather", "scatter"
GATHER2 = "gather over ICI once per chip"
SIBLING = 1
OTHER_CHIPS = (2, 4, 6)


class _Exchange:
    def __init__(self, mode, arrays):
        self.mode, self.arrays, self.n = mode, list(arrays), len(arrays)

    def out_shape(self):
        if self.mode in (GATHER, GATHER2):
            return [jax.ShapeDtypeStruct((N_DEV,) + a.shape, a.dtype) for a in self.arrays]
        return [jax.ShapeDtypeStruct(a.shape, a.dtype) for a in self.arrays]

    def specs(self):
        return [HBM_SPEC] * self.n

    def scratch(self):
        return [pltpu.SemaphoreType.DMA((self.n, N_DEV - 1)), pltpu.SemaphoreType.DMA((self.n, N_DEV - 1)),
                pltpu.SemaphoreType.DMA((self.n,))]

    def _copies(self, in_refs, out_refs, send_sems, recv_sems, local_sems, landing):
        me = _my_index()
        local, remote = [], []
        for a, (src, dst) in enumerate(zip(in_refs, out_refs)):
            part = (lambda j, s=src: s) if self.mode == GATHER else (lambda j, s=src: s.at[j])
            local.append(pltpu.make_async_copy(part(me), dst.at[me], local_sems.at[a]))
            for k in range(1, N_DEV):
                pj = _peer_index(k)
                remote.append(pltpu.make_async_remote_copy(
                    src_ref=part(pj), dst_ref=dst.at[pj if landing else me], send_sem=send_sems.at[a, k - 1],
                    recv_sem=recv_sems.at[a, k - 1], device_id=_peer(k), device_id_type=MESH))
        return local, remote

    def _copy2(self, a, src, dst, sems, slot, relation, to):
        send_sems, recv_sems, _ = sems
        return pltpu.make_async_remote_copy(src_ref=src, dst_ref=dst.at[slot], send_sem=send_sems.at[a, relation - 1],
                                            recv_sem=recv_sems.at[a, relation - 1], device_id=_peer(to), device_id_type=MESH)

    def start(self, in_refs, out_refs, sems):
        if self.mode == GATHER2:
            me = _my_index()
            for a, (src, dst) in enumerate(zip(in_refs, out_refs)):
                pltpu.make_async_copy(src, dst.at[me], sems[2].at[a]).start()
                for k in (SIBLING,) + OTHER_CHIPS:
                    self._copy2(a, src, dst, sems, me, k, k).start()
            return
        local, remote = self._copies(in_refs, out_refs, *sems, landing=False)
        for cp in local + remote:
            cp.start()

    def forward(self, in_refs, out_refs, sems):
        for a, (src, dst) in enumerate(zip(in_refs, out_refs)):
            for r in OTHER_CHIPS:
                pj = _peer_index(r)
                self._copy2(a, src, dst, sems, pj, r, r).wait_recv()
                self._copy2(a, dst.at[pj], dst, sems, pj, r ^ SIBLING, SIBLING).start()

    def wait(self, in_refs, out_refs, sems):
        if self.mode == GATHER2:
            me = _my_index()
            for a, (src, dst) in enumerate(zip(in_refs, out_refs)):
                for k in (SIBLING,) + OTHER_CHIPS:
                    self._copy2(a, src, dst, sems, me, k, k).wait_send()
                self._copy2(a, src, dst, sems, _peer_index(SIBLING), SIBLING, SIBLING).wait_recv()
                for r in OTHER_CHIPS:
                    passed = self._copy2(a, src, dst, sems, _peer_index(r ^ SIBLING), r ^ SIBLING, SIBLING)
                    passed.wait_send()
                    passed.wait_recv()
                pltpu.make_async_copy(src, dst.at[me], sems[2].at[a]).wait()
            return
        local, remote = self._copies(in_refs, out_refs, *sems, landing=True)
        for cp in remote:
            cp.wait_send()
            cp.wait_recv()
        for cp in local:
            cp.wait()

    def ride(self, refs, n_in, n_out, first, mid, last):
        refs = list(refs)
        n = self.n
        x_in = refs[n_in:n_in + n]
        x_out = refs[n_in + n + n_out:n_in + 2 * n + n_out]
        sems = refs[n_in + 2 * n + n_out:n_in + 2 * n + n_out + 3]

        @pl.when(first)
        def _():
            self.start(x_in, x_out, sems)

        if self.mode == GATHER2:
            @pl.when(mid)
            def _():
                self.forward(x_in, x_out, sems)

        @pl.when(last)
        def _():
            self.wait(x_in, x_out, sems)

        return refs[:n_in] + refs[n_in + n:n_in + n + n_out] + refs[n_in + 2 * n + n_out + 3:]

    def call(self, name):
        n = self.n

        def body(*refs):
            ins, outs, sems = refs[:n], refs[n:2 * n], refs[2 * n:]
            self.start(ins, outs, sems)
            if self.mode == GATHER2:
                self.forward(ins, outs, sems)
            self.wait(ins, outs, sems)

        return pl.pallas_call(body, name=name, in_specs=self.specs(), out_specs=self.specs(), out_shape=self.out_shape(),
                              scratch_shapes=self.scratch())(*self.arrays)


def _all_gather(v, name):
    return _Exchange(GATHER, [v]).call(name)[0]


def _hosted(kernel_body, ex, n_in, n_out, first, last, grid):
    if ex is None:
        return kernel_body

    def body(*refs):
        mid = pl.program_id(0) == (2 * grid[0]) // 3 if len(grid) == 1 else None
        kernel_body(*ex.ride(refs, n_in, n_out, first(), mid, last()))

    return body


def _host_call(kernel_body, ex, first, last, name, grid, in_specs, out_specs, out_shape, scratch_shapes, sem, args):
    n_in, n_out = len(in_specs), len(out_specs)
    if ex is None:
        outs = pl.pallas_call(kernel_body, name=name, grid=grid, in_specs=in_specs, out_specs=out_specs, out_shape=out_shape,
                              scratch_shapes=scratch_shapes, compiler_params=_cp(*sem))(*args)
        return list(outs), []
    outs = pl.pallas_call(
        _hosted(kernel_body, ex, n_in, n_out, first, last, grid), name=name, grid=grid,
        in_specs=list(in_specs) + ex.specs(), out_specs=list(out_specs) + ex.specs(),
        out_shape=list(out_shape) + ex.out_shape(), scratch_shapes=ex.scratch() + list(scratch_shapes),
        compiler_params=_cp(*sem))(*args, *ex.arrays)
    return list(outs[:n_out]), list(outs[n_out:])


def _mod_fwd(call, mod_w, bias, name):
    nl, dm, n = mod_w.shape

    def body(c_ref, w_ref, b_ref, o_ref):
        cv = c_ref[...]
        cond = _bf(cv * _sig(cv))
        for layer in range(nl):
            o_ref[layer] = _nn(cond, _bf(w_ref[layer])) + b_ref[layer]

    return pl.pallas_call(
        body, name=name, out_shape=jax.ShapeDtypeStruct((nl, call.shape[0], n), F32),
        compiler_params=pltpu.CompilerParams(vmem_limit_bytes=VMEM_LIMIT),
    )(call, mod_w, bias)


def _mod_bwd(call, dm_all, mod_w, name):
    nl, dm, n = mod_w.shape

    def body(c_ref, d_ref, w_ref, gw_ref, dc_ref):
        cv = c_ref[...]
        cond = _bf(cv * _sig(cv))
        dc = jnp.zeros(cv.shape, F32)
        for layer in range(nl):
            db = _bf(d_ref[layer])
            gw_ref[layer] = _tn(cond, db)
            dc = dc + _nt(db, _bf(w_ref[layer]))
        dc_ref[...] = dc

    return pl.pallas_call(
        body, name=name,
        out_shape=[jax.ShapeDtypeStruct(mod_w.shape, F32), jax.ShapeDtypeStruct(call.shape, F32)],
        compiler_params=pltpu.CompilerParams(vmem_limit_bytes=VMEM_LIMIT),
    )(call, dm_all, mod_w)


def _sum_parts(g, name):
    def body(g_ref, o_ref):
        acc = g_ref[0]
        for j in range(1, g.shape[0]):
            acc = acc + g_ref[j]
        o_ref[...] = acc

    return pl.pallas_call(body, name=name, out_shape=jax.ShapeDtypeStruct(g.shape[1:], g.dtype))(g)


def _small_finish(dcond_g, c_ctx, dlb, lbraw, dm_ctx, dm_lat, name):
    def body(dc_ref, c_ref, dlb_ref, lb_ref, mc_ref, ml_ref, gc_ref, glb_ref, gb_ref):
        acc = dc_ref[0, 0:1, :]
        for j in range(1, N_DEV):
            acc = acc + dc_ref[j, 0:1, :]
        cv = c_ref[...]
        s = _sig(cv)
        gc_ref[...] = acc * (s * (1.0 + cv * (1.0 - s)))
        lb = _lower_bound(lb_ref)
        d0 = dlb_ref[...] * lb * (1.0 - lb)
        glb_ref[0:1, :] = d0
        glb_ref[1:2, :] = -d0
        gb_ref[...] = mc_ref[...] + ml_ref[...]

    return pl.pallas_call(
        body, name=name,
        out_shape=[jax.ShapeDtypeStruct(c_ctx.shape, F32), jax.ShapeDtypeStruct(lbraw.shape, F32),
                   jax.ShapeDtypeStruct(dm_ctx.shape, F32)],
    )(dcond_g, c_ctx, dlb, lbraw, dm_ctx, dm_lat)


def _row_tile(r, cap, mult):
    best = r
    for t in range(mult, min(r, cap) + 1, mult):
        if r % t == 0:
            best = t
    return best


def _adam(g_list, w, m, v, name, ex=None):
    nl, r, cdim = w.shape
    p = g_list[0].shape[0]
    tr = _row_tile(r, 128, 16)
    ni = r // tr

    def body(*refs):
        g_refs = refs[:nl]
        w_ref, m_ref, v_ref, go_ref, d_ref, mo_ref, vo_ref = refs[nl:]
        layer = pl.program_id(0)

        def total(g_ref):
            acc = g_ref[0].astype(F32)
            for j in range(1, p):
                acc = acc + g_ref[j].astype(F32)
            return acc

        g = total(g_refs[0])
        for k in range(1, nl):
            g = jnp.where(layer == k, total(g_refs[k]), g)
        m2 = ADAM_B1 * m_ref[0] + (1.0 - ADAM_B1) * g
        v2 = ADAM_B2 * v_ref[0] + (1.0 - ADAM_B2) * (g * g)
        m_hat = m2 / (1.0 - ADAM_B1 ** ADAM_STEP)
        v_hat = v2 / (1.0 - ADAM_B2 ** ADAM_STEP)
        go_ref[0] = g
        d_ref[0] = -ADAM_LR * (m_hat / (jnp.sqrt(v_hat) + ADAM_EPS) + ADAM_WD * w_ref[0])
        mo_ref[0] = m2
        vo_ref[0] = v2

    def g_spec(k):
        return pl.BlockSpec((p, tr, cdim), lambda la, i: (0, jnp.where(la == k, i, jnp.where(la < k, 0, ni - 1)), 0))

    spec = pl.BlockSpec((1, tr, cdim), lambda la, i: (la, i, 0))
    return _host_call(
        body, ex, lambda: (pl.program_id(0) == 0) & (pl.program_id(1) == 0),
        lambda: (pl.program_id(0) == nl - 1) & (pl.program_id(1) == ni - 1),
        name=name, grid=(nl, ni),
        in_specs=[g_spec(k) for k in range(nl)] + [spec, spec, spec],
        out_specs=[spec] * 4, out_shape=[jax.ShapeDtypeStruct((nl, r, cdim), F32)] * 4,
        scratch_shapes=[], sem=("arbitrary", "arbitrary"), args=(*g_list, w, m, v))


def _f32_as_rows(a, width):
    return lax.bitcast_convert_type(a.reshape(-1), BF16).reshape(-1, width)


def _rows_as_f32(rows):
    return lax.bitcast_convert_type(rows.reshape(rows.shape[:-2] + (-1, 2)), F32)


def _pad_rows(a, mult):
    r = (-a.shape[-2]) % mult
    if r == 0:
        return a
    widths = [(0, 0)] * (a.ndim - 2) + [(0, r), (0, 0)]
    return jnp.pad(a, widths)


def _pack_flat(parts, lane):
    flat = jnp.concatenate([p.reshape(-1).astype(F32) for p in parts])
    n = flat.shape[0]
    rows = -(-n // lane)
    rows += (-rows) % 8
    return jnp.pad(flat, (0, rows * lane - n)).reshape(rows, lane)


def _unpack_flat(packed, shapes):
    flat = packed.reshape(-1)
    out, off = [], 0
    for s in shapes:
        n = math.prod(s)
        out.append(flat[off:off + n].reshape(s))
        off += n
    return out


def kernel(x, c, ctx, c_ctx, mod_w, mod_b, norm_g, ffn_w_in, ffn_w_out, even_w_in, even_w_out, attn_qk_norm_g, attn_sink, hgrn_out_norm_g, hgrn_lb, odd_w_in, odd_w_out, loss_target, m_c_ctx, m_mod_w, m_mod_b, m_norm_g, m_ffn_w_in, m_ffn_w_out, m_even_w_in, m_even_w_out, m_attn_qk_norm_g, m_attn_sink, m_hgrn_out_norm_g, m_hgrn_lb, m_odd_w_in, m_odd_w_out, v_c_ctx, v_mod_w, v_mod_b, v_norm_g, v_ffn_w_in, v_ffn_w_out, v_even_w_in, v_even_w_out, v_attn_qk_norm_g, v_attn_sink, v_hgrn_out_norm_g, v_hgrn_lb, v_odd_w_in, v_odd_w_out):
    me = _my_index()
    lc, dm = ctx.shape[1], x.shape[2]
    nmod = mod_w.shape[2]
    big = (ffn_w_in, ffn_w_out, even_w_in, even_w_out, odd_w_in, odd_w_out)

    extra = _pad_rows(jnp.concatenate([_f32_as_rows(c, dm), _f32_as_rows(norm_g, dm)], axis=0), 16)
    shards = {"ffn_in0": ffn_w_in[0], "ffn_in1": ffn_w_in[1], "ffn_out0": ffn_w_out[0], "ffn_out1": ffn_w_out[1],
              "even_in": even_w_in[0], "even_out": even_w_out[0], "odd_in": odd_w_in[0], "odd_out": odd_w_out[0]}
    shards = {n: a.astype(BF16) for n, a in shards.items()}
    first = _Exchange(GATHER2, [shards["even_in"], extra]).call("gather_first")
    w = {"even_in": _full_weight("even_in", first[0])}
    c_all = _rows_as_f32(first[1][:, 0:2])
    norm_g_all = _rows_as_f32(first[1][:, 2:3]).reshape(N_DEV, 2, 2, -1)
    norm_g_full = norm_g_all.transpose(1, 2, 0, 3).reshape(2, 2, dm)

    call = jnp.concatenate([c_all, c_ctx[None, :], jnp.zeros((16 - N_DEV - 1, dm), F32)], axis=0)
    bias = lax.dynamic_slice_in_dim(mod_b, me * nmod, nmod, axis=1)[:, None, :]
    m_sh = _mod_fwd(call, mod_w, bias, "mod_fwd")
    m_g = _all_gather(m_sh.reshape(-1, nmod), "gather_mod").reshape(N_DEV, 2, 16, nmod)
    m_all = m_g.transpose(1, 2, 0, 3).reshape(2, 16, -1)
    m_lat = lax.dynamic_index_in_dim(m_all, me, axis=1, keepdims=False)
    mv = jnp.stack([m_all[:, N_DEV], m_lat], axis=1)[:, :, None, :]

    _, dxs, gw, small = _local_step((ctx[0], x[0]), loss_target[0], mv, norm_g_full, w, attn_qk_norm_g[0], attn_sink[0],
                                    hgrn_out_norm_g, hgrn_lb, lc, shards)
    grad_x = dxs[None]

    last = _Exchange(SCATTER, [_shard_slots("even_in_b", gw["even_in_b"])])
    big_g = [[gw["ffn_in0"], gw["ffn_in1"]], [gw["ffn_out0"], gw["ffn_out1"]], None, [gw["even_out"]],
             [gw["odd_in"]], [gw["odd_out"]]]
    halves = (2, even_w_in.shape[1] // 2, even_w_in.shape[2])
    big_w = (ffn_w_in, ffn_w_out, even_w_in.reshape(halves), even_w_out, odd_w_in, odd_w_out)
    big_m = (m_ffn_w_in, m_ffn_w_out, m_even_w_in.reshape(halves), m_even_w_out, m_odd_w_in, m_odd_w_out)
    big_v = (v_ffn_w_in, v_ffn_w_out, v_even_w_in.reshape(halves), v_even_w_out, v_odd_w_in, v_odd_w_out)
    big_names = ("ffn_w_in", "ffn_w_out", "even_w_in", "even_w_out", "odd_w_in", "odd_w_out")
    big_out = [None] * 6

    def adam_big(i, ex=None):
        big_out[i], got = _adam(big_g[i], big_w[i], big_m[i], big_v[i], "adam_" + big_names[i], ex)
        return got

    dmv = small["dmv"]
    small_shapes = [(2, 6 * dm), (2, 6 * dm), (2, 2, dm), (2, HEAD_DIM), (ATTN_HEADS,), (HG_D,), (HG_HEADS * HG_D,), (1,)]
    vec = _pack_flat([dmv[:, 0, 0], dmv[:, 1, 0], small["norm_g"], small["qk_g"], small["sink"], small["hg_out_g"],
                      small["lb"], small["loss"]], 128)
    big_g[2] = [gw["even_in_a"], adam_big(0, last)[0]]
    vec_g = adam_big(1, _Exchange(GATHER, [vec]))[0]
    tot = _unpack_flat(_sum_parts(vec_g, "sum_small"), small_shapes)
    dm_ctx_tot, dm_lat_tot, g_norm_full, g_qk, g_sink, g_hg, dlb_tot, loss_tot = tot
    dm_lat_each = vec_g.reshape(N_DEV, -1)[:, 12 * dm:24 * dm].reshape(N_DEV, 2, 6 * dm)
    dm_lat_mine = lax.dynamic_slice_in_dim(dm_lat_each, me * nmod, nmod, axis=2).transpose(1, 0, 2)
    dm_ctx_mine = lax.dynamic_slice_in_dim(dm_ctx_tot, me * nmod, nmod, axis=1)[:, None, :]
    dm_all = jnp.concatenate([dm_lat_mine, dm_ctx_mine, jnp.zeros((2, 16 - N_DEV - 1, nmod), F32)], axis=1)
    g_mod_w, dcond = _mod_bwd(call, dm_all, mod_w, "mod_bwd")
    dcond_g = adam_big(4, _Exchange(GATHER, [dcond[N_DEV:]]))[0]
    g_c_ctx, g_lb, g_mod_b = _small_finish(dcond_g, c_ctx[None, :], dlb_tot[None, :], hgrn_lb, dm_ctx_tot, dm_lat_tot,
                                           "small_finish")
    g_norm = lax.dynamic_slice_in_dim(g_norm_full, me * norm_g.shape[2], norm_g.shape[2], axis=2)
    for i in (3, 5, 2):
        adam_big(i)
    big_out[2] = [o.reshape(even_w_in.shape) for o in big_out[2]]
    big_res = [[big_out[i][k] for i in range(6)] for k in range(4)]

    mod_res, _ = _adam([g_mod_w[0][None], g_mod_w[1][None]], mod_w, m_mod_w, v_mod_w, "adam_mod_w")

    sm_w = (c_ctx, mod_b, norm_g, attn_qk_norm_g, attn_sink, hgrn_out_norm_g, hgrn_lb)
    sm_m = (m_c_ctx, m_mod_b, m_norm_g, m_attn_qk_norm_g, m_attn_sink, m_hgrn_out_norm_g, m_hgrn_lb)
    sm_v = (v_c_ctx, v_mod_b, v_norm_g, v_attn_qk_norm_g, v_attn_sink, v_hgrn_out_norm_g, v_hgrn_lb)
    sm_g = (g_c_ctx, g_mod_b, g_norm, g_qk, g_sink, g_hg, g_lb)
    sm_shapes = [a.shape for a in sm_w]
    sm_out, _ = _adam([_pack_flat(sm_g, 128)[None]], _pack_flat(sm_w, 128)[None], _pack_flat(sm_m, 128)[None],
                      _pack_flat(sm_v, 128)[None], "adam_small")
    sm_res = [_unpack_flat(o, sm_shapes) for o in sm_out]

    def ordered(k):
        s, b = sm_res[k], big_res[k]
        return [s[0], mod_res[k], s[1], s[2], b[0], b[1], b[2], b[3], s[3], s[4], s[5], s[6], b[4], b[5]]

    return (loss_tot[0], grad_x, *ordered(0), *ordered(1), *ordered(2), *ordered(3))
```

```python
import functools
import math

import jax
import jax.numpy as jnp
import numpy as np
from jax import lax
from jax.experimental import pallas as pl
from jax.experimental.pallas import tpu as pltpu

F32 = jnp.float32
BF16 = jnp.bfloat16
EPS = 1e-6
N_DEV = 8
MESH = pl.DeviceIdType.MESH

HEAD_DIM = 64
ATTN_HEADS = 8
ATTN_KV = 2
ATTN_BLOCK = 128
WINDOW = 128
GRID_W = 64
HG_HEADS = 4
HG_D = 128
HG_CHUNK = 64
RET_HEADS = 4
RET_DK = 256
RET_DV = 512
RET_CHUNK = 256
NEG = -1e30

ADAM_LR = 0.001
ADAM_B1 = 0.9
ADAM_B2 = 0.999
ADAM_EPS = 1e-08
ADAM_WD = 0.01
ADAM_STEP = 10

VMEM_LIMIT = 60 * 1024 * 1024


def _cp(*sem):
    return pltpu.CompilerParams(dimension_semantics=sem, vmem_limit_bytes=VMEM_LIMIT)


def _nn(a, b):
    return jnp.dot(a, b, preferred_element_type=F32)


def _nt(a, b):
    return lax.dot_general(a, b, (((1,), (1,)), ((), ())), preferred_element_type=F32)


def _tn(a, b):
    return lax.dot_general(a, b, (((0,), (0,)), ((), ())), preferred_element_type=F32)


ACT = BF16


def _bf(a):
    return a.astype(ACT)


def _sig(x):
    return jax.nn.sigmoid(x)


def _split3(x):
    h = x.astype(BF16)
    r = x - h.astype(F32)
    m = r.astype(BF16)
    lo = (r - m.astype(F32)).astype(BF16)
    return h, m, lo


def _nn3(m01, x):
    h, m, lo = _split3(x)
    return _nn(m01, h) + _nn(m01, m) + _nn(m01, lo)


def _nn3r(x, m01):
    h, m, lo = _split3(x)
    return _nn(h, m01) + _nn(m, m01) + _nn(lo, m01)


def _full(shape):
    nd = len(shape)
    return pl.BlockSpec(shape, lambda *a: (0,) * nd, pipeline_mode=pl.Buffered(1))


def _whole(shape):
    nd = len(shape)
    return pl.BlockSpec(shape, lambda *a: (0,) * nd)


def _rows(tm, width):
    return pl.BlockSpec((tm, width), lambda i: (i, 0))


def _cols(height, tm):
    return pl.BlockSpec((height, tm), lambda i: (0, i))


def _ctx_lat(width):
    return pl.BlockSpec((1, 1, width), lambda i: (jnp.minimum(i, 1), 0, 0))


def _acc_ctx_lat(ref, i, val):
    @pl.when(i <= 1)
    def _():
        ref[...] = val.reshape(ref.shape)

    @pl.when(i > 1)
    def _():
        ref[...] += val.reshape(ref.shape)


def _acc_all(ref, i, val):
    @pl.when(i == 0)
    def _():
        ref[...] = val.reshape(ref.shape)

    @pl.when(i > 0)
    def _():
        ref[...] += val.reshape(ref.shape)


def _tile(n, cap):
    best = None
    for t in range(128, min(n, cap) + 1, 128):
        if n % t == 0:
            best = t
    return n if best is None else best


def _norm_mod(xv, g, shift, scale):
    r = lax.rsqrt(jnp.mean(xv * xv, axis=-1, keepdims=True) + EPS)
    xhat = xv * r
    n = xhat * g
    return r, xhat, n, n * (1.0 + scale) + shift


def _norm_mod_bwd(dh, r, xhat, n, g, scale):
    dshift = jnp.sum(dh, axis=0, keepdims=True)
    dscale = jnp.sum(dh * n, axis=0, keepdims=True)
    dn = dh * (1.0 + scale)
    dg = jnp.sum(dn * xhat, axis=0, keepdims=True)
    dxh = dn * g
    dx = r * (dxh - xhat * jnp.mean(dxh * xhat, axis=-1, keepdims=True))
    return dx, dshift, dscale, dg


def _stream(x):
    if isinstance(x, tuple):
        return list(x), x[0].shape[0] + x[1].shape[0], x[0].shape[1]
    return [x], x.shape[0], x.shape[1]


def _stream_specs(x, tm, dm):
    if isinstance(x, tuple):
        return [pl.BlockSpec((tm, dm), lambda i: (0, 0)), pl.BlockSpec((tm, dm), lambda i: (jnp.maximum(i - 1, 0), 0))]
    return [_rows(tm, dm)]


def _stream_tile(refs):
    if len(refs) == 2:
        return jnp.where(pl.program_id(0) == 0, refs[0][...], refs[1][...])
    return refs[0][...]


def _pre_fwd(x, gain, ms, w, splits, tm, name, ex=None, out_dtype=F32, qk=None):
    xs, T, dm = _stream(x)
    nx = len(xs)
    nt = T // tm
    nq = 0 if qk is None else 3
    ns = len(splits)

    def body(*refs):
        g_ref, ms_ref, w_ref = refs[nx:nx + 3]
        outs = refs[nx + 3 + nq:]
        ms_v = ms_ref[0]
        h = _norm_mod(_stream_tile(refs[:nx]), g_ref[...], ms_v[:, :dm], ms_v[:, dm:])[3]
        hb = _bf(h)
        for k, ((s, e), o_ref) in enumerate(zip(splits, outs[:ns])):
            part = _nn(hb, w_ref[:, s:e])
            o_ref[...] = part.astype(o_ref.dtype)
            if k == 0 and qk is not None:
                gq_ref, c_ref, s_ref = refs[nx + 3:nx + 6]
                _qk_tile_fwd(part, gq_ref, c_ref[...], s_ref[...], *outs[ns:])

    in_specs = _stream_specs(x, tm, dm) + [_full((1, dm)), _ctx_lat(2 * dm), _full(w.shape)]
    out_specs = [_rows(tm, e - s) for s, e in splits]
    out_shape = [jax.ShapeDtypeStruct((T, e - s), out_dtype) for s, e in splits]
    args = [*xs, gain, ms, w]
    if qk is not None:
        qw = ATTN_HEADS * HEAD_DIM
        in_specs += [_full(qk[0].shape), _rows(tm, PAIR), _rows(tm, PAIR)]
        args += list(qk)
        out_specs += [_rows(tm, qw), _rows(tm, PAIR), _rows(tm, PAIR)]
        out_shape += [jax.ShapeDtypeStruct((T, qw), ACT), jax.ShapeDtypeStruct((T, PAIR), ACT), jax.ShapeDtypeStruct((T, PAIR), ACT)]
    return _host_call(
        body, ex, lambda: pl.program_id(0) == 0, lambda: pl.program_id(0) == nt - 1,
        name=name, grid=(nt,), in_specs=in_specs, out_specs=out_specs, out_shape=out_shape,
        scratch_shapes=[], sem=("arbitrary",), args=tuple(args))


def _pre_bwd(x, dx_in, gain, ms, w, pieces, tm, name, latent_dx=False, ex=None, qk=None):
    xs, T, dm = _stream(x)
    nx = len(xs)
    dx_spec = pl.BlockSpec((tm, dm), lambda i: (jnp.maximum(i - 1, 0), 0)) if latent_dx else _rows(tm, dm)
    dx_rows = T - tm if latent_dx else T
    n_out = w.shape[1]
    flat = [a for _, arrs in pieces for a in arrs]
    nq = 0 if qk is None else 6
    qkw = (ATTN_HEADS + ATTN_KV) * HEAD_DIM

    def body(*refs):
        dxin_ref, g_ref, ms_ref, w_ref = refs[nx:nx + 4]
        rest = refs[nx + 4:]
        p_refs = rest[:len(flat)]
        qk_refs = rest[len(flat):len(flat) + nq]
        dx_ref, h_ref, dp_ref, dms_ref, dg_ref = rest[len(flat) + nq:len(flat) + nq + 5]
        i = pl.program_id(0)
        ms_v = ms_ref[0]
        g = g_ref[...]
        scale = ms_v[:, dm:]
        r, xhat, n, h = _norm_mod(_stream_tile(refs[:nx]), g, ms_v[:, :dm], scale)
        h_ref[...] = _bf(h).T
        dh = jnp.zeros((tm, dm), F32)
        if qk is not None:
            dq_ref, dk_ref, pa_ref, gq_ref, c_ref, s_ref = qk_refs
            dqk, dgs = _qk_tile_bwd(dq_ref, dk_ref, pa_ref, gq_ref, c_ref[...], s_ref[...])
            dgq_ref = rest[len(flat) + nq + 5]
            for p, dgp in enumerate(dgs):
                _acc_all(dgq_ref.at[p], i, dgp)
            vb = _bf(dqk)
            dp_ref[:, :qkw] = vb
            dh = dh + _nt(vb, w_ref[:, :qkw])
        k = 0
        for s, arrs in pieces:
            v = p_refs[k][...].astype(F32)
            for j in range(1, len(arrs)):
                v = v + p_refs[k + j][...].astype(F32)
            k += len(arrs)
            vb = _bf(v)
            wd = vb.shape[1]
            dp_ref[:, s:s + wd] = vb
            dh = dh + _nt(vb, w_ref[:, s:s + wd])
        dx, dshift, dscale, dg = _norm_mod_bwd(dh, r, xhat, n, g, scale)
        dx_ref[...] = dxin_ref[...] + dx
        _acc_ctx_lat(dms_ref, i, jnp.concatenate([dshift, dscale], axis=1))
        _acc_all(dg_ref, i, dg)

    nt = T // tm
    in_specs = (_stream_specs(x, tm, dm) + [_rows(tm, dm), _full((1, dm)), _ctx_lat(2 * dm), _full(w.shape)]
                + [_rows(tm, a.shape[1]) for a in flat])
    out_specs = [dx_spec, _cols(dm, tm), _rows(tm, n_out), _ctx_lat(2 * dm), _whole((1, dm))]
    out_shape = [jax.ShapeDtypeStruct((dx_rows, dm), F32), jax.ShapeDtypeStruct((dm, T), ACT),
                 jax.ShapeDtypeStruct((T, n_out), ACT), jax.ShapeDtypeStruct((2, 1, 2 * dm), F32),
                 jax.ShapeDtypeStruct((1, dm), F32)]
    args = [*xs, dx_in, gain, ms, w, *flat]
    if qk is not None:
        dq, dk, pa, gains, cosp, sinp = qk
        in_specs += [_rows(tm, dq.shape[1]), _rows(tm, PAIR), _rows(tm, qkw), _full(gains.shape), _rows(tm, PAIR), _rows(tm, PAIR)]
        args += [dq, dk, pa, gains, cosp, sinp]
        out_specs.append(_whole(gains.shape))
        out_shape.append(jax.ShapeDtypeStruct(gains.shape, F32))
    return _host_call(
        body, ex, lambda: pl.program_id(0) == 0, lambda: pl.program_id(0) == nt - 1,
        name=name, grid=(nt,), in_specs=in_specs, out_specs=out_specs, out_shape=out_shape,
        scratch_shapes=[], sem=("arbitrary",), args=tuple(args))


def _ffn_fwd(x1, gain, ms, w_in, w_out, tm, name, target=None, ex=None):
    T, dm = x1.shape
    fh = w_out.shape[0]
    head = target is not None

    def body(*refs):
        if head:
            x_ref, g_ref, ms_ref, wi_ref, wo_ref, t_ref, x2_ref, u_ref, f_ref, loss_ref = refs
        else:
            x_ref, g_ref, ms_ref, wi_ref, wo_ref, x2_ref, u_ref, f_ref = refs
        ms_v = ms_ref[0]
        xv = x_ref[...]
        h = _norm_mod(xv, g_ref[...], ms_v[:, :dm], ms_v[:, dm:2 * dm])[3]
        u = _nn(_bf(h), wi_ref[...])
        u_ref[...] = _bf(u)
        gt = u[:, :fh]
        act = gt * _sig(gt) * u[:, fh:]
        f = _nn(_bf(act), wo_ref[...])
        f_ref[...] = _bf(f)
        x2 = xv + ms_v[:, 2 * dm:] * f
        if head:
            i = pl.program_id(0)
            e = x2 - t_ref[...]
            x2_ref[...] = jnp.where(i > 0, e * (1.0 / dm), 0.0)
            _acc_all(loss_ref, i, jnp.where(i > 0, jnp.sum(e * e) * (0.5 / dm), 0.0))
        else:
            x2_ref[...] = x2

    ins = [x1, gain, ms, w_in, w_out]
    in_specs = [_rows(tm, dm), _full((1, dm)), _ctx_lat(3 * dm), _full(w_in.shape), _full(w_out.shape)]
    out_specs = [_rows(tm, dm), _rows(tm, 2 * fh), _rows(tm, dm)]
    out_shape = [jax.ShapeDtypeStruct((T, dm), F32), jax.ShapeDtypeStruct((T, 2 * fh), ACT), jax.ShapeDtypeStruct((T, dm), ACT)]
    if head:
        ins.append(target)
        in_specs.append(pl.BlockSpec((tm, dm), lambda i: (jnp.maximum(i - 1, 0), 0)))
        out_specs.append(_whole((1, 1)))
        out_shape.append(jax.ShapeDtypeStruct((1, 1), F32))
    nt = T // tm
    return _host_call(
        body, ex, lambda: pl.program_id(0) == 0, lambda: pl.program_id(0) == nt - 1,
        name=name, grid=(nt,), in_specs=in_specs, out_specs=out_specs, out_shape=out_shape,
        scratch_shapes=[], sem=("arbitrary",), args=tuple(ins))


def _ffn_bwd(x1, dx2, u, f, gain, ms, w_in, w_out, tm, name, ex=None):
    T, dm = x1.shape
    fh = w_out.shape[0]

    def body(x_ref, dx2_ref, u_ref, f_ref, g_ref, ms_ref, wi_ref, wo_ref,
             dx1_ref, h_ref, du_ref, act_ref, df_ref, dms_ref, dg_ref):
        i = pl.program_id(0)
        ms_v = ms_ref[0]
        g = g_ref[...]
        scale = ms_v[:, dm:2 * dm]
        gate = ms_v[:, 2 * dm:]
        r, xhat, n, h = _norm_mod(x_ref[...], g, ms_v[:, :dm], scale)
        h_ref[...] = _bf(h).T
        dx2 = dx2_ref[...]
        dgate = jnp.sum(dx2 * f_ref[...].astype(F32), axis=0, keepdims=True)
        dfb = _bf(dx2 * gate)
        df_ref[...] = dfb
        da = _nt(dfb, wo_ref[...])
        uv = u_ref[...].astype(F32)
        gt = uv[:, :fh]
        up = uv[:, fh:]
        s = _sig(gt)
        sg = gt * s
        act_ref[...] = _bf(sg * up).T
        dgt = _bf(da * up * (s * (1.0 + gt * (1.0 - s))))
        dup = _bf(da * sg)
        du_ref[:, :fh] = dgt
        du_ref[:, fh:] = dup
        dh = _nt(dgt, wi_ref[:, :fh]) + _nt(dup, wi_ref[:, fh:])
        dx, dshift, dscale, dg = _norm_mod_bwd(dh, r, xhat, n, g, scale)
        dx1_ref[...] = dx2 + dx
        _acc_ctx_lat(dms_ref, i, jnp.concatenate([dshift, dscale, dgate], axis=1))
        _acc_all(dg_ref, i, dg)

    nt = T // tm
    return _host_call(
        body, ex, lambda: pl.program_id(0) == 0, lambda: pl.program_id(0) == nt - 1,
        name=name, grid=(nt,),
        in_specs=[_rows(tm, dm), _rows(tm, dm), _rows(tm, 2 * fh), _rows(tm, dm), _full((1, dm)), _ctx_lat(3 * dm),
                  _full(w_in.shape), _full(w_out.shape)],
        out_specs=[_rows(tm, dm), _cols(dm, tm), _rows(tm, 2 * fh), _cols(fh, tm), _rows(tm, dm),
                   _ctx_lat(3 * dm), _whole((1, dm))],
        out_shape=[jax.ShapeDtypeStruct((T, dm), F32), jax.ShapeDtypeStruct((dm, T), ACT),
                   jax.ShapeDtypeStruct((T, 2 * fh), ACT), jax.ShapeDtypeStruct((fh, T), ACT),
                   jax.ShapeDtypeStruct((T, dm), ACT), jax.ShapeDtypeStruct((2, 1, 3 * dm), F32),
                   jax.ShapeDtypeStruct((1, dm), F32)],
        scratch_shapes=[], sem=("arbitrary",), args=(x1, dx2, u, f, gain, ms, w_in, w_out))


def _wgrad(a_t, b, name, rows=None, ex=None):
    T = a_t.shape[1]
    r0, K = (0, a_t.shape[0]) if rows is None else rows
    N = b.shape[1]
    tk, tn, tt = _tile(K, 1408), _tile(N, 1664), _tile(T, 2816)
    nt = T // tt
    assert r0 % tk == 0
    off = r0 // tk
    nk, nn = K // tk, N // tn

    def body(a_ref, b_ref, o_ref, acc_ref):
        t = pl.program_id(2)
        part = _nn(a_ref[...], b_ref[...])

        @pl.when(t == 0)
        def _():
            acc_ref[...] = part

        @pl.when(t > 0)
        def _():
            acc_ref[...] += part

        @pl.when(t == nt - 1)
        def _():
            o_ref[...] = acc_ref[...].astype(o_ref.dtype)

    def at(i, j, t):
        return (pl.program_id(0) == i) & (pl.program_id(1) == j) & (pl.program_id(2) == t)

    outs, got = _host_call(
        body, ex, lambda: at(0, 0, 0), lambda: at(nk - 1, nn - 1, nt - 1),
        name=name, grid=(nk, nn, nt),
        in_specs=[pl.BlockSpec((tk, tt), lambda i, j, t: (i + off, t)), pl.BlockSpec((tt, tn), lambda i, j, t: (t, j))],
        out_specs=[pl.BlockSpec((tk, tn), lambda i, j, t: (i, j))],
        out_shape=[jax.ShapeDtypeStruct((K, N), ACT)],
        scratch_shapes=[pltpu.VMEM((tk, tn), F32)], sem=("arbitrary", "arbitrary", "arbitrary"), args=(a_t, b))
    return outs[0] if ex is None else (outs[0], got)


def _post_fwd(x, o_fw, o_bw, g_src, g_blk, gain, a, w_out, ms, dvh, tm, name):
    xs, T, dm = _stream(x)
    nx = len(xs)
    hv = o_fw.shape[1]
    aw = 0 if a is None else a.shape[1]
    has_gain = gain is not None

    def body(*refs):
        refs = list(refs)
        x_refs = refs[:nx]
        of_ref, ob_ref, g_ref = refs[nx:nx + 3]
        k = nx + 3
        gain_ref = a_ref = None
        if has_gain:
            gain_ref = refs[k]
            k += 1
        if aw:
            a_ref = refs[k]
            k += 1
        w_ref, ms_ref, x1_ref, z_ref, yp_ref = refs[k:k + 5]
        o = of_ref[...].astype(F32) + ob_ref[...].astype(F32)
        gr = g_ref[...].astype(F32)
        if aw:
            z_ref[:, :aw] = _bf(a_ref[...])
        for hd in range(hv // dvh):
            sl = slice(hd * dvh, (hd + 1) * dvh)
            oh = o[:, sl]
            gh = gr[:, sl]
            r = lax.rsqrt(jnp.mean(oh * oh, axis=-1, keepdims=True) + EPS)
            y = oh * r
            if has_gain:
                y = y * gain_ref[...]
            y = y * (gh * _sig(gh))
            z_ref[:, aw + hd * dvh:aw + (hd + 1) * dvh] = _bf(y)
        yp = _nn(z_ref[...], w_ref[...])
        yp_ref[...] = _bf(yp)
        x1_ref[...] = _stream_tile(x_refs) + ms_ref[0] * yp

    ins = xs + [o_fw, o_bw, g_src]
    specs = _stream_specs(x, tm, dm) + [_rows(tm, hv), _rows(tm, hv), pl.BlockSpec((tm, hv), lambda i: (i, g_blk))]
    if has_gain:
        ins.append(gain)
        specs.append(_full(gain.shape))
    if aw:
        ins.append(a)
        specs.append(_rows(tm, aw))
    ins += [w_out, ms]
    specs += [_full(w_out.shape), _ctx_lat(dm)]
    return pl.pallas_call(
        body, name=name, grid=(T // tm,), in_specs=specs,
        out_specs=[_rows(tm, dm), _rows(tm, aw + hv), _rows(tm, dm)],
        out_shape=[jax.ShapeDtypeStruct((T, dm), F32), jax.ShapeDtypeStruct((T, aw + hv), ACT),
                   jax.ShapeDtypeStruct((T, dm), ACT)],
        compiler_params=_cp("arbitrary"),
    )(*ins)


def _post_bwd(dx1, z, yp, o_fw, o_bw, g_src, g_blk, gain, w_out, ms, aw, dvh, tm, name):
    T, dm = dx1.shape
    hv = o_fw.shape[1]
    has_gain = gain is not None

    def body(*refs):
        refs = list(refs)
        dx1_ref, z_ref, yp_ref, of_ref, ob_ref, g_ref = refs[:6]
        k = 6
        gain_ref = None
        if has_gain:
            gain_ref = refs[k]
            k += 1
        w_ref, ms_ref = refs[k:k + 2]
        k += 2
        do_ref, dgr_ref = refs[k:k + 2]
        k += 2
        da_ref = None
        if aw:
            da_ref = refs[k]
            k += 1
        dy_ref, zt_ref, dgate_ref, dgain_ref = refs[k:k + 4]
        i = pl.program_id(0)
        dx1v = dx1_ref[...]
        zt_ref[...] = z_ref[...].T
        _acc_ctx_lat(dgate_ref, i, jnp.sum(dx1v * yp_ref[...].astype(F32), axis=0, keepdims=True))
        dyb = _bf(dx1v * ms_ref[0])
        dy_ref[...] = dyb
        dz = _nt(dyb, w_ref[...])
        if aw:
            da_ref[...] = dz[:, :aw]
        o = of_ref[...].astype(F32) + ob_ref[...].astype(F32)
        gr = g_ref[...].astype(F32)
        dgain = jnp.zeros((1, dvh), F32)
        for hd in range(hv // dvh):
            sl = slice(hd * dvh, (hd + 1) * dvh)
            oh = o[:, sl]
            gh = gr[:, sl]
            dyh = dz[:, aw + hd * dvh:aw + (hd + 1) * dvh]
            r = lax.rsqrt(jnp.mean(oh * oh, axis=-1, keepdims=True) + EPS)
            n = oh * r
            s = _sig(gh)
            sl_g = gh * s
            gn = gain_ref[...] if has_gain else 1.0
            dgr_ref[:, sl] = _bf(dyh * n * gn * (s * (1.0 + gh * (1.0 - s))))
            dn = dyh * gn * sl_g
            dgain = dgain + jnp.sum(dyh * n * sl_g, axis=0, keepdims=True)
            do_ref[:, sl] = _bf(r * (dn - n * jnp.mean(dn * n, axis=-1, keepdims=True)))
        _acc_all(dgain_ref, i, dgain)

    ins = [dx1, z, yp, o_fw, o_bw, g_src]
    specs = [_rows(tm, dm), _rows(tm, aw + hv), _rows(tm, dm), _rows(tm, hv), _rows(tm, hv),
             pl.BlockSpec((tm, hv), lambda i: (i, g_blk))]
    if has_gain:
        ins.append(gain)
        specs.append(_full(gain.shape))
    ins += [w_out, ms]
    specs += [_full(w_out.shape), _ctx_lat(dm)]
    out_specs = [_rows(tm, hv), _rows(tm, hv)]
    out_shape = [jax.ShapeDtypeStruct((T, hv), ACT), jax.ShapeDtypeStruct((T, hv), ACT)]
    if aw:
        out_specs.append(_rows(tm, aw))
        out_shape.append(jax.ShapeDtypeStruct((T, aw), F32))
    out_specs += [_rows(tm, dm), _cols(aw + hv, tm), _ctx_lat(dm), _whole((1, dvh))]
    out_shape += [jax.ShapeDtypeStruct((T, dm), ACT), jax.ShapeDtypeStruct((aw + hv, T), ACT),
                  jax.ShapeDtypeStruct((2, 1, dm), F32), jax.ShapeDtypeStruct((1, dvh), F32)]
    return pl.pallas_call(
        body, name=name, grid=(T // tm,), in_specs=specs, out_specs=out_specs, out_shape=out_shape,
        compiler_params=_cp("arbitrary"),
    )(*ins)


def _loss_bwd(x, target, tm, name):
    T, dm = x.shape

    def body(x_ref, t_ref, dx_ref, loss_ref):
        i = pl.program_id(0)

        @pl.when(i == 0)
        def _():
            dx_ref[...] = jnp.zeros_like(dx_ref)
            loss_ref[...] = jnp.zeros_like(loss_ref)

        @pl.when(i > 0)
        def _():
            e = x_ref[...] - t_ref[...]
            dx_ref[...] = e * (1.0 / dm)
            loss_ref[...] += jnp.sum(e * e) * (0.5 / dm)

    return pl.pallas_call(
        body, name=name, grid=(T // tm,),
        in_specs=[_rows(tm, dm), pl.BlockSpec((tm, dm), lambda i: (jnp.maximum(i - 1, 0), 0))],
        out_specs=[_rows(tm, dm), _whole((1, 1))],
        out_shape=[jax.ShapeDtypeStruct((T, dm), F32), jax.ShapeDtypeStruct((1, 1), F32)],
        compiler_params=_cp("arbitrary"),
    )(x, target)


def _swap_matrix():
    r = lax.broadcasted_iota(jnp.int32, (HEAD_DIM, HEAD_DIM), 0)
    c = lax.broadcasted_iota(jnp.int32, (HEAD_DIM, HEAD_DIM), 1)
    return jnp.where((r + HEAD_DIM // 2) % HEAD_DIM == c, 1.0, 0.0).astype(BF16)


def _qk_prep_fwd(raw, gains, cos2, sin2, tq, name):
    nh, T, hd = raw.shape

    def body(x_ref, g_ref, c_ref, s_ref, o_ref):
        hidx = pl.program_id(0)
        xv = x_ref[0]
        r = lax.rsqrt(jnp.mean(xv * xv, axis=-1, keepdims=True) + EPS)
        n = xv * r * g_ref[0]
        y = n * c_ref[...] + _nn3r(n, _swap_matrix()) * s_ref[...]
        sc = jnp.where(hidx < ATTN_HEADS, HEAD_DIM ** -0.5, 1.0)
        o_ref[0] = _bf(y * sc)

    return pl.pallas_call(
        body, name=name, grid=(nh, T // tq),
        in_specs=[pl.BlockSpec((1, tq, hd), lambda h, i: (h, i, 0)), pl.BlockSpec((1, 1, hd), lambda h, i: (h, 0, 0)),
                  pl.BlockSpec((tq, hd), lambda h, i: (i, 0)), pl.BlockSpec((tq, hd), lambda h, i: (i, 0))],
        out_specs=pl.BlockSpec((1, tq, hd), lambda h, i: (h, i, 0)),
        out_shape=jax.ShapeDtypeStruct((nh, T, hd), ACT),
        compiler_params=_cp("arbitrary", "arbitrary"),
    )(raw, gains, cos2, sin2)


def _qk_prep_bwd(dy, raw, gains, cos2, sin2, tq, name):
    nh, T, hd = raw.shape

    def body(dy_ref, x_ref, g_ref, c_ref, s_ref, dx_ref, dg_ref):
        hidx = pl.program_id(0)
        i = pl.program_id(1)
        xv = x_ref[0]
        g = g_ref[0]
        r = lax.rsqrt(jnp.mean(xv * xv, axis=-1, keepdims=True) + EPS)
        xhat = xv * r
        sc = jnp.where(hidx < ATTN_HEADS, HEAD_DIM ** -0.5, 1.0)
        dyv = dy_ref[0] * sc
        dn = dyv * c_ref[...] + _nn3r(dyv * s_ref[...], _swap_matrix())
        _acc_all(dg_ref, i, jnp.sum(dn * xhat, axis=0, keepdims=True))
        dxh = dn * g
        dx_ref[0] = r * (dxh - xhat * jnp.mean(dxh * xhat, axis=-1, keepdims=True))

    return pl.pallas_call(
        body, name=name, grid=(nh, T // tq),
        in_specs=[pl.BlockSpec((1, tq, hd), lambda h, i: (h, i, 0)), pl.BlockSpec((1, tq, hd), lambda h, i: (h, i, 0)),
                  pl.BlockSpec((1, 1, hd), lambda h, i: (h, 0, 0)),
                  pl.BlockSpec((tq, hd), lambda h, i: (i, 0)), pl.BlockSpec((tq, hd), lambda h, i: (i, 0))],
        out_specs=[pl.BlockSpec((1, tq, hd), lambda h, i: (h, i, 0)), pl.BlockSpec((1, 1, hd), lambda h, i: (h, 0, 0))],
        out_shape=[jax.ShapeDtypeStruct((nh, T, hd), F32), jax.ShapeDtypeStruct((nh, 1, hd), F32)],
        compiler_params=_cp("arbitrary", "arbitrary"),
    )(dy, raw, gains, cos2, sin2)


def _attn_scores(q, k_ref, i, lc, T, sink):
    blk = ATTN_BLOCK
    kc = k_ref[0, pl.ds(blk, lc), :]
    kw = k_ref[0, pl.ds(pl.multiple_of(i * blk, blk), 3 * blk), :]
    s_c = _nt(q, kc)
    s_w = _nt(q, kw)
    row = lax.broadcasted_iota(jnp.int32, (4 * blk, 1), 0)
    qpos = i * blk + (row & (blk - 1))
    kpos = (i - 1) * blk + lax.broadcasted_iota(jnp.int32, (1, 3 * blk), 1)
    valid = (qpos >= lc) & (kpos >= lc) & (kpos < T) & (jnp.abs(kpos - qpos) <= WINDOW)
    s_w = jnp.where(valid, s_w, NEG)
    return kc, kw, s_c, s_w


def _attn_fwd(qt, kp, vp, sinkb, lc, name, ex=None):
    nh, T, hd = qt.shape
    blk = ATTN_BLOCK
    g = nh // ATTN_KV

    def body(q_ref, k_ref, v_ref, sink_ref, o_ref, lse_ref):
        i = pl.program_id(1)
        q = q_ref[...].reshape(g * blk, hd)
        sink = sink_ref[0]
        kc, kw, s_c, s_w = _attn_scores(q, k_ref, i, lc, T, sink)
        m = jnp.maximum(jnp.maximum(jnp.max(s_c, axis=-1, keepdims=True), jnp.max(s_w, axis=-1, keepdims=True)), sink)
        e_c = jnp.exp(s_c - m)
        e_w = jnp.exp(s_w - m)
        den = jnp.exp(sink - m) + jnp.sum(e_c, axis=-1, keepdims=True) + jnp.sum(e_w, axis=-1, keepdims=True)
        inv = 1.0 / den
        vc = v_ref[0, pl.ds(blk, lc), :]
        vw = v_ref[0, pl.ds(pl.multiple_of(i * blk, blk), 3 * blk), :]
        o = _nn(_bf(e_c * inv), vc) + _nn(_bf(e_w * inv), vw)
        o_ref[...] = o.reshape(g, blk, hd)
        lse_ref[...] = (m + jnp.log(den)).reshape(g, blk, 1)

    nb = T // blk
    return _host_call(
        body, ex, lambda: (pl.program_id(0) == 0) & (pl.program_id(1) == 0),
        lambda: (pl.program_id(0) == ATTN_KV - 1) & (pl.program_id(1) == nb - 1),
        name=name, grid=(ATTN_KV, nb),
        in_specs=[pl.BlockSpec((g, blk, hd), lambda kv, i: (kv, i, 0)),
                  pl.BlockSpec((1, T + 2 * blk, hd), lambda kv, i: (kv, 0, 0)),
                  pl.BlockSpec((1, T + 2 * blk, hd), lambda kv, i: (kv, 0, 0)),
                  pl.BlockSpec((1, g * blk, 1), lambda kv, i: (kv, 0, 0))],
        out_specs=[pl.BlockSpec((g, blk, hd), lambda kv, i: (kv, i, 0)),
                   pl.BlockSpec((g, blk, 1), lambda kv, i: (kv, i, 0))],
        out_shape=[jax.ShapeDtypeStruct((nh, T, hd), F32), jax.ShapeDtypeStruct((nh, T, 1), F32)],
        scratch_shapes=[], sem=("arbitrary", "arbitrary"), args=(qt, kp, vp, sinkb))


def _attn_bwd(qt, kp, vp, sinkb, o, lse, do, lc, name):
    nh, T, hd = qt.shape
    blk = ATTN_BLOCK
    g = nh // ATTN_KV

    def body(q_ref, k_ref, v_ref, sink_ref, o_ref, lse_ref, do_ref, dq_ref, dk_ref, dv_ref, ds_ref):
        i = pl.program_id(1)

        @pl.when(i == 0)
        def _():
            dk_ref[...] = jnp.zeros_like(dk_ref)
            dv_ref[...] = jnp.zeros_like(dv_ref)
            ds_ref[...] = jnp.zeros_like(ds_ref)

        q = q_ref[...].reshape(g * blk, hd)
        sink = sink_ref[0]
        lse = lse_ref[...].reshape(g * blk, 1)
        dov = do_ref[...].reshape(g * blk, hd)
        delta = jnp.sum(dov * o_ref[...].reshape(g * blk, hd), axis=-1, keepdims=True)
        kc, kw, s_c, s_w = _attn_scores(q, k_ref, i, lc, T, sink)
        p_c = jnp.exp(s_c - lse)
        p_w = jnp.exp(s_w - lse)
        win = pl.ds(pl.multiple_of(i * blk, blk), 3 * blk)
        vc = v_ref[0, pl.ds(blk, lc), :]
        vw = v_ref[0, win, :]
        dob = _bf(dov)
        ds_c = _bf(p_c * (_nt(dob, vc) - delta))
        ds_w = _bf(p_w * (_nt(dob, vw) - delta))
        dsr = -jnp.exp(sink - lse) * delta
        for hh in range(g):
            ds_ref[0, hh:hh + 1, :] += jnp.sum(dsr[hh * blk:(hh + 1) * blk, :], axis=0, keepdims=True)
        dq_ref[...] = (_nn(ds_c, kc) + _nn(ds_w, kw)).reshape(g, blk, hd)
        dk_ref[0, pl.ds(blk, lc), :] += _tn(ds_c, q)
        dk_ref[0, win, :] += _tn(ds_w, q)
        dv_ref[0, pl.ds(blk, lc), :] += _tn(_bf(p_c), dob)
        dv_ref[0, win, :] += _tn(_bf(p_w), dob)

    qspec = pl.BlockSpec((g, blk, hd), lambda kv, i: (kv, i, 0))
    kspec = pl.BlockSpec((1, T + 2 * blk, hd), lambda kv, i: (kv, 0, 0))
    lspec = pl.BlockSpec((g, blk, 1), lambda kv, i: (kv, i, 0))
    return pl.pallas_call(
        body, name=name, grid=(ATTN_KV, T // blk),
        in_specs=[qspec, kspec, kspec, pl.BlockSpec((1, g * blk, 1), lambda kv, i: (kv, 0, 0)), qspec, lspec, qspec],
        out_specs=[qspec, kspec, kspec, pl.BlockSpec((1, g, 1), lambda kv, i: (kv, 0, 0))],
        out_shape=[jax.ShapeDtypeStruct((nh, T, hd), F32), jax.ShapeDtypeStruct((ATTN_KV, T + 2 * blk, hd), F32),
                   jax.ShapeDtypeStruct((ATTN_KV, T + 2 * blk, hd), F32), jax.ShapeDtypeStruct((ATTN_KV, g, 1), F32)],
        compiler_params=_cp("arbitrary", "arbitrary"),
    )(qt, kp, vp, sinkb, o, lse, do)


PAIR = 2 * HEAD_DIM
N_PAIRS = (ATTN_HEADS + ATTN_KV) // 2


def _lanes():
    return lax.broadcasted_iota(jnp.int32, (1, PAIR), 1)


def _swap32(v):
    first_half = (_lanes() & (HEAD_DIM // 2)) == 0
    return jnp.where(first_half, pltpu.roll(v, PAIR - HEAD_DIM // 2, 1), pltpu.roll(v, HEAD_DIM // 2, 1))


def _head_mean(v):
    r = lax.broadcasted_iota(jnp.int32, (PAIR, PAIR), 0)
    c = lax.broadcasted_iota(jnp.int32, (PAIR, PAIR), 1)
    same = jnp.where((r >= HEAD_DIM) == (c >= HEAD_DIM), 1.0, 0.0).astype(BF16)
    return _nn3r(v, same) * (1.0 / HEAD_DIM)


def _qk_tile_fwd(pa, g_ref, cosv, sinv, q_ref, k_ref, v_ref):
    qw = ATTN_HEADS * HEAD_DIM
    for p in range(N_PAIRS):
        xv = pa[:, p * PAIR:(p + 1) * PAIR]
        n = xv * lax.rsqrt(_head_mean(xv * xv) + EPS) * g_ref[p]
        y = n * cosv + _swap32(n) * sinv
        if p < N_PAIRS - 1:
            q_ref[:, p * PAIR:(p + 1) * PAIR] = _bf(y * HEAD_DIM ** -0.5)
        else:
            k_ref[...] = _bf(y)
    v_ref[...] = _bf(pa[:, qw + PAIR:])


def _qk_tile_bwd(dq_ref, dk_ref, pa_ref, g_ref, cosv, sinv):
    dxs, dgs = [], []
    for p in range(N_PAIRS):
        sl = slice(p * PAIR, (p + 1) * PAIR)
        xv = pa_ref[:, sl]
        r = lax.rsqrt(_head_mean(xv * xv) + EPS)
        xhat = xv * r
        dy = dq_ref[:, sl] * HEAD_DIM ** -0.5 if p < N_PAIRS - 1 else dk_ref[...]
        dn = dy * cosv + _swap32(dy * sinv)
        dgs.append(jnp.sum(dn * xhat, axis=0, keepdims=True))
        dxh = dn * g_ref[p]
        dxs.append(r * (dxh - xhat * _head_mean(dxh * xhat)))
    return jnp.concatenate(dxs, axis=1), dgs


def _qk_slab_fwd(pa, gains, cosp, sinp, tm, name):
    T = pa.shape[0]
    qw = ATTN_HEADS * HEAD_DIM

    def body(pa_ref, g_ref, c_ref, s_ref, q_ref, k_ref, v_ref):
        cosv, sinv = c_ref[...], s_ref[...]
        for p in range(N_PAIRS):
            xv = pa_ref[:, p * PAIR:(p + 1) * PAIR]
            n = xv * lax.rsqrt(_head_mean(xv * xv) + EPS) * g_ref[p]
            y = n * cosv + _swap32(n) * sinv
            if p < N_PAIRS - 1:
                q_ref[:, p * PAIR:(p + 1) * PAIR] = _bf(y * HEAD_DIM ** -0.5)
            else:
                k_ref[...] = _bf(y)
        v_ref[...] = _bf(pa_ref[:, qw + PAIR:])

    return pl.pallas_call(
        body, name=name, grid=(T // tm,),
        in_specs=[_rows(tm, pa.shape[1]), _full(gains.shape), _rows(tm, PAIR), _rows(tm, PAIR)],
        out_specs=[_rows(tm, qw), _rows(tm, PAIR), _rows(tm, PAIR)],
        out_shape=[jax.ShapeDtypeStruct((T, qw), ACT), jax.ShapeDtypeStruct((T, PAIR), ACT),
                   jax.ShapeDtypeStruct((T, PAIR), ACT)],
        compiler_params=_cp("arbitrary"),
    )(pa, gains, cosp, sinp)


def _qk_slab_bwd(dq, dk, pa, gains, cosp, sinp, tm, name):
    T = pa.shape[0]
    qw = ATTN_HEADS * HEAD_DIM

    def body(dq_ref, dk_ref, pa_ref, g_ref, c_ref, s_ref, dx_ref, dg_ref):
        i = pl.program_id(0)
        cosv, sinv = c_ref[...], s_ref[...]
        for p in range(N_PAIRS):
            sl = slice(p * PAIR, (p + 1) * PAIR)
            xv = pa_ref[:, sl]
            r = lax.rsqrt(_head_mean(xv * xv) + EPS)
            xhat = xv * r
            dy = dq_ref[:, sl] * HEAD_DIM ** -0.5 if p < N_PAIRS - 1 else dk_ref[...]
            dn = dy * cosv + _swap32(dy * sinv)
            _acc_all(dg_ref.at[p], i, jnp.sum(dn * xhat, axis=0, keepdims=True))
            dxh = dn * g_ref[p]
            dx_ref[:, sl] = r * (dxh - xhat * _head_mean(dxh * xhat))

    return pl.pallas_call(
        body, name=name, grid=(T // tm,),
        in_specs=[_rows(tm, qw), _rows(tm, PAIR), _rows(tm, qw + PAIR), _full(gains.shape), _rows(tm, PAIR), _rows(tm, PAIR)],
        out_specs=[_rows(tm, qw + PAIR), _whole(gains.shape)],
        out_shape=[jax.ShapeDtypeStruct((T, qw + PAIR), F32), jax.ShapeDtypeStruct(gains.shape, F32)],
        compiler_params=_cp("arbitrary"),
    )(dq, dk, pa, gains, cosp, sinp)


def _attn_window(ref, i, nb):
    blk = ATTN_BLOCK
    starts = [pl.multiple_of(jnp.clip(i + d, 0, nb - 1) * blk, blk) for d in (-1, 0, 1)]
    return starts, jnp.concatenate([ref[pl.ds(s, blk), :] for s in starts], axis=0)


GROUP_HEADS = 2


def _head_groups(n):
    g = ATTN_HEADS // ATTN_KV
    return [(kv, [kv * g + s + j for j in range(n)]) for kv in range(ATTN_KV) for s in range(0, g, n)]


def _attn_mask(i, lc, T, rows):
    blk = ATTN_BLOCK
    row = lax.broadcasted_iota(jnp.int32, (rows, 1), 0)
    qpos = i * blk + (row & (blk - 1))
    kpos = (i - 1) * blk + lax.broadcasted_iota(jnp.int32, (1, 3 * blk), 1)
    return (qpos >= lc) & (kpos >= lc) & (kpos < T) & (jnp.abs(kpos - qpos) <= WINDOW)


def _to_kv_half(v, head, kv):
    return v if head % 2 == kv else pltpu.roll(v, HEAD_DIM, 1)


def _attn_slab_fwd(qt, ks, vs, sinkb, lc, name, ex=None):
    T = qt.shape[0]
    blk = ATTN_BLOCK
    nb = T // blk
    g = ATTN_HEADS // ATTN_KV

    def body(q_ref, k_ref, v_ref, sink_ref, o_ref, lse_ref):
        i = pl.program_id(0)
        lane = _lanes()
        valid = _attn_mask(i, lc, T, GROUP_HEADS * blk)
        kc_all, vc = k_ref[0:lc, :], v_ref[0:lc, :]
        _, kw_all = _attn_window(k_ref, i, nb)
        _, vw = _attn_window(v_ref, i, nb)
        kc, kw = [], []
        for kv in range(ATTN_KV):
            mine = (lane >= kv * HEAD_DIM) & (lane < (kv + 1) * HEAD_DIM)
            kc.append(jnp.where(mine, kc_all, jnp.zeros_like(kc_all)))
            kw.append(jnp.where(mine, kw_all, jnp.zeros_like(kw_all)))
        groups = _head_groups(GROUP_HEADS)
        qg = [jnp.concatenate([_to_kv_half(q_ref[:, (h // 2) * PAIR:(h // 2 + 1) * PAIR], h, kv) for h in heads], axis=0)
              for kv, heads in groups]
        sinks = [sink_ref[kv, (heads[0] - kv * g) * blk:(heads[-1] + 1 - kv * g) * blk] for kv, heads in groups]
        s_c = [_nt(q, kc[kv]) for q, (kv, _) in zip(qg, groups)]
        s_w = [jnp.where(valid, _nt(q, kw[kv]), NEG) for q, (kv, _) in zip(qg, groups)]
        m = [jnp.maximum(jnp.maximum(jnp.max(a, axis=-1, keepdims=True), jnp.max(b, axis=-1, keepdims=True)), s)
             for a, b, s in zip(s_c, s_w, sinks)]
        e_c = [jnp.exp(a - mm) for a, mm in zip(s_c, m)]
        e_w = [jnp.exp(b - mm) for b, mm in zip(s_w, m)]
        den = [jnp.exp(s - mm) + jnp.sum(a, axis=-1, keepdims=True) + jnp.sum(b, axis=-1, keepdims=True)
               for s, mm, a, b in zip(sinks, m, e_c, e_w)]
        inv = [1.0 / d for d in den]
        og = [_nn(_bf(a * r), vc) + _nn(_bf(b * r), vw) for a, b, r in zip(e_c, e_w, inv)]
        placed = [None] * ATTN_HEADS
        for (kv, heads), o2, mm, d in zip(groups, og, m, den):
            lse_ref[heads[0]:heads[-1] + 1] = (mm + jnp.log(d)).reshape(len(heads), blk, 1)
            for j, h in enumerate(heads):
                placed[h] = _to_kv_half(o2[j * blk:(j + 1) * blk], h, kv)
        for p in range(ATTN_HEADS // 2):
            o_ref[:, p * PAIR:(p + 1) * PAIR] = jnp.where(lane < HEAD_DIM, placed[2 * p], placed[2 * p + 1])

    qw = ATTN_HEADS * HEAD_DIM
    return _host_call(
        body, ex, lambda: pl.program_id(0) == 0, lambda: pl.program_id(0) == nb - 1,
        name=name, grid=(nb,),
        in_specs=[_rows(blk, qw), _full((T, PAIR)), _full((T, PAIR)), _full(sinkb.shape)],
        out_specs=[_rows(blk, qw), pl.BlockSpec((ATTN_HEADS, blk, 1), lambda i: (0, i, 0))],
        out_shape=[jax.ShapeDtypeStruct((T, qw), F32), jax.ShapeDtypeStruct((ATTN_HEADS, T, 1), F32)],
        scratch_shapes=[], sem=("arbitrary",), args=(qt, ks, vs, sinkb))


def _attn_slab_bwd(qt, ks, vs, sinkb, o, lse, do, lc, name, ex=None):
    T = qt.shape[0]
    blk = ATTN_BLOCK
    nb = T // blk
    g = ATTN_HEADS // ATTN_KV

    def body(q_ref, k_ref, v_ref, sink_ref, o_ref, lse_ref, do_ref, dq_ref, dk_ref, dv_ref, ds_ref):
        i = pl.program_id(0)

        @pl.when(i == 0)
        def _():
            dk_ref[...] = jnp.zeros_like(dk_ref)
            dv_ref[...] = jnp.zeros_like(dv_ref)
            ds_ref[...] = jnp.zeros_like(ds_ref)

        lane = _lanes()
        valid = _attn_mask(i, lc, T, g * blk)
        kc_all, vc_all = k_ref[0:lc, :], v_ref[0:lc, :]
        starts, kw_all = _attn_window(k_ref, i, nb)
        _, vw_all = _attn_window(v_ref, i, nb)
        dq_pairs = [jnp.zeros((blk, PAIR), F32) for _ in range(ATTN_HEADS // 2)]
        for kv in range(ATTN_KV):
            mine = (lane >= kv * HEAD_DIM) & (lane < (kv + 1) * HEAD_DIM)

            def only(v):
                return jnp.where(mine, v, jnp.zeros_like(v))

            kc, kw, vc, vw = only(kc_all), only(kw_all), only(vc_all), only(vw_all)
            heads = [kv * g + j for j in range(g)]
            qs, dos, deltas = [], [], []
            for h in heads:
                sl = slice((h // 2) * PAIR, (h // 2 + 1) * PAIR)
                dov = do_ref[:, sl]
                qs.append(_to_kv_half(q_ref[:, sl], h, kv))
                dos.append(_bf(_to_kv_half(dov, h, kv)))
                own = (lane < HEAD_DIM) if h % 2 == 0 else (lane >= HEAD_DIM)
                deltas.append(jnp.sum(jnp.where(own, dov * o_ref[:, sl], 0.0), axis=-1, keepdims=True))
            q4, do4, delta = jnp.concatenate(qs, axis=0), jnp.concatenate(dos, axis=0), jnp.concatenate(deltas, axis=0)
            sink = sink_ref[kv]
            lse = lse_ref[kv * g:(kv + 1) * g].reshape(g * blk, 1)
            p_c = jnp.exp(_nt(q4, kc) - lse)
            p_w = jnp.exp(jnp.where(valid, _nt(q4, kw), NEG) - lse)
            ds_c = _bf(p_c * (_nt(do4, vc) - delta))
            ds_w = _bf(p_w * (_nt(do4, vw) - delta))
            dsr = -jnp.exp(sink - lse) * delta
            dq4 = _nn(ds_c, kc) + _nn(ds_w, kw)
            for j, h in enumerate(heads):
                ds_ref[h:h + 1, :] += jnp.sum(dsr[j * blk:(j + 1) * blk, :], axis=0, keepdims=True)
                dq_pairs[h // 2] = dq_pairs[h // 2] + _to_kv_half(dq4[j * blk:(j + 1) * blk], h, kv)
            dk_ref[0:lc, :] += only(_tn(ds_c, q4))
            dv_ref[0:lc, :] += only(_tn(_bf(p_c), do4))
            dkw = only(_tn(ds_w, q4))
            dvw = only(_tn(_bf(p_w), do4))
            for b, s in enumerate(starts):
                dk_ref[pl.ds(s, blk), :] += dkw[b * blk:(b + 1) * blk]
                dv_ref[pl.ds(s, blk), :] += dvw[b * blk:(b + 1) * blk]
        for p in range(ATTN_HEADS // 2):
            dq_ref[:, p * PAIR:(p + 1) * PAIR] = dq_pairs[p]

    qw = ATTN_HEADS * HEAD_DIM
    lspec = pl.BlockSpec((ATTN_HEADS, blk, 1), lambda i: (0, i, 0))
    return _host_call(
        body, ex, lambda: pl.program_id(0) == 0, lambda: pl.program_id(0) == nb - 1,
        name=name, grid=(nb,),
        in_specs=[_rows(blk, qw), _full((T, PAIR)), _full((T, PAIR)), _full(sinkb.shape), _rows(blk, qw), lspec,
                  _rows(blk, qw)],
        out_specs=[_rows(blk, qw), _whole((T, PAIR)), _whole((T, PAIR)), _whole((ATTN_HEADS, 1))],
        out_shape=[jax.ShapeDtypeStruct((T, qw), F32), jax.ShapeDtypeStruct((T, PAIR), F32),
                   jax.ShapeDtypeStruct((T, PAIR), F32), jax.ShapeDtypeStruct((ATTN_HEADS, 1), F32)],
        scratch_shapes=[], sem=("arbitrary",), args=(qt, ks, vs, sinkb, o, lse, do))


def _fw_chunk(s, nc, nt):
    return s


def _bw_chunk(s, nc, nt):
    return jnp.where(s < nc, nc - 1 - s, nt - 1 - (s - nc))


def _tri(c, rev):
    r = lax.broadcasted_iota(jnp.int32, (c, c), 0)
    k = lax.broadcasted_iota(jnp.int32, (c, c), 1)
    return (k >= r) if rev else (k <= r)


def _gla_gates(z, lb, rev):
    c = HG_CHUNK
    sg = _sig(z)
    f = lb + (1.0 - lb) * sg
    cum = _nn3(jnp.where(_tri(c, rev), 1.0, 0.0).astype(BF16), jnp.log(f))
    mid = c - 1 - c // 2 if rev else c // 2
    last = 0 if rev else c - 1
    return sg, f, cum, cum[mid:mid + 1], cum[last:last + 1], last


def _lower_bound(lbraw_ref):
    lr = lbraw_ref[...]
    return _sig(lr[0:1] - lr[1:2])


def _gla_fwd(pb, lbraw, lc, name, ex=None):
    T = pb.shape[0]
    c, hw, d = HG_CHUNK, HG_HEADS * HG_D, HG_D
    nt, nc = T // c, lc // c
    orders = (_fw_chunk, _bw_chunk)

    def body(qf, zf, vf, qb, zb, vb, lb_ref, of_ref, ob_ref, sf_ref, sb_ref, st_ref):
        @pl.when(pl.program_id(0) == 0)
        def _():
            st_ref[...] = jnp.zeros_like(st_ref)

        lb = _lower_bound(lb_ref)
        dirs = ((qf, zf, vf, of_ref, sf_ref), (qb, zb, vb, ob_ref, sb_ref))
        combos = [(dr, h, slice(h * d, (h + 1) * d)) for dr in range(2) for h in range(HG_HEADS)]
        prep = []
        for dr, (q_ref, z_ref, v_ref, _, _) in enumerate(dirs):
            rev = dr == 1
            qr = q_ref[...]
            q = qr * _sig(qr)
            _, f, cum, ref, last, _ = _gla_gates(z_ref[...], lb, rev)
            k = 1.0 - f
            prep.append(dict(q1=_bf(q * jnp.exp(cum - ref)), k1=_bf(k * jnp.exp(ref - cum)), q2=_bf(q * jnp.exp(cum)),
                             k2=_bf(k * jnp.exp(last - cum)), el=jnp.exp(last), v=_bf(v_ref[...]), mask=_tri(c, rev)))
        a = [_bf(jnp.where(prep[dr]["mask"], _nt(prep[dr]["q1"][:, sl], prep[dr]["k1"][:, sl]), 0.0)) for dr, _, sl in combos]
        for (dr, h, sl), a_h in zip(combos, a):
            p = prep[dr]
            o_ref, s_ref = dirs[dr][3], dirs[dr][4]
            st = st_ref[dr, h]
            stb = _bf(st)
            s_ref[0, h] = stb
            o_ref[:, sl] = _nn(a_h, p["v"][:, sl]) + _nt(p["q2"][:, sl], stb)
            st_ref[dr, h] = st * p["el"][:, sl] + _tn(p["v"][:, sl], p["k2"][:, sl])

    def col(order, blkcol):
        return pl.BlockSpec((c, hw), lambda s: (order(s, nc, nt), blkcol))

    def st_spec(order):
        return pl.BlockSpec((1, HG_HEADS, d, d), lambda s: (order(s, nc, nt), 0, 0, 0))

    in_specs = []
    for dr, order in enumerate(orders):
        in_specs += [col(order, 0), col(order, 1 + dr), col(order, 3)]
    in_specs.append(_full(lbraw.shape))
    return _host_call(
        body, ex, lambda: pl.program_id(0) == 0, lambda: pl.program_id(0) == nt - 1,
        name=name, grid=(nt,), in_specs=in_specs,
        out_specs=[col(_fw_chunk, 0), col(_bw_chunk, 0), st_spec(_fw_chunk), st_spec(_bw_chunk)],
        out_shape=[jax.ShapeDtypeStruct((T, hw), F32), jax.ShapeDtypeStruct((T, hw), F32),
                   jax.ShapeDtypeStruct((nt, HG_HEADS, d, d), ACT), jax.ShapeDtypeStruct((nt, HG_HEADS, d, d), ACT)],
        scratch_shapes=[pltpu.VMEM((2, HG_HEADS, d, d), F32)], sem=("arbitrary",),
        args=(pb, pb, pb, pb, pb, pb, lbraw))


def _gla_bwd(pb, lbraw, s_fw, s_bw, do, lc, name, ex=None):
    T = pb.shape[0]
    c, hw, d = HG_CHUNK, HG_HEADS * HG_D, HG_D
    nt, nc = T // c, lc // c

    def rfw(s, nc_, nt_):
        return _fw_chunk(nt_ - 1 - s, nc_, nt_)

    def rbw(s, nc_, nt_):
        return _bw_chunk(nt_ - 1 - s, nc_, nt_)

    def body(qf, zf, vf, sf, dof, qb, zb, vb, sb, dob_, lb_ref,
             dqf, dzf, dvf, dqb, dzb, dvb, dlb_ref, dst_ref):
        step = pl.program_id(0)

        @pl.when(step == 0)
        def _():
            dst_ref[...] = jnp.zeros_like(dst_ref)

        lb = _lower_bound(lb_ref)
        sets = ((qf, zf, vf, sf, dof, dqf, dzf, dvf), (qb, zb, vb, sb, dob_, dqb, dzb, dvb))
        combos = [(dr, h, slice(h * d, (h + 1) * d)) for dr in range(2) for h in range(HG_HEADS)]
        prep = []
        for dr, (q_ref, z_ref, v_ref, _, do_ref, _, _, _) in enumerate(sets):
            rev = dr == 1
            qr = q_ref[...]
            sq = _sig(qr)
            q = qr * sq
            sg, f, cum, ref, last, last_row = _gla_gates(z_ref[...], lb, rev)
            k = 1.0 - f
            e_qr, e_kr, e_q, e_kl = jnp.exp(cum - ref), jnp.exp(ref - cum), jnp.exp(cum), jnp.exp(last - cum)
            q1, k1, q2, k2 = q * e_qr, k * e_kr, q * e_q, k * e_kl
            prep.append(dict(qr=qr, sq=sq, sg=sg, f=f, e_qr=e_qr, e_kr=e_kr, e_q=e_q, e_kl=e_kl, el=jnp.exp(last),
                             q1=q1, k1=k1, q2=q2, k2=k2, q1b=_bf(q1), k1b=_bf(k1), q2b=_bf(q2), k2b=_bf(k2),
                             vb=_bf(v_ref[...]), dob=_bf(do_ref[...]), mask=_tri(c, rev), last_row=last_row,
                             acc_t=jnp.where(_tri(c, not rev), 1.0, 0.0).astype(BF16)))
        a = [_bf(jnp.where(prep[dr]["mask"], _nt(prep[dr]["q1b"][:, sl], prep[dr]["k1b"][:, sl]), 0.0)) for dr, _, sl in combos]
        da = [_bf(jnp.where(prep[dr]["mask"], _nt(prep[dr]["dob"][:, sl], prep[dr]["vb"][:, sl]), 0.0)) for dr, _, sl in combos]
        parts = [dict(dq1=[], dk1=[], dq2=[], dk2=[], dls=[]) for _ in range(2)]
        for (dr, h, sl), a_h, da_h in zip(combos, a, da):
            p = prep[dr]
            s_ref, dv_ref = sets[dr][3], sets[dr][7]
            stb = s_ref[0, h]
            dst = dst_ref[dr, h]
            dstb = _bf(dst)
            dob_h, vb_h = p["dob"][:, sl], p["vb"][:, sl]
            dv_ref[:, sl] = _bf(_tn(a_h, dob_h) + _nt(p["k2b"][:, sl], dstb))
            parts[dr]["dq1"].append(_nn(da_h, p["k1b"][:, sl]))
            parts[dr]["dk1"].append(_tn(da_h, p["q1b"][:, sl]))
            parts[dr]["dq2"].append(_nn(dob_h, stb))
            parts[dr]["dk2"].append(_nn(vb_h, dstb))
            el_h = p["el"][:, sl]
            dst_ref[dr, h] = _tn(dob_h, p["q2b"][:, sl]) + dst * el_h
            parts[dr]["dls"].append(jnp.sum(dst * stb.astype(F32), axis=0, keepdims=True) * el_h)
        dlb_tot = jnp.zeros((1, hw), F32)
        for dr in range(2):
            p = prep[dr]
            dq_ref, dz_ref = sets[dr][5], sets[dr][6]
            dq1, dk1, dq2, dk2, dls = (jnp.concatenate(parts[dr][n], axis=1) for n in ("dq1", "dk1", "dq2", "dk2", "dls"))
            dq = dq1 * p["e_qr"] + dq2 * p["e_q"]
            dk = dk1 * p["e_kr"] + dk2 * p["e_kl"]
            dcum = dq1 * p["q1"] - dk1 * p["k1"] + dq2 * p["q2"] - dk2 * p["k2"]
            dlast = jnp.sum(dk2 * p["k2"], axis=0, keepdims=True) + dls
            rowid = lax.broadcasted_iota(jnp.int32, (c, 1), 0)
            dcum = dcum + jnp.where(rowid == p["last_row"], dlast, 0.0)
            df = _nn3(p["acc_t"], dcum) / p["f"] - dk
            sg = p["sg"]
            dz_ref[...] = _bf(df * (1.0 - lb) * sg * (1.0 - sg))
            dlb_tot = dlb_tot + jnp.sum(df * (1.0 - sg), axis=0, keepdims=True)
            dq_ref[...] = _bf(dq * (p["sq"] * (1.0 + p["qr"] * (1.0 - p["sq"]))))
        _acc_all(dlb_ref, step, dlb_tot)

    def col(order, blkcol):
        return pl.BlockSpec((c, hw), lambda s: (order(s, nc, nt), blkcol))

    def st_spec(order):
        return pl.BlockSpec((1, HG_HEADS, d, d), lambda s: (order(s, nc, nt), 0, 0, 0))

    in_specs = []
    for dr, order in enumerate((rfw, rbw)):
        in_specs += [col(order, 0), col(order, 1 + dr), col(order, 3), st_spec(order), col(order, 0)]
    in_specs.append(_full(lbraw.shape))
    out_specs = [col(rfw, 0)] * 3 + [col(rbw, 0)] * 3 + [_whole((1, hw))]
    out_shape = [jax.ShapeDtypeStruct((T, hw), ACT)] * 6 + [jax.ShapeDtypeStruct((1, hw), F32)]
    return _host_call(
        body, ex, lambda: pl.program_id(0) == 0, lambda: pl.program_id(0) == nt - 1,
        name=name, grid=(nt,), in_specs=in_specs, out_specs=out_specs, out_shape=out_shape,
        scratch_shapes=[pltpu.VMEM((2, HG_HEADS, d, d), F32)], sem=("arbitrary",),
        args=(pb, pb, pb, s_fw, do, pb, pb, pb, s_bw, do, lbraw))


def _ret_log_gamma(h, rev):
    hh = RET_HEADS - 1 - h if rev else h
    return math.log(1.0 - 2.0 ** (-5.0 - hh))


def _rope(x, cos, sin):
    half = x.shape[1] // 2
    x1, x2 = x[:, :half], x[:, half:]
    return jnp.concatenate([x1 * cos - x2 * sin, x2 * cos + x1 * sin], axis=1)


def _unrope(dy, cos, sin):
    half = dy.shape[1] // 2
    d1, d2 = dy[:, :half], dy[:, half:]
    return jnp.concatenate([d1 * cos + d2 * sin, d2 * cos - d1 * sin], axis=1)


def _ret_decays(lg, rev):
    c = RET_CHUNK
    r = lax.broadcasted_iota(jnp.int32, (c, c), 0)
    k = lax.broadcasted_iota(jnp.int32, (c, c), 1)
    rel = (k - r) if rev else (r - k)
    dm = jnp.where(rel >= 0, jnp.exp(lg * jnp.maximum(rel, 0).astype(F32)), 0.0)
    pos = lax.broadcasted_iota(jnp.int32, (c, 1), 0).astype(F32)
    if rev:
        qdec = jnp.exp(lg * (c - pos))
        kdec = jnp.exp(lg * pos)
    else:
        qdec = jnp.exp(lg * (pos + 1.0))
        kdec = jnp.exp(lg * (c - 1.0 - pos))
    return dm, qdec, kdec


def _ret_fwd(q, k, v, cos, sin, lc, name, ex=None):
    T = q.shape[0]
    c, dk, dv = RET_CHUNK, RET_DK, RET_DV
    nt, nc = T // c, lc // c
    kscale = dk ** -0.5

    def body(qf, kf, vf, cf, sf_, qb, kb, vb, cb, sb_, of_ref, ob_ref, stf_ref, stb_ref, st_ref):
        @pl.when(pl.program_id(0) == 0)
        def _():
            st_ref[...] = jnp.zeros_like(st_ref)

        sets = ((qf, kf, vf, cf, sf_, of_ref, stf_ref), (qb, kb, vb, cb, sb_, ob_ref, stb_ref))
        combos = [(dr, h) for dr in range(2) for h in range(RET_HEADS)]
        prep = {}
        for dr, (q_ref, k_ref, v_ref, c_ref, s_ref, _, _) in enumerate(sets):
            rev = dr == 1
            cos_v, sin_v = c_ref[...], s_ref[...]
            for h in range(RET_HEADS):
                lg = _ret_log_gamma(h, rev)
                dm, qdec, kdec = _ret_decays(lg, rev)
                qh = _rope(q_ref[:, h * dk:(h + 1) * dk].astype(F32), cos_v, sin_v)
                kh = _rope(k_ref[:, h * dk:(h + 1) * dk].astype(F32), cos_v, sin_v) * kscale
                prep[dr, h] = dict(qb=_bf(qh), kb=_bf(kh), qin=_bf(qh * qdec), kin=_bf(kh * kdec),
                                   v=_bf(v_ref[:, h * dv:(h + 1) * dv]), dm=dm, decay=math.exp(lg * c))
        sc = {ch: _bf(_nt(prep[ch]["qb"], prep[ch]["kb"]) * prep[ch]["dm"]) for ch in combos}
        for dr, h in combos:
            p = prep[dr, h]
            o_ref, so_ref = sets[dr][5], sets[dr][6]
            st = st_ref[dr, h]
            stb = _bf(st)
            so_ref[0, h] = stb
            o_ref[:, h * dv:(h + 1) * dv] = _bf(_nn(sc[dr, h], p["v"]) + _nt(p["qin"], stb))
            st_ref[dr, h] = st * p["decay"] + _tn(p["v"], p["kin"])

    def spec(order, width):
        return pl.BlockSpec((c, width), lambda s: (order(s, nc, nt), 0))

    def st_spec(order):
        return pl.BlockSpec((1, RET_HEADS, dv, dk), lambda s: (order(s, nc, nt), 0, 0, 0))

    in_specs = []
    for order in (_fw_chunk, _bw_chunk):
        in_specs += [spec(order, RET_HEADS * dk), spec(order, RET_HEADS * dk), spec(order, RET_HEADS * dv),
                     spec(order, dk // 2), spec(order, dk // 2)]
    return _host_call(
        body, ex, lambda: pl.program_id(0) == 0, lambda: pl.program_id(0) == nt - 1,
        name=name, grid=(nt,), in_specs=in_specs,
        out_specs=[spec(_fw_chunk, RET_HEADS * dv), spec(_bw_chunk, RET_HEADS * dv), st_spec(_fw_chunk), st_spec(_bw_chunk)],
        out_shape=[jax.ShapeDtypeStruct((T, RET_HEADS * dv), ACT), jax.ShapeDtypeStruct((T, RET_HEADS * dv), ACT),
                   jax.ShapeDtypeStruct((nt, RET_HEADS, dv, dk), ACT), jax.ShapeDtypeStruct((nt, RET_HEADS, dv, dk), ACT)],
        scratch_shapes=[pltpu.VMEM((2, RET_HEADS, dv, dk), F32)], sem=("arbitrary",),
        args=(q, k, v, cos, sin, q, k, v, cos, sin))


def _ret_bwd(q, k, v, cos, sin, s_fw, s_bw, do, lc, name, ex=None):
    T = q.shape[0]
    c, dk, dv = RET_CHUNK, RET_DK, RET_DV
    nt, nc = T // c, lc // c
    kscale = dk ** -0.5

    def rfw(s, nc_, nt_):
        return _fw_chunk(nt_ - 1 - s, nc_, nt_)

    def rbw(s, nc_, nt_):
        return _bw_chunk(nt_ - 1 - s, nc_, nt_)

    def body(qf, kf, vf, cf, sf_, stf, dof, qb, kb, vb, cb, sb_, stb_, dob_,
             dqf, dkf, dvf, dqb, dkb, dvb, dst_ref):
        @pl.when(pl.program_id(0) == 0)
        def _():
            dst_ref[...] = jnp.zeros_like(dst_ref)

        sets = ((qf, kf, vf, cf, sf_, stf, dof, dqf, dkf, dvf), (qb, kb, vb, cb, sb_, stb_, dob_, dqb, dkb, dvb))
        combos = [(dr, h) for dr in range(2) for h in range(RET_HEADS)]
        prep = {}
        for dr, (q_ref, k_ref, v_ref, c_ref, s_ref, _, do_ref, _, _, _) in enumerate(sets):
            rev = dr == 1
            cos_v, sin_v = c_ref[...], s_ref[...]
            for h in range(RET_HEADS):
                lg = _ret_log_gamma(h, rev)
                dm, qdec, kdec = _ret_decays(lg, rev)
                qh = _rope(q_ref[:, h * dk:(h + 1) * dk].astype(F32), cos_v, sin_v)
                kh = _rope(k_ref[:, h * dk:(h + 1) * dk].astype(F32), cos_v, sin_v) * kscale
                prep[dr, h] = dict(qb=_bf(qh), kb=_bf(kh), qin=_bf(qh * qdec), kin=_bf(kh * kdec),
                                   v=_bf(v_ref[:, h * dv:(h + 1) * dv]), dob=_bf(do_ref[:, h * dv:(h + 1) * dv]),
                                   dm=dm, qdec=qdec, kdec=kdec, decay=math.exp(lg * c), cos=cos_v, sin=sin_v)
        sc = {ch: _bf(_nt(prep[ch]["qb"], prep[ch]["kb"]) * prep[ch]["dm"]) for ch in combos}
        dsc = {ch: _bf(_nt(prep[ch]["dob"], prep[ch]["v"]) * prep[ch]["dm"]) for ch in combos}
        carried = {}
        for dr, h in combos:
            p = prep[dr, h]
            dv_ref = sets[dr][9]
            dst = dst_ref[dr, h]
            dstb = _bf(dst)
            carried[dr, h] = dstb
            dv_ref[:, h * dv:(h + 1) * dv] = _bf(_tn(sc[dr, h], p["dob"]) + _nt(p["kin"], dstb))
            dst_ref[dr, h] = _tn(p["dob"], p["qin"]) + dst * p["decay"]
        for dr, h in combos:
            p = prep[dr, h]
            st_in, dq_ref, dk_ref = sets[dr][5], sets[dr][7], sets[dr][8]
            dq_r = _nn(dsc[dr, h], p["kb"]) + _nn(p["dob"], st_in[0, h]) * p["qdec"]
            dk_r = _tn(dsc[dr, h], p["qb"]) + _nn(p["v"], carried[dr, h]) * p["kdec"]
            dq_ref[:, h * dk:(h + 1) * dk] = _bf(_unrope(dq_r, p["cos"], p["sin"]))
            dk_ref[:, h * dk:(h + 1) * dk] = _bf(_unrope(dk_r * kscale, p["cos"], p["sin"]))

    def spec(order, width):
        return pl.BlockSpec((c, width), lambda s: (order(s, nc, nt), 0))

    def st_spec(order):
        return pl.BlockSpec((1, RET_HEADS, dv, dk), lambda s: (order(s, nc, nt), 0, 0, 0))

    in_specs = []
    for order in (rfw, rbw):
        in_specs += [spec(order, RET_HEADS * dk), spec(order, RET_HEADS * dk), spec(order, RET_HEADS * dv),
                     spec(order, dk // 2), spec(order, dk // 2), st_spec(order), spec(order, RET_HEADS * dv)]
    out_specs, out_shape = [], []
    for order in (rfw, rbw):
        out_specs += [spec(order, RET_HEADS * dk), spec(order, RET_HEADS * dk), spec(order, RET_HEADS * dv)]
        out_shape += [jax.ShapeDtypeStruct((T, RET_HEADS * dk), ACT), jax.ShapeDtypeStruct((T, RET_HEADS * dk), ACT),
                      jax.ShapeDtypeStruct((T, RET_HEADS * dv), ACT)]
    return _host_call(
        body, ex, lambda: pl.program_id(0) == 0, lambda: pl.program_id(0) == nt - 1,
        name=name, grid=(nt,), in_specs=in_specs, out_specs=out_specs, out_shape=out_shape,
        scratch_shapes=[pltpu.VMEM((2, RET_HEADS, dv, dk), F32)], sem=("arbitrary",),
        args=(q, k, v, cos, sin, s_fw, do, q, k, v, cos, sin, s_bw, do))


def _trig_rows(lc, ang):
    ang = ang.astype(np.float64)
    half = ang.shape[1]
    cos = np.concatenate([np.ones((lc, half)), np.cos(ang)], axis=0).astype(np.float32)
    sin = np.concatenate([np.zeros((lc, half)), np.sin(ang)], axis=0).astype(np.float32)
    return cos, sin


def _attn_rope_tables(lc, l):
    t = np.arange(l)
    row = (t // GRID_W).astype(np.float32)
    colp = (t % GRID_W).astype(np.float32)
    n_freq = HEAD_DIM // 4
    inv = np.float32(10000.0) ** (-np.arange(n_freq, dtype=np.float32) / np.float32(n_freq))
    ang = np.concatenate([row[:, None] * inv, colp[:, None] * inv], axis=-1)
    cos, sin = _trig_rows(lc, ang)
    return jnp.asarray(np.concatenate([cos, cos], axis=1)), jnp.asarray(np.concatenate([-sin, sin], axis=1))


def _ret_rope_tables(lc, l):
    theta = np.float32(1.0) / (np.float32(10000.0) ** np.linspace(0.0, 1.0, RET_DK // 2, dtype=np.float32))
    ang = np.arange(l, dtype=np.float32)[:, None] * theta
    cos, sin = _trig_rows(lc, ang)
    return jnp.asarray(cos), jnp.asarray(sin)


def _heads_major(slab, n_heads):
    t = slab.shape[0]
    return slab.reshape(t, n_heads, HEAD_DIM).transpose(1, 0, 2)


def _slab(hm):
    nh, t, hd = hm.shape
    return hm.transpose(1, 0, 2).reshape(t, nh * hd)


COL_SHARDED = ("ffn_in0", "ffn_in1", "even_in", "even_in_a", "even_in_b", "odd_in")


def _full_weight(name, g):
    if name in COL_SHARDED:
        return g.transpose(1, 0, 2).reshape(g.shape[1], -1)
    return g.reshape(-1, g.shape[2])


def _shard_slots(name, g):
    if name in COL_SHARDED:
        return g.reshape(g.shape[0], N_DEV, -1).transpose(1, 0, 2)
    return g.reshape(N_DEV, -1, g.shape[1])


def _local_step(xs, target, mv, norm_g, w, qk_g, sink, hg_out_g, lbraw, lc, shards=None):
    _, T, dm = _stream(xs)
    l = T - lc
    tm = lc
    blk = ATTN_BLOCK
    d2, d3 = 2 * dm, 3 * dm
    w = dict(w)
    gw, recv = {}, {}

    def ms(layer, a, b):
        return mv[layer, :, :, a:b]

    def gather(names):
        return None if shards is None else _Exchange(GATHER2, [shards[n] for n in names])

    def arrived(names, got):
        for n, g in zip(names, got):
            w[n] = _full_weight(n, g)

    def scatter(names):
        return None if shards is None else _Exchange(SCATTER, [_shard_slots(n, gw[n]) for n in names])

    def scattered(names, got):
        for n, g in zip(names, got):
            recv[n] = g

    g00, g01, g10, g11 = (norm_g[i, j][None, :] for i in (0, 1) for j in (0, 1))

    cos2, sin2 = _attn_rope_tables(lc, l)
    cosp, sinp = jnp.concatenate([cos2, cos2], axis=1), jnp.concatenate([sin2, sin2], axis=1)
    gains5 = jnp.concatenate([jnp.broadcast_to(jnp.tile(qk_g[0], 2), (N_PAIRS - 1, PAIR)), jnp.tile(qk_g[1], 2)[None]])[:, None, :]
    riding = ["even_out"]
    (pa, pb, qt, ks, vs), got = _pre_fwd(xs, g00, ms(0, 0, d2), w["even_in"], ((0, 768), (768, 3328)), tm, "pre0_fwd",
                                         gather(riding), qk=(gains5, cosp, sinp))
    arrived(riding, got)
    sinkb = jnp.broadcast_to(sink.reshape(ATTN_KV, 4, 1, 1), (ATTN_KV, 4, blk, 1)).reshape(ATTN_KV, 4 * blk, 1)
    riding = ["ffn_in0"]
    (a_slab, lse), got = _attn_slab_fwd(qt, ks, vs, sinkb, lc, "attn_fwd", gather(riding))
    arrived(riding, got)
    riding = ["ffn_out0", "odd_out"]
    (hg_of, hg_ob, hg_sf, hg_sb), got = _gla_fwd(pb, lbraw, lc, "hgrn_fwd", gather(riding))
    arrived(riding, got)
    x01, z0, yp0 = _post_fwd(xs, hg_of, hg_ob, pb, 4, hg_out_g, a_slab, w["even_out"], ms(0, d2, d3), HG_D, tm, "post0_fwd")
    riding = ["odd_in"]
    (x02, u0, f0), got = _ffn_fwd(x01, g01, ms(0, d3, 6 * dm), w["ffn_in0"], w["ffn_out0"], tm, "ffn0_fwd", ex=gather(riding))
    arrived(riding, got)

    riding = ["ffn_out1"]
    (rq, rk, rv, rg), got = _pre_fwd(x02, g10, ms(1, 0, d2), w["odd_in"],
                                     ((0, 1024), (1024, 2048), (2048, 4096), (4096, 6144)), tm, "pre1_fwd", gather(riding),
                                     out_dtype=ACT)
    arrived(riding, got)
    rcos, rsin = _ret_rope_tables(lc, l)
    riding = ["ffn_in1"]
    (rt_of, rt_ob, rt_sf, rt_sb), got = _ret_fwd(rq, rk, rv, rcos, rsin, lc, "ret_fwd", gather(riding))
    arrived(riding, got)
    x11, z1, yp1 = _post_fwd(x02, rt_of, rt_ob, rg, 0, None, None, w["odd_out"], ms(1, d2, d3), RET_DV, tm, "post1_fwd")
    (dx, u1, f1, loss), _ = _ffn_fwd(x11, g11, ms(1, d3, 6 * dm), w["ffn_in1"], w["ffn_out1"], tm, "ffn1_fwd", target)

    (dx, h, du, act, df, dms_f1, dg11), _ = _ffn_bwd(x11, dx, u1, f1, g11, ms(1, d3, 6 * dm), w["ffn_in1"], w["ffn_out1"], tm,
                                                     "ffn1_bwd")
    gw["ffn_in1"] = _wgrad(h, du, "wg_ffn_in1")
    gw["ffn_out1"] = _wgrad(act, df, "wg_ffn_out1")
    do1, dgr1, dy1, z1_t, dgate_p1, _ = _post_bwd(dx, z1, yp1, rt_of, rt_ob, rg, 0, None, w["odd_out"], ms(1, d2, d3), 0, RET_DV, tm,
                                                  "post1_bwd")
    gw["odd_out"] = _wgrad(z1_t, dy1, "wg_odd_out")
    riding = ["ffn_in1"]
    (dqf, dkf, dvf, dqb, dkb, dvb), got = _ret_bwd(rq, rk, rv, rcos, rsin, rt_sf, rt_sb, do1, lc, "ret_bwd", scatter(riding))
    scattered(riding, got)
    riding = ["odd_out", "ffn_out1"]
    (dx, h, dp, dms_p1, dg10), got = _pre_bwd(x02, dx, g10, ms(1, 0, d2), w["odd_in"],
                                              [(0, [dqf, dqb]), (1024, [dkf, dkb]), (2048, [dvf, dvb]), (4096, [dgr1])], tm,
                                              "pre1_bwd", ex=scatter(riding))
    scattered(riding, got)
    gw["odd_in"] = _wgrad(h, dp, "wg_odd_in")

    riding = ["odd_in"]
    (dx, h, du, act, df, dms_f0, dg01), got = _ffn_bwd(x01, dx, u0, f0, g01, ms(0, d3, 6 * dm), w["ffn_in0"], w["ffn_out0"], tm,
                                                       "ffn0_bwd", scatter(riding))
    scattered(riding, got)
    gw["ffn_in0"] = _wgrad(h, du, "wg_ffn_in0")
    gw["ffn_out0"] = _wgrad(act, df, "wg_ffn_out0")
    do0, dgr0, da0, dy0, z0_t, dgate_p0, d_hg_gain = _post_bwd(dx, z0, yp0, hg_of, hg_ob, pb, 4, hg_out_g, w["even_out"],
                                                              ms(0, d2, d3), 512, HG_D, tm, "post0_bwd")
    gw["even_out"] = _wgrad(z0_t, dy0, "wg_even_out")
    riding = ["ffn_in0", "ffn_out0"]
    (hq_f, hz_f, hv_f, hq_b, hz_b, hv_b, dlb), got = _gla_bwd(pb, lbraw, hg_sf, hg_sb, do0, lc, "hgrn_bwd", scatter(riding))
    scattered(riding, got)
    riding = ["even_out"]
    (dq_att, dk_att, dv_att, dsink), got = _attn_slab_bwd(qt, ks, vs, sinkb, a_slab, lse, da0, lc, "attn_bwd", scatter(riding))
    scattered(riding, got)
    pieces0 = [(640, [dv_att]), (768, [hq_f, hq_b]), (1280, [hz_f]), (1792, [hz_b]), (2304, [hv_f, hv_b]), (2816, [dgr0])]
    (dx, h, dp, dms_p0, dg00, dgain5), _ = _pre_bwd(xs, dx, g00, ms(0, 0, d2), w["even_in"], pieces0, tm, "pre0_bwd",
                                                    latent_dx=shards is not None,
                                                    qk=(dq_att, dk_att, pa, gains5, cosp, sinp))
    if shards is None:
        gw["even_in"] = _wgrad(h, dp, "wg_even_in")
    else:
        half = dm // 2
        gw["even_in_a"] = _wgrad(h, dp, "wg_even_in_a", rows=(0, half))
        gw["even_in_b"], got = _wgrad(h, dp, "wg_even_in_b", rows=(half, half), ex=scatter(["even_in_a"]))
        scattered(["even_in_a"], got)

    dmv = jnp.stack([jnp.concatenate([dms_p0, dgate_p0, dms_f0], axis=2), jnp.concatenate([dms_p1, dgate_p1, dms_f1], axis=2)])
    small = {
        "dmv": dmv,
        "norm_g": jnp.stack([jnp.stack([dg00[0], dg01[0]]), jnp.stack([dg10[0], dg11[0]])]),
        "qk_g": jnp.stack([jnp.sum(dgain5[:N_PAIRS - 1, 0].reshape(-1, HEAD_DIM), axis=0),
                           jnp.sum(dgain5[N_PAIRS - 1, 0].reshape(-1, HEAD_DIM), axis=0)]),
        "sink": dsink.reshape(ATTN_HEADS),
        "hg_out_g": d_hg_gain[0],
        "lb": dlb[0],
        "loss": loss[0, 0],
    }
    if shards is not None:
        gw = {n: recv.get(n, g) for n, g in gw.items()}
    return loss, dx, gw, small


HBM_SPEC = pl.BlockSpec(memory_space=pltpu.HBM)


def _my_index():
    return 4 * lax.axis_index("x") + 2 * lax.axis_index("y") + lax.axis_index("c")


def _peer(k):
    pos = []
    for axis, bit in (("x", 4), ("y", 2), ("c", 1)):
        a = lax.axis_index(axis)
        pos.append(1 - a if k & bit else a)
    return tuple(pos)


def _peer_index(k):
    px, py, pc = _peer(k)
    return 4 * px + 2 * py + pc


GATHER, SCATTER = "gather", "scatter"
GATHER2 = "gather over ICI once per chip"
SIBLING = 1
OTHER_CHIPS = (2, 4, 6)


class _Exchange:
    def __init__(self, mode, arrays):
        self.mode, self.arrays, self.n = mode, list(arrays), len(arrays)

    def out_shape(self):
        if self.mode in (GATHER, GATHER2):
            return [jax.ShapeDtypeStruct((N_DEV,) + a.shape, a.dtype) for a in self.arrays]
        return [jax.ShapeDtypeStruct(a.shape, a.dtype) for a in self.arrays]

    def specs(self):
        return [HBM_SPEC] * self.n

    def scratch(self):
        return [pltpu.SemaphoreType.DMA((self.n, N_DEV - 1)), pltpu.SemaphoreType.DMA((self.n, N_DEV - 1)),
                pltpu.SemaphoreType.DMA((self.n,))]

    def _copies(self, in_refs, out_refs, send_sems, recv_sems, local_sems, landing):
        me = _my_index()
        local, remote = [], []
        for a, (src, dst) in enumerate(zip(in_refs, out_refs)):
            part = (lambda j, s=src: s) if self.mode == GATHER else (lambda j, s=src: s.at[j])
            local.append(pltpu.make_async_copy(part(me), dst.at[me], local_sems.at[a]))
            for k in range(1, N_DEV):
                pj = _peer_index(k)
                remote.append(pltpu.make_async_remote_copy(
                    src_ref=part(pj), dst_ref=dst.at[pj if landing else me], send_sem=send_sems.at[a, k - 1],
                    recv_sem=recv_sems.at[a, k - 1], device_id=_peer(k), device_id_type=MESH))
        return local, remote

    def _copy2(self, a, src, dst, sems, slot, relation, to):
        send_sems, recv_sems, _ = sems
        return pltpu.make_async_remote_copy(src_ref=src, dst_ref=dst.at[slot], send_sem=send_sems.at[a, relation - 1],
                                            recv_sem=recv_sems.at[a, relation - 1], device_id=_peer(to), device_id_type=MESH)

    def start(self, in_refs, out_refs, sems):
        if self.mode == GATHER2:
            me = _my_index()
            for a, (src, dst) in enumerate(zip(in_refs, out_refs)):
                pltpu.make_async_copy(src, dst.at[me], sems[2].at[a]).start()
                for k in (SIBLING,) + OTHER_CHIPS:
                    self._copy2(a, src, dst, sems, me, k, k).start()
            return
        local, remote = self._copies(in_refs, out_refs, *sems, landing=False)
        for cp in local + remote:
            cp.start()

    def forward(self, in_refs, out_refs, sems):
        for a, (src, dst) in enumerate(zip(in_refs, out_refs)):
            for r in OTHER_CHIPS:
                pj = _peer_index(r)
                self._copy2(a, src, dst, sems, pj, r, r).wait_recv()
                self._copy2(a, dst.at[pj], dst, sems, pj, r ^ SIBLING, SIBLING).start()

    def wait(self, in_refs, out_refs, sems):
        if self.mode == GATHER2:
            me = _my_index()
            for a, (src, dst) in enumerate(zip(in_refs, out_refs)):
                for k in (SIBLING,) + OTHER_CHIPS:
                    self._copy2(a, src, dst, sems, me, k, k).wait_send()
                self._copy2(a, src, dst, sems, _peer_index(SIBLING), SIBLING, SIBLING).wait_recv()
                for r in OTHER_CHIPS:
                    passed = self._copy2(a, src, dst, sems, _peer_index(r ^ SIBLING), r ^ SIBLING, SIBLING)
                    passed.wait_send()
                    passed.wait_recv()
                pltpu.make_async_copy(src, dst.at[me], sems[2].at[a]).wait()
            return
        local, remote = self._copies(in_refs, out_refs, *sems, landing=True)
        for cp in remote:
            cp.wait_send()
            cp.wait_recv()
        for cp in local:
            cp.wait()

    def ride(self, refs, n_in, n_out, first, mid, last):
        refs = list(refs)
        n = self.n
        x_in = refs[n_in:n_in + n]
        x_out = refs[n_in + n + n_out:n_in + 2 * n + n_out]
        sems = refs[n_in + 2 * n + n_out:n_in + 2 * n + n_out + 3]

        @pl.when(first)
        def _():
            self.start(x_in, x_out, sems)

        if self.mode == GATHER2:
            @pl.when(mid)
            def _():
                self.forward(x_in, x_out, sems)

        @pl.when(last)
        def _():
            self.wait(x_in, x_out, sems)

        return refs[:n_in] + refs[n_in + n:n_in + n + n_out] + refs[n_in + 2 * n + n_out + 3:]

    def call(self, name):
        n = self.n

        def body(*refs):
            ins, outs, sems = refs[:n], refs[n:2 * n], refs[2 * n:]
            self.start(ins, outs, sems)
            if self.mode == GATHER2:
                self.forward(ins, outs, sems)
            self.wait(ins, outs, sems)

        return pl.pallas_call(body, name=name, in_specs=self.specs(), out_specs=self.specs(), out_shape=self.out_shape(),
                              scratch_shapes=self.scratch())(*self.arrays)


def _all_gather(v, name):
    return _Exchange(GATHER, [v]).call(name)[0]


def _hosted(kernel_body, ex, n_in, n_out, first, last, grid):
    if ex is None:
        return kernel_body

    def body(*refs):
        mid = pl.program_id(0) == (2 * grid[0]) // 3 if len(grid) == 1 else None
        kernel_body(*ex.ride(refs, n_in, n_out, first(), mid, last()))

    return body


def _host_call(kernel_body, ex, first, last, name, grid, in_specs, out_specs, out_shape, scratch_shapes, sem, args):
    n_in, n_out = len(in_specs), len(out_specs)
    if ex is None:
        outs = pl.pallas_call(kernel_body, name=name, grid=grid, in_specs=in_specs, out_specs=out_specs, out_shape=out_shape,
                              scratch_shapes=scratch_shapes, compiler_params=_cp(*sem))(*args)
        return list(outs), []
    outs = pl.pallas_call(
        _hosted(kernel_body, ex, n_in, n_out, first, last, grid), name=name, grid=grid,
        in_specs=list(in_specs) + ex.specs(), out_specs=list(out_specs) + ex.specs(),
        out_shape=list(out_shape) + ex.out_shape(), scratch_shapes=ex.scratch() + list(scratch_shapes),
        compiler_params=_cp(*sem))(*args, *ex.arrays)
    return list(outs[:n_out]), list(outs[n_out:])


def _mod_fwd(call, mod_w, bias, name):
    nl, dm, n = mod_w.shape

    def body(c_ref, w_ref, b_ref, o_ref):
        cv = c_ref[...]
        cond = _bf(cv * _sig(cv))
        for layer in range(nl):
            o_ref[layer] = _nn(cond, _bf(w_ref[layer])) + b_ref[layer]

    return pl.pallas_call(
        body, name=name, out_shape=jax.ShapeDtypeStruct((nl, call.shape[0], n), F32),
        compiler_params=pltpu.CompilerParams(vmem_limit_bytes=VMEM_LIMIT),
    )(call, mod_w, bias)


def _mod_bwd(call, dm_all, mod_w, name):
    nl, dm, n = mod_w.shape

    def body(c_ref, d_ref, w_ref, gw_ref, dc_ref):
        cv = c_ref[...]
        cond = _bf(cv * _sig(cv))
        dc = jnp.zeros(cv.shape, F32)
        for layer in range(nl):
            db = _bf(d_ref[layer])
            gw_ref[layer] = _tn(cond, db)
            dc = dc + _nt(db, _bf(w_ref[layer]))
        dc_ref[...] = dc

    return pl.pallas_call(
        body, name=name,
        out_shape=[jax.ShapeDtypeStruct(mod_w.shape, F32), jax.ShapeDtypeStruct(call.shape, F32)],
        compiler_params=pltpu.CompilerParams(vmem_limit_bytes=VMEM_LIMIT),
    )(call, dm_all, mod_w)


def _sum_parts(g, name):
    def body(g_ref, o_ref):
        acc = g_ref[0]
        for j in range(1, g.shape[0]):
            acc = acc + g_ref[j]
        o_ref[...] = acc

    return pl.pallas_call(body, name=name, out_shape=jax.ShapeDtypeStruct(g.shape[1:], g.dtype))(g)


def _small_finish(dcond_g, c_ctx, dlb, lbraw, dm_ctx, dm_lat, name):
    def body(dc_ref, c_ref, dlb_ref, lb_ref, mc_ref, ml_ref, gc_ref, glb_ref, gb_ref):
        acc = dc_ref[0, 0:1, :]
        for j in range(1, N_DEV):
            acc = acc + dc_ref[j, 0:1, :]
        cv = c_ref[...]
        s = _sig(cv)
        gc_ref[...] = acc * (s * (1.0 + cv * (1.0 - s)))
        lb = _lower_bound(lb_ref)
        d0 = dlb_ref[...] * lb * (1.0 - lb)
        glb_ref[0:1, :] = d0
        glb_ref[1:2, :] = -d0
        gb_ref[...] = mc_ref[...] + ml_ref[...]

    return pl.pallas_call(
        body, name=name,
        out_shape=[jax.ShapeDtypeStruct(c_ctx.shape, F32), jax.ShapeDtypeStruct(lbraw.shape, F32),
                   jax.ShapeDtypeStruct(dm_ctx.shape, F32)],
    )(dcond_g, c_ctx, dlb, lbraw, dm_ctx, dm_lat)


def _row_tile(r, cap, mult):
    best = r
    for t in range(mult, min(r, cap) + 1, mult):
        if r % t == 0:
            best = t
    return best


def _adam(g_list, w, m, v, name, ex=None):
    nl, r, cdim = w.shape
    p = g_list[0].shape[0]
    tr = _row_tile(r, 128, 16)
    ni = r // tr

    def body(*refs):
        g_refs = refs[:nl]
        w_ref, m_ref, v_ref, go_ref, d_ref, mo_ref, vo_ref = refs[nl:]
        layer = pl.program_id(0)

        def total(g_ref):
            acc = g_ref[0].astype(F32)
            for j in range(1, p):
                acc = acc + g_ref[j].astype(F32)
            return acc

        g = total(g_refs[0])
        for k in range(1, nl):
            g = jnp.where(layer == k, total(g_refs[k]), g)
        m2 = ADAM_B1 * m_ref[0] + (1.0 - ADAM_B1) * g
        v2 = ADAM_B2 * v_ref[0] + (1.0 - ADAM_B2) * (g * g)
        m_hat = m2 / (1.0 - ADAM_B1 ** ADAM_STEP)
        v_hat = v2 / (1.0 - ADAM_B2 ** ADAM_STEP)
        go_ref[0] = g
        d_ref[0] = -ADAM_LR * (m_hat / (jnp.sqrt(v_hat) + ADAM_EPS) + ADAM_WD * w_ref[0])
        mo_ref[0] = m2
        vo_ref[0] = v2

    def g_spec(k):
        return pl.BlockSpec((p, tr, cdim), lambda la, i: (0, jnp.where(la == k, i, jnp.where(la < k, 0, ni - 1)), 0))

    spec = pl.BlockSpec((1, tr, cdim), lambda la, i: (la, i, 0))
    return _host_call(
        body, ex, lambda: (pl.program_id(0) == 0) & (pl.program_id(1) == 0),
        lambda: (pl.program_id(0) == nl - 1) & (pl.program_id(1) == ni - 1),
        name=name, grid=(nl, ni),
        in_specs=[g_spec(k) for k in range(nl)] + [spec, spec, spec],
        out_specs=[spec] * 4, out_shape=[jax.ShapeDtypeStruct((nl, r, cdim), F32)] * 4,
        scratch_shapes=[], sem=("arbitrary", "arbitrary"), args=(*g_list, w, m, v))


def _f32_as_rows(a, width):
    return lax.bitcast_convert_type(a.reshape(-1), BF16).reshape(-1, width)


def _rows_as_f32(rows):
    return lax.bitcast_convert_type(rows.reshape(rows.shape[:-2] + (-1, 2)), F32)


def _pad_rows(a, mult):
    r = (-a.shape[-2]) % mult
    if r == 0:
        return a
    widths = [(0, 0)] * (a.ndim - 2) + [(0, r), (0, 0)]
    return jnp.pad(a, widths)


def _pack_flat(parts, lane):
    flat = jnp.concatenate([p.reshape(-1).astype(F32) for p in parts])
    n = flat.shape[0]
    rows = -(-n // lane)
    rows += (-rows) % 8
    return jnp.pad(flat, (0, rows * lane - n)).reshape(rows, lane)


def _unpack_flat(packed, shapes):
    flat = packed.reshape(-1)
    out, off = [], 0
    for s in shapes:
        n = math.prod(s)
        out.append(flat[off:off + n].reshape(s))
        off += n
    return out


def kernel(x, c, ctx, c_ctx, mod_w, mod_b, norm_g, ffn_w_in, ffn_w_out, even_w_in, even_w_out, attn_qk_norm_g, attn_sink, hgrn_out_norm_g, hgrn_lb, odd_w_in, odd_w_out, loss_target, m_c_ctx, m_mod_w, m_mod_b, m_norm_g, m_ffn_w_in, m_ffn_w_out, m_even_w_in, m_even_w_out, m_attn_qk_norm_g, m_attn_sink, m_hgrn_out_norm_g, m_hgrn_lb, m_odd_w_in, m_odd_w_out, v_c_ctx, v_mod_w, v_mod_b, v_norm_g, v_ffn_w_in, v_ffn_w_out, v_even_w_in, v_even_w_out, v_attn_qk_norm_g, v_attn_sink, v_hgrn_out_norm_g, v_hgrn_lb, v_odd_w_in, v_odd_w_out):
    me = _my_index()
    lc, dm = ctx.shape[1], x.shape[2]
    nmod = mod_w.shape[2]
    big = (ffn_w_in, ffn_w_out, even_w_in, even_w_out, odd_w_in, odd_w_out)

    extra = _pad_rows(jnp.concatenate([_f32_as_rows(c, dm), _f32_as_rows(norm_g, dm)], axis=0), 16)
    shards = {"ffn_in0": ffn_w_in[0], "ffn_in1": ffn_w_in[1], "ffn_out0": ffn_w_out[0], "ffn_out1": ffn_w_out[1],
              "even_in": even_w_in[0], "even_out": even_w_out[0], "odd_in": odd_w_in[0], "odd_out": odd_w_out[0]}
    shards = {n: a.astype(BF16) for n, a in shards.items()}
    first = _Exchange(GATHER2, [shards["even_in"], extra]).call("gather_first")
    w = {"even_in": _full_weight("even_in", first[0])}
    c_all = _rows_as_f32(first[1][:, 0:2])
    norm_g_all = _rows_as_f32(first[1][:, 2:3]).reshape(N_DEV, 2, 2, -1)
    norm_g_full = norm_g_all.transpose(1, 2, 0, 3).reshape(2, 2, dm)

    call = jnp.concatenate([c_all, c_ctx[None, :], jnp.zeros((16 - N_DEV - 1, dm), F32)], axis=0)
    bias = lax.dynamic_slice_in_dim(mod_b, me * nmod, nmod, axis=1)[:, None, :]
    m_sh = _mod_fwd(call, mod_w, bias, "mod_fwd")
    m_g = _all_gather(m_sh.reshape(-1, nmod), "gather_mod").reshape(N_DEV, 2, 16, nmod)
    m_all = m_g.transpose(1, 2, 0, 3).reshape(2, 16, -1)
    m_lat = lax.dynamic_index_in_dim(m_all, me, axis=1, keepdims=False)
    mv = jnp.stack([m_all[:, N_DEV], m_lat], axis=1)[:, :, None, :]

    _, dxs, gw, small = _local_step((ctx[0], x[0]), loss_target[0], mv, norm_g_full, w, attn_qk_norm_g[0], attn_sink[0],
                                    hgrn_out_norm_g, hgrn_lb, lc, shards)
    grad_x = dxs[None]

    last = _Exchange(SCATTER, [_shard_slots("even_in_b", gw["even_in_b"])])
    big_g = [[gw["ffn_in0"], gw["ffn_in1"]], [gw["ffn_out0"], gw["ffn_out1"]], None, [gw["even_out"]],
             [gw["odd_in"]], [gw["odd_out"]]]
    halves = (2, even_w_in.shape[1] // 2, even_w_in.shape[2])
    big_w = (ffn_w_in, ffn_w_out, even_w_in.reshape(halves), even_w_out, odd_w_in, odd_w_out)
    big_m = (m_ffn_w_in, m_ffn_w_out, m_even_w_in.reshape(halves), m_even_w_out, m_odd_w_in, m_odd_w_out)
    big_v = (v_ffn_w_in, v_ffn_w_out, v_even_w_in.reshape(halves), v_even_w_out, v_odd_w_in, v_odd_w_out)
    big_names = ("ffn_w_in", "ffn_w_out", "even_w_in", "even_w_out", "odd_w_in", "odd_w_out")
    big_out = [None] * 6

    def adam_big(i, ex=None):
        big_out[i], got = _adam(big_g[i], big_w[i], big_m[i], big_v[i], "adam_" + big_names[i], ex)
        return got

    dmv = small["dmv"]
    small_shapes = [(2, 6 * dm), (2, 6 * dm), (2, 2, dm), (2, HEAD_DIM), (ATTN_HEADS,), (HG_D,), (HG_HEADS * HG_D,), (1,)]
    vec = _pack_flat([dmv[:, 0, 0], dmv[:, 1, 0], small["norm_g"], small["qk_g"], small["sink"], small["hg_out_g"],
                      small["lb"], small["loss"]], 128)
    big_g[2] = [gw["even_in_a"], adam_big(0, last)[0]]
    vec_g = adam_big(1, _Exchange(GATHER, [vec]))[0]
    tot = _unpack_flat(_sum_parts(vec_g, "sum_small"), small_shapes)
    dm_ctx_tot, dm_lat_tot, g_norm_full, g_qk, g_sink, g_hg, dlb_tot, loss_tot = tot
    dm_lat_each = vec_g.reshape(N_DEV, -1)[:, 12 * dm:24 * dm].reshape(N_DEV, 2, 6 * dm)
    dm_lat_mine = lax.dynamic_slice_in_dim(dm_lat_each, me * nmod, nmod, axis=2).transpose(1, 0, 2)
    dm_ctx_mine = lax.dynamic_slice_in_dim(dm_ctx_tot, me * nmod, nmod, axis=1)[:, None, :]
    dm_all = jnp.concatenate([dm_lat_mine, dm_ctx_mine, jnp.zeros((2, 16 - N_DEV - 1, nmod), F32)], axis=1)
    g_mod_w, dcond = _mod_bwd(call, dm_all, mod_w, "mod_bwd")
    dcond_g = adam_big(4, _Exchange(GATHER, [dcond[N_DEV:]]))[0]
    g_c_ctx, g_lb, g_mod_b = _small_finish(dcond_g, c_ctx[None, :], dlb_tot[None, :], hgrn_lb, dm_ctx_tot, dm_lat_tot,
                                           "small_finish")
    g_norm = lax.dynamic_slice_in_dim(g_norm_full, me * norm_g.shape[2], norm_g.shape[2], axis=2)
    for i in (3, 5, 2):
        adam_big(i)
    big_out[2] = [o.reshape(even_w_in.shape) for o in big_out[2]]
    big_res = [[big_out[i][k] for i in range(6)] for k in range(4)]

    mod_res, _ = _adam([g_mod_w[0][None], g_mod_w[1][None]], mod_w, m_mod_w, v_mod_w, "adam_mod_w")

    sm_w = (c_ctx, mod_b, norm_g, attn_qk_norm_g, attn_sink, hgrn_out_norm_g, hgrn_lb)
    sm_m = (m_c_ctx, m_mod_b, m_norm_g, m_attn_qk_norm_g, m_attn_sink, m_hgrn_out_norm_g, m_hgrn_lb)
    sm_v = (v_c_ctx, v_mod_b, v_norm_g, v_attn_qk_norm_g, v_attn_sink, v_hgrn_out_norm_g, v_hgrn_lb)
    sm_g = (g_c_ctx, g_mod_b, g_norm, g_qk, g_sink, g_hg, g_lb)
    sm_shapes = [a.shape for a in sm_w]
    sm_out, _ = _adam([_pack_flat(sm_g, 128)[None]], _pack_flat(sm_w, 128)[None], _pack_flat(sm_m, 128)[None],
                      _pack_flat(sm_v, 128)[None], "adam_small")
    sm_res = [_unpack_flat(o, sm_shapes) for o in sm_out]

    def ordered(k):
        s, b = sm_res[k], big_res[k]
        return [s[0], mod_res[k], s[1], s[2], b[0], b[1], b[2], b[3], s[3], s[4], s[5], s[6], b[4], b[5]]

    return (loss_tot[0], grad_x, *ordered(0), *ordered(1), *ordered(2), *ordered(3))
```

```python
import functools
import math

import jax
import jax.numpy as jnp
import numpy as np
from jax import lax
from jax.experimental import pallas as pl
from jax.experimental.pallas import tpu as pltpu

F32 = jnp.float32
BF16 = jnp.bfloat16
EPS = 1e-6
N_DEV = 8
MESH = pl.DeviceIdType.MESH

HEAD_DIM = 64
ATTN_HEADS = 8
ATTN_KV = 2
ATTN_BLOCK = 128
WINDOW = 128
GRID_W = 64
HG_HEADS = 4
HG_D = 128
HG_CHUNK = 64
HG_STEP_CHUNKS = 4
RET_HEADS = 4
RET_DK = 256
RET_DV = 512
RET_CHUNK = 256
NEG = -1e30

ADAM_LR = 0.001
ADAM_B1 = 0.9
ADAM_B2 = 0.999
ADAM_EPS = 1e-08
ADAM_WD = 0.01
ADAM_STEP = 10

VMEM_LIMIT = 60 * 1024 * 1024


def _cp(*sem):
    return pltpu.CompilerParams(dimension_semantics=sem, vmem_limit_bytes=VMEM_LIMIT)


def _nn(a, b):
    return jnp.dot(a, b, preferred_element_type=F32)


def _nt(a, b):
    return lax.dot_general(a, b, (((1,), (1,)), ((), ())), preferred_element_type=F32)


def _tn(a, b):
    return lax.dot_general(a, b, (((0,), (0,)), ((), ())), preferred_element_type=F32)


ACT = BF16


def _bf(a):
    return a.astype(ACT)


def _sig(x):
    return jax.nn.sigmoid(x)


def _split3(x):
    h = x.astype(BF16)
    r = x - h.astype(F32)
    m = r.astype(BF16)
    lo = (r - m.astype(F32)).astype(BF16)
    return h, m, lo


def _nn3(m01, x):
    h, m, lo = _split3(x)
    return _nn(m01, h) + _nn(m01, m) + _nn(m01, lo)


def _nn3r(x, m01):
    h, m, lo = _split3(x)
    return _nn(h, m01) + _nn(m, m01) + _nn(lo, m01)


def _full(shape):
    nd = len(shape)
    return pl.BlockSpec(shape, lambda *a: (0,) * nd, pipeline_mode=pl.Buffered(1))


def _whole(shape):
    nd = len(shape)
    return pl.BlockSpec(shape, lambda *a: (0,) * nd)


def _rows(tm, width):
    return pl.BlockSpec((tm, width), lambda i: (i, 0))


def _cols(height, tm):
    return pl.BlockSpec((height, tm), lambda i: (0, i))


def _ctx_lat(width):
    return pl.BlockSpec((1, 1, width), lambda i: (jnp.minimum(i, 1), 0, 0))


def _acc_ctx_lat(ref, i, val):
    @pl.when(i <= 1)
    def _():
        ref[...] = val.reshape(ref.shape)

    @pl.when(i > 1)
    def _():
        ref[...] += val.reshape(ref.shape)


def _acc_all(ref, i, val):
    @pl.when(i == 0)
    def _():
        ref[...] = val.reshape(ref.shape)

    @pl.when(i > 0)
    def _():
        ref[...] += val.reshape(ref.shape)


def _tile(n, cap):
    best = None
    for t in range(128, min(n, cap) + 1, 128):
        if n % t == 0:
            best = t
    return n if best is None else best


def _norm_mod(xv, g, shift, scale):
    r = lax.rsqrt(jnp.mean(xv * xv, axis=-1, keepdims=True) + EPS)
    xhat = xv * r
    n = xhat * g
    return r, xhat, n, n * (1.0 + scale) + shift


def _norm_mod_bwd(dh, r, xhat, n, g, scale):
    dshift = jnp.sum(dh, axis=0, keepdims=True)
    dscale = jnp.sum(dh * n, axis=0, keepdims=True)
    dn = dh * (1.0 + scale)
    dg = jnp.sum(dn * xhat, axis=0, keepdims=True)
    dxh = dn * g
    dx = r * (dxh - xhat * jnp.mean(dxh * xhat, axis=-1, keepdims=True))
    return dx, dshift, dscale, dg


def _stream(x):
    if isinstance(x, tuple):
        return list(x), x[0].shape[0] + x[1].shape[0], x[0].shape[1]
    return [x], x.shape[0], x.shape[1]


def _stream_specs(x, tm, dm):
    if isinstance(x, tuple):
        return [pl.BlockSpec((tm, dm), lambda i: (0, 0)), pl.BlockSpec((tm, dm), lambda i: (jnp.maximum(i - 1, 0), 0))]
    return [_rows(tm, dm)]


def _stream_tile(refs):
    if len(refs) == 2:
        return jnp.where(pl.program_id(0) == 0, refs[0][...], refs[1][...])
    return refs[0][...]


def _pre_fwd(x, gain, ms, w, splits, tm, name, ex=None, out_dtype=F32, qk=None):
    xs, T, dm = _stream(x)
    nx = len(xs)
    nt = T // tm
    nq = 0 if qk is None else 3
    ns = len(splits)

    def body(*refs):
        g_ref, ms_ref, w_ref = refs[nx:nx + 3]
        outs = refs[nx + 3 + nq:]
        ms_v = ms_ref[0]
        h = _norm_mod(_stream_tile(refs[:nx]), g_ref[...], ms_v[:, :dm], ms_v[:, dm:])[3]
        hb = _bf(h)
        for k, ((s, e), o_ref) in enumerate(zip(splits, outs[:ns])):
            part = _nn(hb, w_ref[:, s:e])
            o_ref[...] = part.astype(o_ref.dtype)
            if k == 0 and qk is not None:
                gq_ref, c_ref, s_ref = refs[nx + 3:nx + 6]
                _qk_tile_fwd(part, gq_ref, c_ref[...], s_ref[...], *outs[ns:])

    in_specs = _stream_specs(x, tm, dm) + [_full((1, dm)), _ctx_lat(2 * dm), _full(w.shape)]
    out_specs = [_rows(tm, e - s) for s, e in splits]
    out_shape = [jax.ShapeDtypeStruct((T, e - s), out_dtype) for s, e in splits]
    args = [*xs, gain, ms, w]
    if qk is not None:
        qw = ATTN_HEADS * HEAD_DIM
        in_specs += [_full(qk[0].shape), _rows(tm, PAIR), _rows(tm, PAIR)]
        args += list(qk)
        out_specs += [_rows(tm, qw), _rows(tm, PAIR), _rows(tm, PAIR)]
        out_shape += [jax.ShapeDtypeStruct((T, qw), ACT), jax.ShapeDtypeStruct((T, PAIR), ACT), jax.ShapeDtypeStruct((T, PAIR), ACT)]
    return _host_call(
        body, ex, lambda: pl.program_id(0) == 0, lambda: pl.program_id(0) == nt - 1,
        name=name, grid=(nt,), in_specs=in_specs, out_specs=out_specs, out_shape=out_shape,
        scratch_shapes=[], sem=("arbitrary",), args=tuple(args))


def _pre_bwd(x, dx_in, gain, ms, w, pieces, tm, name, latent_dx=False, ex=None, qk=None):
    xs, T, dm = _stream(x)
    nx = len(xs)
    dx_spec = pl.BlockSpec((tm, dm), lambda i: (jnp.maximum(i - 1, 0), 0)) if latent_dx else _rows(tm, dm)
    dx_rows = T - tm if latent_dx else T
    n_out = w.shape[1]
    flat = [a for _, arrs in pieces for a in arrs]
    nq = 0 if qk is None else 6
    qkw = (ATTN_HEADS + ATTN_KV) * HEAD_DIM

    def body(*refs):
        dxin_ref, g_ref, ms_ref, w_ref = refs[nx:nx + 4]
        rest = refs[nx + 4:]
        p_refs = rest[:len(flat)]
        qk_refs = rest[len(flat):len(flat) + nq]
        dx_ref, h_ref, dp_ref, dms_ref, dg_ref = rest[len(flat) + nq:len(flat) + nq + 5]
        i = pl.program_id(0)
        ms_v = ms_ref[0]
        g = g_ref[...]
        scale = ms_v[:, dm:]
        r, xhat, n, h = _norm_mod(_stream_tile(refs[:nx]), g, ms_v[:, :dm], scale)
        h_ref[...] = _bf(h).T
        dh = jnp.zeros((tm, dm), F32)
        if qk is not None:
            dq_ref, dk_ref, pa_ref, gq_ref, c_ref, s_ref = qk_refs
            dqk, dgs = _qk_tile_bwd(dq_ref, dk_ref, pa_ref, gq_ref, c_ref[...], s_ref[...])
            dgq_ref = rest[len(flat) + nq + 5]
            for p, dgp in enumerate(dgs):
                _acc_all(dgq_ref.at[p], i, dgp)
            vb = _bf(dqk)
            dp_ref[:, :qkw] = vb
            dh = dh + _nt(vb, w_ref[:, :qkw])
        k = 0
        for s, arrs in pieces:
            v = p_refs[k][...].astype(F32)
            for j in range(1, len(arrs)):
                v = v + p_refs[k + j][...].astype(F32)
            k += len(arrs)
            vb = _bf(v)
            wd = vb.shape[1]
            dp_ref[:, s:s + wd] = vb
            dh = dh + _nt(vb, w_ref[:, s:s + wd])
        dx, dshift, dscale, dg = _norm_mod_bwd(dh, r, xhat, n, g, scale)
        dx_ref[...] = dxin_ref[...] + dx
        _acc_ctx_lat(dms_ref, i, jnp.concatenate([dshift, dscale], axis=1))
        _acc_all(dg_ref, i, dg)

    nt = T // tm
    in_specs = (_stream_specs(x, tm, dm) + [_rows(tm, dm), _full((1, dm)), _ctx_lat(2 * dm), _full(w.shape)]
                + [_rows(tm, a.shape[1]) for a in flat])
    out_specs = [dx_spec, _cols(dm, tm), _rows(tm, n_out), _ctx_lat(2 * dm), _whole((1, dm))]
    out_shape = [jax.ShapeDtypeStruct((dx_rows, dm), F32), jax.ShapeDtypeStruct((dm, T), ACT),
                 jax.ShapeDtypeStruct((T, n_out), ACT), jax.ShapeDtypeStruct((2, 1, 2 * dm), F32),
                 jax.ShapeDtypeStruct((1, dm), F32)]
    args = [*xs, dx_in, gain, ms, w, *flat]
    if qk is not None:
        dq, dk, pa, gains, cosp, sinp = qk
        in_specs += [_rows(tm, dq.shape[1]), _rows(tm, PAIR), _rows(tm, qkw), _full(gains.shape), _rows(tm, PAIR), _rows(tm, PAIR)]
        args += [dq, dk, pa, gains, cosp, sinp]
        out_specs.append(_whole(gains.shape))
        out_shape.append(jax.ShapeDtypeStruct(gains.shape, F32))
    return _host_call(
        body, ex, lambda: pl.program_id(0) == 0, lambda: pl.program_id(0) == nt - 1,
        name=name, grid=(nt,), in_specs=in_specs, out_specs=out_specs, out_shape=out_shape,
        scratch_shapes=[], sem=("arbitrary",), args=tuple(args))


def _ffn_fwd(x1, gain, ms, w_in, w_out, tm, name, target=None, ex=None):
    T, dm = x1.shape
    fh = w_out.shape[0]
    head = target is not None

    def body(*refs):
        if head:
            x_ref, g_ref, ms_ref, wi_ref, wo_ref, t_ref, x2_ref, u_ref, f_ref, loss_ref = refs
        else:
            x_ref, g_ref, ms_ref, wi_ref, wo_ref, x2_ref, u_ref, f_ref = refs
        ms_v = ms_ref[0]
        xv = x_ref[...]
        h = _norm_mod(xv, g_ref[...], ms_v[:, :dm], ms_v[:, dm:2 * dm])[3]
        u = _nn(_bf(h), wi_ref[...])
        u_ref[...] = _bf(u)
        gt = u[:, :fh]
        act = gt * _sig(gt) * u[:, fh:]
        f = _nn(_bf(act), wo_ref[...])
        f_ref[...] = _bf(f)
        x2 = xv + ms_v[:, 2 * dm:] * f
        if head:
            i = pl.program_id(0)
            e = x2 - t_ref[...]
            x2_ref[...] = jnp.where(i > 0, e * (1.0 / dm), 0.0)
            _acc_all(loss_ref, i, jnp.where(i > 0, jnp.sum(e * e) * (0.5 / dm), 0.0))
        else:
            x2_ref[...] = x2

    ins = [x1, gain, ms, w_in, w_out]
    in_specs = [_rows(tm, dm), _full((1, dm)), _ctx_lat(3 * dm), _full(w_in.shape), _full(w_out.shape)]
    out_specs = [_rows(tm, dm), _rows(tm, 2 * fh), _rows(tm, dm)]
    out_shape = [jax.ShapeDtypeStruct((T, dm), F32), jax.ShapeDtypeStruct((T, 2 * fh), ACT), jax.ShapeDtypeStruct((T, dm), ACT)]
    if head:
        ins.append(target)
        in_specs.append(pl.BlockSpec((tm, dm), lambda i: (jnp.maximum(i - 1, 0), 0)))
        out_specs.append(_whole((1, 1)))
        out_shape.append(jax.ShapeDtypeStruct((1, 1), F32))
    nt = T // tm
    return _host_call(
        body, ex, lambda: pl.program_id(0) == 0, lambda: pl.program_id(0) == nt - 1,
        name=name, grid=(nt,), in_specs=in_specs, out_specs=out_specs, out_shape=out_shape,
        scratch_shapes=[], sem=("arbitrary",), args=tuple(ins))


def _ffn_bwd(x1, dx2, u, f, gain, ms, w_in, w_out, tm, name, ex=None):
    T, dm = x1.shape
    fh = w_out.shape[0]

    def body(x_ref, dx2_ref, u_ref, f_ref, g_ref, ms_ref, wi_ref, wo_ref,
             dx1_ref, h_ref, du_ref, act_ref, df_ref, dms_ref, dg_ref):
        i = pl.program_id(0)
        ms_v = ms_ref[0]
        g = g_ref[...]
        scale = ms_v[:, dm:2 * dm]
        gate = ms_v[:, 2 * dm:]
        r, xhat, n, h = _norm_mod(x_ref[...], g, ms_v[:, :dm], scale)
        h_ref[...] = _bf(h).T
        dx2 = dx2_ref[...]
        dgate = jnp.sum(dx2 * f_ref[...].astype(F32), axis=0, keepdims=True)
        dfb = _bf(dx2 * gate)
        df_ref[...] = dfb
        da = _nt(dfb, wo_ref[...])
        uv = u_ref[...].astype(F32)
        gt = uv[:, :fh]
        up = uv[:, fh:]
        s = _sig(gt)
        sg = gt * s
        act_ref[...] = _bf(sg * up).T
        dgt = _bf(da * up * (s * (1.0 + gt * (1.0 - s))))
        dup = _bf(da * sg)
        du_ref[:, :fh] = dgt
        du_ref[:, fh:] = dup
        dh = _nt(dgt, wi_ref[:, :fh]) + _nt(dup, wi_ref[:, fh:])
        dx, dshift, dscale, dg = _norm_mod_bwd(dh, r, xhat, n, g, scale)
        dx1_ref[...] = dx2 + dx
        _acc_ctx_lat(dms_ref, i, jnp.concatenate([dshift, dscale, dgate], axis=1))
        _acc_all(dg_ref, i, dg)

    nt = T // tm
    return _host_call(
        body, ex, lambda: pl.program_id(0) == 0, lambda: pl.program_id(0) == nt - 1,
        name=name, grid=(nt,),
        in_specs=[_rows(tm, dm), _rows(tm, dm), _rows(tm, 2 * fh), _rows(tm, dm), _full((1, dm)), _ctx_lat(3 * dm),
                  _full(w_in.shape), _full(w_out.shape)],
        out_specs=[_rows(tm, dm), _cols(dm, tm), _rows(tm, 2 * fh), _cols(fh, tm), _rows(tm, dm),
                   _ctx_lat(3 * dm), _whole((1, dm))],
        out_shape=[jax.ShapeDtypeStruct((T, dm), F32), jax.ShapeDtypeStruct((dm, T), ACT),
                   jax.ShapeDtypeStruct((T, 2 * fh), ACT), jax.ShapeDtypeStruct((fh, T), ACT),
                   jax.ShapeDtypeStruct((T, dm), ACT), jax.ShapeDtypeStruct((2, 1, 3 * dm), F32),
                   jax.ShapeDtypeStruct((1, dm), F32)],
        scratch_shapes=[], sem=("arbitrary",), args=(x1, dx2, u, f, gain, ms, w_in, w_out))


def _wgrad(a_t, b, name, rows=None, ex=None):
    T = a_t.shape[1]
    r0, K = (0, a_t.shape[0]) if rows is None else rows
    N = b.shape[1]
    tk, tn, tt = _tile(K, 1408), _tile(N, 1664), _tile(T, 2816)
    nt = T // tt
    assert r0 % tk == 0
    off = r0 // tk
    nk, nn = K // tk, N // tn

    def body(a_ref, b_ref, o_ref, acc_ref):
        t = pl.program_id(2)
        part = _nn(a_ref[...], b_ref[...])

        @pl.when(t == 0)
        def _():
            acc_ref[...] = part

        @pl.when(t > 0)
        def _():
            acc_ref[...] += part

        @pl.when(t == nt - 1)
        def _():
            o_ref[...] = acc_ref[...].astype(o_ref.dtype)

    def at(i, j, t):
        return (pl.program_id(0) == i) & (pl.program_id(1) == j) & (pl.program_id(2) == t)

    outs, got = _host_call(
        body, ex, lambda: at(0, 0, 0), lambda: at(nk - 1, nn - 1, nt - 1),
        name=name, grid=(nk, nn, nt),
        in_specs=[pl.BlockSpec((tk, tt), lambda i, j, t: (i + off, t)), pl.BlockSpec((tt, tn), lambda i, j, t: (t, j))],
        out_specs=[pl.BlockSpec((tk, tn), lambda i, j, t: (i, j))],
        out_shape=[jax.ShapeDtypeStruct((K, N), ACT)],
        scratch_shapes=[pltpu.VMEM((tk, tn), F32)], sem=("arbitrary", "arbitrary", "arbitrary"), args=(a_t, b))
    return outs[0] if ex is None else (outs[0], got)


def _post_fwd(x, o_fw, o_bw, g_src, g_blk, gain, a, w_out, ms, dvh, tm, name):
    xs, T, dm = _stream(x)
    nx = len(xs)
    hv = o_fw.shape[1]
    aw = 0 if a is None else a.shape[1]
    has_gain = gain is not None

    def body(*refs):
        refs = list(refs)
        x_refs = refs[:nx]
        of_ref, ob_ref, g_ref = refs[nx:nx + 3]
        k = nx + 3
        gain_ref = a_ref = None
        if has_gain:
            gain_ref = refs[k]
            k += 1
        if aw:
            a_ref = refs[k]
            k += 1
        w_ref, ms_ref, x1_ref, z_ref, yp_ref = refs[k:k + 5]
        o = of_ref[...].astype(F32) + ob_ref[...].astype(F32)
        gr = g_ref[...].astype(F32)
        if aw:
            z_ref[:, :aw] = _bf(a_ref[...])
        for hd in range(hv // dvh):
            sl = slice(hd * dvh, (hd + 1) * dvh)
            oh = o[:, sl]
            gh = gr[:, sl]
            r = lax.rsqrt(jnp.mean(oh * oh, axis=-1, keepdims=True) + EPS)
            y = oh * r
            if has_gain:
                y = y * gain_ref[...]
            y = y * (gh * _sig(gh))
            z_ref[:, aw + hd * dvh:aw + (hd + 1) * dvh] = _bf(y)
        yp = _nn(z_ref[...], w_ref[...])
        yp_ref[...] = _bf(yp)
        x1_ref[...] = _stream_tile(x_refs) + ms_ref[0] * yp

    ins = xs + [o_fw, o_bw, g_src]
    specs = _stream_specs(x, tm, dm) + [_rows(tm, hv), _rows(tm, hv), pl.BlockSpec((tm, hv), lambda i: (i, g_blk))]
    if has_gain:
        ins.append(gain)
        specs.append(_full(gain.shape))
    if aw:
        ins.append(a)
        specs.append(_rows(tm, aw))
    ins += [w_out, ms]
    specs += [_full(w_out.shape), _ctx_lat(dm)]
    return pl.pallas_call(
        body, name=name, grid=(T // tm,), in_specs=specs,
        out_specs=[_rows(tm, dm), _rows(tm, aw + hv), _rows(tm, dm)],
        out_shape=[jax.ShapeDtypeStruct((T, dm), F32), jax.ShapeDtypeStruct((T, aw + hv), ACT),
                   jax.ShapeDtypeStruct((T, dm), ACT)],
        compiler_params=_cp("arbitrary"),
    )(*ins)


def _post_bwd(dx1, z, yp, o_fw, o_bw, g_src, g_blk, gain, w_out, ms, aw, dvh, tm, name):
    T, dm = dx1.shape
    hv = o_fw.shape[1]
    has_gain = gain is not None

    def body(*refs):
        refs = list(refs)
        dx1_ref, z_ref, yp_ref, of_ref, ob_ref, g_ref = refs[:6]
        k = 6
        gain_ref = None
        if has_gain:
            gain_ref = refs[k]
            k += 1
        w_ref, ms_ref = refs[k:k + 2]
        k += 2
        do_ref, dgr_ref = refs[k:k + 2]
        k += 2
        da_ref = None
        if aw:
            da_ref = refs[k]
            k += 1
        dy_ref, zt_ref, dgate_ref, dgain_ref = refs[k:k + 4]
        i = pl.program_id(0)
        dx1v = dx1_ref[...]
        zt_ref[...] = z_ref[...].T
        _acc_ctx_lat(dgate_ref, i, jnp.sum(dx1v * yp_ref[...].astype(F32), axis=0, keepdims=True))
        dyb = _bf(dx1v * ms_ref[0])
        dy_ref[...] = dyb
        dz = _nt(dyb, w_ref[...])
        if aw:
            da_ref[...] = dz[:, :aw]
        o = of_ref[...].astype(F32) + ob_ref[...].astype(F32)
        gr = g_ref[...].astype(F32)
        dgain = jnp.zeros((1, dvh), F32)
        for hd in range(hv // dvh):
            sl = slice(hd * dvh, (hd + 1) * dvh)
            oh = o[:, sl]
            gh = gr[:, sl]
            dyh = dz[:, aw + hd * dvh:aw + (hd + 1) * dvh]
            r = lax.rsqrt(jnp.mean(oh * oh, axis=-1, keepdims=True) + EPS)
            n = oh * r
            s = _sig(gh)
            sl_g = gh * s
            gn = gain_ref[...] if has_gain else 1.0
            dgr_ref[:, sl] = _bf(dyh * n * gn * (s * (1.0 + gh * (1.0 - s))))
            dn = dyh * gn * sl_g
            dgain = dgain + jnp.sum(dyh * n * sl_g, axis=0, keepdims=True)
            do_ref[:, sl] = _bf(r * (dn - n * jnp.mean(dn * n, axis=-1, keepdims=True)))
        _acc_all(dgain_ref, i, dgain)

    ins = [dx1, z, yp, o_fw, o_bw, g_src]
    specs = [_rows(tm, dm), _rows(tm, aw + hv), _rows(tm, dm), _rows(tm, hv), _rows(tm, hv),
             pl.BlockSpec((tm, hv), lambda i: (i, g_blk))]
    if has_gain:
        ins.append(gain)
        specs.append(_full(gain.shape))
    ins += [w_out, ms]
    specs += [_full(w_out.shape), _ctx_lat(dm)]
    out_specs = [_rows(tm, hv), _rows(tm, hv)]
    out_shape = [jax.ShapeDtypeStruct((T, hv), ACT), jax.ShapeDtypeStruct((T, hv), ACT)]
    if aw:
        out_specs.append(_rows(tm, aw))
        out_shape.append(jax.ShapeDtypeStruct((T, aw), F32))
    out_specs += [_rows(tm, dm), _cols(aw + hv, tm), _ctx_lat(dm), _whole((1, dvh))]
    out_shape += [jax.ShapeDtypeStruct((T, dm), ACT), jax.ShapeDtypeStruct((aw + hv, T), ACT),
                  jax.ShapeDtypeStruct((2, 1, dm), F32), jax.ShapeDtypeStruct((1, dvh), F32)]
    return pl.pallas_call(
        body, name=name, grid=(T // tm,), in_specs=specs, out_specs=out_specs, out_shape=out_shape,
        compiler_params=_cp("arbitrary"),
    )(*ins)


def _loss_bwd(x, target, tm, name):
    T, dm = x.shape

    def body(x_ref, t_ref, dx_ref, loss_ref):
        i = pl.program_id(0)

        @pl.when(i == 0)
        def _():
            dx_ref[...] = jnp.zeros_like(dx_ref)
            loss_ref[...] = jnp.zeros_like(loss_ref)

        @pl.when(i > 0)
        def _():
            e = x_ref[...] - t_ref[...]
            dx_ref[...] = e * (1.0 / dm)
            loss_ref[...] += jnp.sum(e * e) * (0.5 / dm)

    return pl.pallas_call(
        body, name=name, grid=(T // tm,),
        in_specs=[_rows(tm, dm), pl.BlockSpec((tm, dm), lambda i: (jnp.maximum(i - 1, 0), 0))],
        out_specs=[_rows(tm, dm), _whole((1, 1))],
        out_shape=[jax.ShapeDtypeStruct((T, dm), F32), jax.ShapeDtypeStruct((1, 1), F32)],
        compiler_params=_cp("arbitrary"),
    )(x, target)


def _swap_matrix():
    r = lax.broadcasted_iota(jnp.int32, (HEAD_DIM, HEAD_DIM), 0)
    c = lax.broadcasted_iota(jnp.int32, (HEAD_DIM, HEAD_DIM), 1)
    return jnp.where((r + HEAD_DIM // 2) % HEAD_DIM == c, 1.0, 0.0).astype(BF16)


def _qk_prep_fwd(raw, gains, cos2, sin2, tq, name):
    nh, T, hd = raw.shape

    def body(x_ref, g_ref, c_ref, s_ref, o_ref):
        hidx = pl.program_id(0)
        xv = x_ref[0]
        r = lax.rsqrt(jnp.mean(xv * xv, axis=-1, keepdims=True) + EPS)
        n = xv * r * g_ref[0]
        y = n * c_ref[...] + _nn3r(n, _swap_matrix()) * s_ref[...]
        sc = jnp.where(hidx < ATTN_HEADS, HEAD_DIM ** -0.5, 1.0)
        o_ref[0] = _bf(y * sc)

    return pl.pallas_call(
        body, name=name, grid=(nh, T // tq),
        in_specs=[pl.BlockSpec((1, tq, hd), lambda h, i: (h, i, 0)), pl.BlockSpec((1, 1, hd), lambda h, i: (h, 0, 0)),
                  pl.BlockSpec((tq, hd), lambda h, i: (i, 0)), pl.BlockSpec((tq, hd), lambda h, i: (i, 0))],
        out_specs=pl.BlockSpec((1, tq, hd), lambda h, i: (h, i, 0)),
        out_shape=jax.ShapeDtypeStruct((nh, T, hd), ACT),
        compiler_params=_cp("arbitrary", "arbitrary"),
    )(raw, gains, cos2, sin2)


def _qk_prep_bwd(dy, raw, gains, cos2, sin2, tq, name):
    nh, T, hd = raw.shape

    def body(dy_ref, x_ref, g_ref, c_ref, s_ref, dx_ref, dg_ref):
        hidx = pl.program_id(0)
        i = pl.program_id(1)
        xv = x_ref[0]
        g = g_ref[0]
        r = lax.rsqrt(jnp.mean(xv * xv, axis=-1, keepdims=True) + EPS)
        xhat = xv * r
        sc = jnp.where(hidx < ATTN_HEADS, HEAD_DIM ** -0.5, 1.0)
        dyv = dy_ref[0] * sc
        dn = dyv * c_ref[...] + _nn3r(dyv * s_ref[...], _swap_matrix())
        _acc_all(dg_ref, i, jnp.sum(dn * xhat, axis=0, keepdims=True))
        dxh = dn * g
        dx_ref[0] = r * (dxh - xhat * jnp.mean(dxh * xhat, axis=-1, keepdims=True))

    return pl.pallas_call(
        body, name=name, grid=(nh, T // tq),
        in_specs=[pl.BlockSpec((1, tq, hd), lambda h, i: (h, i, 0)), pl.BlockSpec((1, tq, hd), lambda h, i: (h, i, 0)),
                  pl.BlockSpec((1, 1, hd), lambda h, i: (h, 0, 0)),
                  pl.BlockSpec((tq, hd), lambda h, i: (i, 0)), pl.BlockSpec((tq, hd), lambda h, i: (i, 0))],
        out_specs=[pl.BlockSpec((1, tq, hd), lambda h, i: (h, i, 0)), pl.BlockSpec((1, 1, hd), lambda h, i: (h, 0, 0))],
        out_shape=[jax.ShapeDtypeStruct((nh, T, hd), F32), jax.ShapeDtypeStruct((nh, 1, hd), F32)],
        compiler_params=_cp("arbitrary", "arbitrary"),
    )(dy, raw, gains, cos2, sin2)


def _attn_scores(q, k_ref, i, lc, T, sink):
    blk = ATTN_BLOCK
    kc = k_ref[0, pl.ds(blk, lc), :]
    kw = k_ref[0, pl.ds(pl.multiple_of(i * blk, blk), 3 * blk), :]
    s_c = _nt(q, kc)
    s_w = _nt(q, kw)
    row = lax.broadcasted_iota(jnp.int32, (4 * blk, 1), 0)
    qpos = i * blk + (row & (blk - 1))
    kpos = (i - 1) * blk + lax.broadcasted_iota(jnp.int32, (1, 3 * blk), 1)
    valid = (qpos >= lc) & (kpos >= lc) & (kpos < T) & (jnp.abs(kpos - qpos) <= WINDOW)
    s_w = jnp.where(valid, s_w, NEG)
    return kc, kw, s_c, s_w


def _attn_fwd(qt, kp, vp, sinkb, lc, name, ex=None):
    nh, T, hd = qt.shape
    blk = ATTN_BLOCK
    g = nh // ATTN_KV

    def body(q_ref, k_ref, v_ref, sink_ref, o_ref, lse_ref):
        i = pl.program_id(1)
        q = q_ref[...].reshape(g * blk, hd)
        sink = sink_ref[0]
        kc, kw, s_c, s_w = _attn_scores(q, k_ref, i, lc, T, sink)
        m = jnp.maximum(jnp.maximum(jnp.max(s_c, axis=-1, keepdims=True), jnp.max(s_w, axis=-1, keepdims=True)), sink)
        e_c = jnp.exp(s_c - m)
        e_w = jnp.exp(s_w - m)
        den = jnp.exp(sink - m) + jnp.sum(e_c, axis=-1, keepdims=True) + jnp.sum(e_w, axis=-1, keepdims=True)
        inv = 1.0 / den
        vc = v_ref[0, pl.ds(blk, lc), :]
        vw = v_ref[0, pl.ds(pl.multiple_of(i * blk, blk), 3 * blk), :]
        o = _nn(_bf(e_c * inv), vc) + _nn(_bf(e_w * inv), vw)
        o_ref[...] = o.reshape(g, blk, hd)
        lse_ref[...] = (m + jnp.log(den)).reshape(g, blk, 1)

    nb = T // blk
    return _host_call(
        body, ex, lambda: (pl.program_id(0) == 0) & (pl.program_id(1) == 0),
        lambda: (pl.program_id(0) == ATTN_KV - 1) & (pl.program_id(1) == nb - 1),
        name=name, grid=(ATTN_KV, nb),
        in_specs=[pl.BlockSpec((g, blk, hd), lambda kv, i: (kv, i, 0)),
                  pl.BlockSpec((1, T + 2 * blk, hd), lambda kv, i: (kv, 0, 0)),
                  pl.BlockSpec((1, T + 2 * blk, hd), lambda kv, i: (kv, 0, 0)),
                  pl.BlockSpec((1, g * blk, 1), lambda kv, i: (kv, 0, 0))],
        out_specs=[pl.BlockSpec((g, blk, hd), lambda kv, i: (kv, i, 0)),
                   pl.BlockSpec((g, blk, 1), lambda kv, i: (kv, i, 0))],
        out_shape=[jax.ShapeDtypeStruct((nh, T, hd), F32), jax.ShapeDtypeStruct((nh, T, 1), F32)],
        scratch_shapes=[], sem=("arbitrary", "arbitrary"), args=(qt, kp, vp, sinkb))


def _attn_bwd(qt, kp, vp, sinkb, o, lse, do, lc, name):
    nh, T, hd = qt.shape
    blk = ATTN_BLOCK
    g = nh // ATTN_KV

    def body(q_ref, k_ref, v_ref, sink_ref, o_ref, lse_ref, do_ref, dq_ref, dk_ref, dv_ref, ds_ref):
        i = pl.program_id(1)

        @pl.when(i == 0)
        def _():
            dk_ref[...] = jnp.zeros_like(dk_ref)
            dv_ref[...] = jnp.zeros_like(dv_ref)
            ds_ref[...] = jnp.zeros_like(ds_ref)

        q = q_ref[...].reshape(g * blk, hd)
        sink = sink_ref[0]
        lse = lse_ref[...].reshape(g * blk, 1)
        dov = do_ref[...].reshape(g * blk, hd)
        delta = jnp.sum(dov * o_ref[...].reshape(g * blk, hd), axis=-1, keepdims=True)
        kc, kw, s_c, s_w = _attn_scores(q, k_ref, i, lc, T, sink)
        p_c = jnp.exp(s_c - lse)
        p_w = jnp.exp(s_w - lse)
        win = pl.ds(pl.multiple_of(i * blk, blk), 3 * blk)
        vc = v_ref[0, pl.ds(blk, lc), :]
        vw = v_ref[0, win, :]
        dob = _bf(dov)
        ds_c = _bf(p_c * (_nt(dob, vc) - delta))
        ds_w = _bf(p_w * (_nt(dob, vw) - delta))
        dsr = -jnp.exp(sink - lse) * delta
        for hh in range(g):
            ds_ref[0, hh:hh + 1, :] += jnp.sum(dsr[hh * blk:(hh + 1) * blk, :], axis=0, keepdims=True)
        dq_ref[...] = (_nn(ds_c, kc) + _nn(ds_w, kw)).reshape(g, blk, hd)
        dk_ref[0, pl.ds(blk, lc), :] += _tn(ds_c, q)
        dk_ref[0, win, :] += _tn(ds_w, q)
        dv_ref[0, pl.ds(blk, lc), :] += _tn(_bf(p_c), dob)
        dv_ref[0, win, :] += _tn(_bf(p_w), dob)

    qspec = pl.BlockSpec((g, blk, hd), lambda kv, i: (kv, i, 0))
    kspec = pl.BlockSpec((1, T + 2 * blk, hd), lambda kv, i: (kv, 0, 0))
    lspec = pl.BlockSpec((g, blk, 1), lambda kv, i: (kv, i, 0))
    return pl.pallas_call(
        body, name=name, grid=(ATTN_KV, T // blk),
        in_specs=[qspec, kspec, kspec, pl.BlockSpec((1, g * blk, 1), lambda kv, i: (kv, 0, 0)), qspec, lspec, qspec],
        out_specs=[qspec, kspec, kspec, pl.BlockSpec((1, g, 1), lambda kv, i: (kv, 0, 0))],
        out_shape=[jax.ShapeDtypeStruct((nh, T, hd), F32), jax.ShapeDtypeStruct((ATTN_KV, T + 2 * blk, hd), F32),
                   jax.ShapeDtypeStruct((ATTN_KV, T + 2 * blk, hd), F32), jax.ShapeDtypeStruct((ATTN_KV, g, 1), F32)],
        compiler_params=_cp("arbitrary", "arbitrary"),
    )(qt, kp, vp, sinkb, o, lse, do)


PAIR = 2 * HEAD_DIM
N_PAIRS = (ATTN_HEADS + ATTN_KV) // 2


def _lanes():
    return lax.broadcasted_iota(jnp.int32, (1, PAIR), 1)


def _swap32(v):
    first_half = (_lanes() & (HEAD_DIM // 2)) == 0
    return jnp.where(first_half, pltpu.roll(v, PAIR - HEAD_DIM // 2, 1), pltpu.roll(v, HEAD_DIM // 2, 1))


def _head_mean(v):
    r = lax.broadcasted_iota(jnp.int32, (PAIR, PAIR), 0)
    c = lax.broadcasted_iota(jnp.int32, (PAIR, PAIR), 1)
    same = jnp.where((r >= HEAD_DIM) == (c >= HEAD_DIM), 1.0, 0.0).astype(BF16)
    return _nn3r(v, same) * (1.0 / HEAD_DIM)


def _qk_tile_fwd(pa, g_ref, cosv, sinv, q_ref, k_ref, v_ref):
    qw = ATTN_HEADS * HEAD_DIM
    for p in range(N_PAIRS):
        xv = pa[:, p * PAIR:(p + 1) * PAIR]
        n = xv * lax.rsqrt(_head_mean(xv * xv) + EPS) * g_ref[p]
        y = n * cosv + _swap32(n) * sinv
        if p < N_PAIRS - 1:
            q_ref[:, p * PAIR:(p + 1) * PAIR] = _bf(y * HEAD_DIM ** -0.5)
        else:
            k_ref[...] = _bf(y)
    v_ref[...] = _bf(pa[:, qw + PAIR:])


def _qk_tile_bwd(dq_ref, dk_ref, pa_ref, g_ref, cosv, sinv):
    dxs, dgs = [], []
    for p in range(N_PAIRS):
        sl = slice(p * PAIR, (p + 1) * PAIR)
        xv = pa_ref[:, sl]
        r = lax.rsqrt(_head_mean(xv * xv) + EPS)
        xhat = xv * r
        dy = dq_ref[:, sl] * HEAD_DIM ** -0.5 if p < N_PAIRS - 1 else dk_ref[...]
        dn = dy * cosv + _swap32(dy * sinv)
        dgs.append(jnp.sum(dn * xhat, axis=0, keepdims=True))
        dxh = dn * g_ref[p]
        dxs.append(r * (dxh - xhat * _head_mean(dxh * xhat)))
    return jnp.concatenate(dxs, axis=1), dgs


def _qk_slab_fwd(pa, gains, cosp, sinp, tm, name):
    T = pa.shape[0]
    qw = ATTN_HEADS * HEAD_DIM

    def body(pa_ref, g_ref, c_ref, s_ref, q_ref, k_ref, v_ref):
        cosv, sinv = c_ref[...], s_ref[...]
        for p in range(N_PAIRS):
            xv = pa_ref[:, p * PAIR:(p + 1) * PAIR]
            n = xv * lax.rsqrt(_head_mean(xv * xv) + EPS) * g_ref[p]
            y = n * cosv + _swap32(n) * sinv
            if p < N_PAIRS - 1:
                q_ref[:, p * PAIR:(p + 1) * PAIR] = _bf(y * HEAD_DIM ** -0.5)
            else:
                k_ref[...] = _bf(y)
        v_ref[...] = _bf(pa_ref[:, qw + PAIR:])

    return pl.pallas_call(
        body, name=name, grid=(T // tm,),
        in_specs=[_rows(tm, pa.shape[1]), _full(gains.shape), _rows(tm, PAIR), _rows(tm, PAIR)],
        out_specs=[_rows(tm, qw), _rows(tm, PAIR), _rows(tm, PAIR)],
        out_shape=[jax.ShapeDtypeStruct((T, qw), ACT), jax.ShapeDtypeStruct((T, PAIR), ACT),
                   jax.ShapeDtypeStruct((T, PAIR), ACT)],
        compiler_params=_cp("arbitrary"),
    )(pa, gains, cosp, sinp)


def _qk_slab_bwd(dq, dk, pa, gains, cosp, sinp, tm, name):
    T = pa.shape[0]
    qw = ATTN_HEADS * HEAD_DIM

    def body(dq_ref, dk_ref, pa_ref, g_ref, c_ref, s_ref, dx_ref, dg_ref):
        i = pl.program_id(0)
        cosv, sinv = c_ref[...], s_ref[...]
        for p in range(N_PAIRS):
            sl = slice(p * PAIR, (p + 1) * PAIR)
            xv = pa_ref[:, sl]
            r = lax.rsqrt(_head_mean(xv * xv) + EPS)
            xhat = xv * r
            dy = dq_ref[:, sl] * HEAD_DIM ** -0.5 if p < N_PAIRS - 1 else dk_ref[...]
            dn = dy * cosv + _swap32(dy * sinv)
            _acc_all(dg_ref.at[p], i, jnp.sum(dn * xhat, axis=0, keepdims=True))
            dxh = dn * g_ref[p]
            dx_ref[:, sl] = r * (dxh - xhat * _head_mean(dxh * xhat))

    return pl.pallas_call(
        body, name=name, grid=(T // tm,),
        in_specs=[_rows(tm, qw), _rows(tm, PAIR), _rows(tm, qw + PAIR), _full(gains.shape), _rows(tm, PAIR), _rows(tm, PAIR)],
        out_specs=[_rows(tm, qw + PAIR), _whole(gains.shape)],
        out_shape=[jax.ShapeDtypeStruct((T, qw + PAIR), F32), jax.ShapeDtypeStruct(gains.shape, F32)],
        compiler_params=_cp("arbitrary"),
    )(dq, dk, pa, gains, cosp, sinp)


def _attn_window(ref, i, nb):
    blk = ATTN_BLOCK
    starts = [pl.multiple_of(jnp.clip(i + d, 0, nb - 1) * blk, blk) for d in (-1, 0, 1)]
    return starts, jnp.concatenate([ref[pl.ds(s, blk), :] for s in starts], axis=0)


GROUP_HEADS = 2
ATTN_STEP_BLOCKS = 2


def _head_groups(n):
    g = ATTN_HEADS // ATTN_KV
    return [(kv, [kv * g + s + j for j in range(n)]) for kv in range(ATTN_KV) for s in range(0, g, n)]


def _attn_mask(i, lc, T, rows):
    blk = ATTN_BLOCK
    row = lax.broadcasted_iota(jnp.int32, (rows, 1), 0)
    qpos = i * blk + (row & (blk - 1))
    kpos = (i - 1) * blk + lax.broadcasted_iota(jnp.int32, (1, 3 * blk), 1)
    return (qpos >= lc) & (kpos >= lc) & (kpos < T) & (jnp.abs(kpos - qpos) <= WINDOW)


def _to_kv_half(v, head, kv):
    return v if head % 2 == kv else pltpu.roll(v, HEAD_DIM, 1)


def _attn_slab_fwd(qt, ks, vs, sinkb, lc, name, ex=None):
    T = qt.shape[0]
    blk = ATTN_BLOCK
    nb = T // blk
    g = ATTN_HEADS // ATTN_KV

    spb = ATTN_STEP_BLOCKS
    ng = nb // spb

    def one_block(i, rows, q_ref, k_ref, v_ref, sink_ref, o_ref, lse_ref):
        lane = _lanes()
        valid = _attn_mask(i, lc, T, GROUP_HEADS * blk)
        kc_all, vc = k_ref[0:lc, :], v_ref[0:lc, :]
        _, kw_all = _attn_window(k_ref, i, nb)
        _, vw = _attn_window(v_ref, i, nb)
        kc, kw = [], []
        for kv in range(ATTN_KV):
            mine = (lane >= kv * HEAD_DIM) & (lane < (kv + 1) * HEAD_DIM)
            kc.append(jnp.where(mine, kc_all, jnp.zeros_like(kc_all)))
            kw.append(jnp.where(mine, kw_all, jnp.zeros_like(kw_all)))
        groups = _head_groups(GROUP_HEADS)
        qg = [jnp.concatenate([_to_kv_half(q_ref[rows, (h // 2) * PAIR:(h // 2 + 1) * PAIR], h, kv) for h in heads], axis=0)
              for kv, heads in groups]
        sinks = [sink_ref[kv, (heads[0] - kv * g) * blk:(heads[-1] + 1 - kv * g) * blk] for kv, heads in groups]
        s_c = [_nt(q, kc[kv]) for q, (kv, _) in zip(qg, groups)]
        s_w = [jnp.where(valid, _nt(q, kw[kv]), NEG) for q, (kv, _) in zip(qg, groups)]
        m = [jnp.maximum(jnp.maximum(jnp.max(a, axis=-1, keepdims=True), jnp.max(b, axis=-1, keepdims=True)), s)
             for a, b, s in zip(s_c, s_w, sinks)]
        e_c = [jnp.exp(a - mm) for a, mm in zip(s_c, m)]
        e_w = [jnp.exp(b - mm) for b, mm in zip(s_w, m)]
        den = [jnp.exp(s - mm) + jnp.sum(a, axis=-1, keepdims=True) + jnp.sum(b, axis=-1, keepdims=True)
               for s, mm, a, b in zip(sinks, m, e_c, e_w)]
        inv = [1.0 / d for d in den]
        og = [_nn(_bf(a * r), vc) + _nn(_bf(b * r), vw) for a, b, r in zip(e_c, e_w, inv)]
        placed = [None] * ATTN_HEADS
        for (kv, heads), o2, mm, d in zip(groups, og, m, den):
            lse_ref[heads[0]:heads[-1] + 1, rows, :] = (mm + jnp.log(d)).reshape(len(heads), blk, 1)
            for j, h in enumerate(heads):
                placed[h] = _to_kv_half(o2[j * blk:(j + 1) * blk], h, kv)
        for p in range(ATTN_HEADS // 2):
            o_ref[rows, p * PAIR:(p + 1) * PAIR] = jnp.where(lane < HEAD_DIM, placed[2 * p], placed[2 * p + 1])

    def body(*refs):
        for j in range(spb):
            one_block(pl.program_id(0) * spb + j, pl.ds(j * blk, blk), *refs)

    qw = ATTN_HEADS * HEAD_DIM
    return _host_call(
        body, ex, lambda: pl.program_id(0) == 0, lambda: pl.program_id(0) == ng - 1,
        name=name, grid=(ng,),
        in_specs=[_rows(spb * blk, qw), _full((T, PAIR)), _full((T, PAIR)), _full(sinkb.shape)],
        out_specs=[_rows(spb * blk, qw), pl.BlockSpec((ATTN_HEADS, spb * blk, 1), lambda i: (0, i, 0))],
        out_shape=[jax.ShapeDtypeStruct((T, qw), F32), jax.ShapeDtypeStruct((ATTN_HEADS, T, 1), F32)],
        scratch_shapes=[], sem=("arbitrary",), args=(qt, ks, vs, sinkb))


def _attn_slab_bwd(qt, ks, vs, sinkb, o, lse, do, lc, name, ex=None):
    T = qt.shape[0]
    blk = ATTN_BLOCK
    nb = T // blk
    g = ATTN_HEADS // ATTN_KV

    spb = ATTN_STEP_BLOCKS
    ng = nb // spb

    def body(*refs):
        dk_ref, dv_ref, ds_ref = refs[8:11]

        @pl.when(pl.program_id(0) == 0)
        def _():
            dk_ref[...] = jnp.zeros_like(dk_ref)
            dv_ref[...] = jnp.zeros_like(dv_ref)
            ds_ref[...] = jnp.zeros_like(ds_ref)

        for j in range(spb):
            one_block(pl.program_id(0) * spb + j, pl.ds(j * blk, blk), *refs)

    def one_block(i, rows, q_ref, k_ref, v_ref, sink_ref, o_ref, lse_ref, do_ref, dq_ref, dk_ref, dv_ref, ds_ref):
        lane = _lanes()
        valid = _attn_mask(i, lc, T, g * blk)
        kc_all, vc_all = k_ref[0:lc, :], v_ref[0:lc, :]
        starts, kw_all = _attn_window(k_ref, i, nb)
        _, vw_all = _attn_window(v_ref, i, nb)
        dq_pairs = [jnp.zeros((blk, PAIR), F32) for _ in range(ATTN_HEADS // 2)]
        for kv in range(ATTN_KV):
            mine = (lane >= kv * HEAD_DIM) & (lane < (kv + 1) * HEAD_DIM)

            def only(v):
                return jnp.where(mine, v, jnp.zeros_like(v))

            kc, kw, vc, vw = only(kc_all), only(kw_all), only(vc_all), only(vw_all)
            heads = [kv * g + j for j in range(g)]
            qs, dos, deltas = [], [], []
            for h in heads:
                sl = slice((h // 2) * PAIR, (h // 2 + 1) * PAIR)
                dov = do_ref[rows, sl]
                qs.append(_to_kv_half(q_ref[rows, sl], h, kv))
                dos.append(_bf(_to_kv_half(dov, h, kv)))
                own = (lane < HEAD_DIM) if h % 2 == 0 else (lane >= HEAD_DIM)
                deltas.append(jnp.sum(jnp.where(own, dov * o_ref[rows, sl], 0.0), axis=-1, keepdims=True))
            q4, do4, delta = jnp.concatenate(qs, axis=0), jnp.concatenate(dos, axis=0), jnp.concatenate(deltas, axis=0)
            sink = sink_ref[kv]
            lse = lse_ref[kv * g:(kv + 1) * g, rows, :].reshape(g * blk, 1)
            p_c = jnp.exp(_nt(q4, kc) - lse)
            p_w = jnp.exp(jnp.where(valid, _nt(q4, kw), NEG) - lse)
            ds_c = _bf(p_c * (_nt(do4, vc) - delta))
            ds_w = _bf(p_w * (_nt(do4, vw) - delta))
            dsr = -jnp.exp(sink - lse) * delta
            dq4 = _nn(ds_c, kc) + _nn(ds_w, kw)
            for j, h in enumerate(heads):
                ds_ref[h:h + 1, :] += jnp.sum(dsr[j * blk:(j + 1) * blk, :], axis=0, keepdims=True)
                dq_pairs[h // 2] = dq_pairs[h // 2] + _to_kv_half(dq4[j * blk:(j + 1) * blk], h, kv)
            dk_ref[0:lc, :] += only(_tn(ds_c, q4))
            dv_ref[0:lc, :] += only(_tn(_bf(p_c), do4))
            dkw = only(_tn(ds_w, q4))
            dvw = only(_tn(_bf(p_w), do4))
            for b, s in enumerate(starts):
                dk_ref[pl.ds(s, blk), :] += dkw[b * blk:(b + 1) * blk]
                dv_ref[pl.ds(s, blk), :] += dvw[b * blk:(b + 1) * blk]
        for p in range(ATTN_HEADS // 2):
            dq_ref[rows, p * PAIR:(p + 1) * PAIR] = dq_pairs[p]

    qw = ATTN_HEADS * HEAD_DIM
    lspec = pl.BlockSpec((ATTN_HEADS, spb * blk, 1), lambda i: (0, i, 0))
    return _host_call(
        body, ex, lambda: pl.program_id(0) == 0, lambda: pl.program_id(0) == ng - 1,
        name=name, grid=(ng,),
        in_specs=[_rows(spb * blk, qw), _full((T, PAIR)), _full((T, PAIR)), _full(sinkb.shape), _rows(spb * blk, qw), lspec,
                  _rows(spb * blk, qw)],
        out_specs=[_rows(spb * blk, qw), _whole((T, PAIR)), _whole((T, PAIR)), _whole((ATTN_HEADS, 1))],
        out_shape=[jax.ShapeDtypeStruct((T, qw), F32), jax.ShapeDtypeStruct((T, PAIR), F32),
                   jax.ShapeDtypeStruct((T, PAIR), F32), jax.ShapeDtypeStruct((ATTN_HEADS, 1), F32)],
        scratch_shapes=[], sem=("arbitrary",), args=(qt, ks, vs, sinkb, o, lse, do))


def _fw_chunk(s, nc, nt):
    return s


def _bw_chunk(s, nc, nt):
    return jnp.where(s < nc, nc - 1 - s, nt - 1 - (s - nc))


def _tri(c, rev):
    r = lax.broadcasted_iota(jnp.int32, (c, c), 0)
    k = lax.broadcasted_iota(jnp.int32, (c, c), 1)
    return (k >= r) if rev else (k <= r)


def _gla_gates(z, lb, rev):
    c = HG_CHUNK
    sg = _sig(z)
    f = lb + (1.0 - lb) * sg
    cum = _nn3(jnp.where(_tri(c, rev), 1.0, 0.0).astype(BF16), jnp.log(f))
    mid = c - 1 - c // 2 if rev else c // 2
    last = 0 if rev else c - 1
    return sg, f, cum, cum[mid:mid + 1], cum[last:last + 1], last


def _lower_bound(lbraw_ref):
    lr = lbraw_ref[...]
    return _sig(lr[0:1] - lr[1:2])


def _gla_fwd(pb, lbraw, lc, name, ex=None):
    T = pb.shape[0]
    c, hw, d, ns = HG_CHUNK, HG_HEADS * HG_D, HG_D, HG_STEP_CHUNKS
    nt, nc = T // (ns * c), lc // (ns * c)
    orders = (_fw_chunk, _bw_chunk)

    def body(qf, zf, vf, qb, zb, vb, lb_ref, of_ref, ob_ref, sf_ref, sb_ref, st_ref):
        @pl.when(pl.program_id(0) == 0)
        def _():
            st_ref[...] = jnp.zeros_like(st_ref)

        lb = _lower_bound(lb_ref)
        dirs = ((qf, zf, vf, of_ref, sf_ref), (qb, zb, vb, ob_ref, sb_ref))
        combos = [(dr, h, slice(h * d, (h + 1) * d)) for dr in range(2) for h in range(HG_HEADS)]
        for j in range(ns):
            sub = (j, ns - 1 - j)
            rows = [pl.ds(sub[dr] * c, c) for dr in range(2)]
            prep = []
            for dr, (q_ref, z_ref, v_ref, _, _) in enumerate(dirs):
                rev = dr == 1
                qr = q_ref[rows[dr], :]
                q = qr * _sig(qr)
                _, f, cum, ref, last, _ = _gla_gates(z_ref[rows[dr], :], lb, rev)
                k = 1.0 - f
                prep.append(dict(q1=_bf(q * jnp.exp(cum - ref)), k1=_bf(k * jnp.exp(ref - cum)), q2=_bf(q * jnp.exp(cum)),
                                 k2=_bf(k * jnp.exp(last - cum)), el=jnp.exp(last), v=_bf(v_ref[rows[dr], :]),
                                 mask=_tri(c, rev)))
            a = [_bf(jnp.where(prep[dr]["mask"], _nt(prep[dr]["q1"][:, sl], prep[dr]["k1"][:, sl]), 0.0))
                 for dr, _, sl in combos]
            for (dr, h, sl), a_h in zip(combos, a):
                p = prep[dr]
                o_ref, s_ref = dirs[dr][3], dirs[dr][4]
                st = st_ref[dr, h]
                stb = _bf(st)
                s_ref[sub[dr], h] = stb
                o_ref[rows[dr], sl] = _nn(a_h, p["v"][:, sl]) + _nt(p["q2"][:, sl], stb)
                st_ref[dr, h] = st * p["el"][:, sl] + _tn(p["v"][:, sl], p["k2"][:, sl])

    def col(order, blkcol):
        return pl.BlockSpec((ns * c, hw), lambda s: (order(s, nc, nt), blkcol))

    def st_spec(order):
        return pl.BlockSpec((ns, HG_HEADS, d, d), lambda s: (order(s, nc, nt), 0, 0, 0))

    in_specs = []
    for dr, order in enumerate(orders):
        in_specs += [col(order, 0), col(order, 1 + dr), col(order, 3)]
    in_specs.append(_full(lbraw.shape))
    return _host_call(
        body, ex, lambda: pl.program_id(0) == 0, lambda: pl.program_id(0) == nt - 1,
        name=name, grid=(nt,), in_specs=in_specs,
        out_specs=[col(_fw_chunk, 0), col(_bw_chunk, 0), st_spec(_fw_chunk), st_spec(_bw_chunk)],
        out_shape=[jax.ShapeDtypeStruct((T, hw), F32), jax.ShapeDtypeStruct((T, hw), F32),
                   jax.ShapeDtypeStruct((nt * ns, HG_HEADS, d, d), ACT), jax.ShapeDtypeStruct((nt * ns, HG_HEADS, d, d), ACT)],
        scratch_shapes=[pltpu.VMEM((2, HG_HEADS, d, d), F32)], sem=("arbitrary",),
        args=(pb, pb, pb, pb, pb, pb, lbraw))


def _gla_bwd(pb, lbraw, s_fw, s_bw, do, lc, name, ex=None):
    T = pb.shape[0]
    c, hw, d, ns = HG_CHUNK, HG_HEADS * HG_D, HG_D, HG_STEP_CHUNKS
    nt, nc = T // (ns * c), lc // (ns * c)

    def rfw(s, nc_, nt_):
        return _fw_chunk(nt_ - 1 - s, nc_, nt_)

    def rbw(s, nc_, nt_):
        return _bw_chunk(nt_ - 1 - s, nc_, nt_)

    def body(qf, zf, vf, sf, dof, qb, zb, vb, sb, dob_, lb_ref,
             dqf, dzf, dvf, dqb, dzb, dvb, dlb_ref, dst_ref):
        step = pl.program_id(0)

        @pl.when(step == 0)
        def _():
            dst_ref[...] = jnp.zeros_like(dst_ref)

        lb = _lower_bound(lb_ref)
        sets = ((qf, zf, vf, sf, dof, dqf, dzf, dvf), (qb, zb, vb, sb, dob_, dqb, dzb, dvb))
        combos = [(dr, h, slice(h * d, (h + 1) * d)) for dr in range(2) for h in range(HG_HEADS)]
        dlb_tot = jnp.zeros((1, hw), F32)
        for j in range(ns):
            sub = (ns - 1 - j, j)
            rows = [pl.ds(sub[dr] * c, c) for dr in range(2)]
            prep = []
            for dr, (q_ref, z_ref, v_ref, _, do_ref, _, _, _) in enumerate(sets):
                rev = dr == 1
                qr = q_ref[rows[dr], :]
                sq = _sig(qr)
                q = qr * sq
                sg, f, cum, ref, last, last_row = _gla_gates(z_ref[rows[dr], :], lb, rev)
                k = 1.0 - f
                e_qr, e_kr, e_q, e_kl = jnp.exp(cum - ref), jnp.exp(ref - cum), jnp.exp(cum), jnp.exp(last - cum)
                q1, k1, q2, k2 = q * e_qr, k * e_kr, q * e_q, k * e_kl
                prep.append(dict(qr=qr, sq=sq, sg=sg, f=f, e_qr=e_qr, e_kr=e_kr, e_q=e_q, e_kl=e_kl, el=jnp.exp(last),
                                 q1=q1, k1=k1, q2=q2, k2=k2, q1b=_bf(q1), k1b=_bf(k1), q2b=_bf(q2), k2b=_bf(k2),
                                 vb=_bf(v_ref[rows[dr], :]), dob=_bf(do_ref[rows[dr], :]), mask=_tri(c, rev),
                                 last_row=last_row, acc_t=jnp.where(_tri(c, not rev), 1.0, 0.0).astype(BF16)))
            a = [_bf(jnp.where(prep[dr]["mask"], _nt(prep[dr]["q1b"][:, sl], prep[dr]["k1b"][:, sl]), 0.0))
                 for dr, _, sl in combos]
            da = [_bf(jnp.where(prep[dr]["mask"], _nt(prep[dr]["dob"][:, sl], prep[dr]["vb"][:, sl]), 0.0))
                  for dr, _, sl in combos]
            parts = [dict(dq1=[], dk1=[], dq2=[], dk2=[], dls=[]) for _ in range(2)]
            for (dr, h, sl), a_h, da_h in zip(combos, a, da):
                p = prep[dr]
                s_ref, dv_ref = sets[dr][3], sets[dr][7]
                stb = s_ref[sub[dr], h]
                dst = dst_ref[dr, h]
                dstb = _bf(dst)
                dob_h, vb_h = p["dob"][:, sl], p["vb"][:, sl]
                dv_ref[rows[dr], sl] = _bf(_tn(a_h, dob_h) + _nt(p["k2b"][:, sl], dstb))
                parts[dr]["dq1"].append(_nn(da_h, p["k1b"][:, sl]))
                parts[dr]["dk1"].append(_tn(da_h, p["q1b"][:, sl]))
                parts[dr]["dq2"].append(_nn(dob_h, stb))
                parts[dr]["dk2"].append(_nn(vb_h, dstb))
                el_h = p["el"][:, sl]
                dst_ref[dr, h] = _tn(dob_h, p["q2b"][:, sl]) + dst * el_h
                parts[dr]["dls"].append(jnp.sum(dst * stb.astype(F32), axis=0, keepdims=True) * el_h)
            for dr in range(2):
                p = prep[dr]
                dq_ref, dz_ref = sets[dr][5], sets[dr][6]
                dq1, dk1, dq2, dk2, dls = (jnp.concatenate(parts[dr][n], axis=1) for n in ("dq1", "dk1", "dq2", "dk2", "dls"))
                dq = dq1 * p["e_qr"] + dq2 * p["e_q"]
                dk = dk1 * p["e_kr"] + dk2 * p["e_kl"]
                dcum = dq1 * p["q1"] - dk1 * p["k1"] + dq2 * p["q2"] - dk2 * p["k2"]
                dlast = jnp.sum(dk2 * p["k2"], axis=0, keepdims=True) + dls
                rowid = lax.broadcasted_iota(jnp.int32, (c, 1), 0)
                dcum = dcum + jnp.where(rowid == p["last_row"], dlast, 0.0)
                df = _nn3(p["acc_t"], dcum) / p["f"] - dk
                sg = p["sg"]
                dz_ref[rows[dr], :] = _bf(df * (1.0 - lb) * sg * (1.0 - sg))
                dlb_tot = dlb_tot + jnp.sum(df * (1.0 - sg), axis=0, keepdims=True)
                dq_ref[rows[dr], :] = _bf(dq * (p["sq"] * (1.0 + p["qr"] * (1.0 - p["sq"]))))
        _acc_all(dlb_ref, step, dlb_tot)

    def col(order, blkcol):
        return pl.BlockSpec((ns * c, hw), lambda s: (order(s, nc, nt), blkcol))

    def st_spec(order):
        return pl.BlockSpec((ns, HG_HEADS, d, d), lambda s: (order(s, nc, nt), 0, 0, 0))

    in_specs = []
    for dr, order in enumerate((rfw, rbw)):
        in_specs += [col(order, 0), col(order, 1 + dr), col(order, 3), st_spec(order), col(order, 0)]
    in_specs.append(_full(lbraw.shape))
    out_specs = [col(rfw, 0)] * 3 + [col(rbw, 0)] * 3 + [_whole((1, hw))]
    out_shape = [jax.ShapeDtypeStruct((T, hw), ACT)] * 6 + [jax.ShapeDtypeStruct((1, hw), F32)]
    return _host_call(
        body, ex, lambda: pl.program_id(0) == 0, lambda: pl.program_id(0) == nt - 1,
        name=name, grid=(nt,), in_specs=in_specs, out_specs=out_specs, out_shape=out_shape,
        scratch_shapes=[pltpu.VMEM((2, HG_HEADS, d, d), F32)], sem=("arbitrary",),
        args=(pb, pb, pb, s_fw, do, pb, pb, pb, s_bw, do, lbraw))


def _ret_log_gamma(h, rev):
    hh = RET_HEADS - 1 - h if rev else h
    return math.log(1.0 - 2.0 ** (-5.0 - hh))


def _rope(x, cos, sin):
    half = x.shape[1] // 2
    x1, x2 = x[:, :half], x[:, half:]
    return jnp.concatenate([x1 * cos - x2 * sin, x2 * cos + x1 * sin], axis=1)


def _unrope(dy, cos, sin):
    half = dy.shape[1] // 2
    d1, d2 = dy[:, :half], dy[:, half:]
    return jnp.concatenate([d1 * cos + d2 * sin, d2 * cos - d1 * sin], axis=1)


def _ret_decays(lg, rev):
    c = RET_CHUNK
    r = lax.broadcasted_iota(jnp.int32, (c, c), 0)
    k = lax.broadcasted_iota(jnp.int32, (c, c), 1)
    rel = (k - r) if rev else (r - k)
    dm = jnp.where(rel >= 0, jnp.exp(lg * jnp.maximum(rel, 0).astype(F32)), 0.0)
    pos = lax.broadcasted_iota(jnp.int32, (c, 1), 0).astype(F32)
    if rev:
        qdec = jnp.exp(lg * (c - pos))
        kdec = jnp.exp(lg * pos)
    else:
        qdec = jnp.exp(lg * (pos + 1.0))
        kdec = jnp.exp(lg * (c - 1.0 - pos))
    return dm, qdec, kdec


def _ret_fwd(q, k, v, cos, sin, lc, name, ex=None):
    T = q.shape[0]
    c, dk, dv = RET_CHUNK, RET_DK, RET_DV
    nt, nc = T // c, lc // c
    kscale = dk ** -0.5

    def body(qf, kf, vf, cf, sf_, qb, kb, vb, cb, sb_, of_ref, ob_ref, stf_ref, stb_ref, st_ref):
        @pl.when(pl.program_id(0) == 0)
        def _():
            st_ref[...] = jnp.zeros_like(st_ref)

        sets = ((qf, kf, vf, cf, sf_, of_ref, stf_ref), (qb, kb, vb, cb, sb_, ob_ref, stb_ref))
        combos = [(dr, h) for dr in range(2) for h in range(RET_HEADS)]
        prep = {}
        for dr, (q_ref, k_ref, v_ref, c_ref, s_ref, _, _) in enumerate(sets):
            rev = dr == 1
            cos_v, sin_v = c_ref[...], s_ref[...]
            for h in range(RET_HEADS):
                lg = _ret_log_gamma(h, rev)
                dm, qdec, kdec = _ret_decays(lg, rev)
                qh = _rope(q_ref[:, h * dk:(h + 1) * dk].astype(F32), cos_v, sin_v)
                kh = _rope(k_ref[:, h * dk:(h + 1) * dk].astype(F32), cos_v, sin_v) * kscale
                prep[dr, h] = dict(qb=_bf(qh), kb=_bf(kh), qin=_bf(qh * qdec), kin=_bf(kh * kdec),
                                   v=_bf(v_ref[:, h * dv:(h + 1) * dv]), dm=dm, decay=math.exp(lg * c))
        sc = {ch: _bf(_nt(prep[ch]["qb"], prep[ch]["kb"]) * prep[ch]["dm"]) for ch in combos}
        for dr, h in combos:
            p = prep[dr, h]
            o_ref, so_ref = sets[dr][5], sets[dr][6]
            st = st_ref[dr, h]
            stb = _bf(st)
            so_ref[0, h] = stb
            o_ref[:, h * dv:(h + 1) * dv] = _bf(_nn(sc[dr, h], p["v"]) + _nt(p["qin"], stb))
            st_ref[dr, h] = st * p["decay"] + _tn(p["v"], p["kin"])

    def spec(order, width):
        return pl.BlockSpec((c, width), lambda s: (order(s, nc, nt), 0))

    def st_spec(order):
        return pl.BlockSpec((1, RET_HEADS, dv, dk), lambda s: (order(s, nc, nt), 0, 0, 0))

    in_specs = []
    for order in (_fw_chunk, _bw_chunk):
        in_specs += [spec(order, RET_HEADS * dk), spec(order, RET_HEADS * dk), spec(order, RET_HEADS * dv),
                     spec(order, dk // 2), spec(order, dk // 2)]
    return _host_call(
        body, ex, lambda: pl.program_id(0) == 0, lambda: pl.program_id(0) == nt - 1,
        name=name, grid=(nt,), in_specs=in_specs,
        out_specs=[spec(_fw_chunk, RET_HEADS * dv), spec(_bw_chunk, RET_HEADS * dv), st_spec(_fw_chunk), st_spec(_bw_chunk)],
        out_shape=[jax.ShapeDtypeStruct((T, RET_HEADS * dv), ACT), jax.ShapeDtypeStruct((T, RET_HEADS * dv), ACT),
                   jax.ShapeDtypeStruct((nt, RET_HEADS, dv, dk), ACT), jax.ShapeDtypeStruct((nt, RET_HEADS, dv, dk), ACT)],
        scratch_shapes=[pltpu.VMEM((2, RET_HEADS, dv, dk), F32)], sem=("arbitrary",),
        args=(q, k, v, cos, sin, q, k, v, cos, sin))


def _ret_bwd(q, k, v, cos, sin, s_fw, s_bw, do, lc, name, ex=None):
    T = q.shape[0]
    c, dk, dv = RET_CHUNK, RET_DK, RET_DV
    nt, nc = T // c, lc // c
    kscale = dk ** -0.5

    def rfw(s, nc_, nt_):
        return _fw_chunk(nt_ - 1 - s, nc_, nt_)

    def rbw(s, nc_, nt_):
        return _bw_chunk(nt_ - 1 - s, nc_, nt_)

    def body(qf, kf, vf, cf, sf_, stf, dof, qb, kb, vb, cb, sb_, stb_, dob_,
             dqf, dkf, dvf, dqb, dkb, dvb, dst_ref):
        @pl.when(pl.program_id(0) == 0)
        def _():
            dst_ref[...] = jnp.zeros_like(dst_ref)

        sets = ((qf, kf, vf, cf, sf_, stf, dof, dqf, dkf, dvf), (qb, kb, vb, cb, sb_, stb_, dob_, dqb, dkb, dvb))
        combos = [(dr, h) for dr in range(2) for h in range(RET_HEADS)]
        prep = {}
        for dr, (q_ref, k_ref, v_ref, c_ref, s_ref, _, do_ref, _, _, _) in enumerate(sets):
            rev = dr == 1
            cos_v, sin_v = c_ref[...], s_ref[...]
            for h in range(RET_HEADS):
                lg = _ret_log_gamma(h, rev)
                dm, qdec, kdec = _ret_decays(lg, rev)
                qh = _rope(q_ref[:, h * dk:(h + 1) * dk].astype(F32), cos_v, sin_v)
                kh = _rope(k_ref[:, h * dk:(h + 1) * dk].astype(F32), cos_v, sin_v) * kscale
                prep[dr, h] = dict(qb=_bf(qh), kb=_bf(kh), qin=_bf(qh * qdec), kin=_bf(kh * kdec),
                                   v=_bf(v_ref[:, h * dv:(h + 1) * dv]), dob=_bf(do_ref[:, h * dv:(h + 1) * dv]),
                                   dm=dm, qdec=qdec, kdec=kdec, decay=math.exp(lg * c), cos=cos_v, sin=sin_v)
        sc = {ch: _bf(_nt(prep[ch]["qb"], prep[ch]["kb"]) * prep[ch]["dm"]) for ch in combos}
        dsc = {ch: _bf(_nt(prep[ch]["dob"], prep[ch]["v"]) * prep[ch]["dm"]) for ch in combos}
        carried = {}
        for dr, h in combos:
            p = prep[dr, h]
            dv_ref = sets[dr][9]
            dst = dst_ref[dr, h]
            dstb = _bf(dst)
            carried[dr, h] = dstb
            dv_ref[:, h * dv:(h + 1) * dv] = _bf(_tn(sc[dr, h], p["dob"]) + _nt(p["kin"], dstb))
            dst_ref[dr, h] = _tn(p["dob"], p["qin"]) + dst * p["decay"]
        for dr, h in combos:
            p = prep[dr, h]
            st_in, dq_ref, dk_ref = sets[dr][5], sets[dr][7], sets[dr][8]
            dq_r = _nn(dsc[dr, h], p["kb"]) + _nn(p["dob"], st_in[0, h]) * p["qdec"]
            dk_r = _tn(dsc[dr, h], p["qb"]) + _nn(p["v"], carried[dr, h]) * p["kdec"]
            dq_ref[:, h * dk:(h + 1) * dk] = _bf(_unrope(dq_r, p["cos"], p["sin"]))
            dk_ref[:, h * dk:(h + 1) * dk] = _bf(_unrope(dk_r * kscale, p["cos"], p["sin"]))

    def spec(order, width):
        return pl.BlockSpec((c, width), lambda s: (order(s, nc, nt), 0))

    def st_spec(order):
        return pl.BlockSpec((1, RET_HEADS, dv, dk), lambda s: (order(s, nc, nt), 0, 0, 0))

    in_specs = []
    for order in (rfw, rbw):
        in_specs += [spec(order, RET_HEADS * dk), spec(order, RET_HEADS * dk), spec(order, RET_HEADS * dv),
                     spec(order, dk // 2), spec(order, dk // 2), st_spec(order), spec(order, RET_HEADS * dv)]
    out_specs, out_shape = [], []
    for order in (rfw, rbw):
        out_specs += [spec(order, RET_HEADS * dk), spec(order, RET_HEADS * dk), spec(order, RET_HEADS * dv)]
        out_shape += [jax.ShapeDtypeStruct((T, RET_HEADS * dk), ACT), jax.ShapeDtypeStruct((T, RET_HEADS * dk), ACT),
                      jax.ShapeDtypeStruct((T, RET_HEADS * dv), ACT)]
    return _host_call(
        body, ex, lambda: pl.program_id(0) == 0, lambda: pl.program_id(0) == nt - 1,
        name=name, grid=(nt,), in_specs=in_specs, out_specs=out_specs, out_shape=out_shape,
        scratch_shapes=[pltpu.VMEM((2, RET_HEADS, dv, dk), F32)], sem=("arbitrary",),
        args=(q, k, v, cos, sin, s_fw, do, q, k, v, cos, sin, s_bw, do))


def _trig_rows(lc, ang):
    ang = ang.astype(np.float64)
    half = ang.shape[1]
    cos = np.concatenate([np.ones((lc, half)), np.cos(ang)], axis=0).astype(np.float32)
    sin = np.concatenate([np.zeros((lc, half)), np.sin(ang)], axis=0).astype(np.float32)
    return cos, sin


def _attn_rope_tables(lc, l):
    t = np.arange(l)
    row = (t // GRID_W).astype(np.float32)
    colp = (t % GRID_W).astype(np.float32)
    n_freq = HEAD_DIM // 4
    inv = np.float32(10000.0) ** (-np.arange(n_freq, dtype=np.float32) / np.float32(n_freq))
    ang = np.concatenate([row[:, None] * inv, colp[:, None] * inv], axis=-1)
    cos, sin = _trig_rows(lc, ang)
    return jnp.asarray(np.concatenate([cos, cos], axis=1)), jnp.asarray(np.concatenate([-sin, sin], axis=1))


def _ret_rope_tables(lc, l):
    theta = np.float32(1.0) / (np.float32(10000.0) ** np.linspace(0.0, 1.0, RET_DK // 2, dtype=np.float32))
    ang = np.arange(l, dtype=np.float32)[:, None] * theta
    cos, sin = _trig_rows(lc, ang)
    return jnp.asarray(cos), jnp.asarray(sin)


def _heads_major(slab, n_heads):
    t = slab.shape[0]
    return slab.reshape(t, n_heads, HEAD_DIM).transpose(1, 0, 2)


def _slab(hm):
    nh, t, hd = hm.shape
    return hm.transpose(1, 0, 2).reshape(t, nh * hd)


COL_SHARDED = ("ffn_in0", "ffn_in1", "even_in", "even_in_a", "even_in_b", "odd_in")


def _full_weight(name, g):
    if name in COL_SHARDED:
        return g.transpose(1, 0, 2).reshape(g.shape[1], -1)
    return g.reshape(-1, g.shape[2])


def _shard_slots(name, g):
    if name in COL_SHARDED:
        return g.reshape(g.shape[0], N_DEV, -1).transpose(1, 0, 2)
    return g.reshape(N_DEV, -1, g.shape[1])


def _local_step(xs, target, mv, norm_g, w, qk_g, sink, hg_out_g, lbraw, lc, shards=None):
    _, T, dm = _stream(xs)
    l = T - lc
    tm = lc
    blk = ATTN_BLOCK
    d2, d3 = 2 * dm, 3 * dm
    w = dict(w)
    gw, recv = {}, {}

    def ms(layer, a, b):
        return mv[layer, :, :, a:b]

    def gather(names):
        return None if shards is None else _Exchange(GATHER2, [shards[n] for n in names])

    def arrived(names, got):
        for n, g in zip(names, got):
            w[n] = _full_weight(n, g)

    def scatter(names):
        return None if shards is None else _Exchange(SCATTER, [_shard_slots(n, gw[n]) for n in names])

    def scattered(names, got):
        for n, g in zip(names, got):
            recv[n] = g

    g00, g01, g10, g11 = (norm_g[i, j][None, :] for i in (0, 1) for j in (0, 1))

    cos2, sin2 = _attn_rope_tables(lc, l)
    cosp, sinp = jnp.concatenate([cos2, cos2], axis=1), jnp.concatenate([sin2, sin2], axis=1)
    gains5 = jnp.concatenate([jnp.broadcast_to(jnp.tile(qk_g[0], 2), (N_PAIRS - 1, PAIR)), jnp.tile(qk_g[1], 2)[None]])[:, None, :]
    riding = ["even_out"]
    (pa, pb, qt, ks, vs), got = _pre_fwd(xs, g00, ms(0, 0, d2), w["even_in"], ((0, 768), (768, 3328)), tm, "pre0_fwd",
                                         gather(riding), qk=(gains5, cosp, sinp))
    arrived(riding, got)
    sinkb = jnp.broadcast_to(sink.reshape(ATTN_KV, 4, 1, 1), (ATTN_KV, 4, blk, 1)).reshape(ATTN_KV, 4 * blk, 1)
    riding = ["ffn_in0"]
    (a_slab, lse), got = _attn_slab_fwd(qt, ks, vs, sinkb, lc, "attn_fwd", gather(riding))
    arrived(riding, got)
    riding = ["ffn_out0", "odd_out"]
    (hg_of, hg_ob, hg_sf, hg_sb), got = _gla_fwd(pb, lbraw, lc, "hgrn_fwd", gather(riding))
    arrived(riding, got)
    x01, z0, yp0 = _post_fwd(xs, hg_of, hg_ob, pb, 4, hg_out_g, a_slab, w["even_out"], ms(0, d2, d3), HG_D, tm, "post0_fwd")
    riding = ["odd_in"]
    (x02, u0, f0), got = _ffn_fwd(x01, g01, ms(0, d3, 6 * dm), w["ffn_in0"], w["ffn_out0"], tm, "ffn0_fwd", ex=gather(riding))
    arrived(riding, got)

    riding = ["ffn_out1"]
    (rq, rk, rv, rg), got = _pre_fwd(x02, g10, ms(1, 0, d2), w["odd_in"],
                                     ((0, 1024), (1024, 2048), (2048, 4096), (4096, 6144)), tm, "pre1_fwd", gather(riding),
                                     out_dtype=ACT)
    arrived(riding, got)
    rcos, rsin = _ret_rope_tables(lc, l)
    riding = ["ffn_in1"]
    (rt_of, rt_ob, rt_sf, rt_sb), got = _ret_fwd(rq, rk, rv, rcos, rsin, lc, "ret_fwd", gather(riding))
    arrived(riding, got)
    x11, z1, yp1 = _post_fwd(x02, rt_of, rt_ob, rg, 0, None, None, w["odd_out"], ms(1, d2, d3), RET_DV, tm, "post1_fwd")
    (dx, u1, f1, loss), _ = _ffn_fwd(x11, g11, ms(1, d3, 6 * dm), w["ffn_in1"], w["ffn_out1"], tm, "ffn1_fwd", target)

    (dx, h, du, act, df, dms_f1, dg11), _ = _ffn_bwd(x11, dx, u1, f1, g11, ms(1, d3, 6 * dm), w["ffn_in1"], w["ffn_out1"], tm,
                                                     "ffn1_bwd")
    gw["ffn_in1"] = _wgrad(h, du, "wg_ffn_in1")
    gw["ffn_out1"] = _wgrad(act, df, "wg_ffn_out1")
    do1, dgr1, dy1, z1_t, dgate_p1, _ = _post_bwd(dx, z1, yp1, rt_of, rt_ob, rg, 0, None, w["odd_out"], ms(1, d2, d3), 0, RET_DV, tm,
                                                  "post1_bwd")
    gw["odd_out"] = _wgrad(z1_t, dy1, "wg_odd_out")
    riding = ["ffn_in1"]
    (dqf, dkf, dvf, dqb, dkb, dvb), got = _ret_bwd(rq, rk, rv, rcos, rsin, rt_sf, rt_sb, do1, lc, "ret_bwd", scatter(riding))
    scattered(riding, got)
    riding = ["odd_out", "ffn_out1"]
    (dx, h, dp, dms_p1, dg10), got = _pre_bwd(x02, dx, g10, ms(1, 0, d2), w["odd_in"],
                                              [(0, [dqf, dqb]), (1024, [dkf, dkb]), (2048, [dvf, dvb]), (4096, [dgr1])], tm,
                                              "pre1_bwd", ex=scatter(riding))
    scattered(riding, got)
    gw["odd_in"] = _wgrad(h, dp, "wg_odd_in")

    riding = ["odd_in"]
    (dx, h, du, act, df, dms_f0, dg01), got = _ffn_bwd(x01, dx, u0, f0, g01, ms(0, d3, 6 * dm), w["ffn_in0"], w["ffn_out0"], tm,
                                                       "ffn0_bwd", scatter(riding))
    scattered(riding, got)
    gw["ffn_in0"] = _wgrad(h, du, "wg_ffn_in0")
    gw["ffn_out0"] = _wgrad(act, df, "wg_ffn_out0")
    do0, dgr0, da0, dy0, z0_t, dgate_p0, d_hg_gain = _post_bwd(dx, z0, yp0, hg_of, hg_ob, pb, 4, hg_out_g, w["even_out"],
                                                              ms(0, d2, d3), 512, HG_D, tm, "post0_bwd")
    gw["even_out"] = _wgrad(z0_t, dy0, "wg_even_out")
    riding = ["ffn_in0"]
    (hq_f, hz_f, hv_f, hq_b, hz_b, hv_b, dlb), got = _gla_bwd(pb, lbraw, hg_sf, hg_sb, do0, lc, "hgrn_bwd", scatter(riding))
    scattered(riding, got)
    riding = ["even_out", "ffn_out0"]
    (dq_att, dk_att, dv_att, dsink), got = _attn_slab_bwd(qt, ks, vs, sinkb, a_slab, lse, da0, lc, "attn_bwd", scatter(riding))
    scattered(riding, got)
    pieces0 = [(640, [dv_att]), (768, [hq_f, hq_b]), (1280, [hz_f]), (1792, [hz_b]), (2304, [hv_f, hv_b]), (2816, [dgr0])]
    (dx, h, dp, dms_p0, dg00, dgain5), _ = _pre_bwd(xs, dx, g00, ms(0, 0, d2), w["even_in"], pieces0, tm, "pre0_bwd",
                                                    latent_dx=shards is not None,
                                                    qk=(dq_att, dk_att, pa, gains5, cosp, sinp))
    if shards is None:
        gw["even_in"] = _wgrad(h, dp, "wg_even_in")
    else:
        half = dm // 2
        gw["even_in_a"] = _wgrad(h, dp, "wg_even_in_a", rows=(0, half))
        gw["even_in_b"], got = _wgrad(h, dp, "wg_even_in_b", rows=(half, half), ex=scatter(["even_in_a"]))
        scattered(["even_in_a"], got)

    dmv = jnp.stack([jnp.concatenate([dms_p0, dgate_p0, dms_f0], axis=2), jnp.concatenate([dms_p1, dgate_p1, dms_f1], axis=2)])
    small = {
        "dmv": dmv,
        "norm_g": jnp.stack([jnp.stack([dg00[0], dg01[0]]), jnp.stack([dg10[0], dg11[0]])]),
        "qk_g": jnp.stack([jnp.sum(dgain5[:N_PAIRS - 1, 0].reshape(-1, HEAD_DIM), axis=0),
                           jnp.sum(dgain5[N_PAIRS - 1, 0].reshape(-1, HEAD_DIM), axis=0)]),
        "sink": dsink.reshape(ATTN_HEADS),
        "hg_out_g": d_hg_gain[0],
        "lb": dlb[0],
        "loss": loss[0, 0],
    }
    if shards is not None:
        gw = {n: recv.get(n, g) for n, g in gw.items()}
    return loss, dx, gw, small


HBM_SPEC = pl.BlockSpec(memory_space=pltpu.HBM)


def _my_index():
    return 4 * lax.axis_index("x") + 2 * lax.axis_index("y") + lax.axis_index("c")


def _peer(k):
    pos = []
    for axis, bit in (("x", 4), ("y", 2), ("c", 1)):
        a = lax.axis_index(axis)
        pos.append(1 - a if k & bit else a)
    return tuple(pos)


def _peer_index(k):
    px, py, pc = _peer(k)
    return 4 * px + 2 * py + pc


GATHER, SCATTER = "gather", "scatter"
GATHER2 = "gather over ICI once per chip"
SIBLING = 1
OTHER_CHIPS = (2, 4, 6)


class _Exchange:
    def __init__(self, mode, arrays):
        self.mode, self.arrays, self.n = mode, list(arrays), len(arrays)

    def out_shape(self):
        if self.mode in (GATHER, GATHER2):
            return [jax.ShapeDtypeStruct((N_DEV,) + a.shape, a.dtype) for a in self.arrays]
        return [jax.ShapeDtypeStruct(a.shape, a.dtype) for a in self.arrays]

    def specs(self):
        return [HBM_SPEC] * self.n

    def scratch(self):
        return [pltpu.SemaphoreType.DMA((self.n, N_DEV - 1)), pltpu.SemaphoreType.DMA((self.n, N_DEV - 1)),
                pltpu.SemaphoreType.DMA((self.n,))]

    def _copies(self, in_refs, out_refs, send_sems, recv_sems, local_sems, landing):
        me = _my_index()
        local, remote = [], []
        for a, (src, dst) in enumerate(zip(in_refs, out_refs)):
            part = (lambda j, s=src: s) if self.mode == GATHER else (lambda j, s=src: s.at[j])
            local.append(pltpu.make_async_copy(part(me), dst.at[me], local_sems.at[a]))
            for k in range(1, N_DEV):
                pj = _peer_index(k)
                remote.append(pltpu.make_async_remote_copy(
                    src_ref=part(pj), dst_ref=dst.at[pj if landing else me], send_sem=send_sems.at[a, k - 1],
                    recv_sem=recv_sems.at[a, k - 1], device_id=_peer(k), device_id_type=MESH))
        return local, remote

    def _copy2(self, a, src, dst, sems, slot, relation, to):
        send_sems, recv_sems, _ = sems
        return pltpu.make_async_remote_copy(src_ref=src, dst_ref=dst.at[slot], send_sem=send_sems.at[a, relation - 1],
                                            recv_sem=recv_sems.at[a, relation - 1], device_id=_peer(to), device_id_type=MESH)

    def start(self, in_refs, out_refs, sems):
        if self.mode == GATHER2:
            me = _my_index()
            for a, (src, dst) in enumerate(zip(in_refs, out_refs)):
                pltpu.make_async_copy(src, dst.at[me], sems[2].at[a]).start()
                for k in (SIBLING,) + OTHER_CHIPS:
                    self._copy2(a, src, dst, sems, me, k, k).start()
            return
        local, remote = self._copies(in_refs, out_refs, *sems, landing=False)
        for cp in local + remote:
            cp.start()

    def forward(self, in_refs, out_refs, sems):
        for a, (src, dst) in enumerate(zip(in_refs, out_refs)):
            for r in OTHER_CHIPS:
                pj = _peer_index(r)
                self._copy2(a, src, dst, sems, pj, r, r).wait_recv()
                self._copy2(a, dst.at[pj], dst, sems, pj, r ^ SIBLING, SIBLING).start()

    def wait(self, in_refs, out_refs, sems):
        if self.mode == GATHER2:
            me = _my_index()
            for a, (src, dst) in enumerate(zip(in_refs, out_refs)):
                for k in (SIBLING,) + OTHER_CHIPS:
                    self._copy2(a, src, dst, sems, me, k, k).wait_send()
                self._copy2(a, src, dst, sems, _peer_index(SIBLING), SIBLING, SIBLING).wait_recv()
                for r in OTHER_CHIPS:
                    passed = self._copy2(a, src, dst, sems, _peer_index(r ^ SIBLING), r ^ SIBLING, SIBLING)
                    passed.wait_send()
                    passed.wait_recv()
                pltpu.make_async_copy(src, dst.at[me], sems[2].at[a]).wait()
            return
        local, remote = self._copies(in_refs, out_refs, *sems, landing=True)
        for cp in remote:
            cp.wait_send()
            cp.wait_recv()
        for cp in local:
            cp.wait()

    def ride(self, refs, n_in, n_out, first, mid, last):
        refs = list(refs)
        n = self.n
        x_in = refs[n_in:n_in + n]
        x_out = refs[n_in + n + n_out:n_in + 2 * n + n_out]
        sems = refs[n_in + 2 * n + n_out:n_in + 2 * n + n_out + 3]

        @pl.when(first)
        def _():
            self.start(x_in, x_out, sems)

        if self.mode == GATHER2:
            @pl.when(mid)
            def _():
                self.forward(x_in, x_out, sems)

        @pl.when(last)
        def _():
            self.wait(x_in, x_out, sems)

        return refs[:n_in] + refs[n_in + n:n_in + n + n_out] + refs[n_in + 2 * n + n_out + 3:]

    def call(self, name):
        n = self.n

        def body(*refs):
            ins, outs, sems = refs[:n], refs[n:2 * n], refs[2 * n:]
            self.start(ins, outs, sems)
            if self.mode == GATHER2:
                self.forward(ins, outs, sems)
            self.wait(ins, outs, sems)

        return pl.pallas_call(body, name=name, in_specs=self.specs(), out_specs=self.specs(), out_shape=self.out_shape(),
                              scratch_shapes=self.scratch())(*self.arrays)


def _all_gather(v, name):
    return _Exchange(GATHER, [v]).call(name)[0]


def _hosted(kernel_body, ex, n_in, n_out, first, last, grid):
    if ex is None:
        return kernel_body

    def body(*refs):
        mid = pl.program_id(0) == (2 * grid[0]) // 3 if len(grid) == 1 else None
        kernel_body(*ex.ride(refs, n_in, n_out, first(), mid, last()))

    return body


def _host_call(kernel_body, ex, first, last, name, grid, in_specs, out_specs, out_shape, scratch_shapes, sem, args):
    n_in, n_out = len(in_specs), len(out_specs)
    if ex is None:
        outs = pl.pallas_call(kernel_body, name=name, grid=grid, in_specs=in_specs, out_specs=out_specs, out_shape=out_shape,
                              scratch_shapes=scratch_shapes, compiler_params=_cp(*sem))(*args)
        return list(outs), []
    outs = pl.pallas_call(
        _hosted(kernel_body, ex, n_in, n_out, first, last, grid), name=name, grid=grid,
        in_specs=list(in_specs) + ex.specs(), out_specs=list(out_specs) + ex.specs(),
        out_shape=list(out_shape) + ex.out_shape(), scratch_shapes=ex.scratch() + list(scratch_shapes),
        compiler_params=_cp(*sem))(*args, *ex.arrays)
    return list(outs[:n_out]), list(outs[n_out:])


def _mod_fwd(call, mod_w, bias, name):
    nl, dm, n = mod_w.shape

    def body(c_ref, w_ref, b_ref, o_ref):
        cv = c_ref[...]
        cond = _bf(cv * _sig(cv))
        for layer in range(nl):
            o_ref[layer] = _nn(cond, _bf(w_ref[layer])) + b_ref[layer]

    return pl.pallas_call(
        body, name=name, out_shape=jax.ShapeDtypeStruct((nl, call.shape[0], n), F32),
        compiler_params=pltpu.CompilerParams(vmem_limit_bytes=VMEM_LIMIT),
    )(call, mod_w, bias)


def _mod_bwd(call, dm_all, mod_w, name):
    nl, dm, n = mod_w.shape

    def body(c_ref, d_ref, w_ref, gw_ref, dc_ref):
        cv = c_ref[...]
        cond = _bf(cv * _sig(cv))
        dc = jnp.zeros(cv.shape, F32)
        for layer in range(nl):
            db = _bf(d_ref[layer])
            gw_ref[layer] = _tn(cond, db)
            dc = dc + _nt(db, _bf(w_ref[layer]))
        dc_ref[...] = dc

    return pl.pallas_call(
        body, name=name,
        out_shape=[jax.ShapeDtypeStruct(mod_w.shape, F32), jax.ShapeDtypeStruct(call.shape, F32)],
        compiler_params=pltpu.CompilerParams(vmem_limit_bytes=VMEM_LIMIT),
    )(call, dm_all, mod_w)


def _sum_parts(g, name):
    def body(g_ref, o_ref):
        acc = g_ref[0]
        for j in range(1, g.shape[0]):
            acc = acc + g_ref[j]
        o_ref[...] = acc

    return pl.pallas_call(body, name=name, out_shape=jax.ShapeDtypeStruct(g.shape[1:], g.dtype))(g)


def _small_finish(dcond_g, c_ctx, dlb, lbraw, dm_ctx, dm_lat, name):
    def body(dc_ref, c_ref, dlb_ref, lb_ref, mc_ref, ml_ref, gc_ref, glb_ref, gb_ref):
        acc = dc_ref[0, 0:1, :]
        for j in range(1, N_DEV):
            acc = acc + dc_ref[j, 0:1, :]
        cv = c_ref[...]
        s = _sig(cv)
        gc_ref[...] = acc * (s * (1.0 + cv * (1.0 - s)))
        lb = _lower_bound(lb_ref)
        d0 = dlb_ref[...] * lb * (1.0 - lb)
        glb_ref[0:1, :] = d0
        glb_ref[1:2, :] = -d0
        gb_ref[...] = mc_ref[...] + ml_ref[...]

    return pl.pallas_call(
        body, name=name,
        out_shape=[jax.ShapeDtypeStruct(c_ctx.shape, F32), jax.ShapeDtypeStruct(lbraw.shape, F32),
                   jax.ShapeDtypeStruct(dm_ctx.shape, F32)],
    )(dcond_g, c_ctx, dlb, lbraw, dm_ctx, dm_lat)


def _row_tile(r, cap, mult):
    best = r
    for t in range(mult, min(r, cap) + 1, mult):
        if r % t == 0:
            best = t
    return best


def _adam(g_list, w, m, v, name, ex=None):
    nl, r, cdim = w.shape
    p = g_list[0].shape[0]
    tr = _row_tile(r, 128, 16)
    ni = r // tr

    def body(*refs):
        g_refs = refs[:nl]
        w_ref, m_ref, v_ref, go_ref, d_ref, mo_ref, vo_ref = refs[nl:]
        layer = pl.program_id(0)

        def total(g_ref):
            acc = g_ref[0].astype(F32)
            for j in range(1, p):
                acc = acc + g_ref[j].astype(F32)
            return acc

        g = total(g_refs[0])
        for k in range(1, nl):
            g = jnp.where(layer == k, total(g_refs[k]), g)
        m2 = ADAM_B1 * m_ref[0] + (1.0 - ADAM_B1) * g
        v2 = ADAM_B2 * v_ref[0] + (1.0 - ADAM_B2) * (g * g)
        m_hat = m2 / (1.0 - ADAM_B1 ** ADAM_STEP)
        v_hat = v2 / (1.0 - ADAM_B2 ** ADAM_STEP)
        go_ref[0] = g
        d_ref[0] = -ADAM_LR * (m_hat / (jnp.sqrt(v_hat) + ADAM_EPS) + ADAM_WD * w_ref[0])
        mo_ref[0] = m2
        vo_ref[0] = v2

    def g_spec(k):
        return pl.BlockSpec((p, tr, cdim), lambda la, i: (0, jnp.where(la == k, i, jnp.where(la < k, 0, ni - 1)), 0))

    spec = pl.BlockSpec((1, tr, cdim), lambda la, i: (la, i, 0))
    return _host_call(
        body, ex, lambda: (pl.program_id(0) == 0) & (pl.program_id(1) == 0),
        lambda: (pl.program_id(0) == nl - 1) & (pl.program_id(1) == ni - 1),
        name=name, grid=(nl, ni),
        in_specs=[g_spec(k) for k in range(nl)] + [spec, spec, spec],
        out_specs=[spec] * 4, out_shape=[jax.ShapeDtypeStruct((nl, r, cdim), F32)] * 4,
        scratch_shapes=[], sem=("arbitrary", "arbitrary"), args=(*g_list, w, m, v))


def _f32_as_rows(a, width):
    return lax.bitcast_convert_type(a.reshape(-1), BF16).reshape(-1, width)


def _rows_as_f32(rows):
    return lax.bitcast_convert_type(rows.reshape(rows.shape[:-2] + (-1, 2)), F32)


def _pad_rows(a, mult):
    r = (-a.shape[-2]) % mult
    if r == 0:
        return a
    widths = [(0, 0)] * (a.ndim - 2) + [(0, r), (0, 0)]
    return jnp.pad(a, widths)


def _pack_flat(parts, lane):
    flat = jnp.concatenate([p.reshape(-1).astype(F32) for p in parts])
    n = flat.shape[0]
    rows = -(-n // lane)
    rows += (-rows) % 8
    return jnp.pad(flat, (0, rows * lane - n)).reshape(rows, lane)


def _unpack_flat(packed, shapes):
    flat = packed.reshape(-1)
    out, off = [], 0
    for s in shapes:
        n = math.prod(s)
        out.append(flat[off:off + n].reshape(s))
        off += n
    return out


def kernel(x, c, ctx, c_ctx, mod_w, mod_b, norm_g, ffn_w_in, ffn_w_out, even_w_in, even_w_out, attn_qk_norm_g, attn_sink, hgrn_out_norm_g, hgrn_lb, odd_w_in, odd_w_out, loss_target, m_c_ctx, m_mod_w, m_mod_b, m_norm_g, m_ffn_w_in, m_ffn_w_out, m_even_w_in, m_even_w_out, m_attn_qk_norm_g, m_attn_sink, m_hgrn_out_norm_g, m_hgrn_lb, m_odd_w_in, m_odd_w_out, v_c_ctx, v_mod_w, v_mod_b, v_norm_g, v_ffn_w_in, v_ffn_w_out, v_even_w_in, v_even_w_out, v_attn_qk_norm_g, v_attn_sink, v_hgrn_out_norm_g, v_hgrn_lb, v_odd_w_in, v_odd_w_out):
    me = _my_index()
    lc, dm = ctx.shape[1], x.shape[2]
    nmod = mod_w.shape[2]
    big = (ffn_w_in, ffn_w_out, even_w_in, even_w_out, odd_w_in, odd_w_out)

    extra = _pad_rows(jnp.concatenate([_f32_as_rows(c, dm), _f32_as_rows(norm_g, dm)], axis=0), 16)
    shards = {"ffn_in0": ffn_w_in[0], "ffn_in1": ffn_w_in[1], "ffn_out0": ffn_w_out[0], "ffn_out1": ffn_w_out[1],
              "even_in": even_w_in[0], "even_out": even_w_out[0], "odd_in": odd_w_in[0], "odd_out": odd_w_out[0]}
    shards = {n: a.astype(BF16) for n, a in shards.items()}
    first = _Exchange(GATHER2, [shards["even_in"], extra]).call("gather_first")
    w = {"even_in": _full_weight("even_in", first[0])}
    c_all = _rows_as_f32(first[1][:, 0:2])
    norm_g_all = _rows_as_f32(first[1][:, 2:3]).reshape(N_DEV, 2, 2, -1)
    norm_g_full = norm_g_all.transpose(1, 2, 0, 3).reshape(2, 2, dm)

    call = jnp.concatenate([c_all, c_ctx[None, :], jnp.zeros((16 - N_DEV - 1, dm), F32)], axis=0)
    bias = lax.dynamic_slice_in_dim(mod_b, me * nmod, nmod, axis=1)[:, None, :]
    m_sh = _mod_fwd(call, mod_w, bias, "mod_fwd")
    m_g = _all_gather(m_sh.reshape(-1, nmod), "gather_mod").reshape(N_DEV, 2, 16, nmod)
    m_all = m_g.transpose(1, 2, 0, 3).reshape(2, 16, -1)
    m_lat = lax.dynamic_index_in_dim(m_all, me, axis=1, keepdims=False)
    mv = jnp.stack([m_all[:, N_DEV], m_lat], axis=1)[:, :, None, :]

    _, dxs, gw, small = _local_step((ctx[0], x[0]), loss_target[0], mv, norm_g_full, w, attn_qk_norm_g[0], attn_sink[0],
                                    hgrn_out_norm_g, hgrn_lb, lc, shards)
    grad_x = dxs[None]

    last = _Exchange(SCATTER, [_shard_slots("even_in_b", gw["even_in_b"])])
    big_g = [[gw["ffn_in0"], gw["ffn_in1"]], [gw["ffn_out0"], gw["ffn_out1"]], None, [gw["even_out"]],
             [gw["odd_in"]], [gw["odd_out"]]]
    halves = (2, even_w_in.shape[1] // 2, even_w_in.shape[2])
    big_w = (ffn_w_in, ffn_w_out, even_w_in.reshape(halves), even_w_out, odd_w_in, odd_w_out)
    big_m = (m_ffn_w_in, m_ffn_w_out, m_even_w_in.reshape(halves), m_even_w_out, m_odd_w_in, m_odd_w_out)
    big_v = (v_ffn_w_in, v_ffn_w_out, v_even_w_in.reshape(halves), v_even_w_out, v_odd_w_in, v_odd_w_out)
    big_names = ("ffn_w_in", "ffn_w_out", "even_w_in", "even_w_out", "odd_w_in", "odd_w_out")
    big_out = [None] * 6

    def adam_big(i, ex=None):
        big_out[i], got = _adam(big_g[i], big_w[i], big_m[i], big_v[i], "adam_" + big_names[i], ex)
        return got

    dmv = small["dmv"]
    small_shapes = [(2, 6 * dm), (2, 6 * dm), (2, 2, dm), (2, HEAD_DIM), (ATTN_HEADS,), (HG_D,), (HG_HEADS * HG_D,), (1,)]
    vec = _pack_flat([dmv[:, 0, 0], dmv[:, 1, 0], small["norm_g"], small["qk_g"], small["sink"], small["hg_out_g"],
                      small["lb"], small["loss"]], 128)
    big_g[2] = [gw["even_in_a"], adam_big(0, last)[0]]
    vec_g = adam_big(1, _Exchange(GATHER, [vec]))[0]
    tot = _unpack_flat(_sum_parts(vec_g, "sum_small"), small_shapes)
    dm_ctx_tot, dm_lat_tot, g_norm_full, g_qk, g_sink, g_hg, dlb_tot, loss_tot = tot
    dm_lat_each = vec_g.reshape(N_DEV, -1)[:, 12 * dm:24 * dm].reshape(N_DEV, 2, 6 * dm)
    dm_lat_mine = lax.dynamic_slice_in_dim(dm_lat_each, me * nmod, nmod, axis=2).transpose(1, 0, 2)
    dm_ctx_mine = lax.dynamic_slice_in_dim(dm_ctx_tot, me * nmod, nmod, axis=1)[:, None, :]
    dm_all = jnp.concatenate([dm_lat_mine, dm_ctx_mine, jnp.zeros((2, 16 - N_DEV - 1, nmod), F32)], axis=1)
    g_mod_w, dcond = _mod_bwd(call, dm_all, mod_w, "mod_bwd")
    dcond_g = adam_big(4, _Exchange(GATHER, [dcond[N_DEV:]]))[0]
    g_c_ctx, g_lb, g_mod_b = _small_finish(dcond_g, c_ctx[None, :], dlb_tot[None, :], hgrn_lb, dm_ctx_tot, dm_lat_tot,
                                           "small_finish")
    g_norm = lax.dynamic_slice_in_dim(g_norm_full, me * norm_g.shape[2], norm_g.shape[2], axis=2)
    for i in (3, 5, 2):
        adam_big(i)
    big_out[2] = [o.reshape(even_w_in.shape) for o in big_out[2]]
    big_res = [[big_out[i][k] for i in range(6)] for k in range(4)]

    mod_res, _ = _adam([g_mod_w[0][None], g_mod_w[1][None]], mod_w, m_mod_w, v_mod_w, "adam_mod_w")

    sm_w = (c_ctx, mod_b, norm_g, attn_qk_norm_g, attn_sink, hgrn_out_norm_g, hgrn_lb)
    sm_m = (m_c_ctx, m_mod_b, m_norm_g, m_attn_qk_norm_g, m_attn_sink, m_hgrn_out_norm_g, m_hgrn_lb)
    sm_v = (v_c_ctx, v_mod_b, v_norm_g, v_attn_qk_norm_g, v_attn_sink, v_hgrn_out_norm_g, v_hgrn_lb)
    sm_g = (g_c_ctx, g_mod_b, g_norm, g_qk, g_sink, g_hg, g_lb)
    sm_shapes = [a.shape for a in sm_w]
    sm_out, _ = _adam([_pack_flat(sm_g, 128)[None]], _pack_flat(sm_w, 128)[None], _pack_flat(sm_m, 128)[None],
                      _pack_flat(sm_v, 128)[None], "adam_small")
    sm_res = [_unpack_flat(o, sm_shapes) for o in sm_out]

    def ordered(k):
        s, b = sm_res[k], big_res[k]
        return [s[0], mod_res[k], s[1], s[2], b[0], b[1], b[2], b[3], s[3], s[4], s[5], s[6], b[4], b[5]]

    return (loss_tot[0], grad_x, *ordered(0), *ordered(1), *ordered(2), *ordered(3))
```

```python
import functools
import math

import jax
import jax.numpy as jnp
import numpy as np
from jax import lax
from jax.experimental import pallas as pl
from jax.experimental.pallas import tpu as pltpu

F32 = jnp.float32
BF16 = jnp.bfloat16
EPS = 1e-6
N_DEV = 8
MESH = pl.DeviceIdType.MESH

HEAD_DIM = 64
ATTN_HEADS = 8
ATTN_KV = 2
ATTN_BLOCK = 128
WINDOW = 128
GRID_W = 64
HG_HEADS = 4
HG_D = 128
HG_CHUNK = 64
HG_STEP_CHUNKS = 4
RET_HEADS = 4
RET_DK = 256
RET_DV = 512
RET_CHUNK = 256
NEG = -1e30

ADAM_LR = 0.001
ADAM_B1 = 0.9
ADAM_B2 = 0.999
ADAM_EPS = 1e-08
ADAM_WD = 0.01
ADAM_STEP = 10

VMEM_LIMIT = 60 * 1024 * 1024
MXU_WIDTH = 256


def _hidden_chunks(fh, parts=2):
    step = -(-(fh // parts) // MXU_WIDTH) * MXU_WIDTH
    cuts = list(range(0, fh, step)) + [fh]
    return list(zip(cuts[:-1], cuts[1:]))


def _cp(*sem):
    return pltpu.CompilerParams(dimension_semantics=sem, vmem_limit_bytes=VMEM_LIMIT)


def _nn(a, b):
    return jnp.dot(a, b, preferred_element_type=F32)


def _nt(a, b):
    return lax.dot_general(a, b, (((1,), (1,)), ((), ())), preferred_element_type=F32)


def _tn(a, b):
    return lax.dot_general(a, b, (((0,), (0,)), ((), ())), preferred_element_type=F32)


ACT = BF16


def _bf(a):
    return a.astype(ACT)


def _sig(x):
    return jax.nn.sigmoid(x)


def _split3(x):
    h = x.astype(BF16)
    r = x - h.astype(F32)
    m = r.astype(BF16)
    lo = (r - m.astype(F32)).astype(BF16)
    return h, m, lo


def _nn3(m01, x):
    h, m, lo = _split3(x)
    return _nn(m01, h) + _nn(m01, m) + _nn(m01, lo)


def _nn3r(x, m01):
    h, m, lo = _split3(x)
    return _nn(h, m01) + _nn(m, m01) + _nn(lo, m01)


def _full(shape):
    nd = len(shape)
    return pl.BlockSpec(shape, lambda *a: (0,) * nd, pipeline_mode=pl.Buffered(1))


def _whole(shape):
    nd = len(shape)
    return pl.BlockSpec(shape, lambda *a: (0,) * nd)


def _rows(tm, width):
    return pl.BlockSpec((tm, width), lambda i: (i, 0))


def _cols(height, tm):
    return pl.BlockSpec((height, tm), lambda i: (0, i))


def _ctx_lat(width):
    return pl.BlockSpec((1, 1, width), lambda i: (jnp.minimum(i, 1), 0, 0))


def _acc_ctx_lat(ref, i, val):
    @pl.when(i <= 1)
    def _():
        ref[...] = val.reshape(ref.shape)

    @pl.when(i > 1)
    def _():
        ref[...] += val.reshape(ref.shape)


def _acc_all(ref, i, val):
    @pl.when(i == 0)
    def _():
        ref[...] = val.reshape(ref.shape)

    @pl.when(i > 0)
    def _():
        ref[...] += val.reshape(ref.shape)


def _tile(n, cap):
    best = None
    for t in range(128, min(n, cap) + 1, 128):
        if n % t == 0:
            best = t
    return n if best is None else best


def _norm_mod(xv, g, shift, scale):
    r = lax.rsqrt(jnp.mean(xv * xv, axis=-1, keepdims=True) + EPS)
    xhat = xv * r
    n = xhat * g
    return r, xhat, n, n * (1.0 + scale) + shift


def _norm_mod_bwd(dh, r, xhat, n, g, scale):
    dshift = jnp.sum(dh, axis=0, keepdims=True)
    dscale = jnp.sum(dh * n, axis=0, keepdims=True)
    dn = dh * (1.0 + scale)
    dg = jnp.sum(dn * xhat, axis=0, keepdims=True)
    dxh = dn * g
    dx = r * (dxh - xhat * jnp.mean(dxh * xhat, axis=-1, keepdims=True))
    return dx, dshift, dscale, dg


def _stream(x):
    if isinstance(x, tuple):
        return list(x), x[0].shape[0] + x[1].shape[0], x[0].shape[1]
    return [x], x.shape[0], x.shape[1]


def _stream_specs(x, tm, dm):
    if isinstance(x, tuple):
        return [pl.BlockSpec((tm, dm), lambda i: (0, 0)), pl.BlockSpec((tm, dm), lambda i: (jnp.maximum(i - 1, 0), 0))]
    return [_rows(tm, dm)]


def _stream_tile(refs):
    if len(refs) == 2:
        return jnp.where(pl.program_id(0) == 0, refs[0][...], refs[1][...])
    return refs[0][...]


def _pre_fwd(x, gain, ms, w, splits, tm, name, ex=None, out_dtype=F32, qk=None):
    xs, T, dm = _stream(x)
    nx = len(xs)
    nt = T // tm
    nq = 0 if qk is None else 3
    ns = len(splits)

    def body(*refs):
        g_ref, ms_ref, w_ref = refs[nx:nx + 3]
        outs = refs[nx + 3 + nq:]
        ms_v = ms_ref[0]
        h = _norm_mod(_stream_tile(refs[:nx]), g_ref[...], ms_v[:, :dm], ms_v[:, dm:])[3]
        hb = _bf(h)
        for k, ((s, e), o_ref) in enumerate(zip(splits, outs[:ns])):
            part = _nn(hb, w_ref[:, s:e])
            o_ref[...] = part.astype(o_ref.dtype)
            if k == 0 and qk is not None:
                gq_ref, c_ref, s_ref = refs[nx + 3:nx + 6]
                _qk_tile_fwd(part, gq_ref, c_ref[...], s_ref[...], *outs[ns:])

    in_specs = _stream_specs(x, tm, dm) + [_full((1, dm)), _ctx_lat(2 * dm), _full(w.shape)]
    out_specs = [_rows(tm, e - s) for s, e in splits]
    out_shape = [jax.ShapeDtypeStruct((T, e - s), out_dtype) for s, e in splits]
    args = [*xs, gain, ms, w]
    if qk is not None:
        qw = ATTN_HEADS * HEAD_DIM
        in_specs += [_full(qk[0].shape), _rows(tm, PAIR), _rows(tm, PAIR)]
        args += list(qk)
        out_specs += [_rows(tm, qw), _rows(tm, PAIR), _rows(tm, PAIR)]
        out_shape += [jax.ShapeDtypeStruct((T, qw), ACT), jax.ShapeDtypeStruct((T, PAIR), ACT), jax.ShapeDtypeStruct((T, PAIR), ACT)]
    return _host_call(
        body, ex, lambda: pl.program_id(0) == 0, lambda: pl.program_id(0) == nt - 1,
        name=name, grid=(nt,), in_specs=in_specs, out_specs=out_specs, out_shape=out_shape,
        scratch_shapes=[], sem=("arbitrary",), args=tuple(args))


def _pre_bwd(x, dx_in, gain, ms, w, pieces, tm, name, latent_dx=False, ex=None, qk=None):
    xs, T, dm = _stream(x)
    nx = len(xs)
    dx_spec = pl.BlockSpec((tm, dm), lambda i: (jnp.maximum(i - 1, 0), 0)) if latent_dx else _rows(tm, dm)
    dx_rows = T - tm if latent_dx else T
    n_out = w.shape[1]
    flat = [a for _, arrs in pieces for a in arrs]
    nq = 0 if qk is None else 6
    qkw = (ATTN_HEADS + ATTN_KV) * HEAD_DIM

    def body(*refs):
        dxin_ref, g_ref, ms_ref, w_ref = refs[nx:nx + 4]
        rest = refs[nx + 4:]
        p_refs = rest[:len(flat)]
        qk_refs = rest[len(flat):len(flat) + nq]
        dx_ref, h_ref, dp_ref, dms_ref, dg_ref = rest[len(flat) + nq:len(flat) + nq + 5]
        i = pl.program_id(0)
        ms_v = ms_ref[0]
        g = g_ref[...]
        scale = ms_v[:, dm:]
        r, xhat, n, h = _norm_mod(_stream_tile(refs[:nx]), g, ms_v[:, :dm], scale)
        h_ref[...] = _bf(h).T
        dh = jnp.zeros((tm, dm), F32)
        if qk is not None:
            dq_ref, dk_ref, pa_ref, gq_ref, c_ref, s_ref = qk_refs
            dqk, dgs = _qk_tile_bwd(dq_ref, dk_ref, pa_ref, gq_ref, c_ref[...], s_ref[...])
            dgq_ref = rest[len(flat) + nq + 5]
            for p, dgp in enumerate(dgs):
                _acc_all(dgq_ref.at[p], i, dgp)
            vb = _bf(dqk)
            dp_ref[:, :qkw] = vb
            dh = dh + _nt(vb, w_ref[:, :qkw])
        k = 0
        for s, arrs in pieces:
            v = p_refs[k][...].astype(F32)
            for j in range(1, len(arrs)):
                v = v + p_refs[k + j][...].astype(F32)
            k += len(arrs)
            vb = _bf(v)
            wd = vb.shape[1]
            dp_ref[:, s:s + wd] = vb
            dh = dh + _nt(vb, w_ref[:, s:s + wd])
        dx, dshift, dscale, dg = _norm_mod_bwd(dh, r, xhat, n, g, scale)
        dx_ref[...] = dxin_ref[...] + dx
        _acc_ctx_lat(dms_ref, i, jnp.concatenate([dshift, dscale], axis=1))
        _acc_all(dg_ref, i, dg)

    nt = T // tm
    in_specs = (_stream_specs(x, tm, dm) + [_rows(tm, dm), _full((1, dm)), _ctx_lat(2 * dm), _full(w.shape)]
                + [_rows(tm, a.shape[1]) for a in flat])
    out_specs = [dx_spec, _cols(dm, tm), _rows(tm, n_out), _ctx_lat(2 * dm), _whole((1, dm))]
    out_shape = [jax.ShapeDtypeStruct((dx_rows, dm), F32), jax.ShapeDtypeStruct((dm, T), ACT),
                 jax.ShapeDtypeStruct((T, n_out), ACT), jax.ShapeDtypeStruct((2, 1, 2 * dm), F32),
                 jax.ShapeDtypeStruct((1, dm), F32)]
    args = [*xs, dx_in, gain, ms, w, *flat]
    if qk is not None:
        dq, dk, pa, gains, cosp, sinp = qk
        in_specs += [_rows(tm, dq.shape[1]), _rows(tm, PAIR), _rows(tm, qkw), _full(gains.shape), _rows(tm, PAIR), _rows(tm, PAIR)]
        args += [dq, dk, pa, gains, cosp, sinp]
        out_specs.append(_whole(gains.shape))
        out_shape.append(jax.ShapeDtypeStruct(gains.shape, F32))
    return _host_call(
        body, ex, lambda: pl.program_id(0) == 0, lambda: pl.program_id(0) == nt - 1,
        name=name, grid=(nt,), in_specs=in_specs, out_specs=out_specs, out_shape=out_shape,
        scratch_shapes=[], sem=("arbitrary",), args=tuple(args))


def _ffn_fwd(x1, gain, ms, w_in, w_out, tm, name, target=None, ex=None):
    T, dm = x1.shape
    fh = w_out.shape[0]
    head = target is not None

    def body(*refs):
        if head:
            x_ref, g_ref, ms_ref, wi_ref, wo_ref, t_ref, x2_ref, u_ref, f_ref, loss_ref = refs
        else:
            x_ref, g_ref, ms_ref, wi_ref, wo_ref, x2_ref, u_ref, f_ref = refs
        ms_v = ms_ref[0]
        xv = x_ref[...]
        hb = _bf(_norm_mod(xv, g_ref[...], ms_v[:, :dm], ms_v[:, dm:2 * dm])[3])
        f = jnp.zeros((tm, dm), F32)
        for c0, c1 in _hidden_chunks(fh):
            gt = _nn(hb, wi_ref[:, c0:c1])
            up = _nn(hb, wi_ref[:, fh + c0:fh + c1])
            u_ref[:, c0:c1] = _bf(gt)
            u_ref[:, fh + c0:fh + c1] = _bf(up)
            f = f + _nn(_bf(gt * _sig(gt) * up), wo_ref[c0:c1, :])
        f_ref[...] = _bf(f)
        x2 = xv + ms_v[:, 2 * dm:] * f
        if head:
            i = pl.program_id(0)
            e = x2 - t_ref[...]
            x2_ref[...] = jnp.where(i > 0, e * (1.0 / dm), 0.0)
            _acc_all(loss_ref, i, jnp.where(i > 0, jnp.sum(e * e) * (0.5 / dm), 0.0))
        else:
            x2_ref[...] = x2

    ins = [x1, gain, ms, w_in, w_out]
    in_specs = [_rows(tm, dm), _full((1, dm)), _ctx_lat(3 * dm), _full(w_in.shape), _full(w_out.shape)]
    out_specs = [_rows(tm, dm), _rows(tm, 2 * fh), _rows(tm, dm)]
    out_shape = [jax.ShapeDtypeStruct((T, dm), F32), jax.ShapeDtypeStruct((T, 2 * fh), ACT), jax.ShapeDtypeStruct((T, dm), ACT)]
    if head:
        ins.append(target)
        in_specs.append(pl.BlockSpec((tm, dm), lambda i: (jnp.maximum(i - 1, 0), 0)))
        out_specs.append(_whole((1, 1)))
        out_shape.append(jax.ShapeDtypeStruct((1, 1), F32))
    nt = T // tm
    return _host_call(
        body, ex, lambda: pl.program_id(0) == 0, lambda: pl.program_id(0) == nt - 1,
        name=name, grid=(nt,), in_specs=in_specs, out_specs=out_specs, out_shape=out_shape,
        scratch_shapes=[], sem=("arbitrary",), args=tuple(ins))


def _ffn_bwd(x1, dx2, u, f, gain, ms, w_in, w_out, tm, name, ex=None):
    T, dm = x1.shape
    fh = w_out.shape[0]

    def body(x_ref, dx2_ref, u_ref, f_ref, g_ref, ms_ref, wi_ref, wo_ref,
             dx1_ref, h_ref, du_ref, act_ref, df_ref, dms_ref, dg_ref):
        i = pl.program_id(0)
        ms_v = ms_ref[0]
        g = g_ref[...]
        scale = ms_v[:, dm:2 * dm]
        gate = ms_v[:, 2 * dm:]
        r, xhat, n, h = _norm_mod(x_ref[...], g, ms_v[:, :dm], scale)
        h_ref[...] = _bf(h).T
        dx2 = dx2_ref[...]
        dgate = jnp.sum(dx2 * f_ref[...].astype(F32), axis=0, keepdims=True)
        dfb = _bf(dx2 * gate)
        df_ref[...] = dfb
        dh = jnp.zeros((tm, dm), F32)
        for c0, c1 in _hidden_chunks(fh, 1):
            da = _nt(dfb, wo_ref[c0:c1, :])
            gt = u_ref[:, c0:c1].astype(F32)
            up = u_ref[:, fh + c0:fh + c1].astype(F32)
            s = _sig(gt)
            sg = gt * s
            act_ref[c0:c1, :] = _bf(sg * up).T
            dgt = _bf(da * up * (s * (1.0 + gt * (1.0 - s))))
            dup = _bf(da * sg)
            du_ref[:, c0:c1] = dgt
            du_ref[:, fh + c0:fh + c1] = dup
            dh = dh + _nt(dgt, wi_ref[:, c0:c1]) + _nt(dup, wi_ref[:, fh + c0:fh + c1])
        dx, dshift, dscale, dg = _norm_mod_bwd(dh, r, xhat, n, g, scale)
        dx1_ref[...] = dx2 + dx
        _acc_ctx_lat(dms_ref, i, jnp.concatenate([dshift, dscale, dgate], axis=1))
        _acc_all(dg_ref, i, dg)

    nt = T // tm
    return _host_call(
        body, ex, lambda: pl.program_id(0) == 0, lambda: pl.program_id(0) == nt - 1,
        name=name, grid=(nt,),
        in_specs=[_rows(tm, dm), _rows(tm, dm), _rows(tm, 2 * fh), _rows(tm, dm), _full((1, dm)), _ctx_lat(3 * dm),
                  _full(w_in.shape), _full(w_out.shape)],
        out_specs=[_rows(tm, dm), _cols(dm, tm), _rows(tm, 2 * fh), _cols(fh, tm), _rows(tm, dm),
                   _ctx_lat(3 * dm), _whole((1, dm))],
        out_shape=[jax.ShapeDtypeStruct((T, dm), F32), jax.ShapeDtypeStruct((dm, T), ACT),
                   jax.ShapeDtypeStruct((T, 2 * fh), ACT), jax.ShapeDtypeStruct((fh, T), ACT),
                   jax.ShapeDtypeStruct((T, dm), ACT), jax.ShapeDtypeStruct((2, 1, 3 * dm), F32),
                   jax.ShapeDtypeStruct((1, dm), F32)],
        scratch_shapes=[], sem=("arbitrary",), args=(x1, dx2, u, f, gain, ms, w_in, w_out))


def _wgrad(a_t, b, name, rows=None, ex=None):
    T = a_t.shape[1]
    r0, K = (0, a_t.shape[0]) if rows is None else rows
    N = b.shape[1]
    tk, tn, tt = _tile(K, 1408), _tile(N, 1664), _tile(T, 2816)
    nt = T // tt
    assert r0 % tk == 0
    off = r0 // tk
    nk, nn = K // tk, N // tn

    def body(a_ref, b_ref, o_ref, acc_ref):
        t = pl.program_id(2)
        part = _nn(a_ref[...], b_ref[...])

        @pl.when(t == 0)
        def _():
            acc_ref[...] = part

        @pl.when(t > 0)
        def _():
            acc_ref[...] += part

        @pl.when(t == nt - 1)
        def _():
            o_ref[...] = acc_ref[...].astype(o_ref.dtype)

    def at(i, j, t):
        return (pl.program_id(0) == i) & (pl.program_id(1) == j) & (pl.program_id(2) == t)

    outs, got = _host_call(
        body, ex, lambda: at(0, 0, 0), lambda: at(nk - 1, nn - 1, nt - 1),
        name=name, grid=(nk, nn, nt),
        in_specs=[pl.BlockSpec((tk, tt), lambda i, j, t: (i + off, t)), pl.BlockSpec((tt, tn), lambda i, j, t: (t, j))],
        out_specs=[pl.BlockSpec((tk, tn), lambda i, j, t: (i, j))],
        out_shape=[jax.ShapeDtypeStruct((K, N), ACT)],
        scratch_shapes=[pltpu.VMEM((tk, tn), F32)], sem=("arbitrary", "arbitrary", "arbitrary"), args=(a_t, b))
    return outs[0] if ex is None else (outs[0], got)


def _post_fwd(x, o_fw, o_bw, g_src, g_blk, gain, a, w_out, ms, dvh, tm, name):
    xs, T, dm = _stream(x)
    nx = len(xs)
    hv = o_fw.shape[1]
    aw = 0 if a is None else a.shape[1]
    has_gain = gain is not None

    def body(*refs):
        refs = list(refs)
        x_refs = refs[:nx]
        of_ref, ob_ref, g_ref = refs[nx:nx + 3]
        k = nx + 3
        gain_ref = a_ref = None
        if has_gain:
            gain_ref = refs[k]
            k += 1
        if aw:
            a_ref = refs[k]
            k += 1
        w_ref, ms_ref, x1_ref, z_ref, yp_ref = refs[k:k + 5]
        o = of_ref[...].astype(F32) + ob_ref[...].astype(F32)
        gr = g_ref[...].astype(F32)
        if aw:
            z_ref[:, :aw] = _bf(a_ref[...])
        for hd in range(hv // dvh):
            sl = slice(hd * dvh, (hd + 1) * dvh)
            oh = o[:, sl]
            gh = gr[:, sl]
            r = lax.rsqrt(jnp.mean(oh * oh, axis=-1, keepdims=True) + EPS)
            y = oh * r
            if has_gain:
                y = y * gain_ref[...]
            y = y * (gh * _sig(gh))
            z_ref[:, aw + hd * dvh:aw + (hd + 1) * dvh] = _bf(y)
        yp = _nn(z_ref[...], w_ref[...])
        yp_ref[...] = _bf(yp)
        x1_ref[...] = _stream_tile(x_refs) + ms_ref[0] * yp

    ins = xs + [o_fw, o_bw, g_src]
    specs = _stream_specs(x, tm, dm) + [_rows(tm, hv), _rows(tm, hv), pl.BlockSpec((tm, hv), lambda i: (i, g_blk))]
    if has_gain:
        ins.append(gain)
        specs.append(_full(gain.shape))
    if aw:
        ins.append(a)
        specs.append(_rows(tm, aw))
    ins += [w_out, ms]
    specs += [_full(w_out.shape), _ctx_lat(dm)]
    return pl.pallas_call(
        body, name=name, grid=(T // tm,), in_specs=specs,
        out_specs=[_rows(tm, dm), _rows(tm, aw + hv), _rows(tm, dm)],
        out_shape=[jax.ShapeDtypeStruct((T, dm), F32), jax.ShapeDtypeStruct((T, aw + hv), ACT),
                   jax.ShapeDtypeStruct((T, dm), ACT)],
        compiler_params=_cp("arbitrary"),
    )(*ins)


def _post_bwd(dx1, z, yp, o_fw, o_bw, g_src, g_blk, gain, w_out, ms, aw, dvh, tm, name):
    T, dm = dx1.shape
    hv = o_fw.shape[1]
    has_gain = gain is not None

    def body(*refs):
        refs = list(refs)
        dx1_ref, z_ref, yp_ref, of_ref, ob_ref, g_ref = refs[:6]
        k = 6
        gain_ref = None
        if has_gain:
            gain_ref = refs[k]
            k += 1
        w_ref, ms_ref = refs[k:k + 2]
        k += 2
        do_ref, dgr_ref = refs[k:k + 2]
        k += 2
        da_ref = None
        if aw:
            da_ref = refs[k]
            k += 1
        dy_ref, zt_ref, dgate_ref, dgain_ref = refs[k:k + 4]
        i = pl.program_id(0)
        dx1v = dx1_ref[...]
        zt_ref[...] = z_ref[...].T
        _acc_ctx_lat(dgate_ref, i, jnp.sum(dx1v * yp_ref[...].astype(F32), axis=0, keepdims=True))
        dyb = _bf(dx1v * ms_ref[0])
        dy_ref[...] = dyb
        dz = _nt(dyb, w_ref[...])
        if aw:
            da_ref[...] = dz[:, :aw]
        o = of_ref[...].astype(F32) + ob_ref[...].astype(F32)
        gr = g_ref[...].astype(F32)
        dgain = jnp.zeros((1, dvh), F32)
        for hd in range(hv // dvh):
            sl = slice(hd * dvh, (hd + 1) * dvh)
            oh = o[:, sl]
            gh = gr[:, sl]
            dyh = dz[:, aw + hd * dvh:aw + (hd + 1) * dvh]
            r = lax.rsqrt(jnp.mean(oh * oh, axis=-1, keepdims=True) + EPS)
            n = oh * r
            s = _sig(gh)
            sl_g = gh * s
            gn = gain_ref[...] if has_gain else 1.0
            dgr_ref[:, sl] = _bf(dyh * n * gn * (s * (1.0 + gh * (1.0 - s))))
            dn = dyh * gn * sl_g
            dgain = dgain + jnp.sum(dyh * n * sl_g, axis=0, keepdims=True)
            do_ref[:, sl] = _bf(r * (dn - n * jnp.mean(dn * n, axis=-1, keepdims=True)))
        _acc_all(dgain_ref, i, dgain)

    ins = [dx1, z, yp, o_fw, o_bw, g_src]
    specs = [_rows(tm, dm), _rows(tm, aw + hv), _rows(tm, dm), _rows(tm, hv), _rows(tm, hv),
             pl.BlockSpec((tm, hv), lambda i: (i, g_blk))]
    if has_gain:
        ins.append(gain)
        specs.append(_full(gain.shape))
    ins += [w_out, ms]
    specs += [_full(w_out.shape), _ctx_lat(dm)]
    out_specs = [_rows(tm, hv), _rows(tm, hv)]
    out_shape = [jax.ShapeDtypeStruct((T, hv), ACT), jax.ShapeDtypeStruct((T, hv), ACT)]
    if aw:
        out_specs.append(_rows(tm, aw))
        out_shape.append(jax.ShapeDtypeStruct((T, aw), F32))
    out_specs += [_rows(tm, dm), _cols(aw + hv, tm), _ctx_lat(dm), _whole((1, dvh))]
    out_shape += [jax.ShapeDtypeStruct((T, dm), ACT), jax.ShapeDtypeStruct((aw + hv, T), ACT),
                  jax.ShapeDtypeStruct((2, 1, dm), F32), jax.ShapeDtypeStruct((1, dvh), F32)]
    return pl.pallas_call(
        body, name=name, grid=(T // tm,), in_specs=specs, out_specs=out_specs, out_shape=out_shape,
        compiler_params=_cp("arbitrary"),
    )(*ins)


def _loss_bwd(x, target, tm, name):
    T, dm = x.shape

    def body(x_ref, t_ref, dx_ref, loss_ref):
        i = pl.program_id(0)

        @pl.when(i == 0)
        def _():
            dx_ref[...] = jnp.zeros_like(dx_ref)
            loss_ref[...] = jnp.zeros_like(loss_ref)

        @pl.when(i > 0)
        def _():
            e = x_ref[...] - t_ref[...]
            dx_ref[...] = e * (1.0 / dm)
            loss_ref[...] += jnp.sum(e * e) * (0.5 / dm)

    return pl.pallas_call(
        body, name=name, grid=(T // tm,),
        in_specs=[_rows(tm, dm), pl.BlockSpec((tm, dm), lambda i: (jnp.maximum(i - 1, 0), 0))],
        out_specs=[_rows(tm, dm), _whole((1, 1))],
        out_shape=[jax.ShapeDtypeStruct((T, dm), F32), jax.ShapeDtypeStruct((1, 1), F32)],
        compiler_params=_cp("arbitrary"),
    )(x, target)


def _swap_matrix():
    r = lax.broadcasted_iota(jnp.int32, (HEAD_DIM, HEAD_DIM), 0)
    c = lax.broadcasted_iota(jnp.int32, (HEAD_DIM, HEAD_DIM), 1)
    return jnp.where((r + HEAD_DIM // 2) % HEAD_DIM == c, 1.0, 0.0).astype(BF16)


def _qk_prep_fwd(raw, gains, cos2, sin2, tq, name):
    nh, T, hd = raw.shape

    def body(x_ref, g_ref, c_ref, s_ref, o_ref):
        hidx = pl.program_id(0)
        xv = x_ref[0]
        r = lax.rsqrt(jnp.mean(xv * xv, axis=-1, keepdims=True) + EPS)
        n = xv * r * g_ref[0]
        y = n * c_ref[...] + _nn3r(n, _swap_matrix()) * s_ref[...]
        sc = jnp.where(hidx < ATTN_HEADS, HEAD_DIM ** -0.5, 1.0)
        o_ref[0] = _bf(y * sc)

    return pl.pallas_call(
        body, name=name, grid=(nh, T // tq),
        in_specs=[pl.BlockSpec((1, tq, hd), lambda h, i: (h, i, 0)), pl.BlockSpec((1, 1, hd), lambda h, i: (h, 0, 0)),
                  pl.BlockSpec((tq, hd), lambda h, i: (i, 0)), pl.BlockSpec((tq, hd), lambda h, i: (i, 0))],
        out_specs=pl.BlockSpec((1, tq, hd), lambda h, i: (h, i, 0)),
        out_shape=jax.ShapeDtypeStruct((nh, T, hd), ACT),
        compiler_params=_cp("arbitrary", "arbitrary"),
    )(raw, gains, cos2, sin2)


def _qk_prep_bwd(dy, raw, gains, cos2, sin2, tq, name):
    nh, T, hd = raw.shape

    def body(dy_ref, x_ref, g_ref, c_ref, s_ref, dx_ref, dg_ref):
        hidx = pl.program_id(0)
        i = pl.program_id(1)
        xv = x_ref[0]
        g = g_ref[0]
        r = lax.rsqrt(jnp.mean(xv * xv, axis=-1, keepdims=True) + EPS)
        xhat = xv * r
        sc = jnp.where(hidx < ATTN_HEADS, HEAD_DIM ** -0.5, 1.0)
        dyv = dy_ref[0] * sc
        dn = dyv * c_ref[...] + _nn3r(dyv * s_ref[...], _swap_matrix())
        _acc_all(dg_ref, i, jnp.sum(dn * xhat, axis=0, keepdims=True))
        dxh = dn * g
        dx_ref[0] = r * (dxh - xhat * jnp.mean(dxh * xhat, axis=-1, keepdims=True))

    return pl.pallas_call(
        body, name=name, grid=(nh, T // tq),
        in_specs=[pl.BlockSpec((1, tq, hd), lambda h, i: (h, i, 0)), pl.BlockSpec((1, tq, hd), lambda h, i: (h, i, 0)),
                  pl.BlockSpec((1, 1, hd), lambda h, i: (h, 0, 0)),
                  pl.BlockSpec((tq, hd), lambda h, i: (i, 0)), pl.BlockSpec((tq, hd), lambda h, i: (i, 0))],
        out_specs=[pl.BlockSpec((1, tq, hd), lambda h, i: (h, i, 0)), pl.BlockSpec((1, 1, hd), lambda h, i: (h, 0, 0))],
        out_shape=[jax.ShapeDtypeStruct((nh, T, hd), F32), jax.ShapeDtypeStruct((nh, 1, hd), F32)],
        compiler_params=_cp("arbitrary", "arbitrary"),
    )(dy, raw, gains, cos2, sin2)


def _attn_scores(q, k_ref, i, lc, T, sink):
    blk = ATTN_BLOCK
    kc = k_ref[0, pl.ds(blk, lc), :]
    kw = k_ref[0, pl.ds(pl.multiple_of(i * blk, blk), 3 * blk), :]
    s_c = _nt(q, kc)
    s_w = _nt(q, kw)
    row = lax.broadcasted_iota(jnp.int32, (4 * blk, 1), 0)
    qpos = i * blk + (row & (blk - 1))
    kpos = (i - 1) * blk + lax.broadcasted_iota(jnp.int32, (1, 3 * blk), 1)
    valid = (qpos >= lc) & (kpos >= lc) & (kpos < T) & (jnp.abs(kpos - qpos) <= WINDOW)
    s_w = jnp.where(valid, s_w, NEG)
    return kc, kw, s_c, s_w


def _attn_fwd(qt, kp, vp, sinkb, lc, name, ex=None):
    nh, T, hd = qt.shape
    blk = ATTN_BLOCK
    g = nh // ATTN_KV

    def body(q_ref, k_ref, v_ref, sink_ref, o_ref, lse_ref):
        i = pl.program_id(1)
        q = q_ref[...].reshape(g * blk, hd)
        sink = sink_ref[0]
        kc, kw, s_c, s_w = _attn_scores(q, k_ref, i, lc, T, sink)
        m = jnp.maximum(jnp.maximum(jnp.max(s_c, axis=-1, keepdims=True), jnp.max(s_w, axis=-1, keepdims=True)), sink)
        e_c = jnp.exp(s_c - m)
        e_w = jnp.exp(s_w - m)
        den = jnp.exp(sink - m) + jnp.sum(e_c, axis=-1, keepdims=True) + jnp.sum(e_w, axis=-1, keepdims=True)
        inv = 1.0 / den
        vc = v_ref[0, pl.ds(blk, lc), :]
        vw = v_ref[0, pl.ds(pl.multiple_of(i * blk, blk), 3 * blk), :]
        o = _nn(_bf(e_c * inv), vc) + _nn(_bf(e_w * inv), vw)
        o_ref[...] = o.reshape(g, blk, hd)
        lse_ref[...] = (m + jnp.log(den)).reshape(g, blk, 1)

    nb = T // blk
    return _host_call(
        body, ex, lambda: (pl.program_id(0) == 0) & (pl.program_id(1) == 0),
        lambda: (pl.program_id(0) == ATTN_KV - 1) & (pl.program_id(1) == nb - 1),
        name=name, grid=(ATTN_KV, nb),
        in_specs=[pl.BlockSpec((g, blk, hd), lambda kv, i: (kv, i, 0)),
                  pl.BlockSpec((1, T + 2 * blk, hd), lambda kv, i: (kv, 0, 0)),
                  pl.BlockSpec((1, T + 2 * blk, hd), lambda kv, i: (kv, 0, 0)),
                  pl.BlockSpec((1, g * blk, 1), lambda kv, i: (kv, 0, 0))],
        out_specs=[pl.BlockSpec((g, blk, hd), lambda kv, i: (kv, i, 0)),
                   pl.BlockSpec((g, blk, 1), lambda kv, i: (kv, i, 0))],
        out_shape=[jax.ShapeDtypeStruct((nh, T, hd), F32), jax.ShapeDtypeStruct((nh, T, 1), F32)],
        scratch_shapes=[], sem=("arbitrary", "arbitrary"), args=(qt, kp, vp, sinkb))


def _attn_bwd(qt, kp, vp, sinkb, o, lse, do, lc, name):
    nh, T, hd = qt.shape
    blk = ATTN_BLOCK
    g = nh // ATTN_KV

    def body(q_ref, k_ref, v_ref, sink_ref, o_ref, lse_ref, do_ref, dq_ref, dk_ref, dv_ref, ds_ref):
        i = pl.program_id(1)

        @pl.when(i == 0)
        def _():
            dk_ref[...] = jnp.zeros_like(dk_ref)
            dv_ref[...] = jnp.zeros_like(dv_ref)
            ds_ref[...] = jnp.zeros_like(ds_ref)

        q = q_ref[...].reshape(g * blk, hd)
        sink = sink_ref[0]
        lse = lse_ref[...].reshape(g * blk, 1)
        dov = do_ref[...].reshape(g * blk, hd)
        delta = jnp.sum(dov * o_ref[...].reshape(g * blk, hd), axis=-1, keepdims=True)
        kc, kw, s_c, s_w = _attn_scores(q, k_ref, i, lc, T, sink)
        p_c = jnp.exp(s_c - lse)
        p_w = jnp.exp(s_w - lse)
        win = pl.ds(pl.multiple_of(i * blk, blk), 3 * blk)
        vc = v_ref[0, pl.ds(blk, lc), :]
        vw = v_ref[0, win, :]
        dob = _bf(dov)
        ds_c = _bf(p_c * (_nt(dob, vc) - delta))
        ds_w = _bf(p_w * (_nt(dob, vw) - delta))
        dsr = -jnp.exp(sink - lse) * delta
        for hh in range(g):
            ds_ref[0, hh:hh + 1, :] += jnp.sum(dsr[hh * blk:(hh + 1) * blk, :], axis=0, keepdims=True)
        dq_ref[...] = (_nn(ds_c, kc) + _nn(ds_w, kw)).reshape(g, blk, hd)
        dk_ref[0, pl.ds(blk, lc), :] += _tn(ds_c, q)
        dk_ref[0, win, :] += _tn(ds_w, q)
        dv_ref[0, pl.ds(blk, lc), :] += _tn(_bf(p_c), dob)
        dv_ref[0, win, :] += _tn(_bf(p_w), dob)

    qspec = pl.BlockSpec((g, blk, hd), lambda kv, i: (kv, i, 0))
    kspec = pl.BlockSpec((1, T + 2 * blk, hd), lambda kv, i: (kv, 0, 0))
    lspec = pl.BlockSpec((g, blk, 1), lambda kv, i: (kv, i, 0))
    return pl.pallas_call(
        body, name=name, grid=(ATTN_KV, T // blk),
        in_specs=[qspec, kspec, kspec, pl.BlockSpec((1, g * blk, 1), lambda kv, i: (kv, 0, 0)), qspec, lspec, qspec],
        out_specs=[qspec, kspec, kspec, pl.BlockSpec((1, g, 1), lambda kv, i: (kv, 0, 0))],
        out_shape=[jax.ShapeDtypeStruct((nh, T, hd), F32), jax.ShapeDtypeStruct((ATTN_KV, T + 2 * blk, hd), F32),
                   jax.ShapeDtypeStruct((ATTN_KV, T + 2 * blk, hd), F32), jax.ShapeDtypeStruct((ATTN_KV, g, 1), F32)],
        compiler_params=_cp("arbitrary", "arbitrary"),
    )(qt, kp, vp, sinkb, o, lse, do)


PAIR = 2 * HEAD_DIM
N_PAIRS = (ATTN_HEADS + ATTN_KV) // 2


def _lanes():
    return lax.broadcasted_iota(jnp.int32, (1, PAIR), 1)


def _swap32(v):
    first_half = (_lanes() & (HEAD_DIM // 2)) == 0
    return jnp.where(first_half, pltpu.roll(v, PAIR - HEAD_DIM // 2, 1), pltpu.roll(v, HEAD_DIM // 2, 1))


def _head_mean(v):
    r = lax.broadcasted_iota(jnp.int32, (PAIR, PAIR), 0)
    c = lax.broadcasted_iota(jnp.int32, (PAIR, PAIR), 1)
    same = jnp.where((r >= HEAD_DIM) == (c >= HEAD_DIM), 1.0, 0.0).astype(BF16)
    return _nn3r(v, same) * (1.0 / HEAD_DIM)


def _qk_tile_fwd(pa, g_ref, cosv, sinv, q_ref, k_ref, v_ref):
    qw = ATTN_HEADS * HEAD_DIM
    for p in range(N_PAIRS):
        xv = pa[:, p * PAIR:(p + 1) * PAIR]
        n = xv * lax.rsqrt(_head_mean(xv * xv) + EPS) * g_ref[p]
        y = n * cosv + _swap32(n) * sinv
        if p < N_PAIRS - 1:
            q_ref[:, p * PAIR:(p + 1) * PAIR] = _bf(y * HEAD_DIM ** -0.5)
        else:
            k_ref[...] = _bf(y)
    v_ref[...] = _bf(pa[:, qw + PAIR:])


def _qk_tile_bwd(dq_ref, dk_ref, pa_ref, g_ref, cosv, sinv):
    dxs, dgs = [], []
    for p in range(N_PAIRS):
        sl = slice(p * PAIR, (p + 1) * PAIR)
        xv = pa_ref[:, sl]
        r = lax.rsqrt(_head_mean(xv * xv) + EPS)
        xhat = xv * r
        dy = dq_ref[:, sl] * HEAD_DIM ** -0.5 if p < N_PAIRS - 1 else dk_ref[...]
        dn = dy * cosv + _swap32(dy * sinv)
        dgs.append(jnp.sum(dn * xhat, axis=0, keepdims=True))
        dxh = dn * g_ref[p]
        dxs.append(r * (dxh - xhat * _head_mean(dxh * xhat)))
    return jnp.concatenate(dxs, axis=1), dgs


def _qk_slab_fwd(pa, gains, cosp, sinp, tm, name):
    T = pa.shape[0]
    qw = ATTN_HEADS * HEAD_DIM

    def body(pa_ref, g_ref, c_ref, s_ref, q_ref, k_ref, v_ref):
        cosv, sinv = c_ref[...], s_ref[...]
        for p in range(N_PAIRS):
            xv = pa_ref[:, p * PAIR:(p + 1) * PAIR]
            n = xv * lax.rsqrt(_head_mean(xv * xv) + EPS) * g_ref[p]
            y = n * cosv + _swap32(n) * sinv
            if p < N_PAIRS - 1:
                q_ref[:, p * PAIR:(p + 1) * PAIR] = _bf(y * HEAD_DIM ** -0.5)
            else:
                k_ref[...] = _bf(y)
        v_ref[...] = _bf(pa_ref[:, qw + PAIR:])

    return pl.pallas_call(
        body, name=name, grid=(T // tm,),
        in_specs=[_rows(tm, pa.shape[1]), _full(gains.shape), _rows(tm, PAIR), _rows(tm, PAIR)],
        out_specs=[_rows(tm, qw), _rows(tm, PAIR), _rows(tm, PAIR)],
        out_shape=[jax.ShapeDtypeStruct((T, qw), ACT), jax.ShapeDtypeStruct((T, PAIR), ACT),
                   jax.ShapeDtypeStruct((T, PAIR), ACT)],
        compiler_params=_cp("arbitrary"),
    )(pa, gains, cosp, sinp)


def _qk_slab_bwd(dq, dk, pa, gains, cosp, sinp, tm, name):
    T = pa.shape[0]
    qw = ATTN_HEADS * HEAD_DIM

    def body(dq_ref, dk_ref, pa_ref, g_ref, c_ref, s_ref, dx_ref, dg_ref):
        i = pl.program_id(0)
        cosv, sinv = c_ref[...], s_ref[...]
        for p in range(N_PAIRS):
            sl = slice(p * PAIR, (p + 1) * PAIR)
            xv = pa_ref[:, sl]
            r = lax.rsqrt(_head_mean(xv * xv) + EPS)
            xhat = xv * r
            dy = dq_ref[:, sl] * HEAD_DIM ** -0.5 if p < N_PAIRS - 1 else dk_ref[...]
            dn = dy * cosv + _swap32(dy * sinv)
            _acc_all(dg_ref.at[p], i, jnp.sum(dn * xhat, axis=0, keepdims=True))
            dxh = dn * g_ref[p]
            dx_ref[:, sl] = r * (dxh - xhat * _head_mean(dxh * xhat))

    return pl.pallas_call(
        body, name=name, grid=(T // tm,),
        in_specs=[_rows(tm, qw), _rows(tm, PAIR), _rows(tm, qw + PAIR), _full(gains.shape), _rows(tm, PAIR), _rows(tm, PAIR)],
        out_specs=[_rows(tm, qw + PAIR), _whole(gains.shape)],
        out_shape=[jax.ShapeDtypeStruct((T, qw + PAIR), F32), jax.ShapeDtypeStruct(gains.shape, F32)],
        compiler_params=_cp("arbitrary"),
    )(dq, dk, pa, gains, cosp, sinp)


def _attn_window(ref, i, nb):
    blk = ATTN_BLOCK
    starts = [pl.multiple_of(jnp.clip(i + d, 0, nb - 1) * blk, blk) for d in (-1, 0, 1)]
    return starts, jnp.concatenate([ref[pl.ds(s, blk), :] for s in starts], axis=0)


GROUP_HEADS = 2
ATTN_STEP_BLOCKS = 2


def _head_groups(n):
    g = ATTN_HEADS // ATTN_KV
    return [(kv, [kv * g + s + j for j in range(n)]) for kv in range(ATTN_KV) for s in range(0, g, n)]


def _attn_mask(i, lc, T, rows):
    blk = ATTN_BLOCK
    row = lax.broadcasted_iota(jnp.int32, (rows, 1), 0)
    qpos = i * blk + (row & (blk - 1))
    kpos = (i - 1) * blk + lax.broadcasted_iota(jnp.int32, (1, 3 * blk), 1)
    return (qpos >= lc) & (kpos >= lc) & (kpos < T) & (jnp.abs(kpos - qpos) <= WINDOW)


def _to_kv_half(v, head, kv):
    return v if head % 2 == kv else pltpu.roll(v, HEAD_DIM, 1)


def _attn_slab_fwd(qt, ks, vs, sinkb, lc, name, ex=None):
    T = qt.shape[0]
    blk = ATTN_BLOCK
    nb = T // blk
    g = ATTN_HEADS // ATTN_KV

    spb = ATTN_STEP_BLOCKS
    ng = nb // spb

    def one_block(i, rows, q_ref, k_ref, v_ref, sink_ref, o_ref, lse_ref):
        lane = _lanes()
        valid = _attn_mask(i, lc, T, GROUP_HEADS * blk)
        kc_all, vc = k_ref[0:lc, :], v_ref[0:lc, :]
        _, kw_all = _attn_window(k_ref, i, nb)
        _, vw = _attn_window(v_ref, i, nb)
        kc, kw = [], []
        for kv in range(ATTN_KV):
            mine = (lane >= kv * HEAD_DIM) & (lane < (kv + 1) * HEAD_DIM)
            kc.append(jnp.where(mine, kc_all, jnp.zeros_like(kc_all)))
            kw.append(jnp.where(mine, kw_all, jnp.zeros_like(kw_all)))
        groups = _head_groups(GROUP_HEADS)
        qg = [jnp.concatenate([_to_kv_half(q_ref[rows, (h // 2) * PAIR:(h // 2 + 1) * PAIR], h, kv) for h in heads], axis=0)
              for kv, heads in groups]
        sinks = [sink_ref[kv, (heads[0] - kv * g) * blk:(heads[-1] + 1 - kv * g) * blk] for kv, heads in groups]
        s_c = [_nt(q, kc[kv]) for q, (kv, _) in zip(qg, groups)]
        s_w = [jnp.where(valid, _nt(q, kw[kv]), NEG) for q, (kv, _) in zip(qg, groups)]
        m = [jnp.maximum(jnp.maximum(jnp.max(a, axis=-1, keepdims=True), jnp.max(b, axis=-1, keepdims=True)), s)
             for a, b, s in zip(s_c, s_w, sinks)]
        e_c = [jnp.exp(a - mm) for a, mm in zip(s_c, m)]
        e_w = [jnp.exp(b - mm) for b, mm in zip(s_w, m)]
        den = [jnp.exp(s - mm) + jnp.sum(a, axis=-1, keepdims=True) + jnp.sum(b, axis=-1, keepdims=True)
               for s, mm, a, b in zip(sinks, m, e_c, e_w)]
        inv = [1.0 / d for d in den]
        og = [_nn(_bf(a * r), vc) + _nn(_bf(b * r), vw) for a, b, r in zip(e_c, e_w, inv)]
        placed = [None] * ATTN_HEADS
        for (kv, heads), o2, mm, d in zip(groups, og, m, den):
            lse_ref[heads[0]:heads[-1] + 1, rows, :] = (mm + jnp.log(d)).reshape(len(heads), blk, 1)
            for j, h in enumerate(heads):
                placed[h] = _to_kv_half(o2[j * blk:(j + 1) * blk], h, kv)
        for p in range(ATTN_HEADS // 2):
            o_ref[rows, p * PAIR:(p + 1) * PAIR] = jnp.where(lane < HEAD_DIM, placed[2 * p], placed[2 * p + 1])

    def body(*refs):
        for j in range(spb):
            one_block(pl.program_id(0) * spb + j, pl.ds(j * blk, blk), *refs)

    qw = ATTN_HEADS * HEAD_DIM
    return _host_call(
        body, ex, lambda: pl.program_id(0) == 0, lambda: pl.program_id(0) == ng - 1,
        name=name, grid=(ng,),
        in_specs=[_rows(spb * blk, qw), _full((T, PAIR)), _full((T, PAIR)), _full(sinkb.shape)],
        out_specs=[_rows(spb * blk, qw), pl.BlockSpec((ATTN_HEADS, spb * blk, 1), lambda i: (0, i, 0))],
        out_shape=[jax.ShapeDtypeStruct((T, qw), F32), jax.ShapeDtypeStruct((ATTN_HEADS, T, 1), F32)],
        scratch_shapes=[], sem=("arbitrary",), args=(qt, ks, vs, sinkb))


def _attn_slab_bwd(qt, ks, vs, sinkb, o, lse, do, lc, name, ex=None):
    T = qt.shape[0]
    blk = ATTN_BLOCK
    nb = T // blk
    g = ATTN_HEADS // ATTN_KV

    spb = ATTN_STEP_BLOCKS
    ng = nb // spb

    def body(*refs):
        dk_ref, dv_ref, ds_ref = refs[8:11]

        @pl.when(pl.program_id(0) == 0)
        def _():
            dk_ref[...] = jnp.zeros_like(dk_ref)
            dv_ref[...] = jnp.zeros_like(dv_ref)
            ds_ref[...] = jnp.zeros_like(ds_ref)

        for j in range(spb):
            one_block(pl.program_id(0) * spb + j, pl.ds(j * blk, blk), *refs)

    def one_block(i, rows, q_ref, k_ref, v_ref, sink_ref, o_ref, lse_ref, do_ref, dq_ref, dk_ref, dv_ref, ds_ref):
        lane = _lanes()
        valid = _attn_mask(i, lc, T, g * blk)
        kc_all, vc_all = k_ref[0:lc, :], v_ref[0:lc, :]
        starts, kw_all = _attn_window(k_ref, i, nb)
        _, vw_all = _attn_window(v_ref, i, nb)
        dq_pairs = [jnp.zeros((blk, PAIR), F32) for _ in range(ATTN_HEADS // 2)]
        for kv in range(ATTN_KV):
            mine = (lane >= kv * HEAD_DIM) & (lane < (kv + 1) * HEAD_DIM)

            def only(v):
                return jnp.where(mine, v, jnp.zeros_like(v))

            kc, kw, vc, vw = only(kc_all), only(kw_all), only(vc_all), only(vw_all)
            heads = [kv * g + j for j in range(g)]
            qs, dos, deltas = [], [], []
            for h in heads:
                sl = slice((h // 2) * PAIR, (h // 2 + 1) * PAIR)
                dov = do_ref[rows, sl]
                qs.append(_to_kv_half(q_ref[rows, sl], h, kv))
                dos.append(_bf(_to_kv_half(dov, h, kv)))
                own = (lane < HEAD_DIM) if h % 2 == 0 else (lane >= HEAD_DIM)
                deltas.append(jnp.sum(jnp.where(own, dov * o_ref[rows, sl], 0.0), axis=-1, keepdims=True))
            q4, do4, delta = jnp.concatenate(qs, axis=0), jnp.concatenate(dos, axis=0), jnp.concatenate(deltas, axis=0)
            sink = sink_ref[kv]
            lse = lse_ref[kv * g:(kv + 1) * g, rows, :].reshape(g * blk, 1)
            p_c = jnp.exp(_nt(q4, kc) - lse)
            p_w = jnp.exp(jnp.where(valid, _nt(q4, kw), NEG) - lse)
            ds_c = _bf(p_c * (_nt(do4, vc) - delta))
            ds_w = _bf(p_w * (_nt(do4, vw) - delta))
            dsr = -jnp.exp(sink - lse) * delta
            dq4 = _nn(ds_c, kc) + _nn(ds_w, kw)
            for j, h in enumerate(heads):
                ds_ref[h:h + 1, :] += jnp.sum(dsr[j * blk:(j + 1) * blk, :], axis=0, keepdims=True)
                dq_pairs[h // 2] = dq_pairs[h // 2] + _to_kv_half(dq4[j * blk:(j + 1) * blk], h, kv)
            dk_ref[0:lc, :] += only(_tn(ds_c, q4))
            dv_ref[0:lc, :] += only(_tn(_bf(p_c), do4))
            dkw = only(_tn(ds_w, q4))
            dvw = only(_tn(_bf(p_w), do4))
            for b, s in enumerate(starts):
                dk_ref[pl.ds(s, blk), :] += dkw[b * blk:(b + 1) * blk]
                dv_ref[pl.ds(s, blk), :] += dvw[b * blk:(b + 1) * blk]
        for p in range(ATTN_HEADS // 2):
            dq_ref[rows, p * PAIR:(p + 1) * PAIR] = dq_pairs[p]

    qw = ATTN_HEADS * HEAD_DIM
    lspec = pl.BlockSpec((ATTN_HEADS, spb * blk, 1), lambda i: (0, i, 0))
    return _host_call(
        body, ex, lambda: pl.program_id(0) == 0, lambda: pl.program_id(0) == ng - 1,
        name=name, grid=(ng,),
        in_specs=[_rows(spb * blk, qw), _full((T, PAIR)), _full((T, PAIR)), _full(sinkb.shape), _rows(spb * blk, qw), lspec,
                  _rows(spb * blk, qw)],
        out_specs=[_rows(spb * blk, qw), _whole((T, PAIR)), _whole((T, PAIR)), _whole((ATTN_HEADS, 1))],
        out_shape=[jax.ShapeDtypeStruct((T, qw), F32), jax.ShapeDtypeStruct((T, PAIR), F32),
                   jax.ShapeDtypeStruct((T, PAIR), F32), jax.ShapeDtypeStruct((ATTN_HEADS, 1), F32)],
        scratch_shapes=[], sem=("arbitrary",), args=(qt, ks, vs, sinkb, o, lse, do))


def _fw_chunk(s, nc, nt):
    return s


def _bw_chunk(s, nc, nt):
    return jnp.where(s < nc, nc - 1 - s, nt - 1 - (s - nc))


def _tri(c, rev):
    r = lax.broadcasted_iota(jnp.int32, (c, c), 0)
    k = lax.broadcasted_iota(jnp.int32, (c, c), 1)
    return (k >= r) if rev else (k <= r)


def _gla_gates(z, lb, rev):
    c = HG_CHUNK
    sg = _sig(z)
    f = lb + (1.0 - lb) * sg
    cum = _nn3(jnp.where(_tri(c, rev), 1.0, 0.0).astype(BF16), jnp.log(f))
    mid = c - 1 - c // 2 if rev else c // 2
    last = 0 if rev else c - 1
    return sg, f, cum, cum[mid:mid + 1], cum[last:last + 1], last


def _lower_bound(lbraw_ref):
    lr = lbraw_ref[...]
    return _sig(lr[0:1] - lr[1:2])


def _gla_fwd(pb, lbraw, lc, name, ex=None):
    T = pb.shape[0]
    c, hw, d, ns = HG_CHUNK, HG_HEADS * HG_D, HG_D, HG_STEP_CHUNKS
    nt, nc = T // (ns * c), lc // (ns * c)
    orders = (_fw_chunk, _bw_chunk)

    def body(qf, zf, vf, qb, zb, vb, lb_ref, of_ref, ob_ref, sf_ref, sb_ref, st_ref):
        @pl.when(pl.program_id(0) == 0)
        def _():
            st_ref[...] = jnp.zeros_like(st_ref)

        lb = _lower_bound(lb_ref)
        dirs = ((qf, zf, vf, of_ref, sf_ref), (qb, zb, vb, ob_ref, sb_ref))
        combos = [(dr, h, slice(h * d, (h + 1) * d)) for dr in range(2) for h in range(HG_HEADS)]
        for j in range(ns):
            sub = (j, ns - 1 - j)
            rows = [pl.ds(sub[dr] * c, c) for dr in range(2)]
            prep = []
            for dr, (q_ref, z_ref, v_ref, _, _) in enumerate(dirs):
                rev = dr == 1
                qr = q_ref[rows[dr], :]
                q = qr * _sig(qr)
                _, f, cum, ref, last, _ = _gla_gates(z_ref[rows[dr], :], lb, rev)
                k = 1.0 - f
                prep.append(dict(q1=_bf(q * jnp.exp(cum - ref)), k1=_bf(k * jnp.exp(ref - cum)), q2=_bf(q * jnp.exp(cum)),
                                 k2=_bf(k * jnp.exp(last - cum)), el=jnp.exp(last), v=_bf(v_ref[rows[dr], :]),
                                 mask=_tri(c, rev)))
            a = [_bf(jnp.where(prep[dr]["mask"], _nt(prep[dr]["q1"][:, sl], prep[dr]["k1"][:, sl]), 0.0))
                 for dr, _, sl in combos]
            for (dr, h, sl), a_h in zip(combos, a):
                p = prep[dr]
                o_ref, s_ref = dirs[dr][3], dirs[dr][4]
                st = st_ref[dr, h]
                stb = _bf(st)
                s_ref[sub[dr], h] = stb
                o_ref[rows[dr], sl] = _nn(a_h, p["v"][:, sl]) + _nt(p["q2"][:, sl], stb)
                st_ref[dr, h] = st * p["el"][:, sl] + _tn(p["v"][:, sl], p["k2"][:, sl])

    def col(order, blkcol):
        return pl.BlockSpec((ns * c, hw), lambda s: (order(s, nc, nt), blkcol))

    def st_spec(order):
        return pl.BlockSpec((ns, HG_HEADS, d, d), lambda s: (order(s, nc, nt), 0, 0, 0))

    in_specs = []
    for dr, order in enumerate(orders):
        in_specs += [col(order, 0), col(order, 1 + dr), col(order, 3)]
    in_specs.append(_full(lbraw.shape))
    return _host_call(
        body, ex, lambda: pl.program_id(0) == 0, lambda: pl.program_id(0) == nt - 1,
        name=name, grid=(nt,), in_specs=in_specs,
        out_specs=[col(_fw_chunk, 0), col(_bw_chunk, 0), st_spec(_fw_chunk), st_spec(_bw_chunk)],
        out_shape=[jax.ShapeDtypeStruct((T, hw), F32), jax.ShapeDtypeStruct((T, hw), F32),
                   jax.ShapeDtypeStruct((nt * ns, HG_HEADS, d, d), ACT), jax.ShapeDtypeStruct((nt * ns, HG_HEADS, d, d), ACT)],
        scratch_shapes=[pltpu.VMEM((2, HG_HEADS, d, d), F32)], sem=("arbitrary",),
        args=(pb, pb, pb, pb, pb, pb, lbraw))


def _gla_bwd(pb, lbraw, s_fw, s_bw, do, lc, name, ex=None):
    T = pb.shape[0]
    c, hw, d, ns = HG_CHUNK, HG_HEADS * HG_D, HG_D, HG_STEP_CHUNKS
    nt, nc = T // (ns * c), lc // (ns * c)

    def rfw(s, nc_, nt_):
        return _fw_chunk(nt_ - 1 - s, nc_, nt_)

    def rbw(s, nc_, nt_):
        return _bw_chunk(nt_ - 1 - s, nc_, nt_)

    def body(qf, zf, vf, sf, dof, qb, zb, vb, sb, dob_, lb_ref,
             dqf, dzf, dvf, dqb, dzb, dvb, dlb_ref, dst_ref):
        step = pl.program_id(0)

        @pl.when(step == 0)
        def _():
            dst_ref[...] = jnp.zeros_like(dst_ref)

        lb = _lower_bound(lb_ref)
        sets = ((qf, zf, vf, sf, dof, dqf, dzf, dvf), (qb, zb, vb, sb, dob_, dqb, dzb, dvb))
        combos = [(dr, h, slice(h * d, (h + 1) * d)) for dr in range(2) for h in range(HG_HEADS)]
        dlb_tot = jnp.zeros((1, hw), F32)
        for j in range(ns):
            sub = (ns - 1 - j, j)
            rows = [pl.ds(sub[dr] * c, c) for dr in range(2)]
            prep = []
            for dr, (q_ref, z_ref, v_ref, _, do_ref, _, _, _) in enumerate(sets):
                rev = dr == 1
                qr = q_ref[rows[dr], :]
                sq = _sig(qr)
                q = qr * sq
                sg, f, cum, ref, last, last_row = _gla_gates(z_ref[rows[dr], :], lb, rev)
                k = 1.0 - f
                e_qr, e_kr, e_q, e_kl = jnp.exp(cum - ref), jnp.exp(ref - cum), jnp.exp(cum), jnp.exp(last - cum)
                q1, k1, q2, k2 = q * e_qr, k * e_kr, q * e_q, k * e_kl
                prep.append(dict(qr=qr, sq=sq, sg=sg, f=f, e_qr=e_qr, e_kr=e_kr, e_q=e_q, e_kl=e_kl, el=jnp.exp(last),
                                 q1=q1, k1=k1, q2=q2, k2=k2, q1b=_bf(q1), k1b=_bf(k1), q2b=_bf(q2), k2b=_bf(k2),
                                 vb=_bf(v_ref[rows[dr], :]), dob=_bf(do_ref[rows[dr], :]), mask=_tri(c, rev),
                                 last_row=last_row, acc_t=jnp.where(_tri(c, not rev), 1.0, 0.0).astype(BF16)))
            a = [_bf(jnp.where(prep[dr]["mask"], _nt(prep[dr]["q1b"][:, sl], prep[dr]["k1b"][:, sl]), 0.0))
                 for dr, _, sl in combos]
            da = [_bf(jnp.where(prep[dr]["mask"], _nt(prep[dr]["dob"][:, sl], prep[dr]["vb"][:, sl]), 0.0))
                  for dr, _, sl in combos]
            parts = [dict(dq1=[], dk1=[], dq2=[], dk2=[], dls=[]) for _ in range(2)]
            for (dr, h, sl), a_h, da_h in zip(combos, a, da):
                p = prep[dr]
                s_ref, dv_ref = sets[dr][3], sets[dr][7]
                stb = s_ref[sub[dr], h]
                dst = dst_ref[dr, h]
                dstb = _bf(dst)
                dob_h, vb_h = p["dob"][:, sl], p["vb"][:, sl]
                dv_ref[rows[dr], sl] = _bf(_tn(a_h, dob_h) + _nt(p["k2b"][:, sl], dstb))
                parts[dr]["dq1"].append(_nn(da_h, p["k1b"][:, sl]))
                parts[dr]["dk1"].append(_tn(da_h, p["q1b"][:, sl]))
                parts[dr]["dq2"].append(_nn(dob_h, stb))
                parts[dr]["dk2"].append(_nn(vb_h, dstb))
                el_h = p["el"][:, sl]
                dst_ref[dr, h] = _tn(dob_h, p["q2b"][:, sl]) + dst * el_h
                parts[dr]["dls"].append(jnp.sum(dst * stb.astype(F32), axis=0, keepdims=True) * el_h)
            for dr in range(2):
                p = prep[dr]
                dq_ref, dz_ref = sets[dr][5], sets[dr][6]
                dq1, dk1, dq2, dk2, dls = (jnp.concatenate(parts[dr][n], axis=1) for n in ("dq1", "dk1", "dq2", "dk2", "dls"))
                dq = dq1 * p["e_qr"] + dq2 * p["e_q"]
                dk = dk1 * p["e_kr"] + dk2 * p["e_kl"]
                dcum = dq1 * p["q1"] - dk1 * p["k1"] + dq2 * p["q2"] - dk2 * p["k2"]
                dlast = jnp.sum(dk2 * p["k2"], axis=0, keepdims=True) + dls
                rowid = lax.broadcasted_iota(jnp.int32, (c, 1), 0)
                dcum = dcum + jnp.where(rowid == p["last_row"], dlast, 0.0)
                df = _nn3(p["acc_t"], dcum) / p["f"] - dk
                sg = p["sg"]
                dz_ref[rows[dr], :] = _bf(df * (1.0 - lb) * sg * (1.0 - sg))
                dlb_tot = dlb_tot + jnp.sum(df * (1.0 - sg), axis=0, keepdims=True)
                dq_ref[rows[dr], :] = _bf(dq * (p["sq"] * (1.0 + p["qr"] * (1.0 - p["sq"]))))
        _acc_all(dlb_ref, step, dlb_tot)

    def col(order, blkcol):
        return pl.BlockSpec((ns * c, hw), lambda s: (order(s, nc, nt), blkcol))

    def st_spec(order):
        return pl.BlockSpec((ns, HG_HEADS, d, d), lambda s: (order(s, nc, nt), 0, 0, 0))

    in_specs = []
    for dr, order in enumerate((rfw, rbw)):
        in_specs += [col(order, 0), col(order, 1 + dr), col(order, 3), st_spec(order), col(order, 0)]
    in_specs.append(_full(lbraw.shape))
    out_specs = [col(rfw, 0)] * 3 + [col(rbw, 0)] * 3 + [_whole((1, hw))]
    out_shape = [jax.ShapeDtypeStruct((T, hw), ACT)] * 6 + [jax.ShapeDtypeStruct((1, hw), F32)]
    return _host_call(
        body, ex, lambda: pl.program_id(0) == 0, lambda: pl.program_id(0) == nt - 1,
        name=name, grid=(nt,), in_specs=in_specs, out_specs=out_specs, out_shape=out_shape,
        scratch_shapes=[pltpu.VMEM((2, HG_HEADS, d, d), F32)], sem=("arbitrary",),
        args=(pb, pb, pb, s_fw, do, pb, pb, pb, s_bw, do, lbraw))


def _ret_log_gamma(h, rev):
    hh = RET_HEADS - 1 - h if rev else h
    return math.log(1.0 - 2.0 ** (-5.0 - hh))


def _rope(x, cos, sin):
    half = x.shape[1] // 2
    x1, x2 = x[:, :half], x[:, half:]
    return jnp.concatenate([x1 * cos - x2 * sin, x2 * cos + x1 * sin], axis=1)


def _unrope(dy, cos, sin):
    half = dy.shape[1] // 2
    d1, d2 = dy[:, :half], dy[:, half:]
    return jnp.concatenate([d1 * cos + d2 * sin, d2 * cos - d1 * sin], axis=1)


def _ret_decays(lg, rev):
    c = RET_CHUNK
    r = lax.broadcasted_iota(jnp.int32, (c, c), 0)
    k = lax.broadcasted_iota(jnp.int32, (c, c), 1)
    rel = (k - r) if rev else (r - k)
    dm = jnp.where(rel >= 0, jnp.exp(lg * jnp.maximum(rel, 0).astype(F32)), 0.0)
    pos = lax.broadcasted_iota(jnp.int32, (c, 1), 0).astype(F32)
    if rev:
        qdec = jnp.exp(lg * (c - pos))
        kdec = jnp.exp(lg * pos)
    else:
        qdec = jnp.exp(lg * (pos + 1.0))
        kdec = jnp.exp(lg * (c - 1.0 - pos))
    return dm, qdec, kdec


def _ret_fwd(q, k, v, cos, sin, lc, name, ex=None):
    T = q.shape[0]
    c, dk, dv = RET_CHUNK, RET_DK, RET_DV
    nt, nc = T // c, lc // c
    kscale = dk ** -0.5

    def body(qf, kf, vf, cf, sf_, qb, kb, vb, cb, sb_, of_ref, ob_ref, stf_ref, stb_ref, st_ref):
        @pl.when(pl.program_id(0) == 0)
        def _():
            st_ref[...] = jnp.zeros_like(st_ref)

        sets = ((qf, kf, vf, cf, sf_, of_ref, stf_ref), (qb, kb, vb, cb, sb_, ob_ref, stb_ref))
        combos = [(dr, h) for dr in range(2) for h in range(RET_HEADS)]
        prep = {}
        for dr, (q_ref, k_ref, v_ref, c_ref, s_ref, _, _) in enumerate(sets):
            rev = dr == 1
            cos_v, sin_v = c_ref[...], s_ref[...]
            for h in range(RET_HEADS):
                lg = _ret_log_gamma(h, rev)
                dm, qdec, kdec = _ret_decays(lg, rev)
                qh = _rope(q_ref[:, h * dk:(h + 1) * dk].astype(F32), cos_v, sin_v)
                kh = _rope(k_ref[:, h * dk:(h + 1) * dk].astype(F32), cos_v, sin_v) * kscale
                prep[dr, h] = dict(qb=_bf(qh), kb=_bf(kh), qin=_bf(qh * qdec), kin=_bf(kh * kdec),
                                   v=_bf(v_ref[:, h * dv:(h + 1) * dv]), dm=dm, decay=math.exp(lg * c))
        sc = {ch: _bf(_nt(prep[ch]["qb"], prep[ch]["kb"]) * prep[ch]["dm"]) for ch in combos}
        for dr, h in combos:
            p = prep[dr, h]
            o_ref, so_ref = sets[dr][5], sets[dr][6]
            st = st_ref[dr, h]
            stb = _bf(st)
            so_ref[0, h] = stb
            o_ref[:, h * dv:(h + 1) * dv] = _bf(_nn(sc[dr, h], p["v"]) + _nt(p["qin"], stb))
            st_ref[dr, h] = st * p["decay"] + _tn(p["v"], p["kin"])

    def spec(order, width):
        return pl.BlockSpec((c, width), lambda s: (order(s, nc, nt), 0))

    def st_spec(order):
        return pl.BlockSpec((1, RET_HEADS, dv, dk), lambda s: (order(s, nc, nt), 0, 0, 0))

    in_specs = []
    for order in (_fw_chunk, _bw_chunk):
        in_specs += [spec(order, RET_HEADS * dk), spec(order, RET_HEADS * dk), spec(order, RET_HEADS * dv),
                     spec(order, dk // 2), spec(order, dk // 2)]
    return _host_call(
        body, ex, lambda: pl.program_id(0) == 0, lambda: pl.program_id(0) == nt - 1,
        name=name, grid=(nt,), in_specs=in_specs,
        out_specs=[spec(_fw_chunk, RET_HEADS * dv), spec(_bw_chunk, RET_HEADS * dv), st_spec(_fw_chunk), st_spec(_bw_chunk)],
        out_shape=[jax.ShapeDtypeStruct((T, RET_HEADS * dv), ACT), jax.ShapeDtypeStruct((T, RET_HEADS * dv), ACT),
                   jax.ShapeDtypeStruct((nt, RET_HEADS, dv, dk), ACT), jax.ShapeDtypeStruct((nt, RET_HEADS, dv, dk), ACT)],
        scratch_shapes=[pltpu.VMEM((2, RET_HEADS, dv, dk), F32)], sem=("arbitrary",),
        args=(q, k, v, cos, sin, q, k, v, cos, sin))


def _ret_bwd(q, k, v, cos, sin, s_fw, s_bw, do, lc, name, ex=None):
    T = q.shape[0]
    c, dk, dv = RET_CHUNK, RET_DK, RET_DV
    nt, nc = T // c, lc // c
    kscale = dk ** -0.5

    def rfw(s, nc_, nt_):
        return _fw_chunk(nt_ - 1 - s, nc_, nt_)

    def rbw(s, nc_, nt_):
        return _bw_chunk(nt_ - 1 - s, nc_, nt_)

    def body(qf, kf, vf, cf, sf_, stf, dof, qb, kb, vb, cb, sb_, stb_, dob_,
             dqf, dkf, dvf, dqb, dkb, dvb, dst_ref):
        @pl.when(pl.program_id(0) == 0)
        def _():
            dst_ref[...] = jnp.zeros_like(dst_ref)

        sets = ((qf, kf, vf, cf, sf_, stf, dof, dqf, dkf, dvf), (qb, kb, vb, cb, sb_, stb_, dob_, dqb, dkb, dvb))
        combos = [(dr, h) for dr in range(2) for h in range(RET_HEADS)]
        prep = {}
        for dr, (q_ref, k_ref, v_ref, c_ref, s_ref, _, do_ref, _, _, _) in enumerate(sets):
            rev = dr == 1
            cos_v, sin_v = c_ref[...], s_ref[...]
            for h in range(RET_HEADS):
                lg = _ret_log_gamma(h, rev)
                dm, qdec, kdec = _ret_decays(lg, rev)
                qh = _rope(q_ref[:, h * dk:(h + 1) * dk].astype(F32), cos_v, sin_v)
                kh = _rope(k_ref[:, h * dk:(h + 1) * dk].astype(F32), cos_v, sin_v) * kscale
                prep[dr, h] = dict(qb=_bf(qh), kb=_bf(kh), qin=_bf(qh * qdec), kin=_bf(kh * kdec),
                                   v=_bf(v_ref[:, h * dv:(h + 1) * dv]), dob=_bf(do_ref[:, h * dv:(h + 1) * dv]),
                                   dm=dm, qdec=qdec, kdec=kdec, decay=math.exp(lg * c), cos=cos_v, sin=sin_v)
        sc = {ch: _bf(_nt(prep[ch]["qb"], prep[ch]["kb"]) * prep[ch]["dm"]) for ch in combos}
        dsc = {ch: _bf(_nt(prep[ch]["dob"], prep[ch]["v"]) * prep[ch]["dm"]) for ch in combos}
        carried = {}
        for dr, h in combos:
            p = prep[dr, h]
            dv_ref = sets[dr][9]
            dst = dst_ref[dr, h]
            dstb = _bf(dst)
            carried[dr, h] = dstb
            dv_ref[:, h * dv:(h + 1) * dv] = _bf(_tn(sc[dr, h], p["dob"]) + _nt(p["kin"], dstb))
            dst_ref[dr, h] = _tn(p["dob"], p["qin"]) + dst * p["decay"]
        for dr, h in combos:
            p = prep[dr, h]
            st_in, dq_ref, dk_ref = sets[dr][5], sets[dr][7], sets[dr][8]
            dq_r = _nn(dsc[dr, h], p["kb"]) + _nn(p["dob"], st_in[0, h]) * p["qdec"]
            dk_r = _tn(dsc[dr, h], p["qb"]) + _nn(p["v"], carried[dr, h]) * p["kdec"]
            dq_ref[:, h * dk:(h + 1) * dk] = _bf(_unrope(dq_r, p["cos"], p["sin"]))
            dk_ref[:, h * dk:(h + 1) * dk] = _bf(_unrope(dk_r * kscale, p["cos"], p["sin"]))

    def spec(order, width):
        return pl.BlockSpec((c, width), lambda s: (order(s, nc, nt), 0))

    def st_spec(order):
        return pl.BlockSpec((1, RET_HEADS, dv, dk), lambda s: (order(s, nc, nt), 0, 0, 0))

    in_specs = []
    for order in (rfw, rbw):
        in_specs += [spec(order, RET_HEADS * dk), spec(order, RET_HEADS * dk), spec(order, RET_HEADS * dv),
                     spec(order, dk // 2), spec(order, dk // 2), st_spec(order), spec(order, RET_HEADS * dv)]
    out_specs, out_shape = [], []
    for order in (rfw, rbw):
        out_specs += [spec(order, RET_HEADS * dk), spec(order, RET_HEADS * dk), spec(order, RET_HEADS * dv)]
        out_shape += [jax.ShapeDtypeStruct((T, RET_HEADS * dk), ACT), jax.ShapeDtypeStruct((T, RET_HEADS * dk), ACT),
                      jax.ShapeDtypeStruct((T, RET_HEADS * dv), ACT)]
    return _host_call(
        body, ex, lambda: pl.program_id(0) == 0, lambda: pl.program_id(0) == nt - 1,
        name=name, grid=(nt,), in_specs=in_specs, out_specs=out_specs, out_shape=out_shape,
        scratch_shapes=[pltpu.VMEM((2, RET_HEADS, dv, dk), F32)], sem=("arbitrary",),
        args=(q, k, v, cos, sin, s_fw, do, q, k, v, cos, sin, s_bw, do))


def _trig_rows(lc, ang):
    ang = ang.astype(np.float64)
    half = ang.shape[1]
    cos = np.concatenate([np.ones((lc, half)), np.cos(ang)], axis=0).astype(np.float32)
    sin = np.concatenate([np.zeros((lc, half)), np.sin(ang)], axis=0).astype(np.float32)
    return cos, sin


def _attn_rope_tables(lc, l):
    t = np.arange(l)
    row = (t // GRID_W).astype(np.float32)
    colp = (t % GRID_W).astype(np.float32)
    n_freq = HEAD_DIM // 4
    inv = np.float32(10000.0) ** (-np.arange(n_freq, dtype=np.float32) / np.float32(n_freq))
    ang = np.concatenate([row[:, None] * inv, colp[:, None] * inv], axis=-1)
    cos, sin = _trig_rows(lc, ang)
    return jnp.asarray(np.concatenate([cos, cos], axis=1)), jnp.asarray(np.concatenate([-sin, sin], axis=1))


def _ret_rope_tables(lc, l):
    theta = np.float32(1.0) / (np.float32(10000.0) ** np.linspace(0.0, 1.0, RET_DK // 2, dtype=np.float32))
    ang = np.arange(l, dtype=np.float32)[:, None] * theta
    cos, sin = _trig_rows(lc, ang)
    return jnp.asarray(cos), jnp.asarray(sin)


def _heads_major(slab, n_heads):
    t = slab.shape[0]
    return slab.reshape(t, n_heads, HEAD_DIM).transpose(1, 0, 2)


def _slab(hm):
    nh, t, hd = hm.shape
    return hm.transpose(1, 0, 2).reshape(t, nh * hd)


COL_SHARDED = ("ffn_in0", "ffn_in1", "even_in", "even_in_a", "even_in_b", "odd_in")


def _full_weight(name, g):
    if name in COL_SHARDED:
        return g.transpose(1, 0, 2).reshape(g.shape[1], -1)
    return g.reshape(-1, g.shape[2])


def _shard_slots(name, g):
    if name in COL_SHARDED:
        return g.reshape(g.shape[0], N_DEV, -1).transpose(1, 0, 2)
    return g.reshape(N_DEV, -1, g.shape[1])


def _local_step(xs, target, mv, norm_g, w, qk_g, sink, hg_out_g, lbraw, lc, shards=None):
    _, T, dm = _stream(xs)
    l = T - lc
    tm = lc
    blk = ATTN_BLOCK
    d2, d3 = 2 * dm, 3 * dm
    w = dict(w)
    gw, recv = {}, {}

    def ms(layer, a, b):
        return mv[layer, :, :, a:b]

    def gather(names):
        return None if shards is None else _Exchange(GATHER2, [shards[n] for n in names])

    def arrived(names, got):
        for n, g in zip(names, got):
            w[n] = _full_weight(n, g)

    def scatter(names):
        return None if shards is None else _Exchange(SCATTER, [_shard_slots(n, gw[n]) for n in names])

    def scattered(names, got):
        for n, g in zip(names, got):
            recv[n] = g

    g00, g01, g10, g11 = (norm_g[i, j][None, :] for i in (0, 1) for j in (0, 1))

    cos2, sin2 = _attn_rope_tables(lc, l)
    cosp, sinp = jnp.concatenate([cos2, cos2], axis=1), jnp.concatenate([sin2, sin2], axis=1)
    gains5 = jnp.concatenate([jnp.broadcast_to(jnp.tile(qk_g[0], 2), (N_PAIRS - 1, PAIR)), jnp.tile(qk_g[1], 2)[None]])[:, None, :]
    riding = ["even_out"]
    (pa, pb, qt, ks, vs), got = _pre_fwd(xs, g00, ms(0, 0, d2), w["even_in"], ((0, 768), (768, 3328)), tm, "pre0_fwd",
                                         gather(riding), qk=(gains5, cosp, sinp))
    arrived(riding, got)
    sinkb = jnp.broadcast_to(sink.reshape(ATTN_KV, 4, 1, 1), (ATTN_KV, 4, blk, 1)).reshape(ATTN_KV, 4 * blk, 1)
    riding = ["ffn_in0"]
    (a_slab, lse), got = _attn_slab_fwd(qt, ks, vs, sinkb, lc, "attn_fwd", gather(riding))
    arrived(riding, got)
    riding = ["ffn_out0", "odd_out"]
    (hg_of, hg_ob, hg_sf, hg_sb), got = _gla_fwd(pb, lbraw, lc, "hgrn_fwd", gather(riding))
    arrived(riding, got)
    x01, z0, yp0 = _post_fwd(xs, hg_of, hg_ob, pb, 4, hg_out_g, a_slab, w["even_out"], ms(0, d2, d3), HG_D, tm, "post0_fwd")
    riding = ["odd_in"]
    (x02, u0, f0), got = _ffn_fwd(x01, g01, ms(0, d3, 6 * dm), w["ffn_in0"], w["ffn_out0"], tm, "ffn0_fwd", ex=gather(riding))
    arrived(riding, got)

    riding = ["ffn_out1"]
    (rq, rk, rv, rg), got = _pre_fwd(x02, g10, ms(1, 0, d2), w["odd_in"],
                                     ((0, 1024), (1024, 2048), (2048, 4096), (4096, 6144)), tm, "pre1_fwd", gather(riding),
                                     out_dtype=ACT)
    arrived(riding, got)
    rcos, rsin = _ret_rope_tables(lc, l)
    riding = ["ffn_in1"]
    (rt_of, rt_ob, rt_sf, rt_sb), got = _ret_fwd(rq, rk, rv, rcos, rsin, lc, "ret_fwd", gather(riding))
    arrived(riding, got)
    x11, z1, yp1 = _post_fwd(x02, rt_of, rt_ob, rg, 0, None, None, w["odd_out"], ms(1, d2, d3), RET_DV, tm, "post1_fwd")
    (dx, u1, f1, loss), _ = _ffn_fwd(x11, g11, ms(1, d3, 6 * dm), w["ffn_in1"], w["ffn_out1"], tm, "ffn1_fwd", target)

    (dx, h, du, act, df, dms_f1, dg11), _ = _ffn_bwd(x11, dx, u1, f1, g11, ms(1, d3, 6 * dm), w["ffn_in1"], w["ffn_out1"], tm,
                                                     "ffn1_bwd")
    gw["ffn_in1"] = _wgrad(h, du, "wg_ffn_in1")
    gw["ffn_out1"] = _wgrad(act, df, "wg_ffn_out1")
    do1, dgr1, dy1, z1_t, dgate_p1, _ = _post_bwd(dx, z1, yp1, rt_of, rt_ob, rg, 0, None, w["odd_out"], ms(1, d2, d3), 0, RET_DV, tm,
                                                  "post1_bwd")
    gw["odd_out"] = _wgrad(z1_t, dy1, "wg_odd_out")
    riding = ["ffn_in1"]
    (dqf, dkf, dvf, dqb, dkb, dvb), got = _ret_bwd(rq, rk, rv, rcos, rsin, rt_sf, rt_sb, do1, lc, "ret_bwd", scatter(riding))
    scattered(riding, got)
    riding = ["odd_out", "ffn_out1"]
    (dx, h, dp, dms_p1, dg10), got = _pre_bwd(x02, dx, g10, ms(1, 0, d2), w["odd_in"],
                                              [(0, [dqf, dqb]), (1024, [dkf, dkb]), (2048, [dvf, dvb]), (4096, [dgr1])], tm,
                                              "pre1_bwd", ex=scatter(riding))
    scattered(riding, got)
    gw["odd_in"] = _wgrad(h, dp, "wg_odd_in")

    riding = ["odd_in"]
    (dx, h, du, act, df, dms_f0, dg01), got = _ffn_bwd(x01, dx, u0, f0, g01, ms(0, d3, 6 * dm), w["ffn_in0"], w["ffn_out0"], tm,
                                                       "ffn0_bwd", scatter(riding))
    scattered(riding, got)
    gw["ffn_in0"] = _wgrad(h, du, "wg_ffn_in0")
    gw["ffn_out0"] = _wgrad(act, df, "wg_ffn_out0")
    do0, dgr0, da0, dy0, z0_t, dgate_p0, d_hg_gain = _post_bwd(dx, z0, yp0, hg_of, hg_ob, pb, 4, hg_out_g, w["even_out"],
                                                              ms(0, d2, d3), 512, HG_D, tm, "post0_bwd")
    gw["even_out"] = _wgrad(z0_t, dy0, "wg_even_out")
    riding = ["ffn_in0"]
    (hq_f, hz_f, hv_f, hq_b, hz_b, hv_b, dlb), got = _gla_bwd(pb, lbraw, hg_sf, hg_sb, do0, lc, "hgrn_bwd", scatter(riding))
    scattered(riding, got)
    riding = ["even_out", "ffn_out0"]
    (dq_att, dk_att, dv_att, dsink), got = _attn_slab_bwd(qt, ks, vs, sinkb, a_slab, lse, da0, lc, "attn_bwd", scatter(riding))
    scattered(riding, got)
    pieces0 = [(640, [dv_att]), (768, [hq_f, hq_b]), (1280, [hz_f]), (1792, [hz_b]), (2304, [hv_f, hv_b]), (2816, [dgr0])]
    (dx, h, dp, dms_p0, dg00, dgain5), _ = _pre_bwd(xs, dx, g00, ms(0, 0, d2), w["even_in"], pieces0, tm, "pre0_bwd",
                                                    latent_dx=shards is not None,
                                                    qk=(dq_att, dk_att, pa, gains5, cosp, sinp))
    if shards is None:
        gw["even_in"] = _wgrad(h, dp, "wg_even_in")
    else:
        half = dm // 2
        gw["even_in_a"] = _wgrad(h, dp, "wg_even_in_a", rows=(0, half))
        gw["even_in_b"], got = _wgrad(h, dp, "wg_even_in_b", rows=(half, half), ex=scatter(["even_in_a"]))
        scattered(["even_in_a"], got)

    dmv = jnp.stack([jnp.concatenate([dms_p0, dgate_p0, dms_f0], axis=2), jnp.concatenate([dms_p1, dgate_p1, dms_f1], axis=2)])
    small = {
        "dmv": dmv,
        "norm_g": jnp.stack([jnp.stack([dg00[0], dg01[0]]), jnp.stack([dg10[0], dg11[0]])]),
        "qk_g": jnp.stack([jnp.sum(dgain5[:N_PAIRS - 1, 0].reshape(-1, HEAD_DIM), axis=0),
                           jnp.sum(dgain5[N_PAIRS - 1, 0].reshape(-1, HEAD_DIM), axis=0)]),
        "sink": dsink.reshape(ATTN_HEADS),
        "hg_out_g": d_hg_gain[0],
        "lb": dlb[0],
        "loss": loss[0, 0],
    }
    if shards is not None:
        gw = {n: recv.get(n, g) for n, g in gw.items()}
    return loss, dx, gw, small


HBM_SPEC = pl.BlockSpec(memory_space=pltpu.HBM)


def _my_index():
    return 4 * lax.axis_index("x") + 2 * lax.axis_index("y") + lax.axis_index("c")


def _peer(k):
    pos = []
    for axis, bit in (("x", 4), ("y", 2), ("c", 1)):
        a = lax.axis_index(axis)
        pos.append(1 - a if k & bit else a)
    return tuple(pos)


def _peer_index(k):
    px, py, pc = _peer(k)
    return 4 * px + 2 * py + pc


GATHER, SCATTER = "gather", "scatter"
GATHER2 = "gather over ICI once per chip"
SIBLING = 1
OTHER_CHIPS = (2, 4, 6)


class _Exchange:
    def __init__(self, mode, arrays):
        self.mode, self.arrays, self.n = mode, list(arrays), len(arrays)

    def out_shape(self):
        if self.mode in (GATHER, GATHER2):
            return [jax.ShapeDtypeStruct((N_DEV,) + a.shape, a.dtype) for a in self.arrays]
        return [jax.ShapeDtypeStruct(a.shape, a.dtype) for a in self.arrays]

    def specs(self):
        return [HBM_SPEC] * self.n

    def scratch(self):
        return [pltpu.SemaphoreType.DMA((self.n, N_DEV - 1)), pltpu.SemaphoreType.DMA((self.n, N_DEV - 1)),
                pltpu.SemaphoreType.DMA((self.n,))]

    def _copies(self, in_refs, out_refs, send_sems, recv_sems, local_sems, landing):
        me = _my_index()
        local, remote = [], []
        for a, (src, dst) in enumerate(zip(in_refs, out_refs)):
            part = (lambda j, s=src: s) if self.mode == GATHER else (lambda j, s=src: s.at[j])
            local.append(pltpu.make_async_copy(part(me), dst.at[me], local_sems.at[a]))
            for k in range(1, N_DEV):
                pj = _peer_index(k)
                remote.append(pltpu.make_async_remote_copy(
                    src_ref=part(pj), dst_ref=dst.at[pj if landing else me], send_sem=send_sems.at[a, k - 1],
                    recv_sem=recv_sems.at[a, k - 1], device_id=_peer(k), device_id_type=MESH))
        return local, remote

    def _copy2(self, a, src, dst, sems, slot, relation, to):
        send_sems, recv_sems, _ = sems
        return pltpu.make_async_remote_copy(src_ref=src, dst_ref=dst.at[slot], send_sem=send_sems.at[a, relation - 1],
                                            recv_sem=recv_sems.at[a, relation - 1], device_id=_peer(to), device_id_type=MESH)

    def start(self, in_refs, out_refs, sems):
        if self.mode == GATHER2:
            me = _my_index()
            for a, (src, dst) in enumerate(zip(in_refs, out_refs)):
                pltpu.make_async_copy(src, dst.at[me], sems[2].at[a]).start()
                for k in (SIBLING,) + OTHER_CHIPS:
                    self._copy2(a, src, dst, sems, me, k, k).start()
            return
        local, remote = self._copies(in_refs, out_refs, *sems, landing=False)
        for cp in local + remote:
            cp.start()

    def forward(self, in_refs, out_refs, sems):
        for a, (src, dst) in enumerate(zip(in_refs, out_refs)):
            for r in OTHER_CHIPS:
                pj = _peer_index(r)
                self._copy2(a, src, dst, sems, pj, r, r).wait_recv()
                self._copy2(a, dst.at[pj], dst, sems, pj, r ^ SIBLING, SIBLING).start()

    def wait(self, in_refs, out_refs, sems):
        if self.mode == GATHER2:
            me = _my_index()
            for a, (src, dst) in enumerate(zip(in_refs, out_refs)):
                for k in (SIBLING,) + OTHER_CHIPS:
                    self._copy2(a, src, dst, sems, me, k, k).wait_send()
                self._copy2(a, src, dst, sems, _peer_index(SIBLING), SIBLING, SIBLING).wait_recv()
                for r in OTHER_CHIPS:
                    passed = self._copy2(a, src, dst, sems, _peer_index(r ^ SIBLING), r ^ SIBLING, SIBLING)
                    passed.wait_send()
                    passed.wait_recv()
                pltpu.make_async_copy(src, dst.at[me], sems[2].at[a]).wait()
            return
        local, remote = self._copies(in_refs, out_refs, *sems, landing=True)
        for cp in remote:
            cp.wait_send()
            cp.wait_recv()
        for cp in local:
            cp.wait()

    def ride(self, refs, n_in, n_out, first, mid, last):
        refs = list(refs)
        n = self.n
        x_in = refs[n_in:n_in + n]
        x_out = refs[n_in + n + n_out:n_in + 2 * n + n_out]
        sems = refs[n_in + 2 * n + n_out:n_in + 2 * n + n_out + 3]

        @pl.when(first)
        def _():
            self.start(x_in, x_out, sems)

        if self.mode == GATHER2:
            @pl.when(mid)
            def _():
                self.forward(x_in, x_out, sems)

        @pl.when(last)
        def _():
            self.wait(x_in, x_out, sems)

        return refs[:n_in] + refs[n_in + n:n_in + n + n_out] + refs[n_in + 2 * n + n_out + 3:]

    def call(self, name):
        n = self.n

        def body(*refs):
            ins, outs, sems = refs[:n], refs[n:2 * n], refs[2 * n:]
            self.start(ins, outs, sems)
            if self.mode == GATHER2:
                self.forward(ins, outs, sems)
            self.wait(ins, outs, sems)

        return pl.pallas_call(body, name=name, in_specs=self.specs(), out_specs=self.specs(), out_shape=self.out_shape(),
                              scratch_shapes=self.scratch())(*self.arrays)


def _all_gather(v, name):
    return _Exchange(GATHER, [v]).call(name)[0]


def _hosted(kernel_body, ex, n_in, n_out, first, last, grid):
    if ex is None:
        return kernel_body

    def body(*refs):
        mid = pl.program_id(0) == (2 * grid[0]) // 3 if len(grid) == 1 else None
        kernel_body(*ex.ride(refs, n_in, n_out, first(), mid, last()))

    return body


def _host_call(kernel_body, ex, first, last, name, grid, in_specs, out_specs, out_shape, scratch_shapes, sem, args):
    n_in, n_out = len(in_specs), len(out_specs)
    if ex is None:
        outs = pl.pallas_call(kernel_body, name=name, grid=grid, in_specs=in_specs, out_specs=out_specs, out_shape=out_shape,
                              scratch_shapes=scratch_shapes, compiler_params=_cp(*sem))(*args)
        return list(outs), []
    outs = pl.pallas_call(
        _hosted(kernel_body, ex, n_in, n_out, first, last, grid), name=name, grid=grid,
        in_specs=list(in_specs) + ex.specs(), out_specs=list(out_specs) + ex.specs(),
        out_shape=list(out_shape) + ex.out_shape(), scratch_shapes=ex.scratch() + list(scratch_shapes),
        compiler_params=_cp(*sem))(*args, *ex.arrays)
    return list(outs[:n_out]), list(outs[n_out:])


def _mod_fwd(call, mod_w, bias, name):
    nl, dm, n = mod_w.shape

    def body(c_ref, w_ref, b_ref, o_ref):
        cv = c_ref[...]
        cond = _bf(cv * _sig(cv))
        for layer in range(nl):
            o_ref[layer] = _nn(cond, _bf(w_ref[layer])) + b_ref[layer]

    return pl.pallas_call(
        body, name=name, out_shape=jax.ShapeDtypeStruct((nl, call.shape[0], n), F32),
        compiler_params=pltpu.CompilerParams(vmem_limit_bytes=VMEM_LIMIT),
    )(call, mod_w, bias)


def _mod_bwd(call, dm_all, mod_w, name):
    nl, dm, n = mod_w.shape

    def body(c_ref, d_ref, w_ref, gw_ref, dc_ref):
        cv = c_ref[...]
        cond = _bf(cv * _sig(cv))
        dc = jnp.zeros(cv.shape, F32)
        for layer in range(nl):
            db = _bf(d_ref[layer])
            gw_ref[layer] = _tn(cond, db)
            dc = dc + _nt(db, _bf(w_ref[layer]))
        dc_ref[...] = dc

    return pl.pallas_call(
        body, name=name,
        out_shape=[jax.ShapeDtypeStruct(mod_w.shape, F32), jax.ShapeDtypeStruct(call.shape, F32)],
        compiler_params=pltpu.CompilerParams(vmem_limit_bytes=VMEM_LIMIT),
    )(call, dm_all, mod_w)


def _sum_parts(g, name):
    def body(g_ref, o_ref):
        acc = g_ref[0]
        for j in range(1, g.shape[0]):
            acc = acc + g_ref[j]
        o_ref[...] = acc

    return pl.pallas_call(body, name=name, out_shape=jax.ShapeDtypeStruct(g.shape[1:], g.dtype))(g)


def _small_finish(dcond_g, c_ctx, dlb, lbraw, dm_ctx, dm_lat, name):
    def body(dc_ref, c_ref, dlb_ref, lb_ref, mc_ref, ml_ref, gc_ref, glb_ref, gb_ref):
        acc = dc_ref[0, 0:1, :]
        for j in range(1, N_DEV):
            acc = acc + dc_ref[j, 0:1, :]
        cv = c_ref[...]
        s = _sig(cv)
        gc_ref[...] = acc * (s * (1.0 + cv * (1.0 - s)))
        lb = _lower_bound(lb_ref)
        d0 = dlb_ref[...] * lb * (1.0 - lb)
        glb_ref[0:1, :] = d0
        glb_ref[1:2, :] = -d0
        gb_ref[...] = mc_ref[...] + ml_ref[...]

    return pl.pallas_call(
        body, name=name,
        out_shape=[jax.ShapeDtypeStruct(c_ctx.shape, F32), jax.ShapeDtypeStruct(lbraw.shape, F32),
                   jax.ShapeDtypeStruct(dm_ctx.shape, F32)],
    )(dcond_g, c_ctx, dlb, lbraw, dm_ctx, dm_lat)


def _row_tile(r, cap, mult):
    best = r
    for t in range(mult, min(r, cap) + 1, mult):
        if r % t == 0:
            best = t
    return best


def _adam(g_list, w, m, v, name, ex=None):
    nl, r, cdim = w.shape
    p = g_list[0].shape[0]
    tr = _row_tile(r, 128, 16)
    ni = r // tr

    def body(*refs):
        g_refs = refs[:nl]
        w_ref, m_ref, v_ref, go_ref, d_ref, mo_ref, vo_ref = refs[nl:]
        layer = pl.program_id(0)

        def total(g_ref):
            acc = g_ref[0].astype(F32)
            for j in range(1, p):
                acc = acc + g_ref[j].astype(F32)
            return acc

        g = total(g_refs[0])
        for k in range(1, nl):
            g = jnp.where(layer == k, total(g_refs[k]), g)
        m2 = ADAM_B1 * m_ref[0] + (1.0 - ADAM_B1) * g
        v2 = ADAM_B2 * v_ref[0] + (1.0 - ADAM_B2) * (g * g)
        m_hat = m2 / (1.0 - ADAM_B1 ** ADAM_STEP)
        v_hat = v2 / (1.0 - ADAM_B2 ** ADAM_STEP)
        go_ref[0] = g
        d_ref[0] = -ADAM_LR * (m_hat / (jnp.sqrt(v_hat) + ADAM_EPS) + ADAM_WD * w_ref[0])
        mo_ref[0] = m2
        vo_ref[0] = v2

    def g_spec(k):
        return pl.BlockSpec((p, tr, cdim), lambda la, i: (0, jnp.where(la == k, i, jnp.where(la < k, 0, ni - 1)), 0))

    spec = pl.BlockSpec((1, tr, cdim), lambda la, i: (la, i, 0))
    return _host_call(
        body, ex, lambda: (pl.program_id(0) == 0) & (pl.program_id(1) == 0),
        lambda: (pl.program_id(0) == nl - 1) & (pl.program_id(1) == ni - 1),
        name=name, grid=(nl, ni),
        in_specs=[g_spec(k) for k in range(nl)] + [spec, spec, spec],
        out_specs=[spec] * 4, out_shape=[jax.ShapeDtypeStruct((nl, r, cdim), F32)] * 4,
        scratch_shapes=[], sem=("arbitrary", "arbitrary"), args=(*g_list, w, m, v))


def _f32_as_rows(a, width):
    return lax.bitcast_convert_type(a.reshape(-1), BF16).reshape(-1, width)


def _rows_as_f32(rows):
    return lax.bitcast_convert_type(rows.reshape(rows.shape[:-2] + (-1, 2)), F32)


def _pad_rows(a, mult):
    r = (-a.shape[-2]) % mult
    if r == 0:
        return a
    widths = [(0, 0)] * (a.ndim - 2) + [(0, r), (0, 0)]
    return jnp.pad(a, widths)


def _pack_flat(parts, lane):
    flat = jnp.concatenate([p.reshape(-1).astype(F32) for p in parts])
    n = flat.shape[0]
    rows = -(-n // lane)
    rows += (-rows) % 8
    return jnp.pad(flat, (0, rows * lane - n)).reshape(rows, lane)


def _unpack_flat(packed, shapes):
    flat = packed.reshape(-1)
    out, off = [], 0
    for s in shapes:
        n = math.prod(s)
        out.append(flat[off:off + n].reshape(s))
        off += n
    return out


def kernel(x, c, ctx, c_ctx, mod_w, mod_b, norm_g, ffn_w_in, ffn_w_out, even_w_in, even_w_out, attn_qk_norm_g, attn_sink, hgrn_out_norm_g, hgrn_lb, odd_w_in, odd_w_out, loss_target, m_c_ctx, m_mod_w, m_mod_b, m_norm_g, m_ffn_w_in, m_ffn_w_out, m_even_w_in, m_even_w_out, m_attn_qk_norm_g, m_attn_sink, m_hgrn_out_norm_g, m_hgrn_lb, m_odd_w_in, m_odd_w_out, v_c_ctx, v_mod_w, v_mod_b, v_norm_g, v_ffn_w_in, v_ffn_w_out, v_even_w_in, v_even_w_out, v_attn_qk_norm_g, v_attn_sink, v_hgrn_out_norm_g, v_hgrn_lb, v_odd_w_in, v_odd_w_out):
    me = _my_index()
    lc, dm = ctx.shape[1], x.shape[2]
    nmod = mod_w.shape[2]
    big = (ffn_w_in, ffn_w_out, even_w_in, even_w_out, odd_w_in, odd_w_out)

    extra = _pad_rows(jnp.concatenate([_f32_as_rows(c, dm), _f32_as_rows(norm_g, dm)], axis=0), 16)
    shards = {"ffn_in0": ffn_w_in[0], "ffn_in1": ffn_w_in[1], "ffn_out0": ffn_w_out[0], "ffn_out1": ffn_w_out[1],
              "even_in": even_w_in[0], "even_out": even_w_out[0], "odd_in": odd_w_in[0], "odd_out": odd_w_out[0]}
    shards = {n: a.astype(BF16) for n, a in shards.items()}
    first = _Exchange(GATHER2, [shards["even_in"], extra]).call("gather_first")
    w = {"even_in": _full_weight("even_in", first[0])}
    c_all = _rows_as_f32(first[1][:, 0:2])
    norm_g_all = _rows_as_f32(first[1][:, 2:3]).reshape(N_DEV, 2, 2, -1)
    norm_g_full = norm_g_all.transpose(1, 2, 0, 3).reshape(2, 2, dm)

    call = jnp.concatenate([c_all, c_ctx[None, :], jnp.zeros((16 - N_DEV - 1, dm), F32)], axis=0)
    bias = lax.dynamic_slice_in_dim(mod_b, me * nmod, nmod, axis=1)[:, None, :]
    m_sh = _mod_fwd(call, mod_w, bias, "mod_fwd")
    m_g = _all_gather(m_sh.reshape(-1, nmod), "gather_mod").reshape(N_DEV, 2, 16, nmod)
    m_all = m_g.transpose(1, 2, 0, 3).reshape(2, 16, -1)
    m_lat = lax.dynamic_index_in_dim(m_all, me, axis=1, keepdims=False)
    mv = jnp.stack([m_all[:, N_DEV], m_lat], axis=1)[:, :, None, :]

    _, dxs, gw, small = _local_step((ctx[0], x[0]), loss_target[0], mv, norm_g_full, w, attn_qk_norm_g[0], attn_sink[0],
                                    hgrn_out_norm_g, hgrn_lb, lc, shards)
    grad_x = dxs[None]

    last = _Exchange(SCATTER, [_shard_slots("even_in_b", gw["even_in_b"])])
    big_g = [[gw["ffn_in0"], gw["ffn_in1"]], [gw["ffn_out0"], gw["ffn_out1"]], None, [gw["even_out"]],
             [gw["odd_in"]], [gw["odd_out"]]]
    halves = (2, even_w_in.shape[1] // 2, even_w_in.shape[2])
    big_w = (ffn_w_in, ffn_w_out, even_w_in.reshape(halves), even_w_out, odd_w_in, odd_w_out)
    big_m = (m_ffn_w_in, m_ffn_w_out, m_even_w_in.reshape(halves), m_even_w_out, m_odd_w_in, m_odd_w_out)
    big_v = (v_ffn_w_in, v_ffn_w_out, v_even_w_in.reshape(halves), v_even_w_out, v_odd_w_in, v_odd_w_out)
    big_names = ("ffn_w_in", "ffn_w_out", "even_w_in", "even_w_out", "odd_w_in", "odd_w_out")
    big_out = [None] * 6

    def adam_big(i, ex=None):
        big_out[i], got = _adam(big_g[i], big_w[i], big_m[i], big_v[i], "adam_" + big_names[i], ex)
        return got

    dmv = small["dmv"]
    small_shapes = [(2, 6 * dm), (2, 6 * dm), (2, 2, dm), (2, HEAD_DIM), (ATTN_HEADS,), (HG_D,), (HG_HEADS * HG_D,), (1,)]
    vec = _pack_flat([dmv[:, 0, 0], dmv[:, 1, 0], small["norm_g"], small["qk_g"], small["sink"], small["hg_out_g"],
                      small["lb"], small["loss"]], 128)
    big_g[2] = [gw["even_in_a"], adam_big(0, last)[0]]
    vec_g = adam_big(1, _Exchange(GATHER, [vec]))[0]
    tot = _unpack_flat(_sum_parts(vec_g, "sum_small"), small_shapes)
    dm_ctx_tot, dm_lat_tot, g_norm_full, g_qk, g_sink, g_hg, dlb_tot, loss_tot = tot
    dm_lat_each = vec_g.reshape(N_DEV, -1)[:, 12 * dm:24 * dm].reshape(N_DEV, 2, 6 * dm)
    dm_lat_mine = lax.dynamic_slice_in_dim(dm_lat_each, me * nmod, nmod, axis=2).transpose(1, 0, 2)
    dm_ctx_mine = lax.dynamic_slice_in_dim(dm_ctx_tot, me * nmod, nmod, axis=1)[:, None, :]
    dm_all = jnp.concatenate([dm_lat_mine, dm_ctx_mine, jnp.zeros((2, 16 - N_DEV - 1, nmod), F32)], axis=1)
    g_mod_w, dcond = _mod_bwd(call, dm_all, mod_w, "mod_bwd")
    dcond_g = adam_big(4, _Exchange(GATHER, [dcond[N_DEV:]]))[0]
    g_c_ctx, g_lb, g_mod_b = _small_finish(dcond_g, c_ctx[None, :], dlb_tot[None, :], hgrn_lb, dm_ctx_tot, dm_lat_tot,
                                           "small_finish")
    g_norm = lax.dynamic_slice_in_dim(g_norm_full, me * norm_g.shape[2], norm_g.shape[2], axis=2)
    for i in (3, 5, 2):
        adam_big(i)
    big_out[2] = [o.reshape(even_w_in.shape) for o in big_out[2]]
    big_res = [[big_out[i][k] for i in range(6)] for k in range(4)]

    mod_res, _ = _adam([g_mod_w[0][None], g_mod_w[1][None]], mod_w, m_mod_w, v_mod_w, "adam_mod_w")

    sm_w = (c_ctx, mod_b, norm_g, attn_qk_norm_g, attn_sink, hgrn_out_norm_g, hgrn_lb)
    sm_m = (m_c_ctx, m_mod_b, m_norm_g, m_attn_qk_norm_g, m_attn_sink, m_hgrn_out_norm_g, m_hgrn_lb)
    sm_v = (v_c_ctx, v_mod_b, v_norm_g, v_attn_qk_norm_g, v_attn_sink, v_hgrn_out_norm_g, v_hgrn_lb)
    sm_g = (g_c_ctx, g_mod_b, g_norm, g_qk, g_sink, g_hg, g_lb)
    sm_shapes = [a.shape for a in sm_w]
    sm_out, _ = _adam([_pack_flat(sm_g, 128)[None]], _pack_flat(sm_w, 128)[None], _pack_flat(sm_m, 128)[None],
                      _pack_flat(sm_v, 128)[None], "adam_small")
    sm_res = [_unpack_flat(o, sm_shapes) for o in sm_out]

    def ordered(k):
        s, b = sm_res[k], big_res[k]
        return [s[0], mod_res[k], s[1], s[2], b[0], b[1], b[2], b[3], s[3], s[4], s[5], s[6], b[4], b[5]]

    return (loss_tot[0], grad_x, *ordered(0), *ordered(1), *ordered(2), *ordered(3))
```

```python
import functools
import math

import jax
import jax.numpy as jnp
import numpy as np
from jax import lax
from jax.experimental import pallas as pl
from jax.experimental.pallas import tpu as pltpu

F32 = jnp.float32
BF16 = jnp.bfloat16
EPS = 1e-6
N_DEV = 8
MESH = pl.DeviceIdType.MESH

HEAD_DIM = 64
ATTN_HEADS = 8
ATTN_KV = 2
ATTN_BLOCK = 128
WINDOW = 128
GRID_W = 64
HG_HEADS = 4
HG_D = 128
HG_CHUNK = 64
HG_STEP_CHUNKS = 4
RET_HEADS = 4
RET_DK = 256
RET_DV = 512
RET_CHUNK = 256
NEG = -1e30

ADAM_LR = 0.001
ADAM_B1 = 0.9
ADAM_B2 = 0.999
ADAM_EPS = 1e-08
ADAM_WD = 0.01
ADAM_STEP = 10

VMEM_LIMIT = 60 * 1024 * 1024
MXU_WIDTH = 256


def _hidden_chunks(fh, parts=2):
    step = -(-(fh // parts) // MXU_WIDTH) * MXU_WIDTH
    cuts = list(range(0, fh, step)) + [fh]
    return list(zip(cuts[:-1], cuts[1:]))


def _cp(*sem):
    return pltpu.CompilerParams(dimension_semantics=sem, vmem_limit_bytes=VMEM_LIMIT)


def _nn(a, b):
    return jnp.dot(a, b, preferred_element_type=F32)


def _nt(a, b):
    return lax.dot_general(a, b, (((1,), (1,)), ((), ())), preferred_element_type=F32)


def _tn(a, b):
    return lax.dot_general(a, b, (((0,), (0,)), ((), ())), preferred_element_type=F32)


ACT = BF16


def _bf(a):
    return a.astype(ACT)


def _sig(x):
    return jax.nn.sigmoid(x)


def _split3(x):
    h = x.astype(BF16)
    r = x - h.astype(F32)
    m = r.astype(BF16)
    lo = (r - m.astype(F32)).astype(BF16)
    return h, m, lo


def _nn3(m01, x):
    h, m, lo = _split3(x)
    return _nn(m01, h) + _nn(m01, m) + _nn(m01, lo)


def _nn3r(x, m01):
    h, m, lo = _split3(x)
    return _nn(h, m01) + _nn(m, m01) + _nn(lo, m01)


def _full(shape):
    nd = len(shape)
    return pl.BlockSpec(shape, lambda *a: (0,) * nd, pipeline_mode=pl.Buffered(1))


def _whole(shape):
    nd = len(shape)
    return pl.BlockSpec(shape, lambda *a: (0,) * nd)


def _rows(tm, width):
    return pl.BlockSpec((tm, width), lambda i: (i, 0))


def _cols(height, tm):
    return pl.BlockSpec((height, tm), lambda i: (0, i))


def _ctx_lat(width):
    return pl.BlockSpec((1, 1, width), lambda i: (jnp.minimum(i, 1), 0, 0))


def _acc_ctx_lat(ref, i, val):
    @pl.when(i <= 1)
    def _():
        ref[...] = val.reshape(ref.shape)

    @pl.when(i > 1)
    def _():
        ref[...] += val.reshape(ref.shape)


def _acc_all(ref, i, val):
    @pl.when(i == 0)
    def _():
        ref[...] = val.reshape(ref.shape)

    @pl.when(i > 0)
    def _():
        ref[...] += val.reshape(ref.shape)


def _tile(n, cap):
    best = None
    for t in range(128, min(n, cap) + 1, 128):
        if n % t == 0:
            best = t
    return n if best is None else best


def _norm_mod(xv, g, shift, scale):
    r = lax.rsqrt(jnp.mean(xv * xv, axis=-1, keepdims=True) + EPS)
    xhat = xv * r
    n = xhat * g
    return r, xhat, n, n * (1.0 + scale) + shift


def _norm_mod_bwd(dh, r, xhat, n, g, scale):
    dshift = jnp.sum(dh, axis=0, keepdims=True)
    dscale = jnp.sum(dh * n, axis=0, keepdims=True)
    dn = dh * (1.0 + scale)
    dg = jnp.sum(dn * xhat, axis=0, keepdims=True)
    dxh = dn * g
    dx = r * (dxh - xhat * jnp.mean(dxh * xhat, axis=-1, keepdims=True))
    return dx, dshift, dscale, dg


def _stream(x):
    if isinstance(x, tuple):
        return list(x), x[0].shape[0] + x[1].shape[0], x[0].shape[1]
    return [x], x.shape[0], x.shape[1]


def _stream_specs(x, tm, dm):
    if isinstance(x, tuple):
        return [pl.BlockSpec((tm, dm), lambda i: (0, 0)), pl.BlockSpec((tm, dm), lambda i: (jnp.maximum(i - 1, 0), 0))]
    return [_rows(tm, dm)]


def _stream_tile(refs):
    if len(refs) == 2:
        return jnp.where(pl.program_id(0) == 0, refs[0][...], refs[1][...])
    return refs[0][...]


def _pre_fwd(x, gain, ms, w, splits, tm, name, ex=None, out_dtype=F32, qk=None):
    xs, T, dm = _stream(x)
    nx = len(xs)
    nt = T // tm
    nq = 0 if qk is None else 3
    ns = len(splits)

    def body(*refs):
        g_ref, ms_ref, w_ref = refs[nx:nx + 3]
        outs = refs[nx + 3 + nq:]
        ms_v = ms_ref[0]
        h = _norm_mod(_stream_tile(refs[:nx]), g_ref[...], ms_v[:, :dm], ms_v[:, dm:])[3]
        hb = _bf(h)
        for k, ((s, e), o_ref) in enumerate(zip(splits, outs[:ns])):
            part = _nn(hb, w_ref[:, s:e])
            o_ref[...] = part.astype(o_ref.dtype)
            if k == 0 and qk is not None:
                gq_ref, c_ref, s_ref = refs[nx + 3:nx + 6]
                _qk_tile_fwd(part, gq_ref, c_ref[...], s_ref[...], *outs[ns:])

    in_specs = _stream_specs(x, tm, dm) + [_full((1, dm)), _ctx_lat(2 * dm), _full(w.shape)]
    out_specs = [_rows(tm, e - s) for s, e in splits]
    out_shape = [jax.ShapeDtypeStruct((T, e - s), out_dtype) for s, e in splits]
    args = [*xs, gain, ms, w]
    if qk is not None:
        qw = ATTN_HEADS * HEAD_DIM
        in_specs += [_full(qk[0].shape), _rows(tm, PAIR), _rows(tm, PAIR)]
        args += list(qk)
        out_specs += [_rows(tm, qw), _rows(tm, PAIR), _rows(tm, PAIR)]
        out_shape += [jax.ShapeDtypeStruct((T, qw), ACT), jax.ShapeDtypeStruct((T, PAIR), ACT), jax.ShapeDtypeStruct((T, PAIR), ACT)]
    return _host_call(
        body, ex, lambda: pl.program_id(0) == 0, lambda: pl.program_id(0) == nt - 1,
        name=name, grid=(nt,), in_specs=in_specs, out_specs=out_specs, out_shape=out_shape,
        scratch_shapes=[], sem=("arbitrary",), args=tuple(args))


def _pre_bwd(x, dx_in, gain, ms, w, pieces, tm, name, latent_dx=False, ex=None, qk=None):
    xs, T, dm = _stream(x)
    nx = len(xs)
    dx_spec = pl.BlockSpec((tm, dm), lambda i: (jnp.maximum(i - 1, 0), 0)) if latent_dx else _rows(tm, dm)
    dx_rows = T - tm if latent_dx else T
    n_out = w.shape[1]
    flat = [a for _, arrs in pieces for a in arrs]
    nq = 0 if qk is None else 6
    qkw = (ATTN_HEADS + ATTN_KV) * HEAD_DIM

    def body(*refs):
        dxin_ref, g_ref, ms_ref, w_ref = refs[nx:nx + 4]
        rest = refs[nx + 4:]
        p_refs = rest[:len(flat)]
        qk_refs = rest[len(flat):len(flat) + nq]
        dx_ref, h_ref, dp_ref, dms_ref, dg_ref = rest[len(flat) + nq:len(flat) + nq + 5]
        i = pl.program_id(0)
        ms_v = ms_ref[0]
        g = g_ref[...]
        scale = ms_v[:, dm:]
        r, xhat, n, h = _norm_mod(_stream_tile(refs[:nx]), g, ms_v[:, :dm], scale)
        h_ref[...] = _bf(h).T
        dh = jnp.zeros((tm, dm), F32)
        if qk is not None:
            dq_ref, dk_ref, pa_ref, gq_ref, c_ref, s_ref = qk_refs
            dqk, dgs = _qk_tile_bwd(dq_ref, dk_ref, pa_ref, gq_ref, c_ref[...], s_ref[...])
            dgq_ref = rest[len(flat) + nq + 5]
            for p, dgp in enumerate(dgs):
                _acc_all(dgq_ref.at[p], i, dgp)
            vb = _bf(dqk)
            dp_ref[:, :qkw] = vb
            dh = dh + _nt(vb, w_ref[:, :qkw])
        k = 0
        for s, arrs in pieces:
            v = p_refs[k][...].astype(F32)
            for j in range(1, len(arrs)):
                v = v + p_refs[k + j][...].astype(F32)
            k += len(arrs)
            vb = _bf(v)
            wd = vb.shape[1]
            dp_ref[:, s:s + wd] = vb
            dh = dh + _nt(vb, w_ref[:, s:s + wd])
        dx, dshift, dscale, dg = _norm_mod_bwd(dh, r, xhat, n, g, scale)
        dx_ref[...] = dxin_ref[...] + dx
        _acc_ctx_lat(dms_ref, i, jnp.concatenate([dshift, dscale], axis=1))
        _acc_all(dg_ref, i, dg)

    nt = T // tm
    in_specs = (_stream_specs(x, tm, dm) + [_rows(tm, dm), _full((1, dm)), _ctx_lat(2 * dm), _full(w.shape)]
                + [_rows(tm, a.shape[1]) for a in flat])
    out_specs = [dx_spec, _cols(dm, tm), _rows(tm, n_out), _ctx_lat(2 * dm), _whole((1, dm))]
    out_shape = [jax.ShapeDtypeStruct((dx_rows, dm), F32), jax.ShapeDtypeStruct((dm, T), ACT),
                 jax.ShapeDtypeStruct((T, n_out), ACT), jax.ShapeDtypeStruct((2, 1, 2 * dm), F32),
                 jax.ShapeDtypeStruct((1, dm), F32)]
    args = [*xs, dx_in, gain, ms, w, *flat]
    if qk is not None:
        dq, dk, pa, gains, cosp, sinp = qk
        in_specs += [_rows(tm, dq.shape[1]), _rows(tm, PAIR), _rows(tm, qkw), _full(gains.shape), _rows(tm, PAIR), _rows(tm, PAIR)]
        args += [dq, dk, pa, gains, cosp, sinp]
        out_specs.append(_whole(gains.shape))
        out_shape.append(jax.ShapeDtypeStruct(gains.shape, F32))
    return _host_call(
        body, ex, lambda: pl.program_id(0) == 0, lambda: pl.program_id(0) == nt - 1,
        name=name, grid=(nt,), in_specs=in_specs, out_specs=out_specs, out_shape=out_shape,
        scratch_shapes=[], sem=("arbitrary",), args=tuple(args))


def _ffn_fwd(x1, gain, ms, w_in, w_out, tm, name, target=None, ex=None):
    T, dm = x1.shape
    fh = w_out.shape[0]
    head = target is not None

    def body(*refs):
        if head:
            x_ref, g_ref, ms_ref, wi_ref, wo_ref, t_ref, x2_ref, u_ref, f_ref, loss_ref = refs
        else:
            x_ref, g_ref, ms_ref, wi_ref, wo_ref, x2_ref, u_ref, f_ref = refs
        ms_v = ms_ref[0]
        xv = x_ref[...]
        hb = _bf(_norm_mod(xv, g_ref[...], ms_v[:, :dm], ms_v[:, dm:2 * dm])[3])
        f = jnp.zeros((tm, dm), F32)
        for c0, c1 in _hidden_chunks(fh):
            gt = _nn(hb, wi_ref[:, c0:c1])
            up = _nn(hb, wi_ref[:, fh + c0:fh + c1])
            u_ref[:, c0:c1] = _bf(gt)
            u_ref[:, fh + c0:fh + c1] = _bf(up)
            f = f + _nn(_bf(gt * _sig(gt) * up), wo_ref[c0:c1, :])
        f_ref[...] = _bf(f)
        x2 = xv + ms_v[:, 2 * dm:] * f
        if head:
            i = pl.program_id(0)
            e = x2 - t_ref[...]
            x2_ref[...] = jnp.where(i > 0, e * (1.0 / dm), 0.0)
            _acc_all(loss_ref, i, jnp.where(i > 0, jnp.sum(e * e) * (0.5 / dm), 0.0))
        else:
            x2_ref[...] = x2

    ins = [x1, gain, ms, w_in, w_out]
    in_specs = [_rows(tm, dm), _full((1, dm)), _ctx_lat(3 * dm), _full(w_in.shape), _full(w_out.shape)]
    out_specs = [_rows(tm, dm), _rows(tm, 2 * fh), _rows(tm, dm)]
    out_shape = [jax.ShapeDtypeStruct((T, dm), F32), jax.ShapeDtypeStruct((T, 2 * fh), ACT), jax.ShapeDtypeStruct((T, dm), ACT)]
    if head:
        ins.append(target)
        in_specs.append(pl.BlockSpec((tm, dm), lambda i: (jnp.maximum(i - 1, 0), 0)))
        out_specs.append(_whole((1, 1)))
        out_shape.append(jax.ShapeDtypeStruct((1, 1), F32))
    nt = T // tm
    return _host_call(
        body, ex, lambda: pl.program_id(0) == 0, lambda: pl.program_id(0) == nt - 1,
        name=name, grid=(nt,), in_specs=in_specs, out_specs=out_specs, out_shape=out_shape,
        scratch_shapes=[], sem=("arbitrary",), args=tuple(ins))


def _ffn_bwd(x1, dx2, u, f, gain, ms, w_in, w_out, tm, name, ex=None):
    T, dm = x1.shape
    fh = w_out.shape[0]

    def body(x_ref, dx2_ref, u_ref, f_ref, g_ref, ms_ref, wi_ref, wo_ref,
             dx1_ref, h_ref, du_ref, act_ref, df_ref, dms_ref, dg_ref):
        i = pl.program_id(0)
        ms_v = ms_ref[0]
        g = g_ref[...]
        scale = ms_v[:, dm:2 * dm]
        gate = ms_v[:, 2 * dm:]
        r, xhat, n, h = _norm_mod(x_ref[...], g, ms_v[:, :dm], scale)
        h_ref[...] = _bf(h).T
        dx2 = dx2_ref[...]
        dgate = jnp.sum(dx2 * f_ref[...].astype(F32), axis=0, keepdims=True)
        dfb = _bf(dx2 * gate)
        df_ref[...] = dfb
        dh = jnp.zeros((tm, dm), F32)
        for c0, c1 in _hidden_chunks(fh, 1):
            da = _nt(dfb, wo_ref[c0:c1, :])
            gt = u_ref[:, c0:c1].astype(F32)
            up = u_ref[:, fh + c0:fh + c1].astype(F32)
            s = _sig(gt)
            sg = gt * s
            act_ref[c0:c1, :] = _bf(sg * up).T
            dgt = _bf(da * up * (s * (1.0 + gt * (1.0 - s))))
            dup = _bf(da * sg)
            du_ref[:, c0:c1] = dgt
            du_ref[:, fh + c0:fh + c1] = dup
            dh = dh + _nt(dgt, wi_ref[:, c0:c1]) + _nt(dup, wi_ref[:, fh + c0:fh + c1])
        dx, dshift, dscale, dg = _norm_mod_bwd(dh, r, xhat, n, g, scale)
        dx1_ref[...] = dx2 + dx
        _acc_ctx_lat(dms_ref, i, jnp.concatenate([dshift, dscale, dgate], axis=1))
        _acc_all(dg_ref, i, dg)

    nt = T // tm
    return _host_call(
        body, ex, lambda: pl.program_id(0) == 0, lambda: pl.program_id(0) == nt - 1,
        name=name, grid=(nt,),
        in_specs=[_rows(tm, dm), _rows(tm, dm), _rows(tm, 2 * fh), _rows(tm, dm), _full((1, dm)), _ctx_lat(3 * dm),
                  _full(w_in.shape), _full(w_out.shape)],
        out_specs=[_rows(tm, dm), _cols(dm, tm), _rows(tm, 2 * fh), _cols(fh, tm), _rows(tm, dm),
                   _ctx_lat(3 * dm), _whole((1, dm))],
        out_shape=[jax.ShapeDtypeStruct((T, dm), F32), jax.ShapeDtypeStruct((dm, T), ACT),
                   jax.ShapeDtypeStruct((T, 2 * fh), ACT), jax.ShapeDtypeStruct((fh, T), ACT),
                   jax.ShapeDtypeStruct((T, dm), ACT), jax.ShapeDtypeStruct((2, 1, 3 * dm), F32),
                   jax.ShapeDtypeStruct((1, dm), F32)],
        scratch_shapes=[], sem=("arbitrary",), args=(x1, dx2, u, f, gain, ms, w_in, w_out))


def _wgrad(a_t, b, name, rows=None, ex=None):
    T = a_t.shape[1]
    r0, K = (0, a_t.shape[0]) if rows is None else rows
    N = b.shape[1]
    tk, tn, tt = _tile(K, 1408), _tile(N, 1024), _tile(T, 4224)
    nt = T // tt
    assert r0 % tk == 0
    off = r0 // tk
    nk, nn = K // tk, N // tn

    def body(a_ref, b_ref, o_ref, acc_ref):
        t = pl.program_id(2)
        part = _nn(a_ref[...], b_ref[...])

        @pl.when(t == 0)
        def _():
            acc_ref[...] = part

        @pl.when(t > 0)
        def _():
            acc_ref[...] += part

        @pl.when(t == nt - 1)
        def _():
            o_ref[...] = acc_ref[...].astype(o_ref.dtype)

    def at(i, j, t):
        return (pl.program_id(0) == i) & (pl.program_id(1) == j) & (pl.program_id(2) == t)

    outs, got = _host_call(
        body, ex, lambda: at(0, 0, 0), lambda: at(nk - 1, nn - 1, nt - 1),
        name=name, grid=(nk, nn, nt),
        in_specs=[pl.BlockSpec((tk, tt), lambda i, j, t: (i + off, t)), pl.BlockSpec((tt, tn), lambda i, j, t: (t, j))],
        out_specs=[pl.BlockSpec((tk, tn), lambda i, j, t: (i, j))],
        out_shape=[jax.ShapeDtypeStruct((K, N), ACT)],
        scratch_shapes=[pltpu.VMEM((tk, tn), F32)], sem=("arbitrary", "arbitrary", "arbitrary"), args=(a_t, b))
    return outs[0] if ex is None else (outs[0], got)


def _post_fwd(x, o_fw, o_bw, g_src, g_blk, gain, a, w_out, ms, dvh, tm, name):
    xs, T, dm = _stream(x)
    nx = len(xs)
    hv = o_fw.shape[1]
    aw = 0 if a is None else a.shape[1]
    has_gain = gain is not None

    def body(*refs):
        refs = list(refs)
        x_refs = refs[:nx]
        of_ref, ob_ref, g_ref = refs[nx:nx + 3]
        k = nx + 3
        gain_ref = a_ref = None
        if has_gain:
            gain_ref = refs[k]
            k += 1
        if aw:
            a_ref = refs[k]
            k += 1
        w_ref, ms_ref, x1_ref, z_ref, yp_ref = refs[k:k + 5]
        o = of_ref[...].astype(F32) + ob_ref[...].astype(F32)
        gr = g_ref[...].astype(F32)
        if aw:
            z_ref[:, :aw] = _bf(a_ref[...])
        for hd in range(hv // dvh):
            sl = slice(hd * dvh, (hd + 1) * dvh)
            oh = o[:, sl]
            gh = gr[:, sl]
            r = lax.rsqrt(jnp.mean(oh * oh, axis=-1, keepdims=True) + EPS)
            y = oh * r
            if has_gain:
                y = y * gain_ref[...]
            y = y * (gh * _sig(gh))
            z_ref[:, aw + hd * dvh:aw + (hd + 1) * dvh] = _bf(y)
        yp = _nn(z_ref[...], w_ref[...])
        yp_ref[...] = _bf(yp)
        x1_ref[...] = _stream_tile(x_refs) + ms_ref[0] * yp

    ins = xs + [o_fw, o_bw, g_src]
    specs = _stream_specs(x, tm, dm) + [_rows(tm, hv), _rows(tm, hv), pl.BlockSpec((tm, hv), lambda i: (i, g_blk))]
    if has_gain:
        ins.append(gain)
        specs.append(_full(gain.shape))
    if aw:
        ins.append(a)
        specs.append(_rows(tm, aw))
    ins += [w_out, ms]
    specs += [_full(w_out.shape), _ctx_lat(dm)]
    return pl.pallas_call(
        body, name=name, grid=(T // tm,), in_specs=specs,
        out_specs=[_rows(tm, dm), _rows(tm, aw + hv), _rows(tm, dm)],
        out_shape=[jax.ShapeDtypeStruct((T, dm), F32), jax.ShapeDtypeStruct((T, aw + hv), ACT),
                   jax.ShapeDtypeStruct((T, dm), ACT)],
        compiler_params=_cp("arbitrary"),
    )(*ins)


def _post_bwd(dx1, z, yp, o_fw, o_bw, g_src, g_blk, gain, w_out, ms, aw, dvh, tm, name):
    T, dm = dx1.shape
    hv = o_fw.shape[1]
    has_gain = gain is not None

    def body(*refs):
        refs = list(refs)
        dx1_ref, z_ref, yp_ref, of_ref, ob_ref, g_ref = refs[:6]
        k = 6
        gain_ref = None
        if has_gain:
            gain_ref = refs[k]
            k += 1
        w_ref, ms_ref = refs[k:k + 2]
        k += 2
        do_ref, dgr_ref = refs[k:k + 2]
        k += 2
        da_ref = None
        if aw:
            da_ref = refs[k]
            k += 1
        dy_ref, zt_ref, dgate_ref, dgain_ref = refs[k:k + 4]
        i = pl.program_id(0)
        dx1v = dx1_ref[...]
        zt_ref[...] = z_ref[...].T
        _acc_ctx_lat(dgate_ref, i, jnp.sum(dx1v * yp_ref[...].astype(F32), axis=0, keepdims=True))
        dyb = _bf(dx1v * ms_ref[0])
        dy_ref[...] = dyb
        dz = _nt(dyb, w_ref[...])
        if aw:
            da_ref[...] = dz[:, :aw]
        o = of_ref[...].astype(F32) + ob_ref[...].astype(F32)
        gr = g_ref[...].astype(F32)
        dgain = jnp.zeros((1, dvh), F32)
        for hd in range(hv // dvh):
            sl = slice(hd * dvh, (hd + 1) * dvh)
            oh = o[:, sl]
            gh = gr[:, sl]
            dyh = dz[:, aw + hd * dvh:aw + (hd + 1) * dvh]
            r = lax.rsqrt(jnp.mean(oh * oh, axis=-1, keepdims=True) + EPS)
            n = oh * r
            s = _sig(gh)
            sl_g = gh * s
            gn = gain_ref[...] if has_gain else 1.0
            dgr_ref[:, sl] = _bf(dyh * n * gn * (s * (1.0 + gh * (1.0 - s))))
            dn = dyh * gn * sl_g
            dgain = dgain + jnp.sum(dyh * n * sl_g, axis=0, keepdims=True)
            do_ref[:, sl] = _bf(r * (dn - n * jnp.mean(dn * n, axis=-1, keepdims=True)))
        _acc_all(dgain_ref, i, dgain)

    ins = [dx1, z, yp, o_fw, o_bw, g_src]
    specs = [_rows(tm, dm), _rows(tm, aw + hv), _rows(tm, dm), _rows(tm, hv), _rows(tm, hv),
             pl.BlockSpec((tm, hv), lambda i: (i, g_blk))]
    if has_gain:
        ins.append(gain)
        specs.append(_full(gain.shape))
    ins += [w_out, ms]
    specs += [_full(w_out.shape), _ctx_lat(dm)]
    out_specs = [_rows(tm, hv), _rows(tm, hv)]
    out_shape = [jax.ShapeDtypeStruct((T, hv), ACT), jax.ShapeDtypeStruct((T, hv), ACT)]
    if aw:
        out_specs.append(_rows(tm, aw))
        out_shape.append(jax.ShapeDtypeStruct((T, aw), F32))
    out_specs += [_rows(tm, dm), _cols(aw + hv, tm), _ctx_lat(dm), _whole((1, dvh))]
    out_shape += [jax.ShapeDtypeStruct((T, dm), ACT), jax.ShapeDtypeStruct((aw + hv, T), ACT),
                  jax.ShapeDtypeStruct((2, 1, dm), F32), jax.ShapeDtypeStruct((1, dvh), F32)]
    return pl.pallas_call(
        body, name=name, grid=(T // tm,), in_specs=specs, out_specs=out_specs, out_shape=out_shape,
        compiler_params=_cp("arbitrary"),
    )(*ins)


def _loss_bwd(x, target, tm, name):
    T, dm = x.shape

    def body(x_ref, t_ref, dx_ref, loss_ref):
        i = pl.program_id(0)

        @pl.when(i == 0)
        def _():
            dx_ref[...] = jnp.zeros_like(dx_ref)
            loss_ref[...] = jnp.zeros_like(loss_ref)

        @pl.when(i > 0)
        def _():
            e = x_ref[...] - t_ref[...]
            dx_ref[...] = e * (1.0 / dm)
            loss_ref[...] += jnp.sum(e * e) * (0.5 / dm)

    return pl.pallas_call(
        body, name=name, grid=(T // tm,),
        in_specs=[_rows(tm, dm), pl.BlockSpec((tm, dm), lambda i: (jnp.maximum(i - 1, 0), 0))],
        out_specs=[_rows(tm, dm), _whole((1, 1))],
        out_shape=[jax.ShapeDtypeStruct((T, dm), F32), jax.ShapeDtypeStruct((1, 1), F32)],
        compiler_params=_cp("arbitrary"),
    )(x, target)


def _swap_matrix():
    r = lax.broadcasted_iota(jnp.int32, (HEAD_DIM, HEAD_DIM), 0)
    c = lax.broadcasted_iota(jnp.int32, (HEAD_DIM, HEAD_DIM), 1)
    return jnp.where((r + HEAD_DIM // 2) % HEAD_DIM == c, 1.0, 0.0).astype(BF16)


def _qk_prep_fwd(raw, gains, cos2, sin2, tq, name):
    nh, T, hd = raw.shape

    def body(x_ref, g_ref, c_ref, s_ref, o_ref):
        hidx = pl.program_id(0)
        xv = x_ref[0]
        r = lax.rsqrt(jnp.mean(xv * xv, axis=-1, keepdims=True) + EPS)
        n = xv * r * g_ref[0]
        y = n * c_ref[...] + _nn3r(n, _swap_matrix()) * s_ref[...]
        sc = jnp.where(hidx < ATTN_HEADS, HEAD_DIM ** -0.5, 1.0)
        o_ref[0] = _bf(y * sc)

    return pl.pallas_call(
        body, name=name, grid=(nh, T // tq),
        in_specs=[pl.BlockSpec((1, tq, hd), lambda h, i: (h, i, 0)), pl.BlockSpec((1, 1, hd), lambda h, i: (h, 0, 0)),
                  pl.BlockSpec((tq, hd), lambda h, i: (i, 0)), pl.BlockSpec((tq, hd), lambda h, i: (i, 0))],
        out_specs=pl.BlockSpec((1, tq, hd), lambda h, i: (h, i, 0)),
        out_shape=jax.ShapeDtypeStruct((nh, T, hd), ACT),
        compiler_params=_cp("arbitrary", "arbitrary"),
    )(raw, gains, cos2, sin2)


def _qk_prep_bwd(dy, raw, gains, cos2, sin2, tq, name):
    nh, T, hd = raw.shape

    def body(dy_ref, x_ref, g_ref, c_ref, s_ref, dx_ref, dg_ref):
        hidx = pl.program_id(0)
        i = pl.program_id(1)
        xv = x_ref[0]
        g = g_ref[0]
        r = lax.rsqrt(jnp.mean(xv * xv, axis=-1, keepdims=True) + EPS)
        xhat = xv * r
        sc = jnp.where(hidx < ATTN_HEADS, HEAD_DIM ** -0.5, 1.0)
        dyv = dy_ref[0] * sc
        dn = dyv * c_ref[...] + _nn3r(dyv * s_ref[...], _swap_matrix())
        _acc_all(dg_ref, i, jnp.sum(dn * xhat, axis=0, keepdims=True))
        dxh = dn * g
        dx_ref[0] = r * (dxh - xhat * jnp.mean(dxh * xhat, axis=-1, keepdims=True))

    return pl.pallas_call(
        body, name=name, grid=(nh, T // tq),
        in_specs=[pl.BlockSpec((1, tq, hd), lambda h, i: (h, i, 0)), pl.BlockSpec((1, tq, hd), lambda h, i: (h, i, 0)),
                  pl.BlockSpec((1, 1, hd), lambda h, i: (h, 0, 0)),
                  pl.BlockSpec((tq, hd), lambda h, i: (i, 0)), pl.BlockSpec((tq, hd), lambda h, i: (i, 0))],
        out_specs=[pl.BlockSpec((1, tq, hd), lambda h, i: (h, i, 0)), pl.BlockSpec((1, 1, hd), lambda h, i: (h, 0, 0))],
        out_shape=[jax.ShapeDtypeStruct((nh, T, hd), F32), jax.ShapeDtypeStruct((nh, 1, hd), F32)],
        compiler_params=_cp("arbitrary", "arbitrary"),
    )(dy, raw, gains, cos2, sin2)


def _attn_scores(q, k_ref, i, lc, T, sink):
    blk = ATTN_BLOCK
    kc = k_ref[0, pl.ds(blk, lc), :]
    kw = k_ref[0, pl.ds(pl.multiple_of(i * blk, blk), 3 * blk), :]
    s_c = _nt(q, kc)
    s_w = _nt(q, kw)
    row = lax.broadcasted_iota(jnp.int32, (4 * blk, 1), 0)
    qpos = i * blk + (row & (blk - 1))
    kpos = (i - 1) * blk + lax.broadcasted_iota(jnp.int32, (1, 3 * blk), 1)
    valid = (qpos >= lc) & (kpos >= lc) & (kpos < T) & (jnp.abs(kpos - qpos) <= WINDOW)
    s_w = jnp.where(valid, s_w, NEG)
    return kc, kw, s_c, s_w


def _attn_fwd(qt, kp, vp, sinkb, lc, name, ex=None):
    nh, T, hd = qt.shape
    blk = ATTN_BLOCK
    g = nh // ATTN_KV

    def body(q_ref, k_ref, v_ref, sink_ref, o_ref, lse_ref):
        i = pl.program_id(1)
        q = q_ref[...].reshape(g * blk, hd)
        sink = sink_ref[0]
        kc, kw, s_c, s_w = _attn_scores(q, k_ref, i, lc, T, sink)
        m = jnp.maximum(jnp.maximum(jnp.max(s_c, axis=-1, keepdims=True), jnp.max(s_w, axis=-1, keepdims=True)), sink)
        e_c = jnp.exp(s_c - m)
        e_w = jnp.exp(s_w - m)
        den = jnp.exp(sink - m) + jnp.sum(e_c, axis=-1, keepdims=True) + jnp.sum(e_w, axis=-1, keepdims=True)
        inv = 1.0 / den
        vc = v_ref[0, pl.ds(blk, lc), :]
        vw = v_ref[0, pl.ds(pl.multiple_of(i * blk, blk), 3 * blk), :]
        o = _nn(_bf(e_c * inv), vc) + _nn(_bf(e_w * inv), vw)
        o_ref[...] = o.reshape(g, blk, hd)
        lse_ref[...] = (m + jnp.log(den)).reshape(g, blk, 1)

    nb = T // blk
    return _host_call(
        body, ex, lambda: (pl.program_id(0) == 0) & (pl.program_id(1) == 0),
        lambda: (pl.program_id(0) == ATTN_KV - 1) & (pl.program_id(1) == nb - 1),
        name=name, grid=(ATTN_KV, nb),
        in_specs=[pl.BlockSpec((g, blk, hd), lambda kv, i: (kv, i, 0)),
                  pl.BlockSpec((1, T + 2 * blk, hd), lambda kv, i: (kv, 0, 0)),
                  pl.BlockSpec((1, T + 2 * blk, hd), lambda kv, i: (kv, 0, 0)),
                  pl.BlockSpec((1, g * blk, 1), lambda kv, i: (kv, 0, 0))],
        out_specs=[pl.BlockSpec((g, blk, hd), lambda kv, i: (kv, i, 0)),
                   pl.BlockSpec((g, blk, 1), lambda kv, i: (kv, i, 0))],
        out_shape=[jax.ShapeDtypeStruct((nh, T, hd), F32), jax.ShapeDtypeStruct((nh, T, 1), F32)],
        scratch_shapes=[], sem=("arbitrary", "arbitrary"), args=(qt, kp, vp, sinkb))


def _attn_bwd(qt, kp, vp, sinkb, o, lse, do, lc, name):
    nh, T, hd = qt.shape
    blk = ATTN_BLOCK
    g = nh // ATTN_KV

    def body(q_ref, k_ref, v_ref, sink_ref, o_ref, lse_ref, do_ref, dq_ref, dk_ref, dv_ref, ds_ref):
        i = pl.program_id(1)

        @pl.when(i == 0)
        def _():
            dk_ref[...] = jnp.zeros_like(dk_ref)
            dv_ref[...] = jnp.zeros_like(dv_ref)
            ds_ref[...] = jnp.zeros_like(ds_ref)

        q = q_ref[...].reshape(g * blk, hd)
        sink = sink_ref[0]
        lse = lse_ref[...].reshape(g * blk, 1)
        dov = do_ref[...].reshape(g * blk, hd)
        delta = jnp.sum(dov * o_ref[...].reshape(g * blk, hd), axis=-1, keepdims=True)
        kc, kw, s_c, s_w = _attn_scores(q, k_ref, i, lc, T, sink)
        p_c = jnp.exp(s_c - lse)
        p_w = jnp.exp(s_w - lse)
        win = pl.ds(pl.multiple_of(i * blk, blk), 3 * blk)
        vc = v_ref[0, pl.ds(blk, lc), :]
        vw = v_ref[0, win, :]
        dob = _bf(dov)
        ds_c = _bf(p_c * (_nt(dob, vc) - delta))
        ds_w = _bf(p_w * (_nt(dob, vw) - delta))
        dsr = -jnp.exp(sink - lse) * delta
        for hh in range(g):
            ds_ref[0, hh:hh + 1, :] += jnp.sum(dsr[hh * blk:(hh + 1) * blk, :], axis=0, keepdims=True)
        dq_ref[...] = (_nn(ds_c, kc) + _nn(ds_w, kw)).reshape(g, blk, hd)
        dk_ref[0, pl.ds(blk, lc), :] += _tn(ds_c, q)
        dk_ref[0, win, :] += _tn(ds_w, q)
        dv_ref[0, pl.ds(blk, lc), :] += _tn(_bf(p_c), dob)
        dv_ref[0, win, :] += _tn(_bf(p_w), dob)

    qspec = pl.BlockSpec((g, blk, hd), lambda kv, i: (kv, i, 0))
    kspec = pl.BlockSpec((1, T + 2 * blk, hd), lambda kv, i: (kv, 0, 0))
    lspec = pl.BlockSpec((g, blk, 1), lambda kv, i: (kv, i, 0))
    return pl.pallas_call(
        body, name=name, grid=(ATTN_KV, T // blk),
        in_specs=[qspec, kspec, kspec, pl.BlockSpec((1, g * blk, 1), lambda kv, i: (kv, 0, 0)), qspec, lspec, qspec],
        out_specs=[qspec, kspec, kspec, pl.BlockSpec((1, g, 1), lambda kv, i: (kv, 0, 0))],
        out_shape=[jax.ShapeDtypeStruct((nh, T, hd), F32), jax.ShapeDtypeStruct((ATTN_KV, T + 2 * blk, hd), F32),
                   jax.ShapeDtypeStruct((ATTN_KV, T + 2 * blk, hd), F32), jax.ShapeDtypeStruct((ATTN_KV, g, 1), F32)],
        compiler_params=_cp("arbitrary", "arbitrary"),
    )(qt, kp, vp, sinkb, o, lse, do)


PAIR = 2 * HEAD_DIM
N_PAIRS = (ATTN_HEADS + ATTN_KV) // 2


def _lanes():
    return lax.broadcasted_iota(jnp.int32, (1, PAIR), 1)


def _swap32(v):
    first_half = (_lanes() & (HEAD_DIM // 2)) == 0
    return jnp.where(first_half, pltpu.roll(v, PAIR - HEAD_DIM // 2, 1), pltpu.roll(v, HEAD_DIM // 2, 1))


def _head_mean(v):
    r = lax.broadcasted_iota(jnp.int32, (PAIR, PAIR), 0)
    c = lax.broadcasted_iota(jnp.int32, (PAIR, PAIR), 1)
    same = jnp.where((r >= HEAD_DIM) == (c >= HEAD_DIM), 1.0, 0.0).astype(BF16)
    return _nn3r(v, same) * (1.0 / HEAD_DIM)


def _qk_tile_fwd(pa, g_ref, cosv, sinv, q_ref, k_ref, v_ref):
    qw = ATTN_HEADS * HEAD_DIM
    for p in range(N_PAIRS):
        xv = pa[:, p * PAIR:(p + 1) * PAIR]
        n = xv * lax.rsqrt(_head_mean(xv * xv) + EPS) * g_ref[p]
        y = n * cosv + _swap32(n) * sinv
        if p < N_PAIRS - 1:
            q_ref[:, p * PAIR:(p + 1) * PAIR] = _bf(y * HEAD_DIM ** -0.5)
        else:
            k_ref[...] = _bf(y)
    v_ref[...] = _bf(pa[:, qw + PAIR:])


def _qk_tile_bwd(dq_ref, dk_ref, pa_ref, g_ref, cosv, sinv):
    dxs, dgs = [], []
    for p in range(N_PAIRS):
        sl = slice(p * PAIR, (p + 1) * PAIR)
        xv = pa_ref[:, sl]
        r = lax.rsqrt(_head_mean(xv * xv) + EPS)
        xhat = xv * r
        dy = dq_ref[:, sl] * HEAD_DIM ** -0.5 if p < N_PAIRS - 1 else dk_ref[...]
        dn = dy * cosv + _swap32(dy * sinv)
        dgs.append(jnp.sum(dn * xhat, axis=0, keepdims=True))
        dxh = dn * g_ref[p]
        dxs.append(r * (dxh - xhat * _head_mean(dxh * xhat)))
    return jnp.concatenate(dxs, axis=1), dgs


def _qk_slab_fwd(pa, gains, cosp, sinp, tm, name):
    T = pa.shape[0]
    qw = ATTN_HEADS * HEAD_DIM

    def body(pa_ref, g_ref, c_ref, s_ref, q_ref, k_ref, v_ref):
        cosv, sinv = c_ref[...], s_ref[...]
        for p in range(N_PAIRS):
            xv = pa_ref[:, p * PAIR:(p + 1) * PAIR]
            n = xv * lax.rsqrt(_head_mean(xv * xv) + EPS) * g_ref[p]
            y = n * cosv + _swap32(n) * sinv
            if p < N_PAIRS - 1:
                q_ref[:, p * PAIR:(p + 1) * PAIR] = _bf(y * HEAD_DIM ** -0.5)
            else:
                k_ref[...] = _bf(y)
        v_ref[...] = _bf(pa_ref[:, qw + PAIR:])

    return pl.pallas_call(
        body, name=name, grid=(T // tm,),
        in_specs=[_rows(tm, pa.shape[1]), _full(gains.shape), _rows(tm, PAIR), _rows(tm, PAIR)],
        out_specs=[_rows(tm, qw), _rows(tm, PAIR), _rows(tm, PAIR)],
        out_shape=[jax.ShapeDtypeStruct((T, qw), ACT), jax.ShapeDtypeStruct((T, PAIR), ACT),
                   jax.ShapeDtypeStruct((T, PAIR), ACT)],
        compiler_params=_cp("arbitrary"),
    )(pa, gains, cosp, sinp)


def _qk_slab_bwd(dq, dk, pa, gains, cosp, sinp, tm, name):
    T = pa.shape[0]
    qw = ATTN_HEADS * HEAD_DIM

    def body(dq_ref, dk_ref, pa_ref, g_ref, c_ref, s_ref, dx_ref, dg_ref):
        i = pl.program_id(0)
        cosv, sinv = c_ref[...], s_ref[...]
        for p in range(N_PAIRS):
            sl = slice(p * PAIR, (p + 1) * PAIR)
            xv = pa_ref[:, sl]
            r = lax.rsqrt(_head_mean(xv * xv) + EPS)
            xhat = xv * r
            dy = dq_ref[:, sl] * HEAD_DIM ** -0.5 if p < N_PAIRS - 1 else dk_ref[...]
            dn = dy * cosv + _swap32(dy * sinv)
            _acc_all(dg_ref.at[p], i, jnp.sum(dn * xhat, axis=0, keepdims=True))
            dxh = dn * g_ref[p]
            dx_ref[:, sl] = r * (dxh - xhat * _head_mean(dxh * xhat))

    return pl.pallas_call(
        body, name=name, grid=(T // tm,),
        in_specs=[_rows(tm, qw), _rows(tm, PAIR), _rows(tm, qw + PAIR), _full(gains.shape), _rows(tm, PAIR), _rows(tm, PAIR)],
        out_specs=[_rows(tm, qw + PAIR), _whole(gains.shape)],
        out_shape=[jax.ShapeDtypeStruct((T, qw + PAIR), F32), jax.ShapeDtypeStruct(gains.shape, F32)],
        compiler_params=_cp("arbitrary"),
    )(dq, dk, pa, gains, cosp, sinp)


def _attn_window(ref, i, nb):
    blk = ATTN_BLOCK
    starts = [pl.multiple_of(jnp.clip(i + d, 0, nb - 1) * blk, blk) for d in (-1, 0, 1)]
    return starts, jnp.concatenate([ref[pl.ds(s, blk), :] for s in starts], axis=0)


GROUP_HEADS = 2
ATTN_STEP_BLOCKS = 2


def _head_groups(n):
    g = ATTN_HEADS // ATTN_KV
    return [(kv, [kv * g + s + j for j in range(n)]) for kv in range(ATTN_KV) for s in range(0, g, n)]


def _attn_mask(i, lc, T, rows):
    blk = ATTN_BLOCK
    row = lax.broadcasted_iota(jnp.int32, (rows, 1), 0)
    qpos = i * blk + (row & (blk - 1))
    kpos = (i - 1) * blk + lax.broadcasted_iota(jnp.int32, (1, 3 * blk), 1)
    return (qpos >= lc) & (kpos >= lc) & (kpos < T) & (jnp.abs(kpos - qpos) <= WINDOW)


def _to_kv_half(v, head, kv):
    return v if head % 2 == kv else pltpu.roll(v, HEAD_DIM, 1)


def _attn_slab_fwd(qt, ks, vs, sinkb, lc, name, ex=None):
    T = qt.shape[0]
    blk = ATTN_BLOCK
    nb = T // blk
    g = ATTN_HEADS // ATTN_KV

    spb = ATTN_STEP_BLOCKS
    ng = nb // spb

    def one_block(i, rows, q_ref, k_ref, v_ref, sink_ref, o_ref, lse_ref):
        lane = _lanes()
        valid = _attn_mask(i, lc, T, GROUP_HEADS * blk)
        kc_all, vc = k_ref[0:lc, :], v_ref[0:lc, :]
        _, kw_all = _attn_window(k_ref, i, nb)
        _, vw = _attn_window(v_ref, i, nb)
        kc, kw = [], []
        for kv in range(ATTN_KV):
            mine = (lane >= kv * HEAD_DIM) & (lane < (kv + 1) * HEAD_DIM)
            kc.append(jnp.where(mine, kc_all, jnp.zeros_like(kc_all)))
            kw.append(jnp.where(mine, kw_all, jnp.zeros_like(kw_all)))
        groups = _head_groups(GROUP_HEADS)
        qg = [jnp.concatenate([_to_kv_half(q_ref[rows, (h // 2) * PAIR:(h // 2 + 1) * PAIR], h, kv) for h in heads], axis=0)
              for kv, heads in groups]
        sinks = [sink_ref[kv, (heads[0] - kv * g) * blk:(heads[-1] + 1 - kv * g) * blk] for kv, heads in groups]
        s_c = [_nt(q, kc[kv]) for q, (kv, _) in zip(qg, groups)]
        s_w = [jnp.where(valid, _nt(q, kw[kv]), NEG) for q, (kv, _) in zip(qg, groups)]
        m = [jnp.maximum(jnp.maximum(jnp.max(a, axis=-1, keepdims=True), jnp.max(b, axis=-1, keepdims=True)), s)
             for a, b, s in zip(s_c, s_w, sinks)]
        e_c = [jnp.exp(a - mm) for a, mm in zip(s_c, m)]
        e_w = [jnp.exp(b - mm) for b, mm in zip(s_w, m)]
        den = [jnp.exp(s - mm) + jnp.sum(a, axis=-1, keepdims=True) + jnp.sum(b, axis=-1, keepdims=True)
               for s, mm, a, b in zip(sinks, m, e_c, e_w)]
        inv = [1.0 / d for d in den]
        og = [_nn(_bf(a * r), vc) + _nn(_bf(b * r), vw) for a, b, r in zip(e_c, e_w, inv)]
        placed = [None] * ATTN_HEADS
        for (kv, heads), o2, mm, d in zip(groups, og, m, den):
            lse_ref[heads[0]:heads[-1] + 1, rows, :] = (mm + jnp.log(d)).reshape(len(heads), blk, 1)
            for j, h in enumerate(heads):
                placed[h] = _to_kv_half(o2[j * blk:(j + 1) * blk], h, kv)
        for p in range(ATTN_HEADS // 2):
            o_ref[rows, p * PAIR:(p + 1) * PAIR] = jnp.where(lane < HEAD_DIM, placed[2 * p], placed[2 * p + 1])

    def body(*refs):
        for j in range(spb):
            one_block(pl.program_id(0) * spb + j, pl.ds(j * blk, blk), *refs)

    qw = ATTN_HEADS * HEAD_DIM
    return _host_call(
        body, ex, lambda: pl.program_id(0) == 0, lambda: pl.program_id(0) == ng - 1,
        name=name, grid=(ng,),
        in_specs=[_rows(spb * blk, qw), _full((T, PAIR)), _full((T, PAIR)), _full(sinkb.shape)],
        out_specs=[_rows(spb * blk, qw), pl.BlockSpec((ATTN_HEADS, spb * blk, 1), lambda i: (0, i, 0))],
        out_shape=[jax.ShapeDtypeStruct((T, qw), F32), jax.ShapeDtypeStruct((ATTN_HEADS, T, 1), F32)],
        scratch_shapes=[], sem=("arbitrary",), args=(qt, ks, vs, sinkb))


def _attn_slab_bwd(qt, ks, vs, sinkb, o, lse, do, lc, name, ex=None):
    T = qt.shape[0]
    blk = ATTN_BLOCK
    nb = T // blk
    g = ATTN_HEADS // ATTN_KV

    spb = ATTN_STEP_BLOCKS
    ng = nb // spb

    def body(*refs):
        dk_ref, dv_ref, ds_ref = refs[8:11]

        @pl.when(pl.program_id(0) == 0)
        def _():
            dk_ref[...] = jnp.zeros_like(dk_ref)
            dv_ref[...] = jnp.zeros_like(dv_ref)
            ds_ref[...] = jnp.zeros_like(ds_ref)

        for j in range(spb):
            one_block(pl.program_id(0) * spb + j, pl.ds(j * blk, blk), *refs)

    def one_block(i, rows, q_ref, k_ref, v_ref, sink_ref, o_ref, lse_ref, do_ref, dq_ref, dk_ref, dv_ref, ds_ref):
        lane = _lanes()
        valid = _attn_mask(i, lc, T, g * blk)
        kc_all, vc_all = k_ref[0:lc, :], v_ref[0:lc, :]
        starts, kw_all = _attn_window(k_ref, i, nb)
        _, vw_all = _attn_window(v_ref, i, nb)
        dq_pairs = [jnp.zeros((blk, PAIR), F32) for _ in range(ATTN_HEADS // 2)]
        for kv in range(ATTN_KV):
            mine = (lane >= kv * HEAD_DIM) & (lane < (kv + 1) * HEAD_DIM)

            def only(v):
                return jnp.where(mine, v, jnp.zeros_like(v))

            kc, kw, vc, vw = only(kc_all), only(kw_all), only(vc_all), only(vw_all)
            heads = [kv * g + j for j in range(g)]
            qs, dos, deltas = [], [], []
            for h in heads:
                sl = slice((h // 2) * PAIR, (h // 2 + 1) * PAIR)
                dov = do_ref[rows, sl]
                qs.append(_to_kv_half(q_ref[rows, sl], h, kv))
                dos.append(_bf(_to_kv_half(dov, h, kv)))
                own = (lane < HEAD_DIM) if h % 2 == 0 else (lane >= HEAD_DIM)
                deltas.append(jnp.sum(jnp.where(own, dov * o_ref[rows, sl], 0.0), axis=-1, keepdims=True))
            q4, do4, delta = jnp.concatenate(qs, axis=0), jnp.concatenate(dos, axis=0), jnp.concatenate(deltas, axis=0)
            sink = sink_ref[kv]
            lse = lse_ref[kv * g:(kv + 1) * g, rows, :].reshape(g * blk, 1)
            p_c = jnp.exp(_nt(q4, kc) - lse)
            p_w = jnp.exp(jnp.where(valid, _nt(q4, kw), NEG) - lse)
            ds_c = _bf(p_c * (_nt(do4, vc) - delta))
            ds_w = _bf(p_w * (_nt(do4, vw) - delta))
            dsr = -jnp.exp(sink - lse) * delta
            dq4 = _nn(ds_c, kc) + _nn(ds_w, kw)
            for j, h in enumerate(heads):
                ds_ref[h:h + 1, :] += jnp.sum(dsr[j * blk:(j + 1) * blk, :], axis=0, keepdims=True)
                dq_pairs[h // 2] = dq_pairs[h // 2] + _to_kv_half(dq4[j * blk:(j + 1) * blk], h, kv)
            dk_ref[0:lc, :] += only(_tn(ds_c, q4))
            dv_ref[0:lc, :] += only(_tn(_bf(p_c), do4))
            dkw = only(_tn(ds_w, q4))
            dvw = only(_tn(_bf(p_w), do4))
            for b, s in enumerate(starts):
                dk_ref[pl.ds(s, blk), :] += dkw[b * blk:(b + 1) * blk]
                dv_ref[pl.ds(s, blk), :] += dvw[b * blk:(b + 1) * blk]
        for p in range(ATTN_HEADS // 2):
            dq_ref[rows, p * PAIR:(p + 1) * PAIR] = dq_pairs[p]

    qw = ATTN_HEADS * HEAD_DIM
    lspec = pl.BlockSpec((ATTN_HEADS, spb * blk, 1), lambda i: (0, i, 0))
    return _host_call(
        body, ex, lambda: pl.program_id(0) == 0, lambda: pl.program_id(0) == ng - 1,
        name=name, grid=(ng,),
        in_specs=[_rows(spb * blk, qw), _full((T, PAIR)), _full((T, PAIR)), _full(sinkb.shape), _rows(spb * blk, qw), lspec,
                  _rows(spb * blk, qw)],
        out_specs=[_rows(spb * blk, qw), _whole((T, PAIR)), _whole((T, PAIR)), _whole((ATTN_HEADS, 1))],
        out_shape=[jax.ShapeDtypeStruct((T, qw), F32), jax.ShapeDtypeStruct((T, PAIR), F32),
                   jax.ShapeDtypeStruct((T, PAIR), F32), jax.ShapeDtypeStruct((ATTN_HEADS, 1), F32)],
        scratch_shapes=[], sem=("arbitrary",), args=(qt, ks, vs, sinkb, o, lse, do))


def _fw_chunk(s, nc, nt):
    return s


def _bw_chunk(s, nc, nt):
    return jnp.where(s < nc, nc - 1 - s, nt - 1 - (s - nc))


def _tri(c, rev):
    r = lax.broadcasted_iota(jnp.int32, (c, c), 0)
    k = lax.broadcasted_iota(jnp.int32, (c, c), 1)
    return (k >= r) if rev else (k <= r)


def _gla_gates(z, lb, rev):
    c = HG_CHUNK
    sg = _sig(z)
    f = lb + (1.0 - lb) * sg
    cum = _nn3(jnp.where(_tri(c, rev), 1.0, 0.0).astype(BF16), jnp.log(f))
    mid = c - 1 - c // 2 if rev else c // 2
    last = 0 if rev else c - 1
    return sg, f, cum, cum[mid:mid + 1], cum[last:last + 1], last


def _lower_bound(lbraw_ref):
    lr = lbraw_ref[...]
    return _sig(lr[0:1] - lr[1:2])


def _gla_fwd(pb, lbraw, lc, name, ex=None):
    T = pb.shape[0]
    c, hw, d, ns = HG_CHUNK, HG_HEADS * HG_D, HG_D, HG_STEP_CHUNKS
    nt, nc = T // (ns * c), lc // (ns * c)
    orders = (_fw_chunk, _bw_chunk)

    def body(qf, zf, vf, qb, zb, vb, lb_ref, of_ref, ob_ref, sf_ref, sb_ref, st_ref):
        @pl.when(pl.program_id(0) == 0)
        def _():
            st_ref[...] = jnp.zeros_like(st_ref)

        lb = _lower_bound(lb_ref)
        dirs = ((qf, zf, vf, of_ref, sf_ref), (qb, zb, vb, ob_ref, sb_ref))
        combos = [(dr, h, slice(h * d, (h + 1) * d)) for dr in range(2) for h in range(HG_HEADS)]
        for j in range(ns):
            sub = (j, ns - 1 - j)
            rows = [pl.ds(sub[dr] * c, c) for dr in range(2)]
            prep = []
            for dr, (q_ref, z_ref, v_ref, _, _) in enumerate(dirs):
                rev = dr == 1
                qr = q_ref[rows[dr], :]
                q = qr * _sig(qr)
                _, f, cum, ref, last, _ = _gla_gates(z_ref[rows[dr], :], lb, rev)
                k = 1.0 - f
                prep.append(dict(q1=_bf(q * jnp.exp(cum - ref)), k1=_bf(k * jnp.exp(ref - cum)), q2=_bf(q * jnp.exp(cum)),
                                 k2=_bf(k * jnp.exp(last - cum)), el=jnp.exp(last), v=_bf(v_ref[rows[dr], :]),
                                 mask=_tri(c, rev)))
            a = [_bf(jnp.where(prep[dr]["mask"], _nt(prep[dr]["q1"][:, sl], prep[dr]["k1"][:, sl]), 0.0))
                 for dr, _, sl in combos]
            for (dr, h, sl), a_h in zip(combos, a):
                p = prep[dr]
                o_ref, s_ref = dirs[dr][3], dirs[dr][4]
                st = st_ref[dr, h]
                stb = _bf(st)
                s_ref[sub[dr], h] = stb
                o_ref[rows[dr], sl] = _nn(a_h, p["v"][:, sl]) + _nt(p["q2"][:, sl], stb)
                st_ref[dr, h] = st * p["el"][:, sl] + _tn(p["v"][:, sl], p["k2"][:, sl])

    def col(order, blkcol):
        return pl.BlockSpec((ns * c, hw), lambda s: (order(s, nc, nt), blkcol))

    def st_spec(order):
        return pl.BlockSpec((ns, HG_HEADS, d, d), lambda s: (order(s, nc, nt), 0, 0, 0))

    in_specs = []
    for dr, order in enumerate(orders):
        in_specs += [col(order, 0), col(order, 1 + dr), col(order, 3)]
    in_specs.append(_full(lbraw.shape))
    return _host_call(
        body, ex, lambda: pl.program_id(0) == 0, lambda: pl.program_id(0) == nt - 1,
        name=name, grid=(nt,), in_specs=in_specs,
        out_specs=[col(_fw_chunk, 0), col(_bw_chunk, 0), st_spec(_fw_chunk), st_spec(_bw_chunk)],
        out_shape=[jax.ShapeDtypeStruct((T, hw), F32), jax.ShapeDtypeStruct((T, hw), F32),
                   jax.ShapeDtypeStruct((nt * ns, HG_HEADS, d, d), ACT), jax.ShapeDtypeStruct((nt * ns, HG_HEADS, d, d), ACT)],
        scratch_shapes=[pltpu.VMEM((2, HG_HEADS, d, d), F32)], sem=("arbitrary",),
        args=(pb, pb, pb, pb, pb, pb, lbraw))


def _gla_bwd(pb, lbraw, s_fw, s_bw, do, lc, name, ex=None):
    T = pb.shape[0]
    c, hw, d, ns = HG_CHUNK, HG_HEADS * HG_D, HG_D, HG_STEP_CHUNKS
    nt, nc = T // (ns * c), lc // (ns * c)

    def rfw(s, nc_, nt_):
        return _fw_chunk(nt_ - 1 - s, nc_, nt_)

    def rbw(s, nc_, nt_):
        return _bw_chunk(nt_ - 1 - s, nc_, nt_)

    def body(qf, zf, vf, sf, dof, qb, zb, vb, sb, dob_, lb_ref,
             dqf, dzf, dvf, dqb, dzb, dvb, dlb_ref, dst_ref):
        step = pl.program_id(0)

        @pl.when(step == 0)
        def _():
            dst_ref[...] = jnp.zeros_like(dst_ref)

        lb = _lower_bound(lb_ref)
        sets = ((qf, zf, vf, sf, dof, dqf, dzf, dvf), (qb, zb, vb, sb, dob_, dqb, dzb, dvb))
        combos = [(dr, h, slice(h * d, (h + 1) * d)) for dr in range(2) for h in range(HG_HEADS)]
        dlb_tot = jnp.zeros((1, hw), F32)
        for j in range(ns):
            sub = (ns - 1 - j, j)
            rows = [pl.ds(sub[dr] * c, c) for dr in range(2)]
            prep = []
            for dr, (q_ref, z_ref, v_ref, _, do_ref, _, _, _) in enumerate(sets):
                rev = dr == 1
                qr = q_ref[rows[dr], :]
                sq = _sig(qr)
                q = qr * sq
                sg, f, cum, ref, last, last_row = _gla_gates(z_ref[rows[dr], :], lb, rev)
                k = 1.0 - f
                e_qr, e_kr, e_q, e_kl = jnp.exp(cum - ref), jnp.exp(ref - cum), jnp.exp(cum), jnp.exp(last - cum)
                q1, k1, q2, k2 = q * e_qr, k * e_kr, q * e_q, k * e_kl
                prep.append(dict(qr=qr, sq=sq, sg=sg, f=f, e_qr=e_qr, e_kr=e_kr, e_q=e_q, e_kl=e_kl, el=jnp.exp(last),
                                 q1=q1, k1=k1, q2=q2, k2=k2, q1b=_bf(q1), k1b=_bf(k1), q2b=_bf(q2), k2b=_bf(k2),
                                 vb=_bf(v_ref[rows[dr], :]), dob=_bf(do_ref[rows[dr], :]), mask=_tri(c, rev),
                                 last_row=last_row, acc_t=jnp.where(_tri(c, not rev), 1.0, 0.0).astype(BF16)))
            a = [_bf(jnp.where(prep[dr]["mask"], _nt(prep[dr]["q1b"][:, sl], prep[dr]["k1b"][:, sl]), 0.0))
                 for dr, _, sl in combos]
            da = [_bf(jnp.where(prep[dr]["mask"], _nt(prep[dr]["dob"][:, sl], prep[dr]["vb"][:, sl]), 0.0))
                  for dr, _, sl in combos]
            parts = [dict(dq1=[], dk1=[], dq2=[], dk2=[], dls=[]) for _ in range(2)]
            for (dr, h, sl), a_h, da_h in zip(combos, a, da):
                p = prep[dr]
                s_ref, dv_ref = sets[dr][3], sets[dr][7]
                stb = s_ref[sub[dr], h]
                dst = dst_ref[dr, h]
                dstb = _bf(dst)
                dob_h, vb_h = p["dob"][:, sl], p["vb"][:, sl]
                dv_ref[rows[dr], sl] = _bf(_tn(a_h, dob_h) + _nt(p["k2b"][:, sl], dstb))
                parts[dr]["dq1"].append(_nn(da_h, p["k1b"][:, sl]))
                parts[dr]["dk1"].append(_tn(da_h, p["q1b"][:, sl]))
                parts[dr]["dq2"].append(_nn(dob_h, stb))
                parts[dr]["dk2"].append(_nn(vb_h, dstb))
                el_h = p["el"][:, sl]
                dst_ref[dr, h] = _tn(dob_h, p["q2b"][:, sl]) + dst * el_h
                parts[dr]["dls"].append(jnp.sum(dst * stb.astype(F32), axis=0, keepdims=True) * el_h)
            for dr in range(2):
                p = prep[dr]
                dq_ref, dz_ref = sets[dr][5], sets[dr][6]
                dq1, dk1, dq2, dk2, dls = (jnp.concatenate(parts[dr][n], axis=1) for n in ("dq1", "dk1", "dq2", "dk2", "dls"))
                dq = dq1 * p["e_qr"] + dq2 * p["e_q"]
                dk = dk1 * p["e_kr"] + dk2 * p["e_kl"]
                dcum = dq1 * p["q1"] - dk1 * p["k1"] + dq2 * p["q2"] - dk2 * p["k2"]
                dlast = jnp.sum(dk2 * p["k2"], axis=0, keepdims=True) + dls
                rowid = lax.broadcasted_iota(jnp.int32, (c, 1), 0)
                dcum = dcum + jnp.where(rowid == p["last_row"], dlast, 0.0)
                df = _nn3(p["acc_t"], dcum) / p["f"] - dk
                sg = p["sg"]
                dz_ref[rows[dr], :] = _bf(df * (1.0 - lb) * sg * (1.0 - sg))
                dlb_tot = dlb_tot + jnp.sum(df * (1.0 - sg), axis=0, keepdims=True)
                dq_ref[rows[dr], :] = _bf(dq * (p["sq"] * (1.0 + p["qr"] * (1.0 - p["sq"]))))
        _acc_all(dlb_ref, step, dlb_tot)

    def col(order, blkcol):
        return pl.BlockSpec((ns * c, hw), lambda s: (order(s, nc, nt), blkcol))

    def st_spec(order):
        return pl.BlockSpec((ns, HG_HEADS, d, d), lambda s: (order(s, nc, nt), 0, 0, 0))

    in_specs = []
    for dr, order in enumerate((rfw, rbw)):
        in_specs += [col(order, 0), col(order, 1 + dr), col(order, 3), st_spec(order), col(order, 0)]
    in_specs.append(_full(lbraw.shape))
    out_specs = [col(rfw, 0)] * 3 + [col(rbw, 0)] * 3 + [_whole((1, hw))]
    out_shape = [jax.ShapeDtypeStruct((T, hw), ACT)] * 6 + [jax.ShapeDtypeStruct((1, hw), F32)]
    return _host_call(
        body, ex, lambda: pl.program_id(0) == 0, lambda: pl.program_id(0) == nt - 1,
        name=name, grid=(nt,), in_specs=in_specs, out_specs=out_specs, out_shape=out_shape,
        scratch_shapes=[pltpu.VMEM((2, HG_HEADS, d, d), F32)], sem=("arbitrary",),
        args=(pb, pb, pb, s_fw, do, pb, pb, pb, s_bw, do, lbraw))


def _ret_log_gamma(h, rev):
    hh = RET_HEADS - 1 - h if rev else h
    return math.log(1.0 - 2.0 ** (-5.0 - hh))


def _rope(x, cos, sin):
    half = x.shape[1] // 2
    x1, x2 = x[:, :half], x[:, half:]
    return jnp.concatenate([x1 * cos - x2 * sin, x2 * cos + x1 * sin], axis=1)


def _unrope(dy, cos, sin):
    half = dy.shape[1] // 2
    d1, d2 = dy[:, :half], dy[:, half:]
    return jnp.concatenate([d1 * cos + d2 * sin, d2 * cos - d1 * sin], axis=1)


def _ret_decays(lg, rev):
    c = RET_CHUNK
    r = lax.broadcasted_iota(jnp.int32, (c, c), 0)
    k = lax.broadcasted_iota(jnp.int32, (c, c), 1)
    rel = (k - r) if rev else (r - k)
    dm = jnp.where(rel >= 0, jnp.exp(lg * jnp.maximum(rel, 0).astype(F32)), 0.0)
    pos = lax.broadcasted_iota(jnp.int32, (c, 1), 0).astype(F32)
    if rev:
        qdec = jnp.exp(lg * (c - pos))
        kdec = jnp.exp(lg * pos)
    else:
        qdec = jnp.exp(lg * (pos + 1.0))
        kdec = jnp.exp(lg * (c - 1.0 - pos))
    return dm, qdec, kdec


def _ret_fwd(q, k, v, cos, sin, lc, name, ex=None):
    T = q.shape[0]
    c, dk, dv = RET_CHUNK, RET_DK, RET_DV
    nt, nc = T // c, lc // c
    kscale = dk ** -0.5

    def body(qf, kf, vf, cf, sf_, qb, kb, vb, cb, sb_, of_ref, ob_ref, stf_ref, stb_ref, st_ref):
        @pl.when(pl.program_id(0) == 0)
        def _():
            st_ref[...] = jnp.zeros_like(st_ref)

        sets = ((qf, kf, vf, cf, sf_, of_ref, stf_ref), (qb, kb, vb, cb, sb_, ob_ref, stb_ref))
        combos = [(dr, h) for dr in range(2) for h in range(RET_HEADS)]
        prep = {}
        for dr, (q_ref, k_ref, v_ref, c_ref, s_ref, _, _) in enumerate(sets):
            rev = dr == 1
            cos_v, sin_v = c_ref[...], s_ref[...]
            for h in range(RET_HEADS):
                lg = _ret_log_gamma(h, rev)
                dm, qdec, kdec = _ret_decays(lg, rev)
                qh = _rope(q_ref[:, h * dk:(h + 1) * dk].astype(F32), cos_v, sin_v)
                kh = _rope(k_ref[:, h * dk:(h + 1) * dk].astype(F32), cos_v, sin_v) * kscale
                prep[dr, h] = dict(qb=_bf(qh), kb=_bf(kh), qin=_bf(qh * qdec), kin=_bf(kh * kdec),
                                   v=_bf(v_ref[:, h * dv:(h + 1) * dv]), dm=dm, decay=math.exp(lg * c))
        sc = {ch: _bf(_nt(prep[ch]["qb"], prep[ch]["kb"]) * prep[ch]["dm"]) for ch in combos}
        for dr, h in combos:
            p = prep[dr, h]
            o_ref, so_ref = sets[dr][5], sets[dr][6]
            st = st_ref[dr, h]
            stb = _bf(st)
            so_ref[0, h] = stb
            o_ref[:, h * dv:(h + 1) * dv] = _bf(_nn(sc[dr, h], p["v"]) + _nt(p["qin"], stb))
            st_ref[dr, h] = st * p["decay"] + _tn(p["v"], p["kin"])

    def spec(order, width):
        return pl.BlockSpec((c, width), lambda s: (order(s, nc, nt), 0))

    def st_spec(order):
        return pl.BlockSpec((1, RET_HEADS, dv, dk), lambda s: (order(s, nc, nt), 0, 0, 0))

    in_specs = []
    for order in (_fw_chunk, _bw_chunk):
        in_specs += [spec(order, RET_HEADS * dk), spec(order, RET_HEADS * dk), spec(order, RET_HEADS * dv),
                     spec(order, dk // 2), spec(order, dk // 2)]
    return _host_call(
        body, ex, lambda: pl.program_id(0) == 0, lambda: pl.program_id(0) == nt - 1,
        name=name, grid=(nt,), in_specs=in_specs,
        out_specs=[spec(_fw_chunk, RET_HEADS * dv), spec(_bw_chunk, RET_HEADS * dv), st_spec(_fw_chunk), st_spec(_bw_chunk)],
        out_shape=[jax.ShapeDtypeStruct((T, RET_HEADS * dv), ACT), jax.ShapeDtypeStruct((T, RET_HEADS * dv), ACT),
                   jax.ShapeDtypeStruct((nt, RET_HEADS, dv, dk), ACT), jax.ShapeDtypeStruct((nt, RET_HEADS, dv, dk), ACT)],
        scratch_shapes=[pltpu.VMEM((2, RET_HEADS, dv, dk), F32)], sem=("arbitrary",),
        args=(q, k, v, cos, sin, q, k, v, cos, sin))


def _ret_bwd(q, k, v, cos, sin, s_fw, s_bw, do, lc, name, ex=None):
    T = q.shape[0]
    c, dk, dv = RET_CHUNK, RET_DK, RET_DV
    nt, nc = T // c, lc // c
    kscale = dk ** -0.5

    def rfw(s, nc_, nt_):
        return _fw_chunk(nt_ - 1 - s, nc_, nt_)

    def rbw(s, nc_, nt_):
        return _bw_chunk(nt_ - 1 - s, nc_, nt_)

    def body(qf, kf, vf, cf, sf_, stf, dof, qb, kb, vb, cb, sb_, stb_, dob_,
             dqf, dkf, dvf, dqb, dkb, dvb, dst_ref):
        @pl.when(pl.program_id(0) == 0)
        def _():
            dst_ref[...] = jnp.zeros_like(dst_ref)

        sets = ((qf, kf, vf, cf, sf_, stf, dof, dqf, dkf, dvf), (qb, kb, vb, cb, sb_, stb_, dob_, dqb, dkb, dvb))
        combos = [(dr, h) for dr in range(2) for h in range(RET_HEADS)]
        prep = {}
        for dr, (q_ref, k_ref, v_ref, c_ref, s_ref, _, do_ref, _, _, _) in enumerate(sets):
            rev = dr == 1
            cos_v, sin_v = c_ref[...], s_ref[...]
            for h in range(RET_HEADS):
                lg = _ret_log_gamma(h, rev)
                dm, qdec, kdec = _ret_decays(lg, rev)
                qh = _rope(q_ref[:, h * dk:(h + 1) * dk].astype(F32), cos_v, sin_v)
                kh = _rope(k_ref[:, h * dk:(h + 1) * dk].astype(F32), cos_v, sin_v) * kscale
                prep[dr, h] = dict(qb=_bf(qh), kb=_bf(kh), qin=_bf(qh * qdec), kin=_bf(kh * kdec),
                                   v=_bf(v_ref[:, h * dv:(h + 1) * dv]), dob=_bf(do_ref[:, h * dv:(h + 1) * dv]),
                                   dm=dm, qdec=qdec, kdec=kdec, decay=math.exp(lg * c), cos=cos_v, sin=sin_v)
        sc = {ch: _bf(_nt(prep[ch]["qb"], prep[ch]["kb"]) * prep[ch]["dm"]) for ch in combos}
        dsc = {ch: _bf(_nt(prep[ch]["dob"], prep[ch]["v"]) * prep[ch]["dm"]) for ch in combos}
        carried = {}
        for dr, h in combos:
            p = prep[dr, h]
            dv_ref = sets[dr][9]
            dst = dst_ref[dr, h]
            dstb = _bf(dst)
            carried[dr, h] = dstb
            dv_ref[:, h * dv:(h + 1) * dv] = _bf(_tn(sc[dr, h], p["dob"]) + _nt(p["kin"], dstb))
            dst_ref[dr, h] = _tn(p["dob"], p["qin"]) + dst * p["decay"]
        for dr, h in combos:
            p = prep[dr, h]
            st_in, dq_ref, dk_ref = sets[dr][5], sets[dr][7], sets[dr][8]
            dq_r = _nn(dsc[dr, h], p["kb"]) + _nn(p["dob"], st_in[0, h]) * p["qdec"]
            dk_r = _tn(dsc[dr, h], p["qb"]) + _nn(p["v"], carried[dr, h]) * p["kdec"]
            dq_ref[:, h * dk:(h + 1) * dk] = _bf(_unrope(dq_r, p["cos"], p["sin"]))
            dk_ref[:, h * dk:(h + 1) * dk] = _bf(_unrope(dk_r * kscale, p["cos"], p["sin"]))

    def spec(order, width):
        return pl.BlockSpec((c, width), lambda s: (order(s, nc, nt), 0))

    def st_spec(order):
        return pl.BlockSpec((1, RET_HEADS, dv, dk), lambda s: (order(s, nc, nt), 0, 0, 0))

    in_specs = []
    for order in (rfw, rbw):
        in_specs += [spec(order, RET_HEADS * dk), spec(order, RET_HEADS * dk), spec(order, RET_HEADS * dv),
                     spec(order, dk // 2), spec(order, dk // 2), st_spec(order), spec(order, RET_HEADS * dv)]
    out_specs, out_shape = [], []
    for order in (rfw, rbw):
        out_specs += [spec(order, RET_HEADS * dk), spec(order, RET_HEADS * dk), spec(order, RET_HEADS * dv)]
        out_shape += [jax.ShapeDtypeStruct((T, RET_HEADS * dk), ACT), jax.ShapeDtypeStruct((T, RET_HEADS * dk), ACT),
                      jax.ShapeDtypeStruct((T, RET_HEADS * dv), ACT)]
    return _host_call(
        body, ex, lambda: pl.program_id(0) == 0, lambda: pl.program_id(0) == nt - 1,
        name=name, grid=(nt,), in_specs=in_specs, out_specs=out_specs, out_shape=out_shape,
        scratch_shapes=[pltpu.VMEM((2, RET_HEADS, dv, dk), F32)], sem=("arbitrary",),
        args=(q, k, v, cos, sin, s_fw, do, q, k, v, cos, sin, s_bw, do))


def _trig_rows(lc, ang):
    ang = ang.astype(np.float64)
    half = ang.shape[1]
    cos = np.concatenate([np.ones((lc, half)), np.cos(ang)], axis=0).astype(np.float32)
    sin = np.concatenate([np.zeros((lc, half)), np.sin(ang)], axis=0).astype(np.float32)
    return cos, sin


def _attn_rope_tables(lc, l):
    t = np.arange(l)
    row = (t // GRID_W).astype(np.float32)
    colp = (t % GRID_W).astype(np.float32)
    n_freq = HEAD_DIM // 4
    inv = np.float32(10000.0) ** (-np.arange(n_freq, dtype=np.float32) / np.float32(n_freq))
    ang = np.concatenate([row[:, None] * inv, colp[:, None] * inv], axis=-1)
    cos, sin = _trig_rows(lc, ang)
    return jnp.asarray(np.concatenate([cos, cos], axis=1)), jnp.asarray(np.concatenate([-sin, sin], axis=1))


def _ret_rope_tables(lc, l):
    theta = np.float32(1.0) / (np.float32(10000.0) ** np.linspace(0.0, 1.0, RET_DK // 2, dtype=np.float32))
    ang = np.arange(l, dtype=np.float32)[:, None] * theta
    cos, sin = _trig_rows(lc, ang)
    return jnp.asarray(cos), jnp.asarray(sin)


def _heads_major(slab, n_heads):
    t = slab.shape[0]
    return slab.reshape(t, n_heads, HEAD_DIM).transpose(1, 0, 2)


def _slab(hm):
    nh, t, hd = hm.shape
    return hm.transpose(1, 0, 2).reshape(t, nh * hd)


COL_SHARDED = ("ffn_in0", "ffn_in1", "even_in", "even_in_a", "even_in_b", "odd_in")


def _full_weight(name, g):
    if name in COL_SHARDED:
        return g.transpose(1, 0, 2).reshape(g.shape[1], -1)
    return g.reshape(-1, g.shape[2])


def _shard_slots(name, g):
    if name in COL_SHARDED:
        return g.reshape(g.shape[0], N_DEV, -1).transpose(1, 0, 2)
    return g.reshape(N_DEV, -1, g.shape[1])


def _local_step(xs, target, mv, norm_g, w, qk_g, sink, hg_out_g, lbraw, lc, shards=None):
    _, T, dm = _stream(xs)
    l = T - lc
    tm = lc
    blk = ATTN_BLOCK
    d2, d3 = 2 * dm, 3 * dm
    w = dict(w)
    gw, recv = {}, {}

    def ms(layer, a, b):
        return mv[layer, :, :, a:b]

    def gather(names):
        return None if shards is None else _Exchange(GATHER2, [shards[n] for n in names])

    def arrived(names, got):
        for n, g in zip(names, got):
            w[n] = _full_weight(n, g)

    def scatter(names):
        return None if shards is None else _Exchange(SCATTER, [_shard_slots(n, gw[n]) for n in names])

    def scattered(names, got):
        for n, g in zip(names, got):
            recv[n] = g

    g00, g01, g10, g11 = (norm_g[i, j][None, :] for i in (0, 1) for j in (0, 1))

    cos2, sin2 = _attn_rope_tables(lc, l)
    cosp, sinp = jnp.concatenate([cos2, cos2], axis=1), jnp.concatenate([sin2, sin2], axis=1)
    gains5 = jnp.concatenate([jnp.broadcast_to(jnp.tile(qk_g[0], 2), (N_PAIRS - 1, PAIR)), jnp.tile(qk_g[1], 2)[None]])[:, None, :]
    riding = ["even_out"]
    (pa, pb, qt, ks, vs), got = _pre_fwd(xs, g00, ms(0, 0, d2), w["even_in"], ((0, 768), (768, 3328)), tm, "pre0_fwd",
                                         gather(riding), qk=(gains5, cosp, sinp))
    arrived(riding, got)
    sinkb = jnp.broadcast_to(sink.reshape(ATTN_KV, 4, 1, 1), (ATTN_KV, 4, blk, 1)).reshape(ATTN_KV, 4 * blk, 1)
    riding = ["ffn_in0"]
    (a_slab, lse), got = _attn_slab_fwd(qt, ks, vs, sinkb, lc, "attn_fwd", gather(riding))
    arrived(riding, got)
    riding = ["ffn_out0", "odd_out"]
    (hg_of, hg_ob, hg_sf, hg_sb), got = _gla_fwd(pb, lbraw, lc, "hgrn_fwd", gather(riding))
    arrived(riding, got)
    x01, z0, yp0 = _post_fwd(xs, hg_of, hg_ob, pb, 4, hg_out_g, a_slab, w["even_out"], ms(0, d2, d3), HG_D, tm, "post0_fwd")
    riding = ["odd_in"]
    (x02, u0, f0), got = _ffn_fwd(x01, g01, ms(0, d3, 6 * dm), w["ffn_in0"], w["ffn_out0"], tm, "ffn0_fwd", ex=gather(riding))
    arrived(riding, got)

    riding = ["ffn_out1"]
    (rq, rk, rv, rg), got = _pre_fwd(x02, g10, ms(1, 0, d2), w["odd_in"],
                                     ((0, 1024), (1024, 2048), (2048, 4096), (4096, 6144)), tm, "pre1_fwd", gather(riding),
                                     out_dtype=ACT)
    arrived(riding, got)
    rcos, rsin = _ret_rope_tables(lc, l)
    riding = ["ffn_in1"]
    (rt_of, rt_ob, rt_sf, rt_sb), got = _ret_fwd(rq, rk, rv, rcos, rsin, lc, "ret_fwd", gather(riding))
    arrived(riding, got)
    x11, z1, yp1 = _post_fwd(x02, rt_of, rt_ob, rg, 0, None, None, w["odd_out"], ms(1, d2, d3), RET_DV, tm, "post1_fwd")
    (dx, u1, f1, loss), _ = _ffn_fwd(x11, g11, ms(1, d3, 6 * dm), w["ffn_in1"], w["ffn_out1"], tm, "ffn1_fwd", target)

    (dx, h, du, act, df, dms_f1, dg11), _ = _ffn_bwd(x11, dx, u1, f1, g11, ms(1, d3, 6 * dm), w["ffn_in1"], w["ffn_out1"], tm,
                                                     "ffn1_bwd")
    gw["ffn_in1"] = _wgrad(h, du, "wg_ffn_in1")
    gw["ffn_out1"] = _wgrad(act, df, "wg_ffn_out1")
    do1, dgr1, dy1, z1_t, dgate_p1, _ = _post_bwd(dx, z1, yp1, rt_of, rt_ob, rg, 0, None, w["odd_out"], ms(1, d2, d3), 0, RET_DV, tm,
                                                  "post1_bwd")
    gw["odd_out"] = _wgrad(z1_t, dy1, "wg_odd_out")
    riding = ["ffn_in1"]
    (dqf, dkf, dvf, dqb, dkb, dvb), got = _ret_bwd(rq, rk, rv, rcos, rsin, rt_sf, rt_sb, do1, lc, "ret_bwd", scatter(riding))
    scattered(riding, got)
    riding = ["odd_out", "ffn_out1"]
    (dx, h, dp, dms_p1, dg10), got = _pre_bwd(x02, dx, g10, ms(1, 0, d2), w["odd_in"],
                                              [(0, [dqf, dqb]), (1024, [dkf, dkb]), (2048, [dvf, dvb]), (4096, [dgr1])], tm,
                                              "pre1_bwd", ex=scatter(riding))
    scattered(riding, got)
    gw["odd_in"] = _wgrad(h, dp, "wg_odd_in")

    riding = ["odd_in"]
    (dx, h, du, act, df, dms_f0, dg01), got = _ffn_bwd(x01, dx, u0, f0, g01, ms(0, d3, 6 * dm), w["ffn_in0"], w["ffn_out0"], tm,
                                                       "ffn0_bwd", scatter(riding))
    scattered(riding, got)
    gw["ffn_in0"] = _wgrad(h, du, "wg_ffn_in0")
    gw["ffn_out0"] = _wgrad(act, df, "wg_ffn_out0")
    do0, dgr0, da0, dy0, z0_t, dgate_p0, d_hg_gain = _post_bwd(dx, z0, yp0, hg_of, hg_ob, pb, 4, hg_out_g, w["even_out"],
                                                              ms(0, d2, d3), 512, HG_D, tm, "post0_bwd")
    gw["even_out"] = _wgrad(z0_t, dy0, "wg_even_out")
    riding = ["ffn_in0"]
    (hq_f, hz_f, hv_f, hq_b, hz_b, hv_b, dlb), got = _gla_bwd(pb, lbraw, hg_sf, hg_sb, do0, lc, "hgrn_bwd", scatter(riding))
    scattered(riding, got)
    riding = ["even_out", "ffn_out0"]
    (dq_att, dk_att, dv_att, dsink), got = _attn_slab_bwd(qt, ks, vs, sinkb, a_slab, lse, da0, lc, "attn_bwd", scatter(riding))
    scattered(riding, got)
    pieces0 = [(640, [dv_att]), (768, [hq_f, hq_b]), (1280, [hz_f]), (1792, [hz_b]), (2304, [hv_f, hv_b]), (2816, [dgr0])]
    (dx, h, dp, dms_p0, dg00, dgain5), _ = _pre_bwd(xs, dx, g00, ms(0, 0, d2), w["even_in"], pieces0, tm, "pre0_bwd",
                                                    latent_dx=shards is not None,
                                                    qk=(dq_att, dk_att, pa, gains5, cosp, sinp))
    if shards is None:
        gw["even_in"] = _wgrad(h, dp, "wg_even_in")
    else:
        half = dm // 2
        gw["even_in_a"] = _wgrad(h, dp, "wg_even_in_a", rows=(0, half))
        gw["even_in_b"], got = _wgrad(h, dp, "wg_even_in_b", rows=(half, half), ex=scatter(["even_in_a"]))
        scattered(["even_in_a"], got)

    dmv = jnp.stack([jnp.concatenate([dms_p0, dgate_p0, dms_f0], axis=2), jnp.concatenate([dms_p1, dgate_p1, dms_f1], axis=2)])
    small = {
        "dmv": dmv,
        "norm_g": jnp.stack([jnp.stack([dg00[0], dg01[0]]), jnp.stack([dg10[0], dg11[0]])]),
        "qk_g": jnp.stack([jnp.sum(dgain5[:N_PAIRS - 1, 0].reshape(-1, HEAD_DIM), axis=0),
                           jnp.sum(dgain5[N_PAIRS - 1, 0].reshape(-1, HEAD_DIM), axis=0)]),
        "sink": dsink.reshape(ATTN_HEADS),
        "hg_out_g": d_hg_gain[0],
        "lb": dlb[0],
        "loss": loss[0, 0],
    }
    if shards is not None:
        gw = {n: recv.get(n, g) for n, g in gw.items()}
    return loss, dx, gw, small


HBM_SPEC = pl.BlockSpec(memory_space=pltpu.HBM)


def _my_index():
    return 4 * lax.axis_index("x") + 2 * lax.axis_index("y") + lax.axis_index("c")


def _peer(k):
    pos = []
    for axis, bit in (("x", 4), ("y", 2), ("c", 1)):
        a = lax.axis_index(axis)
        pos.append(1 - a if k & bit else a)
    return tuple(pos)


def _peer_index(k):
    px, py, pc = _peer(k)
    return 4 * px + 2 * py + pc


GATHER, SCATTER = "gather", "scatter"
GATHER2 = "gather over ICI once per chip"
SIBLING = 1
OTHER_CHIPS = (2, 4, 6)


class _Exchange:
    def __init__(self, mode, arrays):
        self.mode, self.arrays, self.n = mode, list(arrays), len(arrays)

    def out_shape(self):
        if self.mode in (GATHER, GATHER2):
            return [jax.ShapeDtypeStruct((N_DEV,) + a.shape, a.dtype) for a in self.arrays]
        return [jax.ShapeDtypeStruct(a.shape, a.dtype) for a in self.arrays]

    def specs(self):
        return [HBM_SPEC] * self.n

    def scratch(self):
        return [pltpu.SemaphoreType.DMA((self.n, N_DEV - 1)), pltpu.SemaphoreType.DMA((self.n, N_DEV - 1)),
                pltpu.SemaphoreType.DMA((self.n,))]

    def _copies(self, in_refs, out_refs, send_sems, recv_sems, local_sems, landing):
        me = _my_index()
        local, remote = [], []
        for a, (src, dst) in enumerate(zip(in_refs, out_refs)):
            part = (lambda j, s=src: s) if self.mode == GATHER else (lambda j, s=src: s.at[j])
            local.append(pltpu.make_async_copy(part(me), dst.at[me], local_sems.at[a]))
            for k in range(1, N_DEV):
                pj = _peer_index(k)
                remote.append(pltpu.make_async_remote_copy(
                    src_ref=part(pj), dst_ref=dst.at[pj if landing else me], send_sem=send_sems.at[a, k - 1],
                    recv_sem=recv_sems.at[a, k - 1], device_id=_peer(k), device_id_type=MESH))
        return local, remote

    def _copy2(self, a, src, dst, sems, slot, relation, to):
        send_sems, recv_sems, _ = sems
        return pltpu.make_async_remote_copy(src_ref=src, dst_ref=dst.at[slot], send_sem=send_sems.at[a, relation - 1],
                                            recv_sem=recv_sems.at[a, relation - 1], device_id=_peer(to), device_id_type=MESH)

    def start(self, in_refs, out_refs, sems):
        if self.mode == GATHER2:
            me = _my_index()
            for a, (src, dst) in enumerate(zip(in_refs, out_refs)):
                pltpu.make_async_copy(src, dst.at[me], sems[2].at[a]).start()
                for k in (SIBLING,) + OTHER_CHIPS:
                    self._copy2(a, src, dst, sems, me, k, k).start()
            return
        local, remote = self._copies(in_refs, out_refs, *sems, landing=False)
        for cp in local + remote:
            cp.start()

    def forward(self, in_refs, out_refs, sems):
        for a, (src, dst) in enumerate(zip(in_refs, out_refs)):
            for r in OTHER_CHIPS:
                pj = _peer_index(r)
                self._copy2(a, src, dst, sems, pj, r, r).wait_recv()
                self._copy2(a, dst.at[pj], dst, sems, pj, r ^ SIBLING, SIBLING).start()

    def wait(self, in_refs, out_refs, sems):
        if self.mode == GATHER2:
            me = _my_index()
            for a, (src, dst) in enumerate(zip(in_refs, out_refs)):
                for k in (SIBLING,) + OTHER_CHIPS:
                    self._copy2(a, src, dst, sems, me, k, k).wait_send()
                self._copy2(a, src, dst, sems, _peer_index(SIBLING), SIBLING, SIBLING).wait_recv()
                for r in OTHER_CHIPS:
                    passed = self._copy2(a, src, dst, sems, _peer_index(r ^ SIBLING), r ^ SIBLING, SIBLING)
                    passed.wait_send()
                    passed.wait_recv()
                pltpu.make_async_copy(src, dst.at[me], sems[2].at[a]).wait()
            return
        local, remote = self._copies(in_refs, out_refs, *sems, landing=True)
        for cp in remote:
            cp.wait_send()
            cp.wait_recv()
        for cp in local:
            cp.wait()

    def ride(self, refs, n_in, n_out, first, mid, last):
        refs = list(refs)
        n = self.n
        x_in = refs[n_in:n_in + n]
        x_out = refs[n_in + n + n_out:n_in + 2 * n + n_out]
        sems = refs[n_in + 2 * n + n_out:n_in + 2 * n + n_out + 3]

        @pl.when(first)
        def _():
            self.start(x_in, x_out, sems)

        if self.mode == GATHER2:
            @pl.when(mid)
            def _():
                self.forward(x_in, x_out, sems)

        @pl.when(last)
        def _():
            self.wait(x_in, x_out, sems)

        return refs[:n_in] + refs[n_in + n:n_in + n + n_out] + refs[n_in + 2 * n + n_out + 3:]

    def call(self, name):
        n = self.n

        def body(*refs):
            ins, outs, sems = refs[:n], refs[n:2 * n], refs[2 * n:]
            self.start(ins, outs, sems)
            if self.mode == GATHER2:
                self.forward(ins, outs, sems)
            self.wait(ins, outs, sems)

        return pl.pallas_call(body, name=name, in_specs=self.specs(), out_specs=self.specs(), out_shape=self.out_shape(),
                              scratch_shapes=self.scratch())(*self.arrays)


def _all_gather(v, name):
    return _Exchange(GATHER, [v]).call(name)[0]


def _hosted(kernel_body, ex, n_in, n_out, first, last, grid):
    if ex is None:
        return kernel_body

    def body(*refs):
        mid = pl.program_id(0) == (2 * grid[0]) // 3 if len(grid) == 1 else None
        kernel_body(*ex.ride(refs, n_in, n_out, first(), mid, last()))

    return body


def _host_call(kernel_body, ex, first, last, name, grid, in_specs, out_specs, out_shape, scratch_shapes, sem, args):
    n_in, n_out = len(in_specs), len(out_specs)
    if ex is None:
        outs = pl.pallas_call(kernel_body, name=name, grid=grid, in_specs=in_specs, out_specs=out_specs, out_shape=out_shape,
                              scratch_shapes=scratch_shapes, compiler_params=_cp(*sem))(*args)
        return list(outs), []
    outs = pl.pallas_call(
        _hosted(kernel_body, ex, n_in, n_out, first, last, grid), name=name, grid=grid,
        in_specs=list(in_specs) + ex.specs(), out_specs=list(out_specs) + ex.specs(),
        out_shape=list(out_shape) + ex.out_shape(), scratch_shapes=ex.scratch() + list(scratch_shapes),
        compiler_params=_cp(*sem))(*args, *ex.arrays)
    return list(outs[:n_out]), list(outs[n_out:])


def _mod_fwd(call, mod_w, bias, name):
    nl, dm, n = mod_w.shape

    def body(c_ref, w_ref, b_ref, o_ref):
        cv = c_ref[...]
        cond = _bf(cv * _sig(cv))
        for layer in range(nl):
            o_ref[layer] = _nn(cond, _bf(w_ref[layer])) + b_ref[layer]

    return pl.pallas_call(
        body, name=name, out_shape=jax.ShapeDtypeStruct((nl, call.shape[0], n), F32),
        compiler_params=pltpu.CompilerParams(vmem_limit_bytes=VMEM_LIMIT),
    )(call, mod_w, bias)


def _mod_bwd(call, dm_all, mod_w, name):
    nl, dm, n = mod_w.shape

    def body(c_ref, d_ref, w_ref, gw_ref, dc_ref):
        cv = c_ref[...]
        cond = _bf(cv * _sig(cv))
        dc = jnp.zeros(cv.shape, F32)
        for layer in range(nl):
            db = _bf(d_ref[layer])
            gw_ref[layer] = _tn(cond, db)
            dc = dc + _nt(db, _bf(w_ref[layer]))
        dc_ref[...] = dc

    return pl.pallas_call(
        body, name=name,
        out_shape=[jax.ShapeDtypeStruct(mod_w.shape, F32), jax.ShapeDtypeStruct(call.shape, F32)],
        compiler_params=pltpu.CompilerParams(vmem_limit_bytes=VMEM_LIMIT),
    )(call, dm_all, mod_w)


def _sum_parts(g, name):
    def body(g_ref, o_ref):
        acc = g_ref[0]
        for j in range(1, g.shape[0]):
            acc = acc + g_ref[j]
        o_ref[...] = acc

    return pl.pallas_call(body, name=name, out_shape=jax.ShapeDtypeStruct(g.shape[1:], g.dtype))(g)


def _small_finish(dcond_g, c_ctx, dlb, lbraw, dm_ctx, dm_lat, name):
    def body(dc_ref, c_ref, dlb_ref, lb_ref, mc_ref, ml_ref, gc_ref, glb_ref, gb_ref):
        acc = dc_ref[0, 0:1, :]
        for j in range(1, N_DEV):
            acc = acc + dc_ref[j, 0:1, :]
        cv = c_ref[...]
        s = _sig(cv)
        gc_ref[...] = acc * (s * (1.0 + cv * (1.0 - s)))
        lb = _lower_bound(lb_ref)
        d0 = dlb_ref[...] * lb * (1.0 - lb)
        glb_ref[0:1, :] = d0
        glb_ref[1:2, :] = -d0
        gb_ref[...] = mc_ref[...] + ml_ref[...]

    return pl.pallas_call(
        body, name=name,
        out_shape=[jax.ShapeDtypeStruct(c_ctx.shape, F32), jax.ShapeDtypeStruct(lbraw.shape, F32),
                   jax.ShapeDtypeStruct(dm_ctx.shape, F32)],
    )(dcond_g, c_ctx, dlb, lbraw, dm_ctx, dm_lat)


def _row_tile(r, cap, mult):
    best = r
    for t in range(mult, min(r, cap) + 1, mult):
        if r % t == 0:
            best = t
    return best


def _adam(g_list, w, m, v, name, ex=None):
    nl, r, cdim = w.shape
    p = g_list[0].shape[0]
    tr = _row_tile(r, 128, 16)
    ni = r // tr

    def body(*refs):
        g_refs = refs[:nl]
        w_ref, m_ref, v_ref, go_ref, d_ref, mo_ref, vo_ref = refs[nl:]
        layer = pl.program_id(0)

        def total(g_ref):
            acc = g_ref[0].astype(F32)
            for j in range(1, p):
                acc = acc + g_ref[j].astype(F32)
            return acc

        g = total(g_refs[0])
        for k in range(1, nl):
            g = jnp.where(layer == k, total(g_refs[k]), g)
        m2 = ADAM_B1 * m_ref[0] + (1.0 - ADAM_B1) * g
        v2 = ADAM_B2 * v_ref[0] + (1.0 - ADAM_B2) * (g * g)
        m_hat = m2 / (1.0 - ADAM_B1 ** ADAM_STEP)
        v_hat = v2 / (1.0 - ADAM_B2 ** ADAM_STEP)
        go_ref[0] = g
        d_ref[0] = -ADAM_LR * (m_hat / (jnp.sqrt(v_hat) + ADAM_EPS) + ADAM_WD * w_ref[0])
        mo_ref[0] = m2
        vo_ref[0] = v2

    def g_spec(k):
        return pl.BlockSpec((p, tr, cdim), lambda la, i: (0, jnp.where(la == k, i, jnp.where(la < k, 0, ni - 1)), 0))

    spec = pl.BlockSpec((1, tr, cdim), lambda la, i: (la, i, 0))
    return _host_call(
        body, ex, lambda: (pl.program_id(0) == 0) & (pl.program_id(1) == 0),
        lambda: (pl.program_id(0) == nl - 1) & (pl.program_id(1) == ni - 1),
        name=name, grid=(nl, ni),
        in_specs=[g_spec(k) for k in range(nl)] + [spec, spec, spec],
        out_specs=[spec] * 4, out_shape=[jax.ShapeDtypeStruct((nl, r, cdim), F32)] * 4,
        scratch_shapes=[], sem=("arbitrary", "arbitrary"), args=(*g_list, w, m, v))


def _f32_as_rows(a, width):
    return lax.bitcast_convert_type(a.reshape(-1), BF16).reshape(-1, width)


def _rows_as_f32(rows):
    return lax.bitcast_convert_type(rows.reshape(rows.shape[:-2] + (-1, 2)), F32)


def _pad_rows(a, mult):
    r = (-a.shape[-2]) % mult
    if r == 0:
        return a
    widths = [(0, 0)] * (a.ndim - 2) + [(0, r), (0, 0)]
    return jnp.pad(a, widths)


def _pack_flat(parts, lane):
    flat = jnp.concatenate([p.reshape(-1).astype(F32) for p in parts])
    n = flat.shape[0]
    rows = -(-n // lane)
    rows += (-rows) % 8
    return jnp.pad(flat, (0, rows * lane - n)).reshape(rows, lane)


def _unpack_flat(packed, shapes):
    flat = packed.reshape(-1)
    out, off = [], 0
    for s in shapes:
        n = math.prod(s)
        out.append(flat[off:off + n].reshape(s))
        off += n
    return out


def kernel(x, c, ctx, c_ctx, mod_w, mod_b, norm_g, ffn_w_in, ffn_w_out, even_w_in, even_w_out, attn_qk_norm_g, attn_sink, hgrn_out_norm_g, hgrn_lb, odd_w_in, odd_w_out, loss_target, m_c_ctx, m_mod_w, m_mod_b, m_norm_g, m_ffn_w_in, m_ffn_w_out, m_even_w_in, m_even_w_out, m_attn_qk_norm_g, m_attn_sink, m_hgrn_out_norm_g, m_hgrn_lb, m_odd_w_in, m_odd_w_out, v_c_ctx, v_mod_w, v_mod_b, v_norm_g, v_ffn_w_in, v_ffn_w_out, v_even_w_in, v_even_w_out, v_attn_qk_norm_g, v_attn_sink, v_hgrn_out_norm_g, v_hgrn_lb, v_odd_w_in, v_odd_w_out):
    me = _my_index()
    lc, dm = ctx.shape[1], x.shape[2]
    nmod = mod_w.shape[2]
    big = (ffn_w_in, ffn_w_out, even_w_in, even_w_out, odd_w_in, odd_w_out)

    extra = _pad_rows(jnp.concatenate([_f32_as_rows(c, dm), _f32_as_rows(norm_g, dm)], axis=0), 16)
    shards = {"ffn_in0": ffn_w_in[0], "ffn_in1": ffn_w_in[1], "ffn_out0": ffn_w_out[0], "ffn_out1": ffn_w_out[1],
              "even_in": even_w_in[0], "even_out": even_w_out[0], "odd_in": odd_w_in[0], "odd_out": odd_w_out[0]}
    shards = {n: a.astype(BF16) for n, a in shards.items()}
    first = _Exchange(GATHER2, [shards["even_in"], extra]).call("gather_first")
    w = {"even_in": _full_weight("even_in", first[0])}
    c_all = _rows_as_f32(first[1][:, 0:2])
    norm_g_all = _rows_as_f32(first[1][:, 2:3]).reshape(N_DEV, 2, 2, -1)
    norm_g_full = norm_g_all.transpose(1, 2, 0, 3).reshape(2, 2, dm)

    call = jnp.concatenate([c_all, c_ctx[None, :], jnp.zeros((16 - N_DEV - 1, dm), F32)], axis=0)
    bias = lax.dynamic_slice_in_dim(mod_b, me * nmod, nmod, axis=1)[:, None, :]
    m_sh = _mod_fwd(call, mod_w, bias, "mod_fwd")
    m_g = _all_gather(m_sh.reshape(-1, nmod), "gather_mod").reshape(N_DEV, 2, 16, nmod)
    m_all = m_g.transpose(1, 2, 0, 3).reshape(2, 16, -1)
    m_lat = lax.dynamic_index_in_dim(m_all, me, axis=1, keepdims=False)
    mv = jnp.stack([m_all[:, N_DEV], m_lat], axis=1)[:, :, None, :]

    _, dxs, gw, small = _local_step((ctx[0], x[0]), loss_target[0], mv, norm_g_full, w, attn_qk_norm_g[0], attn_sink[0],
                                    hgrn_out_norm_g, hgrn_lb, lc, shards)
    grad_x = dxs[None]

    last = _Exchange(SCATTER, [_shard_slots("even_in_b", gw["even_in_b"])])
    big_g = [[gw["ffn_in0"], gw["ffn_in1"]], [gw["ffn_out0"], gw["ffn_out1"]], None, [gw["even_out"]],
             [gw["odd_in"]], [gw["odd_out"]]]
    halves = (2, even_w_in.shape[1] // 2, even_w_in.shape[2])
    big_w = (ffn_w_in, ffn_w_out, even_w_in.reshape(halves), even_w_out, odd_w_in, odd_w_out)
    big_m = (m_ffn_w_in, m_ffn_w_out, m_even_w_in.reshape(halves), m_even_w_out, m_odd_w_in, m_odd_w_out)
    big_v = (v_ffn_w_in, v_ffn_w_out, v_even_w_in.reshape(halves), v_even_w_out, v_odd_w_in, v_odd_w_out)
    big_names = ("ffn_w_in", "ffn_w_out", "even_w_in", "even_w_out", "odd_w_in", "odd_w_out")
    big_out = [None] * 6

    def adam_big(i, ex=None):
        big_out[i], got = _adam(big_g[i], big_w[i], big_m[i], big_v[i], "adam_" + big_names[i], ex)
        return got

    dmv = small["dmv"]
    small_shapes = [(2, 6 * dm), (2, 6 * dm), (2, 2, dm), (2, HEAD_DIM), (ATTN_HEADS,), (HG_D,), (HG_HEADS * HG_D,), (1,)]
    vec = _pack_flat([dmv[:, 0, 0], dmv[:, 1, 0], small["norm_g"], small["qk_g"], small["sink"], small["hg_out_g"],
                      small["lb"], small["loss"]], 128)
    big_g[2] = [gw["even_in_a"], adam_big(0, last)[0]]
    vec_g = adam_big(1, _Exchange(GATHER, [vec]))[0]
    tot = _unpack_flat(_sum_parts(vec_g, "sum_small"), small_shapes)
    dm_ctx_tot, dm_lat_tot, g_norm_full, g_qk, g_sink, g_hg, dlb_tot, loss_tot = tot
    dm_lat_each = vec_g.reshape(N_DEV, -1)[:, 12 * dm:24 * dm].reshape(N_DEV, 2, 6 * dm)
    dm_lat_mine = lax.dynamic_slice_in_dim(dm_lat_each, me * nmod, nmod, axis=2).transpose(1, 0, 2)
    dm_ctx_mine = lax.dynamic_slice_in_dim(dm_ctx_tot, me * nmod, nmod, axis=1)[:, None, :]
    dm_all = jnp.concatenate([dm_lat_mine, dm_ctx_mine, jnp.zeros((2, 16 - N_DEV - 1, nmod), F32)], axis=1)
    g_mod_w, dcond = _mod_bwd(call, dm_all, mod_w, "mod_bwd")
    dcond_g = adam_big(4, _Exchange(GATHER, [dcond[N_DEV:]]))[0]
    g_c_ctx, g_lb, g_mod_b = _small_finish(dcond_g, c_ctx[None, :], dlb_tot[None, :], hgrn_lb, dm_ctx_tot, dm_lat_tot,
                                           "small_finish")
    g_norm = lax.dynamic_slice_in_dim(g_norm_full, me * norm_g.shape[2], norm_g.shape[2], axis=2)
    for i in (3, 5, 2):
        adam_big(i)
    big_out[2] = [o.reshape(even_w_in.shape) for o in big_out[2]]
    big_res = [[big_out[i][k] for i in range(6)] for k in range(4)]

    mod_res, _ = _adam([g_mod_w[0][None], g_mod_w[1][None]], mod_w, m_mod_w, v_mod_w, "adam_mod_w")

    sm_w = (c_ctx, mod_b, norm_g, attn_qk_norm_g, attn_sink, hgrn_out_norm_g, hgrn_lb)
    sm_m = (m_c_ctx, m_mod_b, m_norm_g, m_attn_qk_norm_g, m_attn_sink, m_hgrn_out_norm_g, m_hgrn_lb)
    sm_v = (v_c_ctx, v_mod_b, v_norm_g, v_attn_qk_norm_g, v_attn_sink, v_hgrn_out_norm_g, v_hgrn_lb)
    sm_g = (g_c_ctx, g_mod_b, g_norm, g_qk, g_sink, g_hg, g_lb)
    sm_shapes = [a.shape for a in sm_w]
    sm_out, _ = _adam([_pack_flat(sm_g, 128)[None]], _pack_flat(sm_w, 128)[None], _pack_flat(sm_m, 128)[None],
                      _pack_flat(sm_v, 128)[None], "adam_small")
    sm_res = [_unpack_flat(o, sm_shapes) for o in sm_out]

    def ordered(k):
        s, b = sm_res[k], big_res[k]
        return [s[0], mod_res[k], s[1], s[2], b[0], b[1], b[2], b[3], s[3], s[4], s[5], s[6], b[4], b[5]]

    return (loss_tot[0], grad_x, *ordered(0), *ordered(1), *ordered(2), *ordered(3))
```

```python
import functools
import math

import jax
import jax.numpy as jnp
import numpy as np
from jax import lax
from jax.experimental import pallas as pl
from jax.experimental.pallas import tpu as pltpu

F32 = jnp.float32
BF16 = jnp.bfloat16
EPS = 1e-6
N_DEV = 8
MESH = pl.DeviceIdType.MESH

HEAD_DIM = 64
ATTN_HEADS = 8
ATTN_KV = 2
ATTN_BLOCK = 128
WINDOW = 128
GRID_W = 64
HG_HEADS = 4
HG_D = 128
HG_CHUNK = 64
HG_STEP_CHUNKS = 4
RET_HEADS = 4
RET_DK = 256
RET_DV = 512
RET_CHUNK = 256
NEG = -1e30

ADAM_LR = 0.001
ADAM_B1 = 0.9
ADAM_B2 = 0.999
ADAM_EPS = 1e-08
ADAM_WD = 0.01
ADAM_STEP = 10

VMEM_LIMIT = 60 * 1024 * 1024
MXU_WIDTH = 256


def _hidden_chunks(fh, parts=2):
    step = -(-(fh // parts) // MXU_WIDTH) * MXU_WIDTH
    cuts = list(range(0, fh, step)) + [fh]
    return list(zip(cuts[:-1], cuts[1:]))


def _cp(*sem):
    return pltpu.CompilerParams(dimension_semantics=sem, vmem_limit_bytes=VMEM_LIMIT)


def _nn(a, b):
    return jnp.dot(a, b, preferred_element_type=F32)


def _nt(a, b):
    return lax.dot_general(a, b, (((1,), (1,)), ((), ())), preferred_element_type=F32)


def _tn(a, b):
    return lax.dot_general(a, b, (((0,), (0,)), ((), ())), preferred_element_type=F32)


ACT = BF16


def _bf(a):
    return a.astype(ACT)


def _sig(x):
    return jax.nn.sigmoid(x)


def _split3(x):
    h = x.astype(BF16)
    r = x - h.astype(F32)
    m = r.astype(BF16)
    lo = (r - m.astype(F32)).astype(BF16)
    return h, m, lo


def _nn3(m01, x):
    h, m, lo = _split3(x)
    return _nn(m01, h) + _nn(m01, m) + _nn(m01, lo)


def _nn3r(x, m01):
    h, m, lo = _split3(x)
    return _nn(h, m01) + _nn(m, m01) + _nn(lo, m01)


def _full(shape):
    nd = len(shape)
    return pl.BlockSpec(shape, lambda *a: (0,) * nd, pipeline_mode=pl.Buffered(1))


def _whole(shape):
    nd = len(shape)
    return pl.BlockSpec(shape, lambda *a: (0,) * nd)


def _rows(tm, width):
    return pl.BlockSpec((tm, width), lambda i: (i, 0))


def _cols(height, tm):
    return pl.BlockSpec((height, tm), lambda i: (0, i))


def _ctx_lat(width):
    return pl.BlockSpec((1, 1, width), lambda i: (jnp.minimum(i, 1), 0, 0))


def _acc_ctx_lat(ref, i, val):
    @pl.when(i <= 1)
    def _():
        ref[...] = val.reshape(ref.shape)

    @pl.when(i > 1)
    def _():
        ref[...] += val.reshape(ref.shape)


def _acc_all(ref, i, val):
    @pl.when(i == 0)
    def _():
        ref[...] = val.reshape(ref.shape)

    @pl.when(i > 0)
    def _():
        ref[...] += val.reshape(ref.shape)


def _tile(n, cap):
    best = None
    for t in range(128, min(n, cap) + 1, 128):
        if n % t == 0:
            best = t
    return n if best is None else best


def _norm_mod(xv, g, shift, scale):
    r = lax.rsqrt(jnp.mean(xv * xv, axis=-1, keepdims=True) + EPS)
    xhat = xv * r
    n = xhat * g
    return r, xhat, n, n * (1.0 + scale) + shift


def _norm_mod_bwd(dh, r, xhat, n, g, scale):
    dshift = jnp.sum(dh, axis=0, keepdims=True)
    dscale = jnp.sum(dh * n, axis=0, keepdims=True)
    dn = dh * (1.0 + scale)
    dg = jnp.sum(dn * xhat, axis=0, keepdims=True)
    dxh = dn * g
    dx = r * (dxh - xhat * jnp.mean(dxh * xhat, axis=-1, keepdims=True))
    return dx, dshift, dscale, dg


def _stream(x):
    if isinstance(x, tuple):
        return list(x), x[0].shape[0] + x[1].shape[0], x[0].shape[1]
    return [x], x.shape[0], x.shape[1]


def _stream_specs(x, tm, dm):
    if isinstance(x, tuple):
        return [pl.BlockSpec((tm, dm), lambda i: (0, 0)), pl.BlockSpec((tm, dm), lambda i: (jnp.maximum(i - 1, 0), 0))]
    return [_rows(tm, dm)]


def _stream_tile(refs):
    if len(refs) == 2:
        return jnp.where(pl.program_id(0) == 0, refs[0][...], refs[1][...])
    return refs[0][...]


def _pre_fwd(x, gain, ms, w, splits, tm, name, ex=None, out_dtype=F32, qk=None):
    xs, T, dm = _stream(x)
    nx = len(xs)
    nt = T // tm
    nq = 0 if qk is None else 3
    ns = len(splits)

    def body(*refs):
        g_ref, ms_ref, w_ref = refs[nx:nx + 3]
        outs = refs[nx + 3 + nq:]
        ms_v = ms_ref[0]
        h = _norm_mod(_stream_tile(refs[:nx]), g_ref[...], ms_v[:, :dm], ms_v[:, dm:])[3]
        hb = _bf(h)
        for k, ((s, e), o_ref) in enumerate(zip(splits, outs[:ns])):
            part = _nn(hb, w_ref[:, s:e])
            o_ref[...] = part.astype(o_ref.dtype)
            if k == 0 and qk is not None:
                gq_ref, c_ref, s_ref = refs[nx + 3:nx + 6]
                _qk_tile_fwd(part, gq_ref, c_ref[...], s_ref[...], *outs[ns:])

    in_specs = _stream_specs(x, tm, dm) + [_full((1, dm)), _ctx_lat(2 * dm), _full(w.shape)]
    out_specs = [_rows(tm, e - s) for s, e in splits]
    out_shape = [jax.ShapeDtypeStruct((T, e - s), out_dtype) for s, e in splits]
    args = [*xs, gain, ms, w]
    if qk is not None:
        qw = ATTN_HEADS * HEAD_DIM
        in_specs += [_full(qk[0].shape), _rows(tm, PAIR), _rows(tm, PAIR)]
        args += list(qk)
        out_specs += [_rows(tm, qw), _rows(tm, PAIR), _rows(tm, PAIR)]
        out_shape += [jax.ShapeDtypeStruct((T, qw), ACT), jax.ShapeDtypeStruct((T, PAIR), ACT), jax.ShapeDtypeStruct((T, PAIR), ACT)]
    return _host_call(
        body, ex, lambda: pl.program_id(0) == 0, lambda: pl.program_id(0) == nt - 1,
        name=name, grid=(nt,), in_specs=in_specs, out_specs=out_specs, out_shape=out_shape,
        scratch_shapes=[], sem=("arbitrary",), args=tuple(args))


def _pre_bwd(x, dx_in, gain, ms, w, pieces, tm, name, latent_dx=False, ex=None, qk=None):
    xs, T, dm = _stream(x)
    nx = len(xs)
    dx_spec = pl.BlockSpec((tm, dm), lambda i: (jnp.maximum(i - 1, 0), 0)) if latent_dx else _rows(tm, dm)
    dx_rows = T - tm if latent_dx else T
    n_out = w.shape[1]
    flat = [a for _, arrs in pieces for a in arrs]
    nq = 0 if qk is None else 6
    qkw = (ATTN_HEADS + ATTN_KV) * HEAD_DIM

    def body(*refs):
        dxin_ref, g_ref, ms_ref, w_ref = refs[nx:nx + 4]
        rest = refs[nx + 4:]
        p_refs = rest[:len(flat)]
        qk_refs = rest[len(flat):len(flat) + nq]
        dx_ref, h_ref, dp_ref, dms_ref, dg_ref = rest[len(flat) + nq:len(flat) + nq + 5]
        i = pl.program_id(0)
        ms_v = ms_ref[0]
        g = g_ref[...]
        scale = ms_v[:, dm:]
        r, xhat, n, h = _norm_mod(_stream_tile(refs[:nx]), g, ms_v[:, :dm], scale)
        h_ref[...] = _bf(h).T
        dh = jnp.zeros((tm, dm), F32)
        if qk is not None:
            dq_ref, dk_ref, pa_ref, gq_ref, c_ref, s_ref = qk_refs
            dqk, dgs = _qk_tile_bwd(dq_ref, dk_ref, pa_ref, gq_ref, c_ref[...], s_ref[...])
            dgq_ref = rest[len(flat) + nq + 5]
            for p, dgp in enumerate(dgs):
                _acc_all(dgq_ref.at[p], i, dgp)
            vb = _bf(dqk)
            dp_ref[:, :qkw] = vb
            dh = dh + _nt(vb, w_ref[:, :qkw])
        k = 0
        for s, arrs in pieces:
            v = p_refs[k][...].astype(F32)
            for j in range(1, len(arrs)):
                v = v + p_refs[k + j][...].astype(F32)
            k += len(arrs)
            vb = _bf(v)
            wd = vb.shape[1]
            dp_ref[:, s:s + wd] = vb
            dh = dh + _nt(vb, w_ref[:, s:s + wd])
        dx, dshift, dscale, dg = _norm_mod_bwd(dh, r, xhat, n, g, scale)
        dx_ref[...] = dxin_ref[...] + dx
        _acc_ctx_lat(dms_ref, i, jnp.concatenate([dshift, dscale], axis=1))
        _acc_all(dg_ref, i, dg)

    nt = T // tm
    in_specs = (_stream_specs(x, tm, dm) + [_rows(tm, dm), _full((1, dm)), _ctx_lat(2 * dm), _full(w.shape)]
                + [_rows(tm, a.shape[1]) for a in flat])
    out_specs = [dx_spec, _cols(dm, tm), _rows(tm, n_out), _ctx_lat(2 * dm), _whole((1, dm))]
    out_shape = [jax.ShapeDtypeStruct((dx_rows, dm), F32), jax.ShapeDtypeStruct((dm, T), ACT),
                 jax.ShapeDtypeStruct((T, n_out), ACT), jax.ShapeDtypeStruct((2, 1, 2 * dm), F32),
                 jax.ShapeDtypeStruct((1, dm), F32)]
    args = [*xs, dx_in, gain, ms, w, *flat]
    if qk is not None:
        dq, dk, pa, gains, cosp, sinp = qk
        in_specs += [_rows(tm, dq.shape[1]), _rows(tm, PAIR), _rows(tm, qkw), _full(gains.shape), _rows(tm, PAIR), _rows(tm, PAIR)]
        args += [dq, dk, pa, gains, cosp, sinp]
        out_specs.append(_whole(gains.shape))
        out_shape.append(jax.ShapeDtypeStruct(gains.shape, F32))
    return _host_call(
        body, ex, lambda: pl.program_id(0) == 0, lambda: pl.program_id(0) == nt - 1,
        name=name, grid=(nt,), in_specs=in_specs, out_specs=out_specs, out_shape=out_shape,
        scratch_shapes=[], sem=("arbitrary",), args=tuple(args))


def _ffn_fwd(x1, gain, ms, w_in, w_out, tm, name, target=None, ex=None):
    T, dm = x1.shape
    fh = w_out.shape[0]
    head = target is not None

    def body(*refs):
        if head:
            x_ref, g_ref, ms_ref, wi_ref, wo_ref, t_ref, x2_ref, u_ref, f_ref, loss_ref = refs
        else:
            x_ref, g_ref, ms_ref, wi_ref, wo_ref, x2_ref, u_ref, f_ref = refs
        ms_v = ms_ref[0]
        xv = x_ref[...]
        hb = _bf(_norm_mod(xv, g_ref[...], ms_v[:, :dm], ms_v[:, dm:2 * dm])[3])
        f = jnp.zeros((tm, dm), F32)
        for c0, c1 in _hidden_chunks(fh):
            gt = _nn(hb, wi_ref[:, c0:c1])
            up = _nn(hb, wi_ref[:, fh + c0:fh + c1])
            u_ref[:, c0:c1] = _bf(gt)
            u_ref[:, fh + c0:fh + c1] = _bf(up)
            f = f + _nn(_bf(gt * _sig(gt) * up), wo_ref[c0:c1, :])
        f_ref[...] = _bf(f)
        x2 = xv + ms_v[:, 2 * dm:] * f
        if head:
            i = pl.program_id(0)
            e = x2 - t_ref[...]
            x2_ref[...] = jnp.where(i > 0, e * (1.0 / dm), 0.0)
            _acc_all(loss_ref, i, jnp.where(i > 0, jnp.sum(e * e) * (0.5 / dm), 0.0))
        else:
            x2_ref[...] = x2

    ins = [x1, gain, ms, w_in, w_out]
    in_specs = [_rows(tm, dm), _full((1, dm)), _ctx_lat(3 * dm), _full(w_in.shape), _full(w_out.shape)]
    out_specs = [_rows(tm, dm), _rows(tm, 2 * fh), _rows(tm, dm)]
    out_shape = [jax.ShapeDtypeStruct((T, dm), F32), jax.ShapeDtypeStruct((T, 2 * fh), ACT), jax.ShapeDtypeStruct((T, dm), ACT)]
    if head:
        ins.append(target)
        in_specs.append(pl.BlockSpec((tm, dm), lambda i: (jnp.maximum(i - 1, 0), 0)))
        out_specs.append(_whole((1, 1)))
        out_shape.append(jax.ShapeDtypeStruct((1, 1), F32))
    nt = T // tm
    return _host_call(
        body, ex, lambda: pl.program_id(0) == 0, lambda: pl.program_id(0) == nt - 1,
        name=name, grid=(nt,), in_specs=in_specs, out_specs=out_specs, out_shape=out_shape,
        scratch_shapes=[], sem=("arbitrary",), args=tuple(ins))


def _ffn_bwd(x1, dx2, u, f, gain, ms, w_in, w_out, tm, name, ex=None):
    T, dm = x1.shape
    fh = w_out.shape[0]

    def body(x_ref, dx2_ref, u_ref, f_ref, g_ref, ms_ref, wi_ref, wo_ref,
             dx1_ref, h_ref, du_ref, act_ref, df_ref, dms_ref, dg_ref):
        i = pl.program_id(0)
        ms_v = ms_ref[0]
        g = g_ref[...]
        scale = ms_v[:, dm:2 * dm]
        gate = ms_v[:, 2 * dm:]
        r, xhat, n, h = _norm_mod(x_ref[...], g, ms_v[:, :dm], scale)
        h_ref[...] = _bf(h).T
        dx2 = dx2_ref[...]
        dgate = jnp.sum(dx2 * f_ref[...].astype(F32), axis=0, keepdims=True)
        dfb = _bf(dx2 * gate)
        df_ref[...] = dfb
        dh = jnp.zeros((tm, dm), F32)
        for c0, c1 in _hidden_chunks(fh, 1):
            da = _nt(dfb, wo_ref[c0:c1, :])
            gt = u_ref[:, c0:c1].astype(F32)
            up = u_ref[:, fh + c0:fh + c1].astype(F32)
            s = _sig(gt)
            sg = gt * s
            act_ref[c0:c1, :] = _bf(sg * up).T
            dgt = _bf(da * up * (s * (1.0 + gt * (1.0 - s))))
            dup = _bf(da * sg)
            du_ref[:, c0:c1] = dgt
            du_ref[:, fh + c0:fh + c1] = dup
            dh = dh + _nt(dgt, wi_ref[:, c0:c1]) + _nt(dup, wi_ref[:, fh + c0:fh + c1])
        dx, dshift, dscale, dg = _norm_mod_bwd(dh, r, xhat, n, g, scale)
        dx1_ref[...] = dx2 + dx
        _acc_ctx_lat(dms_ref, i, jnp.concatenate([dshift, dscale, dgate], axis=1))
        _acc_all(dg_ref, i, dg)

    nt = T // tm
    return _host_call(
        body, ex, lambda: pl.program_id(0) == 0, lambda: pl.program_id(0) == nt - 1,
        name=name, grid=(nt,),
        in_specs=[_rows(tm, dm), _rows(tm, dm), _rows(tm, 2 * fh), _rows(tm, dm), _full((1, dm)), _ctx_lat(3 * dm),
                  _full(w_in.shape), _full(w_out.shape)],
        out_specs=[_rows(tm, dm), _cols(dm, tm), _rows(tm, 2 * fh), _cols(fh, tm), _rows(tm, dm),
                   _ctx_lat(3 * dm), _whole((1, dm))],
        out_shape=[jax.ShapeDtypeStruct((T, dm), F32), jax.ShapeDtypeStruct((dm, T), ACT),
                   jax.ShapeDtypeStruct((T, 2 * fh), ACT), jax.ShapeDtypeStruct((fh, T), ACT),
                   jax.ShapeDtypeStruct((T, dm), ACT), jax.ShapeDtypeStruct((2, 1, 3 * dm), F32),
                   jax.ShapeDtypeStruct((1, dm), F32)],
        scratch_shapes=[], sem=("arbitrary",), args=(x1, dx2, u, f, gain, ms, w_in, w_out))


def _wgrad(a_t, b, name, rows=None, ex=None):
    T = a_t.shape[1]
    r0, K = (0, a_t.shape[0]) if rows is None else rows
    N = b.shape[1]
    tk, tn, tt = _tile(K, 1408), _tile(N, 1664), _tile(T, 4224)
    nt = T // tt
    assert r0 % tk == 0
    off = r0 // tk
    nk, nn = K // tk, N // tn

    def body(a_ref, b_ref, o_ref, acc_ref):
        t = pl.program_id(2)
        part = _nn(a_ref[...], b_ref[...])

        @pl.when(t == 0)
        def _():
            acc_ref[...] = part

        @pl.when(t > 0)
        def _():
            acc_ref[...] += part

        @pl.when(t == nt - 1)
        def _():
            o_ref[...] = acc_ref[...].astype(o_ref.dtype)

    def at(i, j, t):
        return (pl.program_id(0) == i) & (pl.program_id(1) == j) & (pl.program_id(2) == t)

    outs, got = _host_call(
        body, ex, lambda: at(0, 0, 0), lambda: at(nk - 1, nn - 1, nt - 1),
        name=name, grid=(nk, nn, nt),
        in_specs=[pl.BlockSpec((tk, tt), lambda i, j, t: (i + off, t)), pl.BlockSpec((tt, tn), lambda i, j, t: (t, j))],
        out_specs=[pl.BlockSpec((tk, tn), lambda i, j, t: (i, j))],
        out_shape=[jax.ShapeDtypeStruct((K, N), ACT)],
        scratch_shapes=[pltpu.VMEM((tk, tn), F32)], sem=("arbitrary", "arbitrary", "arbitrary"), args=(a_t, b))
    return outs[0] if ex is None else (outs[0], got)


def _post_fwd(x, o_fw, o_bw, g_src, g_blk, gain, a, w_out, ms, dvh, tm, name):
    xs, T, dm = _stream(x)
    nx = len(xs)
    hv = o_fw.shape[1]
    aw = 0 if a is None else a.shape[1]
    has_gain = gain is not None

    def body(*refs):
        refs = list(refs)
        x_refs = refs[:nx]
        of_ref, ob_ref, g_ref = refs[nx:nx + 3]
        k = nx + 3
        gain_ref = a_ref = None
        if has_gain:
            gain_ref = refs[k]
            k += 1
        if aw:
            a_ref = refs[k]
            k += 1
        w_ref, ms_ref, x1_ref, z_ref, yp_ref = refs[k:k + 5]
        o = of_ref[...].astype(F32) + ob_ref[...].astype(F32)
        gr = g_ref[...].astype(F32)
        if aw:
            z_ref[:, :aw] = _bf(a_ref[...])
        for hd in range(hv // dvh):
            sl = slice(hd * dvh, (hd + 1) * dvh)
            oh = o[:, sl]
            gh = gr[:, sl]
            r = lax.rsqrt(jnp.mean(oh * oh, axis=-1, keepdims=True) + EPS)
            y = oh * r
            if has_gain:
                y = y * gain_ref[...]
            y = y * (gh * _sig(gh))
            z_ref[:, aw + hd * dvh:aw + (hd + 1) * dvh] = _bf(y)
        yp = _nn(z_ref[...], w_ref[...])
        yp_ref[...] = _bf(yp)
        x1_ref[...] = _stream_tile(x_refs) + ms_ref[0] * yp

    ins = xs + [o_fw, o_bw, g_src]
    specs = _stream_specs(x, tm, dm) + [_rows(tm, hv), _rows(tm, hv), pl.BlockSpec((tm, hv), lambda i: (i, g_blk))]
    if has_gain:
        ins.append(gain)
        specs.append(_full(gain.shape))
    if aw:
        ins.append(a)
        specs.append(_rows(tm, aw))
    ins += [w_out, ms]
    specs += [_full(w_out.shape), _ctx_lat(dm)]
    return pl.pallas_call(
        body, name=name, grid=(T // tm,), in_specs=specs,
        out_specs=[_rows(tm, dm), _rows(tm, aw + hv), _rows(tm, dm)],
        out_shape=[jax.ShapeDtypeStruct((T, dm), F32), jax.ShapeDtypeStruct((T, aw + hv), ACT),
                   jax.ShapeDtypeStruct((T, dm), ACT)],
        compiler_params=_cp("arbitrary"),
    )(*ins)


def _post_bwd(dx1, z, yp, o_fw, o_bw, g_src, g_blk, gain, w_out, ms, aw, dvh, tm, name):
    T, dm = dx1.shape
    hv = o_fw.shape[1]
    has_gain = gain is not None

    def body(*refs):
        refs = list(refs)
        dx1_ref, z_ref, yp_ref, of_ref, ob_ref, g_ref = refs[:6]
        k = 6
        gain_ref = None
        if has_gain:
            gain_ref = refs[k]
            k += 1
        w_ref, ms_ref = refs[k:k + 2]
        k += 2
        do_ref, dgr_ref = refs[k:k + 2]
        k += 2
        da_ref = None
        if aw:
            da_ref = refs[k]
            k += 1
        dy_ref, zt_ref, dgate_ref, dgain_ref = refs[k:k + 4]
        i = pl.program_id(0)
        dx1v = dx1_ref[...]
        zt_ref[...] = z_ref[...].T
        _acc_ctx_lat(dgate_ref, i, jnp.sum(dx1v * yp_ref[...].astype(F32), axis=0, keepdims=True))
        dyb = _bf(dx1v * ms_ref[0])
        dy_ref[...] = dyb
        dz = _nt(dyb, w_ref[...])
        if aw:
            da_ref[...] = dz[:, :aw]
        o = of_ref[...].astype(F32) + ob_ref[...].astype(F32)
        gr = g_ref[...].astype(F32)
        dgain = jnp.zeros((1, dvh), F32)
        for hd in range(hv // dvh):
            sl = slice(hd * dvh, (hd + 1) * dvh)
            oh = o[:, sl]
            gh = gr[:, sl]
            dyh = dz[:, aw + hd * dvh:aw + (hd + 1) * dvh]
            r = lax.rsqrt(jnp.mean(oh * oh, axis=-1, keepdims=True) + EPS)
            n = oh * r
            s = _sig(gh)
            sl_g = gh * s
            gn = gain_ref[...] if has_gain else 1.0
            dgr_ref[:, sl] = _bf(dyh * n * gn * (s * (1.0 + gh * (1.0 - s))))
            dn = dyh * gn * sl_g
            dgain = dgain + jnp.sum(dyh * n * sl_g, axis=0, keepdims=True)
            do_ref[:, sl] = _bf(r * (dn - n * jnp.mean(dn * n, axis=-1, keepdims=True)))
        _acc_all(dgain_ref, i, dgain)

    ins = [dx1, z, yp, o_fw, o_bw, g_src]
    specs = [_rows(tm, dm), _rows(tm, aw + hv), _rows(tm, dm), _rows(tm, hv), _rows(tm, hv),
             pl.BlockSpec((tm, hv), lambda i: (i, g_blk))]
    if has_gain:
        ins.append(gain)
        specs.append(_full(gain.shape))
    ins += [w_out, ms]
    specs += [_full(w_out.shape), _ctx_lat(dm)]
    out_specs = [_rows(tm, hv), _rows(tm, hv)]
    out_shape = [jax.ShapeDtypeStruct((T, hv), ACT), jax.ShapeDtypeStruct((T, hv), ACT)]
    if aw:
        out_specs.append(_rows(tm, aw))
        out_shape.append(jax.ShapeDtypeStruct((T, aw), F32))
    out_specs += [_rows(tm, dm), _cols(aw + hv, tm), _ctx_lat(dm), _whole((1, dvh))]
    out_shape += [jax.ShapeDtypeStruct((T, dm), ACT), jax.ShapeDtypeStruct((aw + hv, T), ACT),
                  jax.ShapeDtypeStruct((2, 1, dm), F32), jax.ShapeDtypeStruct((1, dvh), F32)]
    return pl.pallas_call(
        body, name=name, grid=(T // tm,), in_specs=specs, out_specs=out_specs, out_shape=out_shape,
        compiler_params=_cp("arbitrary"),
    )(*ins)


def _loss_bwd(x, target, tm, name):
    T, dm = x.shape

    def body(x_ref, t_ref, dx_ref, loss_ref):
        i = pl.program_id(0)

        @pl.when(i == 0)
        def _():
            dx_ref[...] = jnp.zeros_like(dx_ref)
            loss_ref[...] = jnp.zeros_like(loss_ref)

        @pl.when(i > 0)
        def _():
            e = x_ref[...] - t_ref[...]
            dx_ref[...] = e * (1.0 / dm)
            loss_ref[...] += jnp.sum(e * e) * (0.5 / dm)

    return pl.pallas_call(
        body, name=name, grid=(T // tm,),
        in_specs=[_rows(tm, dm), pl.BlockSpec((tm, dm), lambda i: (jnp.maximum(i - 1, 0), 0))],
        out_specs=[_rows(tm, dm), _whole((1, 1))],
        out_shape=[jax.ShapeDtypeStruct((T, dm), F32), jax.ShapeDtypeStruct((1, 1), F32)],
        compiler_params=_cp("arbitrary"),
    )(x, target)


def _swap_matrix():
    r = lax.broadcasted_iota(jnp.int32, (HEAD_DIM, HEAD_DIM), 0)
    c = lax.broadcasted_iota(jnp.int32, (HEAD_DIM, HEAD_DIM), 1)
    return jnp.where((r + HEAD_DIM // 2) % HEAD_DIM == c, 1.0, 0.0).astype(BF16)


def _qk_prep_fwd(raw, gains, cos2, sin2, tq, name):
    nh, T, hd = raw.shape

    def body(x_ref, g_ref, c_ref, s_ref, o_ref):
        hidx = pl.program_id(0)
        xv = x_ref[0]
        r = lax.rsqrt(jnp.mean(xv * xv, axis=-1, keepdims=True) + EPS)
        n = xv * r * g_ref[0]
        y = n * c_ref[...] + _nn3r(n, _swap_matrix()) * s_ref[...]
        sc = jnp.where(hidx < ATTN_HEADS, HEAD_DIM ** -0.5, 1.0)
        o_ref[0] = _bf(y * sc)

    return pl.pallas_call(
        body, name=name, grid=(nh, T // tq),
        in_specs=[pl.BlockSpec((1, tq, hd), lambda h, i: (h, i, 0)), pl.BlockSpec((1, 1, hd), lambda h, i: (h, 0, 0)),
                  pl.BlockSpec((tq, hd), lambda h, i: (i, 0)), pl.BlockSpec((tq, hd), lambda h, i: (i, 0))],
        out_specs=pl.BlockSpec((1, tq, hd), lambda h, i: (h, i, 0)),
        out_shape=jax.ShapeDtypeStruct((nh, T, hd), ACT),
        compiler_params=_cp("arbitrary", "arbitrary"),
    )(raw, gains, cos2, sin2)


def _qk_prep_bwd(dy, raw, gains, cos2, sin2, tq, name):
    nh, T, hd = raw.shape

    def body(dy_ref, x_ref, g_ref, c_ref, s_ref, dx_ref, dg_ref):
        hidx = pl.program_id(0)
        i = pl.program_id(1)
        xv = x_ref[0]
        g = g_ref[0]
        r = lax.rsqrt(jnp.mean(xv * xv, axis=-1, keepdims=True) + EPS)
        xhat = xv * r
        sc = jnp.where(hidx < ATTN_HEADS, HEAD_DIM ** -0.5, 1.0)
        dyv = dy_ref[0] * sc
        dn = dyv * c_ref[...] + _nn3r(dyv * s_ref[...], _swap_matrix())
        _acc_all(dg_ref, i, jnp.sum(dn * xhat, axis=0, keepdims=True))
        dxh = dn * g
        dx_ref[0] = r * (dxh - xhat * jnp.mean(dxh * xhat, axis=-1, keepdims=True))

    return pl.pallas_call(
        body, name=name, grid=(nh, T // tq),
        in_specs=[pl.BlockSpec((1, tq, hd), lambda h, i: (h, i, 0)), pl.BlockSpec((1, tq, hd), lambda h, i: (h, i, 0)),
                  pl.BlockSpec((1, 1, hd), lambda h, i: (h, 0, 0)),
                  pl.BlockSpec((tq, hd), lambda h, i: (i, 0)), pl.BlockSpec((tq, hd), lambda h, i: (i, 0))],
        out_specs=[pl.BlockSpec((1, tq, hd), lambda h, i: (h, i, 0)), pl.BlockSpec((1, 1, hd), lambda h, i: (h, 0, 0))],
        out_shape=[jax.ShapeDtypeStruct((nh, T, hd), F32), jax.ShapeDtypeStruct((nh, 1, hd), F32)],
        compiler_params=_cp("arbitrary", "arbitrary"),
    )(dy, raw, gains, cos2, sin2)


def _attn_scores(q, k_ref, i, lc, T, sink):
    blk = ATTN_BLOCK
    kc = k_ref[0, pl.ds(blk, lc), :]
    kw = k_ref[0, pl.ds(pl.multiple_of(i * blk, blk), 3 * blk), :]
    s_c = _nt(q, kc)
    s_w = _nt(q, kw)
    row = lax.broadcasted_iota(jnp.int32, (4 * blk, 1), 0)
    qpos = i * blk + (row & (blk - 1))
    kpos = (i - 1) * blk + lax.broadcasted_iota(jnp.int32, (1, 3 * blk), 1)
    valid = (qpos >= lc) & (kpos >= lc) & (kpos < T) & (jnp.abs(kpos - qpos) <= WINDOW)
    s_w = jnp.where(valid, s_w, NEG)
    return kc, kw, s_c, s_w


def _attn_fwd(qt, kp, vp, sinkb, lc, name, ex=None):
    nh, T, hd = qt.shape
    blk = ATTN_BLOCK
    g = nh // ATTN_KV

    def body(q_ref, k_ref, v_ref, sink_ref, o_ref, lse_ref):
        i = pl.program_id(1)
        q = q_ref[...].reshape(g * blk, hd)
        sink = sink_ref[0]
        kc, kw, s_c, s_w = _attn_scores(q, k_ref, i, lc, T, sink)
        m = jnp.maximum(jnp.maximum(jnp.max(s_c, axis=-1, keepdims=True), jnp.max(s_w, axis=-1, keepdims=True)), sink)
        e_c = jnp.exp(s_c - m)
        e_w = jnp.exp(s_w - m)
        den = jnp.exp(sink - m) + jnp.sum(e_c, axis=-1, keepdims=True) + jnp.sum(e_w, axis=-1, keepdims=True)
        inv = 1.0 / den
        vc = v_ref[0, pl.ds(blk, lc), :]
        vw = v_ref[0, pl.ds(pl.multiple_of(i * blk, blk), 3 * blk), :]
        o = _nn(_bf(e_c * inv), vc) + _nn(_bf(e_w * inv), vw)
        o_ref[...] = o.reshape(g, blk, hd)
        lse_ref[...] = (m + jnp.log(den)).reshape(g, blk, 1)

    nb = T // blk
    return _host_call(
        body, ex, lambda: (pl.program_id(0) == 0) & (pl.program_id(1) == 0),
        lambda: (pl.program_id(0) == ATTN_KV - 1) & (pl.program_id(1) == nb - 1),
        name=name, grid=(ATTN_KV, nb),
        in_specs=[pl.BlockSpec((g, blk, hd), lambda kv, i: (kv, i, 0)),
                  pl.BlockSpec((1, T + 2 * blk, hd), lambda kv, i: (kv, 0, 0)),
                  pl.BlockSpec((1, T + 2 * blk, hd), lambda kv, i: (kv, 0, 0)),
                  pl.BlockSpec((1, g * blk, 1), lambda kv, i: (kv, 0, 0))],
        out_specs=[pl.BlockSpec((g, blk, hd), lambda kv, i: (kv, i, 0)),
                   pl.BlockSpec((g, blk, 1), lambda kv, i: (kv, i, 0))],
        out_shape=[jax.ShapeDtypeStruct((nh, T, hd), F32), jax.ShapeDtypeStruct((nh, T, 1), F32)],
        scratch_shapes=[], sem=("arbitrary", "arbitrary"), args=(qt, kp, vp, sinkb))


def _attn_bwd(qt, kp, vp, sinkb, o, lse, do, lc, name):
    nh, T, hd = qt.shape
    blk = ATTN_BLOCK
    g = nh // ATTN_KV

    def body(q_ref, k_ref, v_ref, sink_ref, o_ref, lse_ref, do_ref, dq_ref, dk_ref, dv_ref, ds_ref):
        i = pl.program_id(1)

        @pl.when(i == 0)
        def _():
            dk_ref[...] = jnp.zeros_like(dk_ref)
            dv_ref[...] = jnp.zeros_like(dv_ref)
            ds_ref[...] = jnp.zeros_like(ds_ref)

        q = q_ref[...].reshape(g * blk, hd)
        sink = sink_ref[0]
        lse = lse_ref[...].reshape(g * blk, 1)
        dov = do_ref[...].reshape(g * blk, hd)
        delta = jnp.sum(dov * o_ref[...].reshape(g * blk, hd), axis=-1, keepdims=True)
        kc, kw, s_c, s_w = _attn_scores(q, k_ref, i, lc, T, sink)
        p_c = jnp.exp(s_c - lse)
        p_w = jnp.exp(s_w - lse)
        win = pl.ds(pl.multiple_of(i * blk, blk), 3 * blk)
        vc = v_ref[0, pl.ds(blk, lc), :]
        vw = v_ref[0, win, :]
        dob = _bf(dov)
        ds_c = _bf(p_c * (_nt(dob, vc) - delta))
        ds_w = _bf(p_w * (_nt(dob, vw) - delta))
        dsr = -jnp.exp(sink - lse) * delta
        for hh in range(g):
            ds_ref[0, hh:hh + 1, :] += jnp.sum(dsr[hh * blk:(hh + 1) * blk, :], axis=0, keepdims=True)
        dq_ref[...] = (_nn(ds_c, kc) + _nn(ds_w, kw)).reshape(g, blk, hd)
        dk_ref[0, pl.ds(blk, lc), :] += _tn(ds_c, q)
        dk_ref[0, win, :] += _tn(ds_w, q)
        dv_ref[0, pl.ds(blk, lc), :] += _tn(_bf(p_c), dob)
        dv_ref[0, win, :] += _tn(_bf(p_w), dob)

    qspec = pl.BlockSpec((g, blk, hd), lambda kv, i: (kv, i, 0))
    kspec = pl.BlockSpec((1, T + 2 * blk, hd), lambda kv, i: (kv, 0, 0))
    lspec = pl.BlockSpec((g, blk, 1), lambda kv, i: (kv, i, 0))
    return pl.pallas_call(
        body, name=name, grid=(ATTN_KV, T // blk),
        in_specs=[qspec, kspec, kspec, pl.BlockSpec((1, g * blk, 1), lambda kv, i: (kv, 0, 0)), qspec, lspec, qspec],
        out_specs=[qspec, kspec, kspec, pl.BlockSpec((1, g, 1), lambda kv, i: (kv, 0, 0))],
        out_shape=[jax.ShapeDtypeStruct((nh, T, hd), F32), jax.ShapeDtypeStruct((ATTN_KV, T + 2 * blk, hd), F32),
                   jax.ShapeDtypeStruct((ATTN_KV, T + 2 * blk, hd), F32), jax.ShapeDtypeStruct((ATTN_KV, g, 1), F32)],
        compiler_params=_cp("arbitrary", "arbitrary"),
    )(qt, kp, vp, sinkb, o, lse, do)


PAIR = 2 * HEAD_DIM
N_PAIRS = (ATTN_HEADS + ATTN_KV) // 2


def _lanes():
    return lax.broadcasted_iota(jnp.int32, (1, PAIR), 1)


def _swap32(v):
    first_half = (_lanes() & (HEAD_DIM // 2)) == 0
    return jnp.where(first_half, pltpu.roll(v, PAIR - HEAD_DIM // 2, 1), pltpu.roll(v, HEAD_DIM // 2, 1))


def _head_mean(v):
    r = lax.broadcasted_iota(jnp.int32, (PAIR, PAIR), 0)
    c = lax.broadcasted_iota(jnp.int32, (PAIR, PAIR), 1)
    same = jnp.where((r >= HEAD_DIM) == (c >= HEAD_DIM), 1.0, 0.0).astype(BF16)
    return _nn3r(v, same) * (1.0 / HEAD_DIM)


def _qk_tile_fwd(pa, g_ref, cosv, sinv, q_ref, k_ref, v_ref):
    qw = ATTN_HEADS * HEAD_DIM
    for p in range(N_PAIRS):
        xv = pa[:, p * PAIR:(p + 1) * PAIR]
        n = xv * lax.rsqrt(_head_mean(xv * xv) + EPS) * g_ref[p]
        y = n * cosv + _swap32(n) * sinv
        if p < N_PAIRS - 1:
            q_ref[:, p * PAIR:(p + 1) * PAIR] = _bf(y * HEAD_DIM ** -0.5)
        else:
            k_ref[...] = _bf(y)
    v_ref[...] = _bf(pa[:, qw + PAIR:])


def _qk_tile_bwd(dq_ref, dk_ref, pa_ref, g_ref, cosv, sinv):
    dxs, dgs = [], []
    for p in range(N_PAIRS):
        sl = slice(p * PAIR, (p + 1) * PAIR)
        xv = pa_ref[:, sl]
        r = lax.rsqrt(_head_mean(xv * xv) + EPS)
        xhat = xv * r
        dy = dq_ref[:, sl] * HEAD_DIM ** -0.5 if p < N_PAIRS - 1 else dk_ref[...]
        dn = dy * cosv + _swap32(dy * sinv)
        dgs.append(jnp.sum(dn * xhat, axis=0, keepdims=True))
        dxh = dn * g_ref[p]
        dxs.append(r * (dxh - xhat * _head_mean(dxh * xhat)))
    return jnp.concatenate(dxs, axis=1), dgs


def _qk_slab_fwd(pa, gains, cosp, sinp, tm, name):
    T = pa.shape[0]
    qw = ATTN_HEADS * HEAD_DIM

    def body(pa_ref, g_ref, c_ref, s_ref, q_ref, k_ref, v_ref):
        cosv, sinv = c_ref[...], s_ref[...]
        for p in range(N_PAIRS):
            xv = pa_ref[:, p * PAIR:(p + 1) * PAIR]
            n = xv * lax.rsqrt(_head_mean(xv * xv) + EPS) * g_ref[p]
            y = n * cosv + _swap32(n) * sinv
            if p < N_PAIRS - 1:
                q_ref[:, p * PAIR:(p + 1) * PAIR] = _bf(y * HEAD_DIM ** -0.5)
            else:
                k_ref[...] = _bf(y)
        v_ref[...] = _bf(pa_ref[:, qw + PAIR:])

    return pl.pallas_call(
        body, name=name, grid=(T // tm,),
        in_specs=[_rows(tm, pa.shape[1]), _full(gains.shape), _rows(tm, PAIR), _rows(tm, PAIR)],
        out_specs=[_rows(tm, qw), _rows(tm, PAIR), _rows(tm, PAIR)],
        out_shape=[jax.ShapeDtypeStruct((T, qw), ACT), jax.ShapeDtypeStruct((T, PAIR), ACT),
                   jax.ShapeDtypeStruct((T, PAIR), ACT)],
        compiler_params=_cp("arbitrary"),
    )(pa, gains, cosp, sinp)


def _qk_slab_bwd(dq, dk, pa, gains, cosp, sinp, tm, name):
    T = pa.shape[0]
    qw = ATTN_HEADS * HEAD_DIM

    def body(dq_ref, dk_ref, pa_ref, g_ref, c_ref, s_ref, dx_ref, dg_ref):
        i = pl.program_id(0)
        cosv, sinv = c_ref[...], s_ref[...]
        for p in range(N_PAIRS):
            sl = slice(p * PAIR, (p + 1) * PAIR)
            xv = pa_ref[:, sl]
            r = lax.rsqrt(_head_mean(xv * xv) + EPS)
            xhat = xv * r
            dy = dq_ref[:, sl] * HEAD_DIM ** -0.5 if p < N_PAIRS - 1 else dk_ref[...]
            dn = dy * cosv + _swap32(dy * sinv)
            _acc_all(dg_ref.at[p], i, jnp.sum(dn * xhat, axis=0, keepdims=True))
            dxh = dn * g_ref[p]
            dx_ref[:, sl] = r * (dxh - xhat * _head_mean(dxh * xhat))

    return pl.pallas_call(
        body, name=name, grid=(T // tm,),
        in_specs=[_rows(tm, qw), _rows(tm, PAIR), _rows(tm, qw + PAIR), _full(gains.shape), _rows(tm, PAIR), _rows(tm, PAIR)],
        out_specs=[_rows(tm, qw + PAIR), _whole(gains.shape)],
        out_shape=[jax.ShapeDtypeStruct((T, qw + PAIR), F32), jax.ShapeDtypeStruct(gains.shape, F32)],
        compiler_params=_cp("arbitrary"),
    )(dq, dk, pa, gains, cosp, sinp)


def _attn_window(ref, i, nb):
    blk = ATTN_BLOCK
    starts = [pl.multiple_of(jnp.clip(i + d, 0, nb - 1) * blk, blk) for d in (-1, 0, 1)]
    return starts, jnp.concatenate([ref[pl.ds(s, blk), :] for s in starts], axis=0)


GROUP_HEADS = 2
ATTN_STEP_BLOCKS = 2


def _head_groups(n):
    g = ATTN_HEADS // ATTN_KV
    return [(kv, [kv * g + s + j for j in range(n)]) for kv in range(ATTN_KV) for s in range(0, g, n)]


def _attn_mask(i, lc, T, rows):
    blk = ATTN_BLOCK
    row = lax.broadcasted_iota(jnp.int32, (rows, 1), 0)
    qpos = i * blk + (row & (blk - 1))
    kpos = (i - 1) * blk + lax.broadcasted_iota(jnp.int32, (1, 3 * blk), 1)
    return (qpos >= lc) & (kpos >= lc) & (kpos < T) & (jnp.abs(kpos - qpos) <= WINDOW)


def _to_kv_half(v, head, kv):
    return v if head % 2 == kv else pltpu.roll(v, HEAD_DIM, 1)


def _attn_slab_fwd(qt, ks, vs, sinkb, lc, name, ex=None):
    T = qt.shape[0]
    blk = ATTN_BLOCK
    nb = T // blk
    g = ATTN_HEADS // ATTN_KV

    spb = ATTN_STEP_BLOCKS
    ng = nb // spb

    def one_block(i, rows, q_ref, k_ref, v_ref, sink_ref, o_ref, lse_ref):
        lane = _lanes()
        valid = _attn_mask(i, lc, T, GROUP_HEADS * blk)
        kc_all, vc = k_ref[0:lc, :], v_ref[0:lc, :]
        _, kw_all = _attn_window(k_ref, i, nb)
        _, vw = _attn_window(v_ref, i, nb)
        kc, kw = [], []
        for kv in range(ATTN_KV):
            mine = (lane >= kv * HEAD_DIM) & (lane < (kv + 1) * HEAD_DIM)
            kc.append(jnp.where(mine, kc_all, jnp.zeros_like(kc_all)))
            kw.append(jnp.where(mine, kw_all, jnp.zeros_like(kw_all)))
        groups = _head_groups(GROUP_HEADS)
        qg = [jnp.concatenate([_to_kv_half(q_ref[rows, (h // 2) * PAIR:(h // 2 + 1) * PAIR], h, kv) for h in heads], axis=0)
              for kv, heads in groups]
        sinks = [sink_ref[kv, (heads[0] - kv * g) * blk:(heads[-1] + 1 - kv * g) * blk] for kv, heads in groups]
        s_c = [_nt(q, kc[kv]) for q, (kv, _) in zip(qg, groups)]
        s_w = [jnp.where(valid, _nt(q, kw[kv]), NEG) for q, (kv, _) in zip(qg, groups)]
        m = [jnp.maximum(jnp.maximum(jnp.max(a, axis=-1, keepdims=True), jnp.max(b, axis=-1, keepdims=True)), s)
             for a, b, s in zip(s_c, s_w, sinks)]
        e_c = [jnp.exp(a - mm) for a, mm in zip(s_c, m)]
        e_w = [jnp.exp(b - mm) for b, mm in zip(s_w, m)]
        den = [jnp.exp(s - mm) + jnp.sum(a, axis=-1, keepdims=True) + jnp.sum(b, axis=-1, keepdims=True)
               for s, mm, a, b in zip(sinks, m, e_c, e_w)]
        inv = [1.0 / d for d in den]
        og = [_nn(_bf(a * r), vc) + _nn(_bf(b * r), vw) for a, b, r in zip(e_c, e_w, inv)]
        placed = [None] * ATTN_HEADS
        for (kv, heads), o2, mm, d in zip(groups, og, m, den):
            lse_ref[heads[0]:heads[-1] + 1, rows, :] = (mm + jnp.log(d)).reshape(len(heads), blk, 1)
            for j, h in enumerate(heads):
                placed[h] = _to_kv_half(o2[j * blk:(j + 1) * blk], h, kv)
        for p in range(ATTN_HEADS // 2):
            o_ref[rows, p * PAIR:(p + 1) * PAIR] = jnp.where(lane < HEAD_DIM, placed[2 * p], placed[2 * p + 1])

    def body(*refs):
        for j in range(spb):
            one_block(pl.program_id(0) * spb + j, pl.ds(j * blk, blk), *refs)

    qw = ATTN_HEADS * HEAD_DIM
    return _host_call(
        body, ex, lambda: pl.program_id(0) == 0, lambda: pl.program_id(0) == ng - 1,
        name=name, grid=(ng,),
        in_specs=[_rows(spb * blk, qw), _full((T, PAIR)), _full((T, PAIR)), _full(sinkb.shape)],
        out_specs=[_rows(spb * blk, qw), pl.BlockSpec((ATTN_HEADS, spb * blk, 1), lambda i: (0, i, 0))],
        out_shape=[jax.ShapeDtypeStruct((T, qw), F32), jax.ShapeDtypeStruct((ATTN_HEADS, T, 1), F32)],
        scratch_shapes=[], sem=("arbitrary",), args=(qt, ks, vs, sinkb))


def _attn_slab_bwd(qt, ks, vs, sinkb, o, lse, do, lc, name, ex=None):
    T = qt.shape[0]
    blk = ATTN_BLOCK
    nb = T // blk
    g = ATTN_HEADS // ATTN_KV

    spb = ATTN_STEP_BLOCKS
    ng = nb // spb

    def body(*refs):
        dk_ref, dv_ref, ds_ref = refs[8:11]

        @pl.when(pl.program_id(0) == 0)
        def _():
            dk_ref[...] = jnp.zeros_like(dk_ref)
            dv_ref[...] = jnp.zeros_like(dv_ref)
            ds_ref[...] = jnp.zeros_like(ds_ref)

        for j in range(spb):
            one_block(pl.program_id(0) * spb + j, pl.ds(j * blk, blk), *refs)

    def one_block(i, rows, q_ref, k_ref, v_ref, sink_ref, o_ref, lse_ref, do_ref, dq_ref, dk_ref, dv_ref, ds_ref):
        lane = _lanes()
        valid = _attn_mask(i, lc, T, g * blk)
        kc_all, vc_all = k_ref[0:lc, :], v_ref[0:lc, :]
        starts, kw_all = _attn_window(k_ref, i, nb)
        _, vw_all = _attn_window(v_ref, i, nb)
        dq_pairs = [jnp.zeros((blk, PAIR), F32) for _ in range(ATTN_HEADS // 2)]
        for kv in range(ATTN_KV):
            mine = (lane >= kv * HEAD_DIM) & (lane < (kv + 1) * HEAD_DIM)

            def only(v):
                return jnp.where(mine, v, jnp.zeros_like(v))

            kc, kw, vc, vw = only(kc_all), only(kw_all), only(vc_all), only(vw_all)
            heads = [kv * g + j for j in range(g)]
            qs, dos, deltas = [], [], []
            for h in heads:
                sl = slice((h // 2) * PAIR, (h // 2 + 1) * PAIR)
                dov = do_ref[rows, sl]
                qs.append(_to_kv_half(q_ref[rows, sl], h, kv))
                dos.append(_bf(_to_kv_half(dov, h, kv)))
                own = (lane < HEAD_DIM) if h % 2 == 0 else (lane >= HEAD_DIM)
                deltas.append(jnp.sum(jnp.where(own, dov * o_ref[rows, sl], 0.0), axis=-1, keepdims=True))
            q4, do4, delta = jnp.concatenate(qs, axis=0), jnp.concatenate(dos, axis=0), jnp.concatenate(deltas, axis=0)
            sink = sink_ref[kv]
            lse = lse_ref[kv * g:(kv + 1) * g, rows, :].reshape(g * blk, 1)
            p_c = jnp.exp(_nt(q4, kc) - lse)
            p_w = jnp.exp(jnp.where(valid, _nt(q4, kw), NEG) - lse)
            ds_c = _bf(p_c * (_nt(do4, vc) - delta))
            ds_w = _bf(p_w * (_nt(do4, vw) - delta))
            dsr = -jnp.exp(sink - lse) * delta
            dq4 = _nn(ds_c, kc) + _nn(ds_w, kw)
            for j, h in enumerate(heads):
                ds_ref[h:h + 1, :] += jnp.sum(dsr[j * blk:(j + 1) * blk, :], axis=0, keepdims=True)
                dq_pairs[h // 2] = dq_pairs[h // 2] + _to_kv_half(dq4[j * blk:(j + 1) * blk], h, kv)
            dk_ref[0:lc, :] += only(_tn(ds_c, q4))
            dv_ref[0:lc, :] += only(_tn(_bf(p_c), do4))
            dkw = only(_tn(ds_w, q4))
            dvw = only(_tn(_bf(p_w), do4))
            for b, s in enumerate(starts):
                dk_ref[pl.ds(s, blk), :] += dkw[b * blk:(b + 1) * blk]
                dv_ref[pl.ds(s, blk), :] += dvw[b * blk:(b + 1) * blk]
        for p in range(ATTN_HEADS // 2):
            dq_ref[rows, p * PAIR:(p + 1) * PAIR] = dq_pairs[p]

    qw = ATTN_HEADS * HEAD_DIM
    lspec = pl.BlockSpec((ATTN_HEADS, spb * blk, 1), lambda i: (0, i, 0))
    return _host_call(
        body, ex, lambda: pl.program_id(0) == 0, lambda: pl.program_id(0) == ng - 1,
        name=name, grid=(ng,),
        in_specs=[_rows(spb * blk, qw), _full((T, PAIR)), _full((T, PAIR)), _full(sinkb.shape), _rows(spb * blk, qw), lspec,
                  _rows(spb * blk, qw)],
        out_specs=[_rows(spb * blk, qw), _whole((T, PAIR)), _whole((T, PAIR)), _whole((ATTN_HEADS, 1))],
        out_shape=[jax.ShapeDtypeStruct((T, qw), F32), jax.ShapeDtypeStruct((T, PAIR), F32),
                   jax.ShapeDtypeStruct((T, PAIR), F32), jax.ShapeDtypeStruct((ATTN_HEADS, 1), F32)],
        scratch_shapes=[], sem=("arbitrary",), args=(qt, ks, vs, sinkb, o, lse, do))


def _fw_chunk(s, nc, nt):
    return s


def _bw_chunk(s, nc, nt):
    return jnp.where(s < nc, nc - 1 - s, nt - 1 - (s - nc))


def _tri(c, rev):
    r = lax.broadcasted_iota(jnp.int32, (c, c), 0)
    k = lax.broadcasted_iota(jnp.int32, (c, c), 1)
    return (k >= r) if rev else (k <= r)


def _gla_gates(z, lb, rev):
    c = HG_CHUNK
    sg = _sig(z)
    f = lb + (1.0 - lb) * sg
    cum = _nn3(jnp.where(_tri(c, rev), 1.0, 0.0).astype(BF16), jnp.log(f))
    mid = c - 1 - c // 2 if rev else c // 2
    last = 0 if rev else c - 1
    return sg, f, cum, cum[mid:mid + 1], cum[last:last + 1], last


def _lower_bound(lbraw_ref):
    lr = lbraw_ref[...]
    return _sig(lr[0:1] - lr[1:2])


def _gla_fwd(pb, lbraw, lc, name, ex=None):
    T = pb.shape[0]
    c, hw, d, ns = HG_CHUNK, HG_HEADS * HG_D, HG_D, HG_STEP_CHUNKS
    nt, nc = T // (ns * c), lc // (ns * c)
    orders = (_fw_chunk, _bw_chunk)

    def body(qf, zf, vf, qb, zb, vb, lb_ref, of_ref, ob_ref, sf_ref, sb_ref, st_ref):
        @pl.when(pl.program_id(0) == 0)
        def _():
            st_ref[...] = jnp.zeros_like(st_ref)

        lb = _lower_bound(lb_ref)
        dirs = ((qf, zf, vf, of_ref, sf_ref), (qb, zb, vb, ob_ref, sb_ref))
        combos = [(dr, h, slice(h * d, (h + 1) * d)) for dr in range(2) for h in range(HG_HEADS)]
        for j in range(ns):
            sub = (j, ns - 1 - j)
            rows = [pl.ds(sub[dr] * c, c) for dr in range(2)]
            prep = []
            for dr, (q_ref, z_ref, v_ref, _, _) in enumerate(dirs):
                rev = dr == 1
                qr = q_ref[rows[dr], :]
                q = qr * _sig(qr)
                _, f, cum, ref, last, _ = _gla_gates(z_ref[rows[dr], :], lb, rev)
                k = 1.0 - f
                prep.append(dict(q1=_bf(q * jnp.exp(cum - ref)), k1=_bf(k * jnp.exp(ref - cum)), q2=_bf(q * jnp.exp(cum)),
                                 k2=_bf(k * jnp.exp(last - cum)), el=jnp.exp(last), v=_bf(v_ref[rows[dr], :]),
                                 mask=_tri(c, rev)))
            a = [_bf(jnp.where(prep[dr]["mask"], _nt(prep[dr]["q1"][:, sl], prep[dr]["k1"][:, sl]), 0.0))
                 for dr, _, sl in combos]
            for (dr, h, sl), a_h in zip(combos, a):
                p = prep[dr]
                o_ref, s_ref = dirs[dr][3], dirs[dr][4]
                st = st_ref[dr, h]
                stb = _bf(st)
                s_ref[sub[dr], h] = stb
                o_ref[rows[dr], sl] = _nn(a_h, p["v"][:, sl]) + _nt(p["q2"][:, sl], stb)
                st_ref[dr, h] = st * p["el"][:, sl] + _tn(p["v"][:, sl], p["k2"][:, sl])

    def col(order, blkcol):
        return pl.BlockSpec((ns * c, hw), lambda s: (order(s, nc, nt), blkcol))

    def st_spec(order):
        return pl.BlockSpec((ns, HG_HEADS, d, d), lambda s: (order(s, nc, nt), 0, 0, 0))

    in_specs = []
    for dr, order in enumerate(orders):
        in_specs += [col(order, 0), col(order, 1 + dr), col(order, 3)]
    in_specs.append(_full(lbraw.shape))
    return _host_call(
        body, ex, lambda: pl.program_id(0) == 0, lambda: pl.program_id(0) == nt - 1,
        name=name, grid=(nt,), in_specs=in_specs,
        out_specs=[col(_fw_chunk, 0), col(_bw_chunk, 0), st_spec(_fw_chunk), st_spec(_bw_chunk)],
        out_shape=[jax.ShapeDtypeStruct((T, hw), F32), jax.ShapeDtypeStruct((T, hw), F32),
                   jax.ShapeDtypeStruct((nt * ns, HG_HEADS, d, d), ACT), jax.ShapeDtypeStruct((nt * ns, HG_HEADS, d, d), ACT)],
        scratch_shapes=[pltpu.VMEM((2, HG_HEADS, d, d), F32)], sem=("arbitrary",),
        args=(pb, pb, pb, pb, pb, pb, lbraw))


def _gla_bwd(pb, lbraw, s_fw, s_bw, do, lc, name, ex=None):
    T = pb.shape[0]
    c, hw, d, ns = HG_CHUNK, HG_HEADS * HG_D, HG_D, HG_STEP_CHUNKS
    nt, nc = T // (ns * c), lc // (ns * c)

    def rfw(s, nc_, nt_):
        return _fw_chunk(nt_ - 1 - s, nc_, nt_)

    def rbw(s, nc_, nt_):
        return _bw_chunk(nt_ - 1 - s, nc_, nt_)

    def body(qf, zf, vf, sf, dof, qb, zb, vb, sb, dob_, lb_ref,
             dqf, dzf, dvf, dqb, dzb, dvb, dlb_ref, dst_ref):
        step = pl.program_id(0)

        @pl.when(step == 0)
        def _():
            dst_ref[...] = jnp.zeros_like(dst_ref)

        lb = _lower_bound(lb_ref)
        sets = ((qf, zf, vf, sf, dof, dqf, dzf, dvf), (qb, zb, vb, sb, dob_, dqb, dzb, dvb))
        combos = [(dr, h, slice(h * d, (h + 1) * d)) for dr in range(2) for h in range(HG_HEADS)]
        dlb_tot = jnp.zeros((1, hw), F32)
        for j in range(ns):
            sub = (ns - 1 - j, j)
            rows = [pl.ds(sub[dr] * c, c) for dr in range(2)]
            prep = []
            for dr, (q_ref, z_ref, v_ref, _, do_ref, _, _, _) in enumerate(sets):
                rev = dr == 1
                qr = q_ref[rows[dr], :]
                sq = _sig(qr)
                q = qr * sq
                sg, f, cum, ref, last, last_row = _gla_gates(z_ref[rows[dr], :], lb, rev)
                k = 1.0 - f
                e_qr, e_kr, e_q, e_kl = jnp.exp(cum - ref), jnp.exp(ref - cum), jnp.exp(cum), jnp.exp(last - cum)
                q1, k1, q2, k2 = q * e_qr, k * e_kr, q * e_q, k * e_kl
                prep.append(dict(qr=qr, sq=sq, sg=sg, f=f, e_qr=e_qr, e_kr=e_kr, e_q=e_q, e_kl=e_kl, el=jnp.exp(last),
                                 q1=q1, k1=k1, q2=q2, k2=k2, q1b=_bf(q1), k1b=_bf(k1), q2b=_bf(q2), k2b=_bf(k2),
                                 vb=_bf(v_ref[rows[dr], :]), dob=_bf(do_ref[rows[dr], :]), mask=_tri(c, rev),
                                 last_row=last_row, acc_t=jnp.where(_tri(c, not rev), 1.0, 0.0).astype(BF16)))
            a = [_bf(jnp.where(prep[dr]["mask"], _nt(prep[dr]["q1b"][:, sl], prep[dr]["k1b"][:, sl]), 0.0))
                 for dr, _, sl in combos]
            da = [_bf(jnp.where(prep[dr]["mask"], _nt(prep[dr]["dob"][:, sl], prep[dr]["vb"][:, sl]), 0.0))
                  for dr, _, sl in combos]
            parts = [dict(dq1=[], dk1=[], dq2=[], dk2=[], dls=[]) for _ in range(2)]
            for (dr, h, sl), a_h, da_h in zip(combos, a, da):
                p = prep[dr]
                s_ref, dv_ref = sets[dr][3], sets[dr][7]
                stb = s_ref[sub[dr], h]
                dst = dst_ref[dr, h]
                dstb = _bf(dst)
                dob_h, vb_h = p["dob"][:, sl], p["vb"][:, sl]
                dv_ref[rows[dr], sl] = _bf(_tn(a_h, dob_h) + _nt(p["k2b"][:, sl], dstb))
                parts[dr]["dq1"].append(_nn(da_h, p["k1b"][:, sl]))
                parts[dr]["dk1"].append(_tn(da_h, p["q1b"][:, sl]))
                parts[dr]["dq2"].append(_nn(dob_h, stb))
                parts[dr]["dk2"].append(_nn(vb_h, dstb))
                el_h = p["el"][:, sl]
                dst_ref[dr, h] = _tn(dob_h, p["q2b"][:, sl]) + dst * el_h
                parts[dr]["dls"].append(jnp.sum(dst * stb.astype(F32), axis=0, keepdims=True) * el_h)
            for dr in range(2):
                p = prep[dr]
                dq_ref, dz_ref = sets[dr][5], sets[dr][6]
                dq1, dk1, dq2, dk2, dls = (jnp.concatenate(parts[dr][n], axis=1) for n in ("dq1", "dk1", "dq2", "dk2", "dls"))
                dq = dq1 * p["e_qr"] + dq2 * p["e_q"]
                dk = dk1 * p["e_kr"] + dk2 * p["e_kl"]
                dcum = dq1 * p["q1"] - dk1 * p["k1"] + dq2 * p["q2"] - dk2 * p["k2"]
                dlast = jnp.sum(dk2 * p["k2"], axis=0, keepdims=True) + dls
                rowid = lax.broadcasted_iota(jnp.int32, (c, 1), 0)
                dcum = dcum + jnp.where(rowid == p["last_row"], dlast, 0.0)
                df = _nn3(p["acc_t"], dcum) / p["f"] - dk
                sg = p["sg"]
                dz_ref[rows[dr], :] = _bf(df * (1.0 - lb) * sg * (1.0 - sg))
                dlb_tot = dlb_tot + jnp.sum(df * (1.0 - sg), axis=0, keepdims=True)
                dq_ref[rows[dr], :] = _bf(dq * (p["sq"] * (1.0 + p["qr"] * (1.0 - p["sq"]))))
        _acc_all(dlb_ref, step, dlb_tot)

    def col(order, blkcol):
        return pl.BlockSpec((ns * c, hw), lambda s: (order(s, nc, nt), blkcol))

    def st_spec(order):
        return pl.BlockSpec((ns, HG_HEADS, d, d), lambda s: (order(s, nc, nt), 0, 0, 0))

    in_specs = []
    for dr, order in enumerate((rfw, rbw)):
        in_specs += [col(order, 0), col(order, 1 + dr), col(order, 3), st_spec(order), col(order, 0)]
    in_specs.append(_full(lbraw.shape))
    out_specs = [col(rfw, 0)] * 3 + [col(rbw, 0)] * 3 + [_whole((1, hw))]
    out_shape = [jax.ShapeDtypeStruct((T, hw), ACT)] * 6 + [jax.ShapeDtypeStruct((1, hw), F32)]
    return _host_call(
        body, ex, lambda: pl.program_id(0) == 0, lambda: pl.program_id(0) == nt - 1,
        name=name, grid=(nt,), in_specs=in_specs, out_specs=out_specs, out_shape=out_shape,
        scratch_shapes=[pltpu.VMEM((2, HG_HEADS, d, d), F32)], sem=("arbitrary",),
        args=(pb, pb, pb, s_fw, do, pb, pb, pb, s_bw, do, lbraw))


def _ret_log_gamma(h, rev):
    hh = RET_HEADS - 1 - h if rev else h
    return math.log(1.0 - 2.0 ** (-5.0 - hh))


def _rope(x, cos, sin):
    half = x.shape[1] // 2
    x1, x2 = x[:, :half], x[:, half:]
    return jnp.concatenate([x1 * cos - x2 * sin, x2 * cos + x1 * sin], axis=1)


def _unrope(dy, cos, sin):
    half = dy.shape[1] // 2
    d1, d2 = dy[:, :half], dy[:, half:]
    return jnp.concatenate([d1 * cos + d2 * sin, d2 * cos - d1 * sin], axis=1)


def _ret_decays(lg, rev):
    c = RET_CHUNK
    r = lax.broadcasted_iota(jnp.int32, (c, c), 0)
    k = lax.broadcasted_iota(jnp.int32, (c, c), 1)
    rel = (k - r) if rev else (r - k)
    dm = jnp.where(rel >= 0, jnp.exp(lg * jnp.maximum(rel, 0).astype(F32)), 0.0)
    pos = lax.broadcasted_iota(jnp.int32, (c, 1), 0).astype(F32)
    if rev:
        qdec = jnp.exp(lg * (c - pos))
        kdec = jnp.exp(lg * pos)
    else:
        qdec = jnp.exp(lg * (pos + 1.0))
        kdec = jnp.exp(lg * (c - 1.0 - pos))
    return dm, qdec, kdec


def _ret_fwd(q, k, v, cos, sin, lc, name, ex=None):
    T = q.shape[0]
    c, dk, dv = RET_CHUNK, RET_DK, RET_DV
    nt, nc = T // c, lc // c
    kscale = dk ** -0.5

    def body(qf, kf, vf, cf, sf_, qb, kb, vb, cb, sb_, of_ref, ob_ref, stf_ref, stb_ref, st_ref):
        @pl.when(pl.program_id(0) == 0)
        def _():
            st_ref[...] = jnp.zeros_like(st_ref)

        sets = ((qf, kf, vf, cf, sf_, of_ref, stf_ref), (qb, kb, vb, cb, sb_, ob_ref, stb_ref))
        combos = [(dr, h) for dr in range(2) for h in range(RET_HEADS)]
        prep = {}
        for dr, (q_ref, k_ref, v_ref, c_ref, s_ref, _, _) in enumerate(sets):
            rev = dr == 1
            cos_v, sin_v = c_ref[...], s_ref[...]
            for h in range(RET_HEADS):
                lg = _ret_log_gamma(h, rev)
                dm, qdec, kdec = _ret_decays(lg, rev)
                qh = _rope(q_ref[:, h * dk:(h + 1) * dk].astype(F32), cos_v, sin_v)
                kh = _rope(k_ref[:, h * dk:(h + 1) * dk].astype(F32), cos_v, sin_v) * kscale
                prep[dr, h] = dict(qb=_bf(qh), kb=_bf(kh), qin=_bf(qh * qdec), kin=_bf(kh * kdec),
                                   v=_bf(v_ref[:, h * dv:(h + 1) * dv]), dm=dm, decay=math.exp(lg * c))
        sc = {ch: _bf(_nt(prep[ch]["qb"], prep[ch]["kb"]) * prep[ch]["dm"]) for ch in combos}
        for dr, h in combos:
            p = prep[dr, h]
            o_ref, so_ref = sets[dr][5], sets[dr][6]
            st = st_ref[dr, h]
            stb = _bf(st)
            so_ref[0, h] = stb
            o_ref[:, h * dv:(h + 1) * dv] = _bf(_nn(sc[dr, h], p["v"]) + _nt(p["qin"], stb))
            st_ref[dr, h] = st * p["decay"] + _tn(p["v"], p["kin"])

    def spec(order, width):
        return pl.BlockSpec((c, width), lambda s: (order(s, nc, nt), 0))

    def st_spec(order):
        return pl.BlockSpec((1, RET_HEADS, dv, dk), lambda s: (order(s, nc, nt), 0, 0, 0))

    in_specs = []
    for order in (_fw_chunk, _bw_chunk):
        in_specs += [spec(order, RET_HEADS * dk), spec(order, RET_HEADS * dk), spec(order, RET_HEADS * dv),
                     spec(order, dk // 2), spec(order, dk // 2)]
    return _host_call(
        body, ex, lambda: pl.program_id(0) == 0, lambda: pl.program_id(0) == nt - 1,
        name=name, grid=(nt,), in_specs=in_specs,
        out_specs=[spec(_fw_chunk, RET_HEADS * dv), spec(_bw_chunk, RET_HEADS * dv), st_spec(_fw_chunk), st_spec(_bw_chunk)],
        out_shape=[jax.ShapeDtypeStruct((T, RET_HEADS * dv), ACT), jax.ShapeDtypeStruct((T, RET_HEADS * dv), ACT),
                   jax.ShapeDtypeStruct((nt, RET_HEADS, dv, dk), ACT), jax.ShapeDtypeStruct((nt, RET_HEADS, dv, dk), ACT)],
        scratch_shapes=[pltpu.VMEM((2, RET_HEADS, dv, dk), F32)], sem=("arbitrary",),
        args=(q, k, v, cos, sin, q, k, v, cos, sin))


def _ret_bwd(q, k, v, cos, sin, s_fw, s_bw, do, lc, name, ex=None):
    T = q.shape[0]
    c, dk, dv = RET_CHUNK, RET_DK, RET_DV
    nt, nc = T // c, lc // c
    kscale = dk ** -0.5

    def rfw(s, nc_, nt_):
        return _fw_chunk(nt_ - 1 - s, nc_, nt_)

    def rbw(s, nc_, nt_):
        return _bw_chunk(nt_ - 1 - s, nc_, nt_)

    def body(qf, kf, vf, cf, sf_, stf, dof, qb, kb, vb, cb, sb_, stb_, dob_,
             dqf, dkf, dvf, dqb, dkb, dvb, dst_ref):
        @pl.when(pl.program_id(0) == 0)
        def _():
            dst_ref[...] = jnp.zeros_like(dst_ref)

        sets = ((qf, kf, vf, cf, sf_, stf, dof, dqf, dkf, dvf), (qb, kb, vb, cb, sb_, stb_, dob_, dqb, dkb, dvb))
        combos = [(dr, h) for dr in range(2) for h in range(RET_HEADS)]
        prep = {}
        for dr, (q_ref, k_ref, v_ref, c_ref, s_ref, _, do_ref, _, _, _) in enumerate(sets):
            rev = dr == 1
            cos_v, sin_v = c_ref[...], s_ref[...]
            for h in range(RET_HEADS):
                lg = _ret_log_gamma(h, rev)
                dm, qdec, kdec = _ret_decays(lg, rev)
                qh = _rope(q_ref[:, h * dk:(h + 1) * dk].astype(F32), cos_v, sin_v)
                kh = _rope(k_ref[:, h * dk:(h + 1) * dk].astype(F32), cos_v, sin_v) * kscale
                prep[dr, h] = dict(qb=_bf(qh), kb=_bf(kh), qin=_bf(qh * qdec), kin=_bf(kh * kdec),
                                   v=_bf(v_ref[:, h * dv:(h + 1) * dv]), dob=_bf(do_ref[:, h * dv:(h + 1) * dv]),
                                   dm=dm, qdec=qdec, kdec=kdec, decay=math.exp(lg * c), cos=cos_v, sin=sin_v)
        sc = {ch: _bf(_nt(prep[ch]["qb"], prep[ch]["kb"]) * prep[ch]["dm"]) for ch in combos}
        dsc = {ch: _bf(_nt(prep[ch]["dob"], prep[ch]["v"]) * prep[ch]["dm"]) for ch in combos}
        carried = {}
        for dr, h in combos:
            p = prep[dr, h]
            dv_ref = sets[dr][9]
            dst = dst_ref[dr, h]
            dstb = _bf(dst)
            carried[dr, h] = dstb
            dv_ref[:, h * dv:(h + 1) * dv] = _bf(_tn(sc[dr, h], p["dob"]) + _nt(p["kin"], dstb))
            dst_ref[dr, h] = _tn(p["dob"], p["qin"]) + dst * p["decay"]
        for dr, h in combos:
            p = prep[dr, h]
            st_in, dq_ref, dk_ref = sets[dr][5], sets[dr][7], sets[dr][8]
            dq_r = _nn(dsc[dr, h], p["kb"]) + _nn(p["dob"], st_in[0, h]) * p["qdec"]
            dk_r = _tn(dsc[dr, h], p["qb"]) + _nn(p["v"], carried[dr, h]) * p["kdec"]
            dq_ref[:, h * dk:(h + 1) * dk] = _bf(_unrope(dq_r, p["cos"], p["sin"]))
            dk_ref[:, h * dk:(h + 1) * dk] = _bf(_unrope(dk_r * kscale, p["cos"], p["sin"]))

    def spec(order, width):
        return pl.BlockSpec((c, width), lambda s: (order(s, nc, nt), 0))

    def st_spec(order):
        return pl.BlockSpec((1, RET_HEADS, dv, dk), lambda s: (order(s, nc, nt), 0, 0, 0))

    in_specs = []
    for order in (rfw, rbw):
        in_specs += [spec(order, RET_HEADS * dk), spec(order, RET_HEADS * dk), spec(order, RET_HEADS * dv),
                     spec(order, dk // 2), spec(order, dk // 2), st_spec(order), spec(order, RET_HEADS * dv)]
    out_specs, out_shape = [], []
    for order in (rfw, rbw):
        out_specs += [spec(order, RET_HEADS * dk), spec(order, RET_HEADS * dk), spec(order, RET_HEADS * dv)]
        out_shape += [jax.ShapeDtypeStruct((T, RET_HEADS * dk), ACT), jax.ShapeDtypeStruct((T, RET_HEADS * dk), ACT),
                      jax.ShapeDtypeStruct((T, RET_HEADS * dv), ACT)]
    return _host_call(
        body, ex, lambda: pl.program_id(0) == 0, lambda: pl.program_id(0) == nt - 1,
        name=name, grid=(nt,), in_specs=in_specs, out_specs=out_specs, out_shape=out_shape,
        scratch_shapes=[pltpu.VMEM((2, RET_HEADS, dv, dk), F32)], sem=("arbitrary",),
        args=(q, k, v, cos, sin, s_fw, do, q, k, v, cos, sin, s_bw, do))


def _trig_rows(lc, ang):
    ang = ang.astype(np.float64)
    half = ang.shape[1]
    cos = np.concatenate([np.ones((lc, half)), np.cos(ang)], axis=0).astype(np.float32)
    sin = np.concatenate([np.zeros((lc, half)), np.sin(ang)], axis=0).astype(np.float32)
    return cos, sin


def _attn_rope_tables(lc, l):
    t = np.arange(l)
    row = (t // GRID_W).astype(np.float32)
    colp = (t % GRID_W).astype(np.float32)
    n_freq = HEAD_DIM // 4
    inv = np.float32(10000.0) ** (-np.arange(n_freq, dtype=np.float32) / np.float32(n_freq))
    ang = np.concatenate([row[:, None] * inv, colp[:, None] * inv], axis=-1)
    cos, sin = _trig_rows(lc, ang)
    return jnp.asarray(np.concatenate([cos, cos], axis=1)), jnp.asarray(np.concatenate([-sin, sin], axis=1))


def _ret_rope_tables(lc, l):
    theta = np.float32(1.0) / (np.float32(10000.0) ** np.linspace(0.0, 1.0, RET_DK // 2, dtype=np.float32))
    ang = np.arange(l, dtype=np.float32)[:, None] * theta
    cos, sin = _trig_rows(lc, ang)
    return jnp.asarray(cos), jnp.asarray(sin)


def _heads_major(slab, n_heads):
    t = slab.shape[0]
    return slab.reshape(t, n_heads, HEAD_DIM).transpose(1, 0, 2)


def _slab(hm):
    nh, t, hd = hm.shape
    return hm.transpose(1, 0, 2).reshape(t, nh * hd)


COL_SHARDED = ("ffn_in0", "ffn_in1", "even_in", "even_in_a", "even_in_b", "odd_in")


def _full_weight(name, g):
    if name in COL_SHARDED:
        return g.transpose(1, 0, 2).reshape(g.shape[1], -1)
    return g.reshape(-1, g.shape[2])


def _shard_slots(name, g):
    if name in COL_SHARDED:
        return g.reshape(g.shape[0], N_DEV, -1).transpose(1, 0, 2)
    return g.reshape(N_DEV, -1, g.shape[1])


def _local_step(xs, target, mv, norm_g, w, qk_g, sink, hg_out_g, lbraw, lc, shards=None):
    _, T, dm = _stream(xs)
    l = T - lc
    tm = lc
    blk = ATTN_BLOCK
    d2, d3 = 2 * dm, 3 * dm
    w = dict(w)
    gw, recv = {}, {}

    def ms(layer, a, b):
        return mv[layer, :, :, a:b]

    def gather(names):
        return None if shards is None else _Exchange(GATHER2, [shards[n] for n in names])

    def arrived(names, got):
        for n, g in zip(names, got):
            w[n] = _full_weight(n, g)

    def scatter(names):
        return None if shards is None else _Exchange(SCATTER, [_shard_slots(n, gw[n]) for n in names])

    def scattered(names, got):
        for n, g in zip(names, got):
            recv[n] = g

    g00, g01, g10, g11 = (norm_g[i, j][None, :] for i in (0, 1) for j in (0, 1))

    cos2, sin2 = _attn_rope_tables(lc, l)
    cosp, sinp = jnp.concatenate([cos2, cos2], axis=1), jnp.concatenate([sin2, sin2], axis=1)
    gains5 = jnp.concatenate([jnp.broadcast_to(jnp.tile(qk_g[0], 2), (N_PAIRS - 1, PAIR)), jnp.tile(qk_g[1], 2)[None]])[:, None, :]
    riding = ["even_out"]
    (pa, pb, qt, ks, vs), got = _pre_fwd(xs, g00, ms(0, 0, d2), w["even_in"], ((0, 768), (768, 3328)), tm, "pre0_fwd",
                                         gather(riding), qk=(gains5, cosp, sinp))
    arrived(riding, got)
    sinkb = jnp.broadcast_to(sink.reshape(ATTN_KV, 4, 1, 1), (ATTN_KV, 4, blk, 1)).reshape(ATTN_KV, 4 * blk, 1)
    riding = ["ffn_in0"]
    (a_slab, lse), got = _attn_slab_fwd(qt, ks, vs, sinkb, lc, "attn_fwd", gather(riding))
    arrived(riding, got)
    riding = ["ffn_out0", "odd_out"]
    (hg_of, hg_ob, hg_sf, hg_sb), got = _gla_fwd(pb, lbraw, lc, "hgrn_fwd", gather(riding))
    arrived(riding, got)
    x01, z0, yp0 = _post_fwd(xs, hg_of, hg_ob, pb, 4, hg_out_g, a_slab, w["even_out"], ms(0, d2, d3), HG_D, tm, "post0_fwd")
    riding = ["odd_in"]
    (x02, u0, f0), got = _ffn_fwd(x01, g01, ms(0, d3, 6 * dm), w["ffn_in0"], w["ffn_out0"], tm, "ffn0_fwd", ex=gather(riding))
    arrived(riding, got)

    riding = ["ffn_out1"]
    (rq, rk, rv, rg), got = _pre_fwd(x02, g10, ms(1, 0, d2), w["odd_in"],
                                     ((0, 1024), (1024, 2048), (2048, 4096), (4096, 6144)), tm, "pre1_fwd", gather(riding),
                                     out_dtype=ACT)
    arrived(riding, got)
    rcos, rsin = _ret_rope_tables(lc, l)
    riding = ["ffn_in1"]
    (rt_of, rt_ob, rt_sf, rt_sb), got = _ret_fwd(rq, rk, rv, rcos, rsin, lc, "ret_fwd", gather(riding))
    arrived(riding, got)
    x11, z1, yp1 = _post_fwd(x02, rt_of, rt_ob, rg, 0, None, None, w["odd_out"], ms(1, d2, d3), RET_DV, tm, "post1_fwd")
    (dx, u1, f1, loss), _ = _ffn_fwd(x11, g11, ms(1, d3, 6 * dm), w["ffn_in1"], w["ffn_out1"], tm, "ffn1_fwd", target)

    (dx, h, du, act, df, dms_f1, dg11), _ = _ffn_bwd(x11, dx, u1, f1, g11, ms(1, d3, 6 * dm), w["ffn_in1"], w["ffn_out1"], tm,
                                                     "ffn1_bwd")
    gw["ffn_in1"] = _wgrad(h, du, "wg_ffn_in1")
    gw["ffn_out1"] = _wgrad(act, df, "wg_ffn_out1")
    do1, dgr1, dy1, z1_t, dgate_p1, _ = _post_bwd(dx, z1, yp1, rt_of, rt_ob, rg, 0, None, w["odd_out"], ms(1, d2, d3), 0, RET_DV, tm,
                                                  "post1_bwd")
    gw["odd_out"] = _wgrad(z1_t, dy1, "wg_odd_out")
    riding = ["ffn_in1"]
    (dqf, dkf, dvf, dqb, dkb, dvb), got = _ret_bwd(rq, rk, rv, rcos, rsin, rt_sf, rt_sb, do1, lc, "ret_bwd", scatter(riding))
    scattered(riding, got)
    riding = ["odd_out", "ffn_out1"]
    (dx, h, dp, dms_p1, dg10), got = _pre_bwd(x02, dx, g10, ms(1, 0, d2), w["odd_in"],
                                              [(0, [dqf, dqb]), (1024, [dkf, dkb]), (2048, [dvf, dvb]), (4096, [dgr1])], tm,
                                              "pre1_bwd", ex=scatter(riding))
    scattered(riding, got)
    gw["odd_in"] = _wgrad(h, dp, "wg_odd_in")

    riding = ["odd_in"]
    (dx, h, du, act, df, dms_f0, dg01), got = _ffn_bwd(x01, dx, u0, f0, g01, ms(0, d3, 6 * dm), w["ffn_in0"], w["ffn_out0"], tm,
                                                       "ffn0_bwd", scatter(riding))
    scattered(riding, got)
    gw["ffn_in0"] = _wgrad(h, du, "wg_ffn_in0")
    gw["ffn_out0"] = _wgrad(act, df, "wg_ffn_out0")
    do0, dgr0, da0, dy0, z0_t, dgate_p0, d_hg_gain = _post_bwd(dx, z0, yp0, hg_of, hg_ob, pb, 4, hg_out_g, w["even_out"],
                                                              ms(0, d2, d3), 512, HG_D, tm, "post0_bwd")
    gw["even_out"] = _wgrad(z0_t, dy0, "wg_even_out")
    riding = ["ffn_in0"]
    (hq_f, hz_f, hv_f, hq_b, hz_b, hv_b, dlb), got = _gla_bwd(pb, lbraw, hg_sf, hg_sb, do0, lc, "hgrn_bwd", scatter(riding))
    scattered(riding, got)
    riding = ["even_out", "ffn_out0"]
    (dq_att, dk_att, dv_att, dsink), got = _attn_slab_bwd(qt, ks, vs, sinkb, a_slab, lse, da0, lc, "attn_bwd", scatter(riding))
    scattered(riding, got)
    pieces0 = [(640, [dv_att]), (768, [hq_f, hq_b]), (1280, [hz_f]), (1792, [hz_b]), (2304, [hv_f, hv_b]), (2816, [dgr0])]
    (dx, h, dp, dms_p0, dg00, dgain5), _ = _pre_bwd(xs, dx, g00, ms(0, 0, d2), w["even_in"], pieces0, tm, "pre0_bwd",
                                                    latent_dx=shards is not None,
                                                    qk=(dq_att, dk_att, pa, gains5, cosp, sinp))
    if shards is None:
        gw["even_in"] = _wgrad(h, dp, "wg_even_in")
    else:
        half = dm // 2
        gw["even_in_a"] = _wgrad(h, dp, "wg_even_in_a", rows=(0, half))
        gw["even_in_b"], got = _wgrad(h, dp, "wg_even_in_b", rows=(half, half), ex=scatter(["even_in_a"]))
        scattered(["even_in_a"], got)

    dmv = jnp.stack([jnp.concatenate([dms_p0, dgate_p0, dms_f0], axis=2), jnp.concatenate([dms_p1, dgate_p1, dms_f1], axis=2)])
    small = {
        "dmv": dmv,
        "norm_g": jnp.stack([jnp.stack([dg00[0], dg01[0]]), jnp.stack([dg10[0], dg11[0]])]),
        "qk_g": jnp.stack([jnp.sum(dgain5[:N_PAIRS - 1, 0].reshape(-1, HEAD_DIM), axis=0),
                           jnp.sum(dgain5[N_PAIRS - 1, 0].reshape(-1, HEAD_DIM), axis=0)]),
        "sink": dsink.reshape(ATTN_HEADS),
        "hg_out_g": d_hg_gain[0],
        "lb": dlb[0],
        "loss": loss[0, 0],
    }
    if shards is not None:
        gw = {n: recv.get(n, g) for n, g in gw.items()}
    return loss, dx, gw, small


HBM_SPEC = pl.BlockSpec(memory_space=pltpu.HBM)


def _my_index():
    return 4 * lax.axis_index("x") + 2 * lax.axis_index("y") + lax.axis_index("c")


def _peer(k):
    pos = []
    for axis, bit in (("x", 4), ("y", 2), ("c", 1)):
        a = lax.axis_index(axis)
        pos.append(1 - a if k & bit else a)
    return tuple(pos)


def _peer_index(k):
    px, py, pc = _peer(k)
    return 4 * px + 2 * py + pc


GATHER, SCATTER = "gather", "scatter"
GATHER2 = "gather over ICI once per chip"
SIBLING = 1
OTHER_CHIPS = (2, 4, 6)


class _Exchange:
    def __init__(self, mode, arrays):
        self.mode, self.arrays, self.n = mode, list(arrays), len(arrays)

    def out_shape(self):
        if self.mode in (GATHER, GATHER2):
            return [jax.ShapeDtypeStruct((N_DEV,) + a.shape, a.dtype) for a in self.arrays]
        return [jax.ShapeDtypeStruct(a.shape, a.dtype) for a in self.arrays]

    def specs(self):
        return [HBM_SPEC] * self.n

    def scratch(self):
        return [pltpu.SemaphoreType.DMA((self.n, N_DEV - 1)), pltpu.SemaphoreType.DMA((self.n, N_DEV - 1)),
                pltpu.SemaphoreType.DMA((self.n,))]

    def _copies(self, in_refs, out_refs, send_sems, recv_sems, local_sems, landing):
        me = _my_index()
        local, remote = [], []
        for a, (src, dst) in enumerate(zip(in_refs, out_refs)):
            part = (lambda j, s=src: s) if self.mode == GATHER else (lambda j, s=src: s.at[j])
            local.append(pltpu.make_async_copy(part(me), dst.at[me], local_sems.at[a]))
            for k in range(1, N_DEV):
                pj = _peer_index(k)
                remote.append(pltpu.make_async_remote_copy(
                    src_ref=part(pj), dst_ref=dst.at[pj if landing else me], send_sem=send_sems.at[a, k - 1],
                    recv_sem=recv_sems.at[a, k - 1], device_id=_peer(k), device_id_type=MESH))
        return local, remote

    def _copy2(self, a, src, dst, sems, slot, relation, to):
        send_sems, recv_sems, _ = sems
        return pltpu.make_async_remote_copy(src_ref=src, dst_ref=dst.at[slot], send_sem=send_sems.at[a, relation - 1],
                                            recv_sem=recv_sems.at[a, relation - 1], device_id=_peer(to), device_id_type=MESH)

    def start(self, in_refs, out_refs, sems):
        if self.mode == GATHER2:
            me = _my_index()
            for a, (src, dst) in enumerate(zip(in_refs, out_refs)):
                pltpu.make_async_copy(src, dst.at[me], sems[2].at[a]).start()
                for k in (SIBLING,) + OTHER_CHIPS:
                    self._copy2(a, src, dst, sems, me, k, k).start()
            return
        local, remote = self._copies(in_refs, out_refs, *sems, landing=False)
        for cp in local + remote:
            cp.start()

    def forward(self, in_refs, out_refs, sems):
        for a, (src, dst) in enumerate(zip(in_refs, out_refs)):
            for r in OTHER_CHIPS:
                pj = _peer_index(r)
                self._copy2(a, src, dst, sems, pj, r, r).wait_recv()
                self._copy2(a, dst.at[pj], dst, sems, pj, r ^ SIBLING, SIBLING).start()

    def wait(self, in_refs, out_refs, sems):
        if self.mode == GATHER2:
            me = _my_index()
            for a, (src, dst) in enumerate(zip(in_refs, out_refs)):
                for k in (SIBLING,) + OTHER_CHIPS:
                    self._copy2(a, src, dst, sems, me, k, k).wait_send()
                self._copy2(a, src, dst, sems, _peer_index(SIBLING), SIBLING, SIBLING).wait_recv()
                for r in OTHER_CHIPS:
                    passed = self._copy2(a, src, dst, sems, _peer_index(r ^ SIBLING), r ^ SIBLING, SIBLING)
                    passed.wait_send()
                    passed.wait_recv()
                pltpu.make_async_copy(src, dst.at[me], sems[2].at[a]).wait()
            return
        local, remote = self._copies(in_refs, out_refs, *sems, landing=True)
        for cp in remote:
            cp.wait_send()
            cp.wait_recv()
        for cp in local:
            cp.wait()

    def ride(self, refs, n_in, n_out, first, mid, last):
        refs = list(refs)
        n = self.n
        x_in = refs[n_in:n_in + n]
        x_out = refs[n_in + n + n_out:n_in + 2 * n + n_out]
        sems = refs[n_in + 2 * n + n_out:n_in + 2 * n + n_out + 3]

        @pl.when(first)
        def _():
            self.start(x_in, x_out, sems)

        if self.mode == GATHER2:
            @pl.when(mid)
            def _():
                self.forward(x_in, x_out, sems)

        @pl.when(last)
        def _():
            self.wait(x_in, x_out, sems)

        return refs[:n_in] + refs[n_in + n:n_in + n + n_out] + refs[n_in + 2 * n + n_out + 3:]

    def call(self, name):
        n = self.n

        def body(*refs):
            ins, outs, sems = refs[:n], refs[n:2 * n], refs[2 * n:]
            self.start(ins, outs, sems)
            if self.mode == GATHER2:
                self.forward(ins, outs, sems)
            self.wait(ins, outs, sems)

        return pl.pallas_call(body, name=name, in_specs=self.specs(), out_specs=self.specs(), out_shape=self.out_shape(),
                              scratch_shapes=self.scratch())(*self.arrays)


def _all_gather(v, name):
    return _Exchange(GATHER, [v]).call(name)[0]


def _hosted(kernel_body, ex, n_in, n_out, first, last, grid):
    if ex is None:
        return kernel_body

    def body(*refs):
        mid = pl.program_id(0) == (2 * grid[0]) // 3 if len(grid) == 1 else None
        kernel_body(*ex.ride(refs, n_in, n_out, first(), mid, last()))

    return body


def _host_call(kernel_body, ex, first, last, name, grid, in_specs, out_specs, out_shape, scratch_shapes, sem, args):
    n_in, n_out = len(in_specs), len(out_specs)
    if ex is None:
        outs = pl.pallas_call(kernel_body, name=name, grid=grid, in_specs=in_specs, out_specs=out_specs, out_shape=out_shape,
                              scratch_shapes=scratch_shapes, compiler_params=_cp(*sem))(*args)
        return list(outs), []
    outs = pl.pallas_call(
        _hosted(kernel_body, ex, n_in, n_out, first, last, grid), name=name, grid=grid,
        in_specs=list(in_specs) + ex.specs(), out_specs=list(out_specs) + ex.specs(),
        out_shape=list(out_shape) + ex.out_shape(), scratch_shapes=ex.scratch() + list(scratch_shapes),
        compiler_params=_cp(*sem))(*args, *ex.arrays)
    return list(outs[:n_out]), list(outs[n_out:])


def _mod_fwd(call, mod_w, bias, name):
    nl, dm, n = mod_w.shape

    def body(c_ref, w_ref, b_ref, o_ref):
        cv = c_ref[...]
        cond = _bf(cv * _sig(cv))
        for layer in range(nl):
            o_ref[layer] = _nn(cond, _bf(w_ref[layer])) + b_ref[layer]

    return pl.pallas_call(
        body, name=name, out_shape=jax.ShapeDtypeStruct((nl, call.shape[0], n), F32),
        compiler_params=pltpu.CompilerParams(vmem_limit_bytes=VMEM_LIMIT),
    )(call, mod_w, bias)


def _mod_bwd(call, dm_all, mod_w, name):
    nl, dm, n = mod_w.shape

    def body(c_ref, d_ref, w_ref, gw_ref, dc_ref):
        cv = c_ref[...]
        cond = _bf(cv * _sig(cv))
        dc = jnp.zeros(cv.shape, F32)
        for layer in range(nl):
            db = _bf(d_ref[layer])
            gw_ref[layer] = _tn(cond, db)
            dc = dc + _nt(db, _bf(w_ref[layer]))
        dc_ref[...] = dc

    return pl.pallas_call(
        body, name=name,
        out_shape=[jax.ShapeDtypeStruct(mod_w.shape, F32), jax.ShapeDtypeStruct(call.shape, F32)],
        compiler_params=pltpu.CompilerParams(vmem_limit_bytes=VMEM_LIMIT),
    )(call, dm_all, mod_w)


def _sum_parts(g, name):
    def body(g_ref, o_ref):
        acc = g_ref[0]
        for j in range(1, g.shape[0]):
            acc = acc + g_ref[j]
        o_ref[...] = acc

    return pl.pallas_call(body, name=name, out_shape=jax.ShapeDtypeStruct(g.shape[1:], g.dtype))(g)


def _small_finish(dcond_g, c_ctx, dlb, lbraw, dm_ctx, dm_lat, name):
    def body(dc_ref, c_ref, dlb_ref, lb_ref, mc_ref, ml_ref, gc_ref, glb_ref, gb_ref):
        acc = dc_ref[0, 0:1, :]
        for j in range(1, N_DEV):
            acc = acc + dc_ref[j, 0:1, :]
        cv = c_ref[...]
        s = _sig(cv)
        gc_ref[...] = acc * (s * (1.0 + cv * (1.0 - s)))
        lb = _lower_bound(lb_ref)
        d0 = dlb_ref[...] * lb * (1.0 - lb)
        glb_ref[0:1, :] = d0
        glb_ref[1:2, :] = -d0
        gb_ref[...] = mc_ref[...] + ml_ref[...]

    return pl.pallas_call(
        body, name=name,
        out_shape=[jax.ShapeDtypeStruct(c_ctx.shape, F32), jax.ShapeDtypeStruct(lbraw.shape, F32),
                   jax.ShapeDtypeStruct(dm_ctx.shape, F32)],
    )(dcond_g, c_ctx, dlb, lbraw, dm_ctx, dm_lat)


def _row_tile(r, cap, mult):
    best = r
    for t in range(mult, min(r, cap) + 1, mult):
        if r % t == 0:
            best = t
    return best


def _adam(g_list, w, m, v, name, ex=None):
    nl, r, cdim = w.shape
    p = g_list[0].shape[0]
    tr = _row_tile(r, 128, 16)
    ni = r // tr

    def body(*refs):
        g_refs = refs[:nl]
        w_ref, m_ref, v_ref, go_ref, d_ref, mo_ref, vo_ref = refs[nl:]
        layer = pl.program_id(0)

        def total(g_ref):
            acc = g_ref[0].astype(F32)
            for j in range(1, p):
                acc = acc + g_ref[j].astype(F32)
            return acc

        g = total(g_refs[0])
        for k in range(1, nl):
            g = jnp.where(layer == k, total(g_refs[k]), g)
        m2 = ADAM_B1 * m_ref[0] + (1.0 - ADAM_B1) * g
        v2 = ADAM_B2 * v_ref[0] + (1.0 - ADAM_B2) * (g * g)
        m_hat = m2 / (1.0 - ADAM_B1 ** ADAM_STEP)
        v_hat = v2 / (1.0 - ADAM_B2 ** ADAM_STEP)
        go_ref[0] = g
        d_ref[0] = -ADAM_LR * (m_hat / (jnp.sqrt(v_hat) + ADAM_EPS) + ADAM_WD * w_ref[0])
        mo_ref[0] = m2
        vo_ref[0] = v2

    def g_spec(k):
        return pl.BlockSpec((p, tr, cdim), lambda la, i: (0, jnp.where(la == k, i, jnp.where(la < k, 0, ni - 1)), 0))

    spec = pl.BlockSpec((1, tr, cdim), lambda la, i: (la, i, 0))
    return _host_call(
        body, ex, lambda: (pl.program_id(0) == 0) & (pl.program_id(1) == 0),
        lambda: (pl.program_id(0) == nl - 1) & (pl.program_id(1) == ni - 1),
        name=name, grid=(nl, ni),
        in_specs=[g_spec(k) for k in range(nl)] + [spec, spec, spec],
        out_specs=[spec] * 4, out_shape=[jax.ShapeDtypeStruct((nl, r, cdim), F32)] * 4,
        scratch_shapes=[], sem=("arbitrary", "arbitrary"), args=(*g_list, w, m, v))


def _f32_as_rows(a, width):
    return lax.bitcast_convert_type(a.reshape(-1), BF16).reshape(-1, width)


def _rows_as_f32(rows):
    return lax.bitcast_convert_type(rows.reshape(rows.shape[:-2] + (-1, 2)), F32)


def _pad_rows(a, mult):
    r = (-a.shape[-2]) % mult
    if r == 0:
        return a
    widths = [(0, 0)] * (a.ndim - 2) + [(0, r), (0, 0)]
    return jnp.pad(a, widths)


def _pack_flat(parts, lane):
    flat = jnp.concatenate([p.reshape(-1).astype(F32) for p in parts])
    n = flat.shape[0]
    rows = -(-n // lane)
    rows += (-rows) % 8
    return jnp.pad(flat, (0, rows * lane - n)).reshape(rows, lane)


def _unpack_flat(packed, shapes):
    flat = packed.reshape(-1)
    out, off = [], 0
    for s in shapes:
        n = math.prod(s)
        out.append(flat[off:off + n].reshape(s))
        off += n
    return out


def kernel(x, c, ctx, c_ctx, mod_w, mod_b, norm_g, ffn_w_in, ffn_w_out, even_w_in, even_w_out, attn_qk_norm_g, attn_sink, hgrn_out_norm_g, hgrn_lb, odd_w_in, odd_w_out, loss_target, m_c_ctx, m_mod_w, m_mod_b, m_norm_g, m_ffn_w_in, m_ffn_w_out, m_even_w_in, m_even_w_out, m_attn_qk_norm_g, m_attn_sink, m_hgrn_out_norm_g, m_hgrn_lb, m_odd_w_in, m_odd_w_out, v_c_ctx, v_mod_w, v_mod_b, v_norm_g, v_ffn_w_in, v_ffn_w_out, v_even_w_in, v_even_w_out, v_attn_qk_norm_g, v_attn_sink, v_hgrn_out_norm_g, v_hgrn_lb, v_odd_w_in, v_odd_w_out):
    me = _my_index()
    lc, dm = ctx.shape[1], x.shape[2]
    nmod = mod_w.shape[2]
    big = (ffn_w_in, ffn_w_out, even_w_in, even_w_out, odd_w_in, odd_w_out)

    extra = _pad_rows(jnp.concatenate([_f32_as_rows(c, dm), _f32_as_rows(norm_g, dm)], axis=0), 16)
    shards = {"ffn_in0": ffn_w_in[0], "ffn_in1": ffn_w_in[1], "ffn_out0": ffn_w_out[0], "ffn_out1": ffn_w_out[1],
              "even_in": even_w_in[0], "even_out": even_w_out[0], "odd_in": odd_w_in[0], "odd_out": odd_w_out[0]}
    shards = {n: a.astype(BF16) for n, a in shards.items()}
    first = _Exchange(GATHER2, [shards["even_in"], extra]).call("gather_first")
    w = {"even_in": _full_weight("even_in", first[0])}
    c_all = _rows_as_f32(first[1][:, 0:2])
    norm_g_all = _rows_as_f32(first[1][:, 2:3]).reshape(N_DEV, 2, 2, -1)
    norm_g_full = norm_g_all.transpose(1, 2, 0, 3).reshape(2, 2, dm)

    call = jnp.concatenate([c_all, c_ctx[None, :], jnp.zeros((16 - N_DEV - 1, dm), F32)], axis=0)
    bias = lax.dynamic_slice_in_dim(mod_b, me * nmod, nmod, axis=1)[:, None, :]
    m_sh = _mod_fwd(call, mod_w, bias, "mod_fwd")
    m_g = _all_gather(m_sh.reshape(-1, nmod), "gather_mod").reshape(N_DEV, 2, 16, nmod)
    m_all = m_g.transpose(1, 2, 0, 3).reshape(2, 16, -1)
    m_lat = lax.dynamic_index_in_dim(m_all, me, axis=1, keepdims=False)
    mv = jnp.stack([m_all[:, N_DEV], m_lat], axis=1)[:, :, None, :]

    _, dxs, gw, small = _local_step((ctx[0], x[0]), loss_target[0], mv, norm_g_full, w, attn_qk_norm_g[0], attn_sink[0],
                                    hgrn_out_norm_g, hgrn_lb, lc, shards)
    grad_x = dxs[None]

    last = _Exchange(SCATTER, [_shard_slots("even_in_b", gw["even_in_b"])])
    big_g = [[gw["ffn_in0"], gw["ffn_in1"]], [gw["ffn_out0"], gw["ffn_out1"]], None, [gw["even_out"]],
             [gw["odd_in"]], [gw["odd_out"]]]
    halves = (2, even_w_in.shape[1] // 2, even_w_in.shape[2])
    big_w = (ffn_w_in, ffn_w_out, even_w_in.reshape(halves), even_w_out, odd_w_in, odd_w_out)
    big_m = (m_ffn_w_in, m_ffn_w_out, m_even_w_in.reshape(halves), m_even_w_out, m_odd_w_in, m_odd_w_out)
    big_v = (v_ffn_w_in, v_ffn_w_out, v_even_w_in.reshape(halves), v_even_w_out, v_odd_w_in, v_odd_w_out)
    big_names = ("ffn_w_in", "ffn_w_out", "even_w_in", "even_w_out", "odd_w_in", "odd_w_out")
    big_out = [None] * 6

    def adam_big(i, ex=None):
        big_out[i], got = _adam(big_g[i], big_w[i], big_m[i], big_v[i], "adam_" + big_names[i], ex)
        return got

    dmv = small["dmv"]
    small_shapes = [(2, 6 * dm), (2, 6 * dm), (2, 2, dm), (2, HEAD_DIM), (ATTN_HEADS,), (HG_D,), (HG_HEADS * HG_D,), (1,)]
    vec = _pack_flat([dmv[:, 0, 0], dmv[:, 1, 0], small["norm_g"], small["qk_g"], small["sink"], small["hg_out_g"],
                      small["lb"], small["loss"]], 128)
    big_g[2] = [gw["even_in_a"], adam_big(0, last)[0]]
    vec_g = adam_big(1, _Exchange(GATHER, [vec]))[0]
    tot = _unpack_flat(_sum_parts(vec_g, "sum_small"), small_shapes)
    dm_ctx_tot, dm_lat_tot, g_norm_full, g_qk, g_sink, g_hg, dlb_tot, loss_tot = tot
    dm_lat_each = vec_g.reshape(N_DEV, -1)[:, 12 * dm:24 * dm].reshape(N_DEV, 2, 6 * dm)
    dm_lat_mine = lax.dynamic_slice_in_dim(dm_lat_each, me * nmod, nmod, axis=2).transpose(1, 0, 2)
    dm_ctx_mine = lax.dynamic_slice_in_dim(dm_ctx_tot, me * nmod, nmod, axis=1)[:, None, :]
    dm_all = jnp.concatenate([dm_lat_mine, dm_ctx_mine, jnp.zeros((2, 16 - N_DEV - 1, nmod), F32)], axis=1)
    g_mod_w, dcond = _mod_bwd(call, dm_all, mod_w, "mod_bwd")
    dcond_g = adam_big(4, _Exchange(GATHER, [dcond[N_DEV:]]))[0]
    g_c_ctx, g_lb, g_mod_b = _small_finish(dcond_g, c_ctx[None, :], dlb_tot[None, :], hgrn_lb, dm_ctx_tot, dm_lat_tot,
                                           "small_finish")
    g_norm = lax.dynamic_slice_in_dim(g_norm_full, me * norm_g.shape[2], norm_g.shape[2], axis=2)
    for i in (3, 5, 2):
        adam_big(i)
    big_out[2] = [o.reshape(even_w_in.shape) for o in big_out[2]]
    big_res = [[big_out[i][k] for i in range(6)] for k in range(4)]

    mod_res, _ = _adam([g_mod_w[0][None], g_mod_w[1][None]], mod_w, m_mod_w, v_mod_w, "adam_mod_w")

    sm_w = (c_ctx, mod_b, norm_g, attn_qk_norm_g, attn_sink, hgrn_out_norm_g, hgrn_lb)
    sm_m = (m_c_ctx, m_mod_b, m_norm_g, m_attn_qk_norm_g, m_attn_sink, m_hgrn_out_norm_g, m_hgrn_lb)
    sm_v = (v_c_ctx, v_mod_b, v_norm_g, v_attn_qk_norm_g, v_attn_sink, v_hgrn_out_norm_g, v_hgrn_lb)
    sm_g = (g_c_ctx, g_mod_b, g_norm, g_qk, g_sink, g_hg, g_lb)
    sm_shapes = [a.shape for a in sm_w]
    sm_out, _ = _adam([_pack_flat(sm_g, 128)[None]], _pack_flat(sm_w, 128)[None], _pack_flat(sm_m, 128)[None],
                      _pack_flat(sm_v, 128)[None], "adam_small")
    sm_res = [_unpack_flat(o, sm_shapes) for o in sm_out]

    def ordered(k):
        s, b = sm_res[k], big_res[k]
        return [s[0], mod_res[k], s[1], s[2], b[0], b[1], b[2], b[3], s[3], s[4], s[5], s[6], b[4], b[5]]

    return (loss_tot[0], grad_x, *ordered(0), *ordered(1), *ordered(2), *ordered(3))
```

```python
import functools
import math

import jax
import jax.numpy as jnp
import numpy as np
from jax import lax
from jax.experimental import pallas as pl
from jax.experimental.pallas import tpu as pltpu

F32 = jnp.float32
BF16 = jnp.bfloat16
EPS = 1e-6
N_DEV = 8
MESH = pl.DeviceIdType.MESH

HEAD_DIM = 64
ATTN_HEADS = 8
ATTN_KV = 2
ATTN_BLOCK = 128
WINDOW = 128
GRID_W = 64
HG_HEADS = 4
HG_D = 128
HG_CHUNK = 64
HG_STEP_CHUNKS = 4
RET_HEADS = 4
RET_DK = 256
RET_DV = 512
RET_CHUNK = 256
NEG = -1e30

ADAM_LR = 0.001
ADAM_B1 = 0.9
ADAM_B2 = 0.999
ADAM_EPS = 1e-08
ADAM_WD = 0.01
ADAM_STEP = 10

VMEM_LIMIT = 60 * 1024 * 1024
MXU_WIDTH = 256


def _hidden_chunks(fh, parts=2):
    step = -(-(fh // parts) // MXU_WIDTH) * MXU_WIDTH
    cuts = list(range(0, fh, step)) + [fh]
    return list(zip(cuts[:-1], cuts[1:]))


def _cp(*sem):
    return pltpu.CompilerParams(dimension_semantics=sem, vmem_limit_bytes=VMEM_LIMIT)


def _nn(a, b):
    return jnp.dot(a, b, preferred_element_type=F32)


def _nt(a, b):
    return lax.dot_general(a, b, (((1,), (1,)), ((), ())), preferred_element_type=F32)


def _tn(a, b):
    return lax.dot_general(a, b, (((0,), (0,)), ((), ())), preferred_element_type=F32)


ACT = BF16


def _bf(a):
    return a.astype(ACT)


def _sig(x):
    return jax.nn.sigmoid(x)


def _split3(x):
    h = x.astype(BF16)
    r = x - h.astype(F32)
    m = r.astype(BF16)
    lo = (r - m.astype(F32)).astype(BF16)
    return h, m, lo


def _nn3(m01, x):
    h, m, lo = _split3(x)
    return _nn(m01, h) + _nn(m01, m) + _nn(m01, lo)


def _nn3r(x, m01):
    h, m, lo = _split3(x)
    return _nn(h, m01) + _nn(m, m01) + _nn(lo, m01)


def _full(shape):
    nd = len(shape)
    return pl.BlockSpec(shape, lambda *a: (0,) * nd, pipeline_mode=pl.Buffered(1))


def _whole(shape):
    nd = len(shape)
    return pl.BlockSpec(shape, lambda *a: (0,) * nd)


def _rows(tm, width):
    return pl.BlockSpec((tm, width), lambda i: (i, 0))


def _cols(height, tm):
    return pl.BlockSpec((height, tm), lambda i: (0, i))


def _ctx_lat(width):
    return pl.BlockSpec((1, 1, width), lambda i: (jnp.minimum(i, 1), 0, 0))


def _acc_ctx_lat(ref, i, val):
    @pl.when(i <= 1)
    def _():
        ref[...] = val.reshape(ref.shape)

    @pl.when(i > 1)
    def _():
        ref[...] += val.reshape(ref.shape)


def _acc_all(ref, i, val):
    @pl.when(i == 0)
    def _():
        ref[...] = val.reshape(ref.shape)

    @pl.when(i > 0)
    def _():
        ref[...] += val.reshape(ref.shape)


def _tile(n, cap):
    best = None
    for t in range(128, min(n, cap) + 1, 128):
        if n % t == 0:
            best = t
    return n if best is None else best


def _norm_mod(xv, g, shift, scale):
    r = lax.rsqrt(jnp.mean(xv * xv, axis=-1, keepdims=True) + EPS)
    xhat = xv * r
    n = xhat * g
    return r, xhat, n, n * (1.0 + scale) + shift


def _norm_mod_bwd(dh, r, xhat, n, g, scale):
    dshift = jnp.sum(dh, axis=0, keepdims=True)
    dscale = jnp.sum(dh * n, axis=0, keepdims=True)
    dn = dh * (1.0 + scale)
    dg = jnp.sum(dn * xhat, axis=0, keepdims=True)
    dxh = dn * g
    dx = r * (dxh - xhat * jnp.mean(dxh * xhat, axis=-1, keepdims=True))
    return dx, dshift, dscale, dg


def _stream(x):
    if isinstance(x, tuple):
        return list(x), x[0].shape[0] + x[1].shape[0], x[0].shape[1]
    return [x], x.shape[0], x.shape[1]


def _stream_specs(x, tm, dm):
    if isinstance(x, tuple):
        return [pl.BlockSpec((tm, dm), lambda i: (0, 0)), pl.BlockSpec((tm, dm), lambda i: (jnp.maximum(i - 1, 0), 0))]
    return [_rows(tm, dm)]


def _stream_tile(refs):
    if len(refs) == 2:
        return jnp.where(pl.program_id(0) == 0, refs[0][...], refs[1][...])
    return refs[0][...]


def _pre_fwd(x, gain, ms, w, splits, tm, name, ex=None, out_dtype=F32, qk=None):
    xs, T, dm = _stream(x)
    nx = len(xs)
    nt = T // tm
    nq = 0 if qk is None else 3
    ns = len(splits)

    def body(*refs):
        g_ref, ms_ref, w_ref = refs[nx:nx + 3]
        outs = refs[nx + 3 + nq:]
        ms_v = ms_ref[0]
        h = _norm_mod(_stream_tile(refs[:nx]), g_ref[...], ms_v[:, :dm], ms_v[:, dm:])[3]
        hb = _bf(h)
        for k, ((s, e), o_ref) in enumerate(zip(splits, outs[:ns])):
            part = _nn(hb, w_ref[:, s:e])
            o_ref[...] = part.astype(o_ref.dtype)
            if k == 0 and qk is not None:
                gq_ref, c_ref, s_ref = refs[nx + 3:nx + 6]
                _qk_tile_fwd(part, gq_ref, c_ref[...], s_ref[...], *outs[ns:])

    in_specs = _stream_specs(x, tm, dm) + [_full((1, dm)), _ctx_lat(2 * dm), _full(w.shape)]
    out_specs = [_rows(tm, e - s) for s, e in splits]
    out_shape = [jax.ShapeDtypeStruct((T, e - s), out_dtype) for s, e in splits]
    args = [*xs, gain, ms, w]
    if qk is not None:
        qw = ATTN_HEADS * HEAD_DIM
        in_specs += [_full(qk[0].shape), _rows(tm, PAIR), _rows(tm, PAIR)]
        args += list(qk)
        out_specs += [_rows(tm, qw), _rows(tm, PAIR), _rows(tm, PAIR)]
        out_shape += [jax.ShapeDtypeStruct((T, qw), ACT), jax.ShapeDtypeStruct((T, PAIR), ACT), jax.ShapeDtypeStruct((T, PAIR), ACT)]
    return _host_call(
        body, ex, lambda: pl.program_id(0) == 0, lambda: pl.program_id(0) == nt - 1,
        name=name, grid=(nt,), in_specs=in_specs, out_specs=out_specs, out_shape=out_shape,
        scratch_shapes=[], sem=("arbitrary",), args=tuple(args))


def _pre_bwd(x, dx_in, gain, ms, w, pieces, tm, name, latent_dx=False, ex=None, qk=None):
    xs, T, dm = _stream(x)
    nx = len(xs)
    dx_spec = pl.BlockSpec((tm, dm), lambda i: (jnp.maximum(i - 1, 0), 0)) if latent_dx else _rows(tm, dm)
    dx_rows = T - tm if latent_dx else T
    n_out = w.shape[1]
    flat = [a for _, arrs in pieces for a in arrs]
    nq = 0 if qk is None else 6
    qkw = (ATTN_HEADS + ATTN_KV) * HEAD_DIM

    def body(*refs):
        dxin_ref, g_ref, ms_ref, w_ref = refs[nx:nx + 4]
        rest = refs[nx + 4:]
        p_refs = rest[:len(flat)]
        qk_refs = rest[len(flat):len(flat) + nq]
        dx_ref, h_ref, dp_ref, dms_ref, dg_ref = rest[len(flat) + nq:len(flat) + nq + 5]
        i = pl.program_id(0)
        ms_v = ms_ref[0]
        g = g_ref[...]
        scale = ms_v[:, dm:]
        r, xhat, n, h = _norm_mod(_stream_tile(refs[:nx]), g, ms_v[:, :dm], scale)
        h_ref[...] = _bf(h).T
        dh = jnp.zeros((tm, dm), F32)
        if qk is not None:
            dq_ref, dk_ref, pa_ref, gq_ref, c_ref, s_ref = qk_refs
            dqk, dgs = _qk_tile_bwd(dq_ref, dk_ref, pa_ref, gq_ref, c_ref[...], s_ref[...])
            dgq_ref = rest[len(flat) + nq + 5]
            for p, dgp in enumerate(dgs):
                _acc_all(dgq_ref.at[p], i, dgp)
            vb = _bf(dqk)
            dp_ref[:, :qkw] = vb
            dh = dh + _nt(vb, w_ref[:, :qkw])
        k = 0
        for s, arrs in pieces:
            v = p_refs[k][...].astype(F32)
            for j in range(1, len(arrs)):
                v = v + p_refs[k + j][...].astype(F32)
            k += len(arrs)
            vb = _bf(v)
            wd = vb.shape[1]
            dp_ref[:, s:s + wd] = vb
            dh = dh + _nt(vb, w_ref[:, s:s + wd])
        dx, dshift, dscale, dg = _norm_mod_bwd(dh, r, xhat, n, g, scale)
        dx_ref[...] = dxin_ref[...] + dx
        _acc_ctx_lat(dms_ref, i, jnp.concatenate([dshift, dscale], axis=1))
        _acc_all(dg_ref, i, dg)

    nt = T // tm
    in_specs = (_stream_specs(x, tm, dm) + [_rows(tm, dm), _full((1, dm)), _ctx_lat(2 * dm), _full(w.shape)]
                + [_rows(tm, a.shape[1]) for a in flat])
    out_specs = [dx_spec, _cols(dm, tm), _rows(tm, n_out), _ctx_lat(2 * dm), _whole((1, dm))]
    out_shape = [jax.ShapeDtypeStruct((dx_rows, dm), F32), jax.ShapeDtypeStruct((dm, T), ACT),
                 jax.ShapeDtypeStruct((T, n_out), ACT), jax.ShapeDtypeStruct((2, 1, 2 * dm), F32),
                 jax.ShapeDtypeStruct((1, dm), F32)]
    args = [*xs, dx_in, gain, ms, w, *flat]
    if qk is not None:
        dq, dk, pa, gains, cosp, sinp = qk
        in_specs += [_rows(tm, dq.shape[1]), _rows(tm, PAIR), _rows(tm, qkw), _full(gains.shape), _rows(tm, PAIR), _rows(tm, PAIR)]
        args += [dq, dk, pa, gains, cosp, sinp]
        out_specs.append(_whole(gains.shape))
        out_shape.append(jax.ShapeDtypeStruct(gains.shape, F32))
    return _host_call(
        body, ex, lambda: pl.program_id(0) == 0, lambda: pl.program_id(0) == nt - 1,
        name=name, grid=(nt,), in_specs=in_specs, out_specs=out_specs, out_shape=out_shape,
        scratch_shapes=[], sem=("arbitrary",), args=tuple(args))


def _ffn_fwd(x1, gain, ms, w_in, w_out, tm, name, target=None, ex=None):
    T, dm = x1.shape
    fh = w_out.shape[0]
    head = target is not None

    def body(*refs):
        if head:
            x_ref, g_ref, ms_ref, wi_ref, wo_ref, t_ref, x2_ref, u_ref, f_ref, loss_ref = refs
        else:
            x_ref, g_ref, ms_ref, wi_ref, wo_ref, x2_ref, u_ref, f_ref = refs
        ms_v = ms_ref[0]
        xv = x_ref[...]
        hb = _bf(_norm_mod(xv, g_ref[...], ms_v[:, :dm], ms_v[:, dm:2 * dm])[3])
        f = jnp.zeros((tm, dm), F32)
        for c0, c1 in _hidden_chunks(fh):
            gt = _nn(hb, wi_ref[:, c0:c1])
            up = _nn(hb, wi_ref[:, fh + c0:fh + c1])
            u_ref[:, c0:c1] = _bf(gt)
            u_ref[:, fh + c0:fh + c1] = _bf(up)
            f = f + _nn(_bf(gt * _sig(gt) * up), wo_ref[c0:c1, :])
        f_ref[...] = _bf(f)
        x2 = xv + ms_v[:, 2 * dm:] * f
        if head:
            i = pl.program_id(0)
            e = x2 - t_ref[...]
            x2_ref[...] = jnp.where(i > 0, e * (1.0 / dm), 0.0)
            _acc_all(loss_ref, i, jnp.where(i > 0, jnp.sum(e * e) * (0.5 / dm), 0.0))
        else:
            x2_ref[...] = x2

    ins = [x1, gain, ms, w_in, w_out]
    in_specs = [_rows(tm, dm), _full((1, dm)), _ctx_lat(3 * dm), _full(w_in.shape), _full(w_out.shape)]
    out_specs = [_rows(tm, dm), _rows(tm, 2 * fh), _rows(tm, dm)]
    out_shape = [jax.ShapeDtypeStruct((T, dm), F32), jax.ShapeDtypeStruct((T, 2 * fh), ACT), jax.ShapeDtypeStruct((T, dm), ACT)]
    if head:
        ins.append(target)
        in_specs.append(pl.BlockSpec((tm, dm), lambda i: (jnp.maximum(i - 1, 0), 0)))
        out_specs.append(_whole((1, 1)))
        out_shape.append(jax.ShapeDtypeStruct((1, 1), F32))
    nt = T // tm
    return _host_call(
        body, ex, lambda: pl.program_id(0) == 0, lambda: pl.program_id(0) == nt - 1,
        name=name, grid=(nt,), in_specs=in_specs, out_specs=out_specs, out_shape=out_shape,
        scratch_shapes=[], sem=("arbitrary",), args=tuple(ins))


def _ffn_bwd(x1, dx2, u, f, gain, ms, w_in, w_out, tm, name, ex=None):
    T, dm = x1.shape
    fh = w_out.shape[0]

    def body(x_ref, dx2_ref, u_ref, f_ref, g_ref, ms_ref, wi_ref, wo_ref,
             dx1_ref, h_ref, du_ref, act_ref, df_ref, dms_ref, dg_ref):
        i = pl.program_id(0)
        ms_v = ms_ref[0]
        g = g_ref[...]
        scale = ms_v[:, dm:2 * dm]
        gate = ms_v[:, 2 * dm:]
        r, xhat, n, h = _norm_mod(x_ref[...], g, ms_v[:, :dm], scale)
        h_ref[...] = _bf(h).T
        dx2 = dx2_ref[...]
        dgate = jnp.sum(dx2 * f_ref[...].astype(F32), axis=0, keepdims=True)
        dfb = _bf(dx2 * gate)
        df_ref[...] = dfb
        dh = jnp.zeros((tm, dm), F32)
        for c0, c1 in _hidden_chunks(fh, 1):
            da = _nt(dfb, wo_ref[c0:c1, :])
            gt = u_ref[:, c0:c1].astype(F32)
            up = u_ref[:, fh + c0:fh + c1].astype(F32)
            s = _sig(gt)
            sg = gt * s
            act_ref[c0:c1, :] = _bf(sg * up).T
            dgt = _bf(da * up * (s * (1.0 + gt * (1.0 - s))))
            dup = _bf(da * sg)
            du_ref[:, c0:c1] = dgt
            du_ref[:, fh + c0:fh + c1] = dup
            dh = dh + _nt(dgt, wi_ref[:, c0:c1]) + _nt(dup, wi_ref[:, fh + c0:fh + c1])
        dx, dshift, dscale, dg = _norm_mod_bwd(dh, r, xhat, n, g, scale)
        dx1_ref[...] = dx2 + dx
        _acc_ctx_lat(dms_ref, i, jnp.concatenate([dshift, dscale, dgate], axis=1))
        _acc_all(dg_ref, i, dg)

    nt = T // tm
    return _host_call(
        body, ex, lambda: pl.program_id(0) == 0, lambda: pl.program_id(0) == nt - 1,
        name=name, grid=(nt,),
        in_specs=[_rows(tm, dm), _rows(tm, dm), _rows(tm, 2 * fh), _rows(tm, dm), _full((1, dm)), _ctx_lat(3 * dm),
                  _full(w_in.shape), _full(w_out.shape)],
        out_specs=[_rows(tm, dm), _cols(dm, tm), _rows(tm, 2 * fh), _cols(fh, tm), _rows(tm, dm),
                   _ctx_lat(3 * dm), _whole((1, dm))],
        out_shape=[jax.ShapeDtypeStruct((T, dm), F32), jax.ShapeDtypeStruct((dm, T), ACT),
                   jax.ShapeDtypeStruct((T, 2 * fh), ACT), jax.ShapeDtypeStruct((fh, T), ACT),
                   jax.ShapeDtypeStruct((T, dm), ACT), jax.ShapeDtypeStruct((2, 1, 3 * dm), F32),
                   jax.ShapeDtypeStruct((1, dm), F32)],
        scratch_shapes=[], sem=("arbitrary",), args=(x1, dx2, u, f, gain, ms, w_in, w_out))


def _wgrad(a_t, b, name, rows=None, ex=None):
    T = a_t.shape[1]
    r0, K = (0, a_t.shape[0]) if rows is None else rows
    N = b.shape[1]
    tk, tn, tt = _tile(K, 1408), _tile(N, 1664), _tile(T, 2816)
    nt = T // tt
    assert r0 % tk == 0
    off = r0 // tk
    nk, nn = K // tk, N // tn

    def body(a_ref, b_ref, o_ref, acc_ref):
        t = pl.program_id(2)
        part = _nn(a_ref[...], b_ref[...])

        @pl.when(t == 0)
        def _():
            acc_ref[...] = part

        @pl.when(t > 0)
        def _():
            acc_ref[...] += part

        @pl.when(t == nt - 1)
        def _():
            o_ref[...] = acc_ref[...].astype(o_ref.dtype)

    def at(i, j, t):
        return (pl.program_id(0) == i) & (pl.program_id(1) == j) & (pl.program_id(2) == t)

    outs, got = _host_call(
        body, ex, lambda: at(0, 0, 0), lambda: at(nk - 1, nn - 1, nt - 1),
        name=name, grid=(nk, nn, nt),
        in_specs=[pl.BlockSpec((tk, tt), lambda i, j, t: (i + off, t)), pl.BlockSpec((tt, tn), lambda i, j, t: (t, j))],
        out_specs=[pl.BlockSpec((tk, tn), lambda i, j, t: (i, j))],
        out_shape=[jax.ShapeDtypeStruct((K, N), ACT)],
        scratch_shapes=[pltpu.VMEM((tk, tn), F32)], sem=("arbitrary", "arbitrary", "arbitrary"), args=(a_t, b))
    return outs[0] if ex is None else (outs[0], got)


def _post_fwd(x, o_fw, o_bw, g_src, g_blk, gain, a, w_out, ms, dvh, tm, name):
    xs, T, dm = _stream(x)
    nx = len(xs)
    hv = o_fw.shape[1]
    aw = 0 if a is None else a.shape[1]
    has_gain = gain is not None

    def body(*refs):
        refs = list(refs)
        x_refs = refs[:nx]
        of_ref, ob_ref, g_ref = refs[nx:nx + 3]
        k = nx + 3
        gain_ref = a_ref = None
        if has_gain:
            gain_ref = refs[k]
            k += 1
        if aw:
            a_ref = refs[k]
            k += 1
        w_ref, ms_ref, x1_ref, z_ref, yp_ref = refs[k:k + 5]
        o = of_ref[...].astype(F32) + ob_ref[...].astype(F32)
        gr = g_ref[...].astype(F32)
        if aw:
            z_ref[:, :aw] = _bf(a_ref[...])
        for hd in range(hv // dvh):
            sl = slice(hd * dvh, (hd + 1) * dvh)
            oh = o[:, sl]
            gh = gr[:, sl]
            r = lax.rsqrt(jnp.mean(oh * oh, axis=-1, keepdims=True) + EPS)
            y = oh * r
            if has_gain:
                y = y * gain_ref[...]
            y = y * (gh * _sig(gh))
            z_ref[:, aw + hd * dvh:aw + (hd + 1) * dvh] = _bf(y)
        yp = _nn(z_ref[...], w_ref[...])
        yp_ref[...] = _bf(yp)
        x1_ref[...] = _stream_tile(x_refs) + ms_ref[0] * yp

    ins = xs + [o_fw, o_bw, g_src]
    specs = _stream_specs(x, tm, dm) + [_rows(tm, hv), _rows(tm, hv), pl.BlockSpec((tm, hv), lambda i: (i, g_blk))]
    if has_gain:
        ins.append(gain)
        specs.append(_full(gain.shape))
    if aw:
        ins.append(a)
        specs.append(_rows(tm, aw))
    ins += [w_out, ms]
    specs += [_full(w_out.shape), _ctx_lat(dm)]
    return pl.pallas_call(
        body, name=name, grid=(T // tm,), in_specs=specs,
        out_specs=[_rows(tm, dm), _rows(tm, aw + hv), _rows(tm, dm)],
        out_shape=[jax.ShapeDtypeStruct((T, dm), F32), jax.ShapeDtypeStruct((T, aw + hv), ACT),
                   jax.ShapeDtypeStruct((T, dm), ACT)],
        compiler_params=_cp("arbitrary"),
    )(*ins)


def _post_bwd(dx1, z, yp, o_fw, o_bw, g_src, g_blk, gain, w_out, ms, aw, dvh, tm, name):
    T, dm = dx1.shape
    hv = o_fw.shape[1]
    has_gain = gain is not None

    def body(*refs):
        refs = list(refs)
        dx1_ref, z_ref, yp_ref, of_ref, ob_ref, g_ref = refs[:6]
        k = 6
        gain_ref = None
        if has_gain:
            gain_ref = refs[k]
            k += 1
        w_ref, ms_ref = refs[k:k + 2]
        k += 2
        do_ref, dgr_ref = refs[k:k + 2]
        k += 2
        da_ref = None
        if aw:
            da_ref = refs[k]
            k += 1
        dy_ref, zt_ref, dgate_ref, dgain_ref = refs[k:k + 4]
        i = pl.program_id(0)
        dx1v = dx1_ref[...]
        zt_ref[...] = z_ref[...].T
        _acc_ctx_lat(dgate_ref, i, jnp.sum(dx1v * yp_ref[...].astype(F32), axis=0, keepdims=True))
        dyb = _bf(dx1v * ms_ref[0])
        dy_ref[...] = dyb
        dz = _nt(dyb, w_ref[...])
        if aw:
            da_ref[...] = dz[:, :aw]
        o = of_ref[...].astype(F32) + ob_ref[...].astype(F32)
        gr = g_ref[...].astype(F32)
        dgain = jnp.zeros((1, dvh), F32)
        for hd in range(hv // dvh):
            sl = slice(hd * dvh, (hd + 1) * dvh)
            oh = o[:, sl]
            gh = gr[:, sl]
            dyh = dz[:, aw + hd * dvh:aw + (hd + 1) * dvh]
            r = lax.rsqrt(jnp.mean(oh * oh, axis=-1, keepdims=True) + EPS)
            n = oh * r
            s = _sig(gh)
            sl_g = gh * s
            gn = gain_ref[...] if has_gain else 1.0
            dgr_ref[:, sl] = _bf(dyh * n * gn * (s * (1.0 + gh * (1.0 - s))))
            dn = dyh * gn * sl_g
            dgain = dgain + jnp.sum(dyh * n * sl_g, axis=0, keepdims=True)
            do_ref[:, sl] = _bf(r * (dn - n * jnp.mean(dn * n, axis=-1, keepdims=True)))
        _acc_all(dgain_ref, i, dgain)

    ins = [dx1, z, yp, o_fw, o_bw, g_src]
    specs = [_rows(tm, dm), _rows(tm, aw + hv), _rows(tm, dm), _rows(tm, hv), _rows(tm, hv),
             pl.BlockSpec((tm, hv), lambda i: (i, g_blk))]
    if has_gain:
        ins.append(gain)
        specs.append(_full(gain.shape))
    ins += [w_out, ms]
    specs += [_full(w_out.shape), _ctx_lat(dm)]
    out_specs = [_rows(tm, hv), _rows(tm, hv)]
    out_shape = [jax.ShapeDtypeStruct((T, hv), ACT), jax.ShapeDtypeStruct((T, hv), ACT)]
    if aw:
        out_specs.append(_rows(tm, aw))
        out_shape.append(jax.ShapeDtypeStruct((T, aw), F32))
    out_specs += [_rows(tm, dm), _cols(aw + hv, tm), _ctx_lat(dm), _whole((1, dvh))]
    out_shape += [jax.ShapeDtypeStruct((T, dm), ACT), jax.ShapeDtypeStruct((aw + hv, T), ACT),
                  jax.ShapeDtypeStruct((2, 1, dm), F32), jax.ShapeDtypeStruct((1, dvh), F32)]
    return pl.pallas_call(
        body, name=name, grid=(T // tm,), in_specs=specs, out_specs=out_specs, out_shape=out_shape,
        compiler_params=_cp("arbitrary"),
    )(*ins)


PAIR = 2 * HEAD_DIM
N_PAIRS = (ATTN_HEADS + ATTN_KV) // 2


def _lanes():
    return lax.broadcasted_iota(jnp.int32, (1, PAIR), 1)


def _swap32(v):
    first_half = (_lanes() & (HEAD_DIM // 2)) == 0
    return jnp.where(first_half, pltpu.roll(v, PAIR - HEAD_DIM // 2, 1), pltpu.roll(v, HEAD_DIM // 2, 1))


def _head_mean(v):
    r = lax.broadcasted_iota(jnp.int32, (PAIR, PAIR), 0)
    c = lax.broadcasted_iota(jnp.int32, (PAIR, PAIR), 1)
    same = jnp.where((r >= HEAD_DIM) == (c >= HEAD_DIM), 1.0, 0.0).astype(BF16)
    return _nn3r(v, same) * (1.0 / HEAD_DIM)


def _qk_tile_fwd(pa, g_ref, cosv, sinv, q_ref, k_ref, v_ref):
    qw = ATTN_HEADS * HEAD_DIM
    for p in range(N_PAIRS):
        xv = pa[:, p * PAIR:(p + 1) * PAIR]
        n = xv * lax.rsqrt(_head_mean(xv * xv) + EPS) * g_ref[p]
        y = n * cosv + _swap32(n) * sinv
        if p < N_PAIRS - 1:
            q_ref[:, p * PAIR:(p + 1) * PAIR] = _bf(y * HEAD_DIM ** -0.5)
        else:
            k_ref[...] = _bf(y)
    v_ref[...] = _bf(pa[:, qw + PAIR:])


def _qk_tile_bwd(dq_ref, dk_ref, pa_ref, g_ref, cosv, sinv):
    dxs, dgs = [], []
    for p in range(N_PAIRS):
        sl = slice(p * PAIR, (p + 1) * PAIR)
        xv = pa_ref[:, sl]
        r = lax.rsqrt(_head_mean(xv * xv) + EPS)
        xhat = xv * r
        dy = dq_ref[:, sl] * HEAD_DIM ** -0.5 if p < N_PAIRS - 1 else dk_ref[...]
        dn = dy * cosv + _swap32(dy * sinv)
        dgs.append(jnp.sum(dn * xhat, axis=0, keepdims=True))
        dxh = dn * g_ref[p]
        dxs.append(r * (dxh - xhat * _head_mean(dxh * xhat)))
    return jnp.concatenate(dxs, axis=1), dgs


def _attn_window(ref, i, nb):
    blk = ATTN_BLOCK
    starts = [pl.multiple_of(jnp.clip(i + d, 0, nb - 1) * blk, blk) for d in (-1, 0, 1)]
    return starts, jnp.concatenate([ref[pl.ds(s, blk), :] for s in starts], axis=0)


GROUP_HEADS = 2
ATTN_STEP_BLOCKS = 2


def _head_groups(n):
    g = ATTN_HEADS // ATTN_KV
    return [(kv, [kv * g + s + j for j in range(n)]) for kv in range(ATTN_KV) for s in range(0, g, n)]


def _attn_mask(i, lc, T, rows):
    blk = ATTN_BLOCK
    row = lax.broadcasted_iota(jnp.int32, (rows, 1), 0)
    qpos = i * blk + (row & (blk - 1))
    kpos = (i - 1) * blk + lax.broadcasted_iota(jnp.int32, (1, 3 * blk), 1)
    return (qpos >= lc) & (kpos >= lc) & (kpos < T) & (jnp.abs(kpos - qpos) <= WINDOW)


def _to_kv_half(v, head, kv):
    return v if head % 2 == kv else pltpu.roll(v, HEAD_DIM, 1)


def _attn_slab_fwd(qt, ks, vs, sinkb, lc, name, ex=None):
    T = qt.shape[0]
    blk = ATTN_BLOCK
    nb = T // blk
    g = ATTN_HEADS // ATTN_KV

    spb = ATTN_STEP_BLOCKS
    ng = nb // spb

    def one_block(i, rows, q_ref, k_ref, v_ref, sink_ref, o_ref, lse_ref):
        lane = _lanes()
        valid = _attn_mask(i, lc, T, GROUP_HEADS * blk)
        kc_all, vc = k_ref[0:lc, :], v_ref[0:lc, :]
        _, kw_all = _attn_window(k_ref, i, nb)
        _, vw = _attn_window(v_ref, i, nb)
        kc, kw = [], []
        for kv in range(ATTN_KV):
            mine = (lane >= kv * HEAD_DIM) & (lane < (kv + 1) * HEAD_DIM)
            kc.append(jnp.where(mine, kc_all, jnp.zeros_like(kc_all)))
            kw.append(jnp.where(mine, kw_all, jnp.zeros_like(kw_all)))
        groups = _head_groups(GROUP_HEADS)
        qg = [jnp.concatenate([_to_kv_half(q_ref[rows, (h // 2) * PAIR:(h // 2 + 1) * PAIR], h, kv) for h in heads], axis=0)
              for kv, heads in groups]
        sinks = [sink_ref[kv, (heads[0] - kv * g) * blk:(heads[-1] + 1 - kv * g) * blk] for kv, heads in groups]
        s_c = [_nt(q, kc[kv]) for q, (kv, _) in zip(qg, groups)]
        s_w = [jnp.where(valid, _nt(q, kw[kv]), NEG) for q, (kv, _) in zip(qg, groups)]
        m = [jnp.maximum(jnp.maximum(jnp.max(a, axis=-1, keepdims=True), jnp.max(b, axis=-1, keepdims=True)), s)
             for a, b, s in zip(s_c, s_w, sinks)]
        e_c = [jnp.exp(a - mm) for a, mm in zip(s_c, m)]
        e_w = [jnp.exp(b - mm) for b, mm in zip(s_w, m)]
        den = [jnp.exp(s - mm) + jnp.sum(a, axis=-1, keepdims=True) + jnp.sum(b, axis=-1, keepdims=True)
               for s, mm, a, b in zip(sinks, m, e_c, e_w)]
        inv = [1.0 / d for d in den]
        og = [_nn(_bf(a * r), vc) + _nn(_bf(b * r), vw) for a, b, r in zip(e_c, e_w, inv)]
        placed = [None] * ATTN_HEADS
        for (kv, heads), o2, mm, d in zip(groups, og, m, den):
            lse_ref[heads[0]:heads[-1] + 1, rows, :] = (mm + jnp.log(d)).reshape(len(heads), blk, 1)
            for j, h in enumerate(heads):
                placed[h] = _to_kv_half(o2[j * blk:(j + 1) * blk], h, kv)
        for p in range(ATTN_HEADS // 2):
            o_ref[rows, p * PAIR:(p + 1) * PAIR] = jnp.where(lane < HEAD_DIM, placed[2 * p], placed[2 * p + 1])

    def body(*refs):
        for j in range(spb):
            one_block(pl.program_id(0) * spb + j, pl.ds(j * blk, blk), *refs)

    qw = ATTN_HEADS * HEAD_DIM
    return _host_call(
        body, ex, lambda: pl.program_id(0) == 0, lambda: pl.program_id(0) == ng - 1,
        name=name, grid=(ng,),
        in_specs=[_rows(spb * blk, qw), _full((T, PAIR)), _full((T, PAIR)), _full(sinkb.shape)],
        out_specs=[_rows(spb * blk, qw), pl.BlockSpec((ATTN_HEADS, spb * blk, 1), lambda i: (0, i, 0))],
        out_shape=[jax.ShapeDtypeStruct((T, qw), F32), jax.ShapeDtypeStruct((ATTN_HEADS, T, 1), F32)],
        scratch_shapes=[], sem=("arbitrary",), args=(qt, ks, vs, sinkb))


def _attn_slab_bwd(qt, ks, vs, sinkb, o, lse, do, lc, name, ex=None):
    T = qt.shape[0]
    blk = ATTN_BLOCK
    nb = T // blk
    g = ATTN_HEADS // ATTN_KV

    spb = ATTN_STEP_BLOCKS
    ng = nb // spb

    def body(*refs):
        dk_ref, dv_ref, ds_ref = refs[8:11]

        @pl.when(pl.program_id(0) == 0)
        def _():
            dk_ref[...] = jnp.zeros_like(dk_ref)
            dv_ref[...] = jnp.zeros_like(dv_ref)
            ds_ref[...] = jnp.zeros_like(ds_ref)

        for j in range(spb):
            one_block(pl.program_id(0) * spb + j, pl.ds(j * blk, blk), *refs)

    def one_block(i, rows, q_ref, k_ref, v_ref, sink_ref, o_ref, lse_ref, do_ref, dq_ref, dk_ref, dv_ref, ds_ref):
        lane = _lanes()
        valid = _attn_mask(i, lc, T, g * blk)
        kc_all, vc_all = k_ref[0:lc, :], v_ref[0:lc, :]
        starts, kw_all = _attn_window(k_ref, i, nb)
        _, vw_all = _attn_window(v_ref, i, nb)
        dq_pairs = [jnp.zeros((blk, PAIR), F32) for _ in range(ATTN_HEADS // 2)]
        for kv in range(ATTN_KV):
            mine = (lane >= kv * HEAD_DIM) & (lane < (kv + 1) * HEAD_DIM)

            def only(v):
                return jnp.where(mine, v, jnp.zeros_like(v))

            kc, kw, vc, vw = only(kc_all), only(kw_all), only(vc_all), only(vw_all)
            heads = [kv * g + j for j in range(g)]
            qs, dos, deltas = [], [], []
            for h in heads:
                sl = slice((h // 2) * PAIR, (h // 2 + 1) * PAIR)
                dov = do_ref[rows, sl]
                qs.append(_to_kv_half(q_ref[rows, sl], h, kv))
                dos.append(_bf(_to_kv_half(dov, h, kv)))
                own = (lane < HEAD_DIM) if h % 2 == 0 else (lane >= HEAD_DIM)
                deltas.append(jnp.sum(jnp.where(own, dov * o_ref[rows, sl], 0.0), axis=-1, keepdims=True))
            q4, do4, delta = jnp.concatenate(qs, axis=0), jnp.concatenate(dos, axis=0), jnp.concatenate(deltas, axis=0)
            sink = sink_ref[kv]
            lse = lse_ref[kv * g:(kv + 1) * g, rows, :].reshape(g * blk, 1)
            p_c = jnp.exp(_nt(q4, kc) - lse)
            p_w = jnp.exp(jnp.where(valid, _nt(q4, kw), NEG) - lse)
            ds_c = _bf(p_c * (_nt(do4, vc) - delta))
            ds_w = _bf(p_w * (_nt(do4, vw) - delta))
            dsr = -jnp.exp(sink - lse) * delta
            dq4 = _nn(ds_c, kc) + _nn(ds_w, kw)
            for j, h in enumerate(heads):
                ds_ref[h:h + 1, :] += jnp.sum(dsr[j * blk:(j + 1) * blk, :], axis=0, keepdims=True)
                dq_pairs[h // 2] = dq_pairs[h // 2] + _to_kv_half(dq4[j * blk:(j + 1) * blk], h, kv)
            dk_ref[0:lc, :] += only(_tn(ds_c, q4))
            dv_ref[0:lc, :] += only(_tn(_bf(p_c), do4))
            dkw = only(_tn(ds_w, q4))
            dvw = only(_tn(_bf(p_w), do4))
            for b, s in enumerate(starts):
                dk_ref[pl.ds(s, blk), :] += dkw[b * blk:(b + 1) * blk]
                dv_ref[pl.ds(s, blk), :] += dvw[b * blk:(b + 1) * blk]
        for p in range(ATTN_HEADS // 2):
            dq_ref[rows, p * PAIR:(p + 1) * PAIR] = dq_pairs[p]

    qw = ATTN_HEADS * HEAD_DIM
    lspec = pl.BlockSpec((ATTN_HEADS, spb * blk, 1), lambda i: (0, i, 0))
    return _host_call(
        body, ex, lambda: pl.program_id(0) == 0, lambda: pl.program_id(0) == ng - 1,
        name=name, grid=(ng,),
        in_specs=[_rows(spb * blk, qw), _full((T, PAIR)), _full((T, PAIR)), _full(sinkb.shape), _rows(spb * blk, qw), lspec,
                  _rows(spb * blk, qw)],
        out_specs=[_rows(spb * blk, qw), _whole((T, PAIR)), _whole((T, PAIR)), _whole((ATTN_HEADS, 1))],
        out_shape=[jax.ShapeDtypeStruct((T, qw), F32), jax.ShapeDtypeStruct((T, PAIR), F32),
                   jax.ShapeDtypeStruct((T, PAIR), F32), jax.ShapeDtypeStruct((ATTN_HEADS, 1), F32)],
        scratch_shapes=[], sem=("arbitrary",), args=(qt, ks, vs, sinkb, o, lse, do))


def _fw_chunk(s, nc, nt):
    return s


def _bw_chunk(s, nc, nt):
    return jnp.where(s < nc, nc - 1 - s, nt - 1 - (s - nc))


def _tri(c, rev):
    r = lax.broadcasted_iota(jnp.int32, (c, c), 0)
    k = lax.broadcasted_iota(jnp.int32, (c, c), 1)
    return (k >= r) if rev else (k <= r)


def _gla_gates(z, lb, rev):
    c = HG_CHUNK
    sg = _sig(z)
    f = lb + (1.0 - lb) * sg
    cum = _nn3(jnp.where(_tri(c, rev), 1.0, 0.0).astype(BF16), jnp.log(f))
    mid = c - 1 - c // 2 if rev else c // 2
    last = 0 if rev else c - 1
    return sg, f, cum, cum[mid:mid + 1], cum[last:last + 1], last


def _lower_bound(lbraw_ref):
    lr = lbraw_ref[...]
    return _sig(lr[0:1] - lr[1:2])


def _gla_fwd(pb, lbraw, lc, name, ex=None):
    T = pb.shape[0]
    c, hw, d, ns = HG_CHUNK, HG_HEADS * HG_D, HG_D, HG_STEP_CHUNKS
    nt, nc = T // (ns * c), lc // (ns * c)
    orders = (_fw_chunk, _bw_chunk)

    def body(qf, zf, vf, qb, zb, vb, lb_ref, of_ref, ob_ref, sf_ref, sb_ref, st_ref):
        @pl.when(pl.program_id(0) == 0)
        def _():
            st_ref[...] = jnp.zeros_like(st_ref)

        lb = _lower_bound(lb_ref)
        dirs = ((qf, zf, vf, of_ref, sf_ref), (qb, zb, vb, ob_ref, sb_ref))
        combos = [(dr, h, slice(h * d, (h + 1) * d)) for dr in range(2) for h in range(HG_HEADS)]
        for j in range(ns):
            sub = (j, ns - 1 - j)
            rows = [pl.ds(sub[dr] * c, c) for dr in range(2)]
            prep = []
            for dr, (q_ref, z_ref, v_ref, _, _) in enumerate(dirs):
                rev = dr == 1
                qr = q_ref[rows[dr], :]
                q = qr * _sig(qr)
                _, f, cum, ref, last, _ = _gla_gates(z_ref[rows[dr], :], lb, rev)
                k = 1.0 - f
                prep.append(dict(q1=_bf(q * jnp.exp(cum - ref)), k1=_bf(k * jnp.exp(ref - cum)), q2=_bf(q * jnp.exp(cum)),
                                 k2=_bf(k * jnp.exp(last - cum)), el=jnp.exp(last), v=_bf(v_ref[rows[dr], :]),
                                 mask=_tri(c, rev)))
            a = [_bf(jnp.where(prep[dr]["mask"], _nt(prep[dr]["q1"][:, sl], prep[dr]["k1"][:, sl]), 0.0))
                 for dr, _, sl in combos]
            for (dr, h, sl), a_h in zip(combos, a):
                p = prep[dr]
                o_ref, s_ref = dirs[dr][3], dirs[dr][4]
                st = st_ref[dr, h]
                stb = _bf(st)
                s_ref[sub[dr], h] = stb
                o_ref[rows[dr], sl] = _nn(a_h, p["v"][:, sl]) + _nt(p["q2"][:, sl], stb)
                st_ref[dr, h] = st * p["el"][:, sl] + _tn(p["v"][:, sl], p["k2"][:, sl])

    def col(order, blkcol):
        return pl.BlockSpec((ns * c, hw), lambda s: (order(s, nc, nt), blkcol))

    def st_spec(order):
        return pl.BlockSpec((ns, HG_HEADS, d, d), lambda s: (order(s, nc, nt), 0, 0, 0))

    in_specs = []
    for dr, order in enumerate(orders):
        in_specs += [col(order, 0), col(order, 1 + dr), col(order, 3)]
    in_specs.append(_full(lbraw.shape))
    return _host_call(
        body, ex, lambda: pl.program_id(0) == 0, lambda: pl.program_id(0) == nt - 1,
        name=name, grid=(nt,), in_specs=in_specs,
        out_specs=[col(_fw_chunk, 0), col(_bw_chunk, 0), st_spec(_fw_chunk), st_spec(_bw_chunk)],
        out_shape=[jax.ShapeDtypeStruct((T, hw), F32), jax.ShapeDtypeStruct((T, hw), F32),
                   jax.ShapeDtypeStruct((nt * ns, HG_HEADS, d, d), ACT), jax.ShapeDtypeStruct((nt * ns, HG_HEADS, d, d), ACT)],
        scratch_shapes=[pltpu.VMEM((2, HG_HEADS, d, d), F32)], sem=("arbitrary",),
        args=(pb, pb, pb, pb, pb, pb, lbraw))


def _gla_bwd(pb, lbraw, s_fw, s_bw, do, lc, name, ex=None):
    T = pb.shape[0]
    c, hw, d, ns = HG_CHUNK, HG_HEADS * HG_D, HG_D, HG_STEP_CHUNKS
    nt, nc = T // (ns * c), lc // (ns * c)

    def rfw(s, nc_, nt_):
        return _fw_chunk(nt_ - 1 - s, nc_, nt_)

    def rbw(s, nc_, nt_):
        return _bw_chunk(nt_ - 1 - s, nc_, nt_)

    def body(qf, zf, vf, sf, dof, qb, zb, vb, sb, dob_, lb_ref,
             dqf, dzf, dvf, dqb, dzb, dvb, dlb_ref, dst_ref):
        step = pl.program_id(0)

        @pl.when(step == 0)
        def _():
            dst_ref[...] = jnp.zeros_like(dst_ref)

        lb = _lower_bound(lb_ref)
        sets = ((qf, zf, vf, sf, dof, dqf, dzf, dvf), (qb, zb, vb, sb, dob_, dqb, dzb, dvb))
        combos = [(dr, h, slice(h * d, (h + 1) * d)) for dr in range(2) for h in range(HG_HEADS)]
        dlb_tot = jnp.zeros((1, hw), F32)
        for j in range(ns):
            sub = (ns - 1 - j, j)
            rows = [pl.ds(sub[dr] * c, c) for dr in range(2)]
            prep = []
            for dr, (q_ref, z_ref, v_ref, _, do_ref, _, _, _) in enumerate(sets):
                rev = dr == 1
                qr = q_ref[rows[dr], :]
                sq = _sig(qr)
                q = qr * sq
                sg, f, cum, ref, last, last_row = _gla_gates(z_ref[rows[dr], :], lb, rev)
                k = 1.0 - f
                e_qr, e_kr, e_q, e_kl = jnp.exp(cum - ref), jnp.exp(ref - cum), jnp.exp(cum), jnp.exp(last - cum)
                q1, k1, q2, k2 = q * e_qr, k * e_kr, q * e_q, k * e_kl
                prep.append(dict(qr=qr, sq=sq, sg=sg, f=f, e_qr=e_qr, e_kr=e_kr, e_q=e_q, e_kl=e_kl, el=jnp.exp(last),
                                 q1=q1, k1=k1, q2=q2, k2=k2, q1b=_bf(q1), k1b=_bf(k1), q2b=_bf(q2), k2b=_bf(k2),
                                 vb=_bf(v_ref[rows[dr], :]), dob=_bf(do_ref[rows[dr], :]), mask=_tri(c, rev),
                                 last_row=last_row, acc_t=jnp.where(_tri(c, not rev), 1.0, 0.0).astype(BF16)))
            a = [_bf(jnp.where(prep[dr]["mask"], _nt(prep[dr]["q1b"][:, sl], prep[dr]["k1b"][:, sl]), 0.0))
                 for dr, _, sl in combos]
            da = [_bf(jnp.where(prep[dr]["mask"], _nt(prep[dr]["dob"][:, sl], prep[dr]["vb"][:, sl]), 0.0))
                  for dr, _, sl in combos]
            parts = [dict(dq1=[], dk1=[], dq2=[], dk2=[], dls=[]) for _ in range(2)]
            for (dr, h, sl), a_h, da_h in zip(combos, a, da):
                p = prep[dr]
                s_ref, dv_ref = sets[dr][3], sets[dr][7]
                stb = s_ref[sub[dr], h]
                dst = dst_ref[dr, h]
                dstb = _bf(dst)
                dob_h, vb_h = p["dob"][:, sl], p["vb"][:, sl]
                dv_ref[rows[dr], sl] = _bf(_tn(a_h, dob_h) + _nt(p["k2b"][:, sl], dstb))
                parts[dr]["dq1"].append(_nn(da_h, p["k1b"][:, sl]))
                parts[dr]["dk1"].append(_tn(da_h, p["q1b"][:, sl]))
                parts[dr]["dq2"].append(_nn(dob_h, stb))
                parts[dr]["dk2"].append(_nn(vb_h, dstb))
                el_h = p["el"][:, sl]
                dst_ref[dr, h] = _tn(dob_h, p["q2b"][:, sl]) + dst * el_h
                parts[dr]["dls"].append(jnp.sum(dst * stb.astype(F32), axis=0, keepdims=True) * el_h)
            for dr in range(2):
                p = prep[dr]
                dq_ref, dz_ref = sets[dr][5], sets[dr][6]
                dq1, dk1, dq2, dk2, dls = (jnp.concatenate(parts[dr][n], axis=1) for n in ("dq1", "dk1", "dq2", "dk2", "dls"))
                dq = dq1 * p["e_qr"] + dq2 * p["e_q"]
                dk = dk1 * p["e_kr"] + dk2 * p["e_kl"]
                dcum = dq1 * p["q1"] - dk1 * p["k1"] + dq2 * p["q2"] - dk2 * p["k2"]
                dlast = jnp.sum(dk2 * p["k2"], axis=0, keepdims=True) + dls
                rowid = lax.broadcasted_iota(jnp.int32, (c, 1), 0)
                dcum = dcum + jnp.where(rowid == p["last_row"], dlast, 0.0)
                df = _nn3(p["acc_t"], dcum) / p["f"] - dk
                sg = p["sg"]
                dz_ref[rows[dr], :] = _bf(df * (1.0 - lb) * sg * (1.0 - sg))
                dlb_tot = dlb_tot + jnp.sum(df * (1.0 - sg), axis=0, keepdims=True)
                dq_ref[rows[dr], :] = _bf(dq * (p["sq"] * (1.0 + p["qr"] * (1.0 - p["sq"]))))
        _acc_all(dlb_ref, step, dlb_tot)

    def col(order, blkcol):
        return pl.BlockSpec((ns * c, hw), lambda s: (order(s, nc, nt), blkcol))

    def st_spec(order):
        return pl.BlockSpec((ns, HG_HEADS, d, d), lambda s: (order(s, nc, nt), 0, 0, 0))

    in_specs = []
    for dr, order in enumerate((rfw, rbw)):
        in_specs += [col(order, 0), col(order, 1 + dr), col(order, 3), st_spec(order), col(order, 0)]
    in_specs.append(_full(lbraw.shape))
    out_specs = [col(rfw, 0)] * 3 + [col(rbw, 0)] * 3 + [_whole((1, hw))]
    out_shape = [jax.ShapeDtypeStruct((T, hw), ACT)] * 6 + [jax.ShapeDtypeStruct((1, hw), F32)]
    return _host_call(
        body, ex, lambda: pl.program_id(0) == 0, lambda: pl.program_id(0) == nt - 1,
        name=name, grid=(nt,), in_specs=in_specs, out_specs=out_specs, out_shape=out_shape,
        scratch_shapes=[pltpu.VMEM((2, HG_HEADS, d, d), F32)], sem=("arbitrary",),
        args=(pb, pb, pb, s_fw, do, pb, pb, pb, s_bw, do, lbraw))


def _ret_log_gamma(h, rev):
    hh = RET_HEADS - 1 - h if rev else h
    return math.log(1.0 - 2.0 ** (-5.0 - hh))


def _rope(x, cos, sin):
    half = x.shape[1] // 2
    x1, x2 = x[:, :half], x[:, half:]
    return jnp.concatenate([x1 * cos - x2 * sin, x2 * cos + x1 * sin], axis=1)


def _unrope(dy, cos, sin):
    half = dy.shape[1] // 2
    d1, d2 = dy[:, :half], dy[:, half:]
    return jnp.concatenate([d1 * cos + d2 * sin, d2 * cos - d1 * sin], axis=1)


def _ret_decays(lg, rev):
    c = RET_CHUNK
    r = lax.broadcasted_iota(jnp.int32, (c, c), 0)
    k = lax.broadcasted_iota(jnp.int32, (c, c), 1)
    rel = (k - r) if rev else (r - k)
    dm = jnp.where(rel >= 0, jnp.exp(lg * jnp.maximum(rel, 0).astype(F32)), 0.0)
    pos = lax.broadcasted_iota(jnp.int32, (c, 1), 0).astype(F32)
    if rev:
        qdec = jnp.exp(lg * (c - pos))
        kdec = jnp.exp(lg * pos)
    else:
        qdec = jnp.exp(lg * (pos + 1.0))
        kdec = jnp.exp(lg * (c - 1.0 - pos))
    return dm, qdec, kdec


def _ret_fwd(q, k, v, cos, sin, lc, name, ex=None):
    T = q.shape[0]
    c, dk, dv = RET_CHUNK, RET_DK, RET_DV
    nt, nc = T // c, lc // c
    kscale = dk ** -0.5

    def body(qf, kf, vf, cf, sf_, qb, kb, vb, cb, sb_, of_ref, ob_ref, stf_ref, stb_ref, st_ref):
        @pl.when(pl.program_id(0) == 0)
        def _():
            st_ref[...] = jnp.zeros_like(st_ref)

        sets = ((qf, kf, vf, cf, sf_, of_ref, stf_ref), (qb, kb, vb, cb, sb_, ob_ref, stb_ref))
        combos = [(dr, h) for dr in range(2) for h in range(RET_HEADS)]
        prep = {}
        for dr, (q_ref, k_ref, v_ref, c_ref, s_ref, _, _) in enumerate(sets):
            rev = dr == 1
            cos_v, sin_v = c_ref[...], s_ref[...]
            for h in range(RET_HEADS):
                lg = _ret_log_gamma(h, rev)
                dm, qdec, kdec = _ret_decays(lg, rev)
                qh = _rope(q_ref[:, h * dk:(h + 1) * dk].astype(F32), cos_v, sin_v)
                kh = _rope(k_ref[:, h * dk:(h + 1) * dk].astype(F32), cos_v, sin_v) * kscale
                prep[dr, h] = dict(qb=_bf(qh), kb=_bf(kh), qin=_bf(qh * qdec), kin=_bf(kh * kdec),
                                   v=_bf(v_ref[:, h * dv:(h + 1) * dv]), dm=dm, decay=math.exp(lg * c))
        sc = {ch: _bf(_nt(prep[ch]["qb"], prep[ch]["kb"]) * prep[ch]["dm"]) for ch in combos}
        for dr, h in combos:
            p = prep[dr, h]
            o_ref, so_ref = sets[dr][5], sets[dr][6]
            st = st_ref[dr, h]
            stb = _bf(st)
            so_ref[0, h] = stb
            o_ref[:, h * dv:(h + 1) * dv] = _bf(_nn(sc[dr, h], p["v"]) + _nt(p["qin"], stb))
            st_ref[dr, h] = st * p["decay"] + _tn(p["v"], p["kin"])

    def spec(order, width):
        return pl.BlockSpec((c, width), lambda s: (order(s, nc, nt), 0))

    def st_spec(order):
        return pl.BlockSpec((1, RET_HEADS, dv, dk), lambda s: (order(s, nc, nt), 0, 0, 0))

    in_specs = []
    for order in (_fw_chunk, _bw_chunk):
        in_specs += [spec(order, RET_HEADS * dk), spec(order, RET_HEADS * dk), spec(order, RET_HEADS * dv),
                     spec(order, dk // 2), spec(order, dk // 2)]
    return _host_call(
        body, ex, lambda: pl.program_id(0) == 0, lambda: pl.program_id(0) == nt - 1,
        name=name, grid=(nt,), in_specs=in_specs,
        out_specs=[spec(_fw_chunk, RET_HEADS * dv), spec(_bw_chunk, RET_HEADS * dv), st_spec(_fw_chunk), st_spec(_bw_chunk)],
        out_shape=[jax.ShapeDtypeStruct((T, RET_HEADS * dv), ACT), jax.ShapeDtypeStruct((T, RET_HEADS * dv), ACT),
                   jax.ShapeDtypeStruct((nt, RET_HEADS, dv, dk), ACT), jax.ShapeDtypeStruct((nt, RET_HEADS, dv, dk), ACT)],
        scratch_shapes=[pltpu.VMEM((2, RET_HEADS, dv, dk), F32)], sem=("arbitrary",),
        args=(q, k, v, cos, sin, q, k, v, cos, sin))


def _ret_bwd(q, k, v, cos, sin, s_fw, s_bw, do, lc, name, ex=None):
    T = q.shape[0]
    c, dk, dv = RET_CHUNK, RET_DK, RET_DV
    nt, nc = T // c, lc // c
    kscale = dk ** -0.5

    def rfw(s, nc_, nt_):
        return _fw_chunk(nt_ - 1 - s, nc_, nt_)

    def rbw(s, nc_, nt_):
        return _bw_chunk(nt_ - 1 - s, nc_, nt_)

    def body(qf, kf, vf, cf, sf_, stf, dof, qb, kb, vb, cb, sb_, stb_, dob_,
             dqf, dkf, dvf, dqb, dkb, dvb, dst_ref):
        @pl.when(pl.program_id(0) == 0)
        def _():
            dst_ref[...] = jnp.zeros_like(dst_ref)

        sets = ((qf, kf, vf, cf, sf_, stf, dof, dqf, dkf, dvf), (qb, kb, vb, cb, sb_, stb_, dob_, dqb, dkb, dvb))
        combos = [(dr, h) for dr in range(2) for h in range(RET_HEADS)]
        prep = {}
        for dr, (q_ref, k_ref, v_ref, c_ref, s_ref, _, do_ref, _, _, _) in enumerate(sets):
            rev = dr == 1
            cos_v, sin_v = c_ref[...], s_ref[...]
            for h in range(RET_HEADS):
                lg = _ret_log_gamma(h, rev)
                dm, qdec, kdec = _ret_decays(lg, rev)
                qh = _rope(q_ref[:, h * dk:(h + 1) * dk].astype(F32), cos_v, sin_v)
                kh = _rope(k_ref[:, h * dk:(h + 1) * dk].astype(F32), cos_v, sin_v) * kscale
                prep[dr, h] = dict(qb=_bf(qh), kb=_bf(kh), qin=_bf(qh * qdec), kin=_bf(kh * kdec),
                                   v=_bf(v_ref[:, h * dv:(h + 1) * dv]), dob=_bf(do_ref[:, h * dv:(h + 1) * dv]),
                                   dm=dm, qdec=qdec, kdec=kdec, decay=math.exp(lg * c), cos=cos_v, sin=sin_v)
        sc = {ch: _bf(_nt(prep[ch]["qb"], prep[ch]["kb"]) * prep[ch]["dm"]) for ch in combos}
        dsc = {ch: _bf(_nt(prep[ch]["dob"], prep[ch]["v"]) * prep[ch]["dm"]) for ch in combos}
        carried = {}
        for dr, h in combos:
            p = prep[dr, h]
            dv_ref = sets[dr][9]
            dst = dst_ref[dr, h]
            dstb = _bf(dst)
            carried[dr, h] = dstb
            dv_ref[:, h * dv:(h + 1) * dv] = _bf(_tn(sc[dr, h], p["dob"]) + _nt(p["kin"], dstb))
            dst_ref[dr, h] = _tn(p["dob"], p["qin"]) + dst * p["decay"]
        for dr, h in combos:
            p = prep[dr, h]
            st_in, dq_ref, dk_ref = sets[dr][5], sets[dr][7], sets[dr][8]
            dq_r = _nn(dsc[dr, h], p["kb"]) + _nn(p["dob"], st_in[0, h]) * p["qdec"]
            dk_r = _tn(dsc[dr, h], p["qb"]) + _nn(p["v"], carried[dr, h]) * p["kdec"]
            dq_ref[:, h * dk:(h + 1) * dk] = _bf(_unrope(dq_r, p["cos"], p["sin"]))
            dk_ref[:, h * dk:(h + 1) * dk] = _bf(_unrope(dk_r * kscale, p["cos"], p["sin"]))

    def spec(order, width):
        return pl.BlockSpec((c, width), lambda s: (order(s, nc, nt), 0))

    def st_spec(order):
        return pl.BlockSpec((1, RET_HEADS, dv, dk), lambda s: (order(s, nc, nt), 0, 0, 0))

    in_specs = []
    for order in (rfw, rbw):
        in_specs += [spec(order, RET_HEADS * dk), spec(order, RET_HEADS * dk), spec(order, RET_HEADS * dv),
                     spec(order, dk // 2), spec(order, dk // 2), st_spec(order), spec(order, RET_HEADS * dv)]
    out_specs, out_shape = [], []
    for order in (rfw, rbw):
        out_specs += [spec(order, RET_HEADS * dk), spec(order, RET_HEADS * dk), spec(order, RET_HEADS * dv)]
        out_shape += [jax.ShapeDtypeStruct((T, RET_HEADS * dk), ACT), jax.ShapeDtypeStruct((T, RET_HEADS * dk), ACT),
                      jax.ShapeDtypeStruct((T, RET_HEADS * dv), ACT)]
    return _host_call(
        body, ex, lambda: pl.program_id(0) == 0, lambda: pl.program_id(0) == nt - 1,
        name=name, grid=(nt,), in_specs=in_specs, out_specs=out_specs, out_shape=out_shape,
        scratch_shapes=[pltpu.VMEM((2, RET_HEADS, dv, dk), F32)], sem=("arbitrary",),
        args=(q, k, v, cos, sin, s_fw, do, q, k, v, cos, sin, s_bw, do))


def _trig_rows(lc, ang):
    ang = ang.astype(np.float64)
    half = ang.shape[1]
    cos = np.concatenate([np.ones((lc, half)), np.cos(ang)], axis=0).astype(np.float32)
    sin = np.concatenate([np.zeros((lc, half)), np.sin(ang)], axis=0).astype(np.float32)
    return cos, sin


def _attn_rope_tables(lc, l):
    t = np.arange(l)
    row = (t // GRID_W).astype(np.float32)
    colp = (t % GRID_W).astype(np.float32)
    n_freq = HEAD_DIM // 4
    inv = np.float32(10000.0) ** (-np.arange(n_freq, dtype=np.float32) / np.float32(n_freq))
    ang = np.concatenate([row[:, None] * inv, colp[:, None] * inv], axis=-1)
    cos, sin = _trig_rows(lc, ang)
    return jnp.asarray(np.concatenate([cos, cos], axis=1)), jnp.asarray(np.concatenate([-sin, sin], axis=1))


def _ret_rope_tables(lc, l):
    theta = np.float32(1.0) / (np.float32(10000.0) ** np.linspace(0.0, 1.0, RET_DK // 2, dtype=np.float32))
    ang = np.arange(l, dtype=np.float32)[:, None] * theta
    cos, sin = _trig_rows(lc, ang)
    return jnp.asarray(cos), jnp.asarray(sin)


COL_SHARDED = ("ffn_in0", "ffn_in1", "even_in", "odd_in")


def _col_sharded(name):
    return name in COL_SHARDED or name[:-2] in COL_SHARDED


def _full_weight(name, g):
    if name in COL_SHARDED:
        return g.transpose(1, 0, 2).reshape(g.shape[1], -1)
    return g.reshape(-1, g.shape[2])


def _shard_slots(name, g):
    if _col_sharded(name):
        return g.reshape(g.shape[0], N_DEV, -1).transpose(1, 0, 2)
    return g.reshape(N_DEV, -1, g.shape[1])


def _local_step(xs, target, mv, norm_g, w, qk_g, sink, hg_out_g, lbraw, lc, shards=None):
    _, T, dm = _stream(xs)
    l = T - lc
    tm = lc
    blk = ATTN_BLOCK
    d2, d3 = 2 * dm, 3 * dm
    w = dict(w)
    gw, recv = {}, {}

    def ms(layer, a, b):
        return mv[layer, :, :, a:b]

    def gather(names):
        return None if shards is None else _Exchange(GATHER2, [shards[n] for n in names])

    def arrived(names, got):
        for n, g in zip(names, got):
            w[n] = _full_weight(n, g)

    def scatter(names):
        return None if shards is None else _Exchange(SCATTER, [_shard_slots(n, gw[n]) for n in names])

    def scattered(names, got):
        for n, g in zip(names, got):
            recv[n] = g

    def halve(name):
        if shards is not None:
            g = gw.pop(name)
            gw[name + "_a"], gw[name + "_b"] = g[:g.shape[0] // 2], g[g.shape[0] // 2:]

    g00, g01, g10, g11 = (norm_g[i, j][None, :] for i in (0, 1) for j in (0, 1))

    cos2, sin2 = _attn_rope_tables(lc, l)
    cosp, sinp = jnp.concatenate([cos2, cos2], axis=1), jnp.concatenate([sin2, sin2], axis=1)
    gains5 = jnp.concatenate([jnp.broadcast_to(jnp.tile(qk_g[0], 2), (N_PAIRS - 1, PAIR)), jnp.tile(qk_g[1], 2)[None]])[:, None, :]
    riding = ["even_out"]
    (pa, pb, qt, ks, vs), got = _pre_fwd(xs, g00, ms(0, 0, d2), w["even_in"], ((0, 768), (768, 3328)), tm, "pre0_fwd",
                                         gather(riding), qk=(gains5, cosp, sinp))
    arrived(riding, got)
    sinkb = jnp.broadcast_to(sink.reshape(ATTN_KV, 4, 1, 1), (ATTN_KV, 4, blk, 1)).reshape(ATTN_KV, 4 * blk, 1)
    riding = ["ffn_in0"]
    (a_slab, lse), got = _attn_slab_fwd(qt, ks, vs, sinkb, lc, "attn_fwd", gather(riding))
    arrived(riding, got)
    riding = ["ffn_out0", "odd_out"]
    (hg_of, hg_ob, hg_sf, hg_sb), got = _gla_fwd(pb, lbraw, lc, "hgrn_fwd", gather(riding))
    arrived(riding, got)
    x01, z0, yp0 = _post_fwd(xs, hg_of, hg_ob, pb, 4, hg_out_g, a_slab, w["even_out"], ms(0, d2, d3), HG_D, tm, "post0_fwd")
    riding = ["odd_in"]
    (x02, u0, f0), got = _ffn_fwd(x01, g01, ms(0, d3, 6 * dm), w["ffn_in0"], w["ffn_out0"], tm, "ffn0_fwd", ex=gather(riding))
    arrived(riding, got)

    riding = ["ffn_out1"]
    (rq, rk, rv, rg), got = _pre_fwd(x02, g10, ms(1, 0, d2), w["odd_in"],
                                     ((0, 1024), (1024, 2048), (2048, 4096), (4096, 6144)), tm, "pre1_fwd", gather(riding),
                                     out_dtype=ACT)
    arrived(riding, got)
    rcos, rsin = _ret_rope_tables(lc, l)
    riding = ["ffn_in1"]
    (rt_of, rt_ob, rt_sf, rt_sb), got = _ret_fwd(rq, rk, rv, rcos, rsin, lc, "ret_fwd", gather(riding))
    arrived(riding, got)
    x11, z1, yp1 = _post_fwd(x02, rt_of, rt_ob, rg, 0, None, None, w["odd_out"], ms(1, d2, d3), RET_DV, tm, "post1_fwd")
    (dx, u1, f1, loss), _ = _ffn_fwd(x11, g11, ms(1, d3, 6 * dm), w["ffn_in1"], w["ffn_out1"], tm, "ffn1_fwd", target)

    (dx, h, du, act, df, dms_f1, dg11), _ = _ffn_bwd(x11, dx, u1, f1, g11, ms(1, d3, 6 * dm), w["ffn_in1"], w["ffn_out1"], tm,
                                                     "ffn1_bwd")
    gw["ffn_in1"] = _wgrad(h, du, "wg_ffn_in1")
    halve("ffn_in1")
    gw["ffn_out1"] = _wgrad(act, df, "wg_ffn_out1")
    do1, dgr1, dy1, z1_t, dgate_p1, _ = _post_bwd(dx, z1, yp1, rt_of, rt_ob, rg, 0, None, w["odd_out"], ms(1, d2, d3), 0, RET_DV, tm,
                                                  "post1_bwd")
    gw["odd_out"] = _wgrad(z1_t, dy1, "wg_odd_out")
    riding = ["ffn_in1_a", "ffn_out1"]
    (dqf, dkf, dvf, dqb, dkb, dvb), got = _ret_bwd(rq, rk, rv, rcos, rsin, rt_sf, rt_sb, do1, lc, "ret_bwd", scatter(riding))
    scattered(riding, got)
    riding = ["ffn_in1_b", "odd_out"]
    (dx, h, dp, dms_p1, dg10), got = _pre_bwd(x02, dx, g10, ms(1, 0, d2), w["odd_in"],
                                              [(0, [dqf, dqb]), (1024, [dkf, dkb]), (2048, [dvf, dvb]), (4096, [dgr1])], tm,
                                              "pre1_bwd", ex=scatter(riding))
    scattered(riding, got)
    gw["odd_in"] = _wgrad(h, dp, "wg_odd_in")

    riding = ["odd_in"]
    (dx, h, du, act, df, dms_f0, dg01), got = _ffn_bwd(x01, dx, u0, f0, g01, ms(0, d3, 6 * dm), w["ffn_in0"], w["ffn_out0"], tm,
                                                       "ffn0_bwd", scatter(riding))
    scattered(riding, got)
    gw["ffn_in0"] = _wgrad(h, du, "wg_ffn_in0")
    halve("ffn_in0")
    gw["ffn_out0"] = _wgrad(act, df, "wg_ffn_out0")
    do0, dgr0, da0, dy0, z0_t, dgate_p0, d_hg_gain = _post_bwd(dx, z0, yp0, hg_of, hg_ob, pb, 4, hg_out_g, w["even_out"],
                                                              ms(0, d2, d3), 512, HG_D, tm, "post0_bwd")
    gw["even_out"] = _wgrad(z0_t, dy0, "wg_even_out")
    riding = ["ffn_in0_a", "even_out"]
    (hq_f, hz_f, hv_f, hq_b, hz_b, hv_b, dlb), got = _gla_bwd(pb, lbraw, hg_sf, hg_sb, do0, lc, "hgrn_bwd", scatter(riding))
    scattered(riding, got)
    riding = ["ffn_in0_b"]
    (dq_att, dk_att, dv_att, dsink), got = _attn_slab_bwd(qt, ks, vs, sinkb, a_slab, lse, da0, lc, "attn_bwd", scatter(riding))
    scattered(riding, got)
    pieces0 = [(640, [dv_att]), (768, [hq_f, hq_b]), (1280, [hz_f]), (1792, [hz_b]), (2304, [hv_f, hv_b]), (2816, [dgr0])]
    riding = ["ffn_out0"]
    (dx, h, dp, dms_p0, dg00, dgain5), got = _pre_bwd(xs, dx, g00, ms(0, 0, d2), w["even_in"], pieces0, tm, "pre0_bwd",
                                                      latent_dx=shards is not None, ex=scatter(riding),
                                                      qk=(dq_att, dk_att, pa, gains5, cosp, sinp))
    scattered(riding, got)
    if shards is None:
        gw["even_in"] = _wgrad(h, dp, "wg_even_in")
    else:
        half = dm // 2
        gw["even_in_a"] = _wgrad(h, dp, "wg_even_in_a", rows=(0, half))
        gw["even_in_b"], got = _wgrad(h, dp, "wg_even_in_b", rows=(half, half), ex=scatter(["even_in_a"]))
        scattered(["even_in_a"], got)

    dmv = jnp.stack([jnp.concatenate([dms_p0, dgate_p0, dms_f0], axis=2), jnp.concatenate([dms_p1, dgate_p1, dms_f1], axis=2)])
    small = {
        "dmv": dmv,
        "norm_g": jnp.stack([jnp.stack([dg00[0], dg01[0]]), jnp.stack([dg10[0], dg11[0]])]),
        "qk_g": jnp.stack([jnp.sum(dgain5[:N_PAIRS - 1, 0].reshape(-1, HEAD_DIM), axis=0),
                           jnp.sum(dgain5[N_PAIRS - 1, 0].reshape(-1, HEAD_DIM), axis=0)]),
        "sink": dsink.reshape(ATTN_HEADS),
        "hg_out_g": d_hg_gain[0],
        "lb": dlb[0],
        "loss": loss[0, 0],
    }
    if shards is not None:
        gw = {n: recv.get(n, g) for n, g in gw.items()}
    return loss, dx, gw, small


HBM_SPEC = pl.BlockSpec(memory_space=pltpu.HBM)


def _my_index():
    return 4 * lax.axis_index("x") + 2 * lax.axis_index("y") + lax.axis_index("c")


def _peer(k):
    pos = []
    for axis, bit in (("x", 4), ("y", 2), ("c", 1)):
        a = lax.axis_index(axis)
        pos.append(1 - a if k & bit else a)
    return tuple(pos)


def _peer_index(k):
    px, py, pc = _peer(k)
    return 4 * px + 2 * py + pc


GATHER, SCATTER = "gather", "scatter"
GATHER2 = "gather over ICI once per chip"
SIBLING = 1
OTHER_CHIPS = (2, 4, 6)


class _Exchange:
    def __init__(self, mode, arrays):
        self.mode, self.arrays, self.n = mode, list(arrays), len(arrays)

    def out_shape(self):
        if self.mode in (GATHER, GATHER2):
            return [jax.ShapeDtypeStruct((N_DEV,) + a.shape, a.dtype) for a in self.arrays]
        return [jax.ShapeDtypeStruct(a.shape, a.dtype) for a in self.arrays]

    def specs(self):
        return [HBM_SPEC] * self.n

    def scratch(self):
        return [pltpu.SemaphoreType.DMA((self.n, N_DEV - 1)), pltpu.SemaphoreType.DMA((self.n, N_DEV - 1)),
                pltpu.SemaphoreType.DMA((self.n,))]

    def _copies(self, in_refs, out_refs, send_sems, recv_sems, local_sems, landing):
        me = _my_index()
        local, remote = [], []
        for a, (src, dst) in enumerate(zip(in_refs, out_refs)):
            part = (lambda j, s=src: s) if self.mode == GATHER else (lambda j, s=src: s.at[j])
            local.append(pltpu.make_async_copy(part(me), dst.at[me], local_sems.at[a]))
            for k in range(1, N_DEV):
                pj = _peer_index(k)
                remote.append(pltpu.make_async_remote_copy(
                    src_ref=part(pj), dst_ref=dst.at[pj if landing else me], send_sem=send_sems.at[a, k - 1],
                    recv_sem=recv_sems.at[a, k - 1], device_id=_peer(k), device_id_type=MESH))
        return local, remote

    def _copy2(self, a, src, dst, sems, slot, relation, to):
        send_sems, recv_sems, _ = sems
        return pltpu.make_async_remote_copy(src_ref=src, dst_ref=dst.at[slot], send_sem=send_sems.at[a, relation - 1],
                                            recv_sem=recv_sems.at[a, relation - 1], device_id=_peer(to), device_id_type=MESH)

    def start(self, in_refs, out_refs, sems):
        if self.mode == GATHER2:
            me = _my_index()
            for a, (src, dst) in enumerate(zip(in_refs, out_refs)):
                pltpu.make_async_copy(src, dst.at[me], sems[2].at[a]).start()
                for k in (SIBLING,) + OTHER_CHIPS:
                    self._copy2(a, src, dst, sems, me, k, k).start()
            return
        local, remote = self._copies(in_refs, out_refs, *sems, landing=False)
        for cp in local + remote:
            cp.start()

    def forward(self, in_refs, out_refs, sems):
        for a, (src, dst) in enumerate(zip(in_refs, out_refs)):
            for r in OTHER_CHIPS:
                pj = _peer_index(r)
                self._copy2(a, src, dst, sems, pj, r, r).wait_recv()
                self._copy2(a, dst.at[pj], dst, sems, pj, r ^ SIBLING, SIBLING).start()

    def wait(self, in_refs, out_refs, sems):
        if self.mode == GATHER2:
            me = _my_index()
            for a, (src, dst) in enumerate(zip(in_refs, out_refs)):
                for k in (SIBLING,) + OTHER_CHIPS:
                    self._copy2(a, src, dst, sems, me, k, k).wait_send()
                self._copy2(a, src, dst, sems, _peer_index(SIBLING), SIBLING, SIBLING).wait_recv()
                for r in OTHER_CHIPS:
                    passed = self._copy2(a, src, dst, sems, _peer_index(r ^ SIBLING), r ^ SIBLING, SIBLING)
                    passed.wait_send()
                    passed.wait_recv()
                pltpu.make_async_copy(src, dst.at[me], sems[2].at[a]).wait()
            return
        local, remote = self._copies(in_refs, out_refs, *sems, landing=True)
        for cp in remote:
            cp.wait_send()
            cp.wait_recv()
        for cp in local:
            cp.wait()

    def ride(self, refs, n_in, n_out, first, mid, last):
        refs = list(refs)
        n = self.n
        x_in = refs[n_in:n_in + n]
        x_out = refs[n_in + n + n_out:n_in + 2 * n + n_out]
        sems = refs[n_in + 2 * n + n_out:n_in + 2 * n + n_out + 3]

        @pl.when(first)
        def _():
            self.start(x_in, x_out, sems)

        if self.mode == GATHER2:
            @pl.when(mid)
            def _():
                self.forward(x_in, x_out, sems)

        @pl.when(last)
        def _():
            self.wait(x_in, x_out, sems)

        return refs[:n_in] + refs[n_in + n:n_in + n + n_out] + refs[n_in + 2 * n + n_out + 3:]

    def call(self, name):
        n = self.n

        def body(*refs):
            ins, outs, sems = refs[:n], refs[n:2 * n], refs[2 * n:]
            self.start(ins, outs, sems)
            if self.mode == GATHER2:
                self.forward(ins, outs, sems)
            self.wait(ins, outs, sems)

        return pl.pallas_call(body, name=name, in_specs=self.specs(), out_specs=self.specs(), out_shape=self.out_shape(),
                              scratch_shapes=self.scratch())(*self.arrays)


def _all_gather(v, name):
    return _Exchange(GATHER, [v]).call(name)[0]


def _hosted(kernel_body, ex, n_in, n_out, first, last, grid):
    if ex is None:
        return kernel_body

    def body(*refs):
        mid = pl.program_id(0) == (2 * grid[0]) // 3 if len(grid) == 1 else None
        kernel_body(*ex.ride(refs, n_in, n_out, first(), mid, last()))

    return body


def _host_call(kernel_body, ex, first, last, name, grid, in_specs, out_specs, out_shape, scratch_shapes, sem, args):
    n_in, n_out = len(in_specs), len(out_specs)
    if ex is None:
        outs = pl.pallas_call(kernel_body, name=name, grid=grid, in_specs=in_specs, out_specs=out_specs, out_shape=out_shape,
                              scratch_shapes=scratch_shapes, compiler_params=_cp(*sem))(*args)
        return list(outs), []
    outs = pl.pallas_call(
        _hosted(kernel_body, ex, n_in, n_out, first, last, grid), name=name, grid=grid,
        in_specs=list(in_specs) + ex.specs(), out_specs=list(out_specs) + ex.specs(),
        out_shape=list(out_shape) + ex.out_shape(), scratch_shapes=ex.scratch() + list(scratch_shapes),
        compiler_params=_cp(*sem))(*args, *ex.arrays)
    return list(outs[:n_out]), list(outs[n_out:])


def _mod_fwd(call, mod_w, bias, name):
    nl, dm, n = mod_w.shape

    def body(c_ref, w_ref, b_ref, o_ref):
        cv = c_ref[...]
        cond = _bf(cv * _sig(cv))
        for layer in range(nl):
            o_ref[layer] = _nn(cond, _bf(w_ref[layer])) + b_ref[layer]

    return pl.pallas_call(
        body, name=name, out_shape=jax.ShapeDtypeStruct((nl, call.shape[0], n), F32),
        compiler_params=pltpu.CompilerParams(vmem_limit_bytes=VMEM_LIMIT),
    )(call, mod_w, bias)


def _mod_bwd(call, dm_all, mod_w, name):
    nl, dm, n = mod_w.shape

    def body(c_ref, d_ref, w_ref, gw_ref, dc_ref):
        cv = c_ref[...]
        cond = _bf(cv * _sig(cv))
        dc = jnp.zeros(cv.shape, F32)
        for layer in range(nl):
            db = _bf(d_ref[layer])
            gw_ref[layer] = _tn(cond, db)
            dc = dc + _nt(db, _bf(w_ref[layer]))
        dc_ref[...] = dc

    return pl.pallas_call(
        body, name=name,
        out_shape=[jax.ShapeDtypeStruct(mod_w.shape, F32), jax.ShapeDtypeStruct(call.shape, F32)],
        compiler_params=pltpu.CompilerParams(vmem_limit_bytes=VMEM_LIMIT),
    )(call, dm_all, mod_w)


def _sum_parts(g, name):
    def body(g_ref, o_ref):
        acc = g_ref[0]
        for j in range(1, g.shape[0]):
            acc = acc + g_ref[j]
        o_ref[...] = acc

    return pl.pallas_call(body, name=name, out_shape=jax.ShapeDtypeStruct(g.shape[1:], g.dtype))(g)


def _small_finish(dcond_g, c_ctx, dlb, lbraw, dm_ctx, dm_lat, name):
    def body(dc_ref, c_ref, dlb_ref, lb_ref, mc_ref, ml_ref, gc_ref, glb_ref, gb_ref):
        acc = dc_ref[0, 0:1, :]
        for j in range(1, N_DEV):
            acc = acc + dc_ref[j, 0:1, :]
        cv = c_ref[...]
        s = _sig(cv)
        gc_ref[...] = acc * (s * (1.0 + cv * (1.0 - s)))
        lb = _lower_bound(lb_ref)
        d0 = dlb_ref[...] * lb * (1.0 - lb)
        glb_ref[0:1, :] = d0
        glb_ref[1:2, :] = -d0
        gb_ref[...] = mc_ref[...] + ml_ref[...]

    return pl.pallas_call(
        body, name=name,
        out_shape=[jax.ShapeDtypeStruct(c_ctx.shape, F32), jax.ShapeDtypeStruct(lbraw.shape, F32),
                   jax.ShapeDtypeStruct(dm_ctx.shape, F32)],
    )(dcond_g, c_ctx, dlb, lbraw, dm_ctx, dm_lat)


def _row_tile(r, cap, mult):
    best = r
    for t in range(mult, min(r, cap) + 1, mult):
        if r % t == 0:
            best = t
    return best


def _adam(g_list, w, m, v, name, ex=None):
    nl, r, cdim = w.shape
    p = g_list[0].shape[0]
    tr = _row_tile(r, 128, 16)
    ni = r // tr

    def body(*refs):
        g_refs = refs[:nl]
        w_ref, m_ref, v_ref, go_ref, d_ref, mo_ref, vo_ref = refs[nl:]
        layer = pl.program_id(0)

        def total(g_ref):
            acc = g_ref[0].astype(F32)
            for j in range(1, p):
                acc = acc + g_ref[j].astype(F32)
            return acc

        g = total(g_refs[0])
        for k in range(1, nl):
            g = jnp.where(layer == k, total(g_refs[k]), g)
        m2 = ADAM_B1 * m_ref[0] + (1.0 - ADAM_B1) * g
        v2 = ADAM_B2 * v_ref[0] + (1.0 - ADAM_B2) * (g * g)
        m_hat = m2 / (1.0 - ADAM_B1 ** ADAM_STEP)
        v_hat = v2 / (1.0 - ADAM_B2 ** ADAM_STEP)
        go_ref[0] = g
        d_ref[0] = -ADAM_LR * (m_hat / (jnp.sqrt(v_hat) + ADAM_EPS) + ADAM_WD * w_ref[0])
        mo_ref[0] = m2
        vo_ref[0] = v2

    def g_spec(k):
        return pl.BlockSpec((p, tr, cdim), lambda la, i: (0, jnp.where(la == k, i, jnp.where(la < k, 0, ni - 1)), 0))

    spec = pl.BlockSpec((1, tr, cdim), lambda la, i: (la, i, 0))
    return _host_call(
        body, ex, lambda: (pl.program_id(0) == 0) & (pl.program_id(1) == 0),
        lambda: (pl.program_id(0) == nl - 1) & (pl.program_id(1) == ni - 1),
        name=name, grid=(nl, ni),
        in_specs=[g_spec(k) for k in range(nl)] + [spec, spec, spec],
        out_specs=[spec] * 4, out_shape=[jax.ShapeDtypeStruct((nl, r, cdim), F32)] * 4,
        scratch_shapes=[], sem=("arbitrary", "arbitrary"), args=(*g_list, w, m, v))


def _f32_as_rows(a, width):
    return lax.bitcast_convert_type(a.reshape(-1), BF16).reshape(-1, width)


def _rows_as_f32(rows):
    return lax.bitcast_convert_type(rows.reshape(rows.shape[:-2] + (-1, 2)), F32)


def _pad_rows(a, mult):
    r = (-a.shape[-2]) % mult
    if r == 0:
        return a
    widths = [(0, 0)] * (a.ndim - 2) + [(0, r), (0, 0)]
    return jnp.pad(a, widths)


def _pack_flat(parts, lane):
    flat = jnp.concatenate([p.reshape(-1).astype(F32) for p in parts])
    n = flat.shape[0]
    rows = -(-n // lane)
    rows += (-rows) % 8
    return jnp.pad(flat, (0, rows * lane - n)).reshape(rows, lane)


def _unpack_flat(packed, shapes):
    flat = packed.reshape(-1)
    out, off = [], 0
    for s in shapes:
        n = math.prod(s)
        out.append(flat[off:off + n].reshape(s))
        off += n
    return out


def kernel(x, c, ctx, c_ctx, mod_w, mod_b, norm_g, ffn_w_in, ffn_w_out, even_w_in, even_w_out, attn_qk_norm_g, attn_sink, hgrn_out_norm_g, hgrn_lb, odd_w_in, odd_w_out, loss_target, m_c_ctx, m_mod_w, m_mod_b, m_norm_g, m_ffn_w_in, m_ffn_w_out, m_even_w_in, m_even_w_out, m_attn_qk_norm_g, m_attn_sink, m_hgrn_out_norm_g, m_hgrn_lb, m_odd_w_in, m_odd_w_out, v_c_ctx, v_mod_w, v_mod_b, v_norm_g, v_ffn_w_in, v_ffn_w_out, v_even_w_in, v_even_w_out, v_attn_qk_norm_g, v_attn_sink, v_hgrn_out_norm_g, v_hgrn_lb, v_odd_w_in, v_odd_w_out):
    me = _my_index()
    lc, dm = ctx.shape[1], x.shape[2]
    nmod = mod_w.shape[2]

    extra = _pad_rows(jnp.concatenate([_f32_as_rows(c, dm), _f32_as_rows(norm_g, dm)], axis=0), 16)
    shards = {"ffn_in0": ffn_w_in[0], "ffn_in1": ffn_w_in[1], "ffn_out0": ffn_w_out[0], "ffn_out1": ffn_w_out[1],
              "even_in": even_w_in[0], "even_out": even_w_out[0], "odd_in": odd_w_in[0], "odd_out": odd_w_out[0]}
    shards = {n: a.astype(BF16) for n, a in shards.items()}
    first = _Exchange(GATHER2, [shards["even_in"], extra]).call("gather_first")
    w = {"even_in": _full_weight("even_in", first[0])}
    c_all = _rows_as_f32(first[1][:, 0:2])
    norm_g_all = _rows_as_f32(first[1][:, 2:3]).reshape(N_DEV, 2, 2, -1)
    norm_g_full = norm_g_all.transpose(1, 2, 0, 3).reshape(2, 2, dm)

    call = jnp.concatenate([c_all, c_ctx[None, :], jnp.zeros((16 - N_DEV - 1, dm), F32)], axis=0)
    bias = lax.dynamic_slice_in_dim(mod_b, me * nmod, nmod, axis=1)[:, None, :]
    m_sh = _mod_fwd(call, mod_w, bias, "mod_fwd")
    m_g = _all_gather(m_sh.reshape(-1, nmod), "gather_mod").reshape(N_DEV, 2, 16, nmod)
    m_all = m_g.transpose(1, 2, 0, 3).reshape(2, 16, -1)
    m_lat = lax.dynamic_index_in_dim(m_all, me, axis=1, keepdims=False)
    mv = jnp.stack([m_all[:, N_DEV], m_lat], axis=1)[:, :, None, :]

    _, dxs, gw, small = _local_step((ctx[0], x[0]), loss_target[0], mv, norm_g_full, w, attn_qk_norm_g[0], attn_sink[0],
                                    hgrn_out_norm_g, hgrn_lb, lc, shards)
    grad_x = dxs[None]

    last = _Exchange(SCATTER, [_shard_slots("even_in_b", gw["even_in_b"])])
    big_g = [[gw["ffn_in0_a"], gw["ffn_in0_b"], gw["ffn_in1_a"], gw["ffn_in1_b"]], [gw["ffn_out0"], gw["ffn_out1"]], None,
             [gw["even_out"]], [gw["odd_in"]], [gw["odd_out"]]]
    halves = (2, even_w_in.shape[1] // 2, even_w_in.shape[2])
    ffn_halves = (4, ffn_w_in.shape[1] // 2, ffn_w_in.shape[2])
    big_w = (ffn_w_in.reshape(ffn_halves), ffn_w_out, even_w_in.reshape(halves), even_w_out, odd_w_in, odd_w_out)
    big_m = (m_ffn_w_in.reshape(ffn_halves), m_ffn_w_out, m_even_w_in.reshape(halves), m_even_w_out, m_odd_w_in, m_odd_w_out)
    big_v = (v_ffn_w_in.reshape(ffn_halves), v_ffn_w_out, v_even_w_in.reshape(halves), v_even_w_out, v_odd_w_in, v_odd_w_out)
    big_names = ("ffn_w_in", "ffn_w_out", "even_w_in", "even_w_out", "odd_w_in", "odd_w_out")
    big_out = [None] * 6

    def adam_big(i, ex=None):
        big_out[i], got = _adam(big_g[i], big_w[i], big_m[i], big_v[i], "adam_" + big_names[i], ex)
        return got

    dmv = small["dmv"]
    small_shapes = [(2, 6 * dm), (2, 6 * dm), (2, 2, dm), (2, HEAD_DIM), (ATTN_HEADS,), (HG_D,), (HG_HEADS * HG_D,), (1,)]
    vec = _pack_flat([dmv[:, 0, 0], dmv[:, 1, 0], small["norm_g"], small["qk_g"], small["sink"], small["hg_out_g"],
                      small["lb"], small["loss"]], 128)
    big_g[2] = [gw["even_in_a"], adam_big(0, last)[0]]
    vec_g = adam_big(1, _Exchange(GATHER, [vec]))[0]
    tot = _unpack_flat(_sum_parts(vec_g, "sum_small"), small_shapes)
    dm_ctx_tot, dm_lat_tot, g_norm_full, g_qk, g_sink, g_hg, dlb_tot, loss_tot = tot
    dm_lat_each = vec_g.reshape(N_DEV, -1)[:, 12 * dm:24 * dm].reshape(N_DEV, 2, 6 * dm)
    dm_lat_mine = lax.dynamic_slice_in_dim(dm_lat_each, me * nmod, nmod, axis=2).transpose(1, 0, 2)
    dm_ctx_mine = lax.dynamic_slice_in_dim(dm_ctx_tot, me * nmod, nmod, axis=1)[:, None, :]
    dm_all = jnp.concatenate([dm_lat_mine, dm_ctx_mine, jnp.zeros((2, 16 - N_DEV - 1, nmod), F32)], axis=1)
    g_mod_w, dcond = _mod_bwd(call, dm_all, mod_w, "mod_bwd")
    dcond_g = adam_big(4, _Exchange(GATHER, [dcond[N_DEV:]]))[0]
    g_c_ctx, g_lb, g_mod_b = _small_finish(dcond_g, c_ctx[None, :], dlb_tot[None, :], hgrn_lb, dm_ctx_tot, dm_lat_tot,
                                           "small_finish")
    g_norm = lax.dynamic_slice_in_dim(g_norm_full, me * norm_g.shape[2], norm_g.shape[2], axis=2)
    for i in (3, 5, 2):
        adam_big(i)
    big_out[0] = [o.reshape(ffn_w_in.shape) for o in big_out[0]]
    big_out[2] = [o.reshape(even_w_in.shape) for o in big_out[2]]
    big_res = [[big_out[i][k] for i in range(6)] for k in range(4)]

    mod_res, _ = _adam([g_mod_w[0][None], g_mod_w[1][None]], mod_w, m_mod_w, v_mod_w, "adam_mod_w")

    sm_w = (c_ctx, mod_b, norm_g, attn_qk_norm_g, attn_sink, hgrn_out_norm_g, hgrn_lb)
    sm_m = (m_c_ctx, m_mod_b, m_norm_g, m_attn_qk_norm_g, m_attn_sink, m_hgrn_out_norm_g, m_hgrn_lb)
    sm_v = (v_c_ctx, v_mod_b, v_norm_g, v_attn_qk_norm_g, v_attn_sink, v_hgrn_out_norm_g, v_hgrn_lb)
    sm_g = (g_c_ctx, g_mod_b, g_norm, g_qk, g_sink, g_hg, g_lb)
    sm_shapes = [a.shape for a in sm_w]
    sm_out, _ = _adam([_pack_flat(sm_g, 128)[None]], _pack_flat(sm_w, 128)[None], _pack_flat(sm_m, 128)[None],
                      _pack_flat(sm_v, 128)[None], "adam_small")
    sm_res = [_unpack_flat(o, sm_shapes) for o in sm_out]

    def ordered(k):
        s, b = sm_res[k], big_res[k]
        return [s[0], mod_res[k], s[1], s[2], b[0], b[1], b[2], b[3], s[3], s[4], s[5], s[6], b[4], b[5]]

    return (loss_tot[0], grad_x, *ordered(0), *ordered(1), *ordered(2), *ordered(3))
```

```python
import functools
import math

import jax
import jax.numpy as jnp
import numpy as np
from jax import lax
from jax.experimental import pallas as pl
from jax.experimental.pallas import tpu as pltpu

F32 = jnp.float32
BF16 = jnp.bfloat16
EPS = 1e-6
N_DEV = 8
MESH = pl.DeviceIdType.MESH

HEAD_DIM = 64
ATTN_HEADS = 8
ATTN_KV = 2
ATTN_BLOCK = 128
WINDOW = 128
GRID_W = 64
HG_HEADS = 4
HG_D = 128
HG_CHUNK = 64
HG_STEP_CHUNKS = 4
RET_HEADS = 4
RET_DK = 256
RET_DV = 512
RET_CHUNK = 256
NEG = -1e30

ADAM_LR = 0.001
ADAM_B1 = 0.9
ADAM_B2 = 0.999
ADAM_EPS = 1e-08
ADAM_WD = 0.01
ADAM_STEP = 10

VMEM_LIMIT = 60 * 1024 * 1024
MXU_WIDTH = 256


def _hidden_chunks(fh, parts=2):
    step = -(-(fh // parts) // MXU_WIDTH) * MXU_WIDTH
    cuts = list(range(0, fh, step)) + [fh]
    return list(zip(cuts[:-1], cuts[1:]))


def _cp(*sem):
    return pltpu.CompilerParams(dimension_semantics=sem, vmem_limit_bytes=VMEM_LIMIT)


def _nn(a, b):
    return jnp.dot(a, b, preferred_element_type=F32)


def _nt(a, b):
    return lax.dot_general(a, b, (((1,), (1,)), ((), ())), preferred_element_type=F32)


def _tn(a, b):
    return lax.dot_general(a, b, (((0,), (0,)), ((), ())), preferred_element_type=F32)


ACT = BF16


def _bf(a):
    return a.astype(ACT)


def _sig(x):
    return jax.nn.sigmoid(x)


def _split3(x):
    h = x.astype(BF16)
    r = x - h.astype(F32)
    m = r.astype(BF16)
    lo = (r - m.astype(F32)).astype(BF16)
    return h, m, lo


def _nn3(m01, x):
    h, m, lo = _split3(x)
    return _nn(m01, h) + _nn(m01, m) + _nn(m01, lo)


def _nn3r(x, m01):
    h, m, lo = _split3(x)
    return _nn(h, m01) + _nn(m, m01) + _nn(lo, m01)


def _full(shape):
    nd = len(shape)
    return pl.BlockSpec(shape, lambda *a: (0,) * nd, pipeline_mode=pl.Buffered(1))


def _whole(shape):
    nd = len(shape)
    return pl.BlockSpec(shape, lambda *a: (0,) * nd)


def _rows(tm, width):
    return pl.BlockSpec((tm, width), lambda i: (i, 0))


def _cols(height, tm):
    return pl.BlockSpec((height, tm), lambda i: (0, i))


def _ctx_lat(width):
    return pl.BlockSpec((1, 1, width), lambda i: (jnp.minimum(i, 1), 0, 0))


def _acc_ctx_lat(ref, i, val):
    @pl.when(i <= 1)
    def _():
        ref[...] = val.reshape(ref.shape)

    @pl.when(i > 1)
    def _():
        ref[...] += val.reshape(ref.shape)


def _acc_all(ref, i, val):
    @pl.when(i == 0)
    def _():
        ref[...] = val.reshape(ref.shape)

    @pl.when(i > 0)
    def _():
        ref[...] += val.reshape(ref.shape)


def _tile(n, cap):
    best = None
    for t in range(128, min(n, cap) + 1, 128):
        if n % t == 0:
            best = t
    return n if best is None else best


def _norm_mod(xv, g, shift, scale):
    r = lax.rsqrt(jnp.mean(xv * xv, axis=-1, keepdims=True) + EPS)
    xhat = xv * r
    n = xhat * g
    return r, xhat, n, n * (1.0 + scale) + shift


def _norm_mod_bwd(dh, r, xhat, n, g, scale):
    dshift = jnp.sum(dh, axis=0, keepdims=True)
    dscale = jnp.sum(dh * n, axis=0, keepdims=True)
    dn = dh * (1.0 + scale)
    dg = jnp.sum(dn * xhat, axis=0, keepdims=True)
    dxh = dn * g
    dx = r * (dxh - xhat * jnp.mean(dxh * xhat, axis=-1, keepdims=True))
    return dx, dshift, dscale, dg


def _stream(x):
    if isinstance(x, tuple):
        return list(x), x[0].shape[0] + x[1].shape[0], x[0].shape[1]
    return [x], x.shape[0], x.shape[1]


def _stream_specs(x, tm, dm):
    if isinstance(x, tuple):
        return [pl.BlockSpec((tm, dm), lambda i: (0, 0)), pl.BlockSpec((tm, dm), lambda i: (jnp.maximum(i - 1, 0), 0))]
    return [_rows(tm, dm)]


def _stream_tile(refs):
    if len(refs) == 2:
        return jnp.where(pl.program_id(0) == 0, refs[0][...], refs[1][...])
    return refs[0][...]


def _pre_fwd(x, gain, ms, w, splits, tm, name, ex=None, out_dtype=F32, qk=None):
    xs, T, dm = _stream(x)
    nx = len(xs)
    nt = T // tm
    nq = 0 if qk is None else 3
    ns = len(splits)

    def body(*refs):
        g_ref, ms_ref, w_ref = refs[nx:nx + 3]
        outs = refs[nx + 3 + nq:]
        ms_v = ms_ref[0]
        h = _norm_mod(_stream_tile(refs[:nx]), g_ref[...], ms_v[:, :dm], ms_v[:, dm:])[3]
        hb = _bf(h)
        for k, ((s, e), o_ref) in enumerate(zip(splits, outs[:ns])):
            part = _nn(hb, w_ref[:, s:e])
            o_ref[...] = part.astype(o_ref.dtype)
            if k == 0 and qk is not None:
                gq_ref, c_ref, s_ref = refs[nx + 3:nx + 6]
                _qk_tile_fwd(part, gq_ref, c_ref[...], s_ref[...], *outs[ns:])

    in_specs = _stream_specs(x, tm, dm) + [_full((1, dm)), _ctx_lat(2 * dm), _full(w.shape)]
    out_specs = [_rows(tm, e - s) for s, e in splits]
    out_shape = [jax.ShapeDtypeStruct((T, e - s), out_dtype) for s, e in splits]
    args = [*xs, gain, ms, w]
    if qk is not None:
        qw = ATTN_HEADS * HEAD_DIM
        in_specs += [_full(qk[0].shape), _rows(tm, PAIR), _rows(tm, PAIR)]
        args += list(qk)
        out_specs += [_rows(tm, qw), _rows(tm, PAIR), _rows(tm, PAIR)]
        out_shape += [jax.ShapeDtypeStruct((T, qw), ACT), jax.ShapeDtypeStruct((T, PAIR), ACT), jax.ShapeDtypeStruct((T, PAIR), ACT)]
    return _host_call(
        body, ex, lambda: pl.program_id(0) == 0, lambda: pl.program_id(0) == nt - 1,
        name=name, grid=(nt,), in_specs=in_specs, out_specs=out_specs, out_shape=out_shape,
        scratch_shapes=[], sem=("arbitrary",), args=tuple(args))


def _pre_bwd(x, dx_in, gain, ms, w, pieces, tm, name, latent_dx=False, ex=None, qk=None):
    xs, T, dm = _stream(x)
    nx = len(xs)
    dx_spec = pl.BlockSpec((tm, dm), lambda i: (jnp.maximum(i - 1, 0), 0)) if latent_dx else _rows(tm, dm)
    dx_rows = T - tm if latent_dx else T
    n_out = w.shape[1]
    flat = [a for _, arrs in pieces for a in arrs]
    nq = 0 if qk is None else 6
    qkw = (ATTN_HEADS + ATTN_KV) * HEAD_DIM

    def body(*refs):
        dxin_ref, g_ref, ms_ref, w_ref = refs[nx:nx + 4]
        rest = refs[nx + 4:]
        p_refs = rest[:len(flat)]
        qk_refs = rest[len(flat):len(flat) + nq]
        dx_ref, h_ref, dp_ref, dms_ref, dg_ref = rest[len(flat) + nq:len(flat) + nq + 5]
        i = pl.program_id(0)
        ms_v = ms_ref[0]
        g = g_ref[...]
        scale = ms_v[:, dm:]
        r, xhat, n, h = _norm_mod(_stream_tile(refs[:nx]), g, ms_v[:, :dm], scale)
        h_ref[...] = _bf(h).T
        dh = jnp.zeros((tm, dm), F32)
        if qk is not None:
            dq_ref, dk_ref, pa_ref, gq_ref, c_ref, s_ref = qk_refs
            dqk, dgs = _qk_tile_bwd(dq_ref, dk_ref, pa_ref, gq_ref, c_ref[...], s_ref[...])
            dgq_ref = rest[len(flat) + nq + 5]
            for p, dgp in enumerate(dgs):
                _acc_all(dgq_ref.at[p], i, dgp)
            vb = _bf(dqk)
            dp_ref[:, :qkw] = vb
            dh = dh + _nt(vb, w_ref[:, :qkw])
        k = 0
        for s, arrs in pieces:
            v = p_refs[k][...].astype(F32)
            for j in range(1, len(arrs)):
                v = v + p_refs[k + j][...].astype(F32)
            k += len(arrs)
            vb = _bf(v)
            wd = vb.shape[1]
            dp_ref[:, s:s + wd] = vb
            dh = dh + _nt(vb, w_ref[:, s:s + wd])
        dx, dshift, dscale, dg = _norm_mod_bwd(dh, r, xhat, n, g, scale)
        dx_ref[...] = dxin_ref[...] + dx
        _acc_ctx_lat(dms_ref, i, jnp.concatenate([dshift, dscale], axis=1))
        _acc_all(dg_ref, i, dg)

    nt = T // tm
    in_specs = (_stream_specs(x, tm, dm) + [_rows(tm, dm), _full((1, dm)), _ctx_lat(2 * dm), _full(w.shape)]
                + [_rows(tm, a.shape[1]) for a in flat])
    out_specs = [dx_spec, _cols(dm, tm), _rows(tm, n_out), _ctx_lat(2 * dm), _whole((1, dm))]
    out_shape = [jax.ShapeDtypeStruct((dx_rows, dm), F32), jax.ShapeDtypeStruct((dm, T), ACT),
                 jax.ShapeDtypeStruct((T, n_out), ACT), jax.ShapeDtypeStruct((2, 1, 2 * dm), F32),
                 jax.ShapeDtypeStruct((1, dm), F32)]
    args = [*xs, dx_in, gain, ms, w, *flat]
    if qk is not None:
        dq, dk, pa, gains, cosp, sinp = qk
        in_specs += [_rows(tm, dq.shape[1]), _rows(tm, PAIR), _rows(tm, qkw), _full(gains.shape), _rows(tm, PAIR), _rows(tm, PAIR)]
        args += [dq, dk, pa, gains, cosp, sinp]
        out_specs.append(_whole(gains.shape))
        out_shape.append(jax.ShapeDtypeStruct(gains.shape, F32))
    return _host_call(
        body, ex, lambda: pl.program_id(0) == 0, lambda: pl.program_id(0) == nt - 1,
        name=name, grid=(nt,), in_specs=in_specs, out_specs=out_specs, out_shape=out_shape,
        scratch_shapes=[], sem=("arbitrary",), args=tuple(args))


def _ffn_fwd(x1, gain, ms, w_in, w_out, tm, name, target=None, ex=None):
    T, dm = x1.shape
    fh = w_out.shape[0]
    head = target is not None

    def body(*refs):
        if head:
            x_ref, g_ref, ms_ref, wi_ref, wo_ref, t_ref, x2_ref, u_ref, f_ref, loss_ref = refs
        else:
            x_ref, g_ref, ms_ref, wi_ref, wo_ref, x2_ref, u_ref, f_ref = refs
        ms_v = ms_ref[0]
        xv = x_ref[...]
        hb = _bf(_norm_mod(xv, g_ref[...], ms_v[:, :dm], ms_v[:, dm:2 * dm])[3])
        f = jnp.zeros((tm, dm), F32)
        for c0, c1 in _hidden_chunks(fh):
            gt = _nn(hb, wi_ref[:, c0:c1])
            up = _nn(hb, wi_ref[:, fh + c0:fh + c1])
            u_ref[:, c0:c1] = _bf(gt)
            u_ref[:, fh + c0:fh + c1] = _bf(up)
            f = f + _nn(_bf(gt * _sig(gt) * up), wo_ref[c0:c1, :])
        f_ref[...] = _bf(f)
        x2 = xv + ms_v[:, 2 * dm:] * f
        if head:
            i = pl.program_id(0)
            e = x2 - t_ref[...]
            x2_ref[...] = jnp.where(i > 0, e * (1.0 / dm), 0.0)
            _acc_all(loss_ref, i, jnp.where(i > 0, jnp.sum(e * e) * (0.5 / dm), 0.0))
        else:
            x2_ref[...] = x2

    ins = [x1, gain, ms, w_in, w_out]
    in_specs = [_rows(tm, dm), _full((1, dm)), _ctx_lat(3 * dm), _full(w_in.shape), _full(w_out.shape)]
    out_specs = [_rows(tm, dm), _rows(tm, 2 * fh), _rows(tm, dm)]
    out_shape = [jax.ShapeDtypeStruct((T, dm), F32), jax.ShapeDtypeStruct((T, 2 * fh), ACT), jax.ShapeDtypeStruct((T, dm), ACT)]
    if head:
        ins.append(target)
        in_specs.append(pl.BlockSpec((tm, dm), lambda i: (jnp.maximum(i - 1, 0), 0)))
        out_specs.append(_whole((1, 1)))
        out_shape.append(jax.ShapeDtypeStruct((1, 1), F32))
    nt = T // tm
    return _host_call(
        body, ex, lambda: pl.program_id(0) == 0, lambda: pl.program_id(0) == nt - 1,
        name=name, grid=(nt,), in_specs=in_specs, out_specs=out_specs, out_shape=out_shape,
        scratch_shapes=[], sem=("arbitrary",), args=tuple(ins))


def _ffn_bwd(x1, dx2, u, f, gain, ms, w_in, w_out, tm, name, ex=None):
    T, dm = x1.shape
    fh = w_out.shape[0]

    def body(x_ref, dx2_ref, u_ref, f_ref, g_ref, ms_ref, wi_ref, wo_ref,
             dx1_ref, h_ref, du_ref, act_ref, df_ref, dms_ref, dg_ref):
        i = pl.program_id(0)
        ms_v = ms_ref[0]
        g = g_ref[...]
        scale = ms_v[:, dm:2 * dm]
        gate = ms_v[:, 2 * dm:]
        r, xhat, n, h = _norm_mod(x_ref[...], g, ms_v[:, :dm], scale)
        h_ref[...] = _bf(h).T
        dx2 = dx2_ref[...]
        dgate = jnp.sum(dx2 * f_ref[...].astype(F32), axis=0, keepdims=True)
        dfb = _bf(dx2 * gate)
        df_ref[...] = dfb
        dh = jnp.zeros((tm, dm), F32)
        for c0, c1 in _hidden_chunks(fh, 1):
            da = _nt(dfb, wo_ref[c0:c1, :])
            gt = u_ref[:, c0:c1].astype(F32)
            up = u_ref[:, fh + c0:fh + c1].astype(F32)
            s = _sig(gt)
            sg = gt * s
            act_ref[c0:c1, :] = _bf(sg * up).T
            dgt = _bf(da * up * (s * (1.0 + gt * (1.0 - s))))
            dup = _bf(da * sg)
            du_ref[:, c0:c1] = dgt
            du_ref[:, fh + c0:fh + c1] = dup
            dh = dh + _nt(dgt, wi_ref[:, c0:c1]) + _nt(dup, wi_ref[:, fh + c0:fh + c1])
        dx, dshift, dscale, dg = _norm_mod_bwd(dh, r, xhat, n, g, scale)
        dx1_ref[...] = dx2 + dx
        _acc_ctx_lat(dms_ref, i, jnp.concatenate([dshift, dscale, dgate], axis=1))
        _acc_all(dg_ref, i, dg)

    nt = T // tm
    return _host_call(
        body, ex, lambda: pl.program_id(0) == 0, lambda: pl.program_id(0) == nt - 1,
        name=name, grid=(nt,),
        in_specs=[_rows(tm, dm), _rows(tm, dm), _rows(tm, 2 * fh), _rows(tm, dm), _full((1, dm)), _ctx_lat(3 * dm),
                  _full(w_in.shape), _full(w_out.shape)],
        out_specs=[_rows(tm, dm), _cols(dm, tm), _rows(tm, 2 * fh), _cols(fh, tm), _rows(tm, dm),
                   _ctx_lat(3 * dm), _whole((1, dm))],
        out_shape=[jax.ShapeDtypeStruct((T, dm), F32), jax.ShapeDtypeStruct((dm, T), ACT),
                   jax.ShapeDtypeStruct((T, 2 * fh), ACT), jax.ShapeDtypeStruct((fh, T), ACT),
                   jax.ShapeDtypeStruct((T, dm), ACT), jax.ShapeDtypeStruct((2, 1, 3 * dm), F32),
                   jax.ShapeDtypeStruct((1, dm), F32)],
        scratch_shapes=[], sem=("arbitrary",), args=(x1, dx2, u, f, gain, ms, w_in, w_out))


def _wgrad(a_t, b, name, rows=None, ex=None):
    T = a_t.shape[1]
    r0, K = (0, a_t.shape[0]) if rows is None else rows
    N = b.shape[1]
    tk, tn, tt = _tile(K, 1408), _tile(N, 1664), _tile(T, 2816)
    nt = T // tt
    assert r0 % tk == 0
    off = r0 // tk
    nk, nn = K // tk, N // tn

    def body(a_ref, b_ref, o_ref, acc_ref):
        t = pl.program_id(2)
        part = _nn(a_ref[...], b_ref[...])

        @pl.when(t == 0)
        def _():
            acc_ref[...] = part

        @pl.when(t > 0)
        def _():
            acc_ref[...] += part

        @pl.when(t == nt - 1)
        def _():
            o_ref[...] = acc_ref[...].astype(o_ref.dtype)

    def at(i, j, t):
        return (pl.program_id(0) == i) & (pl.program_id(1) == j) & (pl.program_id(2) == t)

    outs, got = _host_call(
        body, ex, lambda: at(0, 0, 0), lambda: at(nk - 1, nn - 1, nt - 1),
        name=name, grid=(nk, nn, nt),
        in_specs=[pl.BlockSpec((tk, tt), lambda i, j, t: (i + off, t)), pl.BlockSpec((tt, tn), lambda i, j, t: (t, j))],
        out_specs=[pl.BlockSpec((tk, tn), lambda i, j, t: (i, j))],
        out_shape=[jax.ShapeDtypeStruct((K, N), ACT)],
        scratch_shapes=[pltpu.VMEM((tk, tn), F32)], sem=("arbitrary", "arbitrary", "arbitrary"), args=(a_t, b))
    return outs[0] if ex is None else (outs[0], got)


def _post_fwd(x, o_fw, o_bw, g_src, g_blk, gain, a, w_out, ms, dvh, tm, name):
    xs, T, dm = _stream(x)
    nx = len(xs)
    hv = o_fw.shape[1]
    aw = 0 if a is None else a.shape[1]
    has_gain = gain is not None

    def body(*refs):
        refs = list(refs)
        x_refs = refs[:nx]
        of_ref, ob_ref, g_ref = refs[nx:nx + 3]
        k = nx + 3
        gain_ref = a_ref = None
        if has_gain:
            gain_ref = refs[k]
            k += 1
        if aw:
            a_ref = refs[k]
            k += 1
        w_ref, ms_ref, x1_ref, z_ref, yp_ref = refs[k:k + 5]
        o = of_ref[...].astype(F32) + ob_ref[...].astype(F32)
        gr = g_ref[...].astype(F32)
        if aw:
            z_ref[:, :aw] = _bf(a_ref[...])
        for hd in range(hv // dvh):
            sl = slice(hd * dvh, (hd + 1) * dvh)
            oh = o[:, sl]
            gh = gr[:, sl]
            r = lax.rsqrt(jnp.mean(oh * oh, axis=-1, keepdims=True) + EPS)
            y = oh * r
            if has_gain:
                y = y * gain_ref[...]
            y = y * (gh * _sig(gh))
            z_ref[:, aw + hd * dvh:aw + (hd + 1) * dvh] = _bf(y)
        yp = _nn(z_ref[...], w_ref[...])
        yp_ref[...] = _bf(yp)
        x1_ref[...] = _stream_tile(x_refs) + ms_ref[0] * yp

    ins = xs + [o_fw, o_bw, g_src]
    specs = _stream_specs(x, tm, dm) + [_rows(tm, hv), _rows(tm, hv), pl.BlockSpec((tm, hv), lambda i: (i, g_blk))]
    if has_gain:
        ins.append(gain)
        specs.append(_full(gain.shape))
    if aw:
        ins.append(a)
        specs.append(_rows(tm, aw))
    ins += [w_out, ms]
    specs += [_full(w_out.shape), _ctx_lat(dm)]
    return pl.pallas_call(
        body, name=name, grid=(T // tm,), in_specs=specs,
        out_specs=[_rows(tm, dm), _rows(tm, aw + hv), _rows(tm, dm)],
        out_shape=[jax.ShapeDtypeStruct((T, dm), F32), jax.ShapeDtypeStruct((T, aw + hv), ACT),
                   jax.ShapeDtypeStruct((T, dm), ACT)],
        compiler_params=_cp("arbitrary"),
    )(*ins)


def _post_bwd(dx1, z, yp, o_fw, o_bw, g_src, g_blk, gain, w_out, ms, aw, dvh, tm, name):
    T, dm = dx1.shape
    hv = o_fw.shape[1]
    has_gain = gain is not None

    def body(*refs):
        refs = list(refs)
        dx1_ref, z_ref, yp_ref, of_ref, ob_ref, g_ref = refs[:6]
        k = 6
        gain_ref = None
        if has_gain:
            gain_ref = refs[k]
            k += 1
        w_ref, ms_ref = refs[k:k + 2]
        k += 2
        do_ref, dgr_ref = refs[k:k + 2]
        k += 2
        da_ref = None
        if aw:
            da_ref = refs[k]
            k += 1
        dy_ref, zt_ref, dgate_ref, dgain_ref = refs[k:k + 4]
        i = pl.program_id(0)
        dx1v = dx1_ref[...]
        zt_ref[...] = z_ref[...].T
        _acc_ctx_lat(dgate_ref, i, jnp.sum(dx1v * yp_ref[...].astype(F32), axis=0, keepdims=True))
        dyb = _bf(dx1v * ms_ref[0])
        dy_ref[...] = dyb
        dz = _nt(dyb, w_ref[...])
        if aw:
            da_ref[...] = dz[:, :aw]
        o = of_ref[...].astype(F32) + ob_ref[...].astype(F32)
        gr = g_ref[...].astype(F32)
        dgain = jnp.zeros((1, dvh), F32)
        for hd in range(hv // dvh):
            sl = slice(hd * dvh, (hd + 1) * dvh)
            oh = o[:, sl]
            gh = gr[:, sl]
            dyh = dz[:, aw + hd * dvh:aw + (hd + 1) * dvh]
            r = lax.rsqrt(jnp.mean(oh * oh, axis=-1, keepdims=True) + EPS)
            n = oh * r
            s = _sig(gh)
            sl_g = gh * s
            gn = gain_ref[...] if has_gain else 1.0
            dgr_ref[:, sl] = _bf(dyh * n * gn * (s * (1.0 + gh * (1.0 - s))))
            dn = dyh * gn * sl_g
            dgain = dgain + jnp.sum(dyh * n * sl_g, axis=0, keepdims=True)
            do_ref[:, sl] = _bf(r * (dn - n * jnp.mean(dn * n, axis=-1, keepdims=True)))
        _acc_all(dgain_ref, i, dgain)

    ins = [dx1, z, yp, o_fw, o_bw, g_src]
    specs = [_rows(tm, dm), _rows(tm, aw + hv), _rows(tm, dm), _rows(tm, hv), _rows(tm, hv),
             pl.BlockSpec((tm, hv), lambda i: (i, g_blk))]
    if has_gain:
        ins.append(gain)
        specs.append(_full(gain.shape))
    ins += [w_out, ms]
    specs += [_full(w_out.shape), _ctx_lat(dm)]
    out_specs = [_rows(tm, hv), _rows(tm, hv)]
    out_shape = [jax.ShapeDtypeStruct((T, hv), ACT), jax.ShapeDtypeStruct((T, hv), ACT)]
    if aw:
        out_specs.append(_rows(tm, aw))
        out_shape.append(jax.ShapeDtypeStruct((T, aw), F32))
    out_specs += [_rows(tm, dm), _cols(aw + hv, tm), _ctx_lat(dm), _whole((1, dvh))]
    out_shape += [jax.ShapeDtypeStruct((T, dm), ACT), jax.ShapeDtypeStruct((aw + hv, T), ACT),
                  jax.ShapeDtypeStruct((2, 1, dm), F32), jax.ShapeDtypeStruct((1, dvh), F32)]
    return pl.pallas_call(
        body, name=name, grid=(T // tm,), in_specs=specs, out_specs=out_specs, out_shape=out_shape,
        compiler_params=_cp("arbitrary"),
    )(*ins)


PAIR = 2 * HEAD_DIM
N_PAIRS = (ATTN_HEADS + ATTN_KV) // 2


def _lanes():
    return lax.broadcasted_iota(jnp.int32, (1, PAIR), 1)


def _swap32(v):
    first_half = (_lanes() & (HEAD_DIM // 2)) == 0
    return jnp.where(first_half, pltpu.roll(v, PAIR - HEAD_DIM // 2, 1), pltpu.roll(v, HEAD_DIM // 2, 1))


def _head_mean(v):
    r = lax.broadcasted_iota(jnp.int32, (PAIR, PAIR), 0)
    c = lax.broadcasted_iota(jnp.int32, (PAIR, PAIR), 1)
    same = jnp.where((r >= HEAD_DIM) == (c >= HEAD_DIM), 1.0, 0.0).astype(BF16)
    return _nn3r(v, same) * (1.0 / HEAD_DIM)


def _qk_tile_fwd(pa, g_ref, cosv, sinv, q_ref, k_ref, v_ref):
    qw = ATTN_HEADS * HEAD_DIM
    for p in range(N_PAIRS):
        xv = pa[:, p * PAIR:(p + 1) * PAIR]
        n = xv * lax.rsqrt(_head_mean(xv * xv) + EPS) * g_ref[p]
        y = n * cosv + _swap32(n) * sinv
        if p < N_PAIRS - 1:
            q_ref[:, p * PAIR:(p + 1) * PAIR] = _bf(y * HEAD_DIM ** -0.5)
        else:
            k_ref[...] = _bf(y)
    v_ref[...] = _bf(pa[:, qw + PAIR:])


def _qk_tile_bwd(dq_ref, dk_ref, pa_ref, g_ref, cosv, sinv):
    dxs, dgs = [], []
    for p in range(N_PAIRS):
        sl = slice(p * PAIR, (p + 1) * PAIR)
        xv = pa_ref[:, sl]
        r = lax.rsqrt(_head_mean(xv * xv) + EPS)
        xhat = xv * r
        dy = dq_ref[:, sl] * HEAD_DIM ** -0.5 if p < N_PAIRS - 1 else dk_ref[...]
        dn = dy * cosv + _swap32(dy * sinv)
        dgs.append(jnp.sum(dn * xhat, axis=0, keepdims=True))
        dxh = dn * g_ref[p]
        dxs.append(r * (dxh - xhat * _head_mean(dxh * xhat)))
    return jnp.concatenate(dxs, axis=1), dgs


def _attn_window(ref, i, nb):
    blk = ATTN_BLOCK
    starts = [pl.multiple_of(jnp.clip(i + d, 0, nb - 1) * blk, blk) for d in (-1, 0, 1)]
    return starts, jnp.concatenate([ref[pl.ds(s, blk), :] for s in starts], axis=0)


GROUP_HEADS = 2
ATTN_STEP_BLOCKS = 3


def _head_groups(n):
    g = ATTN_HEADS // ATTN_KV
    return [(kv, [kv * g + s + j for j in range(n)]) for kv in range(ATTN_KV) for s in range(0, g, n)]


def _attn_mask(i, lc, T, rows):
    blk = ATTN_BLOCK
    row = lax.broadcasted_iota(jnp.int32, (rows, 1), 0)
    qpos = i * blk + (row & (blk - 1))
    kpos = (i - 1) * blk + lax.broadcasted_iota(jnp.int32, (1, 3 * blk), 1)
    return (qpos >= lc) & (kpos >= lc) & (kpos < T) & (jnp.abs(kpos - qpos) <= WINDOW)


def _to_kv_half(v, head, kv):
    return v if head % 2 == kv else pltpu.roll(v, HEAD_DIM, 1)


def _attn_slab_fwd(qt, ks, vs, sinkb, lc, name, ex=None):
    T = qt.shape[0]
    blk = ATTN_BLOCK
    nb = T // blk
    g = ATTN_HEADS // ATTN_KV

    spb = ATTN_STEP_BLOCKS
    ng = nb // spb

    def one_block(i, rows, q_ref, k_ref, v_ref, sink_ref, o_ref, lse_ref):
        lane = _lanes()
        valid = _attn_mask(i, lc, T, GROUP_HEADS * blk)
        kc_all, vc = k_ref[0:lc, :], v_ref[0:lc, :]
        _, kw_all = _attn_window(k_ref, i, nb)
        _, vw = _attn_window(v_ref, i, nb)
        kc, kw = [], []
        for kv in range(ATTN_KV):
            mine = (lane >= kv * HEAD_DIM) & (lane < (kv + 1) * HEAD_DIM)
            kc.append(jnp.where(mine, kc_all, jnp.zeros_like(kc_all)))
            kw.append(jnp.where(mine, kw_all, jnp.zeros_like(kw_all)))
        groups = _head_groups(GROUP_HEADS)
        qg = [jnp.concatenate([_to_kv_half(q_ref[rows, (h // 2) * PAIR:(h // 2 + 1) * PAIR], h, kv) for h in heads], axis=0)
              for kv, heads in groups]
        sinks = [sink_ref[kv, (heads[0] - kv * g) * blk:(heads[-1] + 1 - kv * g) * blk] for kv, heads in groups]
        s_c = [_nt(q, kc[kv]) for q, (kv, _) in zip(qg, groups)]
        s_w = [jnp.where(valid, _nt(q, kw[kv]), NEG) for q, (kv, _) in zip(qg, groups)]
        m = [jnp.maximum(jnp.maximum(jnp.max(a, axis=-1, keepdims=True), jnp.max(b, axis=-1, keepdims=True)), s)
             for a, b, s in zip(s_c, s_w, sinks)]
        e_c = [jnp.exp(a - mm) for a, mm in zip(s_c, m)]
        e_w = [jnp.exp(b - mm) for b, mm in zip(s_w, m)]
        den = [jnp.exp(s - mm) + jnp.sum(a, axis=-1, keepdims=True) + jnp.sum(b, axis=-1, keepdims=True)
               for s, mm, a, b in zip(sinks, m, e_c, e_w)]
        inv = [1.0 / d for d in den]
        og = [_nn(_bf(a * r), vc) + _nn(_bf(b * r), vw) for a, b, r in zip(e_c, e_w, inv)]
        placed = [None] * ATTN_HEADS
        for (kv, heads), o2, mm, d in zip(groups, og, m, den):
            lse_ref[heads[0]:heads[-1] + 1, rows, :] = (mm + jnp.log(d)).reshape(len(heads), blk, 1)
            for j, h in enumerate(heads):
                placed[h] = _to_kv_half(o2[j * blk:(j + 1) * blk], h, kv)
        for p in range(ATTN_HEADS // 2):
            o_ref[rows, p * PAIR:(p + 1) * PAIR] = jnp.where(lane < HEAD_DIM, placed[2 * p], placed[2 * p + 1])

    def body(*refs):
        for j in range(spb):
            one_block(pl.program_id(0) * spb + j, pl.ds(j * blk, blk), *refs)

    qw = ATTN_HEADS * HEAD_DIM
    return _host_call(
        body, ex, lambda: pl.program_id(0) == 0, lambda: pl.program_id(0) == ng - 1,
        name=name, grid=(ng,),
        in_specs=[_rows(spb * blk, qw), _full((T, PAIR)), _full((T, PAIR)), _full(sinkb.shape)],
        out_specs=[_rows(spb * blk, qw), pl.BlockSpec((ATTN_HEADS, spb * blk, 1), lambda i: (0, i, 0))],
        out_shape=[jax.ShapeDtypeStruct((T, qw), F32), jax.ShapeDtypeStruct((ATTN_HEADS, T, 1), F32)],
        scratch_shapes=[], sem=("arbitrary",), args=(qt, ks, vs, sinkb))


def _attn_slab_bwd(qt, ks, vs, sinkb, o, lse, do, lc, name, ex=None):
    T = qt.shape[0]
    blk = ATTN_BLOCK
    nb = T // blk
    g = ATTN_HEADS // ATTN_KV

    spb = ATTN_STEP_BLOCKS
    ng = nb // spb

    def body(*refs):
        dk_ref, dv_ref, ds_ref = refs[8:11]

        @pl.when(pl.program_id(0) == 0)
        def _():
            dk_ref[...] = jnp.zeros_like(dk_ref)
            dv_ref[...] = jnp.zeros_like(dv_ref)
            ds_ref[...] = jnp.zeros_like(ds_ref)

        for j in range(spb):
            one_block(pl.program_id(0) * spb + j, pl.ds(j * blk, blk), *refs)

    def one_block(i, rows, q_ref, k_ref, v_ref, sink_ref, o_ref, lse_ref, do_ref, dq_ref, dk_ref, dv_ref, ds_ref):
        lane = _lanes()
        valid = _attn_mask(i, lc, T, g * blk)
        kc_all, vc_all = k_ref[0:lc, :], v_ref[0:lc, :]
        starts, kw_all = _attn_window(k_ref, i, nb)
        _, vw_all = _attn_window(v_ref, i, nb)
        dq_pairs = [jnp.zeros((blk, PAIR), F32) for _ in range(ATTN_HEADS // 2)]
        for kv in range(ATTN_KV):
            mine = (lane >= kv * HEAD_DIM) & (lane < (kv + 1) * HEAD_DIM)

            def only(v):
                return jnp.where(mine, v, jnp.zeros_like(v))

            kc, kw, vc, vw = only(kc_all), only(kw_all), only(vc_all), only(vw_all)
            heads = [kv * g + j for j in range(g)]
            qs, dos, deltas = [], [], []
            for h in heads:
                sl = slice((h // 2) * PAIR, (h // 2 + 1) * PAIR)
                dov = do_ref[rows, sl]
                qs.append(_to_kv_half(q_ref[rows, sl], h, kv))
                dos.append(_bf(_to_kv_half(dov, h, kv)))
                own = (lane < HEAD_DIM) if h % 2 == 0 else (lane >= HEAD_DIM)
                deltas.append(jnp.sum(jnp.where(own, dov * o_ref[rows, sl], 0.0), axis=-1, keepdims=True))
            q4, do4, delta = jnp.concatenate(qs, axis=0), jnp.concatenate(dos, axis=0), jnp.concatenate(deltas, axis=0)
            sink = sink_ref[kv]
            lse = lse_ref[kv * g:(kv + 1) * g, rows, :].reshape(g * blk, 1)
            p_c = jnp.exp(_nt(q4, kc) - lse)
            p_w = jnp.exp(jnp.where(valid, _nt(q4, kw), NEG) - lse)
            ds_c = _bf(p_c * (_nt(do4, vc) - delta))
            ds_w = _bf(p_w * (_nt(do4, vw) - delta))
            dsr = -jnp.exp(sink - lse) * delta
            dq4 = _nn(ds_c, kc) + _nn(ds_w, kw)
            for j, h in enumerate(heads):
                ds_ref[h:h + 1, :] += jnp.sum(dsr[j * blk:(j + 1) * blk, :], axis=0, keepdims=True)
                dq_pairs[h // 2] = dq_pairs[h // 2] + _to_kv_half(dq4[j * blk:(j + 1) * blk], h, kv)
            dk_ref[0:lc, :] += only(_tn(ds_c, q4))
            dv_ref[0:lc, :] += only(_tn(_bf(p_c), do4))
            dkw = only(_tn(ds_w, q4))
            dvw = only(_tn(_bf(p_w), do4))
            for b, s in enumerate(starts):
                dk_ref[pl.ds(s, blk), :] += dkw[b * blk:(b + 1) * blk]
                dv_ref[pl.ds(s, blk), :] += dvw[b * blk:(b + 1) * blk]
        for p in range(ATTN_HEADS // 2):
            dq_ref[rows, p * PAIR:(p + 1) * PAIR] = dq_pairs[p]

    qw = ATTN_HEADS * HEAD_DIM
    lspec = pl.BlockSpec((ATTN_HEADS, spb * blk, 1), lambda i: (0, i, 0))
    return _host_call(
        body, ex, lambda: pl.program_id(0) == 0, lambda: pl.program_id(0) == ng - 1,
        name=name, grid=(ng,),
        in_specs=[_rows(spb * blk, qw), _full((T, PAIR)), _full((T, PAIR)), _full(sinkb.shape), _rows(spb * blk, qw), lspec,
                  _rows(spb * blk, qw)],
        out_specs=[_rows(spb * blk, qw), _whole((T, PAIR)), _whole((T, PAIR)), _whole((ATTN_HEADS, 1))],
        out_shape=[jax.ShapeDtypeStruct((T, qw), F32), jax.ShapeDtypeStruct((T, PAIR), F32),
                   jax.ShapeDtypeStruct((T, PAIR), F32), jax.ShapeDtypeStruct((ATTN_HEADS, 1), F32)],
        scratch_shapes=[], sem=("arbitrary",), args=(qt, ks, vs, sinkb, o, lse, do))


def _fw_chunk(s, nc, nt):
    return s


def _bw_chunk(s, nc, nt):
    return jnp.where(s < nc, nc - 1 - s, nt - 1 - (s - nc))


def _tri(c, rev):
    r = lax.broadcasted_iota(jnp.int32, (c, c), 0)
    k = lax.broadcasted_iota(jnp.int32, (c, c), 1)
    return (k >= r) if rev else (k <= r)


def _gla_gates(z, lb, rev):
    c = HG_CHUNK
    sg = _sig(z)
    f = lb + (1.0 - lb) * sg
    cum = _nn3(jnp.where(_tri(c, rev), 1.0, 0.0).astype(BF16), jnp.log(f))
    mid = c - 1 - c // 2 if rev else c // 2
    last = 0 if rev else c - 1
    return sg, f, cum, cum[mid:mid + 1], cum[last:last + 1], last


def _lower_bound(lbraw_ref):
    lr = lbraw_ref[...]
    return _sig(lr[0:1] - lr[1:2])


def _gla_fwd(pb, lbraw, lc, name, ex=None):
    T = pb.shape[0]
    c, hw, d, ns = HG_CHUNK, HG_HEADS * HG_D, HG_D, HG_STEP_CHUNKS
    nt, nc = T // (ns * c), lc // (ns * c)
    orders = (_fw_chunk, _bw_chunk)

    def body(qf, zf, vf, qb, zb, vb, lb_ref, of_ref, ob_ref, sf_ref, sb_ref, st_ref):
        @pl.when(pl.program_id(0) == 0)
        def _():
            st_ref[...] = jnp.zeros_like(st_ref)

        lb = _lower_bound(lb_ref)
        dirs = ((qf, zf, vf, of_ref, sf_ref), (qb, zb, vb, ob_ref, sb_ref))
        combos = [(dr, h, slice(h * d, (h + 1) * d)) for dr in range(2) for h in range(HG_HEADS)]
        for j in range(ns):
            sub = (j, ns - 1 - j)
            rows = [pl.ds(sub[dr] * c, c) for dr in range(2)]
            prep = []
            for dr, (q_ref, z_ref, v_ref, _, _) in enumerate(dirs):
                rev = dr == 1
                qr = q_ref[rows[dr], :]
                q = qr * _sig(qr)
                _, f, cum, ref, last, _ = _gla_gates(z_ref[rows[dr], :], lb, rev)
                k = 1.0 - f
                prep.append(dict(q1=_bf(q * jnp.exp(cum - ref)), k1=_bf(k * jnp.exp(ref - cum)), q2=_bf(q * jnp.exp(cum)),
                                 k2=_bf(k * jnp.exp(last - cum)), el=jnp.exp(last), v=_bf(v_ref[rows[dr], :]),
                                 mask=_tri(c, rev)))
            a = [_bf(jnp.where(prep[dr]["mask"], _nt(prep[dr]["q1"][:, sl], prep[dr]["k1"][:, sl]), 0.0))
                 for dr, _, sl in combos]
            for (dr, h, sl), a_h in zip(combos, a):
                p = prep[dr]
                o_ref, s_ref = dirs[dr][3], dirs[dr][4]
                st = st_ref[dr, h]
                stb = _bf(st)
                s_ref[sub[dr], h] = stb
                o_ref[rows[dr], sl] = _nn(a_h, p["v"][:, sl]) + _nt(p["q2"][:, sl], stb)
                st_ref[dr, h] = st * p["el"][:, sl] + _tn(p["v"][:, sl], p["k2"][:, sl])

    def col(order, blkcol):
        return pl.BlockSpec((ns * c, hw), lambda s: (order(s, nc, nt), blkcol))

    def st_spec(order):
        return pl.BlockSpec((ns, HG_HEADS, d, d), lambda s: (order(s, nc, nt), 0, 0, 0))

    in_specs = []
    for dr, order in enumerate(orders):
        in_specs += [col(order, 0), col(order, 1 + dr), col(order, 3)]
    in_specs.append(_full(lbraw.shape))
    return _host_call(
        body, ex, lambda: pl.program_id(0) == 0, lambda: pl.program_id(0) == nt - 1,
        name=name, grid=(nt,), in_specs=in_specs,
        out_specs=[col(_fw_chunk, 0), col(_bw_chunk, 0), st_spec(_fw_chunk), st_spec(_bw_chunk)],
        out_shape=[jax.ShapeDtypeStruct((T, hw), F32), jax.ShapeDtypeStruct((T, hw), F32),
                   jax.ShapeDtypeStruct((nt * ns, HG_HEADS, d, d), ACT), jax.ShapeDtypeStruct((nt * ns, HG_HEADS, d, d), ACT)],
        scratch_shapes=[pltpu.VMEM((2, HG_HEADS, d, d), F32)], sem=("arbitrary",),
        args=(pb, pb, pb, pb, pb, pb, lbraw))


def _gla_bwd(pb, lbraw, s_fw, s_bw, do, lc, name, ex=None):
    T = pb.shape[0]
    c, hw, d, ns = HG_CHUNK, HG_HEADS * HG_D, HG_D, HG_STEP_CHUNKS
    nt, nc = T // (ns * c), lc // (ns * c)

    def rfw(s, nc_, nt_):
        return _fw_chunk(nt_ - 1 - s, nc_, nt_)

    def rbw(s, nc_, nt_):
        return _bw_chunk(nt_ - 1 - s, nc_, nt_)

    def body(qf, zf, vf, sf, dof, qb, zb, vb, sb, dob_, lb_ref,
             dqf, dzf, dvf, dqb, dzb, dvb, dlb_ref, dst_ref):
        step = pl.program_id(0)

        @pl.when(step == 0)
        def _():
            dst_ref[...] = jnp.zeros_like(dst_ref)

        lb = _lower_bound(lb_ref)
        sets = ((qf, zf, vf, sf, dof, dqf, dzf, dvf), (qb, zb, vb, sb, dob_, dqb, dzb, dvb))
        combos = [(dr, h, slice(h * d, (h + 1) * d)) for dr in range(2) for h in range(HG_HEADS)]
        dlb_tot = jnp.zeros((1, hw), F32)
        for j in range(ns):
            sub = (ns - 1 - j, j)
            rows = [pl.ds(sub[dr] * c, c) for dr in range(2)]
            prep = []
            for dr, (q_ref, z_ref, v_ref, _, do_ref, _, _, _) in enumerate(sets):
                rev = dr == 1
                qr = q_ref[rows[dr], :]
                sq = _sig(qr)
                q = qr * sq
                sg, f, cum, ref, last, last_row = _gla_gates(z_ref[rows[dr], :], lb, rev)
                k = 1.0 - f
                e_qr, e_kr, e_q, e_kl = jnp.exp(cum - ref), jnp.exp(ref - cum), jnp.exp(cum), jnp.exp(last - cum)
                q1, k1, q2, k2 = q * e_qr, k * e_kr, q * e_q, k * e_kl
                prep.append(dict(qr=qr, sq=sq, sg=sg, f=f, e_qr=e_qr, e_kr=e_kr, e_q=e_q, e_kl=e_kl, el=jnp.exp(last),
                                 q1=q1, k1=k1, q2=q2, k2=k2, q1b=_bf(q1), k1b=_bf(k1), q2b=_bf(q2), k2b=_bf(k2),
                                 vb=_bf(v_ref[rows[dr], :]), dob=_bf(do_ref[rows[dr], :]), mask=_tri(c, rev),
                                 last_row=last_row, acc_t=jnp.where(_tri(c, not rev), 1.0, 0.0).astype(BF16)))
            a = [_bf(jnp.where(prep[dr]["mask"], _nt(prep[dr]["q1b"][:, sl], prep[dr]["k1b"][:, sl]), 0.0))
                 for dr, _, sl in combos]
            da = [_bf(jnp.where(prep[dr]["mask"], _nt(prep[dr]["dob"][:, sl], prep[dr]["vb"][:, sl]), 0.0))
                  for dr, _, sl in combos]
            parts = [dict(dq1=[], dk1=[], dq2=[], dk2=[], dls=[]) for _ in range(2)]
            for (dr, h, sl), a_h, da_h in zip(combos, a, da):
                p = prep[dr]
                s_ref, dv_ref = sets[dr][3], sets[dr][7]
                stb = s_ref[sub[dr], h]
                dst = dst_ref[dr, h]
                dstb = _bf(dst)
                dob_h, vb_h = p["dob"][:, sl], p["vb"][:, sl]
                dv_ref[rows[dr], sl] = _bf(_tn(a_h, dob_h) + _nt(p["k2b"][:, sl], dstb))
                parts[dr]["dq1"].append(_nn(da_h, p["k1b"][:, sl]))
                parts[dr]["dk1"].append(_tn(da_h, p["q1b"][:, sl]))
                parts[dr]["dq2"].append(_nn(dob_h, stb))
                parts[dr]["dk2"].append(_nn(vb_h, dstb))
                el_h = p["el"][:, sl]
                dst_ref[dr, h] = _tn(dob_h, p["q2b"][:, sl]) + dst * el_h
                parts[dr]["dls"].append(jnp.sum(dst * stb.astype(F32), axis=0, keepdims=True) * el_h)
            for dr in range(2):
                p = prep[dr]
                dq_ref, dz_ref = sets[dr][5], sets[dr][6]
                dq1, dk1, dq2, dk2, dls = (jnp.concatenate(parts[dr][n], axis=1) for n in ("dq1", "dk1", "dq2", "dk2", "dls"))
                dq = dq1 * p["e_qr"] + dq2 * p["e_q"]
                dk = dk1 * p["e_kr"] + dk2 * p["e_kl"]
                dcum = dq1 * p["q1"] - dk1 * p["k1"] + dq2 * p["q2"] - dk2 * p["k2"]
                dlast = jnp.sum(dk2 * p["k2"], axis=0, keepdims=True) + dls
                rowid = lax.broadcasted_iota(jnp.int32, (c, 1), 0)
                dcum = dcum + jnp.where(rowid == p["last_row"], dlast, 0.0)
                df = _nn3(p["acc_t"], dcum) / p["f"] - dk
                sg = p["sg"]
                dz_ref[rows[dr], :] = _bf(df * (1.0 - lb) * sg * (1.0 - sg))
                dlb_tot = dlb_tot + jnp.sum(df * (1.0 - sg), axis=0, keepdims=True)
                dq_ref[rows[dr], :] = _bf(dq * (p["sq"] * (1.0 + p["qr"] * (1.0 - p["sq"]))))
        _acc_all(dlb_ref, step, dlb_tot)

    def col(order, blkcol):
        return pl.BlockSpec((ns * c, hw), lambda s: (order(s, nc, nt), blkcol))

    def st_spec(order):
        return pl.BlockSpec((ns, HG_HEADS, d, d), lambda s: (order(s, nc, nt), 0, 0, 0))

    in_specs = []
    for dr, order in enumerate((rfw, rbw)):
        in_specs += [col(order, 0), col(order, 1 + dr), col(order, 3), st_spec(order), col(order, 0)]
    in_specs.append(_full(lbraw.shape))
    out_specs = [col(rfw, 0)] * 3 + [col(rbw, 0)] * 3 + [_whole((1, hw))]
    out_shape = [jax.ShapeDtypeStruct((T, hw), ACT)] * 6 + [jax.ShapeDtypeStruct((1, hw), F32)]
    return _host_call(
        body, ex, lambda: pl.program_id(0) == 0, lambda: pl.program_id(0) == nt - 1,
        name=name, grid=(nt,), in_specs=in_specs, out_specs=out_specs, out_shape=out_shape,
        scratch_shapes=[pltpu.VMEM((2, HG_HEADS, d, d), F32)], sem=("arbitrary",),
        args=(pb, pb, pb, s_fw, do, pb, pb, pb, s_bw, do, lbraw))


def _ret_log_gamma(h, rev):
    hh = RET_HEADS - 1 - h if rev else h
    return math.log(1.0 - 2.0 ** (-5.0 - hh))


def _rope(x, cos, sin):
    half = x.shape[1] // 2
    x1, x2 = x[:, :half], x[:, half:]
    return jnp.concatenate([x1 * cos - x2 * sin, x2 * cos + x1 * sin], axis=1)


def _unrope(dy, cos, sin):
    half = dy.shape[1] // 2
    d1, d2 = dy[:, :half], dy[:, half:]
    return jnp.concatenate([d1 * cos + d2 * sin, d2 * cos - d1 * sin], axis=1)


def _ret_decays(lg, rev):
    c = RET_CHUNK
    r = lax.broadcasted_iota(jnp.int32, (c, c), 0)
    k = lax.broadcasted_iota(jnp.int32, (c, c), 1)
    rel = (k - r) if rev else (r - k)
    dm = jnp.where(rel >= 0, jnp.exp(lg * jnp.maximum(rel, 0).astype(F32)), 0.0)
    pos = lax.broadcasted_iota(jnp.int32, (c, 1), 0).astype(F32)
    if rev:
        qdec = jnp.exp(lg * (c - pos))
        kdec = jnp.exp(lg * pos)
    else:
        qdec = jnp.exp(lg * (pos + 1.0))
        kdec = jnp.exp(lg * (c - 1.0 - pos))
    return dm, qdec, kdec


def _ret_fwd(q, k, v, cos, sin, lc, name, ex=None):
    T = q.shape[0]
    c, dk, dv = RET_CHUNK, RET_DK, RET_DV
    nt, nc = T // c, lc // c
    kscale = dk ** -0.5

    def body(qf, kf, vf, cf, sf_, qb, kb, vb, cb, sb_, of_ref, ob_ref, stf_ref, stb_ref, st_ref):
        @pl.when(pl.program_id(0) == 0)
        def _():
            st_ref[...] = jnp.zeros_like(st_ref)

        sets = ((qf, kf, vf, cf, sf_, of_ref, stf_ref), (qb, kb, vb, cb, sb_, ob_ref, stb_ref))
        combos = [(dr, h) for dr in range(2) for h in range(RET_HEADS)]
        prep = {}
        for dr, (q_ref, k_ref, v_ref, c_ref, s_ref, _, _) in enumerate(sets):
            rev = dr == 1
            cos_v, sin_v = c_ref[...], s_ref[...]
            for h in range(RET_HEADS):
                lg = _ret_log_gamma(h, rev)
                dm, qdec, kdec = _ret_decays(lg, rev)
                qh = _rope(q_ref[:, h * dk:(h + 1) * dk].astype(F32), cos_v, sin_v)
                kh = _rope(k_ref[:, h * dk:(h + 1) * dk].astype(F32), cos_v, sin_v) * kscale
                prep[dr, h] = dict(qb=_bf(qh), kb=_bf(kh), qin=_bf(qh * qdec), kin=_bf(kh * kdec),
                                   v=_bf(v_ref[:, h * dv:(h + 1) * dv]), dm=dm, decay=math.exp(lg * c))
        sc = {ch: _bf(_nt(prep[ch]["qb"], prep[ch]["kb"]) * prep[ch]["dm"]) for ch in combos}
        for dr, h in combos:
            p = prep[dr, h]
            o_ref, so_ref = sets[dr][5], sets[dr][6]
            st = st_ref[dr, h]
            stb = _bf(st)
            so_ref[0, h] = stb
            o_ref[:, h * dv:(h + 1) * dv] = _bf(_nn(sc[dr, h], p["v"]) + _nt(p["qin"], stb))
            st_ref[dr, h] = st * p["decay"] + _tn(p["v"], p["kin"])

    def spec(order, width):
        return pl.BlockSpec((c, width), lambda s: (order(s, nc, nt), 0))

    def st_spec(order):
        return pl.BlockSpec((1, RET_HEADS, dv, dk), lambda s: (order(s, nc, nt), 0, 0, 0))

    in_specs = []
    for order in (_fw_chunk, _bw_chunk):
        in_specs += [spec(order, RET_HEADS * dk), spec(order, RET_HEADS * dk), spec(order, RET_HEADS * dv),
                     spec(order, dk // 2), spec(order, dk // 2)]
    return _host_call(
        body, ex, lambda: pl.program_id(0) == 0, lambda: pl.program_id(0) == nt - 1,
        name=name, grid=(nt,), in_specs=in_specs,
        out_specs=[spec(_fw_chunk, RET_HEADS * dv), spec(_bw_chunk, RET_HEADS * dv), st_spec(_fw_chunk), st_spec(_bw_chunk)],
        out_shape=[jax.ShapeDtypeStruct((T, RET_HEADS * dv), ACT), jax.ShapeDtypeStruct((T, RET_HEADS * dv), ACT),
                   jax.ShapeDtypeStruct((nt, RET_HEADS, dv, dk), ACT), jax.ShapeDtypeStruct((nt, RET_HEADS, dv, dk), ACT)],
        scratch_shapes=[pltpu.VMEM((2, RET_HEADS, dv, dk), F32)], sem=("arbitrary",),
        args=(q, k, v, cos, sin, q, k, v, cos, sin))


def _ret_bwd(q, k, v, cos, sin, s_fw, s_bw, do, lc, name, ex=None):
    T = q.shape[0]
    c, dk, dv = RET_CHUNK, RET_DK, RET_DV
    nt, nc = T // c, lc // c
    kscale = dk ** -0.5

    def rfw(s, nc_, nt_):
        return _fw_chunk(nt_ - 1 - s, nc_, nt_)

    def rbw(s, nc_, nt_):
        return _bw_chunk(nt_ - 1 - s, nc_, nt_)

    def body(qf, kf, vf, cf, sf_, stf, dof, qb, kb, vb, cb, sb_, stb_, dob_,
             dqf, dkf, dvf, dqb, dkb, dvb, dst_ref):
        @pl.when(pl.program_id(0) == 0)
        def _():
            dst_ref[...] = jnp.zeros_like(dst_ref)

        sets = ((qf, kf, vf, cf, sf_, stf, dof, dqf, dkf, dvf), (qb, kb, vb, cb, sb_, stb_, dob_, dqb, dkb, dvb))
        combos = [(dr, h) for dr in range(2) for h in range(RET_HEADS)]
        prep = {}
        for dr, (q_ref, k_ref, v_ref, c_ref, s_ref, _, do_ref, _, _, _) in enumerate(sets):
            rev = dr == 1
            cos_v, sin_v = c_ref[...], s_ref[...]
            for h in range(RET_HEADS):
                lg = _ret_log_gamma(h, rev)
                dm, qdec, kdec = _ret_decays(lg, rev)
                qh = _rope(q_ref[:, h * dk:(h + 1) * dk].astype(F32), cos_v, sin_v)
                kh = _rope(k_ref[:, h * dk:(h + 1) * dk].astype(F32), cos_v, sin_v) * kscale
                prep[dr, h] = dict(qb=_bf(qh), kb=_bf(kh), qin=_bf(qh * qdec), kin=_bf(kh * kdec),
                                   v=_bf(v_ref[:, h * dv:(h + 1) * dv]), dob=_bf(do_ref[:, h * dv:(h + 1) * dv]),
                                   dm=dm, qdec=qdec, kdec=kdec, decay=math.exp(lg * c), cos=cos_v, sin=sin_v)
        sc = {ch: _bf(_nt(prep[ch]["qb"], prep[ch]["kb"]) * prep[ch]["dm"]) for ch in combos}
        dsc = {ch: _bf(_nt(prep[ch]["dob"], prep[ch]["v"]) * prep[ch]["dm"]) for ch in combos}
        carried = {}
        for dr, h in combos:
            p = prep[dr, h]
            dv_ref = sets[dr][9]
            dst = dst_ref[dr, h]
            dstb = _bf(dst)
            carried[dr, h] = dstb
            dv_ref[:, h * dv:(h + 1) * dv] = _bf(_tn(sc[dr, h], p["dob"]) + _nt(p["kin"], dstb))
            dst_ref[dr, h] = _tn(p["dob"], p["qin"]) + dst * p["decay"]
        for dr, h in combos:
            p = prep[dr, h]
            st_in, dq_ref, dk_ref = sets[dr][5], sets[dr][7], sets[dr][8]
            dq_r = _nn(dsc[dr, h], p["kb"]) + _nn(p["dob"], st_in[0, h]) * p["qdec"]
            dk_r = _tn(dsc[dr, h], p["qb"]) + _nn(p["v"], carried[dr, h]) * p["kdec"]
            dq_ref[:, h * dk:(h + 1) * dk] = _bf(_unrope(dq_r, p["cos"], p["sin"]))
            dk_ref[:, h * dk:(h + 1) * dk] = _bf(_unrope(dk_r * kscale, p["cos"], p["sin"]))

    def spec(order, width):
        return pl.BlockSpec((c, width), lambda s: (order(s, nc, nt), 0))

    def st_spec(order):
        return pl.BlockSpec((1, RET_HEADS, dv, dk), lambda s: (order(s, nc, nt), 0, 0, 0))

    in_specs = []
    for order in (rfw, rbw):
        in_specs += [spec(order, RET_HEADS * dk), spec(order, RET_HEADS * dk), spec(order, RET_HEADS * dv),
                     spec(order, dk // 2), spec(order, dk // 2), st_spec(order), spec(order, RET_HEADS * dv)]
    out_specs, out_shape = [], []
    for order in (rfw, rbw):
        out_specs += [spec(order, RET_HEADS * dk), spec(order, RET_HEADS * dk), spec(order, RET_HEADS * dv)]
        out_shape += [jax.ShapeDtypeStruct((T, RET_HEADS * dk), ACT), jax.ShapeDtypeStruct((T, RET_HEADS * dk), ACT),
                      jax.ShapeDtypeStruct((T, RET_HEADS * dv), ACT)]
    return _host_call(
        body, ex, lambda: pl.program_id(0) == 0, lambda: pl.program_id(0) == nt - 1,
        name=name, grid=(nt,), in_specs=in_specs, out_specs=out_specs, out_shape=out_shape,
        scratch_shapes=[pltpu.VMEM((2, RET_HEADS, dv, dk), F32)], sem=("arbitrary",),
        args=(q, k, v, cos, sin, s_fw, do, q, k, v, cos, sin, s_bw, do))


def _trig_rows(lc, ang):
    ang = ang.astype(np.float64)
    half = ang.shape[1]
    cos = np.concatenate([np.ones((lc, half)), np.cos(ang)], axis=0).astype(np.float32)
    sin = np.concatenate([np.zeros((lc, half)), np.sin(ang)], axis=0).astype(np.float32)
    return cos, sin


def _attn_rope_tables(lc, l):
    t = np.arange(l)
    row = (t // GRID_W).astype(np.float32)
    colp = (t % GRID_W).astype(np.float32)
    n_freq = HEAD_DIM // 4
    inv = np.float32(10000.0) ** (-np.arange(n_freq, dtype=np.float32) / np.float32(n_freq))
    ang = np.concatenate([row[:, None] * inv, colp[:, None] * inv], axis=-1)
    cos, sin = _trig_rows(lc, ang)
    return jnp.asarray(np.concatenate([cos, cos], axis=1)), jnp.asarray(np.concatenate([-sin, sin], axis=1))


def _ret_rope_tables(lc, l):
    theta = np.float32(1.0) / (np.float32(10000.0) ** np.linspace(0.0, 1.0, RET_DK // 2, dtype=np.float32))
    ang = np.arange(l, dtype=np.float32)[:, None] * theta
    cos, sin = _trig_rows(lc, ang)
    return jnp.asarray(cos), jnp.asarray(sin)


COL_SHARDED =("ffn_in0", "ffn_in1", "even_in", "even_in_a", "even_in_b", "odd_in")


def _full_weight(name, g):
    if name in COL_SHARDED:
        return g.transpose(1, 0, 2).reshape(g.shape[1], -1)
    return g.reshape(-1, g.shape[2])


def _shard_slots(name, g):
    if name in COL_SHARDED:
        return g.reshape(g.shape[0], N_DEV, -1).transpose(1, 0, 2)
    return g.reshape(N_DEV, -1, g.shape[1])


def _local_step(xs, target, mv, norm_g, w, qk_g, sink, hg_out_g, lbraw, lc, shards=None):
    _, T, dm = _stream(xs)
    l = T - lc
    tm = lc
    blk = ATTN_BLOCK
    d2, d3 = 2 * dm, 3 * dm
    w = dict(w)
    gw, recv = {}, {}

    def ms(layer, a, b):
        return mv[layer, :, :, a:b]

    def gather(names):
        return None if shards is None or not names else _Exchange(GATHER2, [shards[n] for n in names])

    def arrived(names, got):
        for n, g in zip(names, got):
            w[n] = _full_weight(n, g)

    def scatter(names):
        return None if shards is None else _Exchange(SCATTER, [_shard_slots(n, gw[n]) for n in names])

    def scattered(names, got):
        for n, g in zip(names, got):
            recv[n] = g

    g00, g01, g10, g11 = (norm_g[i, j][None, :] for i in (0, 1) for j in (0, 1))

    cos2, sin2 = _attn_rope_tables(lc, l)
    cosp, sinp = jnp.concatenate([cos2, cos2], axis=1), jnp.concatenate([sin2, sin2], axis=1)
    gains5 = jnp.concatenate([jnp.broadcast_to(jnp.tile(qk_g[0], 2), (N_PAIRS - 1, PAIR)), jnp.tile(qk_g[1], 2)[None]])[:, None, :]
    riding = []
    (pa, pb, qt, ks, vs), got = _pre_fwd(xs, g00, ms(0, 0, d2), w["even_in"], ((0, 768), (768, 3328)), tm, "pre0_fwd",
                                         gather(riding), qk=(gains5, cosp, sinp))
    arrived(riding, got)
    sinkb = jnp.broadcast_to(sink.reshape(ATTN_KV, 4, 1, 1), (ATTN_KV, 4, blk, 1)).reshape(ATTN_KV, 4 * blk, 1)
    riding = ["ffn_in0", "even_out"]
    (a_slab, lse), got = _attn_slab_fwd(qt, ks, vs, sinkb, lc, "attn_fwd", gather(riding))
    arrived(riding, got)
    riding = ["ffn_out0", "odd_out"]
    (hg_of, hg_ob, hg_sf, hg_sb), got = _gla_fwd(pb, lbraw, lc, "hgrn_fwd", gather(riding))
    arrived(riding, got)
    x01, z0, yp0 = _post_fwd(xs, hg_of, hg_ob, pb, 4, hg_out_g, a_slab, w["even_out"], ms(0, d2, d3), HG_D, tm, "post0_fwd")
    riding = ["odd_in", "ffn_out1"]
    (x02, u0, f0), got = _ffn_fwd(x01, g01, ms(0, d3, 6 * dm), w["ffn_in0"], w["ffn_out0"], tm, "ffn0_fwd", ex=gather(riding))
    arrived(riding, got)

    riding = []
    (rq, rk, rv, rg), got = _pre_fwd(x02, g10, ms(1, 0, d2), w["odd_in"],
                                     ((0, 1024), (1024, 2048), (2048, 4096), (4096, 6144)), tm, "pre1_fwd", gather(riding),
                                     out_dtype=ACT)
    arrived(riding, got)
    rcos, rsin = _ret_rope_tables(lc, l)
    riding = ["ffn_in1"]
    (rt_of, rt_ob, rt_sf, rt_sb), got = _ret_fwd(rq, rk, rv, rcos, rsin, lc, "ret_fwd", gather(riding))
    arrived(riding, got)
    x11, z1, yp1 = _post_fwd(x02, rt_of, rt_ob, rg, 0, None, None, w["odd_out"], ms(1, d2, d3), RET_DV, tm, "post1_fwd")
    (dx, u1, f1, loss), _ = _ffn_fwd(x11, g11, ms(1, d3, 6 * dm), w["ffn_in1"], w["ffn_out1"], tm, "ffn1_fwd", target)

    (dx, h, du, act, df, dms_f1, dg11), _ = _ffn_bwd(x11, dx, u1, f1, g11, ms(1, d3, 6 * dm), w["ffn_in1"], w["ffn_out1"], tm,
                                                     "ffn1_bwd")
    gw["ffn_in1"] = _wgrad(h, du, "wg_ffn_in1")
    gw["ffn_out1"] = _wgrad(act, df, "wg_ffn_out1")
    do1, dgr1, dy1, z1_t, dgate_p1, _ = _post_bwd(dx, z1, yp1, rt_of, rt_ob, rg, 0, None, w["odd_out"], ms(1, d2, d3), 0, RET_DV, tm,
                                                  "post1_bwd")
    gw["odd_out"] = _wgrad(z1_t, dy1, "wg_odd_out")
    riding = ["ffn_in1"]
    (dqf, dkf, dvf, dqb, dkb, dvb), got = _ret_bwd(rq, rk, rv, rcos, rsin, rt_sf, rt_sb, do1, lc, "ret_bwd", scatter(riding))
    scattered(riding, got)
    riding = ["odd_out", "ffn_out1"]
    (dx, h, dp, dms_p1, dg10), got = _pre_bwd(x02, dx, g10, ms(1, 0, d2), w["odd_in"],
                                              [(0, [dqf, dqb]), (1024, [dkf, dkb]), (2048, [dvf, dvb]), (4096, [dgr1])], tm,
                                              "pre1_bwd", ex=scatter(riding))
    scattered(riding, got)
    gw["odd_in"] = _wgrad(h, dp, "wg_odd_in")

    riding = ["odd_in"]
    (dx, h, du, act, df, dms_f0, dg01), got = _ffn_bwd(x01, dx, u0, f0, g01, ms(0, d3, 6 * dm), w["ffn_in0"], w["ffn_out0"], tm,
                                                       "ffn0_bwd", scatter(riding))
    scattered(riding, got)
    gw["ffn_in0"] = _wgrad(h, du, "wg_ffn_in0")
    gw["ffn_out0"] = _wgrad(act, df, "wg_ffn_out0")
    do0, dgr0, da0, dy0, z0_t, dgate_p0, d_hg_gain = _post_bwd(dx, z0, yp0, hg_of, hg_ob, pb, 4, hg_out_g, w["even_out"],
                                                              ms(0, d2, d3), 512, HG_D, tm, "post0_bwd")
    gw["even_out"] = _wgrad(z0_t, dy0, "wg_even_out")
    riding = ["ffn_in0"]
    (hq_f, hz_f, hv_f, hq_b, hz_b, hv_b, dlb), got = _gla_bwd(pb, lbraw, hg_sf, hg_sb, do0, lc, "hgrn_bwd", scatter(riding))
    scattered(riding, got)
    riding = ["even_out", "ffn_out0"]
    (dq_att, dk_att, dv_att, dsink), got = _attn_slab_bwd(qt, ks, vs, sinkb, a_slab, lse, da0, lc, "attn_bwd", scatter(riding))
    scattered(riding, got)
    pieces0 = [(640, [dv_att]), (768, [hq_f, hq_b]), (1280, [hz_f]), (1792, [hz_b]), (2304, [hv_f, hv_b]), (2816, [dgr0])]
    (dx, h, dp, dms_p0, dg00, dgain5), _ = _pre_bwd(xs, dx, g00, ms(0, 0, d2), w["even_in"], pieces0, tm, "pre0_bwd",
                                                    latent_dx=shards is not None,
                                                    qk=(dq_att, dk_att, pa, gains5, cosp, sinp))
    if shards is None:
        gw["even_in"] = _wgrad(h, dp, "wg_even_in")
    else:
        half = dm // 2
        gw["even_in_a"] = _wgrad(h, dp, "wg_even_in_a", rows=(0, half))
        gw["even_in_b"], got = _wgrad(h, dp, "wg_even_in_b", rows=(half, half), ex=scatter(["even_in_a"]))
        scattered(["even_in_a"], got)

    dmv = jnp.stack([jnp.concatenate([dms_p0, dgate_p0, dms_f0], axis=2), jnp.concatenate([dms_p1, dgate_p1, dms_f1], axis=2)])
    small = {
        "dmv": dmv,
        "norm_g": jnp.stack([jnp.stack([dg00[0], dg01[0]]), jnp.stack([dg10[0], dg11[0]])]),
        "qk_g": jnp.stack([jnp.sum(dgain5[:N_PAIRS - 1, 0].reshape(-1, HEAD_DIM), axis=0),
                           jnp.sum(dgain5[N_PAIRS - 1, 0].reshape(-1, HEAD_DIM), axis=0)]),
        "sink": dsink.reshape(ATTN_HEADS),
        "hg_out_g": d_hg_gain[0],
        "lb": dlb[0],
        "loss": loss[0, 0],
    }
    if shards is not None:
        gw = {n: recv.get(n, g) for n, g in gw.items()}
    return loss, dx, gw, small


HBM_SPEC = pl.BlockSpec(memory_space=pltpu.HBM)


def _my_index():
    return 4 * lax.axis_index("x") + 2 * lax.axis_index("y") + lax.axis_index("c")


def _peer(k):
    pos = []
    for axis, bit in (("x", 4), ("y", 2), ("c", 1)):
        a = lax.axis_index(axis)
        pos.append(1 - a if k & bit else a)
    return tuple(pos)


def _peer_index(k):
    px, py, pc = _peer(k)
    return 4 * px + 2 * py + pc


GATHER, SCATTER = "gather", "scatter"
GATHER2 = "gather over ICI once per chip"
SIBLING = 1
OTHER_CHIPS = (2, 4, 6)


class _Exchange:
    def __init__(self, mode, arrays):
        self.mode, self.arrays, self.n = mode, list(arrays), len(arrays)

    def out_shape(self):
        if self.mode in (GATHER, GATHER2):
            return [jax.ShapeDtypeStruct((N_DEV,) + a.shape, a.dtype) for a in self.arrays]
        return [jax.ShapeDtypeStruct(a.shape, a.dtype) for a in self.arrays]

    def specs(self):
        return [HBM_SPEC] * self.n

    def scratch(self):
        return [pltpu.SemaphoreType.DMA((self.n, N_DEV - 1)), pltpu.SemaphoreType.DMA((self.n, N_DEV - 1)),
                pltpu.SemaphoreType.DMA((self.n,))]

    def _copies(self, in_refs, out_refs, send_sems, recv_sems, local_sems, landing):
        me = _my_index()
        local, remote = [], []
        for a, (src, dst) in enumerate(zip(in_refs, out_refs)):
            part = (lambda j, s=src: s) if self.mode == GATHER else (lambda j, s=src: s.at[j])
            local.append(pltpu.make_async_copy(part(me), dst.at[me], local_sems.at[a]))
            for k in range(1, N_DEV):
                pj = _peer_index(k)
                remote.append(pltpu.make_async_remote_copy(
                    src_ref=part(pj), dst_ref=dst.at[pj if landing else me], send_sem=send_sems.at[a, k - 1],
                    recv_sem=recv_sems.at[a, k - 1], device_id=_peer(k), device_id_type=MESH))
        return local, remote

    def _copy2(self, a, src, dst, sems, slot, relation, to):
        send_sems, recv_sems, _ = sems
        return pltpu.make_async_remote_copy(src_ref=src, dst_ref=dst.at[slot], send_sem=send_sems.at[a, relation - 1],
                                            recv_sem=recv_sems.at[a, relation - 1], device_id=_peer(to), device_id_type=MESH)

    def start(self, in_refs, out_refs, sems):
        if self.mode == GATHER2:
            me = _my_index()
            for a, (src, dst) in enumerate(zip(in_refs, out_refs)):
                pltpu.make_async_copy(src, dst.at[me], sems[2].at[a]).start()
                for k in (SIBLING,) + OTHER_CHIPS:
                    self._copy2(a, src, dst, sems, me, k, k).start()
            return
        local, remote = self._copies(in_refs, out_refs, *sems, landing=False)
        for cp in local + remote:
            cp.start()

    def forward(self, in_refs, out_refs, sems):
        for a, (src, dst) in enumerate(zip(in_refs, out_refs)):
            for r in OTHER_CHIPS:
                pj = _peer_index(r)
                self._copy2(a, src, dst, sems, pj, r, r).wait_recv()
                self._copy2(a, dst.at[pj], dst, sems, pj, r ^ SIBLING, SIBLING).start()

    def wait(self, in_refs, out_refs, sems):
        if self.mode == GATHER2:
            me = _my_index()
            for a, (src, dst) in enumerate(zip(in_refs, out_refs)):
                for k in (SIBLING,) + OTHER_CHIPS:
                    self._copy2(a, src, dst, sems, me, k, k).wait_send()
                self._copy2(a, src, dst, sems, _peer_index(SIBLING), SIBLING, SIBLING).wait_recv()
                for r in OTHER_CHIPS:
                    passed = self._copy2(a, src, dst, sems, _peer_index(r ^ SIBLING), r ^ SIBLING, SIBLING)
                    passed.wait_send()
                    passed.wait_recv()
                pltpu.make_async_copy(src, dst.at[me], sems[2].at[a]).wait()
            return
        local, remote = self._copies(in_refs, out_refs, *sems, landing=True)
        for cp in remote:
            cp.wait_send()
            cp.wait_recv()
        for cp in local:
            cp.wait()

    def ride(self, refs, n_in, n_out, first, mid, last):
        refs = list(refs)
        n = self.n
        x_in = refs[n_in:n_in + n]
        x_out = refs[n_in + n + n_out:n_in + 2 * n + n_out]
        sems = refs[n_in + 2 * n + n_out:n_in + 2 * n + n_out + 3]

        @pl.when(first)
        def _():
            self.start(x_in, x_out, sems)

        if self.mode == GATHER2:
            @pl.when(mid)
            def _():
                self.forward(x_in, x_out, sems)

        @pl.when(last)
        def _():
            self.wait(x_in, x_out, sems)

        return refs[:n_in] + refs[n_in + n:n_in + n + n_out] + refs[n_in + 2 * n + n_out + 3:]

    def call(self, name):
        n = self.n

        def body(*refs):
            ins, outs, sems = refs[:n], refs[n:2 * n], refs[2 * n:]
            self.start(ins, outs, sems)
            if self.mode == GATHER2:
                self.forward(ins, outs, sems)
            self.wait(ins, outs, sems)

        return pl.pallas_call(body, name=name, in_specs=self.specs(), out_specs=self.specs(), out_shape=self.out_shape(),
                              scratch_shapes=self.scratch())(*self.arrays)


def _all_gather(v, name):
    return _Exchange(GATHER, [v]).call(name)[0]


def _hosted(kernel_body, ex, n_in, n_out, first, last, grid):
    if ex is None:
        return kernel_body

    def body(*refs):
        mid = pl.program_id(0) == (2 * grid[0]) // 3 if len(grid) == 1 else None
        kernel_body(*ex.ride(refs, n_in, n_out, first(), mid, last()))

    return body


def _host_call(kernel_body, ex, first, last, name, grid, in_specs, out_specs, out_shape, scratch_shapes, sem, args):
    n_in, n_out = len(in_specs), len(out_specs)
    if ex is None:
        outs = pl.pallas_call(kernel_body, name=name, grid=grid, in_specs=in_specs, out_specs=out_specs, out_shape=out_shape,
                              scratch_shapes=scratch_shapes, compiler_params=_cp(*sem))(*args)
        return list(outs), []
    outs = pl.pallas_call(
        _hosted(kernel_body, ex, n_in, n_out, first, last, grid), name=name, grid=grid,
        in_specs=list(in_specs) + ex.specs(), out_specs=list(out_specs) + ex.specs(),
        out_shape=list(out_shape) + ex.out_shape(), scratch_shapes=ex.scratch() + list(scratch_shapes),
        compiler_params=_cp(*sem))(*args, *ex.arrays)
    return list(outs[:n_out]), list(outs[n_out:])


def _mod_fwd(call, mod_w, bias, name):
    nl, dm, n = mod_w.shape

    def body(c_ref, w_ref, b_ref, o_ref):
        cv = c_ref[...]
        cond = _bf(cv * _sig(cv))
        for layer in range(nl):
            o_ref[layer] = _nn(cond, _bf(w_ref[layer])) + b_ref[layer]

    return pl.pallas_call(
        body, name=name, out_shape=jax.ShapeDtypeStruct((nl, call.shape[0], n), F32),
        compiler_params=pltpu.CompilerParams(vmem_limit_bytes=VMEM_LIMIT),
    )(call, mod_w, bias)


def _mod_bwd(call, dm_all, mod_w, name):
    nl, dm, n = mod_w.shape

    def body(c_ref, d_ref, w_ref, gw_ref, dc_ref):
        cv = c_ref[...]
        cond = _bf(cv * _sig(cv))
        dc = jnp.zeros(cv.shape, F32)
        for layer in range(nl):
            db = _bf(d_ref[layer])
            gw_ref[layer] = _tn(cond, db)
            dc = dc + _nt(db, _bf(w_ref[layer]))
        dc_ref[...] = dc

    return pl.pallas_call(
        body, name=name,
        out_shape=[jax.ShapeDtypeStruct(mod_w.shape, F32), jax.ShapeDtypeStruct(call.shape, F32)],
        compiler_params=pltpu.CompilerParams(vmem_limit_bytes=VMEM_LIMIT),
    )(call, dm_all, mod_w)


def _sum_parts(g, name):
    def body(g_ref, o_ref):
        acc = g_ref[0]
        for j in range(1, g.shape[0]):
            acc = acc + g_ref[j]
        o_ref[...] = acc

    return pl.pallas_call(body, name=name, out_shape=jax.ShapeDtypeStruct(g.shape[1:], g.dtype))(g)


def _small_finish(dcond_g, c_ctx, dlb, lbraw, dm_ctx, dm_lat, name):
    def body(dc_ref, c_ref, dlb_ref, lb_ref, mc_ref, ml_ref, gc_ref, glb_ref, gb_ref):
        acc = dc_ref[0, 0:1, :]
        for j in range(1, N_DEV):
            acc = acc + dc_ref[j, 0:1, :]
        cv = c_ref[...]
        s = _sig(cv)
        gc_ref[...] = acc * (s * (1.0 + cv * (1.0 - s)))
        lb = _lower_bound(lb_ref)
        d0 = dlb_ref[...] * lb * (1.0 - lb)
        glb_ref[0:1, :] = d0
        glb_ref[1:2, :] = -d0
        gb_ref[...] = mc_ref[...] + ml_ref[...]

    return pl.pallas_call(
        body, name=name,
        out_shape=[jax.ShapeDtypeStruct(c_ctx.shape, F32), jax.ShapeDtypeStruct(lbraw.shape, F32),
                   jax.ShapeDtypeStruct(dm_ctx.shape, F32)],
    )(dcond_g, c_ctx, dlb, lbraw, dm_ctx, dm_lat)


def _row_tile(r, cap, mult):
    best = r
    for t in range(mult, min(r, cap) + 1, mult):
        if r % t == 0:
            best = t
    return best


def _adam(g_list, w, m, v, name, ex=None):
    nl, r, cdim = w.shape
    p = g_list[0].shape[0]
    tr = _row_tile(r, 128, 16)
    ni = r // tr

    def body(*refs):
        g_refs = refs[:nl]
        w_ref, m_ref, v_ref, go_ref, d_ref, mo_ref, vo_ref = refs[nl:]
        layer = pl.program_id(0)

        def total(g_ref):
            acc = g_ref[0].astype(F32)
            for j in range(1, p):
                acc = acc + g_ref[j].astype(F32)
            return acc

        g = total(g_refs[0])
        for k in range(1, nl):
            g = jnp.where(layer == k, total(g_refs[k]), g)
        m2 = ADAM_B1 * m_ref[0] + (1.0 - ADAM_B1) * g
        v2 = ADAM_B2 * v_ref[0] + (1.0 - ADAM_B2) * (g * g)
        m_hat = m2 / (1.0 - ADAM_B1 ** ADAM_STEP)
        v_hat = v2 / (1.0 - ADAM_B2 ** ADAM_STEP)
        go_ref[0] = g
        d_ref[0] = -ADAM_LR * (m_hat / (jnp.sqrt(v_hat) + ADAM_EPS) + ADAM_WD * w_ref[0])
        mo_ref[0] = m2
        vo_ref[0] = v2

    def g_spec(k):
        return pl.BlockSpec((p, tr, cdim), lambda la, i: (0, jnp.where(la == k, i, jnp.where(la < k, 0, ni - 1)), 0))

    spec = pl.BlockSpec((1, tr, cdim), lambda la, i: (la, i, 0))
    return _host_call(
        body, ex, lambda: (pl.program_id(0) == 0) & (pl.program_id(1) == 0),
        lambda: (pl.program_id(0) == nl - 1) & (pl.program_id(1) == ni - 1),
        name=name, grid=(nl, ni),
        in_specs=[g_spec(k) for k in range(nl)] + [spec, spec, spec],
        out_specs=[spec] * 4, out_shape=[jax.ShapeDtypeStruct((nl, r, cdim), F32)] * 4,
        scratch_shapes=[], sem=("arbitrary", "arbitrary"), args=(*g_list, w, m, v))


def _f32_as_rows(a, width):
    return lax.bitcast_convert_type(a.reshape(-1), BF16).reshape(-1, width)


def _rows_as_f32(rows):
    return lax.bitcast_convert_type(rows.reshape(rows.shape[:-2] + (-1, 2)), F32)


def _pad_rows(a, mult):
    r = (-a.shape[-2]) % mult
    if r == 0:
        return a
    widths = [(0, 0)] * (a.ndim - 2) + [(0, r), (0, 0)]
    return jnp.pad(a, widths)


def _pack_flat(parts, lane):
    flat = jnp.concatenate([p.reshape(-1).astype(F32) for p in parts])
    n = flat.shape[0]
    rows = -(-n // lane)
    rows += (-rows) % 8
    return jnp.pad(flat, (0, rows * lane - n)).reshape(rows, lane)


def _unpack_flat(packed, shapes):
    flat = packed.reshape(-1)
    out, off = [], 0
    for s in shapes:
        n = math.prod(s)
        out.append(flat[off:off + n].reshape(s))
        off += n
    return out


def kernel(x, c, ctx, c_ctx, mod_w, mod_b, norm_g, ffn_w_in, ffn_w_out, even_w_in, even_w_out, attn_qk_norm_g, attn_sink, hgrn_out_norm_g, hgrn_lb, odd_w_in, odd_w_out, loss_target, m_c_ctx, m_mod_w, m_mod_b, m_norm_g, m_ffn_w_in, m_ffn_w_out, m_even_w_in, m_even_w_out, m_attn_qk_norm_g, m_attn_sink, m_hgrn_out_norm_g, m_hgrn_lb, m_odd_w_in, m_odd_w_out, v_c_ctx, v_mod_w, v_mod_b, v_norm_g, v_ffn_w_in, v_ffn_w_out, v_even_w_in, v_even_w_out, v_attn_qk_norm_g, v_attn_sink, v_hgrn_out_norm_g, v_hgrn_lb, v_odd_w_in, v_odd_w_out):
    me = _my_index()
    lc, dm = ctx.shape[1], x.shape[2]
    nmod = mod_w.shape[2]

    extra = _pad_rows(jnp.concatenate([_f32_as_rows(c, dm), _f32_as_rows(norm_g, dm)], axis=0), 16)
    shards = {"ffn_in0": ffn_w_in[0], "ffn_in1": ffn_w_in[1], "ffn_out0": ffn_w_out[0], "ffn_out1": ffn_w_out[1],
              "even_in": even_w_in[0], "even_out": even_w_out[0], "odd_in": odd_w_in[0], "odd_out": odd_w_out[0]}
    shards = {n: a.astype(BF16) for n, a in shards.items()}
    first = _Exchange(GATHER2, [shards["even_in"], extra]).call("gather_first")
    w = {"even_in": _full_weight("even_in", first[0])}
    c_all = _rows_as_f32(first[1][:, 0:2])
    norm_g_all = _rows_as_f32(first[1][:, 2:3]).reshape(N_DEV, 2, 2, -1)
    norm_g_full = norm_g_all.transpose(1, 2, 0, 3).reshape(2, 2, dm)

    call = jnp.concatenate([c_all, c_ctx[None, :], jnp.zeros((16 - N_DEV - 1, dm), F32)], axis=0)
    bias = lax.dynamic_slice_in_dim(mod_b, me * nmod, nmod, axis=1)[:, None, :]
    m_sh = _mod_fwd(call, mod_w, bias, "mod_fwd")
    m_g = _all_gather(m_sh.reshape(-1, nmod), "gather_mod").reshape(N_DEV, 2, 16, nmod)
    m_all = m_g.transpose(1, 2, 0, 3).reshape(2, 16, -1)
    m_lat = lax.dynamic_index_in_dim(m_all, me, axis=1, keepdims=False)
    mv = jnp.stack([m_all[:, N_DEV], m_lat], axis=1)[:, :, None, :]

    _, dxs, gw, small = _local_step((ctx[0], x[0]), loss_target[0], mv, norm_g_full, w, attn_qk_norm_g[0], attn_sink[0],
                                    hgrn_out_norm_g, hgrn_lb, lc, shards)
    grad_x = dxs[None]

    last = _Exchange(SCATTER, [_shard_slots("even_in_b", gw["even_in_b"])])
    big_g = [[gw["ffn_in0"], gw["ffn_in1"]], [gw["ffn_out0"], gw["ffn_out1"]], None, [gw["even_out"]],
             [gw["odd_in"]], [gw["odd_out"]]]
    halves = (2, even_w_in.shape[1] // 2, even_w_in.shape[2])
    big_w = (ffn_w_in, ffn_w_out, even_w_in.reshape(halves), even_w_out, odd_w_in, odd_w_out)
    big_m = (m_ffn_w_in, m_ffn_w_out, m_even_w_in.reshape(halves), m_even_w_out, m_odd_w_in, m_odd_w_out)
    big_v = (v_ffn_w_in, v_ffn_w_out, v_even_w_in.reshape(halves), v_even_w_out, v_odd_w_in, v_odd_w_out)
    big_names = ("ffn_w_in", "ffn_w_out", "even_w_in", "even_w_out", "odd_w_in", "odd_w_out")
    big_out = [None] * 6

    def adam_big(i, ex=None):
        big_out[i], got = _adam(big_g[i], big_w[i], big_m[i], big_v[i], "adam_" + big_names[i], ex)
        return got

    dmv = small["dmv"]
    small_shapes = [(2, 6 * dm), (2, 6 * dm), (2, 2, dm), (2, HEAD_DIM), (ATTN_HEADS,), (HG_D,), (HG_HEADS * HG_D,), (1,)]
    vec = _pack_flat([dmv[:, 0, 0], dmv[:, 1, 0], small["norm_g"], small["qk_g"], small["sink"], small["hg_out_g"],
                      small["lb"], small["loss"]], 128)
    big_g[2] = [gw["even_in_a"], adam_big(0, last)[0]]
    vec_g = adam_big(1, _Exchange(GATHER, [vec]))[0]
    tot = _unpack_flat(_sum_parts(vec_g, "sum_small"), small_shapes)
    dm_ctx_tot, dm_lat_tot, g_norm_full, g_qk, g_sink, g_hg, dlb_tot, loss_tot = tot
    dm_lat_each = vec_g.reshape(N_DEV, -1)[:, 12 * dm:24 * dm].reshape(N_DEV, 2, 6 * dm)
    dm_lat_mine = lax.dynamic_slice_in_dim(dm_lat_each, me * nmod, nmod, axis=2).transpose(1, 0, 2)
    dm_ctx_mine = lax.dynamic_slice_in_dim(dm_ctx_tot, me * nmod, nmod, axis=1)[:, None, :]
    dm_all = jnp.concatenate([dm_lat_mine, dm_ctx_mine, jnp.zeros((2, 16 - N_DEV - 1, nmod), F32)], axis=1)
    g_mod_w, dcond = _mod_bwd(call, dm_all, mod_w, "mod_bwd")
    dcond_g = adam_big(4, _Exchange(GATHER, [dcond[N_DEV:]]))[0]
    g_c_ctx, g_lb, g_mod_b = _small_finish(dcond_g, c_ctx[None, :], dlb_tot[None, :], hgrn_lb, dm_ctx_tot, dm_lat_tot,
                                           "small_finish")
    g_norm = lax.dynamic_slice_in_dim(g_norm_full, me * norm_g.shape[2], norm_g.shape[2], axis=2)
    for i in (3, 5, 2):
        adam_big(i)
    big_out[2] = [o.reshape(even_w_in.shape) for o in big_out[2]]
    big_res = [[big_out[i][k] for i in range(6)] for k in range(4)]

    mod_res, _ = _adam([g_mod_w[0][None], g_mod_w[1][None]], mod_w, m_mod_w, v_mod_w, "adam_mod_w")

    sm_w = (c_ctx, mod_b, norm_g, attn_qk_norm_g, attn_sink, hgrn_out_norm_g, hgrn_lb)
    sm_m = (m_c_ctx, m_mod_b, m_norm_g, m_attn_qk_norm_g, m_attn_sink, m_hgrn_out_norm_g, m_hgrn_lb)
    sm_v = (v_c_ctx, v_mod_b, v_norm_g, v_attn_qk_norm_g, v_attn_sink, v_hgrn_out_norm_g, v_hgrn_lb)
    sm_g = (g_c_ctx, g_mod_b, g_norm, g_qk, g_sink, g_hg, g_lb)
    sm_shapes = [a.shape for a in sm_w]
    sm_out, _ = _adam([_pack_flat(sm_g, 128)[None]], _pack_flat(sm_w, 128)[None], _pack_flat(sm_m, 128)[None],
                      _pack_flat(sm_v, 128)[None], "adam_small")
    sm_res = [_unpack_flat(o, sm_shapes) for o in sm_out]

    def ordered(k):
        s, b = sm_res[k], big_res[k]
        return [s[0], mod_res[k], s[1], s[2], b[0], b[1], b[2], b[3], s[3], s[4], s[5], s[6], b[4], b[5]]

    return (loss_tot[0], grad_x, *ordered(0), *ordered(1), *ordered(2), *ordered(3))
```

```python
import functools
import math

import jax
import jax.numpy as jnp
import numpy as np
from jax import lax
from jax.experimental import pallas as pl
from jax.experimental.pallas import tpu as pltpu

F32 = jnp.float32
BF16 = jnp.bfloat16
EPS = 1e-6
N_DEV = 8
MESH = pl.DeviceIdType.MESH

HEAD_DIM = 64
ATTN_HEADS = 8
ATTN_KV = 2
ATTN_BLOCK = 128
WINDOW = 128
GRID_W = 64
HG_HEADS = 4
HG_D = 128
HG_CHUNK = 64
HG_STEP_CHUNKS = 4
RET_HEADS = 4
RET_DK = 256
RET_DV = 512
RET_CHUNK = 256
NEG = -1e30

ADAM_LR = 0.001
ADAM_B1 = 0.9
ADAM_B2 = 0.999
ADAM_EPS = 1e-08
ADAM_WD = 0.01
ADAM_STEP = 10

VMEM_LIMIT = 60 * 1024 * 1024
MXU_WIDTH = 256


def _hidden_chunks(fh, parts=2):
    step = -(-(fh // parts) // MXU_WIDTH) * MXU_WIDTH
    cuts = list(range(0, fh, step)) + [fh]
    return list(zip(cuts[:-1], cuts[1:]))


def _cp(*sem):
    return pltpu.CompilerParams(dimension_semantics=sem, vmem_limit_bytes=VMEM_LIMIT)


def _nn(a, b):
    return jnp.dot(a, b, preferred_element_type=F32)


def _nt(a, b):
    return lax.dot_general(a, b, (((1,), (1,)), ((), ())), preferred_element_type=F32)


def _tn(a, b):
    return lax.dot_general(a, b, (((0,), (0,)), ((), ())), preferred_element_type=F32)


ACT = BF16


def _bf(a):
    return a.astype(ACT)


def _sig(x):
    return jax.nn.sigmoid(x)


def _split3(x):
    h = x.astype(BF16)
    r = x - h.astype(F32)
    m = r.astype(BF16)
    lo = (r - m.astype(F32)).astype(BF16)
    return h, m, lo


def _nn3(m01, x):
    h, m, lo = _split3(x)
    return _nn(m01, h) + _nn(m01, m) + _nn(m01, lo)


def _nn3r(x, m01):
    h, m, lo = _split3(x)
    return _nn(h, m01) + _nn(m, m01) + _nn(lo, m01)


def _full(shape):
    nd = len(shape)
    return pl.BlockSpec(shape, lambda *a: (0,) * nd, pipeline_mode=pl.Buffered(1))


def _whole(shape):
    nd = len(shape)
    return pl.BlockSpec(shape, lambda *a: (0,) * nd)


def _rows(tm, width):
    return pl.BlockSpec((tm, width), lambda i: (i, 0))


def _cols(height, tm):
    return pl.BlockSpec((height, tm), lambda i: (0, i))


def _ctx_lat(width):
    return pl.BlockSpec((1, 1, width), lambda i: (jnp.minimum(i, 1), 0, 0))


def _acc_ctx_lat(ref, i, val):
    @pl.when(i <= 1)
    def _():
        ref[...] = val.reshape(ref.shape)

    @pl.when(i > 1)
    def _():
        ref[...] += val.reshape(ref.shape)


def _acc_all(ref, i, val):
    @pl.when(i == 0)
    def _():
        ref[...] = val.reshape(ref.shape)

    @pl.when(i > 0)
    def _():
        ref[...] += val.reshape(ref.shape)


def _tile(n, cap):
    best = None
    for t in range(128, min(n, cap) + 1, 128):
        if n % t == 0:
            best = t
    return n if best is None else best


def _norm_mod(xv, g, shift, scale):
    r = lax.rsqrt(jnp.mean(xv * xv, axis=-1, keepdims=True) + EPS)
    xhat = xv * r
    n = xhat * g
    return r, xhat, n, n * (1.0 + scale) + shift


def _norm_mod_bwd(dh, r, xhat, n, g, scale):
    dshift = jnp.sum(dh, axis=0, keepdims=True)
    dscale = jnp.sum(dh * n, axis=0, keepdims=True)
    dn = dh * (1.0 + scale)
    dg = jnp.sum(dn * xhat, axis=0, keepdims=True)
    dxh = dn * g
    dx = r * (dxh - xhat * jnp.mean(dxh * xhat, axis=-1, keepdims=True))
    return dx, dshift, dscale, dg


def _stream(x):
    if isinstance(x, tuple):
        return list(x), x[0].shape[0] + x[1].shape[0], x[0].shape[1]
    return [x], x.shape[0], x.shape[1]


def _stream_specs(x, tm, dm):
    if isinstance(x, tuple):
        return [pl.BlockSpec((tm, dm), lambda i: (0, 0)), pl.BlockSpec((tm, dm), lambda i: (jnp.maximum(i - 1, 0), 0))]
    return [_rows(tm, dm)]


def _stream_tile(refs):
    if len(refs) == 2:
        return jnp.where(pl.program_id(0) == 0, refs[0][...], refs[1][...])
    return refs[0][...]


def _pre_fwd(x, gain, ms, w, splits, tm, name, ex=None, out_dtype=F32, qk=None):
    xs, T, dm = _stream(x)
    nx = len(xs)
    nt = T // tm
    nq = 0 if qk is None else 3
    ns = len(splits)

    def body(*refs):
        g_ref, ms_ref, w_ref = refs[nx:nx + 3]
        outs = refs[nx + 3 + nq:]
        ms_v = ms_ref[0]
        h = _norm_mod(_stream_tile(refs[:nx]), g_ref[...], ms_v[:, :dm], ms_v[:, dm:])[3]
        hb = _bf(h)
        for k, ((s, e), o_ref) in enumerate(zip(splits, outs[:ns])):
            part = _nn(hb, w_ref[:, s:e])
            o_ref[...] = part.astype(o_ref.dtype)
            if k == 0 and qk is not None:
                gq_ref, c_ref, s_ref = refs[nx + 3:nx + 6]
                _qk_tile_fwd(part, gq_ref, c_ref[...], s_ref[...], *outs[ns:])

    in_specs = _stream_specs(x, tm, dm) + [_full((1, dm)), _ctx_lat(2 * dm), _full(w.shape)]
    out_specs = [_rows(tm, e - s) for s, e in splits]
    out_shape = [jax.ShapeDtypeStruct((T, e - s), out_dtype) for s, e in splits]
    args = [*xs, gain, ms, w]
    if qk is not None:
        qw = ATTN_HEADS * HEAD_DIM
        in_specs += [_full(qk[0].shape), _rows(tm, PAIR), _rows(tm, PAIR)]
        args += list(qk)
        out_specs += [_rows(tm, qw), _rows(tm, PAIR), _rows(tm, PAIR)]
        out_shape += [jax.ShapeDtypeStruct((T, qw), ACT), jax.ShapeDtypeStruct((T, PAIR), ACT), jax.ShapeDtypeStruct((T, PAIR), ACT)]
    return _host_call(
        body, ex, lambda: pl.program_id(0) == 0, lambda: pl.program_id(0) == nt - 1,
        name=name, grid=(nt,), in_specs=in_specs, out_specs=out_specs, out_shape=out_shape,
        scratch_shapes=[], sem=("arbitrary",), args=tuple(args))


def _pre_bwd(x, dx_in, gain, ms, w, pieces, tm, name, latent_dx=False, ex=None, qk=None):
    xs, T, dm = _stream(x)
    nx = len(xs)
    dx_spec = pl.BlockSpec((tm, dm), lambda i: (jnp.maximum(i - 1, 0), 0)) if latent_dx else _rows(tm, dm)
    dx_rows = T - tm if latent_dx else T
    n_out = w.shape[1]
    flat = [a for _, arrs in pieces for a in arrs]
    nq = 0 if qk is None else 6
    qkw = (ATTN_HEADS + ATTN_KV) * HEAD_DIM

    def body(*refs):
        dxin_ref, g_ref, ms_ref, w_ref = refs[nx:nx + 4]
        rest = refs[nx + 4:]
        p_refs = rest[:len(flat)]
        qk_refs = rest[len(flat):len(flat) + nq]
        dx_ref, h_ref, dp_ref, dms_ref, dg_ref = rest[len(flat) + nq:len(flat) + nq + 5]
        i = pl.program_id(0)
        ms_v = ms_ref[0]
        g = g_ref[...]
        scale = ms_v[:, dm:]
        r, xhat, n, h = _norm_mod(_stream_tile(refs[:nx]), g, ms_v[:, :dm], scale)
        h_ref[...] = _bf(h).T
        dh = jnp.zeros((tm, dm), F32)
        if qk is not None:
            dq_ref, dk_ref, pa_ref, gq_ref, c_ref, s_ref = qk_refs
            dqk, dgs = _qk_tile_bwd(dq_ref, dk_ref, pa_ref, gq_ref, c_ref[...], s_ref[...])
            dgq_ref = rest[len(flat) + nq + 5]
            for p, dgp in enumerate(dgs):
                _acc_all(dgq_ref.at[p], i, dgp)
            vb = _bf(dqk)
            dp_ref[:, :qkw] = vb
            dh = dh + _nt(vb, w_ref[:, :qkw])
        k = 0
        for s, arrs in pieces:
            v = p_refs[k][...].astype(F32)
            for j in range(1, len(arrs)):
                v = v + p_refs[k + j][...].astype(F32)
            k += len(arrs)
            vb = _bf(v)
            wd = vb.shape[1]
            dp_ref[:, s:s + wd] = vb
            dh = dh + _nt(vb, w_ref[:, s:s + wd])
        dx, dshift, dscale, dg = _norm_mod_bwd(dh, r, xhat, n, g, scale)
        dx_ref[...] = dxin_ref[...] + dx
        _acc_ctx_lat(dms_ref, i, jnp.concatenate([dshift, dscale], axis=1))
        _acc_all(dg_ref, i, dg)

    nt = T // tm
    in_specs = (_stream_specs(x, tm, dm) + [_rows(tm, dm), _full((1, dm)), _ctx_lat(2 * dm), _full(w.shape)]
                + [_rows(tm, a.shape[1]) for a in flat])
    out_specs = [dx_spec, _cols(dm, tm), _rows(tm, n_out), _ctx_lat(2 * dm), _whole((1, dm))]
    out_shape = [jax.ShapeDtypeStruct((dx_rows, dm), F32), jax.ShapeDtypeStruct((dm, T), ACT),
                 jax.ShapeDtypeStruct((T, n_out), ACT), jax.ShapeDtypeStruct((2, 1, 2 * dm), F32),
                 jax.ShapeDtypeStruct((1, dm), F32)]
    args = [*xs, dx_in, gain, ms, w, *flat]
    if qk is not None:
        dq, dk, pa, gains, cosp, sinp = qk
        in_specs += [_rows(tm, dq.shape[1]), _rows(tm, PAIR), _rows(tm, qkw), _full(gains.shape), _rows(tm, PAIR), _rows(tm, PAIR)]
        args += [dq, dk, pa, gains, cosp, sinp]
        out_specs.append(_whole(gains.shape))
        out_shape.append(jax.ShapeDtypeStruct(gains.shape, F32))
    return _host_call(
        body, ex, lambda: pl.program_id(0) == 0, lambda: pl.program_id(0) == nt - 1,
        name=name, grid=(nt,), in_specs=in_specs, out_specs=out_specs, out_shape=out_shape,
        scratch_shapes=[], sem=("arbitrary",), args=tuple(args))


def _ffn_fwd(x1, gain, ms, w_in, w_out, tm, name, target=None, ex=None):
    T, dm = x1.shape
    fh = w_out.shape[0]
    head = target is not None

    def body(*refs):
        if head:
            x_ref, g_ref, ms_ref, wi_ref, wo_ref, t_ref, x2_ref, u_ref, f_ref, loss_ref = refs
        else:
            x_ref, g_ref, ms_ref, wi_ref, wo_ref, x2_ref, u_ref, f_ref = refs
        ms_v = ms_ref[0]
        xv = x_ref[...]
        hb = _bf(_norm_mod(xv, g_ref[...], ms_v[:, :dm], ms_v[:, dm:2 * dm])[3])
        f = jnp.zeros((tm, dm), F32)
        for c0, c1 in _hidden_chunks(fh):
            gt = _nn(hb, wi_ref[:, c0:c1])
            up = _nn(hb, wi_ref[:, fh + c0:fh + c1])
            u_ref[:, c0:c1] = _bf(gt)
            u_ref[:, fh + c0:fh + c1] = _bf(up)
            f = f + _nn(_bf(gt * _sig(gt) * up), wo_ref[c0:c1, :])
        f_ref[...] = _bf(f)
        x2 = xv + ms_v[:, 2 * dm:] * f
        if head:
            i = pl.program_id(0)
            e = x2 - t_ref[...]
            x2_ref[...] = jnp.where(i > 0, e * (1.0 / dm), 0.0)
            _acc_all(loss_ref, i, jnp.where(i > 0, jnp.sum(e * e) * (0.5 / dm), 0.0))
        else:
            x2_ref[...] = x2

    ins = [x1, gain, ms, w_in, w_out]
    in_specs = [_rows(tm, dm), _full((1, dm)), _ctx_lat(3 * dm), _full(w_in.shape), _full(w_out.shape)]
    out_specs = [_rows(tm, dm), _rows(tm, 2 * fh), _rows(tm, dm)]
    out_shape = [jax.ShapeDtypeStruct((T, dm), F32), jax.ShapeDtypeStruct((T, 2 * fh), ACT), jax.ShapeDtypeStruct((T, dm), ACT)]
    if head:
        ins.append(target)
        in_specs.append(pl.BlockSpec((tm, dm), lambda i: (jnp.maximum(i - 1, 0), 0)))
        out_specs.append(_whole((1, 1)))
        out_shape.append(jax.ShapeDtypeStruct((1, 1), F32))
    nt = T // tm
    return _host_call(
        body, ex, lambda: pl.program_id(0) == 0, lambda: pl.program_id(0) == nt - 1,
        name=name, grid=(nt,), in_specs=in_specs, out_specs=out_specs, out_shape=out_shape,
        scratch_shapes=[], sem=("arbitrary",), args=tuple(ins))


def _ffn_bwd(x1, dx2, u, f, gain, ms, w_in, w_out, tm, name, ex=None):
    T, dm = x1.shape
    fh = w_out.shape[0]

    def body(x_ref, dx2_ref, u_ref, f_ref, g_ref, ms_ref, wi_ref, wo_ref,
             dx1_ref, h_ref, du_ref, act_ref, df_ref, dms_ref, dg_ref):
        i = pl.program_id(0)
        ms_v = ms_ref[0]
        g = g_ref[...]
        scale = ms_v[:, dm:2 * dm]
        gate = ms_v[:, 2 * dm:]
        r, xhat, n, h = _norm_mod(x_ref[...], g, ms_v[:, :dm], scale)
        h_ref[...] = _bf(h).T
        dx2 = dx2_ref[...]
        dgate = jnp.sum(dx2 * f_ref[...].astype(F32), axis=0, keepdims=True)
        dfb = _bf(dx2 * gate)
        df_ref[...] = dfb
        dh = jnp.zeros((tm, dm), F32)
        for c0, c1 in _hidden_chunks(fh, 1):
            da = _nt(dfb, wo_ref[c0:c1, :])
            gt = u_ref[:, c0:c1].astype(F32)
            up = u_ref[:, fh + c0:fh + c1].astype(F32)
            s = _sig(gt)
            sg = gt * s
            act_ref[c0:c1, :] = _bf(sg * up).T
            dgt = _bf(da * up * (s * (1.0 + gt * (1.0 - s))))
            dup = _bf(da * sg)
            du_ref[:, c0:c1] = dgt
            du_ref[:, fh + c0:fh + c1] = dup
            dh = dh + _nt(dgt, wi_ref[:, c0:c1]) + _nt(dup, wi_ref[:, fh + c0:fh + c1])
        dx, dshift, dscale, dg = _norm_mod_bwd(dh, r, xhat, n, g, scale)
        dx1_ref[...] = dx2 + dx
        _acc_ctx_lat(dms_ref, i, jnp.concatenate([dshift, dscale, dgate], axis=1))
        _acc_all(dg_ref, i, dg)

    nt = T // tm
    return _host_call(
        body, ex, lambda: pl.program_id(0) == 0, lambda: pl.program_id(0) == nt - 1,
        name=name, grid=(nt,),
        in_specs=[_rows(tm, dm), _rows(tm, dm), _rows(tm, 2 * fh), _rows(tm, dm), _full((1, dm)), _ctx_lat(3 * dm),
                  _full(w_in.shape), _full(w_out.shape)],
        out_specs=[_rows(tm, dm), _cols(dm, tm), _rows(tm, 2 * fh), _cols(fh, tm), _rows(tm, dm),
                   _ctx_lat(3 * dm), _whole((1, dm))],
        out_shape=[jax.ShapeDtypeStruct((T, dm), F32), jax.ShapeDtypeStruct((dm, T), ACT),
                   jax.ShapeDtypeStruct((T, 2 * fh), ACT), jax.ShapeDtypeStruct((fh, T), ACT),
                   jax.ShapeDtypeStruct((T, dm), ACT), jax.ShapeDtypeStruct((2, 1, 3 * dm), F32),
                   jax.ShapeDtypeStruct((1, dm), F32)],
        scratch_shapes=[], sem=("arbitrary",), args=(x1, dx2, u, f, gain, ms, w_in, w_out))


def _wgrad(a_t, b, name, rows=None, ex=None):
    T = a_t.shape[1]
    r0, K = (0, a_t.shape[0]) if rows is None else rows
    N = b.shape[1]
    tk, tn, tt = _tile(K, 1408), _tile(N, 1664), _tile(T, 2816)
    nt = T // tt
    assert r0 % tk == 0
    off = r0 // tk
    nk, nn = K // tk, N // tn

    def body(a_ref, b_ref, o_ref, acc_ref):
        t = pl.program_id(2)
        part = _nn(a_ref[...], b_ref[...])

        @pl.when(t == 0)
        def _():
            acc_ref[...] = part

        @pl.when(t > 0)
        def _():
            acc_ref[...] += part

        @pl.when(t == nt - 1)
        def _():
            o_ref[...] = acc_ref[...].astype(o_ref.dtype)

    def at(i, j, t):
        return (pl.program_id(0) == i) & (pl.program_id(1) == j) & (pl.program_id(2) == t)

    outs, got = _host_call(
        body, ex, lambda: at(0, 0, 0), lambda: at(nk - 1, nn - 1, nt - 1),
        name=name, grid=(nk, nn, nt),
        in_specs=[pl.BlockSpec((tk, tt), lambda i, j, t: (i + off, t)), pl.BlockSpec((tt, tn), lambda i, j, t: (t, j))],
        out_specs=[pl.BlockSpec((tk, tn), lambda i, j, t: (i, j))],
        out_shape=[jax.ShapeDtypeStruct((K, N), ACT)],
        scratch_shapes=[pltpu.VMEM((tk, tn), F32)], sem=("arbitrary", "arbitrary", "arbitrary"), args=(a_t, b))
    return outs[0] if ex is None else (outs[0], got)


def _post_fwd(x, o_fw, o_bw, g_src, g_blk, gain, a, w_out, ms, dvh, tm, name):
    xs, T, dm = _stream(x)
    nx = len(xs)
    hv = o_fw.shape[1]
    aw = 0 if a is None else a.shape[1]
    has_gain = gain is not None

    def body(*refs):
        refs = list(refs)
        x_refs = refs[:nx]
        of_ref, ob_ref, g_ref = refs[nx:nx + 3]
        k = nx + 3
        gain_ref = a_ref = None
        if has_gain:
            gain_ref = refs[k]
            k += 1
        if aw:
            a_ref = refs[k]
            k += 1
        w_ref, ms_ref, x1_ref, z_ref, yp_ref = refs[k:k + 5]
        o = of_ref[...].astype(F32) + ob_ref[...].astype(F32)
        gr = g_ref[...].astype(F32)
        if aw:
            z_ref[:, :aw] = _bf(a_ref[...])
        for hd in range(hv // dvh):
            sl = slice(hd * dvh, (hd + 1) * dvh)
            oh = o[:, sl]
            gh = gr[:, sl]
            r = lax.rsqrt(jnp.mean(oh * oh, axis=-1, keepdims=True) + EPS)
            y = oh * r
            if has_gain:
                y = y * gain_ref[...]
            y = y * (gh * _sig(gh))
            z_ref[:, aw + hd * dvh:aw + (hd + 1) * dvh] = _bf(y)
        yp = _nn(z_ref[...], w_ref[...])
        yp_ref[...] = _bf(yp)
        x1_ref[...] = _stream_tile(x_refs) + ms_ref[0] * yp

    ins = xs + [o_fw, o_bw, g_src]
    specs = _stream_specs(x, tm, dm) + [_rows(tm, hv), _rows(tm, hv), pl.BlockSpec((tm, hv), lambda i: (i, g_blk))]
    if has_gain:
        ins.append(gain)
        specs.append(_full(gain.shape))
    if aw:
        ins.append(a)
        specs.append(_rows(tm, aw))
    ins += [w_out, ms]
    specs += [_full(w_out.shape), _ctx_lat(dm)]
    return pl.pallas_call(
        body, name=name, grid=(T // tm,), in_specs=specs,
        out_specs=[_rows(tm, dm), _rows(tm, aw + hv), _rows(tm, dm)],
        out_shape=[jax.ShapeDtypeStruct((T, dm), F32), jax.ShapeDtypeStruct((T, aw + hv), ACT),
                   jax.ShapeDtypeStruct((T, dm), ACT)],
        compiler_params=_cp("arbitrary"),
    )(*ins)


def _post_bwd(dx1, z, yp, o_fw, o_bw, g_src, g_blk, gain, w_out, ms, aw, dvh, tm, name):
    T, dm = dx1.shape
    hv = o_fw.shape[1]
    has_gain = gain is not None

    def body(*refs):
        refs = list(refs)
        dx1_ref, z_ref, yp_ref, of_ref, ob_ref, g_ref = refs[:6]
        k = 6
        gain_ref = None
        if has_gain:
            gain_ref = refs[k]
            k += 1
        w_ref, ms_ref = refs[k:k + 2]
        k += 2
        do_ref, dgr_ref = refs[k:k + 2]
        k += 2
        da_ref = None
        if aw:
            da_ref = refs[k]
            k += 1
        dy_ref, zt_ref, dgate_ref, dgain_ref = refs[k:k + 4]
        i = pl.program_id(0)
        dx1v = dx1_ref[...]
        zt_ref[...] = z_ref[...].T
        _acc_ctx_lat(dgate_ref, i, jnp.sum(dx1v * yp_ref[...].astype(F32), axis=0, keepdims=True))
        dyb = _bf(dx1v * ms_ref[0])
        dy_ref[...] = dyb
        dz = _nt(dyb, w_ref[...])
        if aw:
            da_ref[...] = dz[:, :aw]
        o = of_ref[...].astype(F32) + ob_ref[...].astype(F32)
        gr = g_ref[...].astype(F32)
        dgain = jnp.zeros((1, dvh), F32)
        for hd in range(hv // dvh):
            sl = slice(hd * dvh, (hd + 1) * dvh)
            oh = o[:, sl]
            gh = gr[:, sl]
            dyh = dz[:, aw + hd * dvh:aw + (hd + 1) * dvh]
            r = lax.rsqrt(jnp.mean(oh * oh, axis=-1, keepdims=True) + EPS)
            n = oh * r
            s = _sig(gh)
            sl_g = gh * s
            gn = gain_ref[...] if has_gain else 1.0
            dgr_ref[:, sl] = _bf(dyh * n * gn * (s * (1.0 + gh * (1.0 - s))))
            dn = dyh * gn * sl_g
            dgain = dgain + jnp.sum(dyh * n * sl_g, axis=0, keepdims=True)
            do_ref[:, sl] = _bf(r * (dn - n * jnp.mean(dn * n, axis=-1, keepdims=True)))
        _acc_all(dgain_ref, i, dgain)

    ins = [dx1, z, yp, o_fw, o_bw, g_src]
    specs = [_rows(tm, dm), _rows(tm, aw + hv), _rows(tm, dm), _rows(tm, hv), _rows(tm, hv),
             pl.BlockSpec((tm, hv), lambda i: (i, g_blk))]
    if has_gain:
        ins.append(gain)
        specs.append(_full(gain.shape))
    ins += [w_out, ms]
    specs += [_full(w_out.shape), _ctx_lat(dm)]
    out_specs = [_rows(tm, hv), _rows(tm, hv)]
    out_shape = [jax.ShapeDtypeStruct((T, hv), ACT), jax.ShapeDtypeStruct((T, hv), ACT)]
    if aw:
        out_specs.append(_rows(tm, aw))
        out_shape.append(jax.ShapeDtypeStruct((T, aw), F32))
    out_specs += [_rows(tm, dm), _cols(aw + hv, tm), _ctx_lat(dm), _whole((1, dvh))]
    out_shape += [jax.ShapeDtypeStruct((T, dm), ACT), jax.ShapeDtypeStruct((aw + hv, T), ACT),
                  jax.ShapeDtypeStruct((2, 1, dm), F32), jax.ShapeDtypeStruct((1, dvh), F32)]
    return pl.pallas_call(
        body, name=name, grid=(T // tm,), in_specs=specs, out_specs=out_specs, out_shape=out_shape,
        compiler_params=_cp("arbitrary"),
    )(*ins)


PAIR = 2 * HEAD_DIM
N_PAIRS = (ATTN_HEADS + ATTN_KV) // 2


def _lanes():
    return lax.broadcasted_iota(jnp.int32, (1, PAIR), 1)


def _swap32(v):
    first_half = (_lanes() & (HEAD_DIM // 2)) == 0
    return jnp.where(first_half, pltpu.roll(v, PAIR - HEAD_DIM // 2, 1), pltpu.roll(v, HEAD_DIM // 2, 1))


def _head_mean(v):
    r = lax.broadcasted_iota(jnp.int32, (PAIR, PAIR), 0)
    c = lax.broadcasted_iota(jnp.int32, (PAIR, PAIR), 1)
    same = jnp.where((r >= HEAD_DIM) == (c >= HEAD_DIM), 1.0, 0.0).astype(BF16)
    return _nn3r(v, same) * (1.0 / HEAD_DIM)


def _qk_tile_fwd(pa, g_ref, cosv, sinv, q_ref, k_ref, v_ref):
    qw = ATTN_HEADS * HEAD_DIM
    for p in range(N_PAIRS):
        xv = pa[:, p * PAIR:(p + 1) * PAIR]
        n = xv * lax.rsqrt(_head_mean(xv * xv) + EPS) * g_ref[p]
        y = n * cosv + _swap32(n) * sinv
        if p < N_PAIRS - 1:
            q_ref[:, p * PAIR:(p + 1) * PAIR] = _bf(y * HEAD_DIM ** -0.5)
        else:
            k_ref[...] = _bf(y)
    v_ref[...] = _bf(pa[:, qw + PAIR:])


def _qk_tile_bwd(dq_ref, dk_ref, pa_ref, g_ref, cosv, sinv):
    dxs, dgs = [], []
    for p in range(N_PAIRS):
        sl = slice(p * PAIR, (p + 1) * PAIR)
        xv = pa_ref[:, sl]
        r = lax.rsqrt(_head_mean(xv * xv) + EPS)
        xhat = xv * r
        dy = dq_ref[:, sl] * HEAD_DIM ** -0.5 if p < N_PAIRS - 1 else dk_ref[...]
        dn = dy * cosv + _swap32(dy * sinv)
        dgs.append(jnp.sum(dn * xhat, axis=0, keepdims=True))
        dxh = dn * g_ref[p]
        dxs.append(r * (dxh - xhat * _head_mean(dxh * xhat)))
    return jnp.concatenate(dxs, axis=1), dgs


def _attn_window(ref, i, nb):
    blk = ATTN_BLOCK
    starts = [pl.multiple_of(jnp.clip(i + d, 0, nb - 1) * blk, blk) for d in (-1, 0, 1)]
    return starts, jnp.concatenate([ref[pl.ds(s, blk), :] for s in starts], axis=0)


GROUP_HEADS = 2
ATTN_STEP_BLOCKS = 6


def _head_groups(n):
    g = ATTN_HEADS // ATTN_KV
    return [(kv, [kv * g + s + j for j in range(n)]) for kv in range(ATTN_KV) for s in range(0, g, n)]


def _attn_mask(i, lc, T, rows):
    blk = ATTN_BLOCK
    row = lax.broadcasted_iota(jnp.int32, (rows, 1), 0)
    qpos = i * blk + (row & (blk - 1))
    kpos = (i - 1) * blk + lax.broadcasted_iota(jnp.int32, (1, 3 * blk), 1)
    return (qpos >= lc) & (kpos >= lc) & (kpos < T) & (jnp.abs(kpos - qpos) <= WINDOW)


def _to_kv_half(v, head, kv):
    return v if head % 2 == kv else pltpu.roll(v, HEAD_DIM, 1)


def _attn_slab_fwd(qt, ks, vs, sinkb, lc, name, ex=None):
    T = qt.shape[0]
    blk = ATTN_BLOCK
    nb = T // blk
    g = ATTN_HEADS // ATTN_KV

    spb = ATTN_STEP_BLOCKS
    ng = nb // spb

    def one_block(i, rows, q_ref, k_ref, v_ref, sink_ref, o_ref, lse_ref):
        lane = _lanes()
        valid = _attn_mask(i, lc, T, GROUP_HEADS * blk)
        kc_all, vc = k_ref[0:lc, :], v_ref[0:lc, :]
        _, kw_all = _attn_window(k_ref, i, nb)
        _, vw = _attn_window(v_ref, i, nb)
        kc, kw = [], []
        for kv in range(ATTN_KV):
            mine = (lane >= kv * HEAD_DIM) & (lane < (kv + 1) * HEAD_DIM)
            kc.append(jnp.where(mine, kc_all, jnp.zeros_like(kc_all)))
            kw.append(jnp.where(mine, kw_all, jnp.zeros_like(kw_all)))
        groups = _head_groups(GROUP_HEADS)
        qg = [jnp.concatenate([_to_kv_half(q_ref[rows, (h // 2) * PAIR:(h // 2 + 1) * PAIR], h, kv) for h in heads], axis=0)
              for kv, heads in groups]
        sinks = [sink_ref[kv, (heads[0] - kv * g) * blk:(heads[-1] + 1 - kv * g) * blk] for kv, heads in groups]
        s_c = [_nt(q, kc[kv]) for q, (kv, _) in zip(qg, groups)]
        s_w = [jnp.where(valid, _nt(q, kw[kv]), NEG) for q, (kv, _) in zip(qg, groups)]
        m = [jnp.maximum(jnp.maximum(jnp.max(a, axis=-1, keepdims=True), jnp.max(b, axis=-1, keepdims=True)), s)
             for a, b, s in zip(s_c, s_w, sinks)]
        e_c = [jnp.exp(a - mm) for a, mm in zip(s_c, m)]
        e_w = [jnp.exp(b - mm) for b, mm in zip(s_w, m)]
        den = [jnp.exp(s - mm) + jnp.sum(a, axis=-1, keepdims=True) + jnp.sum(b, axis=-1, keepdims=True)
               for s, mm, a, b in zip(sinks, m, e_c, e_w)]
        inv = [1.0 / d for d in den]
        og = [_nn(_bf(a * r), vc) + _nn(_bf(b * r), vw) for a, b, r in zip(e_c, e_w, inv)]
        placed = [None] * ATTN_HEADS
        for (kv, heads), o2, mm, d in zip(groups, og, m, den):
            lse_ref[heads[0]:heads[-1] + 1, rows, :] = (mm + jnp.log(d)).reshape(len(heads), blk, 1)
            for j, h in enumerate(heads):
                placed[h] = _to_kv_half(o2[j * blk:(j + 1) * blk], h, kv)
        for p in range(ATTN_HEADS // 2):
            o_ref[rows, p * PAIR:(p + 1) * PAIR] = jnp.where(lane < HEAD_DIM, placed[2 * p], placed[2 * p + 1])

    def body(*refs):
        for j in range(spb):
            one_block(pl.program_id(0) * spb + j, pl.ds(j * blk, blk), *refs)

    qw = ATTN_HEADS * HEAD_DIM
    return _host_call(
        body, ex, lambda: pl.program_id(0) == 0, lambda: pl.program_id(0) == ng - 1,
        name=name, grid=(ng,),
        in_specs=[_rows(spb * blk, qw), _full((T, PAIR)), _full((T, PAIR)), _full(sinkb.shape)],
        out_specs=[_rows(spb * blk, qw), pl.BlockSpec((ATTN_HEADS, spb * blk, 1), lambda i: (0, i, 0))],
        out_shape=[jax.ShapeDtypeStruct((T, qw), F32), jax.ShapeDtypeStruct((ATTN_HEADS, T, 1), F32)],
        scratch_shapes=[], sem=("arbitrary",), args=(qt, ks, vs, sinkb))


def _attn_slab_bwd(qt, ks, vs, sinkb, o, lse, do, lc, name, ex=None):
    T = qt.shape[0]
    blk = ATTN_BLOCK
    nb = T // blk
    g = ATTN_HEADS // ATTN_KV

    spb = ATTN_STEP_BLOCKS
    ng = nb // spb

    def body(*refs):
        dk_ref, dv_ref, ds_ref = refs[8:11]

        @pl.when(pl.program_id(0) == 0)
        def _():
            dk_ref[...] = jnp.zeros_like(dk_ref)
            dv_ref[...] = jnp.zeros_like(dv_ref)
            ds_ref[...] = jnp.zeros_like(ds_ref)

        for j in range(spb):
            one_block(pl.program_id(0) * spb + j, pl.ds(j * blk, blk), *refs)

    def one_block(i, rows, q_ref, k_ref, v_ref, sink_ref, o_ref, lse_ref, do_ref, dq_ref, dk_ref, dv_ref, ds_ref):
        lane = _lanes()
        valid = _attn_mask(i, lc, T, g * blk)
        kc_all, vc_all = k_ref[0:lc, :], v_ref[0:lc, :]
        starts, kw_all = _attn_window(k_ref, i, nb)
        _, vw_all = _attn_window(v_ref, i, nb)
        dq_pairs = [jnp.zeros((blk, PAIR), F32) for _ in range(ATTN_HEADS // 2)]
        for kv in range(ATTN_KV):
            mine = (lane >= kv * HEAD_DIM) & (lane < (kv + 1) * HEAD_DIM)

            def only(v):
                return jnp.where(mine, v, jnp.zeros_like(v))

            kc, kw, vc, vw = only(kc_all), only(kw_all), only(vc_all), only(vw_all)
            heads = [kv * g + j for j in range(g)]
            qs, dos, deltas = [], [], []
            for h in heads:
                sl = slice((h // 2) * PAIR, (h // 2 + 1) * PAIR)
                dov = do_ref[rows, sl]
                qs.append(_to_kv_half(q_ref[rows, sl], h, kv))
                dos.append(_bf(_to_kv_half(dov, h, kv)))
                own = (lane < HEAD_DIM) if h % 2 == 0 else (lane >= HEAD_DIM)
                deltas.append(jnp.sum(jnp.where(own, dov * o_ref[rows, sl], 0.0), axis=-1, keepdims=True))
            q4, do4, delta = jnp.concatenate(qs, axis=0), jnp.concatenate(dos, axis=0), jnp.concatenate(deltas, axis=0)
            sink = sink_ref[kv]
            lse = lse_ref[kv * g:(kv + 1) * g, rows, :].reshape(g * blk, 1)
            p_c = jnp.exp(_nt(q4, kc) - lse)
            p_w = jnp.exp(jnp.where(valid, _nt(q4, kw), NEG) - lse)
            ds_c = _bf(p_c * (_nt(do4, vc) - delta))
            ds_w = _bf(p_w * (_nt(do4, vw) - delta))
            dsr = -jnp.exp(sink - lse) * delta
            dq4 = _nn(ds_c, kc) + _nn(ds_w, kw)
            for j, h in enumerate(heads):
                ds_ref[h:h + 1, :] += jnp.sum(dsr[j * blk:(j + 1) * blk, :], axis=0, keepdims=True)
                dq_pairs[h // 2] = dq_pairs[h // 2] + _to_kv_half(dq4[j * blk:(j + 1) * blk], h, kv)
            dk_ref[0:lc, :] += only(_tn(ds_c, q4))
            dv_ref[0:lc, :] += only(_tn(_bf(p_c), do4))
            dkw = only(_tn(ds_w, q4))
            dvw = only(_tn(_bf(p_w), do4))
            for b, s in enumerate(starts):
                dk_ref[pl.ds(s, blk), :] += dkw[b * blk:(b + 1) * blk]
                dv_ref[pl.ds(s, blk), :] += dvw[b * blk:(b + 1) * blk]
        for p in range(ATTN_HEADS // 2):
            dq_ref[rows, p * PAIR:(p + 1) * PAIR] = dq_pairs[p]

    qw = ATTN_HEADS * HEAD_DIM
    lspec = pl.BlockSpec((ATTN_HEADS, spb * blk, 1), lambda i: (0, i, 0))
    return _host_call(
        body, ex, lambda: pl.program_id(0) == 0, lambda: pl.program_id(0) == ng - 1,
        name=name, grid=(ng,),
        in_specs=[_rows(spb * blk, qw), _full((T, PAIR)), _full((T, PAIR)), _full(sinkb.shape), _rows(spb * blk, qw), lspec,
                  _rows(spb * blk, qw)],
        out_specs=[_rows(spb * blk, qw), _whole((T, PAIR)), _whole((T, PAIR)), _whole((ATTN_HEADS, 1))],
        out_shape=[jax.ShapeDtypeStruct((T, qw), F32), jax.ShapeDtypeStruct((T, PAIR), F32),
                   jax.ShapeDtypeStruct((T, PAIR), F32), jax.ShapeDtypeStruct((ATTN_HEADS, 1), F32)],
        scratch_shapes=[], sem=("arbitrary",), args=(qt, ks, vs, sinkb, o, lse, do))


def _fw_chunk(s, nc, nt):
    return s


def _bw_chunk(s, nc, nt):
    return jnp.where(s < nc, nc - 1 - s, nt - 1 - (s - nc))


def _tri(c, rev):
    r = lax.broadcasted_iota(jnp.int32, (c, c), 0)
    k = lax.broadcasted_iota(jnp.int32, (c, c), 1)
    return (k >= r) if rev else (k <= r)


def _gla_gates(z, lb, rev):
    c = HG_CHUNK
    sg = _sig(z)
    f = lb + (1.0 - lb) * sg
    cum = _nn3(jnp.where(_tri(c, rev), 1.0, 0.0).astype(BF16), jnp.log(f))
    mid = c - 1 - c // 2 if rev else c // 2
    last = 0 if rev else c - 1
    return sg, f, cum, cum[mid:mid + 1], cum[last:last + 1], last


def _lower_bound(lbraw_ref):
    lr = lbraw_ref[...]
    return _sig(lr[0:1] - lr[1:2])


def _gla_fwd(pb, lbraw, lc, name, ex=None):
    T = pb.shape[0]
    c, hw, d, ns = HG_CHUNK, HG_HEADS * HG_D, HG_D, HG_STEP_CHUNKS
    nt, nc = T // (ns * c), lc // (ns * c)
    orders = (_fw_chunk, _bw_chunk)

    def body(qf, zf, vf, qb, zb, vb, lb_ref, of_ref, ob_ref, sf_ref, sb_ref, st_ref):
        @pl.when(pl.program_id(0) == 0)
        def _():
            st_ref[...] = jnp.zeros_like(st_ref)

        lb = _lower_bound(lb_ref)
        dirs = ((qf, zf, vf, of_ref, sf_ref), (qb, zb, vb, ob_ref, sb_ref))
        combos = [(dr, h, slice(h * d, (h + 1) * d)) for dr in range(2) for h in range(HG_HEADS)]
        for j in range(ns):
            sub = (j, ns - 1 - j)
            rows = [pl.ds(sub[dr] * c, c) for dr in range(2)]
            prep = []
            for dr, (q_ref, z_ref, v_ref, _, _) in enumerate(dirs):
                rev = dr == 1
                qr = q_ref[rows[dr], :]
                q = qr * _sig(qr)
                _, f, cum, ref, last, _ = _gla_gates(z_ref[rows[dr], :], lb, rev)
                k = 1.0 - f
                prep.append(dict(q1=_bf(q * jnp.exp(cum - ref)), k1=_bf(k * jnp.exp(ref - cum)), q2=_bf(q * jnp.exp(cum)),
                                 k2=_bf(k * jnp.exp(last - cum)), el=jnp.exp(last), v=_bf(v_ref[rows[dr], :]),
                                 mask=_tri(c, rev)))
            a = [_bf(jnp.where(prep[dr]["mask"], _nt(prep[dr]["q1"][:, sl], prep[dr]["k1"][:, sl]), 0.0))
                 for dr, _, sl in combos]
            for (dr, h, sl), a_h in zip(combos, a):
                p = prep[dr]
                o_ref, s_ref = dirs[dr][3], dirs[dr][4]
                st = st_ref[dr, h]
                stb = _bf(st)
                s_ref[sub[dr], h] = stb
                o_ref[rows[dr], sl] = _nn(a_h, p["v"][:, sl]) + _nt(p["q2"][:, sl], stb)
                st_ref[dr, h] = st * p["el"][:, sl] + _tn(p["v"][:, sl], p["k2"][:, sl])

    def col(order, blkcol):
        return pl.BlockSpec((ns * c, hw), lambda s: (order(s, nc, nt), blkcol))

    def st_spec(order):
        return pl.BlockSpec((ns, HG_HEADS, d, d), lambda s: (order(s, nc, nt), 0, 0, 0))

    in_specs = []
    for dr, order in enumerate(orders):
        in_specs += [col(order, 0), col(order, 1 + dr), col(order, 3)]
    in_specs.append(_full(lbraw.shape))
    return _host_call(
        body, ex, lambda: pl.program_id(0) == 0, lambda: pl.program_id(0) == nt - 1,
        name=name, grid=(nt,), in_specs=in_specs,
        out_specs=[col(_fw_chunk, 0), col(_bw_chunk, 0), st_spec(_fw_chunk), st_spec(_bw_chunk)],
        out_shape=[jax.ShapeDtypeStruct((T, hw), F32), jax.ShapeDtypeStruct((T, hw), F32),
                   jax.ShapeDtypeStruct((nt * ns, HG_HEADS, d, d), ACT), jax.ShapeDtypeStruct((nt * ns, HG_HEADS, d, d), ACT)],
        scratch_shapes=[pltpu.VMEM((2, HG_HEADS, d, d), F32)], sem=("arbitrary",),
        args=(pb, pb, pb, pb, pb, pb, lbraw))


def _gla_bwd(pb, lbraw, s_fw, s_bw, do, lc, name, ex=None):
    T = pb.shape[0]
    c, hw, d, ns = HG_CHUNK, HG_HEADS * HG_D, HG_D, HG_STEP_CHUNKS
    nt, nc = T // (ns * c), lc // (ns * c)

    def rfw(s, nc_, nt_):
        return _fw_chunk(nt_ - 1 - s, nc_, nt_)

    def rbw(s, nc_, nt_):
        return _bw_chunk(nt_ - 1 - s, nc_, nt_)

    def body(qf, zf, vf, sf, dof, qb, zb, vb, sb, dob_, lb_ref,
             dqf, dzf, dvf, dqb, dzb, dvb, dlb_ref, dst_ref):
        step = pl.program_id(0)

        @pl.when(step == 0)
        def _():
            dst_ref[...] = jnp.zeros_like(dst_ref)

        lb = _lower_bound(lb_ref)
        sets = ((qf, zf, vf, sf, dof, dqf, dzf, dvf), (qb, zb, vb, sb, dob_, dqb, dzb, dvb))
        combos = [(dr, h, slice(h * d, (h + 1) * d)) for dr in range(2) for h in range(HG_HEADS)]
        dlb_tot = jnp.zeros((1, hw), F32)
        for j in range(ns):
            sub = (ns - 1 - j, j)
            rows = [pl.ds(sub[dr] * c, c) for dr in range(2)]
            prep = []
            for dr, (q_ref, z_ref, v_ref, _, do_ref, _, _, _) in enumerate(sets):
                rev = dr == 1
                qr = q_ref[rows[dr], :]
                sq = _sig(qr)
                q = qr * sq
                sg, f, cum, ref, last, last_row = _gla_gates(z_ref[rows[dr], :], lb, rev)
                k = 1.0 - f
                e_qr, e_kr, e_q, e_kl = jnp.exp(cum - ref), jnp.exp(ref - cum), jnp.exp(cum), jnp.exp(last - cum)
                q1, k1, q2, k2 = q * e_qr, k * e_kr, q * e_q, k * e_kl
                prep.append(dict(qr=qr, sq=sq, sg=sg, f=f, e_qr=e_qr, e_kr=e_kr, e_q=e_q, e_kl=e_kl, el=jnp.exp(last),
                                 q1=q1, k1=k1, q2=q2, k2=k2, q1b=_bf(q1), k1b=_bf(k1), q2b=_bf(q2), k2b=_bf(k2),
                                 vb=_bf(v_ref[rows[dr], :]), dob=_bf(do_ref[rows[dr], :]), mask=_tri(c, rev),
                                 last_row=last_row, acc_t=jnp.where(_tri(c, not rev), 1.0, 0.0).astype(BF16)))
            a = [_bf(jnp.where(prep[dr]["mask"], _nt(prep[dr]["q1b"][:, sl], prep[dr]["k1b"][:, sl]), 0.0))
                 for dr, _, sl in combos]
            da = [_bf(jnp.where(prep[dr]["mask"], _nt(prep[dr]["dob"][:, sl], prep[dr]["vb"][:, sl]), 0.0))
                  for dr, _, sl in combos]
            parts = [dict(dq1=[], dk1=[], dq2=[], dk2=[], dls=[]) for _ in range(2)]
            for (dr, h, sl), a_h, da_h in zip(combos, a, da):
                p = prep[dr]
                s_ref, dv_ref = sets[dr][3], sets[dr][7]
                stb = s_ref[sub[dr], h]
                dst = dst_ref[dr, h]
                dstb = _bf(dst)
                dob_h, vb_h = p["dob"][:, sl], p["vb"][:, sl]
                dv_ref[rows[dr], sl] = _bf(_tn(a_h, dob_h) + _nt(p["k2b"][:, sl], dstb))
                parts[dr]["dq1"].append(_nn(da_h, p["k1b"][:, sl]))
                parts[dr]["dk1"].append(_tn(da_h, p["q1b"][:, sl]))
                parts[dr]["dq2"].append(_nn(dob_h, stb))
                parts[dr]["dk2"].append(_nn(vb_h, dstb))
                el_h = p["el"][:, sl]
                dst_ref[dr, h] = _tn(dob_h, p["q2b"][:, sl]) + dst * el_h
                parts[dr]["dls"].append(jnp.sum(dst * stb.astype(F32), axis=0, keepdims=True) * el_h)
            for dr in range(2):
                p = prep[dr]
                dq_ref, dz_ref = sets[dr][5], sets[dr][6]
                dq1, dk1, dq2, dk2, dls = (jnp.concatenate(parts[dr][n], axis=1) for n in ("dq1", "dk1", "dq2", "dk2", "dls"))
                dq = dq1 * p["e_qr"] + dq2 * p["e_q"]
                dk = dk1 * p["e_kr"] + dk2 * p["e_kl"]
                dcum = dq1 * p["q1"] - dk1 * p["k1"] + dq2 * p["q2"] - dk2 * p["k2"]
                dlast = jnp.sum(dk2 * p["k2"], axis=0, keepdims=True) + dls
                rowid = lax.broadcasted_iota(jnp.int32, (c, 1), 0)
                dcum = dcum + jnp.where(rowid == p["last_row"], dlast, 0.0)
                df = _nn3(p["acc_t"], dcum) / p["f"] - dk
                sg = p["sg"]
                dz_ref[rows[dr], :] = _bf(df * (1.0 - lb) * sg * (1.0 - sg))
                dlb_tot = dlb_tot + jnp.sum(df * (1.0 - sg), axis=0, keepdims=True)
                dq_ref[rows[dr], :] = _bf(dq * (p["sq"] * (1.0 + p["qr"] * (1.0 - p["sq"]))))
        _acc_all(dlb_ref, step, dlb_tot)

    def col(order, blkcol):
        return pl.BlockSpec((ns * c, hw), lambda s: (order(s, nc, nt), blkcol))

    def st_spec(order):
        return pl.BlockSpec((ns, HG_HEADS, d, d), lambda s: (order(s, nc, nt), 0, 0, 0))

    in_specs = []
    for dr, order in enumerate((rfw, rbw)):
        in_specs += [col(order, 0), col(order, 1 + dr), col(order, 3), st_spec(order), col(order, 0)]
    in_specs.append(_full(lbraw.shape))
    out_specs = [col(rfw, 0)] * 3 + [col(rbw, 0)] * 3 + [_whole((1, hw))]
    out_shape = [jax.ShapeDtypeStruct((T, hw), ACT)] * 6 + [jax.ShapeDtypeStruct((1, hw), F32)]
    return _host_call(
        body, ex, lambda: pl.program_id(0) == 0, lambda: pl.program_id(0) == nt - 1,
        name=name, grid=(nt,), in_specs=in_specs, out_specs=out_specs, out_shape=out_shape,
        scratch_shapes=[pltpu.VMEM((2, HG_HEADS, d, d), F32)], sem=("arbitrary",),
        args=(pb, pb, pb, s_fw, do, pb, pb, pb, s_bw, do, lbraw))


def _ret_log_gamma(h, rev):
    hh = RET_HEADS - 1 - h if rev else h
    return math.log(1.0 - 2.0 ** (-5.0 - hh))


def _rope(x, cos, sin):
    half = x.shape[1] // 2
    x1, x2 = x[:, :half], x[:, half:]
    return jnp.concatenate([x1 * cos - x2 * sin, x2 * cos + x1 * sin], axis=1)


def _unrope(dy, cos, sin):
    half = dy.shape[1] // 2
    d1, d2 = dy[:, :half], dy[:, half:]
    return jnp.concatenate([d1 * cos + d2 * sin, d2 * cos - d1 * sin], axis=1)


def _ret_decays(lg, rev):
    c = RET_CHUNK
    r = lax.broadcasted_iota(jnp.int32, (c, c), 0)
    k = lax.broadcasted_iota(jnp.int32, (c, c), 1)
    rel = (k - r) if rev else (r - k)
    dm = jnp.where(rel >= 0, jnp.exp(lg * jnp.maximum(rel, 0).astype(F32)), 0.0)
    pos = lax.broadcasted_iota(jnp.int32, (c, 1), 0).astype(F32)
    if rev:
        qdec = jnp.exp(lg * (c - pos))
        kdec = jnp.exp(lg * pos)
    else:
        qdec = jnp.exp(lg * (pos + 1.0))
        kdec = jnp.exp(lg * (c - 1.0 - pos))
    return dm, qdec, kdec


def _ret_fwd(q, k, v, cos, sin, lc, name, ex=None):
    T = q.shape[0]
    c, dk, dv = RET_CHUNK, RET_DK, RET_DV
    nt, nc = T // c, lc // c
    kscale = dk ** -0.5

    def body(qf, kf, vf, cf, sf_, qb, kb, vb, cb, sb_, of_ref, ob_ref, stf_ref, stb_ref, st_ref):
        @pl.when(pl.program_id(0) == 0)
        def _():
            st_ref[...] = jnp.zeros_like(st_ref)

        sets = ((qf, kf, vf, cf, sf_, of_ref, stf_ref), (qb, kb, vb, cb, sb_, ob_ref, stb_ref))
        combos = [(dr, h) for dr in range(2) for h in range(RET_HEADS)]
        prep = {}
        for dr, (q_ref, k_ref, v_ref, c_ref, s_ref, _, _) in enumerate(sets):
            rev = dr == 1
            cos_v, sin_v = c_ref[...], s_ref[...]
            for h in range(RET_HEADS):
                lg = _ret_log_gamma(h, rev)
                dm, qdec, kdec = _ret_decays(lg, rev)
                qh = _rope(q_ref[:, h * dk:(h + 1) * dk].astype(F32), cos_v, sin_v)
                kh = _rope(k_ref[:, h * dk:(h + 1) * dk].astype(F32), cos_v, sin_v) * kscale
                prep[dr, h] = dict(qb=_bf(qh), kb=_bf(kh), qin=_bf(qh * qdec), kin=_bf(kh * kdec),
                                   v=_bf(v_ref[:, h * dv:(h + 1) * dv]), dm=dm, decay=math.exp(lg * c))
        sc = {ch: _bf(_nt(prep[ch]["qb"], prep[ch]["kb"]) * prep[ch]["dm"]) for ch in combos}
        for dr, h in combos:
            p = prep[dr, h]
            o_ref, so_ref = sets[dr][5], sets[dr][6]
            st = st_ref[dr, h]
            stb = _bf(st)
            so_ref[0, h] = stb
            o_ref[:, h * dv:(h + 1) * dv] = _bf(_nn(sc[dr, h], p["v"]) + _nt(p["qin"], stb))
            st_ref[dr, h] = st * p["decay"] + _tn(p["v"], p["kin"])

    def spec(order, width):
        return pl.BlockSpec((c, width), lambda s: (order(s, nc, nt), 0))

    def st_spec(order):
        return pl.BlockSpec((1, RET_HEADS, dv, dk), lambda s: (order(s, nc, nt), 0, 0, 0))

    in_specs = []
    for order in (_fw_chunk, _bw_chunk):
        in_specs += [spec(order, RET_HEADS * dk), spec(order, RET_HEADS * dk), spec(order, RET_HEADS * dv),
                     spec(order, dk // 2), spec(order, dk // 2)]
    return _host_call(
        body, ex, lambda: pl.program_id(0) == 0, lambda: pl.program_id(0) == nt - 1,
        name=name, grid=(nt,), in_specs=in_specs,
        out_specs=[spec(_fw_chunk, RET_HEADS * dv), spec(_bw_chunk, RET_HEADS * dv), st_spec(_fw_chunk), st_spec(_bw_chunk)],
        out_shape=[jax.ShapeDtypeStruct((T, RET_HEADS * dv), ACT), jax.ShapeDtypeStruct((T, RET_HEADS * dv), ACT),
                   jax.ShapeDtypeStruct((nt, RET_HEADS, dv, dk), ACT), jax.ShapeDtypeStruct((nt, RET_HEADS, dv, dk), ACT)],
        scratch_shapes=[pltpu.VMEM((2, RET_HEADS, dv, dk), F32)], sem=("arbitrary",),
        args=(q, k, v, cos, sin, q, k, v, cos, sin))


def _ret_bwd(q, k, v, cos, sin, s_fw, s_bw, do, lc, name, ex=None):
    T = q.shape[0]
    c, dk, dv = RET_CHUNK, RET_DK, RET_DV
    nt, nc = T // c, lc // c
    kscale = dk ** -0.5

    def rfw(s, nc_, nt_):
        return _fw_chunk(nt_ - 1 - s, nc_, nt_)

    def rbw(s, nc_, nt_):
        return _bw_chunk(nt_ - 1 - s, nc_, nt_)

    def body(qf, kf, vf, cf, sf_, stf, dof, qb, kb, vb, cb, sb_, stb_, dob_,
             dqf, dkf, dvf, dqb, dkb, dvb, dst_ref):
        @pl.when(pl.program_id(0) == 0)
        def _():
            dst_ref[...] = jnp.zeros_like(dst_ref)

        sets = ((qf, kf, vf, cf, sf_, stf, dof, dqf, dkf, dvf), (qb, kb, vb, cb, sb_, stb_, dob_, dqb, dkb, dvb))
        combos = [(dr, h) for dr in range(2) for h in range(RET_HEADS)]
        prep = {}
        for dr, (q_ref, k_ref, v_ref, c_ref, s_ref, _, do_ref, _, _, _) in enumerate(sets):
            rev = dr == 1
            cos_v, sin_v = c_ref[...], s_ref[...]
            for h in range(RET_HEADS):
                lg = _ret_log_gamma(h, rev)
                dm, qdec, kdec = _ret_decays(lg, rev)
                qh = _rope(q_ref[:, h * dk:(h + 1) * dk].astype(F32), cos_v, sin_v)
                kh = _rope(k_ref[:, h * dk:(h + 1) * dk].astype(F32), cos_v, sin_v) * kscale
                prep[dr, h] = dict(qb=_bf(qh), kb=_bf(kh), qin=_bf(qh * qdec), kin=_bf(kh * kdec),
                                   v=_bf(v_ref[:, h * dv:(h + 1) * dv]), dob=_bf(do_ref[:, h * dv:(h + 1) * dv]),
                                   dm=dm, qdec=qdec, kdec=kdec, decay=math.exp(lg * c), cos=cos_v, sin=sin_v)
        sc = {ch: _bf(_nt(prep[ch]["qb"], prep[ch]["kb"]) * prep[ch]["dm"]) for ch in combos}
        dsc = {ch: _bf(_nt(prep[ch]["dob"], prep[ch]["v"]) * prep[ch]["dm"]) for ch in combos}
        carried = {}
        for dr, h in combos:
            p = prep[dr, h]
            dv_ref = sets[dr][9]
            dst = dst_ref[dr, h]
            dstb = _bf(dst)
            carried[dr, h] = dstb
            dv_ref[:, h * dv:(h + 1) * dv] = _bf(_tn(sc[dr, h], p["dob"]) + _nt(p["kin"], dstb))
            dst_ref[dr, h] = _tn(p["dob"], p["qin"]) + dst * p["decay"]
        for dr, h in combos:
            p = prep[dr, h]
            st_in, dq_ref, dk_ref = sets[dr][5], sets[dr][7], sets[dr][8]
            dq_r = _nn(dsc[dr, h], p["kb"]) + _nn(p["dob"], st_in[0, h]) * p["qdec"]
            dk_r = _tn(dsc[dr, h], p["qb"]) + _nn(p["v"], carried[dr, h]) * p["kdec"]
            dq_ref[:, h * dk:(h + 1) * dk] = _bf(_unrope(dq_r, p["cos"], p["sin"]))
            dk_ref[:, h * dk:(h + 1) * dk] = _bf(_unrope(dk_r * kscale, p["cos"], p["sin"]))

    def spec(order, width):
        return pl.BlockSpec((c, width), lambda s: (order(s, nc, nt), 0))

    def st_spec(order):
        return pl.BlockSpec((1, RET_HEADS, dv, dk), lambda s: (order(s, nc, nt), 0, 0, 0))

    in_specs = []
    for order in (rfw, rbw):
        in_specs += [spec(order, RET_HEADS * dk), spec(order, RET_HEADS * dk), spec(order, RET_HEADS * dv),
                     spec(order, dk // 2), spec(order, dk // 2), st_spec(order), spec(order, RET_HEADS * dv)]
    out_specs, out_shape = [], []
    for order in (rfw, rbw):
        out_specs += [spec(order, RET_HEADS * dk), spec(order, RET_HEADS * dk), spec(order, RET_HEADS * dv)]
        out_shape += [jax.ShapeDtypeStruct((T, RET_HEADS * dk), ACT), jax.ShapeDtypeStruct((T, RET_HEADS * dk), ACT),
                      jax.ShapeDtypeStruct((T, RET_HEADS * dv), ACT)]
    return _host_call(
        body, ex, lambda: pl.program_id(0) == 0, lambda: pl.program_id(0) == nt - 1,
        name=name, grid=(nt,), in_specs=in_specs, out_specs=out_specs, out_shape=out_shape,
        scratch_shapes=[pltpu.VMEM((2, RET_HEADS, dv, dk), F32)], sem=("arbitrary",),
        args=(q, k, v, cos, sin, s_fw, do, q, k, v, cos, sin, s_bw, do))


def _trig_rows(lc, ang):
    ang = ang.astype(np.float64)
    half = ang.shape[1]
    cos = np.concatenate([np.ones((lc, half)), np.cos(ang)], axis=0).astype(np.float32)
    sin = np.concatenate([np.zeros((lc, half)), np.sin(ang)], axis=0).astype(np.float32)
    return cos, sin


def _attn_rope_tables(lc, l):
    t = np.arange(l)
    row = (t // GRID_W).astype(np.float32)
    colp = (t % GRID_W).astype(np.float32)
    n_freq = HEAD_DIM // 4
    inv = np.float32(10000.0) ** (-np.arange(n_freq, dtype=np.float32) / np.float32(n_freq))
    ang = np.concatenate([row[:, None] * inv, colp[:, None] * inv], axis=-1)
    cos, sin = _trig_rows(lc, ang)
    return jnp.asarray(np.concatenate([cos, cos], axis=1)), jnp.asarray(np.concatenate([-sin, sin], axis=1))


def _ret_rope_tables(lc, l):
    theta = np.float32(1.0) / (np.float32(10000.0) ** np.linspace(0.0, 1.0, RET_DK // 2, dtype=np.float32))
    ang = np.arange(l, dtype=np.float32)[:, None] * theta
    cos, sin = _trig_rows(lc, ang)
    return jnp.asarray(cos), jnp.asarray(sin)


COL_SHARDED =("ffn_in0", "ffn_in1", "even_in", "even_in_a", "even_in_b", "odd_in")


def _full_weight(name, g):
    if name in COL_SHARDED:
        return g.transpose(1, 0, 2).reshape(g.shape[1], -1)
    return g.reshape(-1, g.shape[2])


def _shard_slots(name, g):
    if name in COL_SHARDED:
        return g.reshape(g.shape[0], N_DEV, -1).transpose(1, 0, 2)
    return g.reshape(N_DEV, -1, g.shape[1])


def _local_step(xs, target, mv, norm_g, w, qk_g, sink, hg_out_g, lbraw, lc, shards=None):
    _, T, dm = _stream(xs)
    l = T - lc
    tm = lc
    blk = ATTN_BLOCK
    d2, d3 = 2 * dm, 3 * dm
    w = dict(w)
    gw, recv = {}, {}

    def ms(layer, a, b):
        return mv[layer, :, :, a:b]

    def gather(names):
        return None if shards is None or not names else _Exchange(GATHER2, [shards[n] for n in names])

    def arrived(names, got):
        for n, g in zip(names, got):
            w[n] = _full_weight(n, g)

    def scatter(names):
        return None if shards is None else _Exchange(SCATTER, [_shard_slots(n, gw[n]) for n in names])

    def scattered(names, got):
        for n, g in zip(names, got):
            recv[n] = g

    g00, g01, g10, g11 = (norm_g[i, j][None, :] for i in (0, 1) for j in (0, 1))

    cos2, sin2 = _attn_rope_tables(lc, l)
    cosp, sinp = jnp.concatenate([cos2, cos2], axis=1), jnp.concatenate([sin2, sin2], axis=1)
    gains5 = jnp.concatenate([jnp.broadcast_to(jnp.tile(qk_g[0], 2), (N_PAIRS - 1, PAIR)), jnp.tile(qk_g[1], 2)[None]])[:, None, :]
    riding = []
    (pa, pb, qt, ks, vs), got = _pre_fwd(xs, g00, ms(0, 0, d2), w["even_in"], ((0, 768), (768, 3328)), tm, "pre0_fwd",
                                         gather(riding), qk=(gains5, cosp, sinp))
    arrived(riding, got)
    sinkb = jnp.broadcast_to(sink.reshape(ATTN_KV, 4, 1, 1), (ATTN_KV, 4, blk, 1)).reshape(ATTN_KV, 4 * blk, 1)
    riding = ["ffn_in0", "even_out"]
    (a_slab, lse), got = _attn_slab_fwd(qt, ks, vs, sinkb, lc, "attn_fwd", gather(riding))
    arrived(riding, got)
    riding = ["ffn_out0", "odd_out"]
    (hg_of, hg_ob, hg_sf, hg_sb), got = _gla_fwd(pb, lbraw, lc, "hgrn_fwd", gather(riding))
    arrived(riding, got)
    x01, z0, yp0 = _post_fwd(xs, hg_of, hg_ob, pb, 4, hg_out_g, a_slab, w["even_out"], ms(0, d2, d3), HG_D, tm, "post0_fwd")
    riding = ["odd_in", "ffn_out1"]
    (x02, u0, f0), got = _ffn_fwd(x01, g01, ms(0, d3, 6 * dm), w["ffn_in0"], w["ffn_out0"], tm, "ffn0_fwd", ex=gather(riding))
    arrived(riding, got)

    riding = []
    (rq, rk, rv, rg), got = _pre_fwd(x02, g10, ms(1, 0, d2), w["odd_in"],
                                     ((0, 1024), (1024, 2048), (2048, 4096), (4096, 6144)), tm, "pre1_fwd", gather(riding),
                                     out_dtype=ACT)
    arrived(riding, got)
    rcos, rsin = _ret_rope_tables(lc, l)
    riding = ["ffn_in1"]
    (rt_of, rt_ob, rt_sf, rt_sb), got = _ret_fwd(rq, rk, rv, rcos, rsin, lc, "ret_fwd", gather(riding))
    arrived(riding, got)
    x11, z1, yp1 = _post_fwd(x02, rt_of, rt_ob, rg, 0, None, None, w["odd_out"], ms(1, d2, d3), RET_DV, tm, "post1_fwd")
    (dx, u1, f1, loss), _ = _ffn_fwd(x11, g11, ms(1, d3, 6 * dm), w["ffn_in1"], w["ffn_out1"], tm, "ffn1_fwd", target)

    (dx, h, du, act, df, dms_f1, dg11), _ = _ffn_bwd(x11, dx, u1, f1, g11, ms(1, d3, 6 * dm), w["ffn_in1"], w["ffn_out1"], tm,
                                                     "ffn1_bwd")
    gw["ffn_in1"] = _wgrad(h, du, "wg_ffn_in1")
    gw["ffn_out1"] = _wgrad(act, df, "wg_ffn_out1")
    do1, dgr1, dy1, z1_t, dgate_p1, _ = _post_bwd(dx, z1, yp1, rt_of, rt_ob, rg, 0, None, w["odd_out"], ms(1, d2, d3), 0, RET_DV, tm,
                                                  "post1_bwd")
    gw["odd_out"] = _wgrad(z1_t, dy1, "wg_odd_out")
    riding = ["ffn_in1"]
    (dqf, dkf, dvf, dqb, dkb, dvb), got = _ret_bwd(rq, rk, rv, rcos, rsin, rt_sf, rt_sb, do1, lc, "ret_bwd", scatter(riding))
    scattered(riding, got)
    riding = ["odd_out", "ffn_out1"]
    (dx, h, dp, dms_p1, dg10), got = _pre_bwd(x02, dx, g10, ms(1, 0, d2), w["odd_in"],
                                              [(0, [dqf, dqb]), (1024, [dkf, dkb]), (2048, [dvf, dvb]), (4096, [dgr1])], tm,
                                              "pre1_bwd", ex=scatter(riding))
    scattered(riding, got)
    gw["odd_in"] = _wgrad(h, dp, "wg_odd_in")

    riding = ["odd_in"]
    (dx, h, du, act, df, dms_f0, dg01), got = _ffn_bwd(x01, dx, u0, f0, g01, ms(0, d3, 6 * dm), w["ffn_in0"], w["ffn_out0"], tm,
                                                       "ffn0_bwd", scatter(riding))
    scattered(riding, got)
    gw["ffn_in0"] = _wgrad(h, du, "wg_ffn_in0")
    gw["ffn_out0"] = _wgrad(act, df, "wg_ffn_out0")
    do0, dgr0, da0, dy0, z0_t, dgate_p0, d_hg_gain = _post_bwd(dx, z0, yp0, hg_of, hg_ob, pb, 4, hg_out_g, w["even_out"],
                                                              ms(0, d2, d3), 512, HG_D, tm, "post0_bwd")
    gw["even_out"] = _wgrad(z0_t, dy0, "wg_even_out")
    riding = ["ffn_in0"]
    (hq_f, hz_f, hv_f, hq_b, hz_b, hv_b, dlb), got = _gla_bwd(pb, lbraw, hg_sf, hg_sb, do0, lc, "hgrn_bwd", scatter(riding))
    scattered(riding, got)
    riding = ["even_out", "ffn_out0"]
    (dq_att, dk_att, dv_att, dsink), got = _attn_slab_bwd(qt, ks, vs, sinkb, a_slab, lse, da0, lc, "attn_bwd", scatter(riding))
    scattered(riding, got)
    pieces0 = [(640, [dv_att]), (768, [hq_f, hq_b]), (1280, [hz_f]), (1792, [hz_b]), (2304, [hv_f, hv_b]), (2816, [dgr0])]
    (dx, h, dp, dms_p0, dg00, dgain5), _ = _pre_bwd(xs, dx, g00, ms(0, 0, d2), w["even_in"], pieces0, tm, "pre0_bwd",
                                                    latent_dx=shards is not None,
                                                    qk=(dq_att, dk_att, pa, gains5, cosp, sinp))
    if shards is None:
        gw["even_in"] = _wgrad(h, dp, "wg_even_in")
    else:
        half = dm // 2
        gw["even_in_a"] = _wgrad(h, dp, "wg_even_in_a", rows=(0, half))
        gw["even_in_b"], got = _wgrad(h, dp, "wg_even_in_b", rows=(half, half), ex=scatter(["even_in_a"]))
        scattered(["even_in_a"], got)

    dmv = jnp.stack([jnp.concatenate([dms_p0, dgate_p0, dms_f0], axis=2), jnp.concatenate([dms_p1, dgate_p1, dms_f1], axis=2)])
    small = {
        "dmv": dmv,
        "norm_g": jnp.stack([jnp.stack([dg00[0], dg01[0]]), jnp.stack([dg10[0], dg11[0]])]),
        "qk_g": jnp.stack([jnp.sum(dgain5[:N_PAIRS - 1, 0].reshape(-1, HEAD_DIM), axis=0),
                           jnp.sum(dgain5[N_PAIRS - 1, 0].reshape(-1, HEAD_DIM), axis=0)]),
        "sink": dsink.reshape(ATTN_HEADS),
        "hg_out_g": d_hg_gain[0],
        "lb": dlb[0],
        "loss": loss[0, 0],
    }
    if shards is not None:
        gw = {n: recv.get(n, g) for n, g in gw.items()}
    return loss, dx, gw, small


HBM_SPEC = pl.BlockSpec(memory_space=pltpu.HBM)


def _my_index():
    return 4 * lax.axis_index("x") + 2 * lax.axis_index("y") + lax.axis_index("c")


def _peer(k):
    pos = []
    for axis, bit in (("x", 4), ("y", 2), ("c", 1)):
        a = lax.axis_index(axis)
        pos.append(1 - a if k & bit else a)
    return tuple(pos)


def _peer_index(k):
    px, py, pc = _peer(k)
    return 4 * px + 2 * py + pc


GATHER, SCATTER = "gather", "scatter"
GATHER2 = "gather over ICI once per chip"
SIBLING = 1
OTHER_CHIPS = (2, 4, 6)


class _Exchange:
    def __init__(self, mode, arrays):
        self.mode, self.arrays, self.n = mode, list(arrays), len(arrays)

    def out_shape(self):
        if self.mode in (GATHER, GATHER2):
            return [jax.ShapeDtypeStruct((N_DEV,) + a.shape, a.dtype) for a in self.arrays]
        return [jax.ShapeDtypeStruct(a.shape, a.dtype) for a in self.arrays]

    def specs(self):
        return [HBM_SPEC] * self.n

    def scratch(self):
        return [pltpu.SemaphoreType.DMA((self.n, N_DEV - 1)), pltpu.SemaphoreType.DMA((self.n, N_DEV - 1)),
                pltpu.SemaphoreType.DMA((self.n,))]

    def _copies(self, in_refs, out_refs, send_sems, recv_sems, local_sems, landing):
        me = _my_index()
        local, remote = [], []
        for a, (src, dst) in enumerate(zip(in_refs, out_refs)):
            part = (lambda j, s=src: s) if self.mode == GATHER else (lambda j, s=src: s.at[j])
            local.append(pltpu.make_async_copy(part(me), dst.at[me], local_sems.at[a]))
            for k in range(1, N_DEV):
                pj = _peer_index(k)
                remote.append(pltpu.make_async_remote_copy(
                    src_ref=part(pj), dst_ref=dst.at[pj if landing else me], send_sem=send_sems.at[a, k - 1],
                    recv_sem=recv_sems.at[a, k - 1], device_id=_peer(k), device_id_type=MESH))
        return local, remote

    def _copy2(self, a, src, dst, sems, slot, relation, to):
        send_sems, recv_sems, _ = sems
        return pltpu.make_async_remote_copy(src_ref=src, dst_ref=dst.at[slot], send_sem=send_sems.at[a, relation - 1],
                                            recv_sem=recv_sems.at[a, relation - 1], device_id=_peer(to), device_id_type=MESH)

    def start(self, in_refs, out_refs, sems):
        if self.mode == GATHER2:
            me = _my_index()
            for a, (src, dst) in enumerate(zip(in_refs, out_refs)):
                pltpu.make_async_copy(src, dst.at[me], sems[2].at[a]).start()
                for k in (SIBLING,) + OTHER_CHIPS:
                    self._copy2(a, src, dst, sems, me, k, k).start()
            return
        local, remote = self._copies(in_refs, out_refs, *sems, landing=False)
        for cp in local + remote:
            cp.start()

    def forward(self, in_refs, out_refs, sems):
        for a, (src, dst) in enumerate(zip(in_refs, out_refs)):
            for r in OTHER_CHIPS:
                pj = _peer_index(r)
                self._copy2(a, src, dst, sems, pj, r, r).wait_recv()
                self._copy2(a, dst.at[pj], dst, sems, pj, r ^ SIBLING, SIBLING).start()

    def wait(self, in_refs, out_refs, sems):
        if self.mode == GATHER2:
            me = _my_index()
            for a, (src, dst) in enumerate(zip(in_refs, out_refs)):
                for k in (SIBLING,) + OTHER_CHIPS:
                    self._copy2(a, src, dst, sems, me, k, k).wait_send()
                self._copy2(a, src, dst, sems, _peer_index(SIBLING), SIBLING, SIBLING).wait_recv()
                for r in OTHER_CHIPS:
                    passed = self._copy2(a, src, dst, sems, _peer_index(r ^ SIBLING), r ^ SIBLING, SIBLING)
                    passed.wait_send()
                    passed.wait_recv()
                pltpu.make_async_copy(src, dst.at[me], sems[2].at[a]).wait()
            return
        local, remote = self._copies(in_refs, out_refs, *sems, landing=True)
        for cp in remote:
            cp.wait_send()
            cp.wait_recv()
        for cp in local:
            cp.wait()

    def ride(self, refs, n_in, n_out, first, mid, last):
        refs = list(refs)
        n = self.n
        x_in = refs[n_in:n_in + n]
        x_out = refs[n_in + n + n_out:n_in + 2 * n + n_out]
        sems = refs[n_in + 2 * n + n_out:n_in + 2 * n + n_out + 3]

        @pl.when(first)
        def _():
            self.start(x_in, x_out, sems)

        if self.mode == GATHER2:
            @pl.when(mid)
            def _():
                self.forward(x_in, x_out, sems)

        @pl.when(last)
        def _():
            self.wait(x_in, x_out, sems)

        return refs[:n_in] + refs[n_in + n:n_in + n + n_out] + refs[n_in + 2 * n + n_out + 3:]

    def call(self, name):
        n = self.n

        def body(*refs):
            ins, outs, sems = refs[:n], refs[n:2 * n], refs[2 * n:]
            self.start(ins, outs, sems)
            if self.mode == GATHER2:
                self.forward(ins, outs, sems)
            self.wait(ins, outs, sems)

        return pl.pallas_call(body, name=name, in_specs=self.specs(), out_specs=self.specs(), out_shape=self.out_shape(),
                              scratch_shapes=self.scratch())(*self.arrays)


def _all_gather(v, name):
    return _Exchange(GATHER, [v]).call(name)[0]


def _hosted(kernel_body, ex, n_in, n_out, first, last, grid):
    if ex is None:
        return kernel_body

    def body(*refs):
        mid = pl.program_id(0) == (2 * grid[0]) // 3 if len(grid) == 1 else None
        kernel_body(*ex.ride(refs, n_in, n_out, first(), mid, last()))

    return body


def _host_call(kernel_body, ex, first, last, name, grid, in_specs, out_specs, out_shape, scratch_shapes, sem, args):
    n_in, n_out = len(in_specs), len(out_specs)
    if ex is None:
        outs = pl.pallas_call(kernel_body, name=name, grid=grid, in_specs=in_specs, out_specs=out_specs, out_shape=out_shape,
                              scratch_shapes=scratch_shapes, compiler_params=_cp(*sem))(*args)
        return list(outs), []
    outs = pl.pallas_call(
        _hosted(kernel_body, ex, n_in, n_out, first, last, grid), name=name, grid=grid,
        in_specs=list(in_specs) + ex.specs(), out_specs=list(out_specs) + ex.specs(),
        out_shape=list(out_shape) + ex.out_shape(), scratch_shapes=ex.scratch() + list(scratch_shapes),
        compiler_params=_cp(*sem))(*args, *ex.arrays)
    return list(outs[:n_out]), list(outs[n_out:])


def _mod_fwd(call, mod_w, bias, name):
    nl, dm, n = mod_w.shape

    def body(c_ref, w_ref, b_ref, o_ref):
        cv = c_ref[...]
        cond = _bf(cv * _sig(cv))
        for layer in range(nl):
            o_ref[layer] = _nn(cond, _bf(w_ref[layer])) + b_ref[layer]

    return pl.pallas_call(
        body, name=name, out_shape=jax.ShapeDtypeStruct((nl, call.shape[0], n), F32),
        compiler_params=pltpu.CompilerParams(vmem_limit_bytes=VMEM_LIMIT),
    )(call, mod_w, bias)


def _mod_bwd(call, dm_all, mod_w, name):
    nl, dm, n = mod_w.shape

    def body(c_ref, d_ref, w_ref, gw_ref, dc_ref):
        cv = c_ref[...]
        cond = _bf(cv * _sig(cv))
        dc = jnp.zeros(cv.shape, F32)
        for layer in range(nl):
            db = _bf(d_ref[layer])
            gw_ref[layer] = _tn(cond, db)
            dc = dc + _nt(db, _bf(w_ref[layer]))
        dc_ref[...] = dc

    return pl.pallas_call(
        body, name=name,
        out_shape=[jax.ShapeDtypeStruct(mod_w.shape, F32), jax.ShapeDtypeStruct(call.shape, F32)],
        compiler_params=pltpu.CompilerParams(vmem_limit_bytes=VMEM_LIMIT),
    )(call, dm_all, mod_w)


def _sum_parts(g, name):
    def body(g_ref, o_ref):
        acc = g_ref[0]
        for j in range(1, g.shape[0]):
            acc = acc + g_ref[j]
        o_ref[...] = acc

    return pl.pallas_call(body, name=name, out_shape=jax.ShapeDtypeStruct(g.shape[1:], g.dtype))(g)


def _small_finish(dcond_g, c_ctx, dlb, lbraw, dm_ctx, dm_lat, name):
    def body(dc_ref, c_ref, dlb_ref, lb_ref, mc_ref, ml_ref, gc_ref, glb_ref, gb_ref):
        acc = dc_ref[0, 0:1, :]
        for j in range(1, N_DEV):
            acc = acc + dc_ref[j, 0:1, :]
        cv = c_ref[...]
        s = _sig(cv)
        gc_ref[...] = acc * (s * (1.0 + cv * (1.0 - s)))
        lb = _lower_bound(lb_ref)
        d0 = dlb_ref[...] * lb * (1.0 - lb)
        glb_ref[0:1, :] = d0
        glb_ref[1:2, :] = -d0
        gb_ref[...] = mc_ref[...] + ml_ref[...]

    return pl.pallas_call(
        body, name=name,
        out_shape=[jax.ShapeDtypeStruct(c_ctx.shape, F32), jax.ShapeDtypeStruct(lbraw.shape, F32),
                   jax.ShapeDtypeStruct(dm_ctx.shape, F32)],
    )(dcond_g, c_ctx, dlb, lbraw, dm_ctx, dm_lat)


def _row_tile(r, cap, mult):
    best = r
    for t in range(mult, min(r, cap) + 1, mult):
        if r % t == 0:
            best = t
    return best


def _adam(g_list, w, m, v, name, ex=None):
    nl, r, cdim = w.shape
    p = g_list[0].shape[0]
    tr = _row_tile(r, 128, 16)
    ni = r // tr

    def body(*refs):
        g_refs = refs[:nl]
        w_ref, m_ref, v_ref, go_ref, d_ref, mo_ref, vo_ref = refs[nl:]
        layer = pl.program_id(0)

        def total(g_ref):
            acc = g_ref[0].astype(F32)
            for j in range(1, p):
                acc = acc + g_ref[j].astype(F32)
            return acc

        g = total(g_refs[0])
        for k in range(1, nl):
            g = jnp.where(layer == k, total(g_refs[k]), g)
        m2 = ADAM_B1 * m_ref[0] + (1.0 - ADAM_B1) * g
        v2 = ADAM_B2 * v_ref[0] + (1.0 - ADAM_B2) * (g * g)
        m_hat = m2 / (1.0 - ADAM_B1 ** ADAM_STEP)
        v_hat = v2 / (1.0 - ADAM_B2 ** ADAM_STEP)
        go_ref[0] = g
        d_ref[0] = -ADAM_LR * (m_hat / (jnp.sqrt(v_hat) + ADAM_EPS) + ADAM_WD * w_ref[0])
        mo_ref[0] = m2
        vo_ref[0] = v2

    def g_spec(k):
        return pl.BlockSpec((p, tr, cdim), lambda la, i: (0, jnp.where(la == k, i, jnp.where(la < k, 0, ni - 1)), 0))

    spec = pl.BlockSpec((1, tr, cdim), lambda la, i: (la, i, 0))
    return _host_call(
        body, ex, lambda: (pl.program_id(0) == 0) & (pl.program_id(1) == 0),
        lambda: (pl.program_id(0) == nl - 1) & (pl.program_id(1) == ni - 1),
        name=name, grid=(nl, ni),
        in_specs=[g_spec(k) for k in range(nl)] + [spec, spec, spec],
        out_specs=[spec] * 4, out_shape=[jax.ShapeDtypeStruct((nl, r, cdim), F32)] * 4,
        scratch_shapes=[], sem=("arbitrary", "arbitrary"), args=(*g_list, w, m, v))


def _f32_as_rows(a, width):
    return lax.bitcast_convert_type(a.reshape(-1), BF16).reshape(-1, width)


def _rows_as_f32(rows):
    return lax.bitcast_convert_type(rows.reshape(rows.shape[:-2] + (-1, 2)), F32)


def _pad_rows(a, mult):
    r = (-a.shape[-2]) % mult
    if r == 0:
        return a
    widths = [(0, 0)] * (a.ndim - 2) + [(0, r), (0, 0)]
    return jnp.pad(a, widths)


def _pack_flat(parts, lane):
    flat = jnp.concatenate([p.reshape(-1).astype(F32) for p in parts])
    n = flat.shape[0]
    rows = -(-n // lane)
    rows += (-rows) % 8
    return jnp.pad(flat, (0, rows * lane - n)).reshape(rows, lane)


def _unpack_flat(packed, shapes):
    flat = packed.reshape(-1)
    out, off = [], 0
    for s in shapes:
        n = math.prod(s)
        out.append(flat[off:off + n].reshape(s))
        off += n
    return out


def kernel(x, c, ctx, c_ctx, mod_w, mod_b, norm_g, ffn_w_in, ffn_w_out, even_w_in, even_w_out, attn_qk_norm_g, attn_sink, hgrn_out_norm_g, hgrn_lb, odd_w_in, odd_w_out, loss_target, m_c_ctx, m_mod_w, m_mod_b, m_norm_g, m_ffn_w_in, m_ffn_w_out, m_even_w_in, m_even_w_out, m_attn_qk_norm_g, m_attn_sink, m_hgrn_out_norm_g, m_hgrn_lb, m_odd_w_in, m_odd_w_out, v_c_ctx, v_mod_w, v_mod_b, v_norm_g, v_ffn_w_in, v_ffn_w_out, v_even_w_in, v_even_w_out, v_attn_qk_norm_g, v_attn_sink, v_hgrn_out_norm_g, v_hgrn_lb, v_odd_w_in, v_odd_w_out):
    me = _my_index()
    lc, dm = ctx.shape[1], x.shape[2]
    nmod = mod_w.shape[2]

    extra = _pad_rows(jnp.concatenate([_f32_as_rows(c, dm), _f32_as_rows(norm_g, dm)], axis=0), 16)
    shards = {"ffn_in0": ffn_w_in[0], "ffn_in1": ffn_w_in[1], "ffn_out0": ffn_w_out[0], "ffn_out1": ffn_w_out[1],
              "even_in": even_w_in[0], "even_out": even_w_out[0], "odd_in": odd_w_in[0], "odd_out": odd_w_out[0]}
    shards = {n: a.astype(BF16) for n, a in shards.items()}
    first = _Exchange(GATHER2, [shards["even_in"], extra]).call("gather_first")
    w = {"even_in": _full_weight("even_in", first[0])}
    c_all = _rows_as_f32(first[1][:, 0:2])
    norm_g_all = _rows_as_f32(first[1][:, 2:3]).reshape(N_DEV, 2, 2, -1)
    norm_g_full = norm_g_all.transpose(1, 2, 0, 3).reshape(2, 2, dm)

    call = jnp.concatenate([c_all, c_ctx[None, :], jnp.zeros((16 - N_DEV - 1, dm), F32)], axis=0)
    bias = lax.dynamic_slice_in_dim(mod_b, me * nmod, nmod, axis=1)[:, None, :]
    m_sh = _mod_fwd(call, mod_w, bias, "mod_fwd")
    m_g = _all_gather(m_sh.reshape(-1, nmod), "gather_mod").reshape(N_DEV, 2, 16, nmod)
    m_all = m_g.transpose(1, 2, 0, 3).reshape(2, 16, -1)
    m_lat = lax.dynamic_index_in_dim(m_all, me, axis=1, keepdims=False)
    mv = jnp.stack([m_all[:, N_DEV], m_lat], axis=1)[:, :, None, :]

    _, dxs, gw, small = _local_step((ctx[0], x[0]), loss_target[0], mv, norm_g_full, w, attn_qk_norm_g[0], attn_sink[0],
                                    hgrn_out_norm_g, hgrn_lb, lc, shards)
    grad_x = dxs[None]

    last = _Exchange(SCATTER, [_shard_slots("even_in_b", gw["even_in_b"])])
    big_g = [[gw["ffn_in0"], gw["ffn_in1"]], [gw["ffn_out0"], gw["ffn_out1"]], None, [gw["even_out"]],
             [gw["odd_in"]], [gw["odd_out"]]]
    halves = (2, even_w_in.shape[1] // 2, even_w_in.shape[2])
    big_w = (ffn_w_in, ffn_w_out, even_w_in.reshape(halves), even_w_out, odd_w_in, odd_w_out)
    big_m = (m_ffn_w_in, m_ffn_w_out, m_even_w_in.reshape(halves), m_even_w_out, m_odd_w_in, m_odd_w_out)
    big_v = (v_ffn_w_in, v_ffn_w_out, v_even_w_in.reshape(halves), v_even_w_out, v_odd_w_in, v_odd_w_out)
    big_names = ("ffn_w_in", "ffn_w_out", "even_w_in", "even_w_out", "odd_w_in", "odd_w_out")
    big_out = [None] * 6

    def adam_big(i, ex=None):
        big_out[i], got = _adam(big_g[i], big_w[i], big_m[i], big_v[i], "adam_" + big_names[i], ex)
        return got

    dmv = small["dmv"]
    small_shapes = [(2, 6 * dm), (2, 6 * dm), (2, 2, dm), (2, HEAD_DIM), (ATTN_HEADS,), (HG_D,), (HG_HEADS * HG_D,), (1,)]
    vec = _pack_flat([dmv[:, 0, 0], dmv[:, 1, 0], small["norm_g"], small["qk_g"], small["sink"], small["hg_out_g"],
                      small["lb"], small["loss"]], 128)
    big_g[2] = [gw["even_in_a"], adam_big(0, last)[0]]
    vec_g = adam_big(1, _Exchange(GATHER, [vec]))[0]
    tot = _unpack_flat(_sum_parts(vec_g, "sum_small"), small_shapes)
    dm_ctx_tot, dm_lat_tot, g_norm_full, g_qk, g_sink, g_hg, dlb_tot, loss_tot = tot
    dm_lat_each = vec_g.reshape(N_DEV, -1)[:, 12 * dm:24 * dm].reshape(N_DEV, 2, 6 * dm)
    dm_lat_mine = lax.dynamic_slice_in_dim(dm_lat_each, me * nmod, nmod, axis=2).transpose(1, 0, 2)
    dm_ctx_mine = lax.dynamic_slice_in_dim(dm_ctx_tot, me * nmod, nmod, axis=1)[:, None, :]
    dm_all = jnp.concatenate([dm_lat_mine, dm_ctx_mine, jnp.zeros((2, 16 - N_DEV - 1, nmod), F32)], axis=1)
    g_mod_w, dcond = _mod_bwd(call, dm_all, mod_w, "mod_bwd")
    dcond_g = adam_big(4, _Exchange(GATHER, [dcond[N_DEV:]]))[0]
    g_c_ctx, g_lb, g_mod_b = _small_finish(dcond_g, c_ctx[None, :], dlb_tot[None, :], hgrn_lb, dm_ctx_tot, dm_lat_tot,
                                           "small_finish")
    g_norm = lax.dynamic_slice_in_dim(g_norm_full, me * norm_g.shape[2], norm_g.shape[2], axis=2)
    for i in (3, 5, 2):
        adam_big(i)
    big_out[2] = [o.reshape(even_w_in.shape) for o in big_out[2]]
    big_res = [[big_out[i][k] for i in range(6)] for k in range(4)]

    mod_res, _ = _adam([g_mod_w[0][None], g_mod_w[1][None]], mod_w, m_mod_w, v_mod_w, "adam_mod_w")

    sm_w = (c_ctx, mod_b, norm_g, attn_qk_norm_g, attn_sink, hgrn_out_norm_g, hgrn_lb)
    sm_m = (m_c_ctx, m_mod_b, m_norm_g, m_attn_qk_norm_g, m_attn_sink, m_hgrn_out_norm_g, m_hgrn_lb)
    sm_v = (v_c_ctx, v_mod_b, v_norm_g, v_attn_qk_norm_g, v_attn_sink, v_hgrn_out_norm_g, v_hgrn_lb)
    sm_g = (g_c_ctx, g_mod_b, g_norm, g_qk, g_sink, g_hg, g_lb)
    sm_shapes = [a.shape for a in sm_w]
    sm_out, _ = _adam([_pack_flat(sm_g, 128)[None]], _pack_flat(sm_w, 128)[None], _pack_flat(sm_m, 128)[None],
                      _pack_flat(sm_v, 128)[None], "adam_small")
    sm_res = [_unpack_flat(o, sm_shapes) for o in sm_out]

    def ordered(k):
        s, b = sm_res[k], big_res[k]
        return [s[0], mod_res[k], s[1], s[2], b[0], b[1], b[2], b[3], s[3], s[4], s[5], s[6], b[4], b[5]]

    return (loss_tot[0], grad_x, *ordered(0), *ordered(1), *ordered(2), *ordered(3))
```
